```python
import math
import jax, jax.numpy as jnp
from jax import lax
import numpy as np

D_MODEL = 1024
BATCH = 8
SEQ = 4096
DEPTH = 4

CHUNK = 64
N_MEM = 256
EPS = 1e-6
NEG_INF = -1e30
N_NORMS = 6

BRANCH_DIM = D_MODEL // 2
N_BRANCH = 3

HEAD_DIM = 64
A_Q_HEADS = BRANCH_DIM // HEAD_DIM
A_KV_HEADS = 2
A_GROUP = A_Q_HEADS // A_KV_HEADS
A_Q_DIM = A_Q_HEADS * HEAD_DIM
A_KV_DIM = A_KV_HEADS * HEAD_DIM
WINDOW = 128
WINDOW_CHUNKS = WINDOW // CHUNK
ATT_BLOCK = 128

SGU_CHUNK = 128
SGU_GROUPS = 4
SGU_DIM = BRANCH_DIM
SGU_GROUP_DIM = SGU_DIM // SGU_GROUPS

POOL_WINDOWS = (2, 4, 8, 16)
POOL_GROUPS = 4
POOL_DIM = BRANCH_DIM
POOL_GROUP_DIM = POOL_DIM // POOL_GROUPS

IN_SIZES = (A_Q_DIM, A_KV_DIM, A_KV_DIM, SGU_DIM, SGU_DIM, POOL_DIM, N_BRANCH * D_MODEL)
IN_DIM = A_Q_DIM + 2 * A_KV_DIM + 2 * SGU_DIM + POOL_DIM + N_BRANCH * D_MODEL

MEM_HEADS = 4
MEM_HEAD_DIM = 128
MEM_DIM = MEM_HEADS * MEM_HEAD_DIM

D_FF = 4 * D_MODEL

kernel_name = "hybrid_gated_parallel_streaming_trunk"


def rmsnorm(x, g):
    xf = x.astype(jnp.float32)
    y = xf * lax.rsqrt(jnp.mean(xf * xf, axis=-1, keepdims=True) + EPS)
    return (y * g.astype(jnp.float32)).astype(x.dtype)


def split_columns(proj):
    parts, start = [], 0
    for size in IN_SIZES:
        parts.append(proj[..., start:start + size])
        start += size
    return parts


def window_sink_attention(q, k, v, sinks):
    B, S, _ = q.shape
    nb = S // ATT_BLOCK
    qb = q.reshape(B, nb, ATT_BLOCK, A_KV_HEADS, A_GROUP, HEAD_DIM)

    def band(t):
        t = t.reshape(B, S, A_KV_HEADS, HEAD_DIM)
        t = jnp.pad(t, ((0, 0), (ATT_BLOCK, 0), (0, 0), (0, 0)))
        t = t.reshape(B, nb + 1, ATT_BLOCK, A_KV_HEADS, HEAD_DIM)
        return jnp.concatenate([t[:, :-1], t[:, 1:]], axis=2)

    kb, vb = band(k), band(v)
    s = jnp.einsum('bnqhgd,bnkhd->bnhgqk', qb, kb).astype(jnp.float32) * (1.0 / math.sqrt(HEAD_DIM))

    blk = jnp.arange(nb)[:, None, None]
    qpos = blk * ATT_BLOCK + jnp.arange(ATT_BLOCK)[None, :, None]
    kpos = (blk - 1) * ATT_BLOCK + jnp.arange(2 * ATT_BLOCK)[None, None, :]
    qc, kc = qpos // CHUNK, kpos // CHUNK
    valid = (kpos >= 0) & (kc <= qc) & (kc >= qc - WINDOW_CHUNKS)
    s = jnp.where(valid[None, :, None, None], s, NEG_INF)

    sink = sinks.astype(jnp.float32).reshape(A_KV_HEADS, A_GROUP)[None, None, :, :, None, None]
    m = jnp.maximum(jnp.max(s, axis=-1, keepdims=True), sink)
    p = jnp.exp(s - m)
    p = p / (jnp.sum(p, axis=-1, keepdims=True) + jnp.exp(sink - m))
    o = jnp.einsum('bnhgqk,bnkhd->bnqhgd', p.astype(v.dtype), vb)
    return o.reshape(B, S, A_Q_DIM)


def spatial_gating(u, v, g_sgu, w_s, b_s):
    B, S, _ = u.shape
    nc = S // SGU_CHUNK
    u = jax.nn.gelu(u)
    v = rmsnorm(jax.nn.gelu(v), g_sgu)
    vb = v.reshape(B, nc, SGU_CHUNK, SGU_GROUPS, SGU_GROUP_DIM)
    pc = jnp.arange(SGU_CHUNK) // CHUNK
    mask = pc[None, :] <= pc[:, None]
    w = jnp.where(mask[None], w_s, 0.0).astype(v.dtype)
    sp = jnp.einsum('gij,bnjgc->bnigc', w, vb) + b_s.T[:, :, None].astype(v.dtype)
    return u * sp.reshape(B, S, SGU_DIM)


def multiscale_pool(c, w_pool, pool_scale):
    B, S, _ = c.shape
    cs = jnp.cumsum(c.astype(jnp.float32), axis=1)
    t = jnp.arange(S)
    outs = []
    for gi, w in enumerate(POOL_WINDOWS):
        cg = cs[..., gi * POOL_GROUP_DIM:(gi + 1) * POOL_GROUP_DIM]
        lag = jnp.pad(cg, ((0, 0), (w, 0), (0, 0)))[:, :S]
        cnt = jnp.minimum(t + 1, w).astype(jnp.float32)[None, :, None]
        outs.append((cg - lag) / cnt)
    pooled = jnp.concatenate(outs, axis=-1).astype(c.dtype) - c
    pooled = pooled.reshape(B, S, POOL_GROUPS, POOL_GROUP_DIM)
    mixed = jnp.einsum('bsgc,gcd->bsgd', pooled, w_pool).reshape(B, S, POOL_DIM)
    return mixed * pool_scale


def memory_attention(h, mem_n, w_q, w_kv, w_o):
    B, S, _ = h.shape
    q = (h @ w_q).reshape(B, S, MEM_HEADS, MEM_HEAD_DIM)
    kv = mem_n @ w_kv
    k = kv[..., :MEM_DIM].reshape(B, N_MEM, MEM_HEADS, MEM_HEAD_DIM)
    v = kv[..., MEM_DIM:].reshape(B, N_MEM, MEM_HEADS, MEM_HEAD_DIM)
    s = jnp.einsum('bshd,bmhd->bhsm', q, k).astype(jnp.float32) * (1.0 / math.sqrt(MEM_HEAD_DIM))
    p = jax.nn.softmax(s, axis=-1).astype(h.dtype)
    o = jnp.einsum('bhsm,bmhd->bshd', p, v).reshape(B, S, MEM_DIM)
    return o @ w_o


def _fwd_setup_inputs(seed: int = 0) -> dict:
    key = jax.random.key(seed)
    ks = jax.random.split(key, 20)
    f32 = jnp.float32

    def dense(k, shape, fan_in):
        return jax.random.normal(k, shape, f32) * (fan_in ** -0.5)

    return {
        "x": jax.random.normal(ks[0], (BATCH, SEQ, D_MODEL), f32),
        "mem": jax.random.normal(ks[1], (BATCH, N_MEM, D_MODEL), f32),
        "g_norm": 1.0 + 0.1 * jax.random.normal(ks[2], (DEPTH, N_NORMS, D_MODEL), f32),
        "g_mem": 1.0 + 0.1 * jax.random.normal(ks[3], (DEPTH, D_MODEL), f32),
        "w_in": dense(ks[4], (DEPTH, D_MODEL, IN_DIM), D_MODEL),
        "attn_sinks": 0.5 * jax.random.normal(ks[5], (DEPTH, A_Q_HEADS), f32),
        "w_spatial": dense(ks[6], (DEPTH, SGU_GROUPS, SGU_CHUNK, SGU_CHUNK), SGU_CHUNK),
        "b_spatial": 1.0 + 0.1 * jax.random.normal(ks[7], (DEPTH, SGU_GROUPS, SGU_CHUNK), f32),
        "g_sgu": 1.0 + 0.1 * jax.random.normal(ks[8], (DEPTH, SGU_DIM), f32),
        "w_pool": dense(ks[9], (DEPTH, POOL_GROUPS, POOL_GROUP_DIM, POOL_GROUP_DIM), POOL_GROUP_DIM),
        "pool_scale": 1.0 + 0.1 * jax.random.normal(ks[10], (DEPTH, POOL_DIM), f32),
        "w_branch": dense(ks[11], (DEPTH, N_BRANCH, BRANCH_DIM, D_MODEL), BRANCH_DIM),
        "w_out": dense(ks[12], (DEPTH, D_MODEL, D_MODEL), D_MODEL),
        "w_q_mem": dense(ks[13], (DEPTH, D_MODEL, MEM_DIM), D_MODEL),
        "w_kv_mem": dense(ks[14], (DEPTH, D_MODEL, 2 * MEM_DIM), D_MODEL),
        "w_o_mem": dense(ks[15], (DEPTH, MEM_DIM, D_MODEL), MEM_DIM),
        "w_up": dense(ks[16], (DEPTH, D_MODEL, D_FF), D_MODEL),
        "w_down": dense(ks[17], (DEPTH, D_FF, D_MODEL), D_FF),
    }


def _fwd_reference(x, mem, g_norm, g_mem, w_in, attn_sinks, w_spatial, b_spatial, g_sgu,
              w_pool, pool_scale, w_branch, w_out, w_q_mem, w_kv_mem, w_o_mem, w_up, w_down):
    B, S, _ = x.shape
    for l in range(DEPTH):
        h = rmsnorm(x, g_norm[l, 0])
        q, k, v, su, sv, pc, gate = split_columns(h @ w_in[l])
        ya = window_sink_attention(q, k, v, attn_sinks[l])
        yb = spatial_gating(su, sv, g_sgu[l], w_spatial[l], b_spatial[l])
        yc = multiscale_pool(pc, w_pool[l], pool_scale[l])
        branches = jnp.stack([ya, yb, yc], axis=2)
        proj = jnp.einsum('bsnc,ncd->bsnd', branches, w_branch[l])
        gates = jax.nn.sigmoid(gate.reshape(B, S, N_BRANCH, D_MODEL))
        merged = jnp.sum(gates * proj, axis=2)
        x = x + rmsnorm(merged @ w_out[l], g_norm[l, 1])
        hm = rmsnorm(x, g_norm[l, 2])
        mem_n = rmsnorm(mem, g_mem[l])
        ym = memory_attention(hm, mem_n, w_q_mem[l], w_kv_mem[l], w_o_mem[l])
        x = x + rmsnorm(ym, g_norm[l, 3])
        hf = rmsnorm(x, g_norm[l, 4])
        yf = jnp.square(jax.nn.relu(hf @ w_up[l])) @ w_down[l]
        x = x + rmsnorm(yf, g_norm[l, 5])
    return x


import jax as _jax
import jax.numpy as _jnp

TWIN_FORMAT = 'train_step'
FWD_PARAMS = ['x', 'mem', 'g_norm', 'g_mem', 'w_in', 'attn_sinks', 'w_spatial', 'b_spatial', 'g_sgu', 'w_pool', 'pool_scale', 'w_branch', 'w_out', 'w_q_mem', 'w_kv_mem', 'w_o_mem', 'w_up', 'w_down']
TWIN_WEIGHTS = ['g_norm', 'g_mem', 'w_in', 'attn_sinks', 'w_spatial', 'b_spatial', 'g_sgu', 'w_pool', 'pool_scale', 'w_branch', 'w_out', 'w_q_mem', 'w_kv_mem', 'w_o_mem', 'w_up', 'w_down']
TWIN_DIFF_INPUT = 'x'
TWIN_INPUTS = ['x', 'mem', 'g_norm', 'g_mem', 'w_in', 'attn_sinks', 'w_spatial', 'b_spatial', 'g_sgu', 'w_pool', 'pool_scale', 'w_branch', 'w_out', 'w_q_mem', 'w_kv_mem', 'w_o_mem', 'w_up', 'w_down', 'loss_target', 'm_g_norm', 'm_g_mem', 'm_w_in', 'm_attn_sinks', 'm_w_spatial', 'm_b_spatial', 'm_g_sgu', 'm_w_pool', 'm_pool_scale', 'm_w_branch', 'm_w_out', 'm_w_q_mem', 'm_w_kv_mem', 'm_w_o_mem', 'm_w_up', 'm_w_down', 'v_g_norm', 'v_g_mem', 'v_w_in', 'v_attn_sinks', 'v_w_spatial', 'v_b_spatial', 'v_g_sgu', 'v_w_pool', 'v_pool_scale', 'v_w_branch', 'v_w_out', 'v_w_q_mem', 'v_w_kv_mem', 'v_w_o_mem', 'v_w_up', 'v_w_down']
TWIN_OUTPUTS = ['loss', 'grad_x', 'grad_g_norm', 'grad_g_mem', 'grad_w_in', 'grad_attn_sinks', 'grad_w_spatial', 'grad_b_spatial', 'grad_g_sgu', 'grad_w_pool', 'grad_pool_scale', 'grad_w_branch', 'grad_w_out', 'grad_w_q_mem', 'grad_w_kv_mem', 'grad_w_o_mem', 'grad_w_up', 'grad_w_down', 'delta_g_norm', 'delta_g_mem', 'delta_w_in', 'delta_attn_sinks', 'delta_w_spatial', 'delta_b_spatial', 'delta_g_sgu', 'delta_w_pool', 'delta_pool_scale', 'delta_w_branch', 'delta_w_out', 'delta_w_q_mem', 'delta_w_kv_mem', 'delta_w_o_mem', 'delta_w_up', 'delta_w_down', 'new_m_g_norm', 'new_m_g_mem', 'new_m_w_in', 'new_m_attn_sinks', 'new_m_w_spatial', 'new_m_b_spatial', 'new_m_g_sgu', 'new_m_w_pool', 'new_m_pool_scale', 'new_m_w_branch', 'new_m_w_out', 'new_m_w_q_mem', 'new_m_w_kv_mem', 'new_m_w_o_mem', 'new_m_w_up', 'new_m_w_down', 'new_v_g_norm', 'new_v_g_mem', 'new_v_w_in', 'new_v_attn_sinks', 'new_v_w_spatial', 'new_v_b_spatial', 'new_v_g_sgu', 'new_v_w_pool', 'new_v_pool_scale', 'new_v_w_branch', 'new_v_w_out', 'new_v_w_q_mem', 'new_v_w_kv_mem', 'new_v_w_o_mem', 'new_v_w_up', 'new_v_w_down']
TWIN_LEAF_KINDS = {'loss': 'loss', 'grad_x': 'grad_x', 'grad_g_norm': 'grad_w', 'grad_g_mem': 'grad_w', 'grad_w_in': 'grad_w', 'grad_attn_sinks': 'grad_w', 'grad_w_spatial': 'grad_w', 'grad_b_spatial': 'grad_w', 'grad_g_sgu': 'grad_w', 'grad_w_pool': 'grad_w', 'grad_pool_scale': 'grad_w', 'grad_w_branch': 'grad_w', 'grad_w_out': 'grad_w', 'grad_w_q_mem': 'grad_w', 'grad_w_kv_mem': 'grad_w', 'grad_w_o_mem': 'grad_w', 'grad_w_up': 'grad_w', 'grad_w_down': 'grad_w', 'delta_g_norm': 'delta_w', 'delta_g_mem': 'delta_w', 'delta_w_in': 'delta_w', 'delta_attn_sinks': 'delta_w', 'delta_w_spatial': 'delta_w', 'delta_b_spatial': 'delta_w', 'delta_g_sgu': 'delta_w', 'delta_w_pool': 'delta_w', 'delta_pool_scale': 'delta_w', 'delta_w_branch': 'delta_w', 'delta_w_out': 'delta_w', 'delta_w_q_mem': 'delta_w', 'delta_w_kv_mem': 'delta_w', 'delta_w_o_mem': 'delta_w', 'delta_w_up': 'delta_w', 'delta_w_down': 'delta_w', 'new_m_g_norm': 'new_m', 'new_m_g_mem': 'new_m', 'new_m_w_in': 'new_m', 'new_m_attn_sinks': 'new_m', 'new_m_w_spatial': 'new_m', 'new_m_b_spatial': 'new_m', 'new_m_g_sgu': 'new_m', 'new_m_w_pool': 'new_m', 'new_m_pool_scale': 'new_m', 'new_m_w_branch': 'new_m', 'new_m_w_out': 'new_m', 'new_m_w_q_mem': 'new_m', 'new_m_w_kv_mem': 'new_m', 'new_m_w_o_mem': 'new_m', 'new_m_w_up': 'new_m', 'new_m_w_down': 'new_m', 'new_v_g_norm': 'new_v', 'new_v_g_mem': 'new_v', 'new_v_w_in': 'new_v', 'new_v_attn_sinks': 'new_v', 'new_v_w_spatial': 'new_v', 'new_v_b_spatial': 'new_v', 'new_v_g_sgu': 'new_v', 'new_v_w_pool': 'new_v', 'new_v_pool_scale': 'new_v', 'new_v_w_branch': 'new_v', 'new_v_w_out': 'new_v', 'new_v_w_q_mem': 'new_v', 'new_v_w_kv_mem': 'new_v', 'new_v_w_o_mem': 'new_v', 'new_v_w_up': 'new_v', 'new_v_w_down': 'new_v'}


def _forward(args):
    return _fwd_reference(*[args[k] for k in FWD_PARAMS])


def _output_shape():
    def fwd():
        inp = _fwd_setup_inputs(0)
        return _fwd_reference(*[inp[k] for k in FWD_PARAMS])
    out = _jax.eval_shape(fwd)
    return out.shape, out.dtype

N_MICROBATCH = 1
ADAM_LR = 0.001
ADAM_B1 = 0.9
ADAM_B2 = 0.999
ADAM_EPS = 1e-08
ADAM_WD = 0.01
ADAM_STEP = 10
PER_EXAMPLE_BATCH_AXIS = {'x': 0, 'mem': 0, 'loss_target': 0}
SHARED_INPUTS = []
_WEIGHT_DTYPES = {'g_norm': _jnp.float32, 'g_mem': _jnp.float32, 'w_in': _jnp.float32, 'attn_sinks': _jnp.float32, 'w_spatial': _jnp.float32, 'b_spatial': _jnp.float32, 'g_sgu': _jnp.float32, 'w_pool': _jnp.float32, 'pool_scale': _jnp.float32, 'w_branch': _jnp.float32, 'w_out': _jnp.float32, 'w_q_mem': _jnp.float32, 'w_kv_mem': _jnp.float32, 'w_o_mem': _jnp.float32, 'w_up': _jnp.float32, 'w_down': _jnp.float32}
MOMENT_SCALE = {'g_norm': 2.632030e+01, 'g_mem': 1.783373e+01, 'w_in': 4.024554e+00, 'attn_sinks': 3.978524e-01, 'w_spatial': 9.503797e-01, 'b_spatial': 1.455098e+00, 'g_sgu': 1.034359e+00, 'w_pool': 2.659196e+00, 'pool_scale': 2.946958e+00, 'w_branch': 7.517381e+00, 'w_out': 1.311040e+01, 'w_q_mem': 9.075711e+00, 'w_kv_mem': 1.713851e+01, 'w_o_mem': 1.572967e+01, 'w_up': 4.645725e+00, 'w_down': 1.846940e+01}


def _to_microbatches(a, axis):
    t = _jnp.moveaxis(a, axis, 0)
    t = t.reshape((N_MICROBATCH, t.shape[0] // N_MICROBATCH) + t.shape[1:])
    return _jnp.moveaxis(t, 1, axis + 1)


def setup_inputs(seed: int = 0) -> dict:
    inp = _fwd_setup_inputs(seed)
    key = _jax.random.fold_in(_jax.random.key(seed), 7919)
    shape, _ = _output_shape()
    out = dict(inp)
    out["loss_target"] = _jax.random.normal(_jax.random.fold_in(key, 0), shape, _jnp.float32)
    for i, name in enumerate(TWIN_WEIGHTS):
        w = inp[name].astype(_jnp.float32)
        if MOMENT_SCALE is None:
            s = _jnp.sqrt(_jnp.mean(_jnp.square(w)) + 1e-30)
        else:
            s = MOMENT_SCALE[name]
        km, kv = _jax.random.split(_jax.random.fold_in(key, i + 1))
        out[name] = w
        out["m_" + name] = s * _jax.random.normal(km, w.shape, _jnp.float32)
        out["v_" + name] = (s * s) * _jax.random.uniform(kv, w.shape, _jnp.float32, 0.5, 1.5)
    if N_MICROBATCH > 1:
        for name, axis in PER_EXAMPLE_BATCH_AXIS.items():
            out[name] = _to_microbatches(out[name], axis)
    return {'x': out['x'], 'mem': out['mem'], 'g_norm': out['g_norm'], 'g_mem': out['g_mem'], 'w_in': out['w_in'], 'attn_sinks': out['attn_sinks'], 'w_spatial': out['w_spatial'], 'b_spatial': out['b_spatial'], 'g_sgu': out['g_sgu'], 'w_pool': out['w_pool'], 'pool_scale': out['pool_scale'], 'w_branch': out['w_branch'], 'w_out': out['w_out'], 'w_q_mem': out['w_q_mem'], 'w_kv_mem': out['w_kv_mem'], 'w_o_mem': out['w_o_mem'], 'w_up': out['w_up'], 'w_down': out['w_down'], 'loss_target': out['loss_target'], 'm_g_norm': out['m_g_norm'], 'm_g_mem': out['m_g_mem'], 'm_w_in': out['m_w_in'], 'm_attn_sinks': out['m_attn_sinks'], 'm_w_spatial': out['m_w_spatial'], 'm_b_spatial': out['m_b_spatial'], 'm_g_sgu': out['m_g_sgu'], 'm_w_pool': out['m_w_pool'], 'm_pool_scale': out['m_pool_scale'], 'm_w_branch': out['m_w_branch'], 'm_w_out': out['m_w_out'], 'm_w_q_mem': out['m_w_q_mem'], 'm_w_kv_mem': out['m_w_kv_mem'], 'm_w_o_mem': out['m_w_o_mem'], 'm_w_up': out['m_w_up'], 'm_w_down': out['m_w_down'], 'v_g_norm': out['v_g_norm'], 'v_g_mem': out['v_g_mem'], 'v_w_in': out['v_w_in'], 'v_attn_sinks': out['v_attn_sinks'], 'v_w_spatial': out['v_w_spatial'], 'v_b_spatial': out['v_b_spatial'], 'v_g_sgu': out['v_g_sgu'], 'v_w_pool': out['v_w_pool'], 'v_pool_scale': out['v_pool_scale'], 'v_w_branch': out['v_w_branch'], 'v_w_out': out['v_w_out'], 'v_w_q_mem': out['v_w_q_mem'], 'v_w_kv_mem': out['v_w_kv_mem'], 'v_w_o_mem': out['v_w_o_mem'], 'v_w_up': out['v_w_up'], 'v_w_down': out['v_w_down']}


def _loss(weights, diff, rest, loss_target):
    with _jax.named_scope("forward"):
        args = {**rest, TWIN_DIFF_INPUT: diff, **{k: w.astype(_WEIGHT_DTYPES[k]) for k, w in weights.items()}}
        y = _forward(args)
    with _jax.named_scope("loss_head"):
        err = _jnp.square(y.astype(_jnp.float32) - loss_target)
        return 0.5 * _jnp.sum(_jnp.mean(err, axis=-1)) if err.ndim else 0.5 * err


def _adamw(w, g, m, v):
    m = ADAM_B1 * m + (1.0 - ADAM_B1) * g
    v = ADAM_B2 * v + (1.0 - ADAM_B2) * _jnp.square(g)
    m_hat = m / (1.0 - ADAM_B1 ** ADAM_STEP)
    v_hat = v / (1.0 - ADAM_B2 ** ADAM_STEP)
    delta = -ADAM_LR * (m_hat / (_jnp.sqrt(v_hat) + ADAM_EPS) + ADAM_WD * w)
    return delta, m, v


def reference(x, mem, g_norm, g_mem, w_in, attn_sinks, w_spatial, b_spatial, g_sgu, w_pool, pool_scale, w_branch, w_out, w_q_mem, w_kv_mem, w_o_mem, w_up, w_down, loss_target, m_g_norm, m_g_mem, m_w_in, m_attn_sinks, m_w_spatial, m_b_spatial, m_g_sgu, m_w_pool, m_pool_scale, m_w_branch, m_w_out, m_w_q_mem, m_w_kv_mem, m_w_o_mem, m_w_up, m_w_down, v_g_norm, v_g_mem, v_w_in, v_attn_sinks, v_w_spatial, v_b_spatial, v_g_sgu, v_w_pool, v_pool_scale, v_w_branch, v_w_out, v_w_q_mem, v_w_kv_mem, v_w_o_mem, v_w_up, v_w_down):
    given = dict(x=x, mem=mem, g_norm=g_norm, g_mem=g_mem, w_in=w_in, attn_sinks=attn_sinks, w_spatial=w_spatial, b_spatial=b_spatial, g_sgu=g_sgu, w_pool=w_pool, pool_scale=pool_scale, w_branch=w_branch, w_out=w_out, w_q_mem=w_q_mem, w_kv_mem=w_kv_mem, w_o_mem=w_o_mem, w_up=w_up, w_down=w_down, loss_target=loss_target, m_g_norm=m_g_norm, m_g_mem=m_g_mem, m_w_in=m_w_in, m_attn_sinks=m_attn_sinks, m_w_spatial=m_w_spatial, m_b_spatial=m_b_spatial, m_g_sgu=m_g_sgu, m_w_pool=m_w_pool, m_pool_scale=m_pool_scale, m_w_branch=m_w_branch, m_w_out=m_w_out, m_w_q_mem=m_w_q_mem, m_w_kv_mem=m_w_kv_mem, m_w_o_mem=m_w_o_mem, m_w_up=m_w_up, m_w_down=m_w_down, v_g_norm=v_g_norm, v_g_mem=v_g_mem, v_w_in=v_w_in, v_attn_sinks=v_attn_sinks, v_w_spatial=v_w_spatial, v_b_spatial=v_b_spatial, v_g_sgu=v_g_sgu, v_w_pool=v_w_pool, v_pool_scale=v_pool_scale, v_w_branch=v_w_branch, v_w_out=v_w_out, v_w_q_mem=v_w_q_mem, v_w_kv_mem=v_w_kv_mem, v_w_o_mem=v_w_o_mem, v_w_up=v_w_up, v_w_down=v_w_down)
    weights = {n: given[n] for n in TWIN_WEIGHTS}
    shared = {n: given[n] for n in SHARED_INPUTS}
    per_example = {n: given[n] for n in ['x', 'mem']}
    grad_fn = _jax.value_and_grad(_loss, argnums=(0, 1))

    def one_microbatch(ex, loss_target):
        ex = dict(ex)
        diff = ex.pop(TWIN_DIFF_INPUT)
        return grad_fn(weights, diff, {**shared, **ex}, loss_target)

    if N_MICROBATCH == 1:
        loss, (grad_w, grad_x) = one_microbatch(per_example, given["loss_target"])
    else:
        def body(carry, xs):
            loss_sum, grad_sum = carry
            l_k, (gw_k, gx_k) = one_microbatch(xs[0], xs[1])
            with _jax.named_scope("update"):
                return (loss_sum + l_k, _jax.tree.map(_jnp.add, grad_sum, gw_k)), gx_k

        init = (_jnp.zeros((), _jnp.float32), _jax.tree.map(_jnp.zeros_like, weights))
        (loss, grad_w), grad_x = _jax.lax.scan(body, init, (per_example, given["loss_target"]))
    with _jax.named_scope("update"):
        delta_w, new_m, new_v = {}, {}, {}
        for n in TWIN_WEIGHTS:
            delta_w[n], new_m[n], new_v[n] = _adamw(weights[n], grad_w[n], given["m_" + n], given["v_" + n])
    return (loss, grad_x, *[grad_w[n] for n in TWIN_WEIGHTS], *[delta_w[n] for n in TWIN_WEIGHTS],
            *[new_m[n] for n in TWIN_WEIGHTS], *[new_v[n] for n in TWIN_WEIGHTS])
```

```python
import functools
import math

import jax
import jax.numpy as jnp
from jax import lax
from jax.experimental import pallas as pl
from jax.experimental.pallas import tpu as pltpu

F32 = jnp.float32
BF16 = jnp.bfloat16
MESH = pl.DeviceIdType.MESH

EPS = 1e-6
NEG_INF = -1e30
BLK = 128
HALO = 16
POOL_WINDOWS = (2, 4, 8, 16)
ATT_SCALE = 1.0 / math.sqrt(64.0)
MEM_SCALE = 1.0 / math.sqrt(128.0)
C_Q, C_K, C_V, C_SU, C_SV, C_PC, C_GATE, C_END = 0, 512, 640, 768, 1280, 1792, 2304, 5376

ADAM_LR, ADAM_B1, ADAM_B2, ADAM_EPS, ADAM_WD, ADAM_STEP = 0.001, 0.9, 0.999, 1e-08, 0.01, 10

VMEM_LIMIT_BYTES = 56 * 1024 * 1024

_DIMS = {
    "nn": (((1,), (0,)), ((), ())),
    "nt": (((1,), (1,)), ((), ())),
    "tn": (((0,), (0,)), ((), ())),
}


def _dot(a, b, mode):
    return lax.dot_general(a, b, _DIMS[mode], preferred_element_type=F32)


def _params(semantics):
    return pltpu.CompilerParams(dimension_semantics=semantics, vmem_limit_bytes=VMEM_LIMIT_BYTES)


def _tile(dim, pref):
    if dim <= pref:
        return dim
    t = (pref // 128) * 128
    while t >= 128:
        if dim % t == 0:
            return t
        t -= 128
    raise ValueError(f"no tile for {dim}")


def _rms(x, g):
    return x * lax.rsqrt(jnp.mean(x * x, axis=-1, keepdims=True) + EPS) * g


def _mm(a, b, mode, name, *, out_dtypes=(F32,), a_pre=(), b_pre=(), into=None, out_pre=(),
        extras=(), epi=None, tm=1024, tn=1024, tk=1024):
    a2, b2 = a.shape[len(a_pre):], b.shape[len(b_pre):]
    if mode == "nn":
        (M, K), (K2, N) = a2, b2
    elif mode == "nt":
        (M, K), (N, K2) = a2, b2
    else:
        (K, M), (K2, N) = a2, b2
    assert K == K2, (a.shape, b.shape, mode)
    tm, tn, tk = _tile(M, tm), _tile(N, tn), _tile(K, tk)
    nk = K // tk
    na, nb_, no = len(a_pre), len(b_pre), len(out_pre)
    if mode == "tn":
        a_spec = pl.BlockSpec((None,) * na + (tk, tm), lambda i, j, k: a_pre + (k, i))
    else:
        a_spec = pl.BlockSpec((None,) * na + (tm, tk), lambda i, j, k: a_pre + (i, k))
    if mode == "nt":
        b_spec = pl.BlockSpec((None,) * nb_ + (tn, tk), lambda i, j, k: b_pre + (j, k))
    else:
        b_spec = pl.BlockSpec((None,) * nb_ + (tk, tn), lambda i, j, k: b_pre + (k, j))
    tile_spec = pl.BlockSpec((tm, tn), lambda i, j, k: (i, j))
    ne, nout = len(extras), len(out_dtypes)
    in_specs = [a_spec, b_spec] + [tile_spec] * ne
    operands = [a, b, *extras]
    aliases = {}
    if into is not None:
        assert nout == 1
        in_specs.append(pl.BlockSpec(memory_space=pl.ANY))
        operands.append(into)
        aliases = {len(operands) - 1: 0}
        out_shape = [jax.ShapeDtypeStruct(into.shape, into.dtype)]
        out_specs = [pl.BlockSpec((None,) * no + (tm, tn), lambda i, j, k: out_pre + (i, j))]
    else:
        out_shape = [jax.ShapeDtypeStruct((M, N), dt) for dt in out_dtypes]
        out_specs = [tile_spec] * nout

    def body(*refs):
        a_ref, b_ref = refs[0], refs[1]
        ex = refs[2:2 + ne]
        pos = 2 + ne + (1 if into is not None else 0)
        outs = refs[pos:pos + nout]
        acc_ref = refs[pos + nout] if nk > 1 else None

        def finish(acc):
            vals = epi(acc, *[e[...] for e in ex]) if epi is not None else (acc,)
            for o, v in zip(outs, vals):
                o[...] = v.astype(o.dtype)

        def prod():
            return _dot(a_ref[...].astype(BF16), b_ref[...].astype(BF16), mode)

        if nk == 1:
            finish(prod())
        else:
            k = pl.program_id(2)

            @pl.when(k == 0)
            def _():
                acc_ref[...] = jnp.zeros_like(acc_ref)

            acc_ref[...] += prod()

            @pl.when(k == nk - 1)
            def _():
                finish(acc_ref[...])

    res = pl.pallas_call(
        body, name=name, grid=(M // tm, N // tn, nk),
        in_specs=in_specs, out_specs=out_specs, out_shape=out_shape,
        scratch_shapes=[pltpu.VMEM((tm, tn), F32)] if nk > 1 else [],
        input_output_aliases=aliases,
        compiler_params=_params(("parallel", "parallel", "arbitrary")),
    )(*operands)
    return res[0] if nout == 1 else tuple(res)


def _resnorm_fn(has_post, has_pre):
    def f(*a):
        x, k = a[0], 1
        if has_post:
            x, k = x + _rms(a[1], a[2]), 3
        outs = [x]
        if has_pre:
            outs.append(_rms(x, a[k]))
        return tuple(outs)
    return f


def _row_spec(T, W):
    return pl.BlockSpec((T, W), lambda i: (i, 0))


def _par_spec(W):
    return pl.BlockSpec((1, W), lambda i: (0, 0))


def _resnorm_fwd(xr, y, gp, gq, name, T=512):
    S, D = xr.shape
    T = min(T, S)
    has_post, has_pre = y is not None, gq is not None
    f = _resnorm_fn(has_post, has_pre)
    ins = [xr] + ([y, gp] if has_post else []) + ([gq] if has_pre else [])
    in_specs = [_row_spec(T, D)] + ([_row_spec(T, D), _par_spec(D)] if has_post else []) + ([_par_spec(D)] if has_pre else [])
    out_shape, out_specs = [], []
    if has_post:
        out_shape.append(jax.ShapeDtypeStruct((S, D), F32)); out_specs.append(_row_spec(T, D))
    if has_pre:
        out_shape.append(jax.ShapeDtypeStruct((S, D), BF16)); out_specs.append(_row_spec(T, D))
    n_in = len(ins)

    def body(*refs):
        vals = f(*[r[...] for r in refs[:n_in]])
        outs = list(refs[n_in:])
        if has_post:
            outs.pop(0)[...] = vals[0]
        if has_pre:
            outs.pop(0)[...] = vals[1].astype(BF16)

    res = pl.pallas_call(body, name=name, grid=(S // T,), in_specs=in_specs, out_specs=out_specs,
                         out_shape=out_shape, compiler_params=_params(("parallel",)))(*ins)
    return tuple(res)


def _resnorm_bwd(xr, y, gp, gq, dres, dh, name, T=512):
    S, D = xr.shape
    T = min(T, S)
    has_post, has_pre, has_res = y is not None, gq is not None, dres is not None
    f = _resnorm_fn(has_post, has_pre)
    ins = [xr] + ([y, gp] if has_post else []) + ([gq] if has_pre else [])
    in_specs = [_row_spec(T, D)] + ([_row_spec(T, D), _par_spec(D)] if has_post else []) + ([_par_spec(D)] if has_pre else [])
    n_prim = len(ins)
    if has_res:
        ins.append(dres); in_specs.append(_row_spec(T, D))
    if has_pre:
        ins.append(dh); in_specs.append(_row_spec(T, D))
    n_in = len(ins)
    out_shape = [jax.ShapeDtypeStruct((S, D), F32)]
    out_specs = [_row_spec(T, D)]
    if has_post:
        out_shape += [jax.ShapeDtypeStruct((S, D), BF16), jax.ShapeDtypeStruct((1, D), F32)]
        out_specs += [_row_spec(T, D), _par_spec(D)]
    if has_pre:
        out_shape.append(jax.ShapeDtypeStruct((1, D), F32)); out_specs.append(_par_spec(D))

    def body(*refs):
        i = pl.program_id(0)
        prim = [r[...] for r in refs[:n_prim]]
        rest = list(refs[n_prim:n_in])
        ct_x = rest.pop(0)[...] if has_res else jnp.zeros((T, D), F32)
        cts = [ct_x]
        if has_pre:
            cts.append(rest.pop(0)[...].astype(F32))
        _, vjp = jax.vjp(f, *prim)
        grads = list(vjp(tuple(cts)))
        outs = list(refs[n_in:])
        outs.pop(0)[...] = grads.pop(0)
        acc = []
        if has_post:
            outs.pop(0)[...] = grads.pop(0).astype(BF16)
            acc.append((outs.pop(0), grads.pop(0)))
        if has_pre:
            acc.append((outs.pop(0), grads.pop(0)))

        @pl.when(i == 0)
        def _():
            for o, _g in acc:
                o[...] = jnp.zeros_like(o)

        for o, g in acc:
            o[...] += g

    res = pl.pallas_call(body, name=name, grid=(S // T,), in_specs=in_specs, out_specs=out_specs,
                         out_shape=out_shape, compiler_params=_params(("arbitrary",)))(*ins)
    return tuple(res)


def _final_fwd(xr, y, gp, target, name, T=512):
    S, D = xr.shape
    T = min(T, S)

    def body(x_ref, y_ref, g_ref, t_ref, dy_ref, loss_ref):
        i = pl.program_id(0)
        e = x_ref[...] + _rms(y_ref[...], g_ref[...]) - t_ref[...]
        dy_ref[...] = e / D

        @pl.when(i == 0)
        def _():
            loss_ref[...] = jnp.zeros_like(loss_ref)

        loss_ref[...] += 0.5 * jnp.sum(jnp.sum(e * e, axis=-1, keepdims=True) / D, axis=0, keepdims=True)

    return pl.pallas_call(
        body, name=name, grid=(S // T,),
        in_specs=[_row_spec(T, D), _row_spec(T, D), _par_spec(D), _row_spec(T, D)],
        out_specs=[_row_spec(T, D), pl.BlockSpec((1, 128), lambda i: (0, 0))],
        out_shape=[jax.ShapeDtypeStruct((S, D), F32), jax.ShapeDtypeStruct((1, 128), F32)],
        compiler_params=_params(("arbitrary",)))(xr, y, gp, target)


def _lane_lo():
    return lax.broadcasted_iota(jnp.int32, (1, BLK), 1) < 64


def _att_mask(not_first):
    r = lax.broadcasted_iota(jnp.int32, (BLK, 2 * BLK), 0)
    c = lax.broadcasted_iota(jnp.int32, (BLK, 2 * BLK), 1)
    qc, kc = 2 + r // 64, c // 64
    return (kc <= qc) & (kc >= qc - 2) & (not_first | (c >= BLK))


def _softmax_sink(s, sk):
    m = jnp.maximum(jnp.max(s, axis=-1, keepdims=True), sk)
    e = jnp.exp(s - m)
    es = jnp.exp(sk - m)
    z = jnp.sum(e, axis=-1, keepdims=True) + es
    return e / z, es / z


def _att_operands(cur, kvp, t, lo):
    h = t // 2
    qt = cur[:, C_Q + BLK * t:C_Q + BLK * (t + 1)]
    q_lo = jnp.where(lo, qt, 0.0).astype(BF16)
    q_hi = jnp.where(lo, 0.0, qt).astype(BF16)
    kband = jnp.concatenate([kvp[:, 0:BLK], cur[:, C_K:C_K + BLK]], axis=0)
    vband = jnp.concatenate([kvp[:, BLK:2 * BLK], cur[:, C_V:C_V + BLK]], axis=0)
    kroll = pltpu.roll(kband, 64, 1)
    vroll = pltpu.roll(vband, 64, 1)
    ka, kb = (kband, kroll) if h == 0 else (kroll, kband)
    va = jnp.where(lo, vband if h == 0 else vroll, 0.0)
    vb = jnp.where(lo, 0.0, vroll if h == 0 else vband)
    return q_lo, q_hi, ka.astype(BF16), kb.astype(BF16), va.astype(BF16), vb.astype(BF16)


def _sgu_mask():
    r = lax.broadcasted_iota(jnp.int32, (BLK, BLK), 0)
    c = lax.broadcasted_iota(jnp.int32, (BLK, BLK), 1)
    return (c // 64) <= (r // 64)


def _pool_cnt(blk, w):
    t = blk * BLK + lax.broadcasted_iota(jnp.int32, (BLK, 1), 0)
    return jnp.minimum(t + 1, w).astype(F32)


def _mix_in_specs(nb, rev):
    def b(i):
        return nb - 1 - i if rev else i
    return [
        pl.BlockSpec((BLK, C_GATE), lambda i: (b(i), 0)),
        pl.BlockSpec((BLK, 2 * BLK), lambda i: (jnp.maximum(b(i) - 1, 0), C_K // (2 * BLK))),
        pl.BlockSpec((HALO, C_GATE), lambda i: (jnp.maximum(b(i) * (BLK // HALO) - 1, 0), 0)),
        pl.BlockSpec((8, BLK), lambda i: (0, 0)),
        pl.BlockSpec((4, BLK, BLK), lambda i: (0, 0, 0)),
        pl.BlockSpec((4, BLK, 1), lambda i: (0, 0, 0)),
        pl.BlockSpec((1, 512), lambda i: (0, 0)),
        pl.BlockSpec((4, BLK, BLK), lambda i: (0, 0, 0)),
        pl.BlockSpec((1, 512), lambda i: (0, 0)),
    ]


def _mix_fwd(proj, sinks_b, ws, bs3, gsgu, wp, ps, name):
    S = proj.shape[0]
    nb = S // BLK

    def body(cur_ref, kvp_ref, pcp_ref, sk_ref, ws_ref, bs_ref, gs_ref, wp_ref, ps_ref, br_ref, ext_ref):
        i = pl.program_id(0)
        not_first = i > 0
        lo = _lane_lo()
        cur, kvp = cur_ref[...], kvp_ref[...]
        mask = _att_mask(not_first)
        for t in range(4):
            q_lo, q_hi, ka, kb, va, vb = _att_operands(cur, kvp, t, lo)
            s_lo = jnp.where(mask, _dot(q_lo, ka, "nt") * ATT_SCALE, NEG_INF)
            s_hi = jnp.where(mask, _dot(q_hi, kb, "nt") * ATT_SCALE, NEG_INF)
            p_lo, _ = _softmax_sink(s_lo, sk_ref[2 * t:2 * t + 1, 0:1])
            p_hi, _ = _softmax_sink(s_hi, sk_ref[2 * t + 1:2 * t + 2, 0:1])
            o = _dot(p_lo.astype(BF16), va, "nn") + _dot(p_hi.astype(BF16), vb, "nn")
            br_ref[0, :, BLK * t:BLK * (t + 1)] = o.astype(BF16)
        gu = jax.nn.gelu(cur[:, C_SU:C_SV])
        vn = _rms(jax.nn.gelu(cur[:, C_SV:C_PC]), gs_ref[...]).astype(BF16)
        wmask = _sgu_mask()
        for g in range(4):
            wm = jnp.where(wmask, ws_ref[g], 0.0).astype(BF16)
            sp = _dot(wm, vn[:, BLK * g:BLK * (g + 1)], "nn") + bs_ref[g]
            br_ref[1, :, BLK * g:BLK * (g + 1)] = (gu[:, BLK * g:BLK * (g + 1)] * sp).astype(BF16)
        c = cur[:, C_PC:C_GATE]
        ext_ref[0:HALO, :] = jnp.where(not_first, pcp_ref[:, C_PC:C_GATE], 0.0)
        ext_ref[HALO:HALO + BLK, :] = c
        for g, w in enumerate(POOL_WINDOWS):
            sl = slice(BLK * g, BLK * (g + 1))
            acc = ext_ref[HALO:HALO + BLK, sl]
            for k in range(1, w):
                acc = acc + ext_ref[HALO - k:HALO - k + BLK, sl]
            pooled = acc / _pool_cnt(i, w) - c[:, sl]
            mixed = _dot(pooled.astype(BF16), wp_ref[g].astype(BF16), "nn")
            br_ref[2, :, sl] = (mixed * ps_ref[:, sl]).astype(BF16)

    return pl.pallas_call(
        body, name=name, grid=(nb,),
        in_specs=_mix_in_specs(nb, False),
        out_specs=pl.BlockSpec((3, BLK, 512), lambda i: (0, i, 0)),
        out_shape=jax.ShapeDtypeStruct((3, S, 512), BF16),
        scratch_shapes=[pltpu.VMEM((HALO + BLK, 512), F32)],
        compiler_params=_params(("parallel",)),
    )(proj, proj, proj, sinks_b, ws, bs3, gsgu, wp, ps)


def _mix_bwd(proj, dbr, dproj, sinks_b, ws, bs3, gsgu, wp, ps, name):
    S = proj.shape[0]
    nb = S // BLK

    def body(cur_ref, kvp_ref, pcp_ref, sk_ref, ws_ref, bs_ref, gs_ref, wp_ref, ps_ref, dbr_ref, _dproj_in,
             dp_ref, dsk_ref, dws_ref, dbs_ref, dgs_ref, dwp_ref, dps_ref,
             ext_ref, z_ref, ckv_ref, cpc_ref):
        i = pl.program_id(0)
        blk = nb - 1 - i
        not_first = blk > 0
        lo = _lane_lo()

        @pl.when(i == 0)
        def _():
            for r in (dsk_ref, dws_ref, dbs_ref, dgs_ref, dwp_ref, dps_ref, ckv_ref, cpc_ref, z_ref):
                r[...] = jnp.zeros_like(r)

        cur, kvp = cur_ref[...], kvp_ref[...]
        mask = _att_mask(not_first)
        dk_band = jnp.zeros((2 * BLK, BLK), F32)
        dk_roll = jnp.zeros((2 * BLK, BLK), F32)
        dv_band = jnp.zeros((2 * BLK, BLK), F32)
        dv_roll = jnp.zeros((2 * BLK, BLK), F32)
        for t in range(4):
            h = t // 2
            q_lo, q_hi, ka, kb, va, vb = _att_operands(cur, kvp, t, lo)
            do = dbr_ref[0, :, BLK * t:BLK * (t + 1)].astype(BF16)
            dq = jnp.zeros((BLK, BLK), F32)
            for half, (qm, km, vm) in enumerate(((q_lo, ka, va), (q_hi, kb, vb))):
                sk = sk_ref[2 * t + half:2 * t + half + 1, 0:1]
                s = jnp.where(mask, _dot(qm, km, "nt") * ATT_SCALE, NEG_INF)
                p, p_sink = _softmax_sink(s, sk)
                dp = _dot(do, vm, "nt")
                rs = jnp.sum(p * dp, axis=-1, keepdims=True)
                ds = (p * (dp - rs) * ATT_SCALE).astype(BF16)
                dsk_ref[2 * t + half:2 * t + half + 1, :] += jnp.broadcast_to(
                    -jnp.sum(p_sink * rs, axis=0, keepdims=True), (1, BLK))
                dvm = _dot(p.astype(BF16), do, "tn")
                dkm = _dot(ds, qm, "tn")
                dqm = _dot(ds, km, "nn")
                if half == 0:
                    dq = dq + jnp.where(lo, dqm, 0.0)
                    dvm = jnp.where(lo, dvm, 0.0)
                else:
                    dq = dq + jnp.where(lo, 0.0, dqm)
                    dvm = jnp.where(lo, 0.0, dvm)
                if (h == 0) == (half == 0):
                    dk_band, dv_band = dk_band + dkm, dv_band + dvm
                else:
                    dk_roll, dv_roll = dk_roll + dkm, dv_roll + dvm
            dp_ref[:, C_Q + BLK * t:C_Q + BLK * (t + 1)] = dq.astype(BF16)
        dk = dk_band + pltpu.roll(dk_roll, 64, 1)
        dv = dv_band + pltpu.roll(dv_roll, 64, 1)
        dp_ref[:, C_K:C_K + BLK] = (dk[BLK:] + ckv_ref[:, 0:BLK]).astype(BF16)
        dp_ref[:, C_V:C_V + BLK] = (dv[BLK:] + ckv_ref[:, BLK:]).astype(BF16)
        ckv_ref[:, 0:BLK] = dk[:BLK]
        ckv_ref[:, BLK:] = dv[:BLK]
        su, sv = cur[:, C_SU:C_SV], cur[:, C_SV:C_PC]
        gu, vjp_u = jax.vjp(jax.nn.gelu, su)
        vn, vjp_v = jax.vjp(lambda a, g: _rms(jax.nn.gelu(a), g), sv, gs_ref[...])
        vn16 = vn.astype(BF16)
        wmask = _sgu_mask()
        dgu, dvn = [], []
        for g in range(4):
            sl = slice(BLK * g, BLK * (g + 1))
            wm = jnp.where(wmask, ws_ref[g], 0.0).astype(BF16)
            sp = _dot(wm, vn16[:, sl], "nn") + bs_ref[g]
            dyb = dbr_ref[1, :, sl]
            dgu.append(dyb * sp)
            dsp = dyb * gu[:, sl]
            dsp16 = dsp.astype(BF16)
            dvn.append(_dot(wm, dsp16, "tn"))
            dws_ref[g] += jnp.where(wmask, _dot(dsp16, vn16[:, sl], "nt"), 0.0)
            dbs_ref[g] += jnp.sum(dsp, axis=1, keepdims=True)
        (dsu,) = vjp_u(jnp.concatenate(dgu, axis=1))
        dsv, dgs = vjp_v(jnp.concatenate(dvn, axis=1))
        dp_ref[:, C_SU:C_SV] = dsu.astype(BF16)
        dp_ref[:, C_SV:C_PC] = dsv.astype(BF16)
        dgs_ref[...] += dgs
        c = cur[:, C_PC:C_GATE]
        ext_ref[0:HALO, :] = jnp.where(not_first, pcp_ref[:, C_PC:C_GATE], 0.0)
        ext_ref[HALO:HALO + BLK, :] = c
        for g, w in enumerate(POOL_WINDOWS):
            sl = slice(BLK * g, BLK * (g + 1))
            acc = ext_ref[HALO:HALO + BLK, sl]
            for k in range(1, w):
                acc = acc + ext_ref[HALO - k:HALO - k + BLK, sl]
            cnt = _pool_cnt(blk, w)
            pooled16 = (acc / cnt - c[:, sl]).astype(BF16)
            wp16 = wp_ref[g].astype(BF16)
            mixed = _dot(pooled16, wp16, "nn")
            dyc = dbr_ref[2, :, sl]
            dps_ref[:, sl] += jnp.sum(dyc * mixed, axis=0, keepdims=True)
            dmixed16 = (dyc * ps_ref[:, sl]).astype(BF16)
            dwp_ref[g] += _dot(pooled16, dmixed16, "tn")
            dpooled = _dot(dmixed16, wp16, "nt")
            z_ref[HALO:HALO + BLK, sl] = dpooled / cnt
            dext = z_ref[0:HALO + BLK, sl]
            for k in range(1, w):
                dext = dext + z_ref[k:k + HALO + BLK, sl]
            dp_ref[:, C_PC + BLK * g:C_PC + BLK * (g + 1)] = (
                dext[HALO:] - dpooled + jnp.concatenate([jnp.zeros((BLK - HALO, BLK), F32), cpc_ref[:, sl]], axis=0)
            ).astype(BF16)
            cpc_ref[:, sl] = dext[:HALO]

    n_in = 11
    small = [jax.ShapeDtypeStruct((8, BLK), F32), jax.ShapeDtypeStruct((4, BLK, BLK), F32),
             jax.ShapeDtypeStruct((4, BLK, 1), F32), jax.ShapeDtypeStruct((1, 512), F32),
             jax.ShapeDtypeStruct((4, BLK, BLK), F32), jax.ShapeDtypeStruct((1, 512), F32)]
    small_specs = [pl.BlockSpec((8, BLK), lambda i: (0, 0)), pl.BlockSpec((4, BLK, BLK), lambda i: (0, 0, 0)),
                   pl.BlockSpec((4, BLK, 1), lambda i: (0, 0, 0)), pl.BlockSpec((1, 512), lambda i: (0, 0)),
                   pl.BlockSpec((4, BLK, BLK), lambda i: (0, 0, 0)), pl.BlockSpec((1, 512), lambda i: (0, 0))]
    res = pl.pallas_call(
        body, name=name, grid=(nb,),
        in_specs=_mix_in_specs(nb, True) + [
            pl.BlockSpec((3, BLK, 512), lambda i: (0, nb - 1 - i, 0)),
            pl.BlockSpec(memory_space=pl.ANY)],
        out_specs=[pl.BlockSpec((BLK, C_GATE), lambda i: (nb - 1 - i, 0))] + small_specs,
        out_shape=[jax.ShapeDtypeStruct(dproj.shape, dproj.dtype)] + small,
        scratch_shapes=[pltpu.VMEM((HALO + BLK, 512), F32), pltpu.VMEM((2 * HALO + BLK, 512), F32),
                        pltpu.VMEM((BLK, 2 * BLK), F32), pltpu.VMEM((HALO, 512), F32)],
        input_output_aliases={n_in - 1: 0},
        compiler_params=_params(("arbitrary",)),
    )(proj, proj, proj, sinks_b, ws, bs3, gsgu, wp, ps, dbr, dproj)
    return tuple(res)


_GW = 256


def _merge_fwd(proj, pb, name, T=1024):
    S, D = pb.shape[1], pb.shape[2]
    T = min(T, S)

    def body(gate_ref, pb_ref, out_ref, acc_ref):
        n = pl.program_id(2)

        @pl.when(n == 0)
        def _():
            acc_ref[...] = jnp.zeros_like(acc_ref)

        acc_ref[...] += jax.nn.sigmoid(gate_ref[...]) * pb_ref[...]

        @pl.when(n == 2)
        def _():
            out_ref[...] = acc_ref[...].astype(BF16)

    return pl.pallas_call(
        body, name=name, grid=(S // T, D // _GW, 3),
        in_specs=[pl.BlockSpec((T, _GW), lambda i, j, n: (i, C_GATE // _GW + n * (D // _GW) + j)),
                  pl.BlockSpec((None, T, _GW), lambda i, j, n: (n, i, j))],
        out_specs=pl.BlockSpec((T, _GW), lambda i, j, n: (i, j)),
        out_shape=jax.ShapeDtypeStruct((S, D), BF16),
        scratch_shapes=[pltpu.VMEM((T, _GW), F32)],
        compiler_params=_params(("parallel", "parallel", "arbitrary")),
    )(proj, pb)


def _merge_bwd(proj, pb, dmerged, name, T=1024):
    S, D = pb.shape[1], pb.shape[2]
    T = min(T, S)

    def body(gate_ref, pb_ref, dm_ref, dgate_ref, dpb_ref):
        sg = jax.nn.sigmoid(gate_ref[...])
        dm = dm_ref[...]
        dpb_ref[...] = (dm * sg).astype(BF16)
        dgate_ref[...] = (dm * pb_ref[...] * sg * (1.0 - sg)).astype(BF16)

    gate_map = lambda i, n, j: (i, C_GATE // _GW + n * (D // _GW) + j)
    return pl.pallas_call(
        body, name=name, grid=(S // T, 3, D // _GW),
        in_specs=[pl.BlockSpec((T, _GW), gate_map),
                  pl.BlockSpec((None, T, _GW), lambda i, n, j: (n, i, j)),
                  pl.BlockSpec((T, _GW), lambda i, n, j: (i, j))],
        out_specs=[pl.BlockSpec((T, _GW), gate_map),
                   pl.BlockSpec((None, T, _GW), lambda i, n, j: (n, i, j))],
        out_shape=[jax.ShapeDtypeStruct((S, C_END), BF16), jax.ShapeDtypeStruct((3, S, D), BF16)],
        compiler_params=_params(("parallel", "parallel", "parallel")),
    )(proj, pb, dmerged)


def _memattn_fwd(qm, kv, name, T=512):
    S, NM = qm.shape[0], kv.shape[0]
    T = min(T, S)

    def body(q_ref, kv_ref, o_ref):
        for h in range(4):
            sl = slice(128 * h, 128 * (h + 1))
            k = kv_ref[:, sl].astype(BF16)
            v = kv_ref[:, 512 + 128 * h:512 + 128 * (h + 1)].astype(BF16)
            s = _dot(q_ref[:, sl].astype(BF16), k, "nt") * MEM_SCALE
            p = jax.nn.softmax(s, axis=-1)
            o_ref[:, sl] = _dot(p.astype(BF16), v, "nn").astype(BF16)

    return pl.pallas_call(
        body, name=name, grid=(S // T,),
        in_specs=[_row_spec(T, 512), pl.BlockSpec((NM, 1024), lambda i: (0, 0))],
        out_specs=_row_spec(T, 512), out_shape=jax.ShapeDtypeStruct((S, 512), BF16),
        compiler_params=_params(("parallel",)))(qm, kv)


def _memattn_bwd(qm, kv, dom, name, T=512):
    S, NM = qm.shape[0], kv.shape[0]
    T = min(T, S)

    def body(q_ref, kv_ref, do_ref, dq_ref, dkv_ref):
        i = pl.program_id(0)

        @pl.when(i == 0)
        def _():
            dkv_ref[...] = jnp.zeros_like(dkv_ref)

        for h in range(4):
            sl = slice(128 * h, 128 * (h + 1))
            sv_ = slice(512 + 128 * h, 512 + 128 * (h + 1))
            q = q_ref[:, sl].astype(BF16)
            k = kv_ref[:, sl].astype(BF16)
            v = kv_ref[:, sv_].astype(BF16)
            do = do_ref[:, sl].astype(BF16)
            p = jax.nn.softmax(_dot(q, k, "nt") * MEM_SCALE, axis=-1)
            dp = _dot(do, v, "nt")
            ds = (p * (dp - jnp.sum(p * dp, axis=-1, keepdims=True)) * MEM_SCALE).astype(BF16)
            dq_ref[:, sl] = _dot(ds, k, "nn").astype(BF16)
            dkv_ref[:, sl] += _dot(ds, q, "tn")
            dkv_ref[:, sv_] += _dot(p.astype(BF16), do, "tn")

    return pl.pallas_call(
        body, name=name, grid=(S // T,),
        in_specs=[_row_spec(T, 512), pl.BlockSpec((NM, 1024), lambda i: (0, 0)), _row_spec(T, 512)],
        out_specs=[_row_spec(T, 512), pl.BlockSpec((NM, 1024), lambda i: (0, 0))],
        out_shape=[jax.ShapeDtypeStruct((S, 512), BF16), jax.ShapeDtypeStruct((NM, 1024), F32)],
        compiler_params=_params(("arbitrary",)))(qm, kv, dom)


def _adamw(w, g, m, v, name, TR=512):
    R, C = w.shape
    TR = R if R <= TR else _row_tile(R, TR)
    c1 = 1.0 - ADAM_B1 ** ADAM_STEP
    c2 = 1.0 - ADAM_B2 ** ADAM_STEP

    def body(w_ref, g_ref, m_ref, v_ref, d_ref, nm_ref, nv_ref):
        gv = g_ref[...]
        nm = ADAM_B1 * m_ref[...] + (1.0 - ADAM_B1) * gv
        nv = ADAM_B2 * v_ref[...] + (1.0 - ADAM_B2) * jnp.square(gv)
        d_ref[...] = -ADAM_LR * ((nm / c1) / (jnp.sqrt(nv / c2) + ADAM_EPS) + ADAM_WD * w_ref[...])
        nm_ref[...] = nm
        nv_ref[...] = nv

    spec = pl.BlockSpec((TR, C), lambda i: (i, 0))
    return pl.pallas_call(
        body, name=name, grid=(R // TR,), in_specs=[spec] * 4, out_specs=[spec] * 3,
        out_shape=[jax.ShapeDtypeStruct((R, C), F32)] * 3,
        compiler_params=_params(("parallel",)))(w, g, m, v)


def _row_tile(R, pref):
    t = (pref // 8) * 8
    while t >= 8:
        if R % t == 0:
            return t
        t -= 8
    raise ValueError(f"no row tile for {R}")


def _sum_slots(stack, name, TR=512):
    n, R, C = stack.shape
    TR = R if R <= TR else _row_tile(R, TR)

    def body(s_ref, o_ref):
        acc = s_ref[0]
        for k in range(1, n):
            acc = acc + s_ref[k]
        o_ref[...] = acc

    return pl.pallas_call(
        body, name=name, grid=(R // TR,),
        in_specs=[pl.BlockSpec((n, TR, C), lambda i: (0, i, 0))],
        out_specs=pl.BlockSpec((TR, C), lambda i: (i, 0)),
        out_shape=jax.ShapeDtypeStruct((R, C), F32),
        compiler_params=_params(("parallel",)))(stack)


_ANY = pl.BlockSpec(memory_space=pl.ANY)


def _chip_of(j, c):
    return (j // 2, j % 2, c)


def _gather_weights(shards, name):
    n = len(shards)

    def body(*refs):
        src, dst = refs[:n], refs[n:2 * n]
        send_sems, recv_sems, loc_sems = refs[2 * n:]
        j = 2 * lax.axis_index("x") + lax.axis_index("y")
        c = lax.axis_index("c")
        local, remote = [], []
        for a in range(n):
            loc = pltpu.make_async_copy(src[a], dst[a].at[:, j], loc_sems.at[a])
            loc.start()
            local.append(loc)
            for d in range(1, 4):
                cp = pltpu.make_async_remote_copy(
                    src_ref=src[a], dst_ref=dst[a].at[:, j], send_sem=send_sems.at[a, d - 1],
                    recv_sem=recv_sems.at[a, d - 1], device_id=_chip_of((j + d) % 4, c), device_id_type=MESH)
                cp.start()
                remote.append(cp)
        for a in range(n):
            for d in range(1, 4):
                frm = (j + 4 - d) % 4
                pltpu.make_async_remote_copy(
                    src_ref=src[a], dst_ref=dst[a].at[:, frm], send_sem=send_sems.at[a, d - 1],
                    recv_sem=recv_sems.at[a, d - 1], device_id=_chip_of(frm, c), device_id_type=MESH).wait_recv()
        for cp in remote:
            cp.wait_send()
        for cp in local:
            cp.wait()

    return pl.pallas_call(
        body, name=name,
        in_specs=[_ANY] * n, out_specs=[_ANY] * n,
        out_shape=[jax.ShapeDtypeStruct((s.shape[0], 4) + s.shape[1:], s.dtype) for s in shards],
        scratch_shapes=[pltpu.SemaphoreType.DMA((n, 3)), pltpu.SemaphoreType.DMA((n, 3)), pltpu.SemaphoreType.DMA((n,))],
    )(*shards)


def _pair_exchange(grads, name):
    n = len(grads)

    def body(*refs):
        src, dst = refs[:n], refs[n:2 * n]
        send_sems, recv_sems = refs[2 * n:]
        c = lax.axis_index("c")
        sib = (lax.axis_index("x"), lax.axis_index("y"), 1 - c)
        cps = []
        for a in range(n):
            cp = pltpu.make_async_remote_copy(
                src_ref=src[a].at[:, pl.ds(1 - c, 1)], dst_ref=dst[a], send_sem=send_sems.at[a],
                recv_sem=recv_sems.at[a], device_id=sib, device_id_type=MESH)
            cp.start()
            cps.append(cp)
        for cp in cps:
            cp.wait_recv()
        for cp in cps:
            cp.wait_send()

    return pl.pallas_call(
        body, name=name, in_specs=[_ANY] * n, out_specs=[_ANY] * n,
        out_shape=[jax.ShapeDtypeStruct((g.shape[0], 1) + g.shape[2:], g.dtype) for g in grads],
        scratch_shapes=[pltpu.SemaphoreType.DMA((n,)), pltpu.SemaphoreType.DMA((n,))],
    )(*grads)


def _pair_add(g4, r1, c_arr, name, TR=512):
    B4, _, rh, C = g4.shape
    B = B4 // 4
    TR = rh if rh <= TR else _row_tile(rh, TR)

    def body(c_ref, g_ref, r_ref, o_ref):
        o_ref[...] = g_ref[...] + r_ref[...]

    return pl.pallas_call(
        body, name=name,
        grid_spec=pltpu.PrefetchScalarGridSpec(
            num_scalar_prefetch=1, grid=(B, 4, rh // TR),
            in_specs=[pl.BlockSpec((None, None, TR, C), lambda b, p, t, cr: (b * 4 + p, cr[0], t, 0)),
                      pl.BlockSpec((None, None, TR, C), lambda b, p, t, cr: (b * 4 + p, 0, t, 0))],
            out_specs=pl.BlockSpec((None, None, TR, C), lambda b, p, t, cr: (p, b, t, 0))),
        out_shape=jax.ShapeDtypeStruct((4, B, rh, C), F32),
        compiler_params=_params(("parallel", "parallel", "parallel")),
    )(c_arr, g4, r1)


def _chip_exchange(parts, name):
    n = len(parts)

    def body(*refs):
        src, dst = refs[:n], refs[n:2 * n]
        send_sems, recv_sems = refs[2 * n:]
        j = 2 * lax.axis_index("x") + lax.axis_index("y")
        c = lax.axis_index("c")
        cps = []
        for a in range(n):
            for d in range(1, 4):
                to = (j + d) % 4
                cp = pltpu.make_async_remote_copy(
                    src_ref=src[a].at[to], dst_ref=dst[a].at[d - 1], send_sem=send_sems.at[a, d - 1],
                    recv_sem=recv_sems.at[a, d - 1], device_id=_chip_of(to, c), device_id_type=MESH)
                cp.start()
                cps.append(cp)
        for cp in cps:
            cp.wait_recv()
        for cp in cps:
            cp.wait_send()

    return pl.pallas_call(
        body, name=name, in_specs=[_ANY] * n, out_specs=[_ANY] * n,
        out_shape=[jax.ShapeDtypeStruct((3,) + p.shape[1:], p.dtype) for p in parts],
        scratch_shapes=[pltpu.SemaphoreType.DMA((n, 3)), pltpu.SemaphoreType.DMA((n, 3))],
    )(*parts)


def _chip_add(part, r2, j_arr, name, TR=512):
    _, B, rh, C = part.shape
    TR = rh if rh <= TR else _row_tile(rh, TR)

    def body(j_ref, p_ref, r_ref, o_ref):
        o_ref[...] = p_ref[...] + r_ref[0] + r_ref[1] + r_ref[2]

    return pl.pallas_call(
        body, name=name,
        grid_spec=pltpu.PrefetchScalarGridSpec(
            num_scalar_prefetch=1, grid=(B, rh // TR),
            in_specs=[pl.BlockSpec((None, None, TR, C), lambda b, t, jr: (jr[0], b, t, 0)),
                      pl.BlockSpec((3, None, TR, C), lambda b, t, jr: (0, b, t, 0))],
            out_specs=pl.BlockSpec((None, TR, C), lambda b, t, jr: (b, t, 0))),
        out_shape=jax.ShapeDtypeStruct((B, rh, C), F32),
        compiler_params=_params(("parallel", "parallel")),
    )(j_arr, part, r2)


def _pair_share(halves, name):
    n = len(halves)

    def body(*refs):
        src, dst = refs[:n], refs[n:2 * n]
        send_sems, recv_sems, loc_sems = refs[2 * n:]
        c = lax.axis_index("c")
        sib = (lax.axis_index("x"), lax.axis_index("y"), 1 - c)
        local, remote = [], []
        for a in range(n):
            loc = pltpu.make_async_copy(src[a], dst[a].at[:, c], loc_sems.at[a])
            loc.start()
            local.append(loc)
            cp = pltpu.make_async_remote_copy(
                src_ref=src[a], dst_ref=dst[a].at[:, c], send_sem=send_sems.at[a],
                recv_sem=recv_sems.at[a], device_id=sib, device_id_type=MESH)
            cp.start()
            remote.append(cp)
        for a in range(n):
            pltpu.make_async_remote_copy(
                src_ref=src[a], dst_ref=dst[a].at[:, 1 - c], send_sem=send_sems.at[a],
                recv_sem=recv_sems.at[a], device_id=sib, device_id_type=MESH).wait_recv()
        for cp in remote:
            cp.wait_send()
        for cp in local:
            cp.wait()

    return pl.pallas_call(
        body, name=name, in_specs=[_ANY] * n, out_specs=[_ANY] * n,
        out_shape=[jax.ShapeDtypeStruct((h.shape[0], 2) + h.shape[1:], h.dtype) for h in halves],
        scratch_shapes=[pltpu.SemaphoreType.DMA((n,)), pltpu.SemaphoreType.DMA((n,)), pltpu.SemaphoreType.DMA((n,))],
    )(*halves)


def _gather_all(flat, name):
    def body(src, dst, send_sems, recv_sems, loc_sem):
        me = 4 * lax.axis_index("x") + 2 * lax.axis_index("y") + lax.axis_index("c")
        loc = pltpu.make_async_copy(src, dst.at[me], loc_sem)
        loc.start()
        cps = []
        for d in range(1, 8):
            to = (me + d) % 8
            cp = pltpu.make_async_remote_copy(
                src_ref=src, dst_ref=dst.at[me], send_sem=send_sems.at[d - 1], recv_sem=recv_sems.at[d - 1],
                device_id=(to // 4, (to // 2) % 2, to % 2), device_id_type=MESH)
            cp.start()
            cps.append(cp)
        for d in range(1, 8):
            frm = (me + 8 - d) % 8
            pltpu.make_async_remote_copy(
                src_ref=src, dst_ref=dst.at[frm], send_sem=send_sems.at[d - 1], recv_sem=recv_sems.at[d - 1],
                device_id=(frm // 4, (frm // 2) % 2, frm % 2), device_id_type=MESH).wait_recv()
        for cp in cps:
            cp.wait_send()
        loc.wait()

    return pl.pallas_call(
        body, name=name, in_specs=[_ANY], out_specs=_ANY,
        out_shape=jax.ShapeDtypeStruct((8,) + flat.shape, flat.dtype),
        scratch_shapes=[pltpu.SemaphoreType.DMA((7,)), pltpu.SemaphoreType.DMA((7,)), pltpu.SemaphoreType.DMA],
    )(flat)


def _reduce_scatter(grads, c_arr, j_arr):
    g4 = [g.reshape(g.shape[0] * 4, 2, g.shape[1] // 8, g.shape[2]) for g in grads]
    r1 = _pair_exchange(g4, "rs_pair_exchange")
    parts = [_pair_add(g, r, c_arr, "rs_pair_add") for g, r in zip(g4, r1)]
    r2 = _chip_exchange(parts, "rs_chip_exchange")
    halves = [_chip_add(p, r, j_arr, "rs_chip_add") for p, r in zip(parts, r2)]
    full = _pair_share(halves, "rs_pair_share")
    return [f.reshape(f.shape[0], f.shape[1] * f.shape[2], f.shape[3]) for f in full]


def _relu2_epi(acc):
    return acc, jnp.square(jnp.maximum(acc, 0.0))


def _relu2_bwd_epi(acc, u):
    return (acc * (2.0 * jnp.maximum(u, 0.0)),)


def _forward_backward(x, mem, target, W, P):
    L = W["wout"].shape[0]
    S, D = x.shape
    gn = lambda l, i: P["g_norm"][l, i][None]

    saved = []
    (h,) = _resnorm_fwd(x, None, None, gn(0, 0), "norm_in")
    xr = x
    for l in range(L):
        proj = _mm(h, W["winT"], "nt", "in_proj", b_pre=(l,), tn=768)
        small = (jnp.broadcast_to(P["sinks"][l][:, None], (8, BLK)), P["ws"][l], P["bs"][l][:, :, None],
                 P["gsgu"][l][None], P["wp"][l], P["ps"][l][None])
        br = _mix_fwd(proj, *small, "mix_fwd")
        pb = lax.empty((3, S, D), F32)
        for n in range(3):
            pb = _mm(br, W["wbT"], "nt", "branch_proj", a_pre=(n,), b_pre=(3 * l + n,), into=pb, out_pre=(n,))
        merged = _merge_fwd(proj, pb, "merge_fwd")
        z = _mm(merged, W["wout"], "nn", "out_proj", b_pre=(l,))
        x1, hm = _resnorm_fwd(xr, z, gn(l, 1), gn(l, 2), "resnorm_fwd")
        qm = _mm(hm, W["wq"], "nn", "mem_q", b_pre=(l,))
        (memn,) = _resnorm_fwd(mem, None, None, P["g_mem"][l][None], "mem_norm")
        kv = _mm(memn, W["wkv"], "nn", "mem_kv", b_pre=(l,))
        om = _memattn_fwd(qm, kv, "memattn_fwd")
        ym = _mm(om, W["woT"], "nt", "mem_o", b_pre=(l,))
        x2, hf = _resnorm_fwd(x1, ym, gn(l, 3), gn(l, 4), "resnorm_fwd")
        u, a = _mm(hf, W["wupT"], "nt", "mlp_up", b_pre=(l,), out_dtypes=(F32, BF16), epi=_relu2_epi)
        yf = _mm(a, W["wdown"], "nn", "mlp_down", b_pre=(l,))
        saved.append(dict(x0=xr, h=h, proj=proj, small=small, br=br, pb=pb, merged=merged, z=z, x1=x1, hm=hm,
                          qm=qm, memn=memn, kv=kv, om=om, ym=ym, x2=x2, hf=hf, u=u, a=a, yf=yf))
        if l < L - 1:
            xr, h = _resnorm_fwd(x2, yf, gn(l, 5), gn(l + 1, 0), "resnorm_fwd")
    dres, loss = _final_fwd(saved[-1]["x2"], saved[-1]["yf"], gn(L - 1, 5), target, "loss_head")

    G = {k: lax.empty(v.shape, F32) for k, v in W.items()}
    dgn = [[None] * 6 for _ in range(L)]
    dsmall = {k: [None] * L for k in ("g_mem", "sinks", "ws", "bs", "gsgu", "wp", "ps")}
    dh = None
    for l in reversed(range(L)):
        s = saved[l]
        if l == L - 1:
            dx2, dyf, dgn[l][5] = _resnorm_bwd(s["x2"], s["yf"], gn(l, 5), None, dres, None, "resnorm_bwd_top")
        else:
            dx2, dyf, dgn[l][5], dgn[l + 1][0] = _resnorm_bwd(s["x2"], s["yf"], gn(l, 5), gn(l + 1, 0), dres, dh, "resnorm_bwd")
        du = _mm(dyf, W["wdown"], "nt", "mlp_down_dx", b_pre=(l,), out_dtypes=(BF16,), extras=(s["u"],), epi=_relu2_bwd_epi)
        G["wdown"] = _mm(s["a"], dyf, "tn", "mlp_down_dw", into=G["wdown"], out_pre=(l,))
        dhf = _mm(du, W["wupT"], "nn", "mlp_up_dx", b_pre=(l,))
        G["wupT"] = _mm(du, s["hf"], "tn", "mlp_up_dw", into=G["wupT"], out_pre=(l,))
        dx1, dym, dgn[l][3], dgn[l][4] = _resnorm_bwd(s["x1"], s["ym"], gn(l, 3), gn(l, 4), dx2, dhf, "resnorm_bwd")
        dom = _mm(dym, W["woT"], "nn", "mem_o_dx", b_pre=(l,))
        G["woT"] = _mm(dym, s["om"], "tn", "mem_o_dw", into=G["woT"], out_pre=(l,))
        dqm, dkv = _memattn_bwd(s["qm"], s["kv"], dom, "memattn_bwd")
        dmemn = _mm(dkv, W["wkv"], "nt", "mem_kv_dx", b_pre=(l,))
        G["wkv"] = _mm(s["memn"], dkv, "tn", "mem_kv_dw", into=G["wkv"], out_pre=(l,))
        _, dsmall["g_mem"][l] = _resnorm_bwd(mem, None, None, P["g_mem"][l][None], None, dmemn, "mem_norm_bwd")
        dhm = _mm(dqm, W["wq"], "nt", "mem_q_dx", b_pre=(l,))
        G["wq"] = _mm(s["hm"], dqm, "tn", "mem_q_dw", into=G["wq"], out_pre=(l,))
        dx0, dz, dgn[l][1], dgn[l][2] = _resnorm_bwd(s["x0"], s["z"], gn(l, 1), gn(l, 2), dx1, dhm, "resnorm_bwd")
        dmerged = _mm(dz, W["wout"], "nt", "out_proj_dx", b_pre=(l,))
        G["wout"] = _mm(s["merged"], dz, "tn", "out_proj_dw", into=G["wout"], out_pre=(l,))
        dproj, dpb = _merge_bwd(s["proj"], s["pb"], dmerged, "merge_bwd")
        dbr = lax.empty((3, S, 512), F32)
        for n in range(3):
            dbr = _mm(dpb, W["wbT"], "nn", "branch_proj_dx", a_pre=(n,), b_pre=(3 * l + n,), into=dbr, out_pre=(n,))
            G["wbT"] = _mm(dpb, s["br"], "tn", "branch_proj_dw", a_pre=(n,), b_pre=(n,), into=G["wbT"], out_pre=(3 * l + n,))
        (dproj, dsmall["sinks"][l], dsmall["ws"][l], dsmall["bs"][l], dsmall["gsgu"][l], dsmall["wp"][l],
         dsmall["ps"][l]) = _mix_bwd(s["proj"], dbr, dproj, *s["small"], "mix_bwd")
        dh = _mm(dproj, W["winT"], "nn", "in_proj_dx", b_pre=(l,), tk=768)
        G["winT"] = _mm(dproj, s["h"], "tn", "in_proj_dw", into=G["winT"], out_pre=(l,), tm=768)
        dres = dx0
    grad_x, dgn[0][0] = _resnorm_bwd(x, None, None, gn(0, 0), dres, dh, "norm_in_bwd")

    small_grads = dict(
        g_norm=jnp.stack([jnp.concatenate(row, axis=0) for row in dgn]),
        g_mem=jnp.concatenate(dsmall["g_mem"], axis=0),
        sinks=jnp.stack([d[:, 0] for d in dsmall["sinks"]]),
        ws=jnp.stack(dsmall["ws"]),
        bs=jnp.stack([d[:, :, 0] for d in dsmall["bs"]]),
        gsgu=jnp.concatenate(dsmall["gsgu"], axis=0),
        wp=jnp.stack(dsmall["wp"]),
        ps=jnp.concatenate(dsmall["ps"], axis=0),
    )
    return loss, grad_x, G, small_grads


_PACK_UNIT = 1024


def _pack(arrays):
    flat = []
    for a in arrays:
        f = a.reshape(-1)
        pad = (-f.shape[0]) % _PACK_UNIT
        flat.append(jnp.pad(f, (0, pad)) if pad else f)
    return jnp.concatenate(flat).reshape(-1, 128)


def _unpack(packed, like):
    flat, out, pos = packed.reshape(-1), [], 0
    for a in like:
        n = math.prod(a.shape)
        out.append(flat[pos:pos + n].reshape(a.shape))
        pos += n + (-n) % _PACK_UNIT
    return out


_BIG = ("w_in", "w_branch", "w_out", "w_q_mem", "w_kv_mem", "w_o_mem", "w_up", "w_down")
_SMALL = ("g_norm", "g_mem", "attn_sinks", "w_spatial", "b_spatial", "g_sgu", "w_pool", "pool_scale")
_WEIGHTS = ("g_norm", "g_mem", "w_in", "attn_sinks", "w_spatial", "b_spatial", "g_sgu", "w_pool", "pool_scale",
            "w_branch", "w_out", "w_q_mem", "w_kv_mem", "w_o_mem", "w_up", "w_down")


def _to_working(name, w):
    if name == "w_in":
        return jnp.swapaxes(w, 1, 2)
    if name == "w_branch":
        t = jnp.swapaxes(w, 2, 3)
        return t.reshape(t.shape[0] * 3, t.shape[2], t.shape[3])
    if name in ("w_o_mem", "w_up"):
        return jnp.swapaxes(w, 1, 2)
    return w


def _from_working(name, g):
    if name == "w_in":
        return jnp.swapaxes(g, 1, 2)
    if name == "w_branch":
        return jnp.swapaxes(g.reshape(g.shape[0] // 3, 3, g.shape[1], g.shape[2]), 2, 3)
    if name in ("w_o_mem", "w_up"):
        return jnp.swapaxes(g, 1, 2)
    return g


_WKEY = dict(w_in="winT", w_branch="wbT", w_out="wout", w_q_mem="wq", w_kv_mem="wkv", w_o_mem="woT",
             w_up="wupT", w_down="wdown")


def kernel(x, mem, g_norm, g_mem, w_in, attn_sinks, w_spatial, b_spatial, g_sgu, w_pool, pool_scale, w_branch, w_out, w_q_mem, w_kv_mem, w_o_mem, w_up, w_down, loss_target, m_g_norm, m_g_mem, m_w_in, m_attn_sinks, m_w_spatial, m_b_spatial, m_g_sgu, m_w_pool, m_pool_scale, m_w_branch, m_w_out, m_w_q_mem, m_w_kv_mem, m_w_o_mem, m_w_up, m_w_down, v_g_norm, v_g_mem, v_w_in, v_attn_sinks, v_w_spatial, v_b_spatial, v_g_sgu, v_w_pool, v_pool_scale, v_w_branch, v_w_out, v_w_q_mem, v_w_kv_mem, v_w_o_mem, v_w_up, v_w_down):
    w = dict(g_norm=g_norm, g_mem=g_mem, w_in=w_in, attn_sinks=attn_sinks, w_spatial=w_spatial, b_spatial=b_spatial,
             g_sgu=g_sgu, w_pool=w_pool, pool_scale=pool_scale, w_branch=w_branch, w_out=w_out, w_q_mem=w_q_mem,
             w_kv_mem=w_kv_mem, w_o_mem=w_o_mem, w_up=w_up, w_down=w_down)
    m = dict(g_norm=m_g_norm, g_mem=m_g_mem, w_in=m_w_in, attn_sinks=m_attn_sinks, w_spatial=m_w_spatial,
             b_spatial=m_b_spatial, g_sgu=m_g_sgu, w_pool=m_w_pool, pool_scale=m_pool_scale, w_branch=m_w_branch,
             w_out=m_w_out, w_q_mem=m_w_q_mem, w_kv_mem=m_w_kv_mem, w_o_mem=m_w_o_mem, w_up=m_w_up, w_down=m_w_down)
    v = dict(g_norm=v_g_norm, g_mem=v_g_mem, w_in=v_w_in, attn_sinks=v_attn_sinks, w_spatial=v_w_spatial,
             b_spatial=v_b_spatial, g_sgu=v_g_sgu, w_pool=v_w_pool, pool_scale=v_pool_scale, w_branch=v_w_branch,
             w_out=v_w_out, w_q_mem=v_w_q_mem, w_kv_mem=v_w_kv_mem, w_o_mem=v_w_o_mem, w_up=v_w_up, w_down=v_w_down)
    L = g_norm.shape[0]
    j = 2 * lax.axis_index("x") + lax.axis_index("y")
    c = lax.axis_index("c")
    j_arr = jnp.reshape(j, (1,)).astype(jnp.int32)
    c_arr = jnp.reshape(c, (1,)).astype(jnp.int32)

    shards = [_to_working(n, w[n]).astype(BF16) for n in _BIG] + [g_norm.reshape(1, L * 6, g_norm.shape[2])]
    gathered = _gather_weights(shards, "gather_weights")
    W = {_WKEY[n]: g.reshape(g.shape[0], 4 * g.shape[2], g.shape[3]) for n, g in zip(_BIG, gathered[:-1])}
    gn_full = jnp.transpose(gathered[-1][0], (1, 0, 2)).reshape(L, 6, 4 * g_norm.shape[2])
    P = dict(g_norm=gn_full, g_mem=g_mem, sinks=attn_sinks, ws=w_spatial, bs=b_spatial, gsgu=g_sgu, wp=w_pool,
             ps=pool_scale)

    loss_part, grad_x, G, sg = _forward_backward(x[0], mem[0], loss_target[0], W, P)
    loss = lax.psum(loss_part[0, 0], ("x", "y", "c"))

    summed = _reduce_scatter([G[_WKEY[n]] for n in _BIG], c_arr, j_arr)
    grads = {n: _from_working(n, g) for n, g in zip(_BIG, summed)}

    full_small = [sg["g_norm"], sg["g_mem"], sg["sinks"], sg["ws"], sg["bs"], sg["gsgu"], sg["wp"], sg["ps"]]
    packed = _pack(full_small)
    total = _sum_slots(_gather_all(packed, "gather_small_grads"), "sum_small_grads")
    for n, g in zip(_SMALL, _unpack(total, full_small)):
        grads[n] = lax.dynamic_slice_in_dim(g, j * g_norm.shape[2], g_norm.shape[2], axis=2) if n == "g_norm" else g

    delta, new_m, new_v = {}, {}, {}
    for n in _BIG:
        shp = w[n].shape
        two_d = lambda t: t.reshape(-1, shp[-1])
        d_, m_, v_ = _adamw(two_d(w[n]), two_d(grads[n]), two_d(m[n]), two_d(v[n]), "adamw")
        delta[n], new_m[n], new_v[n] = d_.reshape(shp), m_.reshape(shp), v_.reshape(shp)
    small_w = [w[n] for n in _SMALL]
    d_, m_, v_ = _adamw(_pack(small_w), _pack([grads[n] for n in _SMALL]), _pack([m[n] for n in _SMALL]),
                        _pack([v[n] for n in _SMALL]), "adamw_small")
    for n, dd, mm_, vv in zip(_SMALL, _unpack(d_, small_w), _unpack(m_, small_w), _unpack(v_, small_w)):
        delta[n], new_m[n], new_v[n] = dd, mm_, vv

    return (loss, grad_x[None], *[grads[n] for n in _WEIGHTS], *[delta[n] for n in _WEIGHTS],
            *[new_m[n] for n in _WEIGHTS], *[new_v[n] for n in _WEIGHTS])
```

```python
import functools
import math

import jax
import jax.numpy as jnp
from jax import lax
from jax.experimental import pallas as pl
from jax.experimental.pallas import tpu as pltpu

F32 = jnp.float32
BF16 = jnp.bfloat16
MESH = pl.DeviceIdType.MESH

EPS = 1e-6
NEG_INF = -1e30
BLK = 128
HALO = 16
POOL_WINDOWS = (2, 4, 8, 16)
ATT_SCALE = 1.0 / math.sqrt(64.0)
MEM_SCALE = 1.0 / math.sqrt(128.0)
C_Q, C_K, C_V, C_SU, C_SV, C_PC, C_GATE, C_END = 0, 512, 640, 768, 1280, 1792, 2304, 5376

ADAM_LR, ADAM_B1, ADAM_B2, ADAM_EPS, ADAM_WD, ADAM_STEP = 0.001, 0.9, 0.999, 1e-08, 0.01, 10

VMEM_LIMIT_BYTES = 56 * 1024 * 1024

_DIMS = {
    "nn": (((1,), (0,)), ((), ())),
    "nt": (((1,), (1,)), ((), ())),
    "tn": (((0,), (0,)), ((), ())),
}


def _dot(a, b, mode):
    return lax.dot_general(a, b, _DIMS[mode], preferred_element_type=F32)


def _params(semantics):
    return pltpu.CompilerParams(dimension_semantics=semantics, vmem_limit_bytes=VMEM_LIMIT_BYTES)


def _tile(dim, pref):
    if dim <= pref:
        return dim
    t = (pref // 128) * 128
    while t >= 128:
        if dim % t == 0:
            return t
        t -= 128
    raise ValueError(f"no tile for {dim}")


def _rms(x, g):
    return x * lax.rsqrt(jnp.mean(x * x, axis=-1, keepdims=True) + EPS) * g


def _mm(a, b, mode, name, *, out_dtypes=(F32,), a_pre=(), b_pre=(), into=None, out_pre=(),
        extras=(), epi=None, tm=1024, tn=1024, tk=1024):
    a2, b2 = a.shape[len(a_pre):], b.shape[len(b_pre):]
    if mode == "nn":
        (M, K), (K2, N) = a2, b2
    elif mode == "nt":
        (M, K), (N, K2) = a2, b2
    else:
        (K, M), (K2, N) = a2, b2
    assert K == K2, (a.shape, b.shape, mode)
    tm, tn, tk = _tile(M, tm), _tile(N, tn), _tile(K, tk)
    nk = K // tk
    na, nb_, no = len(a_pre), len(b_pre), len(out_pre)
    if mode == "tn":
        a_spec = pl.BlockSpec((None,) * na + (tk, tm), lambda i, j, k: a_pre + (k, i))
    else:
        a_spec = pl.BlockSpec((None,) * na + (tm, tk), lambda i, j, k: a_pre + (i, k))
    if mode == "nt":
        b_spec = pl.BlockSpec((None,) * nb_ + (tn, tk), lambda i, j, k: b_pre + (j, k))
    else:
        b_spec = pl.BlockSpec((None,) * nb_ + (tk, tn), lambda i, j, k: b_pre + (k, j))
    tile_spec = pl.BlockSpec((tm, tn), lambda i, j, k: (i, j))
    ne, nout = len(extras), len(out_dtypes)
    in_specs = [a_spec, b_spec] + [tile_spec] * ne
    operands = [a, b, *extras]
    aliases = {}
    if into is not None:
        assert nout == 1
        in_specs.append(pl.BlockSpec(memory_space=pl.ANY))
        operands.append(into)
        aliases = {len(operands) - 1: 0}
        out_shape = [jax.ShapeDtypeStruct(into.shape, into.dtype)]
        out_specs = [pl.BlockSpec((None,) * no + (tm, tn), lambda i, j, k: out_pre + (i, j))]
    else:
        out_shape = [jax.ShapeDtypeStruct((M, N), dt) for dt in out_dtypes]
        out_specs = [tile_spec] * nout

    def body(*refs):
        a_ref, b_ref = refs[0], refs[1]
        ex = refs[2:2 + ne]
        pos = 2 + ne + (1 if into is not None else 0)
        outs = refs[pos:pos + nout]
        acc_ref = refs[pos + nout] if nk > 1 else None

        def finish(acc):
            vals = epi(acc, *[e[...] for e in ex]) if epi is not None else (acc,)
            for o, v in zip(outs, vals):
                o[...] = v.astype(o.dtype)

        def prod():
            return _dot(a_ref[...].astype(BF16), b_ref[...].astype(BF16), mode)

        if nk == 1:
            finish(prod())
        else:
            k = pl.program_id(2)

            @pl.when(k == 0)
            def _():
                acc_ref[...] = jnp.zeros_like(acc_ref)

            acc_ref[...] += prod()

            @pl.when(k == nk - 1)
            def _():
                finish(acc_ref[...])

    res = pl.pallas_call(
        body, name=name, grid=(M // tm, N // tn, nk),
        in_specs=in_specs, out_specs=out_specs, out_shape=out_shape,
        scratch_shapes=[pltpu.VMEM((tm, tn), F32)] if nk > 1 else [],
        input_output_aliases=aliases,
        compiler_params=_params(("parallel", "parallel", "arbitrary")),
    )(*operands)
    return res[0] if nout == 1 else tuple(res)


def _resnorm_fn(has_post, has_pre):
    def f(*a):
        x, k = a[0], 1
        if has_post:
            x, k = x + _rms(a[1], a[2]), 3
        outs = [x]
        if has_pre:
            outs.append(_rms(x, a[k]))
        return tuple(outs)
    return f


def _row_spec(T, W):
    return pl.BlockSpec((T, W), lambda i: (i, 0))


def _par_spec(W):
    return pl.BlockSpec((1, W), lambda i: (0, 0))


def _resnorm_fwd(xr, y, gp, gq, name, T=512):
    S, D = xr.shape
    T = min(T, S)
    has_post, has_pre = y is not None, gq is not None
    f = _resnorm_fn(has_post, has_pre)
    ins = [xr] + ([y, gp] if has_post else []) + ([gq] if has_pre else [])
    in_specs = [_row_spec(T, D)] + ([_row_spec(T, D), _par_spec(D)] if has_post else []) + ([_par_spec(D)] if has_pre else [])
    out_shape, out_specs = [], []
    if has_post:
        out_shape.append(jax.ShapeDtypeStruct((S, D), F32)); out_specs.append(_row_spec(T, D))
    if has_pre:
        out_shape.append(jax.ShapeDtypeStruct((S, D), BF16)); out_specs.append(_row_spec(T, D))
    n_in = len(ins)

    def body(*refs):
        vals = f(*[r[...] for r in refs[:n_in]])
        outs = list(refs[n_in:])
        if has_post:
            outs.pop(0)[...] = vals[0]
        if has_pre:
            outs.pop(0)[...] = vals[1].astype(BF16)

    res = pl.pallas_call(body, name=name, grid=(S // T,), in_specs=in_specs, out_specs=out_specs,
                         out_shape=out_shape, compiler_params=_params(("parallel",)))(*ins)
    return tuple(res)


def _resnorm_bwd(xr, y, gp, gq, dres, dh, name, T=512):
    S, D = xr.shape
    T = min(T, S)
    has_post, has_pre, has_res = y is not None, gq is not None, dres is not None
    f = _resnorm_fn(has_post, has_pre)
    ins = [xr] + ([y, gp] if has_post else []) + ([gq] if has_pre else [])
    in_specs = [_row_spec(T, D)] + ([_row_spec(T, D), _par_spec(D)] if has_post else []) + ([_par_spec(D)] if has_pre else [])
    n_prim = len(ins)
    if has_res:
        ins.append(dres); in_specs.append(_row_spec(T, D))
    if has_pre:
        ins.append(dh); in_specs.append(_row_spec(T, D))
    n_in = len(ins)
    out_shape = [jax.ShapeDtypeStruct((S, D), F32)]
    out_specs = [_row_spec(T, D)]
    if has_post:
        out_shape += [jax.ShapeDtypeStruct((S, D), BF16), jax.ShapeDtypeStruct((1, D), F32)]
        out_specs += [_row_spec(T, D), _par_spec(D)]
    if has_pre:
        out_shape.append(jax.ShapeDtypeStruct((1, D), F32)); out_specs.append(_par_spec(D))

    def body(*refs):
        i = pl.program_id(0)
        prim = [r[...] for r in refs[:n_prim]]
        rest = list(refs[n_prim:n_in])
        ct_x = rest.pop(0)[...] if has_res else jnp.zeros((T, D), F32)
        cts = [ct_x]
        if has_pre:
            cts.append(rest.pop(0)[...].astype(F32))
        _, vjp = jax.vjp(f, *prim)
        grads = list(vjp(tuple(cts)))
        outs = list(refs[n_in:])
        outs.pop(0)[...] = grads.pop(0)
        acc = []
        if has_post:
            outs.pop(0)[...] = grads.pop(0).astype(BF16)
            acc.append((outs.pop(0), grads.pop(0)))
        if has_pre:
            acc.append((outs.pop(0), grads.pop(0)))

        @pl.when(i == 0)
        def _():
            for o, _g in acc:
                o[...] = jnp.zeros_like(o)

        for o, g in acc:
            o[...] += g

    res = pl.pallas_call(body, name=name, grid=(S // T,), in_specs=in_specs, out_specs=out_specs,
                         out_shape=out_shape, compiler_params=_params(("arbitrary",)))(*ins)
    return tuple(res)


def _final_fwd(xr, y, gp, target, name, T=512):
    S, D = xr.shape
    T = min(T, S)

    def body(x_ref, y_ref, g_ref, t_ref, dy_ref, loss_ref):
        i = pl.program_id(0)
        e = x_ref[...] + _rms(y_ref[...], g_ref[...]) - t_ref[...]
        dy_ref[...] = e / D

        @pl.when(i == 0)
        def _():
            loss_ref[...] = jnp.zeros_like(loss_ref)

        loss_ref[...] += 0.5 * jnp.sum(jnp.sum(e * e, axis=-1, keepdims=True) / D, axis=0, keepdims=True)

    return pl.pallas_call(
        body, name=name, grid=(S // T,),
        in_specs=[_row_spec(T, D), _row_spec(T, D), _par_spec(D), _row_spec(T, D)],
        out_specs=[_row_spec(T, D), pl.BlockSpec((1, 128), lambda i: (0, 0))],
        out_shape=[jax.ShapeDtypeStruct((S, D), F32), jax.ShapeDtypeStruct((1, 128), F32)],
        compiler_params=_params(("arbitrary",)))(xr, y, gp, target)


def _lane_lo():
    return lax.broadcasted_iota(jnp.int32, (1, BLK), 1) < 64


def _att_mask(not_first):
    r = lax.broadcasted_iota(jnp.int32, (BLK, 2 * BLK), 0)
    c = lax.broadcasted_iota(jnp.int32, (BLK, 2 * BLK), 1)
    qc, kc = 2 + r // 64, c // 64
    return (kc <= qc) & (kc >= qc - 2) & (not_first | (c >= BLK))


def _softmax_sink(s, sk):
    m = jnp.maximum(jnp.max(s, axis=-1, keepdims=True), sk)
    e = jnp.exp(s - m)
    es = jnp.exp(sk - m)
    z = jnp.sum(e, axis=-1, keepdims=True) + es
    return e / z, es / z


def _att_operands(cur, kvp, t, lo):
    h = t // 2
    qt = cur[:, C_Q + BLK * t:C_Q + BLK * (t + 1)]
    q_lo = jnp.where(lo, qt, 0.0).astype(BF16)
    q_hi = jnp.where(lo, 0.0, qt).astype(BF16)
    kband = jnp.concatenate([kvp[:, 0:BLK], cur[:, C_K:C_K + BLK]], axis=0)
    vband = jnp.concatenate([kvp[:, BLK:2 * BLK], cur[:, C_V:C_V + BLK]], axis=0)
    kroll = pltpu.roll(kband, 64, 1)
    vroll = pltpu.roll(vband, 64, 1)
    ka, kb = (kband, kroll) if h == 0 else (kroll, kband)
    va = jnp.where(lo, vband if h == 0 else vroll, 0.0)
    vb = jnp.where(lo, 0.0, vroll if h == 0 else vband)
    return q_lo, q_hi, ka.astype(BF16), kb.astype(BF16), va.astype(BF16), vb.astype(BF16)


def _sgu_mask():
    r = lax.broadcasted_iota(jnp.int32, (BLK, BLK), 0)
    c = lax.broadcasted_iota(jnp.int32, (BLK, BLK), 1)
    return (c // 64) <= (r // 64)


def _pool_cnt(blk, w):
    t = blk * BLK + lax.broadcasted_iota(jnp.int32, (BLK, 1), 0)
    return jnp.minimum(t + 1, w).astype(F32)


def _mix_in_specs(nb, rev):
    def b(i):
        return nb - 1 - i if rev else i
    return [
        pl.BlockSpec((BLK, C_GATE), lambda i: (b(i), 0)),
        pl.BlockSpec((BLK, 2 * BLK), lambda i: (jnp.maximum(b(i) - 1, 0), C_K // (2 * BLK))),
        pl.BlockSpec((HALO, C_GATE), lambda i: (jnp.maximum(b(i) * (BLK // HALO) - 1, 0), 0)),
        pl.BlockSpec((8, BLK), lambda i: (0, 0)),
        pl.BlockSpec((4, BLK, BLK), lambda i: (0, 0, 0)),
        pl.BlockSpec((4, BLK, 1), lambda i: (0, 0, 0)),
        pl.BlockSpec((1, 512), lambda i: (0, 0)),
        pl.BlockSpec((4, BLK, BLK), lambda i: (0, 0, 0)),
        pl.BlockSpec((1, 512), lambda i: (0, 0)),
    ]


def _mix_fwd(proj, sinks_b, ws, bs3, gsgu, wp, ps, name):
    S = proj.shape[0]
    nb = S // BLK

    def body(cur_ref, kvp_ref, pcp_ref, sk_ref, ws_ref, bs_ref, gs_ref, wp_ref, ps_ref, br_ref, ext_ref):
        i = pl.program_id(0)
        not_first = i > 0
        lo = _lane_lo()
        cur, kvp = cur_ref[...], kvp_ref[...]
        mask = _att_mask(not_first)
        for t in range(4):
            q_lo, q_hi, ka, kb, va, vb = _att_operands(cur, kvp, t, lo)
            s_lo = jnp.where(mask, _dot(q_lo, ka, "nt") * ATT_SCALE, NEG_INF)
            s_hi = jnp.where(mask, _dot(q_hi, kb, "nt") * ATT_SCALE, NEG_INF)
            p_lo, _ = _softmax_sink(s_lo, sk_ref[2 * t:2 * t + 1, 0:1])
            p_hi, _ = _softmax_sink(s_hi, sk_ref[2 * t + 1:2 * t + 2, 0:1])
            o = _dot(p_lo.astype(BF16), va, "nn") + _dot(p_hi.astype(BF16), vb, "nn")
            br_ref[0, :, BLK * t:BLK * (t + 1)] = o.astype(BF16)
        gu = jax.nn.gelu(cur[:, C_SU:C_SV])
        vn = _rms(jax.nn.gelu(cur[:, C_SV:C_PC]), gs_ref[...]).astype(BF16)
        wmask = _sgu_mask()
        for g in range(4):
            wm = jnp.where(wmask, ws_ref[g], 0.0).astype(BF16)
            sp = _dot(wm, vn[:, BLK * g:BLK * (g + 1)], "nn") + bs_ref[g]
            br_ref[1, :, BLK * g:BLK * (g + 1)] = (gu[:, BLK * g:BLK * (g + 1)] * sp).astype(BF16)
        c = cur[:, C_PC:C_GATE]
        ext_ref[0:HALO, :] = jnp.where(not_first, pcp_ref[:, C_PC:C_GATE], 0.0)
        ext_ref[HALO:HALO + BLK, :] = c
        for g, w in enumerate(POOL_WINDOWS):
            sl = slice(BLK * g, BLK * (g + 1))
            acc = ext_ref[HALO:HALO + BLK, sl]
            for k in range(1, w):
                acc = acc + ext_ref[HALO - k:HALO - k + BLK, sl]
            pooled = acc / _pool_cnt(i, w) - c[:, sl]
            mixed = _dot(pooled.astype(BF16), wp_ref[g].astype(BF16), "nn")
            br_ref[2, :, sl] = (mixed * ps_ref[:, sl]).astype(BF16)

    return pl.pallas_call(
        body, name=name, grid=(nb,),
        in_specs=_mix_in_specs(nb, False),
        out_specs=pl.BlockSpec((3, BLK, 512), lambda i: (0, i, 0)),
        out_shape=jax.ShapeDtypeStruct((3, S, 512), BF16),
        scratch_shapes=[pltpu.VMEM((HALO + BLK, 512), F32)],
        compiler_params=_params(("parallel",)),
    )(proj, proj, proj, sinks_b, ws, bs3, gsgu, wp, ps)


def _mix_bwd(proj, dbr, dproj, sinks_b, ws, bs3, gsgu, wp, ps, name):
    S = proj.shape[0]
    nb = S // BLK

    def body(cur_ref, kvp_ref, pcp_ref, sk_ref, ws_ref, bs_ref, gs_ref, wp_ref, ps_ref, dbr_ref, _dproj_in,
             dp_ref, dsk_ref, dws_ref, dbs_ref, dgs_ref, dwp_ref, dps_ref,
             ext_ref, z_ref, ckv_ref, cpc_ref):
        i = pl.program_id(0)
        blk = nb - 1 - i
        not_first = blk > 0
        lo = _lane_lo()

        @pl.when(i == 0)
        def _():
            for r in (dsk_ref, dws_ref, dbs_ref, dgs_ref, dwp_ref, dps_ref, ckv_ref, cpc_ref, z_ref):
                r[...] = jnp.zeros_like(r)

        cur, kvp = cur_ref[...], kvp_ref[...]
        mask = _att_mask(not_first)
        dk_band = jnp.zeros((2 * BLK, BLK), F32)
        dk_roll = jnp.zeros((2 * BLK, BLK), F32)
        dv_band = jnp.zeros((2 * BLK, BLK), F32)
        dv_roll = jnp.zeros((2 * BLK, BLK), F32)
        for t in range(4):
            h = t // 2
            q_lo, q_hi, ka, kb, va, vb = _att_operands(cur, kvp, t, lo)
            do = dbr_ref[0, :, BLK * t:BLK * (t + 1)].astype(BF16)
            dq = jnp.zeros((BLK, BLK), F32)
            for half, (qm, km, vm) in enumerate(((q_lo, ka, va), (q_hi, kb, vb))):
                sk = sk_ref[2 * t + half:2 * t + half + 1, 0:1]
                s = jnp.where(mask, _dot(qm, km, "nt") * ATT_SCALE, NEG_INF)
                p, p_sink = _softmax_sink(s, sk)
                dp = _dot(do, vm, "nt")
                rs = jnp.sum(p * dp, axis=-1, keepdims=True)
                ds = (p * (dp - rs) * ATT_SCALE).astype(BF16)
                dsk_ref[2 * t + half:2 * t + half + 1, :] += jnp.broadcast_to(
                    -jnp.sum(p_sink * rs, axis=0, keepdims=True), (1, BLK))
                dvm = _dot(p.astype(BF16), do, "tn")
                dkm = _dot(ds, qm, "tn")
                dqm = _dot(ds, km, "nn")
                if half == 0:
                    dq = dq + jnp.where(lo, dqm, 0.0)
                    dvm = jnp.where(lo, dvm, 0.0)
                else:
                    dq = dq + jnp.where(lo, 0.0, dqm)
                    dvm = jnp.where(lo, 0.0, dvm)
                if (h == 0) == (half == 0):
                    dk_band, dv_band = dk_band + dkm, dv_band + dvm
                else:
                    dk_roll, dv_roll = dk_roll + dkm, dv_roll + dvm
            dp_ref[:, C_Q + BLK * t:C_Q + BLK * (t + 1)] = dq.astype(BF16)
        dk = dk_band + pltpu.roll(dk_roll, 64, 1)
        dv = dv_band + pltpu.roll(dv_roll, 64, 1)
        dp_ref[:, C_K:C_K + BLK] = (dk[BLK:] + ckv_ref[:, 0:BLK]).astype(BF16)
        dp_ref[:, C_V:C_V + BLK] = (dv[BLK:] + ckv_ref[:, BLK:]).astype(BF16)
        ckv_ref[:, 0:BLK] = dk[:BLK]
        ckv_ref[:, BLK:] = dv[:BLK]
        su, sv = cur[:, C_SU:C_SV], cur[:, C_SV:C_PC]
        gu, vjp_u = jax.vjp(jax.nn.gelu, su)
        vn, vjp_v = jax.vjp(lambda a, g: _rms(jax.nn.gelu(a), g), sv, gs_ref[...])
        vn16 = vn.astype(BF16)
        wmask = _sgu_mask()
        dgu, dvn = [], []
        for g in range(4):
            sl = slice(BLK * g, BLK * (g + 1))
            wm = jnp.where(wmask, ws_ref[g], 0.0).astype(BF16)
            sp = _dot(wm, vn16[:, sl], "nn") + bs_ref[g]
            dyb = dbr_ref[1, :, sl]
            dgu.append(dyb * sp)
            dsp = dyb * gu[:, sl]
            dsp16 = dsp.astype(BF16)
            dvn.append(_dot(wm, dsp16, "tn"))
            dws_ref[g] += jnp.where(wmask, _dot(dsp16, vn16[:, sl], "nt"), 0.0)
            dbs_ref[g] += jnp.sum(dsp, axis=1, keepdims=True)
        (dsu,) = vjp_u(jnp.concatenate(dgu, axis=1))
        dsv, dgs = vjp_v(jnp.concatenate(dvn, axis=1))
        dp_ref[:, C_SU:C_SV] = dsu.astype(BF16)
        dp_ref[:, C_SV:C_PC] = dsv.astype(BF16)
        dgs_ref[...] += dgs
        c = cur[:, C_PC:C_GATE]
        ext_ref[0:HALO, :] = jnp.where(not_first, pcp_ref[:, C_PC:C_GATE], 0.0)
        ext_ref[HALO:HALO + BLK, :] = c
        for g, w in enumerate(POOL_WINDOWS):
            sl = slice(BLK * g, BLK * (g + 1))
            acc = ext_ref[HALO:HALO + BLK, sl]
            for k in range(1, w):
                acc = acc + ext_ref[HALO - k:HALO - k + BLK, sl]
            cnt = _pool_cnt(blk, w)
            pooled16 = (acc / cnt - c[:, sl]).astype(BF16)
            wp16 = wp_ref[g].astype(BF16)
            mixed = _dot(pooled16, wp16, "nn")
            dyc = dbr_ref[2, :, sl]
            dps_ref[:, sl] += jnp.sum(dyc * mixed, axis=0, keepdims=True)
            dmixed16 = (dyc * ps_ref[:, sl]).astype(BF16)
            dwp_ref[g] += _dot(pooled16, dmixed16, "tn")
            dpooled = _dot(dmixed16, wp16, "nt")
            z_ref[HALO:HALO + BLK, sl] = dpooled / cnt
            dext = z_ref[0:HALO + BLK, sl]
            for k in range(1, w):
                dext = dext + z_ref[k:k + HALO + BLK, sl]
            dp_ref[:, C_PC + BLK * g:C_PC + BLK * (g + 1)] = (
                dext[HALO:] - dpooled + jnp.concatenate([jnp.zeros((BLK - HALO, BLK), F32), cpc_ref[:, sl]], axis=0)
            ).astype(BF16)
            cpc_ref[:, sl] = dext[:HALO]

    n_in = 11
    small = [jax.ShapeDtypeStruct((8, BLK), F32), jax.ShapeDtypeStruct((4, BLK, BLK), F32),
             jax.ShapeDtypeStruct((4, BLK, 1), F32), jax.ShapeDtypeStruct((1, 512), F32),
             jax.ShapeDtypeStruct((4, BLK, BLK), F32), jax.ShapeDtypeStruct((1, 512), F32)]
    small_specs = [pl.BlockSpec((8, BLK), lambda i: (0, 0)), pl.BlockSpec((4, BLK, BLK), lambda i: (0, 0, 0)),
                   pl.BlockSpec((4, BLK, 1), lambda i: (0, 0, 0)), pl.BlockSpec((1, 512), lambda i: (0, 0)),
                   pl.BlockSpec((4, BLK, BLK), lambda i: (0, 0, 0)), pl.BlockSpec((1, 512), lambda i: (0, 0))]
    res = pl.pallas_call(
        body, name=name, grid=(nb,),
        in_specs=_mix_in_specs(nb, True) + [
            pl.BlockSpec((3, BLK, 512), lambda i: (0, nb - 1 - i, 0)),
            pl.BlockSpec(memory_space=pl.ANY)],
        out_specs=[pl.BlockSpec((BLK, C_GATE), lambda i: (nb - 1 - i, 0))] + small_specs,
        out_shape=[jax.ShapeDtypeStruct(dproj.shape, dproj.dtype)] + small,
        scratch_shapes=[pltpu.VMEM((HALO + BLK, 512), F32), pltpu.VMEM((2 * HALO + BLK, 512), F32),
                        pltpu.VMEM((BLK, 2 * BLK), F32), pltpu.VMEM((HALO, 512), F32)],
        input_output_aliases={n_in - 1: 0},
        compiler_params=_params(("arbitrary",)),
    )(proj, proj, proj, sinks_b, ws, bs3, gsgu, wp, ps, dbr, dproj)
    return tuple(res)


_GW = 256


def _merge_fwd(proj, pb, name, T=1024):
    S, D = pb.shape[1], pb.shape[2]
    T = min(T, S)

    def body(gate_ref, pb_ref, out_ref, acc_ref):
        n = pl.program_id(2)

        @pl.when(n == 0)
        def _():
            acc_ref[...] = jnp.zeros_like(acc_ref)

        acc_ref[...] += jax.nn.sigmoid(gate_ref[...]) * pb_ref[...]

        @pl.when(n == 2)
        def _():
            out_ref[...] = acc_ref[...].astype(BF16)

    return pl.pallas_call(
        body, name=name, grid=(S // T, D // _GW, 3),
        in_specs=[pl.BlockSpec((T, _GW), lambda i, j, n: (i, C_GATE // _GW + n * (D // _GW) + j)),
                  pl.BlockSpec((None, T, _GW), lambda i, j, n: (n, i, j))],
        out_specs=pl.BlockSpec((T, _GW), lambda i, j, n: (i, j)),
        out_shape=jax.ShapeDtypeStruct((S, D), BF16),
        scratch_shapes=[pltpu.VMEM((T, _GW), F32)],
        compiler_params=_params(("parallel", "parallel", "arbitrary")),
    )(proj, pb)


def _merge_bwd(proj, pb, dmerged, name, T=1024):
    S, D = pb.shape[1], pb.shape[2]
    T = min(T, S)

    def body(gate_ref, pb_ref, dm_ref, dgate_ref, dpb_ref):
        sg = jax.nn.sigmoid(gate_ref[...])
        dm = dm_ref[...]
        dpb_ref[...] = (dm * sg).astype(BF16)
        dgate_ref[...] = (dm * pb_ref[...] * sg * (1.0 - sg)).astype(BF16)

    gate_map = lambda i, n, j: (i, C_GATE // _GW + n * (D // _GW) + j)
    return pl.pallas_call(
        body, name=name, grid=(S // T, 3, D // _GW),
        in_specs=[pl.BlockSpec((T, _GW), gate_map),
                  pl.BlockSpec((None, T, _GW), lambda i, n, j: (n, i, j)),
                  pl.BlockSpec((T, _GW), lambda i, n, j: (i, j))],
        out_specs=[pl.BlockSpec((T, _GW), gate_map),
                   pl.BlockSpec((None, T, _GW), lambda i, n, j: (n, i, j))],
        out_shape=[jax.ShapeDtypeStruct((S, C_END), BF16), jax.ShapeDtypeStruct((3, S, D), BF16)],
        compiler_params=_params(("parallel", "parallel", "parallel")),
    )(proj, pb, dmerged)


def _memattn_fwd(qm, kv, name, T=512):
    S, NM = qm.shape[0], kv.shape[0]
    T = min(T, S)

    def body(q_ref, kv_ref, o_ref):
        for h in range(4):
            sl = slice(128 * h, 128 * (h + 1))
            k = kv_ref[:, sl].astype(BF16)
            v = kv_ref[:, 512 + 128 * h:512 + 128 * (h + 1)].astype(BF16)
            s = _dot(q_ref[:, sl].astype(BF16), k, "nt") * MEM_SCALE
            p = jax.nn.softmax(s, axis=-1)
            o_ref[:, sl] = _dot(p.astype(BF16), v, "nn").astype(BF16)

    return pl.pallas_call(
        body, name=name, grid=(S // T,),
        in_specs=[_row_spec(T, 512), pl.BlockSpec((NM, 1024), lambda i: (0, 0))],
        out_specs=_row_spec(T, 512), out_shape=jax.ShapeDtypeStruct((S, 512), BF16),
        compiler_params=_params(("parallel",)))(qm, kv)


def _memattn_bwd(qm, kv, dom, name, T=512):
    S, NM = qm.shape[0], kv.shape[0]
    T = min(T, S)

    def body(q_ref, kv_ref, do_ref, dq_ref, dkv_ref):
        i = pl.program_id(0)

        @pl.when(i == 0)
        def _():
            dkv_ref[...] = jnp.zeros_like(dkv_ref)

        for h in range(4):
            sl = slice(128 * h, 128 * (h + 1))
            sv_ = slice(512 + 128 * h, 512 + 128 * (h + 1))
            q = q_ref[:, sl].astype(BF16)
            k = kv_ref[:, sl].astype(BF16)
            v = kv_ref[:, sv_].astype(BF16)
            do = do_ref[:, sl].astype(BF16)
            p = jax.nn.softmax(_dot(q, k, "nt") * MEM_SCALE, axis=-1)
            dp = _dot(do, v, "nt")
            ds = (p * (dp - jnp.sum(p * dp, axis=-1, keepdims=True)) * MEM_SCALE).astype(BF16)
            dq_ref[:, sl] = _dot(ds, k, "nn").astype(BF16)
            dkv_ref[:, sl] += _dot(ds, q, "tn")
            dkv_ref[:, sv_] += _dot(p.astype(BF16), do, "tn")

    return pl.pallas_call(
        body, name=name, grid=(S // T,),
        in_specs=[_row_spec(T, 512), pl.BlockSpec((NM, 1024), lambda i: (0, 0)), _row_spec(T, 512)],
        out_specs=[_row_spec(T, 512), pl.BlockSpec((NM, 1024), lambda i: (0, 0))],
        out_shape=[jax.ShapeDtypeStruct((S, 512), BF16), jax.ShapeDtypeStruct((NM, 1024), F32)],
        compiler_params=_params(("arbitrary",)))(qm, kv, dom)


def _adamw(w, g, m, v, name, TR=512):
    R, C = w.shape
    TR = R if R <= TR else _row_tile(R, TR)
    c1 = 1.0 - ADAM_B1 ** ADAM_STEP
    c2 = 1.0 - ADAM_B2 ** ADAM_STEP

    def body(w_ref, g_ref, m_ref, v_ref, d_ref, nm_ref, nv_ref):
        gv = g_ref[...]
        nm = ADAM_B1 * m_ref[...] + (1.0 - ADAM_B1) * gv
        nv = ADAM_B2 * v_ref[...] + (1.0 - ADAM_B2) * jnp.square(gv)
        d_ref[...] = -ADAM_LR * ((nm / c1) / (jnp.sqrt(nv / c2) + ADAM_EPS) + ADAM_WD * w_ref[...])
        nm_ref[...] = nm
        nv_ref[...] = nv

    spec = pl.BlockSpec((TR, C), lambda i: (i, 0))
    return pl.pallas_call(
        body, name=name, grid=(R // TR,), in_specs=[spec] * 4, out_specs=[spec] * 3,
        out_shape=[jax.ShapeDtypeStruct((R, C), F32)] * 3,
        compiler_params=_params(("parallel",)))(w, g, m, v)


def _row_tile(R, pref):
    t = (pref // 8) * 8
    while t >= 8:
        if R % t == 0:
            return t
        t -= 8
    raise ValueError(f"no row tile for {R}")


def _sum_slots(stack, name, TR=512):
    n, R, C = stack.shape
    TR = R if R <= TR else _row_tile(R, TR)

    def body(s_ref, o_ref):
        acc = s_ref[0]
        for k in range(1, n):
            acc = acc + s_ref[k]
        o_ref[...] = acc

    return pl.pallas_call(
        body, name=name, grid=(R // TR,),
        in_specs=[pl.BlockSpec((n, TR, C), lambda i: (0, i, 0))],
        out_specs=pl.BlockSpec((TR, C), lambda i: (i, 0)),
        out_shape=jax.ShapeDtypeStruct((R, C), F32),
        compiler_params=_params(("parallel",)))(stack)


_ANY = pl.BlockSpec(memory_space=pl.ANY)


def _chip_of(j, c):
    return (j // 2, j % 2, c)


def _own_slab(shard, dtype, j_arr, name, TR=512):
    B, r, C = shard.shape
    rh = r // 2
    TR = rh if rh <= TR else _row_tile(rh, TR)
    nt = rh // TR

    def body(j_ref, s_ref, o_ref):
        o_ref[...] = s_ref[...].astype(o_ref.dtype)

    return pl.pallas_call(
        body, name=name,
        grid_spec=pltpu.PrefetchScalarGridSpec(
            num_scalar_prefetch=1, grid=(B, 2, nt),
            in_specs=[pl.BlockSpec((None, TR, C), lambda b, h, t, jr: (b, h * nt + t, 0))],
            out_specs=pl.BlockSpec((None, None, None, TR, C), lambda b, h, t, jr: (b, jr[0], h, t, 0))),
        out_shape=jax.ShapeDtypeStruct((B, 4, 2, rh, C), dtype),
        compiler_params=_params(("parallel", "parallel", "parallel")),
    )(j_arr, shard)


def _gather_weights(bufs, name):
    n = len(bufs)

    def body(*refs):
        buf = refs[n:2 * n]
        send_sems, recv_sems, fsend_sems, frecv_sems = refs[2 * n:]
        x, y, c = lax.axis_index("x"), lax.axis_index("y"), lax.axis_index("c")
        j = 2 * x + y
        sib = (x, y, 1 - c)
        sends = []
        for d in range(1, 4):
            for a in range(n):
                cp = pltpu.make_async_remote_copy(
                    src_ref=buf[a].at[:, j, c], dst_ref=buf[a].at[:, j, c], send_sem=send_sems.at[a, d - 1],
                    recv_sem=recv_sems.at[a, d - 1], device_id=_chip_of((j + d) % 4, c), device_id_type=MESH)
                cp.start()
                sends.append(cp)
        for d in range(1, 4):
            frm = (j + 4 - d) % 4
            for a in range(n):
                pltpu.make_async_remote_copy(
                    src_ref=buf[a].at[:, frm, c], dst_ref=buf[a].at[:, frm, c], send_sem=send_sems.at[a, d - 1],
                    recv_sem=recv_sems.at[a, d - 1], device_id=_chip_of(frm, c), device_id_type=MESH).wait_recv()
                cp = pltpu.make_async_remote_copy(
                    src_ref=buf[a].at[:, frm, c], dst_ref=buf[a].at[:, frm, c], send_sem=fsend_sems.at[a, d - 1],
                    recv_sem=frecv_sems.at[a, d - 1], device_id=sib, device_id_type=MESH)
                cp.start()
                sends.append(cp)
        for d in range(1, 4):
            frm = (j + 4 - d) % 4
            for a in range(n):
                pltpu.make_async_remote_copy(
                    src_ref=buf[a].at[:, frm, 1 - c], dst_ref=buf[a].at[:, frm, 1 - c], send_sem=fsend_sems.at[a, d - 1],
                    recv_sem=frecv_sems.at[a, d - 1], device_id=sib, device_id_type=MESH).wait_recv()
        for cp in sends:
            cp.wait_send()

    return pl.pallas_call(
        body, name=name,
        in_specs=[_ANY] * n, out_specs=[_ANY] * n,
        out_shape=[jax.ShapeDtypeStruct(b.shape, b.dtype) for b in bufs],
        scratch_shapes=[pltpu.SemaphoreType.DMA((n, 3))] * 4,
        input_output_aliases={a: a for a in range(n)},
    )(*bufs)


def _pair_exchange(grads, name):
    n = len(grads)

    def body(*refs):
        src, dst = refs[:n], refs[n:2 * n]
        send_sems, recv_sems = refs[2 * n:]
        c = lax.axis_index("c")
        sib = (lax.axis_index("x"), lax.axis_index("y"), 1 - c)
        cps = []
        for a in range(n):
            cp = pltpu.make_async_remote_copy(
                src_ref=src[a].at[:, pl.ds(1 - c, 1)], dst_ref=dst[a], send_sem=send_sems.at[a],
                recv_sem=recv_sems.at[a], device_id=sib, device_id_type=MESH)
            cp.start()
            cps.append(cp)
        for cp in cps:
            cp.wait_recv()
        for cp in cps:
            cp.wait_send()

    return pl.pallas_call(
        body, name=name, in_specs=[_ANY] * n, out_specs=[_ANY] * n,
        out_shape=[jax.ShapeDtypeStruct((g.shape[0], 1) + g.shape[2:], g.dtype) for g in grads],
        scratch_shapes=[pltpu.SemaphoreType.DMA((n,)), pltpu.SemaphoreType.DMA((n,))],
    )(*grads)


def _pair_add(g4, r1, cj_arr, name, TR=512):
    B4, _, rh, C = g4.shape
    B = B4 // 4
    TR = rh if rh <= TR else _row_tile(rh, TR)

    def body(cj_ref, g_ref, r_ref, o16_ref, own_ref):
        s = g_ref[...] + r_ref[...]
        o16_ref[...] = s.astype(BF16)

        @pl.when(pl.program_id(2) == cj_ref[1])
        def _():
            own_ref[...] = s

    return pl.pallas_call(
        body, name=name,
        grid_spec=pltpu.PrefetchScalarGridSpec(
            num_scalar_prefetch=1, grid=(B, rh // TR, 4),
            in_specs=[pl.BlockSpec((None, None, TR, C), lambda b, t, p, cj: (b * 4 + p, cj[0], t, 0)),
                      pl.BlockSpec((None, None, TR, C), lambda b, t, p, cj: (b * 4 + p, 0, t, 0))],
            out_specs=[pl.BlockSpec((None, None, TR, C), lambda b, t, p, cj: (p, b, t, 0)),
                       pl.BlockSpec((None, TR, C), lambda b, t, p, cj: (b, t, 0))]),
        out_shape=[jax.ShapeDtypeStruct((4, B, rh, C), BF16), jax.ShapeDtypeStruct((B, rh, C), F32)],
        compiler_params=_params(("parallel", "parallel", "arbitrary")),
    )(cj_arr, g4, r1)


def _chip_exchange(parts, name):
    n = len(parts)

    def body(*refs):
        src, dst = refs[:n], refs[n:2 * n]
        send_sems, recv_sems = refs[2 * n:]
        j = 2 * lax.axis_index("x") + lax.axis_index("y")
        c = lax.axis_index("c")
        cps = []
        for a in range(n):
            for d in range(1, 4):
                to = (j + d) % 4
                cp = pltpu.make_async_remote_copy(
                    src_ref=src[a].at[to], dst_ref=dst[a].at[d - 1], send_sem=send_sems.at[a, d - 1],
                    recv_sem=recv_sems.at[a, d - 1], device_id=_chip_of(to, c), device_id_type=MESH)
                cp.start()
                cps.append(cp)
        for cp in cps:
            cp.wait_recv()
        for cp in cps:
            cp.wait_send()

    return pl.pallas_call(
        body, name=name, in_specs=[_ANY] * n, out_specs=[_ANY] * n,
        out_shape=[jax.ShapeDtypeStruct((3,) + p.shape[1:], p.dtype) for p in parts],
        scratch_shapes=[pltpu.SemaphoreType.DMA((n, 3)), pltpu.SemaphoreType.DMA((n, 3))],
    )(*parts)


def _chip_add(own, r2, cj_arr, name, TR=512):
    B, rh, C = own.shape
    TR = rh if rh <= TR else _row_tile(rh, TR)

    def body(cj_ref, p_ref, r_ref, o_ref):
        o_ref[...] = p_ref[...] + r_ref[0].astype(F32) + r_ref[1].astype(F32) + r_ref[2].astype(F32)

    return pl.pallas_call(
        body, name=name,
        grid_spec=pltpu.PrefetchScalarGridSpec(
            num_scalar_prefetch=1, grid=(B, rh // TR),
            in_specs=[pl.BlockSpec((None, TR, C), lambda b, t, cj: (b, t, 0)),
                      pl.BlockSpec((3, None, TR, C), lambda b, t, cj: (0, b, t, 0))],
            out_specs=pl.BlockSpec((None, None, TR, C), lambda b, t, cj: (b, cj[0], t, 0))),
        out_shape=jax.ShapeDtypeStruct((B, 2, rh, C), F32),
        compiler_params=_params(("parallel", "parallel")),
    )(cj_arr, own, r2)


def _pair_share(bufs, name):
    n = len(bufs)

    def body(*refs):
        buf = refs[n:2 * n]
        send_sems, recv_sems = refs[2 * n:]
        c = lax.axis_index("c")
        sib = (lax.axis_index("x"), lax.axis_index("y"), 1 - c)
        cps = []
        for a in range(n):
            cp = pltpu.make_async_remote_copy(
                src_ref=buf[a].at[:, c], dst_ref=buf[a].at[:, c], send_sem=send_sems.at[a],
                recv_sem=recv_sems.at[a], device_id=sib, device_id_type=MESH)
            cp.start()
            cps.append(cp)
        for a in range(n):
            pltpu.make_async_remote_copy(
                src_ref=buf[a].at[:, 1 - c], dst_ref=buf[a].at[:, 1 - c], send_sem=send_sems.at[a],
                recv_sem=recv_sems.at[a], device_id=sib, device_id_type=MESH).wait_recv()
        for cp in cps:
            cp.wait_send()

    return pl.pallas_call(
        body, name=name, in_specs=[_ANY] * n, out_specs=[_ANY] * n,
        out_shape=[jax.ShapeDtypeStruct(b.shape, b.dtype) for b in bufs],
        scratch_shapes=[pltpu.SemaphoreType.DMA((n,)), pltpu.SemaphoreType.DMA((n,))],
        input_output_aliases={a: a for a in range(n)},
    )(*bufs)


def _gather_all(flat, name):
    def body(src, dst, send_sems, recv_sems, loc_sem):
        me = 4 * lax.axis_index("x") + 2 * lax.axis_index("y") + lax.axis_index("c")
        loc = pltpu.make_async_copy(src, dst.at[me], loc_sem)
        loc.start()
        cps = []
        for d in range(1, 8):
            to = (me + d) % 8
            cp = pltpu.make_async_remote_copy(
                src_ref=src, dst_ref=dst.at[me], send_sem=send_sems.at[d - 1], recv_sem=recv_sems.at[d - 1],
                device_id=(to // 4, (to // 2) % 2, to % 2), device_id_type=MESH)
            cp.start()
            cps.append(cp)
        for d in range(1, 8):
            frm = (me + 8 - d) % 8
            pltpu.make_async_remote_copy(
                src_ref=src, dst_ref=dst.at[frm], send_sem=send_sems.at[d - 1], recv_sem=recv_sems.at[d - 1],
                device_id=(frm // 4, (frm // 2) % 2, frm % 2), device_id_type=MESH).wait_recv()
        for cp in cps:
            cp.wait_send()
        loc.wait()

    return pl.pallas_call(
        body, name=name, in_specs=[_ANY], out_specs=_ANY,
        out_shape=jax.ShapeDtypeStruct((8,) + flat.shape, flat.dtype),
        scratch_shapes=[pltpu.SemaphoreType.DMA((7,)), pltpu.SemaphoreType.DMA((7,)), pltpu.SemaphoreType.DMA],
    )(flat)


def _reduce_scatter(grads, cj_arr):
    g4 = [g.reshape(g.shape[0] * 4, 2, g.shape[1] // 8, g.shape[2]) for g in grads]
    r1 = _pair_exchange(g4, "rs_pair_exchange")
    added = [_pair_add(g, r, cj_arr, "rs_pair_add") for g, r in zip(g4, r1)]
    r2 = _chip_exchange([p16 for p16, _ in added], "rs_chip_exchange")
    halves = [_chip_add(own, r, cj_arr, "rs_chip_add") for (_, own), r in zip(added, r2)]
    full = _pair_share(halves, "rs_pair_share")
    return [f.reshape(f.shape[0], f.shape[1] * f.shape[2], f.shape[3]) for f in full]


def _relu2_epi(acc):
    return acc, jnp.square(jnp.maximum(acc, 0.0))


def _relu2_bwd_epi(acc, u):
    return (acc * (2.0 * jnp.maximum(u, 0.0)),)


def _forward_backward(x, mem, target, W, P):
    L = W["wout"].shape[0]
    S, D = x.shape
    gn = lambda l, i: P["g_norm"][l, i][None]

    saved = []
    (h,) = _resnorm_fwd(x, None, None, gn(0, 0), "norm_in")
    xr = x
    for l in range(L):
        proj = _mm(h, W["winT"], "nt", "in_proj", b_pre=(l,), tn=768)
        small = (jnp.broadcast_to(P["sinks"][l][:, None], (8, BLK)), P["ws"][l], P["bs"][l][:, :, None],
                 P["gsgu"][l][None], P["wp"][l], P["ps"][l][None])
        br = _mix_fwd(proj, *small, "mix_fwd")
        pb = lax.empty((3, S, D), F32)
        for n in range(3):
            pb = _mm(br, W["wbT"], "nt", "branch_proj", a_pre=(n,), b_pre=(3 * l + n,), into=pb, out_pre=(n,))
        merged = _merge_fwd(proj, pb, "merge_fwd")
        z = _mm(merged, W["wout"], "nn", "out_proj", b_pre=(l,))
        x1, hm = _resnorm_fwd(xr, z, gn(l, 1), gn(l, 2), "resnorm_fwd")
        qm = _mm(hm, W["wq"], "nn", "mem_q", b_pre=(l,))
        (memn,) = _resnorm_fwd(mem, None, None, P["g_mem"][l][None], "mem_norm")
        kv = _mm(memn, W["wkv"], "nn", "mem_kv", b_pre=(l,))
        om = _memattn_fwd(qm, kv, "memattn_fwd")
        ym = _mm(om, W["woT"], "nt", "mem_o", b_pre=(l,))
        x2, hf = _resnorm_fwd(x1, ym, gn(l, 3), gn(l, 4), "resnorm_fwd")
        u, a = _mm(hf, W["wupT"], "nt", "mlp_up", b_pre=(l,), out_dtypes=(F32, BF16), epi=_relu2_epi)
        yf = _mm(a, W["wdown"], "nn", "mlp_down", b_pre=(l,))
        saved.append(dict(x0=xr, h=h, proj=proj, small=small, br=br, pb=pb, merged=merged, z=z, x1=x1, hm=hm,
                          qm=qm, memn=memn, kv=kv, om=om, ym=ym, x2=x2, hf=hf, u=u, a=a, yf=yf))
        if l < L - 1:
            xr, h = _resnorm_fwd(x2, yf, gn(l, 5), gn(l + 1, 0), "resnorm_fwd")
    dres, loss = _final_fwd(saved[-1]["x2"], saved[-1]["yf"], gn(L - 1, 5), target, "loss_head")

    G = {k: lax.empty(v.shape, F32) for k, v in W.items()}
    dgn = [[None] * 6 for _ in range(L)]
    dsmall = {k: [None] * L for k in ("g_mem", "sinks", "ws", "bs", "gsgu", "wp", "ps")}
    dh = None
    for l in reversed(range(L)):
        s = saved[l]
        if l == L - 1:
            dx2, dyf, dgn[l][5] = _resnorm_bwd(s["x2"], s["yf"], gn(l, 5), None, dres, None, "resnorm_bwd_top")
        else:
            dx2, dyf, dgn[l][5], dgn[l + 1][0] = _resnorm_bwd(s["x2"], s["yf"], gn(l, 5), gn(l + 1, 0), dres, dh, "resnorm_bwd")
        du = _mm(dyf, W["wdown"], "nt", "mlp_down_dx", b_pre=(l,), out_dtypes=(BF16,), extras=(s["u"],), epi=_relu2_bwd_epi)
        G["wdown"] = _mm(s["a"], dyf, "tn", "mlp_down_dw", into=G["wdown"], out_pre=(l,))
        dhf = _mm(du, W["wupT"], "nn", "mlp_up_dx", b_pre=(l,))
        G["wupT"] = _mm(du, s["hf"], "tn", "mlp_up_dw", into=G["wupT"], out_pre=(l,))
        dx1, dym, dgn[l][3], dgn[l][4] = _resnorm_bwd(s["x1"], s["ym"], gn(l, 3), gn(l, 4), dx2, dhf, "resnorm_bwd")
        dom = _mm(dym, W["woT"], "nn", "mem_o_dx", b_pre=(l,))
        G["woT"] = _mm(dym, s["om"], "tn", "mem_o_dw", into=G["woT"], out_pre=(l,))
        dqm, dkv = _memattn_bwd(s["qm"], s["kv"], dom, "memattn_bwd")
        dmemn = _mm(dkv, W["wkv"], "nt", "mem_kv_dx", b_pre=(l,))
        G["wkv"] = _mm(s["memn"], dkv, "tn", "mem_kv_dw", into=G["wkv"], out_pre=(l,))
        _, dsmall["g_mem"][l] = _resnorm_bwd(mem, None, None, P["g_mem"][l][None], None, dmemn, "mem_norm_bwd")
        dhm = _mm(dqm, W["wq"], "nt", "mem_q_dx", b_pre=(l,))
        G["wq"] = _mm(s["hm"], dqm, "tn", "mem_q_dw", into=G["wq"], out_pre=(l,))
        dx0, dz, dgn[l][1], dgn[l][2] = _resnorm_bwd(s["x0"], s["z"], gn(l, 1), gn(l, 2), dx1, dhm, "resnorm_bwd")
        dmerged = _mm(dz, W["wout"], "nt", "out_proj_dx", b_pre=(l,))
        G["wout"] = _mm(s["merged"], dz, "tn", "out_proj_dw", into=G["wout"], out_pre=(l,))
        dproj, dpb = _merge_bwd(s["proj"], s["pb"], dmerged, "merge_bwd")
        dbr = lax.empty((3, S, 512), F32)
        for n in range(3):
            dbr = _mm(dpb, W["wbT"], "nn", "branch_proj_dx", a_pre=(n,), b_pre=(3 * l + n,), into=dbr, out_pre=(n,))
            G["wbT"] = _mm(dpb, s["br"], "tn", "branch_proj_dw", a_pre=(n,), b_pre=(n,), into=G["wbT"], out_pre=(3 * l + n,))
        (dproj, dsmall["sinks"][l], dsmall["ws"][l], dsmall["bs"][l], dsmall["gsgu"][l], dsmall["wp"][l],
         dsmall["ps"][l]) = _mix_bwd(s["proj"], dbr, dproj, *s["small"], "mix_bwd")
        dh = _mm(dproj, W["winT"], "nn", "in_proj_dx", b_pre=(l,), tk=768)
        G["winT"] = _mm(dproj, s["h"], "tn", "in_proj_dw", into=G["winT"], out_pre=(l,), tm=768)
        dres = dx0
    grad_x, dgn[0][0] = _resnorm_bwd(x, None, None, gn(0, 0), dres, dh, "norm_in_bwd")

    small_grads = dict(
        g_norm=jnp.stack([jnp.concatenate(row, axis=0) for row in dgn]),
        g_mem=jnp.concatenate(dsmall["g_mem"], axis=0),
        sinks=jnp.stack([d[:, 0] for d in dsmall["sinks"]]),
        ws=jnp.stack(dsmall["ws"]),
        bs=jnp.stack([d[:, :, 0] for d in dsmall["bs"]]),
        gsgu=jnp.concatenate(dsmall["gsgu"], axis=0),
        wp=jnp.stack(dsmall["wp"]),
        ps=jnp.concatenate(dsmall["ps"], axis=0),
    )
    return loss, grad_x, G, small_grads


_PACK_UNIT = 1024
_PACK_ROWS = 512


def _pack(arrays):
    flat, total = [], 0
    for a in arrays:
        f = a.reshape(-1)
        pad = (-f.shape[0]) % _PACK_UNIT
        flat.append(jnp.pad(f, (0, pad)) if pad else f)
        total += f.shape[0] + pad
    tail = (-total) % (_PACK_ROWS * 128)
    if tail:
        flat.append(jnp.zeros((tail,), flat[0].dtype))
    return jnp.concatenate(flat).reshape(-1, 128)


def _unpack(packed, like):
    flat, out, pos = packed.reshape(-1), [], 0
    for a in like:
        n = math.prod(a.shape)
        out.append(flat[pos:pos + n].reshape(a.shape))
        pos += n + (-n) % _PACK_UNIT
    return out


_BIG = ("w_in", "w_branch", "w_out", "w_q_mem", "w_kv_mem", "w_o_mem", "w_up", "w_down")
_SMALL = ("g_norm", "g_mem", "attn_sinks", "w_spatial", "b_spatial", "g_sgu", "w_pool", "pool_scale")
_WEIGHTS = ("g_norm", "g_mem", "w_in", "attn_sinks", "w_spatial", "b_spatial", "g_sgu", "w_pool", "pool_scale",
            "w_branch", "w_out", "w_q_mem", "w_kv_mem", "w_o_mem", "w_up", "w_down")


def _to_working(name, w):
    if name == "w_in":
        return jnp.swapaxes(w, 1, 2)
    if name == "w_branch":
        t = jnp.swapaxes(w, 2, 3)
        return t.reshape(t.shape[0] * 3, t.shape[2], t.shape[3])
    if name in ("w_o_mem", "w_up"):
        return jnp.swapaxes(w, 1, 2)
    return w


def _from_working(name, g):
    if name == "w_in":
        return jnp.swapaxes(g, 1, 2)
    if name == "w_branch":
        return jnp.swapaxes(g.reshape(g.shape[0] // 3, 3, g.shape[1], g.shape[2]), 2, 3)
    if name in ("w_o_mem", "w_up"):
        return jnp.swapaxes(g, 1, 2)
    return g


_WKEY = dict(w_in="winT", w_branch="wbT", w_out="wout", w_q_mem="wq", w_kv_mem="wkv", w_o_mem="woT",
             w_up="wupT", w_down="wdown")


def kernel(x, mem, g_norm, g_mem, w_in, attn_sinks, w_spatial, b_spatial, g_sgu, w_pool, pool_scale, w_branch, w_out, w_q_mem, w_kv_mem, w_o_mem, w_up, w_down, loss_target, m_g_norm, m_g_mem, m_w_in, m_attn_sinks, m_w_spatial, m_b_spatial, m_g_sgu, m_w_pool, m_pool_scale, m_w_branch, m_w_out, m_w_q_mem, m_w_kv_mem, m_w_o_mem, m_w_up, m_w_down, v_g_norm, v_g_mem, v_w_in, v_attn_sinks, v_w_spatial, v_b_spatial, v_g_sgu, v_w_pool, v_pool_scale, v_w_branch, v_w_out, v_w_q_mem, v_w_kv_mem, v_w_o_mem, v_w_up, v_w_down):
    w = dict(g_norm=g_norm, g_mem=g_mem, w_in=w_in, attn_sinks=attn_sinks, w_spatial=w_spatial, b_spatial=b_spatial,
             g_sgu=g_sgu, w_pool=w_pool, pool_scale=pool_scale, w_branch=w_branch, w_out=w_out, w_q_mem=w_q_mem,
             w_kv_mem=w_kv_mem, w_o_mem=w_o_mem, w_up=w_up, w_down=w_down)
    m = dict(g_norm=m_g_norm, g_mem=m_g_mem, w_in=m_w_in, attn_sinks=m_attn_sinks, w_spatial=m_w_spatial,
             b_spatial=m_b_spatial, g_sgu=m_g_sgu, w_pool=m_w_pool, pool_scale=m_pool_scale, w_branch=m_w_branch,
             w_out=m_w_out, w_q_mem=m_w_q_mem, w_kv_mem=m_w_kv_mem, w_o_mem=m_w_o_mem, w_up=m_w_up, w_down=m_w_down)
    v = dict(g_norm=v_g_norm, g_mem=v_g_mem, w_in=v_w_in, attn_sinks=v_attn_sinks, w_spatial=v_w_spatial,
             b_spatial=v_b_spatial, g_sgu=v_g_sgu, w_pool=v_w_pool, pool_scale=v_pool_scale, w_branch=v_w_branch,
             w_out=v_w_out, w_q_mem=v_w_q_mem, w_kv_mem=v_w_kv_mem, w_o_mem=v_w_o_mem, w_up=v_w_up, w_down=v_w_down)
    L = g_norm.shape[0]
    j = 2 * lax.axis_index("x") + lax.axis_index("y")
    c = lax.axis_index("c")
    j_arr = jnp.reshape(j, (1,)).astype(jnp.int32)
    cj_arr = jnp.stack([c, j]).astype(jnp.int32)

    gs = g_norm.shape[2]
    bufs = [_own_slab(_to_working(n, w[n]), BF16, j_arr, "own_slab") for n in _BIG]
    bufs.append(_own_slab(g_norm.reshape(1, L * 6 * gs // 128, 128), F32, j_arr, "own_slab_norm"))
    gathered = _gather_weights(bufs, "gather_weights")
    W = {_WKEY[n]: g.reshape(g.shape[0], 8 * g.shape[3], g.shape[4]) for n, g in zip(_BIG, gathered[:-1])}
    gn_full = jnp.transpose(gathered[-1].reshape(4, L * 6, gs), (1, 0, 2)).reshape(L, 6, 4 * gs)
    P = dict(g_norm=gn_full, g_mem=g_mem, sinks=attn_sinks, ws=w_spatial, bs=b_spatial, gsgu=g_sgu, wp=w_pool,
             ps=pool_scale)

    loss_part, grad_x, G, sg = _forward_backward(x[0], mem[0], loss_target[0], W, P)
    loss = lax.psum(loss_part[0, 0], ("x", "y", "c"))

    summed = _reduce_scatter([G[_WKEY[n]] for n in _BIG], cj_arr)
    grads = {n: _from_working(n, g) for n, g in zip(_BIG, summed)}

    full_small = [sg["g_norm"], sg["g_mem"], sg["sinks"], sg["ws"], sg["bs"], sg["gsgu"], sg["wp"], sg["ps"]]
    packed = _pack(full_small)
    total = _sum_slots(_gather_all(packed, "gather_small_grads"), "sum_small_grads")
    for n, g in zip(_SMALL, _unpack(total, full_small)):
        grads[n] = lax.dynamic_slice_in_dim(g, j * g_norm.shape[2], g_norm.shape[2], axis=2) if n == "g_norm" else g

    delta, new_m, new_v = {}, {}, {}
    for n in _BIG:
        shp = w[n].shape
        two_d = lambda t: t.reshape(-1, shp[-1])
        d_, m_, v_ = _adamw(two_d(w[n]), two_d(grads[n]), two_d(m[n]), two_d(v[n]), "adamw")
        delta[n], new_m[n], new_v[n] = d_.reshape(shp), m_.reshape(shp), v_.reshape(shp)
    small_w = [w[n] for n in _SMALL]
    d_, m_, v_ = _adamw(_pack(small_w), _pack([grads[n] for n in _SMALL]), _pack([m[n] for n in _SMALL]),
                        _pack([v[n] for n in _SMALL]), "adamw_small")
    for n, dd, mm_, vv in zip(_SMALL, _unpack(d_, small_w), _unpack(m_, small_w), _unpack(v_, small_w)):
        delta[n], new_m[n], new_v[n] = dd, mm_, vv

    return (loss, grad_x[None], *[grads[n] for n in _WEIGHTS], *[delta[n] for n in _WEIGHTS],
            *[new_m[n] for n in _WEIGHTS], *[new_v[n] for n in _WEIGHTS])
```

```python
import functools
import math

import jax
import jax.numpy as jnp
from jax import lax
from jax.experimental import pallas as pl
from jax.experimental.pallas import tpu as pltpu

F32 = jnp.float32
BF16 = jnp.bfloat16
MESH = pl.DeviceIdType.MESH

EPS = 1e-6
NEG_INF = -1e30
BLK = 128
HALO = 16
POOL_WINDOWS = (2, 4, 8, 16)
ATT_SCALE = 1.0 / math.sqrt(64.0)
MEM_SCALE = 1.0 / math.sqrt(128.0)
C_Q, C_K, C_V, C_SU, C_SV, C_PC, C_GATE, C_END = 0, 512, 640, 768, 1280, 1792, 2304, 5376

ADAM_LR, ADAM_B1, ADAM_B2, ADAM_EPS, ADAM_WD, ADAM_STEP = 0.001, 0.9, 0.999, 1e-08, 0.01, 10

VMEM_LIMIT_BYTES = 56 * 1024 * 1024

_DIMS = {
    "nn": (((1,), (0,)), ((), ())),
    "nt": (((1,), (1,)), ((), ())),
    "tn": (((0,), (0,)), ((), ())),
}


def _dot(a, b, mode):
    return lax.dot_general(a, b, _DIMS[mode], preferred_element_type=F32)


def _params(semantics):
    return pltpu.CompilerParams(dimension_semantics=semantics, vmem_limit_bytes=VMEM_LIMIT_BYTES)


def _tile(dim, pref):
    if dim <= pref:
        return dim
    t = (pref // 128) * 128
    while t >= 128:
        if dim % t == 0:
            return t
        t -= 128
    raise ValueError(f"no tile for {dim}")


def _rms(x, g):
    return x * lax.rsqrt(jnp.mean(x * x, axis=-1, keepdims=True) + EPS) * g


def _mm(a, b, mode, name, *, out_dtypes=(F32,), a_pre=(), b_pre=(), into=None, out_pre=(),
        extras=(), epi=None, tm=1024, tn=1024, tk=1024):
    a2, b2 = a.shape[len(a_pre):], b.shape[len(b_pre):]
    if mode == "nn":
        (M, K), (K2, N) = a2, b2
    elif mode == "nt":
        (M, K), (N, K2) = a2, b2
    else:
        (K, M), (K2, N) = a2, b2
    assert K == K2, (a.shape, b.shape, mode)
    tm, tn, tk = _tile(M, tm), _tile(N, tn), _tile(K, tk)
    nk = K // tk
    na, nb_, no = len(a_pre), len(b_pre), len(out_pre)
    if mode == "tn":
        a_spec = pl.BlockSpec((None,) * na + (tk, tm), lambda i, j, k: a_pre + (k, i))
    else:
        a_spec = pl.BlockSpec((None,) * na + (tm, tk), lambda i, j, k: a_pre + (i, k))
    if mode == "nt":
        b_spec = pl.BlockSpec((None,) * nb_ + (tn, tk), lambda i, j, k: b_pre + (j, k))
    else:
        b_spec = pl.BlockSpec((None,) * nb_ + (tk, tn), lambda i, j, k: b_pre + (k, j))
    tile_spec = pl.BlockSpec((tm, tn), lambda i, j, k: (i, j))
    ne, nout = len(extras), len(out_dtypes)
    in_specs = [a_spec, b_spec] + [tile_spec] * ne
    operands = [a, b, *extras]
    aliases = {}
    if into is not None:
        assert nout == 1
        in_specs.append(pl.BlockSpec(memory_space=pl.ANY))
        operands.append(into)
        aliases = {len(operands) - 1: 0}
        out_shape = [jax.ShapeDtypeStruct(into.shape, into.dtype)]
        out_specs = [pl.BlockSpec((None,) * no + (tm, tn), lambda i, j, k: out_pre + (i, j))]
    else:
        out_shape = [jax.ShapeDtypeStruct((M, N), dt) for dt in out_dtypes]
        out_specs = [tile_spec] * nout

    def body(*refs):
        a_ref, b_ref = refs[0], refs[1]
        ex = refs[2:2 + ne]
        pos = 2 + ne + (1 if into is not None else 0)
        outs = refs[pos:pos + nout]
        acc_ref = refs[pos + nout] if nk > 1 else None

        def finish(acc):
            vals = epi(acc, *[e[...] for e in ex]) if epi is not None else (acc,)
            for o, v in zip(outs, vals):
                o[...] = v.astype(o.dtype)

        def prod():
            return _dot(a_ref[...].astype(BF16), b_ref[...].astype(BF16), mode)

        if nk == 1:
            finish(prod())
        else:
            k = pl.program_id(2)

            @pl.when(k == 0)
            def _():
                acc_ref[...] = jnp.zeros_like(acc_ref)

            acc_ref[...] += prod()

            @pl.when(k == nk - 1)
            def _():
                finish(acc_ref[...])

    res = pl.pallas_call(
        body, name=name, grid=(M // tm, N // tn, nk),
        in_specs=in_specs, out_specs=out_specs, out_shape=out_shape,
        scratch_shapes=[pltpu.VMEM((tm, tn), F32)] if nk > 1 else [],
        input_output_aliases=aliases,
        compiler_params=_params(("parallel", "parallel", "arbitrary")),
    )(*operands)
    return res[0] if nout == 1 else tuple(res)


def _resnorm_fn(has_post, has_pre):
    def f(*a):
        x, k = a[0], 1
        if has_post:
            x, k = x + _rms(a[1], a[2]), 3
        outs = [x]
        if has_pre:
            outs.append(_rms(x, a[k]))
        return tuple(outs)
    return f


def _row_spec(T, W):
    return pl.BlockSpec((T, W), lambda i: (i, 0))


def _par_spec(W):
    return pl.BlockSpec((1, W), lambda i: (0, 0))


def _resnorm_fwd(xr, y, gp, gq, name, T=512, deps=()):
    S, D = xr.shape
    T = min(T, S)
    has_post, has_pre = y is not None, gq is not None
    f = _resnorm_fn(has_post, has_pre)
    ins = [xr] + ([y, gp] if has_post else []) + ([gq] if has_pre else [])
    in_specs = [_row_spec(T, D)] + ([_row_spec(T, D), _par_spec(D)] if has_post else []) + ([_par_spec(D)] if has_pre else [])
    out_shape, out_specs = [], []
    if has_post:
        out_shape.append(jax.ShapeDtypeStruct((S, D), F32)); out_specs.append(_row_spec(T, D))
    if has_pre:
        out_shape.append(jax.ShapeDtypeStruct((S, D), BF16)); out_specs.append(_row_spec(T, D))
    n_in, n_dep = len(ins), len(deps)

    def body(*refs):
        vals = f(*[r[...] for r in refs[:n_in]])
        outs = list(refs[n_in + n_dep:])
        if has_post:
            outs.pop(0)[...] = vals[0]
        if has_pre:
            outs.pop(0)[...] = vals[1].astype(BF16)

    res = pl.pallas_call(body, name=name, grid=(S // T,),
                         in_specs=in_specs + [pl.BlockSpec(memory_space=pl.ANY)] * n_dep, out_specs=out_specs,
                         out_shape=out_shape, compiler_params=_params(("parallel",)))(*ins, *deps)
    return tuple(res)


def _resnorm_bwd(xr, y, gp, gq, dres, dh, name, T=512, deps=()):
    S, D = xr.shape
    T = min(T, S)
    has_post, has_pre, has_res = y is not None, gq is not None, dres is not None
    f = _resnorm_fn(has_post, has_pre)
    ins = [xr] + ([y, gp] if has_post else []) + ([gq] if has_pre else [])
    in_specs = [_row_spec(T, D)] + ([_row_spec(T, D), _par_spec(D)] if has_post else []) + ([_par_spec(D)] if has_pre else [])
    n_prim = len(ins)
    if has_res:
        ins.append(dres); in_specs.append(_row_spec(T, D))
    if has_pre:
        ins.append(dh); in_specs.append(_row_spec(T, D))
    n_in, n_dep = len(ins), len(deps)
    out_shape = [jax.ShapeDtypeStruct((S, D), F32)]
    out_specs = [_row_spec(T, D)]
    if has_post:
        out_shape += [jax.ShapeDtypeStruct((S, D), BF16), jax.ShapeDtypeStruct((1, D), F32)]
        out_specs += [_row_spec(T, D), _par_spec(D)]
    if has_pre:
        out_shape.append(jax.ShapeDtypeStruct((1, D), F32)); out_specs.append(_par_spec(D))

    def body(*refs):
        i = pl.program_id(0)
        prim = [r[...] for r in refs[:n_prim]]
        rest = list(refs[n_prim:n_in])
        ct_x = rest.pop(0)[...] if has_res else jnp.zeros((T, D), F32)
        cts = [ct_x]
        if has_pre:
            cts.append(rest.pop(0)[...].astype(F32))
        _, vjp = jax.vjp(f, *prim)
        grads = list(vjp(tuple(cts)))
        outs = list(refs[n_in + n_dep:])
        outs.pop(0)[...] = grads.pop(0)
        acc = []
        if has_post:
            outs.pop(0)[...] = grads.pop(0).astype(BF16)
            acc.append((outs.pop(0), grads.pop(0)))
        if has_pre:
            acc.append((outs.pop(0), grads.pop(0)))

        @pl.when(i == 0)
        def _():
            for o, _g in acc:
                o[...] = jnp.zeros_like(o)

        for o, g in acc:
            o[...] += g

    res = pl.pallas_call(body, name=name, grid=(S // T,),
                         in_specs=in_specs + [pl.BlockSpec(memory_space=pl.ANY)] * n_dep, out_specs=out_specs,
                         out_shape=out_shape, compiler_params=_params(("arbitrary",)))(*ins, *deps)
    return tuple(res)


def _final_fwd(xr, y, gp, target, name, T=512):
    S, D = xr.shape
    T = min(T, S)

    def body(x_ref, y_ref, g_ref, t_ref, dy_ref, loss_ref):
        i = pl.program_id(0)
        e = x_ref[...] + _rms(y_ref[...], g_ref[...]) - t_ref[...]
        dy_ref[...] = e / D

        @pl.when(i == 0)
        def _():
            loss_ref[...] = jnp.zeros_like(loss_ref)

        loss_ref[...] += 0.5 * jnp.sum(jnp.sum(e * e, axis=-1, keepdims=True) / D, axis=0, keepdims=True)

    return pl.pallas_call(
        body, name=name, grid=(S // T,),
        in_specs=[_row_spec(T, D), _row_spec(T, D), _par_spec(D), _row_spec(T, D)],
        out_specs=[_row_spec(T, D), pl.BlockSpec((1, 128), lambda i: (0, 0))],
        out_shape=[jax.ShapeDtypeStruct((S, D), F32), jax.ShapeDtypeStruct((1, 128), F32)],
        compiler_params=_params(("arbitrary",)))(xr, y, gp, target)


def _lane_lo():
    return lax.broadcasted_iota(jnp.int32, (1, BLK), 1) < 64


def _att_mask(not_first):
    r = lax.broadcasted_iota(jnp.int32, (BLK, 2 * BLK), 0)
    c = lax.broadcasted_iota(jnp.int32, (BLK, 2 * BLK), 1)
    qc, kc = 2 + r // 64, c // 64
    return (kc <= qc) & (kc >= qc - 2) & (not_first | (c >= BLK))


def _softmax_sink(s, sk):
    m = jnp.maximum(jnp.max(s, axis=-1, keepdims=True), sk)
    e = jnp.exp(s - m)
    es = jnp.exp(sk - m)
    z = jnp.sum(e, axis=-1, keepdims=True) + es
    return e / z, es / z


def _att_operands(cur, kvp, t, lo):
    h = t // 2
    qt = cur[:, C_Q + BLK * t:C_Q + BLK * (t + 1)]
    q_lo = jnp.where(lo, qt, 0.0).astype(BF16)
    q_hi = jnp.where(lo, 0.0, qt).astype(BF16)
    kband = jnp.concatenate([kvp[:, 0:BLK], cur[:, C_K:C_K + BLK]], axis=0)
    vband = jnp.concatenate([kvp[:, BLK:2 * BLK], cur[:, C_V:C_V + BLK]], axis=0)
    kroll = pltpu.roll(kband, 64, 1)
    vroll = pltpu.roll(vband, 64, 1)
    ka, kb = (kband, kroll) if h == 0 else (kroll, kband)
    va = jnp.where(lo, vband if h == 0 else vroll, 0.0)
    vb = jnp.where(lo, 0.0, vroll if h == 0 else vband)
    return q_lo, q_hi, ka.astype(BF16), kb.astype(BF16), va.astype(BF16), vb.astype(BF16)


def _sgu_mask():
    r = lax.broadcasted_iota(jnp.int32, (BLK, BLK), 0)
    c = lax.broadcasted_iota(jnp.int32, (BLK, BLK), 1)
    return (c // 64) <= (r // 64)


def _pool_cnt(blk, w):
    t = blk * BLK + lax.broadcasted_iota(jnp.int32, (BLK, 1), 0)
    return jnp.minimum(t + 1, w).astype(F32)


def _mix_in_specs(nb, rev):
    def b(i):
        return nb - 1 - i if rev else i
    return [
        pl.BlockSpec((BLK, C_GATE), lambda i: (b(i), 0)),
        pl.BlockSpec((BLK, 2 * BLK), lambda i: (jnp.maximum(b(i) - 1, 0), C_K // (2 * BLK))),
        pl.BlockSpec((HALO, C_GATE), lambda i: (jnp.maximum(b(i) * (BLK // HALO) - 1, 0), 0)),
        pl.BlockSpec((8, BLK), lambda i: (0, 0)),
        pl.BlockSpec((4, BLK, BLK), lambda i: (0, 0, 0)),
        pl.BlockSpec((4, BLK, 1), lambda i: (0, 0, 0)),
        pl.BlockSpec((1, 512), lambda i: (0, 0)),
        pl.BlockSpec((4, BLK, BLK), lambda i: (0, 0, 0)),
        pl.BlockSpec((1, 512), lambda i: (0, 0)),
    ]


def _mix_fwd(proj, sinks_b, ws, bs3, gsgu, wp, ps, name):
    S = proj.shape[0]
    nb = S // BLK

    def body(cur_ref, kvp_ref, pcp_ref, sk_ref, ws_ref, bs_ref, gs_ref, wp_ref, ps_ref, br_ref, ext_ref):
        i = pl.program_id(0)
        not_first = i > 0
        lo = _lane_lo()
        cur, kvp = cur_ref[...], kvp_ref[...]
        mask = _att_mask(not_first)
        for t in range(4):
            q_lo, q_hi, ka, kb, va, vb = _att_operands(cur, kvp, t, lo)
            s_lo = jnp.where(mask, _dot(q_lo, ka, "nt") * ATT_SCALE, NEG_INF)
            s_hi = jnp.where(mask, _dot(q_hi, kb, "nt") * ATT_SCALE, NEG_INF)
            p_lo, _ = _softmax_sink(s_lo, sk_ref[2 * t:2 * t + 1, 0:1])
            p_hi, _ = _softmax_sink(s_hi, sk_ref[2 * t + 1:2 * t + 2, 0:1])
            o = _dot(p_lo.astype(BF16), va, "nn") + _dot(p_hi.astype(BF16), vb, "nn")
            br_ref[0, :, BLK * t:BLK * (t + 1)] = o.astype(BF16)
        gu = jax.nn.gelu(cur[:, C_SU:C_SV])
        vn = _rms(jax.nn.gelu(cur[:, C_SV:C_PC]), gs_ref[...]).astype(BF16)
        wmask = _sgu_mask()
        for g in range(4):
            wm = jnp.where(wmask, ws_ref[g], 0.0).astype(BF16)
            sp = _dot(wm, vn[:, BLK * g:BLK * (g + 1)], "nn") + bs_ref[g]
            br_ref[1, :, BLK * g:BLK * (g + 1)] = (gu[:, BLK * g:BLK * (g + 1)] * sp).astype(BF16)
        c = cur[:, C_PC:C_GATE]
        ext_ref[0:HALO, :] = jnp.where(not_first, pcp_ref[:, C_PC:C_GATE], 0.0)
        ext_ref[HALO:HALO + BLK, :] = c
        for g, w in enumerate(POOL_WINDOWS):
            sl = slice(BLK * g, BLK * (g + 1))
            acc = ext_ref[HALO:HALO + BLK, sl]
            for k in range(1, w):
                acc = acc + ext_ref[HALO - k:HALO - k + BLK, sl]
            pooled = acc / _pool_cnt(i, w) - c[:, sl]
            mixed = _dot(pooled.astype(BF16), wp_ref[g].astype(BF16), "nn")
            br_ref[2, :, sl] = (mixed * ps_ref[:, sl]).astype(BF16)

    return pl.pallas_call(
        body, name=name, grid=(nb,),
        in_specs=_mix_in_specs(nb, False),
        out_specs=pl.BlockSpec((3, BLK, 512), lambda i: (0, i, 0)),
        out_shape=jax.ShapeDtypeStruct((3, S, 512), BF16),
        scratch_shapes=[pltpu.VMEM((HALO + BLK, 512), F32)],
        compiler_params=_params(("parallel",)),
    )(proj, proj, proj, sinks_b, ws, bs3, gsgu, wp, ps)


def _mix_bwd(proj, dbr, dproj, sinks_b, ws, bs3, gsgu, wp, ps, name):
    S = proj.shape[0]
    nb = S // BLK

    def body(cur_ref, kvp_ref, pcp_ref, sk_ref, ws_ref, bs_ref, gs_ref, wp_ref, ps_ref, dbr_ref, _dproj_in,
             dp_ref, dsk_ref, dws_ref, dbs_ref, dgs_ref, dwp_ref, dps_ref,
             ext_ref, z_ref, ckv_ref, cpc_ref):
        i = pl.program_id(0)
        blk = nb - 1 - i
        not_first = blk > 0
        lo = _lane_lo()

        @pl.when(i == 0)
        def _():
            for r in (dsk_ref, dws_ref, dbs_ref, dgs_ref, dwp_ref, dps_ref, ckv_ref, cpc_ref, z_ref):
                r[...] = jnp.zeros_like(r)

        cur, kvp = cur_ref[...], kvp_ref[...]
        mask = _att_mask(not_first)
        dk_band = jnp.zeros((2 * BLK, BLK), F32)
        dk_roll = jnp.zeros((2 * BLK, BLK), F32)
        dv_band = jnp.zeros((2 * BLK, BLK), F32)
        dv_roll = jnp.zeros((2 * BLK, BLK), F32)
        for t in range(4):
            h = t // 2
            q_lo, q_hi, ka, kb, va, vb = _att_operands(cur, kvp, t, lo)
            do = dbr_ref[0, :, BLK * t:BLK * (t + 1)].astype(BF16)
            dq = jnp.zeros((BLK, BLK), F32)
            for half, (qm, km, vm) in enumerate(((q_lo, ka, va), (q_hi, kb, vb))):
                sk = sk_ref[2 * t + half:2 * t + half + 1, 0:1]
                s = jnp.where(mask, _dot(qm, km, "nt") * ATT_SCALE, NEG_INF)
                p, p_sink = _softmax_sink(s, sk)
                dp = _dot(do, vm, "nt")
                rs = jnp.sum(p * dp, axis=-1, keepdims=True)
                ds = (p * (dp - rs) * ATT_SCALE).astype(BF16)
                dsk_ref[2 * t + half:2 * t + half + 1, :] += jnp.broadcast_to(
                    -jnp.sum(p_sink * rs, axis=0, keepdims=True), (1, BLK))
                dvm = _dot(p.astype(BF16), do, "tn")
                dkm = _dot(ds, qm, "tn")
                dqm = _dot(ds, km, "nn")
                if half == 0:
                    dq = dq + jnp.where(lo, dqm, 0.0)
                    dvm = jnp.where(lo, dvm, 0.0)
                else:
                    dq = dq + jnp.where(lo, 0.0, dqm)
                    dvm = jnp.where(lo, 0.0, dvm)
                if (h == 0) == (half == 0):
                    dk_band, dv_band = dk_band + dkm, dv_band + dvm
                else:
                    dk_roll, dv_roll = dk_roll + dkm, dv_roll + dvm
            dp_ref[:, C_Q + BLK * t:C_Q + BLK * (t + 1)] = dq.astype(BF16)
        dk = dk_band + pltpu.roll(dk_roll, 64, 1)
        dv = dv_band + pltpu.roll(dv_roll, 64, 1)
        dp_ref[:, C_K:C_K + BLK] = (dk[BLK:] + ckv_ref[:, 0:BLK]).astype(BF16)
        dp_ref[:, C_V:C_V + BLK] = (dv[BLK:] + ckv_ref[:, BLK:]).astype(BF16)
        ckv_ref[:, 0:BLK] = dk[:BLK]
        ckv_ref[:, BLK:] = dv[:BLK]
        su, sv = cur[:, C_SU:C_SV], cur[:, C_SV:C_PC]
        gu, vjp_u = jax.vjp(jax.nn.gelu, su)
        vn, vjp_v = jax.vjp(lambda a, g: _rms(jax.nn.gelu(a), g), sv, gs_ref[...])
        vn16 = vn.astype(BF16)
        wmask = _sgu_mask()
        dgu, dvn = [], []
        for g in range(4):
            sl = slice(BLK * g, BLK * (g + 1))
            wm = jnp.where(wmask, ws_ref[g], 0.0).astype(BF16)
            sp = _dot(wm, vn16[:, sl], "nn") + bs_ref[g]
            dyb = dbr_ref[1, :, sl]
            dgu.append(dyb * sp)
            dsp = dyb * gu[:, sl]
            dsp16 = dsp.astype(BF16)
            dvn.append(_dot(wm, dsp16, "tn"))
            dws_ref[g] += jnp.where(wmask, _dot(dsp16, vn16[:, sl], "nt"), 0.0)
            dbs_ref[g] += jnp.sum(dsp, axis=1, keepdims=True)
        (dsu,) = vjp_u(jnp.concatenate(dgu, axis=1))
        dsv, dgs = vjp_v(jnp.concatenate(dvn, axis=1))
        dp_ref[:, C_SU:C_SV] = dsu.astype(BF16)
        dp_ref[:, C_SV:C_PC] = dsv.astype(BF16)
        dgs_ref[...] += dgs
        c = cur[:, C_PC:C_GATE]
        ext_ref[0:HALO, :] = jnp.where(not_first, pcp_ref[:, C_PC:C_GATE], 0.0)
        ext_ref[HALO:HALO + BLK, :] = c
        for g, w in enumerate(POOL_WINDOWS):
            sl = slice(BLK * g, BLK * (g + 1))
            acc = ext_ref[HALO:HALO + BLK, sl]
            for k in range(1, w):
                acc = acc + ext_ref[HALO - k:HALO - k + BLK, sl]
            cnt = _pool_cnt(blk, w)
            pooled16 = (acc / cnt - c[:, sl]).astype(BF16)
            wp16 = wp_ref[g].astype(BF16)
            mixed = _dot(pooled16, wp16, "nn")
            dyc = dbr_ref[2, :, sl]
            dps_ref[:, sl] += jnp.sum(dyc * mixed, axis=0, keepdims=True)
            dmixed16 = (dyc * ps_ref[:, sl]).astype(BF16)
            dwp_ref[g] += _dot(pooled16, dmixed16, "tn")
            dpooled = _dot(dmixed16, wp16, "nt")
            z_ref[HALO:HALO + BLK, sl] = dpooled / cnt
            dext = z_ref[0:HALO + BLK, sl]
            for k in range(1, w):
                dext = dext + z_ref[k:k + HALO + BLK, sl]
            dp_ref[:, C_PC + BLK * g:C_PC + BLK * (g + 1)] = (
                dext[HALO:] - dpooled + jnp.concatenate([jnp.zeros((BLK - HALO, BLK), F32), cpc_ref[:, sl]], axis=0)
            ).astype(BF16)
            cpc_ref[:, sl] = dext[:HALO]

    n_in = 11
    small = [jax.ShapeDtypeStruct((8, BLK), F32), jax.ShapeDtypeStruct((4, BLK, BLK), F32),
             jax.ShapeDtypeStruct((4, BLK, 1), F32), jax.ShapeDtypeStruct((1, 512), F32),
             jax.ShapeDtypeStruct((4, BLK, BLK), F32), jax.ShapeDtypeStruct((1, 512), F32)]
    small_specs = [pl.BlockSpec((8, BLK), lambda i: (0, 0)), pl.BlockSpec((4, BLK, BLK), lambda i: (0, 0, 0)),
                   pl.BlockSpec((4, BLK, 1), lambda i: (0, 0, 0)), pl.BlockSpec((1, 512), lambda i: (0, 0)),
                   pl.BlockSpec((4, BLK, BLK), lambda i: (0, 0, 0)), pl.BlockSpec((1, 512), lambda i: (0, 0))]
    res = pl.pallas_call(
        body, name=name, grid=(nb,),
        in_specs=_mix_in_specs(nb, True) + [
            pl.BlockSpec((3, BLK, 512), lambda i: (0, nb - 1 - i, 0)),
            pl.BlockSpec(memory_space=pl.ANY)],
        out_specs=[pl.BlockSpec((BLK, C_GATE), lambda i: (nb - 1 - i, 0))] + small_specs,
        out_shape=[jax.ShapeDtypeStruct(dproj.shape, dproj.dtype)] + small,
        scratch_shapes=[pltpu.VMEM((HALO + BLK, 512), F32), pltpu.VMEM((2 * HALO + BLK, 512), F32),
                        pltpu.VMEM((BLK, 2 * BLK), F32), pltpu.VMEM((HALO, 512), F32)],
        input_output_aliases={n_in - 1: 0},
        compiler_params=_params(("arbitrary",)),
    )(proj, proj, proj, sinks_b, ws, bs3, gsgu, wp, ps, dbr, dproj)
    return tuple(res)


_GW = 256


def _merge_fwd(proj, pb, name, T=1024):
    S, D = pb.shape[1], pb.shape[2]
    T = min(T, S)

    def body(gate_ref, pb_ref, out_ref, acc_ref):
        n = pl.program_id(2)

        @pl.when(n == 0)
        def _():
            acc_ref[...] = jnp.zeros_like(acc_ref)

        acc_ref[...] += jax.nn.sigmoid(gate_ref[...]) * pb_ref[...]

        @pl.when(n == 2)
        def _():
            out_ref[...] = acc_ref[...].astype(BF16)

    return pl.pallas_call(
        body, name=name, grid=(S // T, D // _GW, 3),
        in_specs=[pl.BlockSpec((T, _GW), lambda i, j, n: (i, C_GATE // _GW + n * (D // _GW) + j)),
                  pl.BlockSpec((None, T, _GW), lambda i, j, n: (n, i, j))],
        out_specs=pl.BlockSpec((T, _GW), lambda i, j, n: (i, j)),
        out_shape=jax.ShapeDtypeStruct((S, D), BF16),
        scratch_shapes=[pltpu.VMEM((T, _GW), F32)],
        compiler_params=_params(("parallel", "parallel", "arbitrary")),
    )(proj, pb)


def _merge_bwd(proj, pb, dmerged, name, T=1024):
    S, D = pb.shape[1], pb.shape[2]
    T = min(T, S)

    def body(gate_ref, pb_ref, dm_ref, dgate_ref, dpb_ref):
        sg = jax.nn.sigmoid(gate_ref[...])
        dm = dm_ref[...]
        dpb_ref[...] = (dm * sg).astype(BF16)
        dgate_ref[...] = (dm * pb_ref[...] * sg * (1.0 - sg)).astype(BF16)

    gate_map = lambda i, n, j: (i, C_GATE // _GW + n * (D // _GW) + j)
    return pl.pallas_call(
        body, name=name, grid=(S // T, 3, D // _GW),
        in_specs=[pl.BlockSpec((T, _GW), gate_map),
                  pl.BlockSpec((None, T, _GW), lambda i, n, j: (n, i, j)),
                  pl.BlockSpec((T, _GW), lambda i, n, j: (i, j))],
        out_specs=[pl.BlockSpec((T, _GW), gate_map),
                   pl.BlockSpec((None, T, _GW), lambda i, n, j: (n, i, j))],
        out_shape=[jax.ShapeDtypeStruct((S, C_END), BF16), jax.ShapeDtypeStruct((3, S, D), BF16)],
        compiler_params=_params(("parallel", "parallel", "parallel")),
    )(proj, pb, dmerged)


def _memattn_fwd(qm, kv, name, T=512):
    S, NM = qm.shape[0], kv.shape[0]
    T = min(T, S)

    def body(q_ref, kv_ref, o_ref):
        for h in range(4):
            sl = slice(128 * h, 128 * (h + 1))
            k = kv_ref[:, sl].astype(BF16)
            v = kv_ref[:, 512 + 128 * h:512 + 128 * (h + 1)].astype(BF16)
            s = _dot(q_ref[:, sl].astype(BF16), k, "nt") * MEM_SCALE
            p = jax.nn.softmax(s, axis=-1)
            o_ref[:, sl] = _dot(p.astype(BF16), v, "nn").astype(BF16)

    return pl.pallas_call(
        body, name=name, grid=(S // T,),
        in_specs=[_row_spec(T, 512), pl.BlockSpec((NM, 1024), lambda i: (0, 0))],
        out_specs=_row_spec(T, 512), out_shape=jax.ShapeDtypeStruct((S, 512), BF16),
        compiler_params=_params(("parallel",)))(qm, kv)


def _memattn_bwd(qm, kv, dom, name, T=512):
    S, NM = qm.shape[0], kv.shape[0]
    T = min(T, S)

    def body(q_ref, kv_ref, do_ref, dq_ref, dkv_ref):
        i = pl.program_id(0)

        @pl.when(i == 0)
        def _():
            dkv_ref[...] = jnp.zeros_like(dkv_ref)

        for h in range(4):
            sl = slice(128 * h, 128 * (h + 1))
            sv_ = slice(512 + 128 * h, 512 + 128 * (h + 1))
            q = q_ref[:, sl].astype(BF16)
            k = kv_ref[:, sl].astype(BF16)
            v = kv_ref[:, sv_].astype(BF16)
            do = do_ref[:, sl].astype(BF16)
            p = jax.nn.softmax(_dot(q, k, "nt") * MEM_SCALE, axis=-1)
            dp = _dot(do, v, "nt")
            ds = (p * (dp - jnp.sum(p * dp, axis=-1, keepdims=True)) * MEM_SCALE).astype(BF16)
            dq_ref[:, sl] = _dot(ds, k, "nn").astype(BF16)
            dkv_ref[:, sl] += _dot(ds, q, "tn")
            dkv_ref[:, sv_] += _dot(p.astype(BF16), do, "tn")

    return pl.pallas_call(
        body, name=name, grid=(S // T,),
        in_specs=[_row_spec(T, 512), pl.BlockSpec((NM, 1024), lambda i: (0, 0)), _row_spec(T, 512)],
        out_specs=[_row_spec(T, 512), pl.BlockSpec((NM, 1024), lambda i: (0, 0))],
        out_shape=[jax.ShapeDtypeStruct((S, 512), BF16), jax.ShapeDtypeStruct((NM, 1024), F32)],
        compiler_params=_params(("arbitrary",)))(qm, kv, dom)


def _adamw(w, g, m, v, name, TR=512):
    R, C = w.shape
    TR = R if R <= TR else _row_tile(R, TR)
    c1 = 1.0 - ADAM_B1 ** ADAM_STEP
    c2 = 1.0 - ADAM_B2 ** ADAM_STEP

    def body(w_ref, g_ref, m_ref, v_ref, d_ref, nm_ref, nv_ref):
        gv = g_ref[...]
        nm = ADAM_B1 * m_ref[...] + (1.0 - ADAM_B1) * gv
        nv = ADAM_B2 * v_ref[...] + (1.0 - ADAM_B2) * jnp.square(gv)
        d_ref[...] = -ADAM_LR * ((nm / c1) / (jnp.sqrt(nv / c2) + ADAM_EPS) + ADAM_WD * w_ref[...])
        nm_ref[...] = nm
        nv_ref[...] = nv

    spec = pl.BlockSpec((TR, C), lambda i: (i, 0))
    return pl.pallas_call(
        body, name=name, grid=(R // TR,), in_specs=[spec] * 4, out_specs=[spec] * 3,
        out_shape=[jax.ShapeDtypeStruct((R, C), F32)] * 3,
        compiler_params=_params(("parallel",)))(w, g, m, v)


def _row_tile(R, pref):
    t = (pref // 8) * 8
    while t >= 8:
        if R % t == 0:
            return t
        t -= 8
    raise ValueError(f"no row tile for {R}")


def _sum_slots(stack, name, TR=512):
    n, R, C = stack.shape
    TR = R if R <= TR else _row_tile(R, TR)

    def body(s_ref, o_ref):
        acc = s_ref[0]
        for k in range(1, n):
            acc = acc + s_ref[k]
        o_ref[...] = acc

    return pl.pallas_call(
        body, name=name, grid=(R // TR,),
        in_specs=[pl.BlockSpec((n, TR, C), lambda i: (0, i, 0))],
        out_specs=pl.BlockSpec((TR, C), lambda i: (i, 0)),
        out_shape=jax.ShapeDtypeStruct((R, C), F32),
        compiler_params=_params(("parallel",)))(stack)


_ANY = pl.BlockSpec(memory_space=pl.ANY)


def _chip_of(j, c):
    return (j // 2, j % 2, c)


def _own_slab(shard, dtype, j_arr, name, first=0, count=None, TR=512):
    N, r, C = shard.shape
    B = N if count is None else count
    rh = r // 2
    TR = rh if rh <= TR else _row_tile(rh, TR)
    nt = rh // TR

    def body(j_ref, s_ref, o_ref):
        o_ref[...] = s_ref[...].astype(o_ref.dtype)

    return pl.pallas_call(
        body, name=name,
        grid_spec=pltpu.PrefetchScalarGridSpec(
            num_scalar_prefetch=1, grid=(B, 2, nt),
            in_specs=[pl.BlockSpec((None, TR, C), lambda b, h, t, jr: (first + b, h * nt + t, 0))],
            out_specs=pl.BlockSpec((None, None, None, TR, C), lambda b, h, t, jr: (b, jr[0], h, t, 0))),
        out_shape=jax.ShapeDtypeStruct((B, 4, 2, rh, C), dtype),
        compiler_params=_params(("parallel", "parallel", "parallel")),
    )(j_arr, shard)


def _gather_weights(bufs, name):
    n = len(bufs)

    def body(*refs):
        buf = refs[n:2 * n]
        send_sems, recv_sems, fsend_sems, frecv_sems = refs[2 * n:]
        x, y, c = lax.axis_index("x"), lax.axis_index("y"), lax.axis_index("c")
        j = 2 * x + y
        sib = (x, y, 1 - c)
        sends = []
        for d in range(1, 4):
            for a in range(n):
                cp = pltpu.make_async_remote_copy(
                    src_ref=buf[a].at[:, j, c], dst_ref=buf[a].at[:, j, c], send_sem=send_sems.at[a, d - 1],
                    recv_sem=recv_sems.at[a, d - 1], device_id=_chip_of((j + d) % 4, c), device_id_type=MESH)
                cp.start()
                sends.append(cp)
        for d in range(1, 4):
            frm = (j + 4 - d) % 4
            for a in range(n):
                pltpu.make_async_remote_copy(
                    src_ref=buf[a].at[:, frm, c], dst_ref=buf[a].at[:, frm, c], send_sem=send_sems.at[a, d - 1],
                    recv_sem=recv_sems.at[a, d - 1], device_id=_chip_of(frm, c), device_id_type=MESH).wait_recv()
                cp = pltpu.make_async_remote_copy(
                    src_ref=buf[a].at[:, frm, c], dst_ref=buf[a].at[:, frm, c], send_sem=fsend_sems.at[a, d - 1],
                    recv_sem=frecv_sems.at[a, d - 1], device_id=sib, device_id_type=MESH)
                cp.start()
                sends.append(cp)
        for d in range(1, 4):
            frm = (j + 4 - d) % 4
            for a in range(n):
                pltpu.make_async_remote_copy(
                    src_ref=buf[a].at[:, frm, 1 - c], dst_ref=buf[a].at[:, frm, 1 - c], send_sem=fsend_sems.at[a, d - 1],
                    recv_sem=frecv_sems.at[a, d - 1], device_id=sib, device_id_type=MESH).wait_recv()
        for cp in sends:
            cp.wait_send()

    return pl.pallas_call(
        body, name=name,
        in_specs=[_ANY] * n, out_specs=[_ANY] * n,
        out_shape=[jax.ShapeDtypeStruct(b.shape, b.dtype) for b in bufs],
        scratch_shapes=[pltpu.SemaphoreType.DMA((n, 3))] * 4,
        input_output_aliases={a: a for a in range(n)},
    )(*bufs)


_HBM = pl.BlockSpec(memory_space=pltpu.HBM)
_SEM = pl.BlockSpec(memory_space=pltpu.SEMAPHORE)
_DATAFLOW = pltpu.SideEffectType.DATAFLOW_SIDE_EFFECTING


def _in_hbm(arrays):
    return [pltpu.with_memory_space_constraint(a, pltpu.HBM) for a in arrays]


def _gather_start(bufs, name):
    n = len(bufs)

    def body(*refs):
        buf = refs[:n]
        send_sems, recv_sems = refs[n], refs[n + 1]
        token = refs[2 * n + 2]
        c = lax.axis_index("c")
        j = 2 * lax.axis_index("x") + lax.axis_index("y")
        for d in range(1, 4):
            for a in range(n):
                pltpu.make_async_remote_copy(
                    src_ref=buf[a].at[:, j, c], dst_ref=buf[a].at[:, j, c], send_sem=send_sems.at[3 * a + d - 1],
                    recv_sem=recv_sems.at[3 * a + d - 1], device_id=_chip_of((j + d) % 4, c), device_id_type=MESH).start()
        token[...] = jnp.zeros_like(token)

    return pl.pallas_call(
        body, name=name,
        out_shape=(pltpu.SemaphoreType.DMA((3 * n,)), pltpu.SemaphoreType.DMA((3 * n,)),
                   *[pltpu.HBM(b.shape, b.dtype) for b in bufs], jax.ShapeDtypeStruct((8, 128), F32)),
        in_specs=[_HBM] * n,
        out_specs=(_SEM, _SEM, *[_HBM] * n, pl.BlockSpec(memory_space=pltpu.VMEM)),
        input_output_aliases={a: 2 + a for a in range(n)},
        compiler_params=pltpu.CompilerParams(has_side_effects=_DATAFLOW),
    )(*_in_hbm(bufs))


def _gather_wait(handle, after, name):
    send_sems, recv_sems, *bufs = handle[:-1]
    n = len(bufs)

    def body(*refs):
        buf = refs[:n]
        send_sems, recv_sems = refs[n], refs[n + 1]
        c = lax.axis_index("c")
        j = 2 * lax.axis_index("x") + lax.axis_index("y")
        for d in range(1, 4):
            frm = (j + 4 - d) % 4
            for a in range(n):
                cp = pltpu.make_async_remote_copy(
                    src_ref=buf[a].at[:, j, c], dst_ref=buf[a].at[:, frm, c], send_sem=send_sems.at[3 * a + d - 1],
                    recv_sem=recv_sems.at[3 * a + d - 1], device_id=_chip_of(frm, c), device_id_type=MESH)
                cp.wait_send()
                cp.wait_recv()

    return pl.pallas_call(
        body, name=name,
        out_shape=[pltpu.HBM(b.shape, b.dtype) for b in bufs],
        in_specs=[_HBM] * n + [_SEM, _SEM, _ANY], out_specs=[_HBM] * n,
        input_output_aliases={a: a for a in range(n)},
        compiler_params=pltpu.CompilerParams(has_side_effects=_DATAFLOW),
    )(*bufs, send_sems, recv_sems, after)


def _gather_forward(bufs, name):
    n = len(bufs)

    def body(*refs):
        buf = refs[n:2 * n]
        send_sems, recv_sems = refs[2 * n:]
        x, y, c = lax.axis_index("x"), lax.axis_index("y"), lax.axis_index("c")
        j = 2 * x + y
        sib = (x, y, 1 - c)
        sends = []
        for d in range(1, 4):
            frm = (j + 4 - d) % 4
            for a in range(n):
                cp = pltpu.make_async_remote_copy(
                    src_ref=buf[a].at[:, frm, c], dst_ref=buf[a].at[:, frm, c], send_sem=send_sems.at[a, d - 1],
                    recv_sem=recv_sems.at[a, d - 1], device_id=sib, device_id_type=MESH)
                cp.start()
                sends.append(cp)
        for d in range(1, 4):
            frm = (j + 4 - d) % 4
            for a in range(n):
                pltpu.make_async_remote_copy(
                    src_ref=buf[a].at[:, frm, 1 - c], dst_ref=buf[a].at[:, frm, 1 - c], send_sem=send_sems.at[a, d - 1],
                    recv_sem=recv_sems.at[a, d - 1], device_id=sib, device_id_type=MESH).wait_recv()
        for cp in sends:
            cp.wait_send()

    return pl.pallas_call(
        body, name=name,
        in_specs=[_ANY] * n, out_specs=[_ANY] * n,
        out_shape=[jax.ShapeDtypeStruct(b.shape, b.dtype) for b in bufs],
        scratch_shapes=[pltpu.SemaphoreType.DMA((n, 3))] * 2,
        input_output_aliases={a: a for a in range(n)},
    )(*bufs)


def _chip_start(parts, name):
    n = len(parts)
    lands = [lax.empty((3,) + p.shape[1:], p.dtype) for p in parts]

    def body(*refs):
        src, dst = refs[:n], refs[n:2 * n]
        send_sems, recv_sems = refs[2 * n], refs[2 * n + 1]
        token = refs[4 * n + 2]
        c = lax.axis_index("c")
        j = 2 * lax.axis_index("x") + lax.axis_index("y")
        for d in range(1, 4):
            to = (j + d) % 4
            for a in range(n):
                pltpu.make_async_remote_copy(
                    src_ref=src[a].at[to], dst_ref=dst[a].at[d - 1], send_sem=send_sems.at[3 * a + d - 1],
                    recv_sem=recv_sems.at[3 * a + d - 1], device_id=_chip_of(to, c), device_id_type=MESH).start()
        token[...] = jnp.zeros_like(token)

    both = list(parts) + lands
    return pl.pallas_call(
        body, name=name,
        out_shape=(pltpu.SemaphoreType.DMA((3 * n,)), pltpu.SemaphoreType.DMA((3 * n,)),
                   *[pltpu.HBM(b.shape, b.dtype) for b in both], jax.ShapeDtypeStruct((8, 128), F32)),
        in_specs=[_HBM] * (2 * n),
        out_specs=(_SEM, _SEM, *[_HBM] * (2 * n), pl.BlockSpec(memory_space=pltpu.VMEM)),
        input_output_aliases={a: 2 + a for a in range(2 * n)},
        compiler_params=pltpu.CompilerParams(has_side_effects=_DATAFLOW),
    )(*_in_hbm(both))


def _chip_wait(handle, after, name):
    send_sems, recv_sems, *both = handle[:-1]
    n = len(both) // 2

    def body(*refs):
        src, dst = refs[:n], refs[n:2 * n]
        send_sems, recv_sems = refs[2 * n], refs[2 * n + 1]
        c = lax.axis_index("c")
        j = 2 * lax.axis_index("x") + lax.axis_index("y")
        for d in range(1, 4):
            to = (j + d) % 4
            for a in range(n):
                cp = pltpu.make_async_remote_copy(
                    src_ref=src[a].at[to], dst_ref=dst[a].at[d - 1], send_sem=send_sems.at[3 * a + d - 1],
                    recv_sem=recv_sems.at[3 * a + d - 1], device_id=_chip_of(to, c), device_id_type=MESH)
                cp.wait_send()
                cp.wait_recv()

    res = pl.pallas_call(
        body, name=name,
        out_shape=[pltpu.HBM(b.shape, b.dtype) for b in both],
        in_specs=[_HBM] * (2 * n) + [_SEM, _SEM, _ANY], out_specs=[_HBM] * (2 * n),
        input_output_aliases={a: a for a in range(2 * n)},
        compiler_params=pltpu.CompilerParams(has_side_effects=_DATAFLOW),
    )(*both, send_sems, recv_sems, after)
    return list(res[n:])


def _pair_exchange(grads, name):
    n = len(grads)

    def body(*refs):
        src, dst = refs[:n], refs[n:2 * n]
        send_sems, recv_sems = refs[2 * n:]
        c = lax.axis_index("c")
        sib = (lax.axis_index("x"), lax.axis_index("y"), 1 - c)
        cps = []
        for a in range(n):
            cp = pltpu.make_async_remote_copy(
                src_ref=src[a].at[:, pl.ds(1 - c, 1)], dst_ref=dst[a], send_sem=send_sems.at[a],
                recv_sem=recv_sems.at[a], device_id=sib, device_id_type=MESH)
            cp.start()
            cps.append(cp)
        for cp in cps:
            cp.wait_recv()
        for cp in cps:
            cp.wait_send()

    return pl.pallas_call(
        body, name=name, in_specs=[_ANY] * n, out_specs=[_ANY] * n,
        out_shape=[jax.ShapeDtypeStruct((g.shape[0], 1) + g.shape[2:], g.dtype) for g in grads],
        scratch_shapes=[pltpu.SemaphoreType.DMA((n,)), pltpu.SemaphoreType.DMA((n,))],
    )(*grads)


def _pair_add(g4, r1, cj_arr, name, TR=512):
    B4, _, rh, C = g4.shape
    B = B4 // 4
    TR = rh if rh <= TR else _row_tile(rh, TR)

    def body(cj_ref, g_ref, r_ref, o16_ref, own_ref):
        s = g_ref[...] + r_ref[...]
        o16_ref[...] = s.astype(BF16)

        @pl.when(pl.program_id(2) == cj_ref[1])
        def _():
            own_ref[...] = s

    return pl.pallas_call(
        body, name=name,
        grid_spec=pltpu.PrefetchScalarGridSpec(
            num_scalar_prefetch=1, grid=(B, rh // TR, 4),
            in_specs=[pl.BlockSpec((None, None, TR, C), lambda b, t, p, cj: (b * 4 + p, cj[0], t, 0)),
                      pl.BlockSpec((None, None, TR, C), lambda b, t, p, cj: (b * 4 + p, 0, t, 0))],
            out_specs=[pl.BlockSpec((None, None, TR, C), lambda b, t, p, cj: (p, b, t, 0)),
                       pl.BlockSpec((None, TR, C), lambda b, t, p, cj: (b, t, 0))]),
        out_shape=[jax.ShapeDtypeStruct((4, B, rh, C), BF16), jax.ShapeDtypeStruct((B, rh, C), F32)],
        compiler_params=_params(("parallel", "parallel", "arbitrary")),
    )(cj_arr, g4, r1)


def _chip_exchange(parts, name):
    n = len(parts)

    def body(*refs):
        src, dst = refs[:n], refs[n:2 * n]
        send_sems, recv_sems = refs[2 * n:]
        j = 2 * lax.axis_index("x") + lax.axis_index("y")
        c = lax.axis_index("c")
        cps = []
        for a in range(n):
            for d in range(1, 4):
                to = (j + d) % 4
                cp = pltpu.make_async_remote_copy(
                    src_ref=src[a].at[to], dst_ref=dst[a].at[d - 1], send_sem=send_sems.at[a, d - 1],
                    recv_sem=recv_sems.at[a, d - 1], device_id=_chip_of(to, c), device_id_type=MESH)
                cp.start()
                cps.append(cp)
        for cp in cps:
            cp.wait_recv()
        for cp in cps:
            cp.wait_send()

    return pl.pallas_call(
        body, name=name, in_specs=[_ANY] * n, out_specs=[_ANY] * n,
        out_shape=[jax.ShapeDtypeStruct((3,) + p.shape[1:], p.dtype) for p in parts],
        scratch_shapes=[pltpu.SemaphoreType.DMA((n, 3)), pltpu.SemaphoreType.DMA((n, 3))],
    )(*parts)


def _chip_add(own, r2, cj_arr, into, first, name, TR=512):
    B, rh, C = own.shape
    TR = rh if rh <= TR else _row_tile(rh, TR)

    def body(cj_ref, p_ref, r_ref, _into_ref, o_ref):
        o_ref[...] = p_ref[...] + r_ref[0].astype(F32) + r_ref[1].astype(F32) + r_ref[2].astype(F32)

    return pl.pallas_call(
        body, name=name,
        grid_spec=pltpu.PrefetchScalarGridSpec(
            num_scalar_prefetch=1, grid=(B, rh // TR),
            in_specs=[pl.BlockSpec((None, TR, C), lambda b, t, cj: (b, t, 0)),
                      pl.BlockSpec((3, None, TR, C), lambda b, t, cj: (0, b, t, 0)),
                      _ANY],
            out_specs=pl.BlockSpec((None, None, TR, C), lambda b, t, cj: (first + b, cj[0], t, 0))),
        out_shape=jax.ShapeDtypeStruct(into.shape, F32),
        input_output_aliases={3: 0},
        compiler_params=_params(("parallel", "parallel")),
    )(cj_arr, own, r2, into)


def _pair_share(bufs, name):
    n = len(bufs)

    def body(*refs):
        buf = refs[n:2 * n]
        send_sems, recv_sems = refs[2 * n:]
        c = lax.axis_index("c")
        sib = (lax.axis_index("x"), lax.axis_index("y"), 1 - c)
        cps = []
        for a in range(n):
            cp = pltpu.make_async_remote_copy(
                src_ref=buf[a].at[:, c], dst_ref=buf[a].at[:, c], send_sem=send_sems.at[a],
                recv_sem=recv_sems.at[a], device_id=sib, device_id_type=MESH)
            cp.start()
            cps.append(cp)
        for a in range(n):
            pltpu.make_async_remote_copy(
                src_ref=buf[a].at[:, 1 - c], dst_ref=buf[a].at[:, 1 - c], send_sem=send_sems.at[a],
                recv_sem=recv_sems.at[a], device_id=sib, device_id_type=MESH).wait_recv()
        for cp in cps:
            cp.wait_send()

    return pl.pallas_call(
        body, name=name, in_specs=[_ANY] * n, out_specs=[_ANY] * n,
        out_shape=[jax.ShapeDtypeStruct(b.shape, b.dtype) for b in bufs],
        scratch_shapes=[pltpu.SemaphoreType.DMA((n,)), pltpu.SemaphoreType.DMA((n,))],
        input_output_aliases={a: a for a in range(n)},
    )(*bufs)


def _gather_all(flat, name):
    def body(src, dst, send_sems, recv_sems, loc_sem):
        me = 4 * lax.axis_index("x") + 2 * lax.axis_index("y") + lax.axis_index("c")
        loc = pltpu.make_async_copy(src, dst.at[me], loc_sem)
        loc.start()
        cps = []
        for d in range(1, 8):
            to = (me + d) % 8
            cp = pltpu.make_async_remote_copy(
                src_ref=src, dst_ref=dst.at[me], send_sem=send_sems.at[d - 1], recv_sem=recv_sems.at[d - 1],
                device_id=(to // 4, (to // 2) % 2, to % 2), device_id_type=MESH)
            cp.start()
            cps.append(cp)
        for d in range(1, 8):
            frm = (me + 8 - d) % 8
            pltpu.make_async_remote_copy(
                src_ref=src, dst_ref=dst.at[frm], send_sem=send_sems.at[d - 1], recv_sem=recv_sems.at[d - 1],
                device_id=(frm // 4, (frm // 2) % 2, frm % 2), device_id_type=MESH).wait_recv()
        for cp in cps:
            cp.wait_send()
        loc.wait()

    return pl.pallas_call(
        body, name=name, in_specs=[_ANY], out_specs=_ANY,
        out_shape=jax.ShapeDtypeStruct((8,) + flat.shape, flat.dtype),
        scratch_shapes=[pltpu.SemaphoreType.DMA((7,)), pltpu.SemaphoreType.DMA((7,)), pltpu.SemaphoreType.DMA],
    )(flat)


class _ReduceScatter:
    def __init__(self, n_layers, cj_arr):
        self.L, self.cj = n_layers, cj_arr
        self.total = None
        self.pending = None

    def _land(self, layer, owns, r2):
        if self.total is None:
            self.total = [lax.empty((self.L * o.shape[0], 2) + o.shape[1:], F32) for o in owns]
        self.total = [_chip_add(o, r, self.cj, t, layer * o.shape[0], "rs_chip_add")
                      for o, r, t in zip(owns, r2, self.total)]

    def add_layer(self, layer, grads):
        g4 = [g.reshape(g.shape[0] * 4, 2, g.shape[1] // 8, g.shape[2]) for g in grads]
        r1 = _pair_exchange(g4, "rs_pair_exchange")
        added = [_pair_add(g, r, self.cj, "rs_pair_add") for g, r in zip(g4, r1)]
        parts, owns = [p for p, _ in added], [o for _, o in added]
        if self.pending is not None:
            handle, p_layer, p_owns = self.pending
            self._land(p_layer, p_owns, _chip_wait(handle, owns[-1], "rs_chip_wait"))
            self.pending = None
        if layer > 0:
            handle = _chip_start(parts, "rs_chip_start")
            self.pending = (handle, layer, owns)
            return (handle[-1],)
        self._land(layer, owns, _chip_exchange(parts, "rs_chip_exchange"))
        return ()

    def result(self):
        full = _pair_share(self.total, "rs_pair_share")
        return [f.reshape(f.shape[0], f.shape[1] * f.shape[2], f.shape[3]) for f in full]


def _relu2_epi(acc):
    return acc, jnp.square(jnp.maximum(acc, 0.0))


def _relu2_bwd_epi(acc, u):
    return (acc * (2.0 * jnp.maximum(u, 0.0)),)


_GRAD_ORDER = ("winT", "wbT", "wout", "wq", "wkv", "woT", "wupT", "wdown")


def _forward_backward(x, mem, target, weights_of, P, grads_done, first_deps=()):
    L = P["g_norm"].shape[0]
    S, D = x.shape
    gn = lambda l, i: P["g_norm"][l, i][None]

    saved = []
    (h,) = _resnorm_fwd(x, None, None, gn(0, 0), "norm_in", deps=first_deps)
    xr = x
    for l in range(L):
        W = weights_of(l, xr)
        proj = _mm(h, W["winT"], "nt", "in_proj", b_pre=(0,), tn=768)
        small = (jnp.broadcast_to(P["sinks"][l][:, None], (8, BLK)), P["ws"][l], P["bs"][l][:, :, None],
                 P["gsgu"][l][None], P["wp"][l], P["ps"][l][None])
        br = _mix_fwd(proj, *small, "mix_fwd")
        pb = lax.empty((3, S, D), F32)
        for n in range(3):
            pb = _mm(br, W["wbT"], "nt", "branch_proj", a_pre=(n,), b_pre=(n,), into=pb, out_pre=(n,))
        merged = _merge_fwd(proj, pb, "merge_fwd")
        z = _mm(merged, W["wout"], "nn", "out_proj", b_pre=(0,))
        x1, hm = _resnorm_fwd(xr, z, gn(l, 1), gn(l, 2), "resnorm_fwd")
        qm = _mm(hm, W["wq"], "nn", "mem_q", b_pre=(0,))
        (memn,) = _resnorm_fwd(mem, None, None, P["g_mem"][l][None], "mem_norm")
        kv = _mm(memn, W["wkv"], "nn", "mem_kv", b_pre=(0,))
        om = _memattn_fwd(qm, kv, "memattn_fwd")
        ym = _mm(om, W["woT"], "nt", "mem_o", b_pre=(0,))
        x2, hf = _resnorm_fwd(x1, ym, gn(l, 3), gn(l, 4), "resnorm_fwd")
        u, a = _mm(hf, W["wupT"], "nt", "mlp_up", b_pre=(0,), out_dtypes=(F32, BF16), epi=_relu2_epi)
        yf = _mm(a, W["wdown"], "nn", "mlp_down", b_pre=(0,))
        saved.append(dict(W=W, x0=xr, h=h, proj=proj, small=small, br=br, pb=pb, merged=merged, z=z, x1=x1, hm=hm,
                          qm=qm, memn=memn, kv=kv, om=om, ym=ym, x2=x2, hf=hf, u=u, a=a, yf=yf))
        if l < L - 1:
            xr, h = _resnorm_fwd(x2, yf, gn(l, 5), gn(l + 1, 0), "resnorm_fwd")
    dres, loss = _final_fwd(saved[-1]["x2"], saved[-1]["yf"], gn(L - 1, 5), target, "loss_head")

    dgn = [[None] * 6 for _ in range(L)]
    dsmall = {k: [None] * L for k in ("g_mem", "sinks", "ws", "bs", "gsgu", "wp", "ps")}
    dh = None
    for l in reversed(range(L)):
        s = saved[l]
        W, G = s["W"], {}
        if l == L - 1:
            dx2, dyf, dgn[l][5] = _resnorm_bwd(s["x2"], s["yf"], gn(l, 5), None, dres, None, "resnorm_bwd_top")
        else:
            dx2, dyf, dgn[l][5], dgn[l + 1][0] = _resnorm_bwd(s["x2"], s["yf"], gn(l, 5), gn(l + 1, 0), dres, dh,
                                                              "resnorm_bwd", deps=deps)
        du = _mm(dyf, W["wdown"], "nt", "mlp_down_dx", b_pre=(0,), out_dtypes=(BF16,), extras=(s["u"],), epi=_relu2_bwd_epi)
        G["wdown"] = _mm(s["a"], dyf, "tn", "mlp_down_dw")[None]
        dhf = _mm(du, W["wupT"], "nn", "mlp_up_dx", b_pre=(0,))
        G["wupT"] = _mm(du, s["hf"], "tn", "mlp_up_dw")[None]
        dx1, dym, dgn[l][3], dgn[l][4] = _resnorm_bwd(s["x1"], s["ym"], gn(l, 3), gn(l, 4), dx2, dhf, "resnorm_bwd")
        dom = _mm(dym, W["woT"], "nn", "mem_o_dx", b_pre=(0,))
        G["woT"] = _mm(dym, s["om"], "tn", "mem_o_dw")[None]
        dqm, dkv = _memattn_bwd(s["qm"], s["kv"], dom, "memattn_bwd")
        dmemn = _mm(dkv, W["wkv"], "nt", "mem_kv_dx", b_pre=(0,))
        G["wkv"] = _mm(s["memn"], dkv, "tn", "mem_kv_dw")[None]
        _, dsmall["g_mem"][l] = _resnorm_bwd(mem, None, None, P["g_mem"][l][None], None, dmemn, "mem_norm_bwd")
        dhm = _mm(dqm, W["wq"], "nt", "mem_q_dx", b_pre=(0,))
        G["wq"] = _mm(s["hm"], dqm, "tn", "mem_q_dw")[None]
        dx0, dz, dgn[l][1], dgn[l][2] = _resnorm_bwd(s["x0"], s["z"], gn(l, 1), gn(l, 2), dx1, dhm, "resnorm_bwd")
        dmerged = _mm(dz, W["wout"], "nt", "out_proj_dx", b_pre=(0,))
        G["wout"] = _mm(s["merged"], dz, "tn", "out_proj_dw")[None]
        dproj, dpb = _merge_bwd(s["proj"], s["pb"], dmerged, "merge_bwd")
        dbr = lax.empty((3, S, 512), F32)
        G["wbT"] = lax.empty(W["wbT"].shape, F32)
        for n in range(3):
            dbr = _mm(dpb, W["wbT"], "nn", "branch_proj_dx", a_pre=(n,), b_pre=(n,), into=dbr, out_pre=(n,))
            G["wbT"] = _mm(dpb, s["br"], "tn", "branch_proj_dw", a_pre=(n,), b_pre=(n,), into=G["wbT"], out_pre=(n,))
        (dproj, dsmall["sinks"][l], dsmall["ws"][l], dsmall["bs"][l], dsmall["gsgu"][l], dsmall["wp"][l],
         dsmall["ps"][l]) = _mix_bwd(s["proj"], dbr, dproj, *s["small"], "mix_bwd")
        dh = _mm(dproj, W["winT"], "nn", "in_proj_dx", b_pre=(0,), tk=768)
        G["winT"] = _mm(dproj, s["h"], "tn", "in_proj_dw", tm=768)[None]
        deps = grads_done(l, [G[k] for k in _GRAD_ORDER])
        dres = dx0
    grad_x, dgn[0][0] = _resnorm_bwd(x, None, None, gn(0, 0), dres, dh, "norm_in_bwd")

    small_grads = dict(
        g_norm=jnp.stack([jnp.concatenate(row, axis=0) for row in dgn]),
        g_mem=jnp.concatenate(dsmall["g_mem"], axis=0),
        sinks=jnp.stack([d[:, 0] for d in dsmall["sinks"]]),
        ws=jnp.stack(dsmall["ws"]),
        bs=jnp.stack([d[:, :, 0] for d in dsmall["bs"]]),
        gsgu=jnp.concatenate(dsmall["gsgu"], axis=0),
        wp=jnp.stack(dsmall["wp"]),
        ps=jnp.concatenate(dsmall["ps"], axis=0),
    )
    return loss, grad_x, small_grads


_PACK_UNIT = 1024
_PACK_ROWS = 512


def _pack(arrays):
    flat, total = [], 0
    for a in arrays:
        f = a.reshape(-1)
        pad = (-f.shape[0]) % _PACK_UNIT
        flat.append(jnp.pad(f, (0, pad)) if pad else f)
        total += f.shape[0] + pad
    tail = (-total) % (_PACK_ROWS * 128)
    if tail:
        flat.append(jnp.zeros((tail,), flat[0].dtype))
    return jnp.concatenate(flat).reshape(-1, 128)


def _unpack(packed, like):
    flat, out, pos = packed.reshape(-1), [], 0
    for a in like:
        n = math.prod(a.shape)
        out.append(flat[pos:pos + n].reshape(a.shape))
        pos += n + (-n) % _PACK_UNIT
    return out


_BIG = ("w_in", "w_branch", "w_out", "w_q_mem", "w_kv_mem", "w_o_mem", "w_up", "w_down")
_SMALL = ("g_norm", "g_mem", "attn_sinks", "w_spatial", "b_spatial", "g_sgu", "w_pool", "pool_scale")
_WEIGHTS = ("g_norm", "g_mem", "w_in", "attn_sinks", "w_spatial", "b_spatial", "g_sgu", "w_pool", "pool_scale",
            "w_branch", "w_out", "w_q_mem", "w_kv_mem", "w_o_mem", "w_up", "w_down")


def _to_working(name, w):
    if name == "w_in":
        return jnp.swapaxes(w, 1, 2)
    if name == "w_branch":
        t = jnp.swapaxes(w, 2, 3)
        return t.reshape(t.shape[0] * 3, t.shape[2], t.shape[3])
    if name in ("w_o_mem", "w_up"):
        return jnp.swapaxes(w, 1, 2)
    return w


def _from_working(name, g):
    if name == "w_in":
        return jnp.swapaxes(g, 1, 2)
    if name == "w_branch":
        return jnp.swapaxes(g.reshape(g.shape[0] // 3, 3, g.shape[1], g.shape[2]), 2, 3)
    if name in ("w_o_mem", "w_up"):
        return jnp.swapaxes(g, 1, 2)
    return g


_WKEY = dict(w_in="winT", w_branch="wbT", w_out="wout", w_q_mem="wq", w_kv_mem="wkv", w_o_mem="woT",
             w_up="wupT", w_down="wdown")


def kernel(x, mem, g_norm, g_mem, w_in, attn_sinks, w_spatial, b_spatial, g_sgu, w_pool, pool_scale, w_branch, w_out, w_q_mem, w_kv_mem, w_o_mem, w_up, w_down, loss_target, m_g_norm, m_g_mem, m_w_in, m_attn_sinks, m_w_spatial, m_b_spatial, m_g_sgu, m_w_pool, m_pool_scale, m_w_branch, m_w_out, m_w_q_mem, m_w_kv_mem, m_w_o_mem, m_w_up, m_w_down, v_g_norm, v_g_mem, v_w_in, v_attn_sinks, v_w_spatial, v_b_spatial, v_g_sgu, v_w_pool, v_pool_scale, v_w_branch, v_w_out, v_w_q_mem, v_w_kv_mem, v_w_o_mem, v_w_up, v_w_down):
    w = dict(g_norm=g_norm, g_mem=g_mem, w_in=w_in, attn_sinks=attn_sinks, w_spatial=w_spatial, b_spatial=b_spatial,
             g_sgu=g_sgu, w_pool=w_pool, pool_scale=pool_scale, w_branch=w_branch, w_out=w_out, w_q_mem=w_q_mem,
             w_kv_mem=w_kv_mem, w_o_mem=w_o_mem, w_up=w_up, w_down=w_down)
    m = dict(g_norm=m_g_norm, g_mem=m_g_mem, w_in=m_w_in, attn_sinks=m_attn_sinks, w_spatial=m_w_spatial,
             b_spatial=m_b_spatial, g_sgu=m_g_sgu, w_pool=m_w_pool, pool_scale=m_pool_scale, w_branch=m_w_branch,
             w_out=m_w_out, w_q_mem=m_w_q_mem, w_kv_mem=m_w_kv_mem, w_o_mem=m_w_o_mem, w_up=m_w_up, w_down=m_w_down)
    v = dict(g_norm=v_g_norm, g_mem=v_g_mem, w_in=v_w_in, attn_sinks=v_attn_sinks, w_spatial=v_w_spatial,
             b_spatial=v_b_spatial, g_sgu=v_g_sgu, w_pool=v_w_pool, pool_scale=v_pool_scale, w_branch=v_w_branch,
             w_out=v_w_out, w_q_mem=v_w_q_mem, w_kv_mem=v_w_kv_mem, w_o_mem=v_w_o_mem, w_up=v_w_up, w_down=v_w_down)
    L = g_norm.shape[0]
    j = 2 * lax.axis_index("x") + lax.axis_index("y")
    c = lax.axis_index("c")
    j_arr = jnp.reshape(j, (1,)).astype(jnp.int32)
    cj_arr = jnp.stack([c, j]).astype(jnp.int32)

    gs = g_norm.shape[2]
    working = [_to_working(n, w[n]) for n in _BIG]
    per_layer = [wk.shape[0] // L for wk in working]

    def own_slabs(l):
        return [_own_slab(wk, BF16, j_arr, "own_slab", first=l * b, count=b) for wk, b in zip(working, per_layer)]

    first = _gather_weights(own_slabs(0) + [_own_slab(g_norm.reshape(1, L * 6 * gs // 128, 128), F32, j_arr,
                                                      "own_slab_norm")], "gather_weights")
    in_flight = {l: _gather_start(own_slabs(l), "gather_start") for l in range(1, L)}
    gn_full = jnp.transpose(first[-1].reshape(4, L * 6, gs), (1, 0, 2)).reshape(L, 6, 4 * gs)
    P = dict(g_norm=gn_full, g_mem=g_mem, sinks=attn_sinks, ws=w_spatial, bs=b_spatial, gsgu=g_sgu, wp=w_pool,
             ps=pool_scale)

    def weights_of(l, after):
        got = first[:-1] if l == 0 else _gather_forward(_gather_wait(in_flight[l], after, "gather_wait"), "gather_forward")
        return {k: g.reshape(g.shape[0], 8 * g.shape[3], g.shape[4]) for k, g in zip(_GRAD_ORDER, got)}

    rs = _ReduceScatter(L, cj_arr)
    loss_part, grad_x, sg = _forward_backward(x[0], mem[0], loss_target[0], weights_of, P, rs.add_layer,
                                              first_deps=[h[-1] for h in in_flight.values()])
    loss = lax.psum(loss_part[0, 0], ("x", "y", "c"))
    grads = {n: _from_working(n, g) for n, g in zip(_BIG, rs.result())}

    full_small = [sg["g_norm"], sg["g_mem"], sg["sinks"], sg["ws"], sg["bs"], sg["gsgu"], sg["wp"], sg["ps"]]
    packed = _pack(full_small)
    total = _sum_slots(_gather_all(packed, "gather_small_grads"), "sum_small_grads")
    for n, g in zip(_SMALL, _unpack(total, full_small)):
        grads[n] = lax.dynamic_slice_in_dim(g, j * g_norm.shape[2], g_norm.shape[2], axis=2) if n == "g_norm" else g

    delta, new_m, new_v = {}, {}, {}
    for n in _BIG:
        shp = w[n].shape
        two_d = lambda t: t.reshape(-1, shp[-1])
        d_, m_, v_ = _adamw(two_d(w[n]), two_d(grads[n]), two_d(m[n]), two_d(v[n]), "adamw")
        delta[n], new_m[n], new_v[n] = d_.reshape(shp), m_.reshape(shp), v_.reshape(shp)
    small_w = [w[n] for n in _SMALL]
    d_, m_, v_ = _adamw(_pack(small_w), _pack([grads[n] for n in _SMALL]), _pack([m[n] for n in _SMALL]),
                        _pack([v[n] for n in _SMALL]), "adamw_small")
    for n, dd, mm_, vv in zip(_SMALL, _unpack(d_, small_w), _unpack(m_, small_w), _unpack(v_, small_w)):
        delta[n], new_m[n], new_v[n] = dd, mm_, vv

    return (loss, grad_x[None], *[grads[n] for n in _WEIGHTS], *[delta[n] for n in _WEIGHTS],
            *[new_m[n] for n in _WEIGHTS], *[new_v[n] for n in _WEIGHTS])
```

```python
import functools
import math

import jax
import jax.numpy as jnp
from jax import lax
from jax.experimental import pallas as pl
from jax.experimental.pallas import tpu as pltpu

F32 = jnp.float32
BF16 = jnp.bfloat16
MESH = pl.DeviceIdType.MESH

EPS = 1e-6
NEG_INF = -1e30
BLK = 128
HALO = 16
POOL_WINDOWS = (2, 4, 8, 16)
ATT_SCALE = 1.0 / math.sqrt(64.0)
MEM_SCALE = 1.0 / math.sqrt(128.0)
C_Q, C_K, C_V, C_SU, C_SV, C_PC, C_GATE, C_END = 0, 512, 640, 768, 1280, 1792, 2304, 5376

ADAM_LR, ADAM_B1, ADAM_B2, ADAM_EPS, ADAM_WD, ADAM_STEP = 0.001, 0.9, 0.999, 1e-08, 0.01, 10

VMEM_LIMIT_BYTES = 56 * 1024 * 1024

_DIMS = {
    "nn": (((1,), (0,)), ((), ())),
    "nt": (((1,), (1,)), ((), ())),
    "tn": (((0,), (0,)), ((), ())),
}


def _dot(a, b, mode):
    return lax.dot_general(a, b, _DIMS[mode], preferred_element_type=F32)


def _params(semantics):
    return pltpu.CompilerParams(dimension_semantics=semantics, vmem_limit_bytes=VMEM_LIMIT_BYTES)


def _tile(dim, pref):
    if dim <= pref:
        return dim
    t = (pref // 128) * 128
    while t >= 128:
        if dim % t == 0:
            return t
        t -= 128
    raise ValueError(f"no tile for {dim}")


def _rms(x, g):
    return x * lax.rsqrt(jnp.mean(x * x, axis=-1, keepdims=True) + EPS) * g


def _mm(a, b, mode, name, *, out_dtypes=(F32,), a_pre=(), b_pre=(), into=None, out_pre=(),
        extras=(), epi=None, tm=1024, tn=1024, tk=1024):
    a2, b2 = a.shape[len(a_pre):], b.shape[len(b_pre):]
    if mode == "nn":
        (M, K), (K2, N) = a2, b2
    elif mode == "nt":
        (M, K), (N, K2) = a2, b2
    else:
        (K, M), (K2, N) = a2, b2
    assert K == K2, (a.shape, b.shape, mode)
    tm, tn, tk = _tile(M, tm), _tile(N, tn), _tile(K, tk)
    nk = K // tk
    na, nb_, no = len(a_pre), len(b_pre), len(out_pre)
    if mode == "tn":
        a_spec = pl.BlockSpec((None,) * na + (tk, tm), lambda i, j, k: a_pre + (k, i))
    else:
        a_spec = pl.BlockSpec((None,) * na + (tm, tk), lambda i, j, k: a_pre + (i, k))
    if mode == "nt":
        b_spec = pl.BlockSpec((None,) * nb_ + (tn, tk), lambda i, j, k: b_pre + (j, k))
    else:
        b_spec = pl.BlockSpec((None,) * nb_ + (tk, tn), lambda i, j, k: b_pre + (k, j))
    tile_spec = pl.BlockSpec((tm, tn), lambda i, j, k: (i, j))
    ne, nout = len(extras), len(out_dtypes)
    in_specs = [a_spec, b_spec] + [tile_spec] * ne
    operands = [a, b, *extras]
    aliases = {}
    if into is not None:
        assert nout == 1
        in_specs.append(pl.BlockSpec(memory_space=pl.ANY))
        operands.append(into)
        aliases = {len(operands) - 1: 0}
        out_shape = [jax.ShapeDtypeStruct(into.shape, into.dtype)]
        out_specs = [pl.BlockSpec((None,) * no + (tm, tn), lambda i, j, k: out_pre + (i, j))]
    else:
        out_shape = [jax.ShapeDtypeStruct((M, N), dt) for dt in out_dtypes]
        out_specs = [tile_spec] * nout

    def body(*refs):
        a_ref, b_ref = refs[0], refs[1]
        ex = refs[2:2 + ne]
        pos = 2 + ne + (1 if into is not None else 0)
        outs = refs[pos:pos + nout]
        acc_ref = refs[pos + nout] if nk > 1 else None

        def finish(acc):
            vals = epi(acc, *[e[...] for e in ex]) if epi is not None else (acc,)
            for o, v in zip(outs, vals):
                o[...] = v.astype(o.dtype)

        def prod():
            return _dot(a_ref[...].astype(BF16), b_ref[...].astype(BF16), mode)

        if nk == 1:
            finish(prod())
        else:
            k = pl.program_id(2)

            @pl.when(k == 0)
            def _():
                acc_ref[...] = jnp.zeros_like(acc_ref)

            acc_ref[...] += prod()

            @pl.when(k == nk - 1)
            def _():
                finish(acc_ref[...])

    res = pl.pallas_call(
        body, name=name, grid=(M // tm, N // tn, nk),
        in_specs=in_specs, out_specs=out_specs, out_shape=out_shape,
        scratch_shapes=[pltpu.VMEM((tm, tn), F32)] if nk > 1 else [],
        input_output_aliases=aliases,
        compiler_params=_params(("parallel", "parallel", "arbitrary")),
    )(*operands)
    return res[0] if nout == 1 else tuple(res)


def _resnorm_fn(has_post, has_pre):
    def f(*a):
        x, k = a[0], 1
        if has_post:
            x, k = x + _rms(a[1], a[2]), 3
        outs = [x]
        if has_pre:
            outs.append(_rms(x, a[k]))
        return tuple(outs)
    return f


def _row_spec(T, W):
    return pl.BlockSpec((T, W), lambda i: (i, 0))


def _par_spec(W):
    return pl.BlockSpec((1, W), lambda i: (0, 0))


def _resnorm_fwd(xr, y, gp, gq, name, T=512, deps=()):
    S, D = xr.shape
    T = min(T, S)
    has_post, has_pre = y is not None, gq is not None
    f = _resnorm_fn(has_post, has_pre)
    ins = [xr] + ([y, gp] if has_post else []) + ([gq] if has_pre else [])
    in_specs = [_row_spec(T, D)] + ([_row_spec(T, D), _par_spec(D)] if has_post else []) + ([_par_spec(D)] if has_pre else [])
    out_shape, out_specs = [], []
    if has_post:
        out_shape.append(jax.ShapeDtypeStruct((S, D), F32)); out_specs.append(_row_spec(T, D))
    if has_pre:
        out_shape.append(jax.ShapeDtypeStruct((S, D), BF16)); out_specs.append(_row_spec(T, D))
    n_in, n_dep = len(ins), len(deps)

    def body(*refs):
        vals = f(*[r[...] for r in refs[:n_in]])
        outs = list(refs[n_in + n_dep:])
        if has_post:
            outs.pop(0)[...] = vals[0]
        if has_pre:
            outs.pop(0)[...] = vals[1].astype(BF16)

    res = pl.pallas_call(body, name=name, grid=(S // T,),
                         in_specs=in_specs + [pl.BlockSpec(memory_space=pl.ANY)] * n_dep, out_specs=out_specs,
                         out_shape=out_shape, compiler_params=_params(("parallel",)))(*ins, *deps)
    return tuple(res)


def _resnorm_bwd(xr, y, gp, gq, dres, dh, name, T=512, deps=()):
    S, D = xr.shape
    T = min(T, S)
    has_post, has_pre, has_res = y is not None, gq is not None, dres is not None
    f = _resnorm_fn(has_post, has_pre)
    ins = [xr] + ([y, gp] if has_post else []) + ([gq] if has_pre else [])
    in_specs = [_row_spec(T, D)] + ([_row_spec(T, D), _par_spec(D)] if has_post else []) + ([_par_spec(D)] if has_pre else [])
    n_prim = len(ins)
    if has_res:
        ins.append(dres); in_specs.append(_row_spec(T, D))
    if has_pre:
        ins.append(dh); in_specs.append(_row_spec(T, D))
    n_in, n_dep = len(ins), len(deps)
    out_shape = [jax.ShapeDtypeStruct((S, D), F32)]
    out_specs = [_row_spec(T, D)]
    if has_post:
        out_shape += [jax.ShapeDtypeStruct((S, D), BF16), jax.ShapeDtypeStruct((1, D), F32)]
        out_specs += [_row_spec(T, D), _par_spec(D)]
    if has_pre:
        out_shape.append(jax.ShapeDtypeStruct((1, D), F32)); out_specs.append(_par_spec(D))

    def body(*refs):
        i = pl.program_id(0)
        prim = [r[...] for r in refs[:n_prim]]
        rest = list(refs[n_prim:n_in])
        ct_x = rest.pop(0)[...] if has_res else jnp.zeros((T, D), F32)
        cts = [ct_x]
        if has_pre:
            cts.append(rest.pop(0)[...].astype(F32))
        _, vjp = jax.vjp(f, *prim)
        grads = list(vjp(tuple(cts)))
        outs = list(refs[n_in + n_dep:])
        outs.pop(0)[...] = grads.pop(0)
        acc = []
        if has_post:
            outs.pop(0)[...] = grads.pop(0).astype(BF16)
            acc.append((outs.pop(0), grads.pop(0)))
        if has_pre:
            acc.append((outs.pop(0), grads.pop(0)))

        @pl.when(i == 0)
        def _():
            for o, _g in acc:
                o[...] = jnp.zeros_like(o)

        for o, g in acc:
            o[...] += g

    res = pl.pallas_call(body, name=name, grid=(S // T,),
                         in_specs=in_specs + [pl.BlockSpec(memory_space=pl.ANY)] * n_dep, out_specs=out_specs,
                         out_shape=out_shape, compiler_params=_params(("arbitrary",)))(*ins, *deps)
    return tuple(res)


def _final_fwd(xr, y, gp, target, name, T=512):
    S, D = xr.shape
    T = min(T, S)

    def body(x_ref, y_ref, g_ref, t_ref, dy_ref, loss_ref):
        i = pl.program_id(0)
        e = x_ref[...] + _rms(y_ref[...], g_ref[...]) - t_ref[...]
        dy_ref[...] = e / D

        @pl.when(i == 0)
        def _():
            loss_ref[...] = jnp.zeros_like(loss_ref)

        loss_ref[...] += 0.5 * jnp.sum(jnp.sum(e * e, axis=-1, keepdims=True) / D, axis=0, keepdims=True)

    return pl.pallas_call(
        body, name=name, grid=(S // T,),
        in_specs=[_row_spec(T, D), _row_spec(T, D), _par_spec(D), _row_spec(T, D)],
        out_specs=[_row_spec(T, D), pl.BlockSpec((1, 128), lambda i: (0, 0))],
        out_shape=[jax.ShapeDtypeStruct((S, D), F32), jax.ShapeDtypeStruct((1, 128), F32)],
        compiler_params=_params(("arbitrary",)))(xr, y, gp, target)


def _lane_lo():
    return lax.broadcasted_iota(jnp.int32, (1, BLK), 1) < 64


def _att_mask(not_first):
    r = lax.broadcasted_iota(jnp.int32, (BLK, 2 * BLK), 0)
    c = lax.broadcasted_iota(jnp.int32, (BLK, 2 * BLK), 1)
    qc, kc = 2 + r // 64, c // 64
    return (kc <= qc) & (kc >= qc - 2) & (not_first | (c >= BLK))


def _softmax_sink(s, sk):
    m = jnp.maximum(jnp.max(s, axis=-1, keepdims=True), sk)
    e = jnp.exp(s - m)
    es = jnp.exp(sk - m)
    z = jnp.sum(e, axis=-1, keepdims=True) + es
    return e / z, es / z


def _att_operands(cur, kvp, t, lo):
    h = t // 2
    qt = cur[:, C_Q + BLK * t:C_Q + BLK * (t + 1)]
    q_lo = jnp.where(lo, qt, 0.0).astype(BF16)
    q_hi = jnp.where(lo, 0.0, qt).astype(BF16)
    kband = jnp.concatenate([kvp[:, 0:BLK], cur[:, C_K:C_K + BLK]], axis=0)
    vband = jnp.concatenate([kvp[:, BLK:2 * BLK], cur[:, C_V:C_V + BLK]], axis=0)
    kroll = pltpu.roll(kband, 64, 1)
    vroll = pltpu.roll(vband, 64, 1)
    ka, kb = (kband, kroll) if h == 0 else (kroll, kband)
    va = jnp.where(lo, vband if h == 0 else vroll, 0.0)
    vb = jnp.where(lo, 0.0, vroll if h == 0 else vband)
    return q_lo, q_hi, ka.astype(BF16), kb.astype(BF16), va.astype(BF16), vb.astype(BF16)


def _sgu_mask():
    r = lax.broadcasted_iota(jnp.int32, (BLK, BLK), 0)
    c = lax.broadcasted_iota(jnp.int32, (BLK, BLK), 1)
    return (c // 64) <= (r // 64)


def _pool_cnt(blk, w):
    t = blk * BLK + lax.broadcasted_iota(jnp.int32, (BLK, 1), 0)
    return jnp.minimum(t + 1, w).astype(F32)


def _mix_in_specs(nb, rev):
    def b(i):
        return nb - 1 - i if rev else i
    return [
        pl.BlockSpec((BLK, C_GATE), lambda i: (b(i), 0)),
        pl.BlockSpec((BLK, 2 * BLK), lambda i: (jnp.maximum(b(i) - 1, 0), C_K // (2 * BLK))),
        pl.BlockSpec((HALO, C_GATE), lambda i: (jnp.maximum(b(i) * (BLK // HALO) - 1, 0), 0)),
        pl.BlockSpec((8, BLK), lambda i: (0, 0)),
        pl.BlockSpec((4, BLK, BLK), lambda i: (0, 0, 0)),
        pl.BlockSpec((4, BLK, 1), lambda i: (0, 0, 0)),
        pl.BlockSpec((1, 512), lambda i: (0, 0)),
        pl.BlockSpec((4, BLK, BLK), lambda i: (0, 0, 0)),
        pl.BlockSpec((1, 512), lambda i: (0, 0)),
    ]


def _mix_fwd(proj, sinks_b, ws, bs3, gsgu, wp, ps, name):
    S = proj.shape[0]
    nb = S // BLK

    def body(cur_ref, kvp_ref, pcp_ref, sk_ref, ws_ref, bs_ref, gs_ref, wp_ref, ps_ref, br_ref, ext_ref):
        i = pl.program_id(0)
        not_first = i > 0
        lo = _lane_lo()
        cur, kvp = cur_ref[...], kvp_ref[...]
        mask = _att_mask(not_first)
        for t in range(4):
            q_lo, q_hi, ka, kb, va, vb = _att_operands(cur, kvp, t, lo)
            s_lo = jnp.where(mask, _dot(q_lo, ka, "nt") * ATT_SCALE, NEG_INF)
            s_hi = jnp.where(mask, _dot(q_hi, kb, "nt") * ATT_SCALE, NEG_INF)
            p_lo, _ = _softmax_sink(s_lo, sk_ref[2 * t:2 * t + 1, 0:1])
            p_hi, _ = _softmax_sink(s_hi, sk_ref[2 * t + 1:2 * t + 2, 0:1])
            o = _dot(p_lo.astype(BF16), va, "nn") + _dot(p_hi.astype(BF16), vb, "nn")
            br_ref[0, :, BLK * t:BLK * (t + 1)] = o.astype(BF16)
        gu = jax.nn.gelu(cur[:, C_SU:C_SV])
        vn = _rms(jax.nn.gelu(cur[:, C_SV:C_PC]), gs_ref[...]).astype(BF16)
        wmask = _sgu_mask()
        for g in range(4):
            wm = jnp.where(wmask, ws_ref[g], 0.0).astype(BF16)
            sp = _dot(wm, vn[:, BLK * g:BLK * (g + 1)], "nn") + bs_ref[g]
            br_ref[1, :, BLK * g:BLK * (g + 1)] = (gu[:, BLK * g:BLK * (g + 1)] * sp).astype(BF16)
        c = cur[:, C_PC:C_GATE]
        ext_ref[0:HALO, :] = jnp.where(not_first, pcp_ref[:, C_PC:C_GATE], 0.0)
        ext_ref[HALO:HALO + BLK, :] = c
        for g, w in enumerate(POOL_WINDOWS):
            sl = slice(BLK * g, BLK * (g + 1))
            acc = ext_ref[HALO:HALO + BLK, sl]
            for k in range(1, w):
                acc = acc + ext_ref[HALO - k:HALO - k + BLK, sl]
            pooled = acc / _pool_cnt(i, w) - c[:, sl]
            mixed = _dot(pooled.astype(BF16), wp_ref[g].astype(BF16), "nn")
            br_ref[2, :, sl] = (mixed * ps_ref[:, sl]).astype(BF16)

    return pl.pallas_call(
        body, name=name, grid=(nb,),
        in_specs=_mix_in_specs(nb, False),
        out_specs=pl.BlockSpec((3, BLK, 512), lambda i: (0, i, 0)),
        out_shape=jax.ShapeDtypeStruct((3, S, 512), BF16),
        scratch_shapes=[pltpu.VMEM((HALO + BLK, 512), F32)],
        compiler_params=_params(("parallel",)),
    )(proj, proj, proj, sinks_b, ws, bs3, gsgu, wp, ps)


def _mix_bwd(proj, dbr, dproj, sinks_b, ws, bs3, gsgu, wp, ps, name):
    S = proj.shape[0]
    nb = S // BLK

    def body(cur_ref, kvp_ref, pcp_ref, sk_ref, ws_ref, bs_ref, gs_ref, wp_ref, ps_ref, dbr_ref, _dproj_in,
             dp_ref, dsk_ref, dws_ref, dbs_ref, dgs_ref, dwp_ref, dps_ref,
             ext_ref, z_ref, ckv_ref, cpc_ref):
        i = pl.program_id(0)
        blk = nb - 1 - i
        not_first = blk > 0
        lo = _lane_lo()

        @pl.when(i == 0)
        def _():
            for r in (dsk_ref, dws_ref, dbs_ref, dgs_ref, dwp_ref, dps_ref, ckv_ref, cpc_ref, z_ref):
                r[...] = jnp.zeros_like(r)

        cur, kvp = cur_ref[...], kvp_ref[...]
        mask = _att_mask(not_first)
        dk_band = jnp.zeros((2 * BLK, BLK), F32)
        dk_roll = jnp.zeros((2 * BLK, BLK), F32)
        dv_band = jnp.zeros((2 * BLK, BLK), F32)
        dv_roll = jnp.zeros((2 * BLK, BLK), F32)
        for t in range(4):
            h = t // 2
            q_lo, q_hi, ka, kb, va, vb = _att_operands(cur, kvp, t, lo)
            do = dbr_ref[0, :, BLK * t:BLK * (t + 1)].astype(BF16)
            dq = jnp.zeros((BLK, BLK), F32)
            for half, (qm, km, vm) in enumerate(((q_lo, ka, va), (q_hi, kb, vb))):
                sk = sk_ref[2 * t + half:2 * t + half + 1, 0:1]
                s = jnp.where(mask, _dot(qm, km, "nt") * ATT_SCALE, NEG_INF)
                p, p_sink = _softmax_sink(s, sk)
                dp = _dot(do, vm, "nt")
                rs = jnp.sum(p * dp, axis=-1, keepdims=True)
                ds = (p * (dp - rs) * ATT_SCALE).astype(BF16)
                dsk_ref[2 * t + half:2 * t + half + 1, :] += jnp.broadcast_to(
                    -jnp.sum(p_sink * rs, axis=0, keepdims=True), (1, BLK))
                dvm = _dot(p.astype(BF16), do, "tn")
                dkm = _dot(ds, qm, "tn")
                dqm = _dot(ds, km, "nn")
                if half == 0:
                    dq = dq + jnp.where(lo, dqm, 0.0)
                    dvm = jnp.where(lo, dvm, 0.0)
                else:
                    dq = dq + jnp.where(lo, 0.0, dqm)
                    dvm = jnp.where(lo, 0.0, dvm)
                if (h == 0) == (half == 0):
                    dk_band, dv_band = dk_band + dkm, dv_band + dvm
                else:
                    dk_roll, dv_roll = dk_roll + dkm, dv_roll + dvm
            dp_ref[:, C_Q + BLK * t:C_Q + BLK * (t + 1)] = dq.astype(BF16)
        dk = dk_band + pltpu.roll(dk_roll, 64, 1)
        dv = dv_band + pltpu.roll(dv_roll, 64, 1)
        dp_ref[:, C_K:C_K + BLK] = (dk[BLK:] + ckv_ref[:, 0:BLK]).astype(BF16)
        dp_ref[:, C_V:C_V + BLK] = (dv[BLK:] + ckv_ref[:, BLK:]).astype(BF16)
        ckv_ref[:, 0:BLK] = dk[:BLK]
        ckv_ref[:, BLK:] = dv[:BLK]
        su, sv = cur[:, C_SU:C_SV], cur[:, C_SV:C_PC]
        gu, vjp_u = jax.vjp(jax.nn.gelu, su)
        vn, vjp_v = jax.vjp(lambda a, g: _rms(jax.nn.gelu(a), g), sv, gs_ref[...])
        vn16 = vn.astype(BF16)
        wmask = _sgu_mask()
        dgu, dvn = [], []
        for g in range(4):
            sl = slice(BLK * g, BLK * (g + 1))
            wm = jnp.where(wmask, ws_ref[g], 0.0).astype(BF16)
            sp = _dot(wm, vn16[:, sl], "nn") + bs_ref[g]
            dyb = dbr_ref[1, :, sl]
            dgu.append(dyb * sp)
            dsp = dyb * gu[:, sl]
            dsp16 = dsp.astype(BF16)
            dvn.append(_dot(wm, dsp16, "tn"))
            dws_ref[g] += jnp.where(wmask, _dot(dsp16, vn16[:, sl], "nt"), 0.0)
            dbs_ref[g] += jnp.sum(dsp, axis=1, keepdims=True)
        (dsu,) = vjp_u(jnp.concatenate(dgu, axis=1))
        dsv, dgs = vjp_v(jnp.concatenate(dvn, axis=1))
        dp_ref[:, C_SU:C_SV] = dsu.astype(BF16)
        dp_ref[:, C_SV:C_PC] = dsv.astype(BF16)
        dgs_ref[...] += dgs
        c = cur[:, C_PC:C_GATE]
        ext_ref[0:HALO, :] = jnp.where(not_first, pcp_ref[:, C_PC:C_GATE], 0.0)
        ext_ref[HALO:HALO + BLK, :] = c
        for g, w in enumerate(POOL_WINDOWS):
            sl = slice(BLK * g, BLK * (g + 1))
            acc = ext_ref[HALO:HALO + BLK, sl]
            for k in range(1, w):
                acc = acc + ext_ref[HALO - k:HALO - k + BLK, sl]
            cnt = _pool_cnt(blk, w)
            pooled16 = (acc / cnt - c[:, sl]).astype(BF16)
            wp16 = wp_ref[g].astype(BF16)
            mixed = _dot(pooled16, wp16, "nn")
            dyc = dbr_ref[2, :, sl]
            dps_ref[:, sl] += jnp.sum(dyc * mixed, axis=0, keepdims=True)
            dmixed16 = (dyc * ps_ref[:, sl]).astype(BF16)
            dwp_ref[g] += _dot(pooled16, dmixed16, "tn")
            dpooled = _dot(dmixed16, wp16, "nt")
            z_ref[HALO:HALO + BLK, sl] = dpooled / cnt
            dext = z_ref[0:HALO + BLK, sl]
            for k in range(1, w):
                dext = dext + z_ref[k:k + HALO + BLK, sl]
            dp_ref[:, C_PC + BLK * g:C_PC + BLK * (g + 1)] = (
                dext[HALO:] - dpooled + jnp.concatenate([jnp.zeros((BLK - HALO, BLK), F32), cpc_ref[:, sl]], axis=0)
            ).astype(BF16)
            cpc_ref[:, sl] = dext[:HALO]

    n_in = 11
    small = [jax.ShapeDtypeStruct((8, BLK), F32), jax.ShapeDtypeStruct((4, BLK, BLK), F32),
             jax.ShapeDtypeStruct((4, BLK, 1), F32), jax.ShapeDtypeStruct((1, 512), F32),
             jax.ShapeDtypeStruct((4, BLK, BLK), F32), jax.ShapeDtypeStruct((1, 512), F32)]
    small_specs = [pl.BlockSpec((8, BLK), lambda i: (0, 0)), pl.BlockSpec((4, BLK, BLK), lambda i: (0, 0, 0)),
                   pl.BlockSpec((4, BLK, 1), lambda i: (0, 0, 0)), pl.BlockSpec((1, 512), lambda i: (0, 0)),
                   pl.BlockSpec((4, BLK, BLK), lambda i: (0, 0, 0)), pl.BlockSpec((1, 512), lambda i: (0, 0))]
    res = pl.pallas_call(
        body, name=name, grid=(nb,),
        in_specs=_mix_in_specs(nb, True) + [
            pl.BlockSpec((3, BLK, 512), lambda i: (0, nb - 1 - i, 0)),
            pl.BlockSpec(memory_space=pl.ANY)],
        out_specs=[pl.BlockSpec((BLK, C_GATE), lambda i: (nb - 1 - i, 0))] + small_specs,
        out_shape=[jax.ShapeDtypeStruct(dproj.shape, dproj.dtype)] + small,
        scratch_shapes=[pltpu.VMEM((HALO + BLK, 512), F32), pltpu.VMEM((2 * HALO + BLK, 512), F32),
                        pltpu.VMEM((BLK, 2 * BLK), F32), pltpu.VMEM((HALO, 512), F32)],
        input_output_aliases={n_in - 1: 0},
        compiler_params=_params(("arbitrary",)),
    )(proj, proj, proj, sinks_b, ws, bs3, gsgu, wp, ps, dbr, dproj)
    return tuple(res)


_GW = 256


def _merge_fwd(proj, pb, name, T=1024):
    S, D = pb.shape[1], pb.shape[2]
    T = min(T, S)

    def body(gate_ref, pb_ref, out_ref, acc_ref):
        n = pl.program_id(2)

        @pl.when(n == 0)
        def _():
            acc_ref[...] = jnp.zeros_like(acc_ref)

        acc_ref[...] += jax.nn.sigmoid(gate_ref[...]) * pb_ref[...]

        @pl.when(n == 2)
        def _():
            out_ref[...] = acc_ref[...].astype(BF16)

    return pl.pallas_call(
        body, name=name, grid=(S // T, D // _GW, 3),
        in_specs=[pl.BlockSpec((T, _GW), lambda i, j, n: (i, C_GATE // _GW + n * (D // _GW) + j)),
                  pl.BlockSpec((None, T, _GW), lambda i, j, n: (n, i, j))],
        out_specs=pl.BlockSpec((T, _GW), lambda i, j, n: (i, j)),
        out_shape=jax.ShapeDtypeStruct((S, D), BF16),
        scratch_shapes=[pltpu.VMEM((T, _GW), F32)],
        compiler_params=_params(("parallel", "parallel", "arbitrary")),
    )(proj, pb)


def _merge_bwd(proj, pb, dmerged, name, T=1024):
    S, D = pb.shape[1], pb.shape[2]
    T = min(T, S)

    def body(gate_ref, pb_ref, dm_ref, dgate_ref, dpb_ref):
        sg = jax.nn.sigmoid(gate_ref[...])
        dm = dm_ref[...]
        dpb_ref[...] = (dm * sg).astype(BF16)
        dgate_ref[...] = (dm * pb_ref[...] * sg * (1.0 - sg)).astype(BF16)

    gate_map = lambda i, n, j: (i, C_GATE // _GW + n * (D // _GW) + j)
    return pl.pallas_call(
        body, name=name, grid=(S // T, 3, D // _GW),
        in_specs=[pl.BlockSpec((T, _GW), gate_map),
                  pl.BlockSpec((None, T, _GW), lambda i, n, j: (n, i, j)),
                  pl.BlockSpec((T, _GW), lambda i, n, j: (i, j))],
        out_specs=[pl.BlockSpec((T, _GW), gate_map),
                   pl.BlockSpec((None, T, _GW), lambda i, n, j: (n, i, j))],
        out_shape=[jax.ShapeDtypeStruct((S, C_END), BF16), jax.ShapeDtypeStruct((3, S, D), BF16)],
        compiler_params=_params(("parallel", "parallel", "parallel")),
    )(proj, pb, dmerged)


def _memattn_fwd(qm, kv, name, T=512):
    S, NM = qm.shape[0], kv.shape[0]
    T = min(T, S)

    def body(q_ref, kv_ref, o_ref):
        for h in range(4):
            sl = slice(128 * h, 128 * (h + 1))
            k = kv_ref[:, sl].astype(BF16)
            v = kv_ref[:, 512 + 128 * h:512 + 128 * (h + 1)].astype(BF16)
            s = _dot(q_ref[:, sl].astype(BF16), k, "nt") * MEM_SCALE
            p = jax.nn.softmax(s, axis=-1)
            o_ref[:, sl] = _dot(p.astype(BF16), v, "nn").astype(BF16)

    return pl.pallas_call(
        body, name=name, grid=(S // T,),
        in_specs=[_row_spec(T, 512), pl.BlockSpec((NM, 1024), lambda i: (0, 0))],
        out_specs=_row_spec(T, 512), out_shape=jax.ShapeDtypeStruct((S, 512), BF16),
        compiler_params=_params(("parallel",)))(qm, kv)


def _memattn_bwd(qm, kv, dom, name, T=512):
    S, NM = qm.shape[0], kv.shape[0]
    T = min(T, S)

    def body(q_ref, kv_ref, do_ref, dq_ref, dkv_ref):
        i = pl.program_id(0)

        @pl.when(i == 0)
        def _():
            dkv_ref[...] = jnp.zeros_like(dkv_ref)

        for h in range(4):
            sl = slice(128 * h, 128 * (h + 1))
            sv_ = slice(512 + 128 * h, 512 + 128 * (h + 1))
            q = q_ref[:, sl].astype(BF16)
            k = kv_ref[:, sl].astype(BF16)
            v = kv_ref[:, sv_].astype(BF16)
            do = do_ref[:, sl].astype(BF16)
            p = jax.nn.softmax(_dot(q, k, "nt") * MEM_SCALE, axis=-1)
            dp = _dot(do, v, "nt")
            ds = (p * (dp - jnp.sum(p * dp, axis=-1, keepdims=True)) * MEM_SCALE).astype(BF16)
            dq_ref[:, sl] = _dot(ds, k, "nn").astype(BF16)
            dkv_ref[:, sl] += _dot(ds, q, "tn")
            dkv_ref[:, sv_] += _dot(p.astype(BF16), do, "tn")

    return pl.pallas_call(
        body, name=name, grid=(S // T,),
        in_specs=[_row_spec(T, 512), pl.BlockSpec((NM, 1024), lambda i: (0, 0)), _row_spec(T, 512)],
        out_specs=[_row_spec(T, 512), pl.BlockSpec((NM, 1024), lambda i: (0, 0))],
        out_shape=[jax.ShapeDtypeStruct((S, 512), BF16), jax.ShapeDtypeStruct((NM, 1024), F32)],
        compiler_params=_params(("arbitrary",)))(qm, kv, dom)


def _adamw(w, g, m, v, name, TR=512):
    R, C = w.shape
    TR = R if R <= TR else _row_tile(R, TR)
    c1 = 1.0 - ADAM_B1 ** ADAM_STEP
    c2 = 1.0 - ADAM_B2 ** ADAM_STEP

    def body(w_ref, g_ref, m_ref, v_ref, d_ref, nm_ref, nv_ref):
        gv = g_ref[...]
        nm = ADAM_B1 * m_ref[...] + (1.0 - ADAM_B1) * gv
        nv = ADAM_B2 * v_ref[...] + (1.0 - ADAM_B2) * jnp.square(gv)
        d_ref[...] = -ADAM_LR * ((nm / c1) / (jnp.sqrt(nv / c2) + ADAM_EPS) + ADAM_WD * w_ref[...])
        nm_ref[...] = nm
        nv_ref[...] = nv

    spec = pl.BlockSpec((TR, C), lambda i: (i, 0))
    return pl.pallas_call(
        body, name=name, grid=(R // TR,), in_specs=[spec] * 4, out_specs=[spec] * 3,
        out_shape=[jax.ShapeDtypeStruct((R, C), F32)] * 3,
        compiler_params=_params(("parallel",)))(w, g, m, v)


def _row_tile(R, pref):
    t = (pref // 8) * 8
    while t >= 8:
        if R % t == 0:
            return t
        t -= 8
    raise ValueError(f"no row tile for {R}")


def _sum_slots(stack, name, TR=512):
    n, R, C = stack.shape
    TR = R if R <= TR else _row_tile(R, TR)

    def body(s_ref, o_ref):
        acc = s_ref[0]
        for k in range(1, n):
            acc = acc + s_ref[k]
        o_ref[...] = acc

    return pl.pallas_call(
        body, name=name, grid=(R // TR,),
        in_specs=[pl.BlockSpec((n, TR, C), lambda i: (0, i, 0))],
        out_specs=pl.BlockSpec((TR, C), lambda i: (i, 0)),
        out_shape=jax.ShapeDtypeStruct((R, C), F32),
        compiler_params=_params(("parallel",)))(stack)


_ANY = pl.BlockSpec(memory_space=pl.ANY)


def _chip_of(j, c):
    return (j // 2, j % 2, c)


def _own_slab(shard, dtype, j_arr, name, first=0, count=None, plus=None, deps=(), TR=512):
    N, r, C = shard.shape
    B = N if count is None else count
    rh = r // 2
    TR = rh if rh <= TR else _row_tile(rh, TR)
    nt = rh // TR
    ins = [shard] if plus is None else [shard, plus]

    def body(j_ref, *refs):
        val = refs[0][...] if plus is None else refs[0][...] + refs[1][...]
        refs[-1][...] = val.astype(refs[-1].dtype)

    return pl.pallas_call(
        body, name=name,
        grid_spec=pltpu.PrefetchScalarGridSpec(
            num_scalar_prefetch=1, grid=(B, 2, nt),
            in_specs=[pl.BlockSpec((None, TR, C), lambda b, h, t, jr: (first + b, h * nt + t, 0))] * len(ins)
            + [_ANY] * len(deps),
            out_specs=pl.BlockSpec((None, None, None, TR, C), lambda b, h, t, jr: (b, jr[0], h, t, 0))),
        out_shape=jax.ShapeDtypeStruct((B, 4, 2, rh, C), dtype),
        compiler_params=_params(("parallel", "parallel", "parallel")),
    )(j_arr, *ins, *deps)


def _gather_weights(bufs, name):
    n = len(bufs)

    def body(*refs):
        buf = refs[n:2 * n]
        send_sems, recv_sems, fsend_sems, frecv_sems = refs[2 * n:]
        x, y, c = lax.axis_index("x"), lax.axis_index("y"), lax.axis_index("c")
        j = 2 * x + y
        sib = (x, y, 1 - c)
        sends = []
        for d in range(1, 4):
            for a in range(n):
                cp = pltpu.make_async_remote_copy(
                    src_ref=buf[a].at[:, j, c], dst_ref=buf[a].at[:, j, c], send_sem=send_sems.at[a, d - 1],
                    recv_sem=recv_sems.at[a, d - 1], device_id=_chip_of((j + d) % 4, c), device_id_type=MESH)
                cp.start()
                sends.append(cp)
        for d in range(1, 4):
            frm = (j + 4 - d) % 4
            for a in range(n):
                pltpu.make_async_remote_copy(
                    src_ref=buf[a].at[:, frm, c], dst_ref=buf[a].at[:, frm, c], send_sem=send_sems.at[a, d - 1],
                    recv_sem=recv_sems.at[a, d - 1], device_id=_chip_of(frm, c), device_id_type=MESH).wait_recv()
                cp = pltpu.make_async_remote_copy(
                    src_ref=buf[a].at[:, frm, c], dst_ref=buf[a].at[:, frm, c], send_sem=fsend_sems.at[a, d - 1],
                    recv_sem=frecv_sems.at[a, d - 1], device_id=sib, device_id_type=MESH)
                cp.start()
                sends.append(cp)
        for d in range(1, 4):
            frm = (j + 4 - d) % 4
            for a in range(n):
                pltpu.make_async_remote_copy(
                    src_ref=buf[a].at[:, frm, 1 - c], dst_ref=buf[a].at[:, frm, 1 - c], send_sem=fsend_sems.at[a, d - 1],
                    recv_sem=frecv_sems.at[a, d - 1], device_id=sib, device_id_type=MESH).wait_recv()
        for cp in sends:
            cp.wait_send()

    return pl.pallas_call(
        body, name=name,
        in_specs=[_ANY] * n, out_specs=[_ANY] * n,
        out_shape=[jax.ShapeDtypeStruct(b.shape, b.dtype) for b in bufs],
        scratch_shapes=[pltpu.SemaphoreType.DMA((n, 3))] * 4,
        input_output_aliases={a: a for a in range(n)},
    )(*bufs)


_HBM = pl.BlockSpec(memory_space=pltpu.HBM)
_SEM = pl.BlockSpec(memory_space=pltpu.SEMAPHORE)
_DATAFLOW = pltpu.SideEffectType.DATAFLOW_SIDE_EFFECTING


def _in_hbm(arrays):
    return [pltpu.with_memory_space_constraint(a, pltpu.HBM) for a in arrays]


def _gather_start(bufs, name):
    n = len(bufs)

    def body(*refs):
        buf = refs[:n]
        send_sems, recv_sems = refs[n], refs[n + 1]
        token = refs[2 * n + 2]
        c = lax.axis_index("c")
        j = 2 * lax.axis_index("x") + lax.axis_index("y")
        for d in range(1, 4):
            for a in range(n):
                pltpu.make_async_remote_copy(
                    src_ref=buf[a].at[:, j, c], dst_ref=buf[a].at[:, j, c], send_sem=send_sems.at[3 * a + d - 1],
                    recv_sem=recv_sems.at[3 * a + d - 1], device_id=_chip_of((j + d) % 4, c), device_id_type=MESH).start()
        token[...] = jnp.zeros_like(token)

    return pl.pallas_call(
        body, name=name,
        out_shape=(pltpu.SemaphoreType.DMA((3 * n,)), pltpu.SemaphoreType.DMA((3 * n,)),
                   *[pltpu.HBM(b.shape, b.dtype) for b in bufs], jax.ShapeDtypeStruct((8, 128), F32)),
        in_specs=[_HBM] * n,
        out_specs=(_SEM, _SEM, *[_HBM] * n, pl.BlockSpec(memory_space=pltpu.VMEM)),
        input_output_aliases={a: 2 + a for a in range(n)},
        compiler_params=pltpu.CompilerParams(has_side_effects=_DATAFLOW),
    )(*_in_hbm(bufs))


def _gather_wait(handle, after, name):
    send_sems, recv_sems, *bufs = handle[:-1]
    n = len(bufs)

    def body(*refs):
        buf = refs[:n]
        send_sems, recv_sems = refs[n], refs[n + 1]
        c = lax.axis_index("c")
        j = 2 * lax.axis_index("x") + lax.axis_index("y")
        for d in range(1, 4):
            frm = (j + 4 - d) % 4
            for a in range(n):
                cp = pltpu.make_async_remote_copy(
                    src_ref=buf[a].at[:, j, c], dst_ref=buf[a].at[:, frm, c], send_sem=send_sems.at[3 * a + d - 1],
                    recv_sem=recv_sems.at[3 * a + d - 1], device_id=_chip_of(frm, c), device_id_type=MESH)
                cp.wait_send()
                cp.wait_recv()

    return pl.pallas_call(
        body, name=name,
        out_shape=[pltpu.HBM(b.shape, b.dtype) for b in bufs],
        in_specs=[_HBM] * n + [_SEM, _SEM, _ANY], out_specs=[_HBM] * n,
        input_output_aliases={a: a for a in range(n)},
        compiler_params=pltpu.CompilerParams(has_side_effects=_DATAFLOW),
    )(*bufs, send_sems, recv_sems, after)


def _gather_forward(bufs, name):
    n = len(bufs)

    def body(*refs):
        buf = refs[n:2 * n]
        send_sems, recv_sems = refs[2 * n:]
        x, y, c = lax.axis_index("x"), lax.axis_index("y"), lax.axis_index("c")
        j = 2 * x + y
        sib = (x, y, 1 - c)
        sends = []
        for d in range(1, 4):
            frm = (j + 4 - d) % 4
            for a in range(n):
                cp = pltpu.make_async_remote_copy(
                    src_ref=buf[a].at[:, frm, c], dst_ref=buf[a].at[:, frm, c], send_sem=send_sems.at[a, d - 1],
                    recv_sem=recv_sems.at[a, d - 1], device_id=sib, device_id_type=MESH)
                cp.start()
                sends.append(cp)
        for d in range(1, 4):
            frm = (j + 4 - d) % 4
            for a in range(n):
                pltpu.make_async_remote_copy(
                    src_ref=buf[a].at[:, frm, 1 - c], dst_ref=buf[a].at[:, frm, 1 - c], send_sem=send_sems.at[a, d - 1],
                    recv_sem=recv_sems.at[a, d - 1], device_id=sib, device_id_type=MESH).wait_recv()
        for cp in sends:
            cp.wait_send()

    return pl.pallas_call(
        body, name=name,
        in_specs=[_ANY] * n, out_specs=[_ANY] * n,
        out_shape=[jax.ShapeDtypeStruct(b.shape, b.dtype) for b in bufs],
        scratch_shapes=[pltpu.SemaphoreType.DMA((n, 3))] * 2,
        input_output_aliases={a: a for a in range(n)},
    )(*bufs)


def _chip_start(parts, name):
    n = len(parts)
    lands = [lax.empty((3,) + p.shape[1:], p.dtype) for p in parts]

    def body(*refs):
        src, dst = refs[:n], refs[n:2 * n]
        send_sems, recv_sems = refs[2 * n], refs[2 * n + 1]
        token = refs[4 * n + 2]
        c = lax.axis_index("c")
        j = 2 * lax.axis_index("x") + lax.axis_index("y")
        for d in range(1, 4):
            to = (j + d) % 4
            for a in range(n):
                pltpu.make_async_remote_copy(
                    src_ref=src[a].at[to], dst_ref=dst[a].at[d - 1], send_sem=send_sems.at[3 * a + d - 1],
                    recv_sem=recv_sems.at[3 * a + d - 1], device_id=_chip_of(to, c), device_id_type=MESH).start()
        token[...] = jnp.zeros_like(token)

    both = list(parts) + lands
    return pl.pallas_call(
        body, name=name,
        out_shape=(pltpu.SemaphoreType.DMA((3 * n,)), pltpu.SemaphoreType.DMA((3 * n,)),
                   *[pltpu.HBM(b.shape, b.dtype) for b in both], jax.ShapeDtypeStruct((8, 128), F32)),
        in_specs=[_HBM] * (2 * n),
        out_specs=(_SEM, _SEM, *[_HBM] * (2 * n), pl.BlockSpec(memory_space=pltpu.VMEM)),
        input_output_aliases={a: 2 + a for a in range(2 * n)},
        compiler_params=pltpu.CompilerParams(has_side_effects=_DATAFLOW),
    )(*_in_hbm(both))


def _chip_wait(handle, after, name):
    send_sems, recv_sems, *both = handle[:-1]
    n = len(both) // 2

    def body(*refs):
        src, dst = refs[:n], refs[n:2 * n]
        send_sems, recv_sems = refs[2 * n], refs[2 * n + 1]
        c = lax.axis_index("c")
        j = 2 * lax.axis_index("x") + lax.axis_index("y")
        for d in range(1, 4):
            to = (j + d) % 4
            for a in range(n):
                cp = pltpu.make_async_remote_copy(
                    src_ref=src[a].at[to], dst_ref=dst[a].at[d - 1], send_sem=send_sems.at[3 * a + d - 1],
                    recv_sem=recv_sems.at[3 * a + d - 1], device_id=_chip_of(to, c), device_id_type=MESH)
                cp.wait_send()
                cp.wait_recv()

    res = pl.pallas_call(
        body, name=name,
        out_shape=[pltpu.HBM(b.shape, b.dtype) for b in both],
        in_specs=[_HBM] * (2 * n) + [_SEM, _SEM, _ANY], out_specs=[_HBM] * (2 * n),
        input_output_aliases={a: a for a in range(2 * n)},
        compiler_params=pltpu.CompilerParams(has_side_effects=_DATAFLOW),
    )(*both, send_sems, recv_sems, after)
    return list(res[n:])


def _pair_exchange(grads, name):
    n = len(grads)

    def body(*refs):
        src, dst = refs[:n], refs[n:2 * n]
        send_sems, recv_sems = refs[2 * n:]
        c = lax.axis_index("c")
        sib = (lax.axis_index("x"), lax.axis_index("y"), 1 - c)
        cps = []
        for a in range(n):
            cp = pltpu.make_async_remote_copy(
                src_ref=src[a].at[:, pl.ds(1 - c, 1)], dst_ref=dst[a], send_sem=send_sems.at[a],
                recv_sem=recv_sems.at[a], device_id=sib, device_id_type=MESH)
            cp.start()
            cps.append(cp)
        for cp in cps:
            cp.wait_recv()
        for cp in cps:
            cp.wait_send()

    return pl.pallas_call(
        body, name=name, in_specs=[_ANY] * n, out_specs=[_ANY] * n,
        out_shape=[jax.ShapeDtypeStruct((g.shape[0], 1) + g.shape[2:], g.dtype) for g in grads],
        scratch_shapes=[pltpu.SemaphoreType.DMA((n,)), pltpu.SemaphoreType.DMA((n,))],
    )(*grads)


def _pair_add(g4, r1, cj_arr, name, TR=512):
    B4, _, rh, C = g4.shape
    B = B4 // 4
    TR = rh if rh <= TR else _row_tile(rh, TR)

    def body(cj_ref, g_ref, r_ref, o16_ref, own_ref):
        s = g_ref[...] + r_ref[...]
        o16_ref[...] = s.astype(BF16)

        @pl.when(pl.program_id(2) == cj_ref[1])
        def _():
            own_ref[...] = s

    return pl.pallas_call(
        body, name=name,
        grid_spec=pltpu.PrefetchScalarGridSpec(
            num_scalar_prefetch=1, grid=(B, rh // TR, 4),
            in_specs=[pl.BlockSpec((None, None, TR, C), lambda b, t, p, cj: (b * 4 + p, cj[0], t, 0)),
                      pl.BlockSpec((None, None, TR, C), lambda b, t, p, cj: (b * 4 + p, 0, t, 0))],
            out_specs=[pl.BlockSpec((None, None, TR, C), lambda b, t, p, cj: (p, b, t, 0)),
                       pl.BlockSpec((None, TR, C), lambda b, t, p, cj: (b, t, 0))]),
        out_shape=[jax.ShapeDtypeStruct((4, B, rh, C), BF16), jax.ShapeDtypeStruct((B, rh, C), F32)],
        compiler_params=_params(("parallel", "parallel", "arbitrary")),
    )(cj_arr, g4, r1)


def _chip_add(own, r2, cj_arr, into, first, name, TR=512):
    B, rh, C = own.shape
    TR = rh if rh <= TR else _row_tile(rh, TR)

    def body(cj_ref, p_ref, r_ref, _into_ref, o_ref):
        o_ref[...] = p_ref[...] + r_ref[0].astype(F32) + r_ref[1].astype(F32) + r_ref[2].astype(F32)

    return pl.pallas_call(
        body, name=name,
        grid_spec=pltpu.PrefetchScalarGridSpec(
            num_scalar_prefetch=1, grid=(B, rh // TR),
            in_specs=[pl.BlockSpec((None, TR, C), lambda b, t, cj: (b, t, 0)),
                      pl.BlockSpec((3, None, TR, C), lambda b, t, cj: (0, b, t, 0)),
                      _ANY],
            out_specs=pl.BlockSpec((None, None, TR, C), lambda b, t, cj: (first + b, cj[0], t, 0))),
        out_shape=jax.ShapeDtypeStruct(into.shape, F32),
        input_output_aliases={3: 0},
        compiler_params=_params(("parallel", "parallel")),
    )(cj_arr, own, r2, into)


def _pair_share(bufs, name):
    n = len(bufs)

    def body(*refs):
        buf = refs[n:2 * n]
        send_sems, recv_sems = refs[2 * n:]
        c = lax.axis_index("c")
        sib = (lax.axis_index("x"), lax.axis_index("y"), 1 - c)
        cps = []
        for a in range(n):
            cp = pltpu.make_async_remote_copy(
                src_ref=buf[a].at[:, c], dst_ref=buf[a].at[:, c], send_sem=send_sems.at[a],
                recv_sem=recv_sems.at[a], device_id=sib, device_id_type=MESH)
            cp.start()
            cps.append(cp)
        for a in range(n):
            pltpu.make_async_remote_copy(
                src_ref=buf[a].at[:, 1 - c], dst_ref=buf[a].at[:, 1 - c], send_sem=send_sems.at[a],
                recv_sem=recv_sems.at[a], device_id=sib, device_id_type=MESH).wait_recv()
        for cp in cps:
            cp.wait_send()

    return pl.pallas_call(
        body, name=name, in_specs=[_ANY] * n, out_specs=[_ANY] * n,
        out_shape=[jax.ShapeDtypeStruct(b.shape, b.dtype) for b in bufs],
        scratch_shapes=[pltpu.SemaphoreType.DMA((n,)), pltpu.SemaphoreType.DMA((n,))],
        input_output_aliases={a: a for a in range(n)},
    )(*bufs)


def _pair_swap(arr, name):
    def body(src, dst, send_sem, recv_sem):
        sib = (lax.axis_index("x"), lax.axis_index("y"), 1 - lax.axis_index("c"))
        cp = pltpu.make_async_remote_copy(src_ref=src, dst_ref=dst, send_sem=send_sem, recv_sem=recv_sem,
                                          device_id=sib, device_id_type=MESH)
        cp.start()
        cp.wait_recv()
        cp.wait_send()

    return pl.pallas_call(
        body, name=name, in_specs=[_ANY], out_specs=_ANY,
        out_shape=jax.ShapeDtypeStruct(arr.shape, arr.dtype),
        scratch_shapes=[pltpu.SemaphoreType.DMA, pltpu.SemaphoreType.DMA],
    )(arr)


class _ReduceScatter:
    def __init__(self, n_layers, cj_arr):
        self.L, self.cj = n_layers, cj_arr
        self.total = None
        self.pending = None

    def _land(self, layer, owns, r2):
        if self.total is None:
            self.total = [lax.empty((self.L * o.shape[0], 2) + o.shape[1:], F32) for o in owns]
        self.total = [_chip_add(o, r, self.cj, t, layer * o.shape[0], "rs_chip_add")
                      for o, r, t in zip(owns, r2, self.total)]

    def add_layer(self, layer, grads):
        g4 = [g.reshape(g.shape[0] * 4, 2, g.shape[1] // 8, g.shape[2]) for g in grads]
        r1 = _pair_exchange(g4, "rs_pair_exchange")
        added = [_pair_add(g, r, self.cj, "rs_pair_add") for g, r in zip(g4, r1)]
        parts, owns = [p for p, _ in added], [o for _, o in added]
        if self.pending is not None:
            handle, p_layer, p_owns = self.pending
            self._land(p_layer, p_owns, _chip_wait(handle, owns[-1], "rs_chip_wait"))
            self.pending = None
        handle = _chip_start(parts, "rs_chip_start")
        self.pending = (handle, layer, owns)
        return (handle[-1],)

    def result(self, after):
        handle, p_layer, p_owns = self.pending
        self._land(p_layer, p_owns, _chip_wait(handle, after, "rs_chip_wait"))
        full = _pair_share(self.total, "rs_pair_share")
        return [f.reshape(f.shape[0], f.shape[1] * f.shape[2], f.shape[3]) for f in full]


def _relu2_epi(acc):
    return acc, jnp.square(jnp.maximum(acc, 0.0))


def _relu2_bwd_epi(acc, u):
    return (acc * (2.0 * jnp.maximum(u, 0.0)),)


_GRAD_ORDER = ("winT", "wbT", "wout", "wq", "wkv", "woT", "wupT", "wdown")


def _forward_backward(x, mem, target, weights_of, P, grads_done, first_deps=()):
    L = P["g_norm"].shape[0]
    S, D = x.shape
    gn = lambda l, i: P["g_norm"][l, i][None]

    saved = []
    (h,) = _resnorm_fwd(x, None, None, gn(0, 0), "norm_in", deps=first_deps)
    xr = x
    for l in range(L):
        W = weights_of(l, xr)
        proj = _mm(h, W["winT"], "nt", "in_proj", b_pre=(0,), tn=768)
        small = (jnp.broadcast_to(P["sinks"][l][:, None], (8, BLK)), P["ws"][l], P["bs"][l][:, :, None],
                 P["gsgu"][l][None], P["wp"][l], P["ps"][l][None])
        br = _mix_fwd(proj, *small, "mix_fwd")
        pb = lax.empty((3, S, D), F32)
        for n in range(3):
            pb = _mm(br, W["wbT"], "nt", "branch_proj", a_pre=(n,), b_pre=(n,), into=pb, out_pre=(n,))
        merged = _merge_fwd(proj, pb, "merge_fwd")
        z = _mm(merged, W["wout"], "nn", "out_proj", b_pre=(0,))
        x1, hm = _resnorm_fwd(xr, z, gn(l, 1), gn(l, 2), "resnorm_fwd")
        qm = _mm(hm, W["wq"], "nn", "mem_q", b_pre=(0,))
        (memn,) = _resnorm_fwd(mem, None, None, P["g_mem"][l][None], "mem_norm")
        kv = _mm(memn, W["wkv"], "nn", "mem_kv", b_pre=(0,))
        om = _memattn_fwd(qm, kv, "memattn_fwd")
        ym = _mm(om, W["woT"], "nt", "mem_o", b_pre=(0,))
        x2, hf = _resnorm_fwd(x1, ym, gn(l, 3), gn(l, 4), "resnorm_fwd")
        u, a = _mm(hf, W["wupT"], "nt", "mlp_up", b_pre=(0,), out_dtypes=(F32, BF16), epi=_relu2_epi)
        yf = _mm(a, W["wdown"], "nn", "mlp_down", b_pre=(0,))
        saved.append(dict(W=W, x0=xr, h=h, proj=proj, small=small, br=br, pb=pb, merged=merged, z=z, x1=x1, hm=hm,
                          qm=qm, memn=memn, kv=kv, om=om, ym=ym, x2=x2, hf=hf, u=u, a=a, yf=yf))
        if l < L - 1:
            xr, h = _resnorm_fwd(x2, yf, gn(l, 5), gn(l + 1, 0), "resnorm_fwd")
    dres, loss = _final_fwd(saved[-1]["x2"], saved[-1]["yf"], gn(L - 1, 5), target, "loss_head")

    dgn = [[None] * 6 for _ in range(L)]
    dsmall = {k: [None] * L for k in ("g_mem", "sinks", "ws", "bs", "gsgu", "wp", "ps")}
    dh = None
    for l in reversed(range(L)):
        s = saved[l]
        W, G = s["W"], {}
        if l == L - 1:
            dx2, dyf, dgn[l][5] = _resnorm_bwd(s["x2"], s["yf"], gn(l, 5), None, dres, None, "resnorm_bwd_top")
        else:
            dx2, dyf, dgn[l][5], dgn[l + 1][0] = _resnorm_bwd(s["x2"], s["yf"], gn(l, 5), gn(l + 1, 0), dres, dh,
                                                              "resnorm_bwd", deps=deps)
        du = _mm(dyf, W["wdown"], "nt", "mlp_down_dx", b_pre=(0,), out_dtypes=(BF16,), extras=(s["u"],), epi=_relu2_bwd_epi)
        G["wdown"] = _mm(s["a"], dyf, "tn", "mlp_down_dw")[None]
        dhf = _mm(du, W["wupT"], "nn", "mlp_up_dx", b_pre=(0,))
        G["wupT"] = _mm(du, s["hf"], "tn", "mlp_up_dw")[None]
        dx1, dym, dgn[l][3], dgn[l][4] = _resnorm_bwd(s["x1"], s["ym"], gn(l, 3), gn(l, 4), dx2, dhf, "resnorm_bwd")
        dom = _mm(dym, W["woT"], "nn", "mem_o_dx", b_pre=(0,))
        G["woT"] = _mm(dym, s["om"], "tn", "mem_o_dw")[None]
        dqm, dkv = _memattn_bwd(s["qm"], s["kv"], dom, "memattn_bwd")
        dmemn = _mm(dkv, W["wkv"], "nt", "mem_kv_dx", b_pre=(0,))
        G["wkv"] = _mm(s["memn"], dkv, "tn", "mem_kv_dw")[None]
        _, dsmall["g_mem"][l] = _resnorm_bwd(mem, None, None, P["g_mem"][l][None], None, dmemn, "mem_norm_bwd")
        dhm = _mm(dqm, W["wq"], "nt", "mem_q_dx", b_pre=(0,))
        G["wq"] = _mm(s["hm"], dqm, "tn", "mem_q_dw")[None]
        dx0, dz, dgn[l][1], dgn[l][2] = _resnorm_bwd(s["x0"], s["z"], gn(l, 1), gn(l, 2), dx1, dhm, "resnorm_bwd")
        dmerged = _mm(dz, W["wout"], "nt", "out_proj_dx", b_pre=(0,))
        G["wout"] = _mm(s["merged"], dz, "tn", "out_proj_dw")[None]
        dproj, dpb = _merge_bwd(s["proj"], s["pb"], dmerged, "merge_bwd")
        dbr = lax.empty((3, S, 512), F32)
        G["wbT"] = lax.empty(W["wbT"].shape, F32)
        for n in range(3):
            dbr = _mm(dpb, W["wbT"], "nn", "branch_proj_dx", a_pre=(n,), b_pre=(n,), into=dbr, out_pre=(n,))
            G["wbT"] = _mm(dpb, s["br"], "tn", "branch_proj_dw", a_pre=(n,), b_pre=(n,), into=G["wbT"], out_pre=(n,))
        (dproj, dsmall["sinks"][l], dsmall["ws"][l], dsmall["bs"][l], dsmall["gsgu"][l], dsmall["wp"][l],
         dsmall["ps"][l]) = _mix_bwd(s["proj"], dbr, dproj, *s["small"], "mix_bwd")
        dh = _mm(dproj, W["winT"], "nn", "in_proj_dx", b_pre=(0,), tk=768)
        G["winT"] = _mm(dproj, s["h"], "tn", "in_proj_dw", tm=768)[None]
        deps = grads_done(l, [G[k] for k in _GRAD_ORDER])
        dres = dx0
    grad_x, dgn[0][0] = _resnorm_bwd(x, None, None, gn(0, 0), dres, dh, "norm_in_bwd", deps=deps)

    small_grads = dict(
        g_norm=jnp.stack([jnp.concatenate(row, axis=0) for row in dgn]),
        g_mem=jnp.concatenate(dsmall["g_mem"], axis=0),
        sinks=jnp.stack([d[:, 0] for d in dsmall["sinks"]]),
        ws=jnp.stack(dsmall["ws"]),
        bs=jnp.stack([d[:, :, 0] for d in dsmall["bs"]]),
        gsgu=jnp.concatenate(dsmall["gsgu"], axis=0),
        wp=jnp.stack(dsmall["wp"]),
        ps=jnp.concatenate(dsmall["ps"], axis=0),
    )
    return loss, grad_x, small_grads


_PACK_ROWS = 512


def _as_rows(a):
    n = math.prod(a.shape)
    if n % 128:
        a = jnp.pad(a.reshape(-1), (0, (-n) % 128))
    r = a.reshape(-1, 128)
    return jnp.pad(r, ((0, (-r.shape[0]) % 8), (0, 0))) if r.shape[0] % 8 else r


def _pack(arrays):
    rows = [_as_rows(a) for a in arrays]
    total = sum(r.shape[0] for r in rows)
    tail = (-total) % _PACK_ROWS
    if tail:
        rows.append(jnp.zeros((tail, 128), rows[0].dtype))
    return jnp.concatenate(rows, axis=0)


def _unpack(packed, like):
    out, pos = [], 0
    for a in like:
        n = math.prod(a.shape)
        nr = -(-n // 128)
        rows = packed[pos:pos + nr]
        out.append((rows.reshape(-1)[:n] if n % 128 else rows).reshape(a.shape))
        pos += nr + (-nr) % 8
    return out


_BIG = ("w_in", "w_branch", "w_out", "w_q_mem", "w_kv_mem", "w_o_mem", "w_up", "w_down")
_SMALL = ("g_norm", "g_mem", "attn_sinks", "w_spatial", "b_spatial", "g_sgu", "w_pool", "pool_scale")
_WEIGHTS = ("g_norm", "g_mem", "w_in", "attn_sinks", "w_spatial", "b_spatial", "g_sgu", "w_pool", "pool_scale",
            "w_branch", "w_out", "w_q_mem", "w_kv_mem", "w_o_mem", "w_up", "w_down")


def _to_working(name, w):
    if name == "w_in":
        return jnp.swapaxes(w, 1, 2)
    if name == "w_branch":
        t = jnp.swapaxes(w, 2, 3)
        return t.reshape(t.shape[0] * 3, t.shape[2], t.shape[3])
    if name in ("w_o_mem", "w_up"):
        return jnp.swapaxes(w, 1, 2)
    return w


def _from_working(name, g):
    if name == "w_in":
        return jnp.swapaxes(g, 1, 2)
    if name == "w_branch":
        return jnp.swapaxes(g.reshape(g.shape[0] // 3, 3, g.shape[1], g.shape[2]), 2, 3)
    if name in ("w_o_mem", "w_up"):
        return jnp.swapaxes(g, 1, 2)
    return g


_WKEY = dict(w_in="winT", w_branch="wbT", w_out="wout", w_q_mem="wq", w_kv_mem="wkv", w_o_mem="woT",
             w_up="wupT", w_down="wdown")


def kernel(x, mem, g_norm, g_mem, w_in, attn_sinks, w_spatial, b_spatial, g_sgu, w_pool, pool_scale, w_branch, w_out, w_q_mem, w_kv_mem, w_o_mem, w_up, w_down, loss_target, m_g_norm, m_g_mem, m_w_in, m_attn_sinks, m_w_spatial, m_b_spatial, m_g_sgu, m_w_pool, m_pool_scale, m_w_branch, m_w_out, m_w_q_mem, m_w_kv_mem, m_w_o_mem, m_w_up, m_w_down, v_g_norm, v_g_mem, v_w_in, v_attn_sinks, v_w_spatial, v_b_spatial, v_g_sgu, v_w_pool, v_pool_scale, v_w_branch, v_w_out, v_w_q_mem, v_w_kv_mem, v_w_o_mem, v_w_up, v_w_down):
    w = dict(g_norm=g_norm, g_mem=g_mem, w_in=w_in, attn_sinks=attn_sinks, w_spatial=w_spatial, b_spatial=b_spatial,
             g_sgu=g_sgu, w_pool=w_pool, pool_scale=pool_scale, w_branch=w_branch, w_out=w_out, w_q_mem=w_q_mem,
             w_kv_mem=w_kv_mem, w_o_mem=w_o_mem, w_up=w_up, w_down=w_down)
    m = dict(g_norm=m_g_norm, g_mem=m_g_mem, w_in=m_w_in, attn_sinks=m_attn_sinks, w_spatial=m_w_spatial,
             b_spatial=m_b_spatial, g_sgu=m_g_sgu, w_pool=m_w_pool, pool_scale=m_pool_scale, w_branch=m_w_branch,
             w_out=m_w_out, w_q_mem=m_w_q_mem, w_kv_mem=m_w_kv_mem, w_o_mem=m_w_o_mem, w_up=m_w_up, w_down=m_w_down)
    v = dict(g_norm=v_g_norm, g_mem=v_g_mem, w_in=v_w_in, attn_sinks=v_attn_sinks, w_spatial=v_w_spatial,
             b_spatial=v_b_spatial, g_sgu=v_g_sgu, w_pool=v_w_pool, pool_scale=v_pool_scale, w_branch=v_w_branch,
             w_out=v_w_out, w_q_mem=v_w_q_mem, w_kv_mem=v_w_kv_mem, w_o_mem=v_w_o_mem, w_up=v_w_up, w_down=v_w_down)
    L = g_norm.shape[0]
    j = 2 * lax.axis_index("x") + lax.axis_index("y")
    c = lax.axis_index("c")
    j_arr = jnp.reshape(j, (1,)).astype(jnp.int32)
    cj_arr = jnp.stack([c, j]).astype(jnp.int32)

    gs = g_norm.shape[2]
    working = [_to_working(n, w[n]) for n in _BIG]
    per_layer = [wk.shape[0] // L for wk in working]

    def own_slabs(l, deps):
        return [_own_slab(wk, BF16, j_arr, "own_slab", first=l * b, count=b, deps=deps if a == 0 else ())
                for a, (wk, b) in enumerate(zip(working, per_layer))]

    norm_slab = _own_slab(g_norm.reshape(1, L * 6 * gs // 128, 128), F32, j_arr, "own_slab_norm")
    in_flight = {0: _gather_start(own_slabs(0, ()) + [norm_slab], "gather_start_first")}
    for l in range(1, L):
        in_flight[l] = _gather_start(own_slabs(l, (in_flight[l - 1][-1],)), "gather_start")
    tokens = [h[-1] for h in in_flight.values()]
    first = _gather_forward(_gather_wait(in_flight[0], tokens[-1], "gather_wait_first"), "gather_forward_first")
    gn_full = jnp.transpose(first[-1].reshape(4, L * 6, gs), (1, 0, 2)).reshape(L, 6, 4 * gs)
    P = dict(g_norm=gn_full, g_mem=g_mem, sinks=attn_sinks, ws=w_spatial, bs=b_spatial, gsgu=g_sgu, wp=w_pool,
             ps=pool_scale)

    def weights_of(l, after):
        got = first[:-1] if l == 0 else _gather_forward(_gather_wait(in_flight[l], after, "gather_wait"), "gather_forward")
        return {k: g.reshape(g.shape[0], 8 * g.shape[3], g.shape[4]) for k, g in zip(_GRAD_ORDER, got)}

    rs = _ReduceScatter(L, cj_arr)
    loss_part, grad_x, sg = _forward_backward(x[0], mem[0], loss_target[0], weights_of, P, rs.add_layer,
                                              first_deps=tokens)
    loss = lax.psum(loss_part[0, 0], ("x", "y", "c"))

    full_small = [sg["g_norm"], sg["g_mem"], sg["sinks"], sg["ws"], sg["bs"], sg["gsgu"], sg["wp"], sg["ps"]]
    packed = _pack(full_small)
    pair_sum = _own_slab(packed[None], F32, j_arr, "small_grads_pair_sum", plus=_pair_swap(packed, "small_grads_swap")[None])
    (chip_sums,) = _gather_weights([pair_sum], "gather_small_grads")
    total = _sum_slots(chip_sums.reshape(4, *packed.shape), "sum_small_grads")
    grads = {n: _from_working(n, g) for n, g in zip(_BIG, rs.result(total))}
    for n, g in zip(_SMALL, _unpack(total, full_small)):
        grads[n] = lax.dynamic_slice_in_dim(g, j * g_norm.shape[2], g_norm.shape[2], axis=2) if n == "g_norm" else g

    delta, new_m, new_v = {}, {}, {}
    for n in _BIG:
        view = (lambda t: jnp.swapaxes(t, 1, 2)) if n == "w_in" else (lambda t: t)
        shp = view(w[n]).shape
        two_d = lambda t: view(t).reshape(-1, shp[-1])
        d_, m_, v_ = _adamw(two_d(w[n]), two_d(grads[n]), two_d(m[n]), two_d(v[n]), "adamw")
        delta[n], new_m[n], new_v[n] = view(d_.reshape(shp)), view(m_.reshape(shp)), view(v_.reshape(shp))
    small_w = [w[n] for n in _SMALL]
    d_, m_, v_ = _adamw(_pack(small_w), _pack([grads[n] for n in _SMALL]), _pack([m[n] for n in _SMALL]),
                        _pack([v[n] for n in _SMALL]), "adamw_small")
    for n, dd, mm_, vv in zip(_SMALL, _unpack(d_, small_w), _unpack(m_, small_w), _unpack(v_, small_w)):
        delta[n], new_m[n], new_v[n] = dd, mm_, vv

    return (loss, grad_x[None], *[grads[n] for n in _WEIGHTS], *[delta[n] for n in _WEIGHTS],
            *[new_m[n] for n in _WEIGHTS], *[new_v[n] for n in _WEIGHTS])
```

```python
import functools
import math

import jax
import jax.numpy as jnp
from jax import lax
from jax.experimental import pallas as pl
from jax.experimental.pallas import tpu as pltpu

F32 = jnp.float32
BF16 = jnp.bfloat16
MESH = pl.DeviceIdType.MESH

EPS = 1e-6
NEG_INF = -1e30
BLK = 128
HALO = 16
POOL_WINDOWS = (2, 4, 8, 16)
ATT_SCALE = 1.0 / math.sqrt(64.0)
MEM_SCALE = 1.0 / math.sqrt(128.0)
C_Q, C_K, C_V, C_SU, C_SV, C_PC, C_GATE, C_END = 0, 512, 640, 768, 1280, 1792, 2304, 5376

ADAM_LR, ADAM_B1, ADAM_B2, ADAM_EPS, ADAM_WD, ADAM_STEP = 0.001, 0.9, 0.999, 1e-08, 0.01, 10

VMEM_LIMIT_BYTES = 56 * 1024 * 1024

_DIMS = {
    "nn": (((1,), (0,)), ((), ())),
    "nt": (((1,), (1,)), ((), ())),
    "tn": (((0,), (0,)), ((), ())),
}


def _dot(a, b, mode):
    return lax.dot_general(a, b, _DIMS[mode], preferred_element_type=F32)


def _params(semantics):
    return pltpu.CompilerParams(dimension_semantics=semantics, vmem_limit_bytes=VMEM_LIMIT_BYTES)


def _tile(dim, pref):
    if dim <= pref:
        return dim
    t = (pref // 128) * 128
    while t >= 128:
        if dim % t == 0:
            return t
        t -= 128
    raise ValueError(f"no tile for {dim}")


def _rms(x, g):
    return x * lax.rsqrt(jnp.mean(x * x, axis=-1, keepdims=True) + EPS) * g


def _mm(a, b, mode, name, *, out_dtypes=(F32,), a_pre=(), b_pre=(), into=None, out_pre=(),
        extras=(), epi=None, deps=(), tm=1024, tn=1024, tk=1024):
    a2, b2 = a.shape[len(a_pre):], b.shape[len(b_pre):]
    if mode == "nn":
        (M, K), (K2, N) = a2, b2
    elif mode == "nt":
        (M, K), (N, K2) = a2, b2
    else:
        (K, M), (K2, N) = a2, b2
    assert K == K2, (a.shape, b.shape, mode)
    tm, tn, tk = _tile(M, tm), _tile(N, tn), _tile(K, tk)
    nk = K // tk
    na, nb_, no = len(a_pre), len(b_pre), len(out_pre)
    if mode == "tn":
        a_spec = pl.BlockSpec((None,) * na + (tk, tm), lambda i, j, k: a_pre + (k, i))
    else:
        a_spec = pl.BlockSpec((None,) * na + (tm, tk), lambda i, j, k: a_pre + (i, k))
    if mode == "nt":
        b_spec = pl.BlockSpec((None,) * nb_ + (tn, tk), lambda i, j, k: b_pre + (j, k))
    else:
        b_spec = pl.BlockSpec((None,) * nb_ + (tk, tn), lambda i, j, k: b_pre + (k, j))
    tile_spec = pl.BlockSpec((tm, tn), lambda i, j, k: (i, j))
    ne, nout = len(extras), len(out_dtypes)
    in_specs = [a_spec, b_spec] + [tile_spec] * ne
    operands = [a, b, *extras]
    aliases = {}
    if into is not None:
        assert nout == 1
        in_specs.append(pl.BlockSpec(memory_space=pl.ANY))
        operands.append(into)
        aliases = {len(operands) - 1: 0}
        out_shape = [jax.ShapeDtypeStruct(into.shape, into.dtype)]
        out_specs = [pl.BlockSpec((None,) * no + (tm, tn), lambda i, j, k: out_pre + (i, j))]
    else:
        out_shape = [jax.ShapeDtypeStruct((M, N), dt) for dt in out_dtypes]
        out_specs = [tile_spec] * nout
    in_specs += [pl.BlockSpec(memory_space=pl.ANY)] * len(deps)
    operands += list(deps)

    def body(*refs):
        a_ref, b_ref = refs[0], refs[1]
        ex = refs[2:2 + ne]
        pos = 2 + ne + (1 if into is not None else 0) + len(deps)
        outs = refs[pos:pos + nout]
        acc_ref = refs[pos + nout] if nk > 1 else None

        def finish(acc):
            vals = epi(acc, *[e[...] for e in ex]) if epi is not None else (acc,)
            for o, v in zip(outs, vals):
                o[...] = v.astype(o.dtype)

        def prod():
            return _dot(a_ref[...].astype(BF16), b_ref[...].astype(BF16), mode)

        if nk == 1:
            finish(prod())
        else:
            k = pl.program_id(2)

            @pl.when(k == 0)
            def _():
                acc_ref[...] = jnp.zeros_like(acc_ref)

            acc_ref[...] += prod()

            @pl.when(k == nk - 1)
            def _():
                finish(acc_ref[...])

    res = pl.pallas_call(
        body, name=name, grid=(M // tm, N // tn, nk),
        in_specs=in_specs, out_specs=out_specs, out_shape=out_shape,
        scratch_shapes=[pltpu.VMEM((tm, tn), F32)] if nk > 1 else [],
        input_output_aliases=aliases,
        compiler_params=_params(("parallel", "parallel", "arbitrary")),
    )(*operands)
    return res[0] if nout == 1 else tuple(res)


def _resnorm_fn(has_post, has_pre):
    def f(*a):
        x, k = a[0], 1
        if has_post:
            x, k = x + _rms(a[1], a[2]), 3
        outs = [x]
        if has_pre:
            outs.append(_rms(x, a[k]))
        return tuple(outs)
    return f


def _row_spec(T, W):
    return pl.BlockSpec((T, W), lambda i: (i, 0))


def _par_spec(W):
    return pl.BlockSpec((1, W), lambda i: (0, 0))


def _resnorm_fwd(xr, y, gp, gq, name, T=512, deps=()):
    S, D = xr.shape
    T = min(T, S)
    has_post, has_pre = y is not None, gq is not None
    f = _resnorm_fn(has_post, has_pre)
    ins = [xr] + ([y, gp] if has_post else []) + ([gq] if has_pre else [])
    in_specs = [_row_spec(T, D)] + ([_row_spec(T, D), _par_spec(D)] if has_post else []) + ([_par_spec(D)] if has_pre else [])
    out_shape, out_specs = [], []
    if has_post:
        out_shape.append(jax.ShapeDtypeStruct((S, D), F32)); out_specs.append(_row_spec(T, D))
    if has_pre:
        out_shape.append(jax.ShapeDtypeStruct((S, D), BF16)); out_specs.append(_row_spec(T, D))
    n_in, n_dep = len(ins), len(deps)

    def body(*refs):
        vals = f(*[r[...] for r in refs[:n_in]])
        outs = list(refs[n_in + n_dep:])
        if has_post:
            outs.pop(0)[...] = vals[0]
        if has_pre:
            outs.pop(0)[...] = vals[1].astype(BF16)

    res = pl.pallas_call(body, name=name, grid=(S // T,),
                         in_specs=in_specs + [pl.BlockSpec(memory_space=pl.ANY)] * n_dep, out_specs=out_specs,
                         out_shape=out_shape, compiler_params=_params(("parallel",)))(*ins, *deps)
    return tuple(res)


def _resnorm_bwd(xr, y, gp, gq, dres, dh, name, T=512, deps=()):
    S, D = xr.shape
    T = min(T, S)
    has_post, has_pre, has_res = y is not None, gq is not None, dres is not None
    f = _resnorm_fn(has_post, has_pre)
    ins = [xr] + ([y, gp] if has_post else []) + ([gq] if has_pre else [])
    in_specs = [_row_spec(T, D)] + ([_row_spec(T, D), _par_spec(D)] if has_post else []) + ([_par_spec(D)] if has_pre else [])
    n_prim = len(ins)
    if has_res:
        ins.append(dres); in_specs.append(_row_spec(T, D))
    if has_pre:
        ins.append(dh); in_specs.append(_row_spec(T, D))
    n_in, n_dep = len(ins), len(deps)
    out_shape = [jax.ShapeDtypeStruct((S, D), F32)]
    out_specs = [_row_spec(T, D)]
    if has_post:
        out_shape += [jax.ShapeDtypeStruct((S, D), BF16), jax.ShapeDtypeStruct((1, D), F32)]
        out_specs += [_row_spec(T, D), _par_spec(D)]
    if has_pre:
        out_shape.append(jax.ShapeDtypeStruct((1, D), F32)); out_specs.append(_par_spec(D))

    def body(*refs):
        i = pl.program_id(0)
        prim = [r[...] for r in refs[:n_prim]]
        rest = list(refs[n_prim:n_in])
        ct_x = rest.pop(0)[...] if has_res else jnp.zeros((T, D), F32)
        cts = [ct_x]
        if has_pre:
            cts.append(rest.pop(0)[...].astype(F32))
        _, vjp = jax.vjp(f, *prim)
        grads = list(vjp(tuple(cts)))
        outs = list(refs[n_in + n_dep:])
        outs.pop(0)[...] = grads.pop(0)
        acc = []
        if has_post:
            outs.pop(0)[...] = grads.pop(0).astype(BF16)
            acc.append((outs.pop(0), grads.pop(0)))
        if has_pre:
            acc.append((outs.pop(0), grads.pop(0)))

        @pl.when(i == 0)
        def _():
            for o, _g in acc:
                o[...] = jnp.zeros_like(o)

        for o, g in acc:
            o[...] += g

    res = pl.pallas_call(body, name=name, grid=(S // T,),
                         in_specs=in_specs + [pl.BlockSpec(memory_space=pl.ANY)] * n_dep, out_specs=out_specs,
                         out_shape=out_shape, compiler_params=_params(("arbitrary",)))(*ins, *deps)
    return tuple(res)


def _final_fwd(xr, y, gp, target, name, T=512):
    S, D = xr.shape
    T = min(T, S)

    def body(x_ref, y_ref, g_ref, t_ref, dy_ref, loss_ref):
        i = pl.program_id(0)
        e = x_ref[...] + _rms(y_ref[...], g_ref[...]) - t_ref[...]
        dy_ref[...] = e / D

        @pl.when(i == 0)
        def _():
            loss_ref[...] = jnp.zeros_like(loss_ref)

        loss_ref[...] += 0.5 * jnp.sum(jnp.sum(e * e, axis=-1, keepdims=True) / D, axis=0, keepdims=True)

    return pl.pallas_call(
        body, name=name, grid=(S // T,),
        in_specs=[_row_spec(T, D), _row_spec(T, D), _par_spec(D), _row_spec(T, D)],
        out_specs=[_row_spec(T, D), pl.BlockSpec((1, 128), lambda i: (0, 0))],
        out_shape=[jax.ShapeDtypeStruct((S, D), F32), jax.ShapeDtypeStruct((1, 128), F32)],
        compiler_params=_params(("arbitrary",)))(xr, y, gp, target)


def _lane_lo():
    return lax.broadcasted_iota(jnp.int32, (1, BLK), 1) < 64


def _att_mask(not_first):
    r = lax.broadcasted_iota(jnp.int32, (BLK, 2 * BLK), 0)
    c = lax.broadcasted_iota(jnp.int32, (BLK, 2 * BLK), 1)
    qc, kc = 2 + r // 64, c // 64
    return (kc <= qc) & (kc >= qc - 2) & (not_first | (c >= BLK))


def _softmax_sink(s, sk):
    m = jnp.maximum(jnp.max(s, axis=-1, keepdims=True), sk)
    e = jnp.exp(s - m)
    es = jnp.exp(sk - m)
    z = jnp.sum(e, axis=-1, keepdims=True) + es
    return e / z, es / z


def _att_operands(cur, kvp, t, lo):
    h = t // 2
    qt = cur[:, C_Q + BLK * t:C_Q + BLK * (t + 1)]
    q_lo = jnp.where(lo, qt, 0.0).astype(BF16)
    q_hi = jnp.where(lo, 0.0, qt).astype(BF16)
    kband = jnp.concatenate([kvp[:, 0:BLK], cur[:, C_K:C_K + BLK]], axis=0)
    vband = jnp.concatenate([kvp[:, BLK:2 * BLK], cur[:, C_V:C_V + BLK]], axis=0)
    kroll = pltpu.roll(kband, 64, 1)
    vroll = pltpu.roll(vband, 64, 1)
    ka, kb = (kband, kroll) if h == 0 else (kroll, kband)
    va = jnp.where(lo, vband if h == 0 else vroll, 0.0)
    vb = jnp.where(lo, 0.0, vroll if h == 0 else vband)
    return q_lo, q_hi, ka.astype(BF16), kb.astype(BF16), va.astype(BF16), vb.astype(BF16)


def _sgu_mask():
    r = lax.broadcasted_iota(jnp.int32, (BLK, BLK), 0)
    c = lax.broadcasted_iota(jnp.int32, (BLK, BLK), 1)
    return (c // 64) <= (r // 64)


def _pool_cnt(blk, w):
    t = blk * BLK + lax.broadcasted_iota(jnp.int32, (BLK, 1), 0)
    return jnp.minimum(t + 1, w).astype(F32)


def _mix_in_specs(nb, rev):
    def b(i):
        return nb - 1 - i if rev else i
    return [
        pl.BlockSpec((BLK, C_GATE), lambda i: (b(i), 0)),
        pl.BlockSpec((BLK, 2 * BLK), lambda i: (jnp.maximum(b(i) - 1, 0), C_K // (2 * BLK))),
        pl.BlockSpec((HALO, C_GATE), lambda i: (jnp.maximum(b(i) * (BLK // HALO) - 1, 0), 0)),
        pl.BlockSpec((8, BLK), lambda i: (0, 0)),
        pl.BlockSpec((4, BLK, BLK), lambda i: (0, 0, 0)),
        pl.BlockSpec((4, BLK, 1), lambda i: (0, 0, 0)),
        pl.BlockSpec((1, 512), lambda i: (0, 0)),
        pl.BlockSpec((4, BLK, BLK), lambda i: (0, 0, 0)),
        pl.BlockSpec((1, 512), lambda i: (0, 0)),
    ]


def _mix_fwd(proj, sinks_b, ws, bs3, gsgu, wp, ps, name):
    S = proj.shape[0]
    nb = S // BLK

    def body(cur_ref, kvp_ref, pcp_ref, sk_ref, ws_ref, bs_ref, gs_ref, wp_ref, ps_ref, br_ref, ext_ref):
        i = pl.program_id(0)
        not_first = i > 0
        lo = _lane_lo()
        cur, kvp = cur_ref[...], kvp_ref[...]
        mask = _att_mask(not_first)
        for t in range(4):
            q_lo, q_hi, ka, kb, va, vb = _att_operands(cur, kvp, t, lo)
            s_lo = jnp.where(mask, _dot(q_lo, ka, "nt") * ATT_SCALE, NEG_INF)
            s_hi = jnp.where(mask, _dot(q_hi, kb, "nt") * ATT_SCALE, NEG_INF)
            p_lo, _ = _softmax_sink(s_lo, sk_ref[2 * t:2 * t + 1, 0:1])
            p_hi, _ = _softmax_sink(s_hi, sk_ref[2 * t + 1:2 * t + 2, 0:1])
            o = _dot(p_lo.astype(BF16), va, "nn") + _dot(p_hi.astype(BF16), vb, "nn")
            br_ref[0, :, BLK * t:BLK * (t + 1)] = o.astype(BF16)
        gu = jax.nn.gelu(cur[:, C_SU:C_SV])
        vn = _rms(jax.nn.gelu(cur[:, C_SV:C_PC]), gs_ref[...]).astype(BF16)
        wmask = _sgu_mask()
        for g in range(4):
            wm = jnp.where(wmask, ws_ref[g], 0.0).astype(BF16)
            sp = _dot(wm, vn[:, BLK * g:BLK * (g + 1)], "nn") + bs_ref[g]
            br_ref[1, :, BLK * g:BLK * (g + 1)] = (gu[:, BLK * g:BLK * (g + 1)] * sp).astype(BF16)
        c = cur[:, C_PC:C_GATE]
        ext_ref[0:HALO, :] = jnp.where(not_first, pcp_ref[:, C_PC:C_GATE], 0.0)
        ext_ref[HALO:HALO + BLK, :] = c
        for g, w in enumerate(POOL_WINDOWS):
            sl = slice(BLK * g, BLK * (g + 1))
            acc = ext_ref[HALO:HALO + BLK, sl]
            for k in range(1, w):
                acc = acc + ext_ref[HALO - k:HALO - k + BLK, sl]
            pooled = acc / _pool_cnt(i, w) - c[:, sl]
            mixed = _dot(pooled.astype(BF16), wp_ref[g].astype(BF16), "nn")
            br_ref[2, :, sl] = (mixed * ps_ref[:, sl]).astype(BF16)

    return pl.pallas_call(
        body, name=name, grid=(nb,),
        in_specs=_mix_in_specs(nb, False),
        out_specs=pl.BlockSpec((3, BLK, 512), lambda i: (0, i, 0)),
        out_shape=jax.ShapeDtypeStruct((3, S, 512), BF16),
        scratch_shapes=[pltpu.VMEM((HALO + BLK, 512), F32)],
        compiler_params=_params(("parallel",)),
    )(proj, proj, proj, sinks_b, ws, bs3, gsgu, wp, ps)


def _mix_bwd(proj, dbr, dproj, sinks_b, ws, bs3, gsgu, wp, ps, name):
    S = proj.shape[0]
    nb = S // BLK

    def body(cur_ref, kvp_ref, pcp_ref, sk_ref, ws_ref, bs_ref, gs_ref, wp_ref, ps_ref, dbr_ref, _dproj_in,
             dp_ref, dsk_ref, dws_ref, dbs_ref, dgs_ref, dwp_ref, dps_ref,
             ext_ref, z_ref, ckv_ref, cpc_ref):
        i = pl.program_id(0)
        blk = nb - 1 - i
        not_first = blk > 0
        lo = _lane_lo()

        @pl.when(i == 0)
        def _():
            for r in (dsk_ref, dws_ref, dbs_ref, dgs_ref, dwp_ref, dps_ref, ckv_ref, cpc_ref, z_ref):
                r[...] = jnp.zeros_like(r)

        cur, kvp = cur_ref[...], kvp_ref[...]
        mask = _att_mask(not_first)
        dk_band = jnp.zeros((2 * BLK, BLK), F32)
        dk_roll = jnp.zeros((2 * BLK, BLK), F32)
        dv_band = jnp.zeros((2 * BLK, BLK), F32)
        dv_roll = jnp.zeros((2 * BLK, BLK), F32)
        for t in range(4):
            h = t // 2
            q_lo, q_hi, ka, kb, va, vb = _att_operands(cur, kvp, t, lo)
            do = dbr_ref[0, :, BLK * t:BLK * (t + 1)].astype(BF16)
            dq = jnp.zeros((BLK, BLK), F32)
            for half, (qm, km, vm) in enumerate(((q_lo, ka, va), (q_hi, kb, vb))):
                sk = sk_ref[2 * t + half:2 * t + half + 1, 0:1]
                s = jnp.where(mask, _dot(qm, km, "nt") * ATT_SCALE, NEG_INF)
                p, p_sink = _softmax_sink(s, sk)
                dp = _dot(do, vm, "nt")
                rs = jnp.sum(p * dp, axis=-1, keepdims=True)
                ds = (p * (dp - rs) * ATT_SCALE).astype(BF16)
                dsk_ref[2 * t + half:2 * t + half + 1, :] += jnp.broadcast_to(
                    -jnp.sum(p_sink * rs, axis=0, keepdims=True), (1, BLK))
                dvm = _dot(p.astype(BF16), do, "tn")
                dkm = _dot(ds, qm, "tn")
                dqm = _dot(ds, km, "nn")
                if half == 0:
                    dq = dq + jnp.where(lo, dqm, 0.0)
                    dvm = jnp.where(lo, dvm, 0.0)
                else:
                    dq = dq + jnp.where(lo, 0.0, dqm)
                    dvm = jnp.where(lo, 0.0, dvm)
                if (h == 0) == (half == 0):
                    dk_band, dv_band = dk_band + dkm, dv_band + dvm
                else:
                    dk_roll, dv_roll = dk_roll + dkm, dv_roll + dvm
            dp_ref[:, C_Q + BLK * t:C_Q + BLK * (t + 1)] = dq.astype(BF16)
        dk = dk_band + pltpu.roll(dk_roll, 64, 1)
        dv = dv_band + pltpu.roll(dv_roll, 64, 1)
        dp_ref[:, C_K:C_K + BLK] = (dk[BLK:] + ckv_ref[:, 0:BLK]).astype(BF16)
        dp_ref[:, C_V:C_V + BLK] = (dv[BLK:] + ckv_ref[:, BLK:]).astype(BF16)
        ckv_ref[:, 0:BLK] = dk[:BLK]
        ckv_ref[:, BLK:] = dv[:BLK]
        su, sv = cur[:, C_SU:C_SV], cur[:, C_SV:C_PC]
        gu, vjp_u = jax.vjp(jax.nn.gelu, su)
        vn, vjp_v = jax.vjp(lambda a, g: _rms(jax.nn.gelu(a), g), sv, gs_ref[...])
        vn16 = vn.astype(BF16)
        wmask = _sgu_mask()
        dgu, dvn = [], []
        for g in range(4):
            sl = slice(BLK * g, BLK * (g + 1))
            wm = jnp.where(wmask, ws_ref[g], 0.0).astype(BF16)
            sp = _dot(wm, vn16[:, sl], "nn") + bs_ref[g]
            dyb = dbr_ref[1, :, sl]
            dgu.append(dyb * sp)
            dsp = dyb * gu[:, sl]
            dsp16 = dsp.astype(BF16)
            dvn.append(_dot(wm, dsp16, "tn"))
            dws_ref[g] += jnp.where(wmask, _dot(dsp16, vn16[:, sl], "nt"), 0.0)
            dbs_ref[g] += jnp.sum(dsp, axis=1, keepdims=True)
        (dsu,) = vjp_u(jnp.concatenate(dgu, axis=1))
        dsv, dgs = vjp_v(jnp.concatenate(dvn, axis=1))
        dp_ref[:, C_SU:C_SV] = dsu.astype(BF16)
        dp_ref[:, C_SV:C_PC] = dsv.astype(BF16)
        dgs_ref[...] += dgs
        c = cur[:, C_PC:C_GATE]
        ext_ref[0:HALO, :] = jnp.where(not_first, pcp_ref[:, C_PC:C_GATE], 0.0)
        ext_ref[HALO:HALO + BLK, :] = c
        for g, w in enumerate(POOL_WINDOWS):
            sl = slice(BLK * g, BLK * (g + 1))
            acc = ext_ref[HALO:HALO + BLK, sl]
            for k in range(1, w):
                acc = acc + ext_ref[HALO - k:HALO - k + BLK, sl]
            cnt = _pool_cnt(blk, w)
            pooled16 = (acc / cnt - c[:, sl]).astype(BF16)
            wp16 = wp_ref[g].astype(BF16)
            mixed = _dot(pooled16, wp16, "nn")
            dyc = dbr_ref[2, :, sl]
            dps_ref[:, sl] += jnp.sum(dyc * mixed, axis=0, keepdims=True)
            dmixed16 = (dyc * ps_ref[:, sl]).astype(BF16)
            dwp_ref[g] += _dot(pooled16, dmixed16, "tn")
            dpooled = _dot(dmixed16, wp16, "nt")
            z_ref[HALO:HALO + BLK, sl] = dpooled / cnt
            dext = z_ref[0:HALO + BLK, sl]
            for k in range(1, w):
                dext = dext + z_ref[k:k + HALO + BLK, sl]
            dp_ref[:, C_PC + BLK * g:C_PC + BLK * (g + 1)] = (
                dext[HALO:] - dpooled + jnp.concatenate([jnp.zeros((BLK - HALO, BLK), F32), cpc_ref[:, sl]], axis=0)
            ).astype(BF16)
            cpc_ref[:, sl] = dext[:HALO]

    n_in = 11
    small = [jax.ShapeDtypeStruct((8, BLK), F32), jax.ShapeDtypeStruct((4, BLK, BLK), F32),
             jax.ShapeDtypeStruct((4, BLK, 1), F32), jax.ShapeDtypeStruct((1, 512), F32),
             jax.ShapeDtypeStruct((4, BLK, BLK), F32), jax.ShapeDtypeStruct((1, 512), F32)]
    small_specs = [pl.BlockSpec((8, BLK), lambda i: (0, 0)), pl.BlockSpec((4, BLK, BLK), lambda i: (0, 0, 0)),
                   pl.BlockSpec((4, BLK, 1), lambda i: (0, 0, 0)), pl.BlockSpec((1, 512), lambda i: (0, 0)),
                   pl.BlockSpec((4, BLK, BLK), lambda i: (0, 0, 0)), pl.BlockSpec((1, 512), lambda i: (0, 0))]
    res = pl.pallas_call(
        body, name=name, grid=(nb,),
        in_specs=_mix_in_specs(nb, True) + [
            pl.BlockSpec((3, BLK, 512), lambda i: (0, nb - 1 - i, 0)),
            pl.BlockSpec(memory_space=pl.ANY)],
        out_specs=[pl.BlockSpec((BLK, C_GATE), lambda i: (nb - 1 - i, 0))] + small_specs,
        out_shape=[jax.ShapeDtypeStruct(dproj.shape, dproj.dtype)] + small,
        scratch_shapes=[pltpu.VMEM((HALO + BLK, 512), F32), pltpu.VMEM((2 * HALO + BLK, 512), F32),
                        pltpu.VMEM((BLK, 2 * BLK), F32), pltpu.VMEM((HALO, 512), F32)],
        input_output_aliases={n_in - 1: 0},
        compiler_params=_params(("arbitrary",)),
    )(proj, proj, proj, sinks_b, ws, bs3, gsgu, wp, ps, dbr, dproj)
    return tuple(res)


_GW = 256


def _merge_fwd(proj, pb, name, T=1024):
    S, D = pb.shape[1], pb.shape[2]
    T = min(T, S)

    def body(gate_ref, pb_ref, out_ref, acc_ref):
        n = pl.program_id(2)

        @pl.when(n == 0)
        def _():
            acc_ref[...] = jnp.zeros_like(acc_ref)

        acc_ref[...] += jax.nn.sigmoid(gate_ref[...]) * pb_ref[...]

        @pl.when(n == 2)
        def _():
            out_ref[...] = acc_ref[...].astype(BF16)

    return pl.pallas_call(
        body, name=name, grid=(S // T, D // _GW, 3),
        in_specs=[pl.BlockSpec((T, _GW), lambda i, j, n: (i, C_GATE // _GW + n * (D // _GW) + j)),
                  pl.BlockSpec((None, T, _GW), lambda i, j, n: (n, i, j))],
        out_specs=pl.BlockSpec((T, _GW), lambda i, j, n: (i, j)),
        out_shape=jax.ShapeDtypeStruct((S, D), BF16),
        scratch_shapes=[pltpu.VMEM((T, _GW), F32)],
        compiler_params=_params(("parallel", "parallel", "arbitrary")),
    )(proj, pb)


def _merge_bwd(proj, pb, dmerged, name, T=1024):
    S, D = pb.shape[1], pb.shape[2]
    T = min(T, S)

    def body(gate_ref, pb_ref, dm_ref, dgate_ref, dpb_ref):
        sg = jax.nn.sigmoid(gate_ref[...])
        dm = dm_ref[...]
        dpb_ref[...] = (dm * sg).astype(BF16)
        dgate_ref[...] = (dm * pb_ref[...] * sg * (1.0 - sg)).astype(BF16)

    gate_map = lambda i, n, j: (i, C_GATE // _GW + n * (D // _GW) + j)
    return pl.pallas_call(
        body, name=name, grid=(S // T, 3, D // _GW),
        in_specs=[pl.BlockSpec((T, _GW), gate_map),
                  pl.BlockSpec((None, T, _GW), lambda i, n, j: (n, i, j)),
                  pl.BlockSpec((T, _GW), lambda i, n, j: (i, j))],
        out_specs=[pl.BlockSpec((T, _GW), gate_map),
                   pl.BlockSpec((None, T, _GW), lambda i, n, j: (n, i, j))],
        out_shape=[jax.ShapeDtypeStruct((S, C_END), BF16), jax.ShapeDtypeStruct((3, S, D), BF16)],
        compiler_params=_params(("parallel", "parallel", "parallel")),
    )(proj, pb, dmerged)


def _memattn_fwd(qm, kv, name, T=512):
    S, NM = qm.shape[0], kv.shape[0]
    T = min(T, S)

    def body(q_ref, kv_ref, o_ref):
        for h in range(4):
            sl = slice(128 * h, 128 * (h + 1))
            k = kv_ref[:, sl].astype(BF16)
            v = kv_ref[:, 512 + 128 * h:512 + 128 * (h + 1)].astype(BF16)
            s = _dot(q_ref[:, sl].astype(BF16), k, "nt") * MEM_SCALE
            p = jax.nn.softmax(s, axis=-1)
            o_ref[:, sl] = _dot(p.astype(BF16), v, "nn").astype(BF16)

    return pl.pallas_call(
        body, name=name, grid=(S // T,),
        in_specs=[_row_spec(T, 512), pl.BlockSpec((NM, 1024), lambda i: (0, 0))],
        out_specs=_row_spec(T, 512), out_shape=jax.ShapeDtypeStruct((S, 512), BF16),
        compiler_params=_params(("parallel",)))(qm, kv)


def _memattn_bwd(qm, kv, dom, name, T=512):
    S, NM = qm.shape[0], kv.shape[0]
    T = min(T, S)

    def body(q_ref, kv_ref, do_ref, dq_ref, dkv_ref):
        i = pl.program_id(0)

        @pl.when(i == 0)
        def _():
            dkv_ref[...] = jnp.zeros_like(dkv_ref)

        for h in range(4):
            sl = slice(128 * h, 128 * (h + 1))
            sv_ = slice(512 + 128 * h, 512 + 128 * (h + 1))
            q = q_ref[:, sl].astype(BF16)
            k = kv_ref[:, sl].astype(BF16)
            v = kv_ref[:, sv_].astype(BF16)
            do = do_ref[:, sl].astype(BF16)
            p = jax.nn.softmax(_dot(q, k, "nt") * MEM_SCALE, axis=-1)
            dp = _dot(do, v, "nt")
            ds = (p * (dp - jnp.sum(p * dp, axis=-1, keepdims=True)) * MEM_SCALE).astype(BF16)
            dq_ref[:, sl] = _dot(ds, k, "nn").astype(BF16)
            dkv_ref[:, sl] += _dot(ds, q, "tn")
            dkv_ref[:, sv_] += _dot(p.astype(BF16), do, "tn")

    return pl.pallas_call(
        body, name=name, grid=(S // T,),
        in_specs=[_row_spec(T, 512), pl.BlockSpec((NM, 1024), lambda i: (0, 0)), _row_spec(T, 512)],
        out_specs=[_row_spec(T, 512), pl.BlockSpec((NM, 1024), lambda i: (0, 0))],
        out_shape=[jax.ShapeDtypeStruct((S, 512), BF16), jax.ShapeDtypeStruct((NM, 1024), F32)],
        compiler_params=_params(("arbitrary",)))(qm, kv, dom)


def _adamw(w, g, m, v, name, TR=512):
    R, C = w.shape
    TR = R if R <= TR else _row_tile(R, TR)
    c1 = 1.0 - ADAM_B1 ** ADAM_STEP
    c2 = 1.0 - ADAM_B2 ** ADAM_STEP

    def body(w_ref, g_ref, m_ref, v_ref, d_ref, nm_ref, nv_ref):
        gv = g_ref[...]
        nm = ADAM_B1 * m_ref[...] + (1.0 - ADAM_B1) * gv
        nv = ADAM_B2 * v_ref[...] + (1.0 - ADAM_B2) * jnp.square(gv)
        d_ref[...] = -ADAM_LR * ((nm / c1) / (jnp.sqrt(nv / c2) + ADAM_EPS) + ADAM_WD * w_ref[...])
        nm_ref[...] = nm
        nv_ref[...] = nv

    spec = pl.BlockSpec((TR, C), lambda i: (i, 0))
    return pl.pallas_call(
        body, name=name, grid=(R // TR,), in_specs=[spec] * 4, out_specs=[spec] * 3,
        out_shape=[jax.ShapeDtypeStruct((R, C), F32)] * 3,
        compiler_params=_params(("parallel",)))(w, g, m, v)


def _row_tile(R, pref):
    t = (pref // 8) * 8
    while t >= 8:
        if R % t == 0:
            return t
        t -= 8
    raise ValueError(f"no row tile for {R}")


def _sum_slots(stack, name, TR=512):
    n, R, C = stack.shape
    TR = R if R <= TR else _row_tile(R, TR)

    def body(s_ref, o_ref):
        acc = s_ref[0]
        for k in range(1, n):
            acc = acc + s_ref[k]
        o_ref[...] = acc

    return pl.pallas_call(
        body, name=name, grid=(R // TR,),
        in_specs=[pl.BlockSpec((n, TR, C), lambda i: (0, i, 0))],
        out_specs=pl.BlockSpec((TR, C), lambda i: (i, 0)),
        out_shape=jax.ShapeDtypeStruct((R, C), F32),
        compiler_params=_params(("parallel",)))(stack)


_ANY = pl.BlockSpec(memory_space=pl.ANY)


def _chip_of(j, c):
    return (j // 2, j % 2, c)


def _own_slab(shard, dtype, j_arr, name, first=0, count=None, plus=None, deps=(), TR=512):
    N, r, C = shard.shape
    B = N if count is None else count
    rh = r // 2
    TR = rh if rh <= TR else _row_tile(rh, TR)
    nt = rh // TR
    ins = [shard] if plus is None else [shard, plus]

    def body(j_ref, *refs):
        val = refs[0][...] if plus is None else refs[0][...] + refs[1][...]
        refs[-1][...] = val.astype(refs[-1].dtype)

    return pl.pallas_call(
        body, name=name,
        grid_spec=pltpu.PrefetchScalarGridSpec(
            num_scalar_prefetch=1, grid=(B, 2, nt),
            in_specs=[pl.BlockSpec((None, TR, C), lambda b, h, t, jr: (first + b, h * nt + t, 0))] * len(ins)
            + [_ANY] * len(deps),
            out_specs=pl.BlockSpec((None, None, None, TR, C), lambda b, h, t, jr: (b, jr[0], h, t, 0))),
        out_shape=jax.ShapeDtypeStruct((B, 4, 2, rh, C), dtype),
        compiler_params=_params(("parallel", "parallel", "parallel")),
    )(j_arr, *ins, *deps)


def _gather_weights(bufs, name):
    n = len(bufs)

    def body(*refs):
        buf = refs[n:2 * n]
        send_sems, recv_sems, fsend_sems, frecv_sems = refs[2 * n:]
        x, y, c = lax.axis_index("x"), lax.axis_index("y"), lax.axis_index("c")
        j = 2 * x + y
        sib = (x, y, 1 - c)
        sends = []
        for d in range(1, 4):
            for a in range(n):
                cp = pltpu.make_async_remote_copy(
                    src_ref=buf[a].at[:, j, c], dst_ref=buf[a].at[:, j, c], send_sem=send_sems.at[a, d - 1],
                    recv_sem=recv_sems.at[a, d - 1], device_id=_chip_of((j + d) % 4, c), device_id_type=MESH)
                cp.start()
                sends.append(cp)
        for d in range(1, 4):
            frm = (j + 4 - d) % 4
            for a in range(n):
                pltpu.make_async_remote_copy(
                    src_ref=buf[a].at[:, frm, c], dst_ref=buf[a].at[:, frm, c], send_sem=send_sems.at[a, d - 1],
                    recv_sem=recv_sems.at[a, d - 1], device_id=_chip_of(frm, c), device_id_type=MESH).wait_recv()
                cp = pltpu.make_async_remote_copy(
                    src_ref=buf[a].at[:, frm, c], dst_ref=buf[a].at[:, frm, c], send_sem=fsend_sems.at[a, d - 1],
                    recv_sem=frecv_sems.at[a, d - 1], device_id=sib, device_id_type=MESH)
                cp.start()
                sends.append(cp)
        for d in range(1, 4):
            frm = (j + 4 - d) % 4
            for a in range(n):
                pltpu.make_async_remote_copy(
                    src_ref=buf[a].at[:, frm, 1 - c], dst_ref=buf[a].at[:, frm, 1 - c], send_sem=fsend_sems.at[a, d - 1],
                    recv_sem=frecv_sems.at[a, d - 1], device_id=sib, device_id_type=MESH).wait_recv()
        for cp in sends:
            cp.wait_send()

    return pl.pallas_call(
        body, name=name,
        in_specs=[_ANY] * n, out_specs=[_ANY] * n,
        out_shape=[jax.ShapeDtypeStruct(b.shape, b.dtype) for b in bufs],
        scratch_shapes=[pltpu.SemaphoreType.DMA((n, 3))] * 4,
        input_output_aliases={a: a for a in range(n)},
    )(*bufs)


_HBM = pl.BlockSpec(memory_space=pltpu.HBM)
_SEM = pl.BlockSpec(memory_space=pltpu.SEMAPHORE)
_DATAFLOW = pltpu.SideEffectType.DATAFLOW_SIDE_EFFECTING


def _in_hbm(arrays):
    return [pltpu.with_memory_space_constraint(a, pltpu.HBM) for a in arrays]


def _start_copies(bufs, plan, count, name, deps=()):
    n, k = len(bufs), len(deps)

    def body(*refs):
        send_sems, recv_sems = refs[n + k], refs[n + k + 1]
        for i, (src, dst, dev) in enumerate(plan(refs[:n], False)):
            pltpu.make_async_remote_copy(src_ref=src, dst_ref=dst, send_sem=send_sems.at[i], recv_sem=recv_sems.at[i],
                                         device_id=dev, device_id_type=MESH).start()
        refs[-1][...] = jnp.zeros_like(refs[-1])

    return pl.pallas_call(
        body, name=name,
        out_shape=(pltpu.SemaphoreType.DMA((count,)), pltpu.SemaphoreType.DMA((count,)),
                   *[pltpu.HBM(b.shape, b.dtype) for b in bufs], jax.ShapeDtypeStruct((8, 128), F32)),
        in_specs=[_HBM] * n + [_ANY] * k,
        out_specs=(_SEM, _SEM, *[_HBM] * n, pl.BlockSpec(memory_space=pltpu.VMEM)),
        input_output_aliases={a: 2 + a for a in range(n)},
        compiler_params=pltpu.CompilerParams(has_side_effects=_DATAFLOW),
    )(*_in_hbm(bufs), *deps)


def _wait_copies(handle, plan, afters, name):
    send_sems, recv_sems, *bufs = handle[:-1]
    n = len(bufs)

    def body(*refs):
        send_sems, recv_sems = refs[n], refs[n + 1]
        for i, (src, dst, dev) in enumerate(plan(refs[:n], True)):
            cp = pltpu.make_async_remote_copy(src_ref=src, dst_ref=dst, send_sem=send_sems.at[i], recv_sem=recv_sems.at[i],
                                              device_id=dev, device_id_type=MESH)
            cp.wait_send()
            cp.wait_recv()

    return list(pl.pallas_call(
        body, name=name,
        out_shape=[pltpu.HBM(b.shape, b.dtype) for b in bufs],
        in_specs=[_HBM] * n + [_SEM, _SEM] + [_ANY] * len(afters), out_specs=[_HBM] * n,
        input_output_aliases={a: a for a in range(n)},
        compiler_params=pltpu.CompilerParams(has_side_effects=_DATAFLOW),
    )(*bufs, send_sems, recv_sems, *afters))


def _gather_plan(buf, waiting):
    c = lax.axis_index("c")
    j = 2 * lax.axis_index("x") + lax.axis_index("y")
    copies = []
    for d in range(1, 4):
        to, frm = (j + d) % 4, (j + 4 - d) % 4
        for b in buf:
            copies.append((b.at[:, j, c], b.at[:, frm if waiting else j, c], _chip_of(frm if waiting else to, c)))
    return copies


def _chip_plan(buf, waiting):
    n = len(buf) // 2
    c = lax.axis_index("c")
    j = 2 * lax.axis_index("x") + lax.axis_index("y")
    copies = []
    for d in range(1, 4):
        to = (j + d) % 4
        for a in range(n):
            copies.append((buf[a].at[to], buf[n + a].at[d - 1], _chip_of(to, c)))
    return copies


def _pair_plan(buf, waiting):
    n = len(buf) // 2
    c = lax.axis_index("c")
    sib = (lax.axis_index("x"), lax.axis_index("y"), 1 - c)
    return [(buf[a].at[:, pl.ds(1 - c, 1)], buf[n + a], sib) for a in range(n)]


def _gather_forward(bufs, name):
    n = len(bufs)

    def body(*refs):
        buf = refs[n:2 * n]
        send_sems, recv_sems = refs[2 * n:]
        x, y, c = lax.axis_index("x"), lax.axis_index("y"), lax.axis_index("c")
        j = 2 * x + y
        sib = (x, y, 1 - c)
        sends = []
        for d in range(1, 4):
            frm = (j + 4 - d) % 4
            for a in range(n):
                cp = pltpu.make_async_remote_copy(
                    src_ref=buf[a].at[:, frm, c], dst_ref=buf[a].at[:, frm, c], send_sem=send_sems.at[a, d - 1],
                    recv_sem=recv_sems.at[a, d - 1], device_id=sib, device_id_type=MESH)
                cp.start()
                sends.append(cp)
        for d in range(1, 4):
            frm = (j + 4 - d) % 4
            for a in range(n):
                pltpu.make_async_remote_copy(
                    src_ref=buf[a].at[:, frm, 1 - c], dst_ref=buf[a].at[:, frm, 1 - c], send_sem=send_sems.at[a, d - 1],
                    recv_sem=recv_sems.at[a, d - 1], device_id=sib, device_id_type=MESH).wait_recv()
        for cp in sends:
            cp.wait_send()

    return pl.pallas_call(
        body, name=name,
        in_specs=[_ANY] * n, out_specs=[_ANY] * n,
        out_shape=[jax.ShapeDtypeStruct(b.shape, b.dtype) for b in bufs],
        scratch_shapes=[pltpu.SemaphoreType.DMA((n, 3))] * 2,
        input_output_aliases={a: a for a in range(n)},
    )(*bufs)


def _pair_add(g4, r1, cj_arr, name, TR=512):
    B4, _, rh, C = g4.shape
    B = B4 // 4
    TR = rh if rh <= TR else _row_tile(rh, TR)

    def body(cj_ref, g_ref, r_ref, o16_ref, own_ref):
        s = g_ref[...] + r_ref[...]
        o16_ref[...] = s.astype(BF16)

        @pl.when(pl.program_id(2) == cj_ref[1])
        def _():
            own_ref[...] = s

    return pl.pallas_call(
        body, name=name,
        grid_spec=pltpu.PrefetchScalarGridSpec(
            num_scalar_prefetch=1, grid=(B, rh // TR, 4),
            in_specs=[pl.BlockSpec((None, None, TR, C), lambda b, t, p, cj: (b * 4 + p, cj[0], t, 0)),
                      pl.BlockSpec((None, None, TR, C), lambda b, t, p, cj: (b * 4 + p, 0, t, 0))],
            out_specs=[pl.BlockSpec((None, None, TR, C), lambda b, t, p, cj: (p, b, t, 0)),
                       pl.BlockSpec((None, TR, C), lambda b, t, p, cj: (b, t, 0))]),
        out_shape=[jax.ShapeDtypeStruct((4, B, rh, C), BF16), jax.ShapeDtypeStruct((B, rh, C), F32)],
        compiler_params=_params(("parallel", "parallel", "arbitrary")),
    )(cj_arr, g4, r1)


def _chip_add(own, r2, cj_arr, into, first, name, TR=512):
    B, rh, C = own.shape
    TR = rh if rh <= TR else _row_tile(rh, TR)

    def body(cj_ref, p_ref, r_ref, _into_ref, o_ref):
        o_ref[...] = p_ref[...] + r_ref[0].astype(F32) + r_ref[1].astype(F32) + r_ref[2].astype(F32)

    return pl.pallas_call(
        body, name=name,
        grid_spec=pltpu.PrefetchScalarGridSpec(
            num_scalar_prefetch=1, grid=(B, rh // TR),
            in_specs=[pl.BlockSpec((None, TR, C), lambda b, t, cj: (b, t, 0)),
                      pl.BlockSpec((3, None, TR, C), lambda b, t, cj: (0, b, t, 0)),
                      _ANY],
            out_specs=pl.BlockSpec((None, None, TR, C), lambda b, t, cj: (first + b, cj[0], t, 0))),
        out_shape=jax.ShapeDtypeStruct(into.shape, F32),
        input_output_aliases={3: 0},
        compiler_params=_params(("parallel", "parallel")),
    )(cj_arr, own, r2, into)


def _pair_share(bufs, name):
    n = len(bufs)

    def body(*refs):
        buf = refs[n:2 * n]
        send_sems, recv_sems = refs[2 * n:]
        c = lax.axis_index("c")
        sib = (lax.axis_index("x"), lax.axis_index("y"), 1 - c)
        cps = []
        for a in range(n):
            cp = pltpu.make_async_remote_copy(
                src_ref=buf[a].at[:, c], dst_ref=buf[a].at[:, c], send_sem=send_sems.at[a],
                recv_sem=recv_sems.at[a], device_id=sib, device_id_type=MESH)
            cp.start()
            cps.append(cp)
        for a in range(n):
            pltpu.make_async_remote_copy(
                src_ref=buf[a].at[:, 1 - c], dst_ref=buf[a].at[:, 1 - c], send_sem=send_sems.at[a],
                recv_sem=recv_sems.at[a], device_id=sib, device_id_type=MESH).wait_recv()
        for cp in cps:
            cp.wait_send()

    return pl.pallas_call(
        body, name=name, in_specs=[_ANY] * n, out_specs=[_ANY] * n,
        out_shape=[jax.ShapeDtypeStruct(b.shape, b.dtype) for b in bufs],
        scratch_shapes=[pltpu.SemaphoreType.DMA((n,)), pltpu.SemaphoreType.DMA((n,))],
        input_output_aliases={a: a for a in range(n)},
    )(*bufs)


def _pair_swap(arr, name):
    def body(src, dst, send_sem, recv_sem):
        sib = (lax.axis_index("x"), lax.axis_index("y"), 1 - lax.axis_index("c"))
        cp = pltpu.make_async_remote_copy(src_ref=src, dst_ref=dst, send_sem=send_sem, recv_sem=recv_sem,
                                          device_id=sib, device_id_type=MESH)
        cp.start()
        cp.wait_recv()
        cp.wait_send()

    return pl.pallas_call(
        body, name=name, in_specs=[_ANY], out_specs=_ANY,
        out_shape=jax.ShapeDtypeStruct(arr.shape, arr.dtype),
        scratch_shapes=[pltpu.SemaphoreType.DMA, pltpu.SemaphoreType.DMA],
    )(arr)


class _ReduceScatter:
    def __init__(self, n_layers, cj_arr):
        self.L, self.cj = n_layers, cj_arr
        self.total = None
        self.pair = None
        self.chip = None

    def _land(self, after):
        handle, layer, owns = self.chip
        n = len(owns)
        r2 = _wait_copies(handle, _chip_plan, (after,), "rs_chip_wait")[n:]
        if self.total is None:
            self.total = [lax.empty((self.L * o.shape[0], 2) + o.shape[1:], F32) for o in owns]
        self.total = [_chip_add(o, r, self.cj, t, layer * o.shape[0], "rs_chip_add")
                      for o, r, t in zip(owns, r2, self.total)]
        self.chip = None

    def add_layer(self, layer, grads):
        g4 = [g.reshape(g.shape[0] * 4, 2, g.shape[1] // 8, g.shape[2]) for g in grads]
        lands = [lax.empty((g.shape[0], 1) + g.shape[2:], F32) for g in g4]
        handle = _start_copies(g4 + lands, _pair_plan, len(g4), "rs_pair_start")
        self.pair = (handle, layer)
        return (handle[-1],)

    def advance(self, after):
        if self.pair is None:
            return ()
        handle, layer = self.pair
        both = _wait_copies(handle, _pair_plan, (after,), "rs_pair_wait")
        n = len(both) // 2
        added = [_pair_add(g, r, self.cj, "rs_pair_add") for g, r in zip(both[:n], both[n:])]
        parts, owns = [p for p, _ in added], [o for _, o in added]
        if self.chip is not None:
            self._land(owns[-1])
        lands = [lax.empty((3,) + p.shape[1:], p.dtype) for p in parts]
        handle = _start_copies(parts + lands, _chip_plan, 3 * n, "rs_chip_start")
        self.pair, self.chip = None, (handle, layer, owns)
        return (handle[-1],)

    def result(self, after):
        self._land(after)
        full = _pair_share(self.total, "rs_pair_share")
        return [f.reshape(f.shape[0], f.shape[1] * f.shape[2], f.shape[3]) for f in full]


def _relu2_epi(acc):
    return acc, jnp.square(jnp.maximum(acc, 0.0))


def _relu2_bwd_epi(acc, u):
    return (acc * (2.0 * jnp.maximum(u, 0.0)),)


_GRAD_ORDER = ("winT", "wbT", "wout", "wq", "wkv", "woT", "wupT", "wdown")


def _forward_backward(x, mem, target, weights_of, P, grads_done, grads_advance, first_deps=()):
    L = P["g_norm"].shape[0]
    S, D = x.shape
    gn = lambda l, i: P["g_norm"][l, i][None]

    saved = []
    (h,) = _resnorm_fwd(x, None, None, gn(0, 0), "norm_in", deps=first_deps)
    xr = x
    for l in range(L):
        W = weights_of(l, xr)
        proj = _mm(h, W["winT"], "nt", "in_proj", b_pre=(0,), tn=768)
        small = (jnp.broadcast_to(P["sinks"][l][:, None], (8, BLK)), P["ws"][l], P["bs"][l][:, :, None],
                 P["gsgu"][l][None], P["wp"][l], P["ps"][l][None])
        br = _mix_fwd(proj, *small, "mix_fwd")
        pb = lax.empty((3, S, D), F32)
        for n in range(3):
            pb = _mm(br, W["wbT"], "nt", "branch_proj", a_pre=(n,), b_pre=(n,), into=pb, out_pre=(n,))
        merged = _merge_fwd(proj, pb, "merge_fwd")
        z = _mm(merged, W["wout"], "nn", "out_proj", b_pre=(0,))
        x1, hm = _resnorm_fwd(xr, z, gn(l, 1), gn(l, 2), "resnorm_fwd")
        qm = _mm(hm, W["wq"], "nn", "mem_q", b_pre=(0,))
        (memn,) = _resnorm_fwd(mem, None, None, P["g_mem"][l][None], "mem_norm")
        kv = _mm(memn, W["wkv"], "nn", "mem_kv", b_pre=(0,))
        om = _memattn_fwd(qm, kv, "memattn_fwd")
        ym = _mm(om, W["woT"], "nt", "mem_o", b_pre=(0,))
        x2, hf = _resnorm_fwd(x1, ym, gn(l, 3), gn(l, 4), "resnorm_fwd")
        u, a = _mm(hf, W["wupT"], "nt", "mlp_up", b_pre=(0,), out_dtypes=(F32, BF16), epi=_relu2_epi)
        yf = _mm(a, W["wdown"], "nn", "mlp_down", b_pre=(0,))
        saved.append(dict(W=W, x0=xr, h=h, proj=proj, small=small, br=br, pb=pb, merged=merged, z=z, x1=x1, hm=hm,
                          qm=qm, memn=memn, kv=kv, om=om, ym=ym, x2=x2, hf=hf, u=u, a=a, yf=yf))
        if l < L - 1:
            xr, h = _resnorm_fwd(x2, yf, gn(l, 5), gn(l + 1, 0), "resnorm_fwd")
    dres, loss = _final_fwd(saved[-1]["x2"], saved[-1]["yf"], gn(L - 1, 5), target, "loss_head")

    dgn = [[None] * 6 for _ in range(L)]
    dsmall = {k: [None] * L for k in ("g_mem", "sinks", "ws", "bs", "gsgu", "wp", "ps")}
    dh = None
    for l in reversed(range(L)):
        s = saved[l]
        W, G = s["W"], {}
        if l == L - 1:
            dx2, dyf, dgn[l][5] = _resnorm_bwd(s["x2"], s["yf"], gn(l, 5), None, dres, None, "resnorm_bwd_top")
        else:
            dx2, dyf, dgn[l][5], dgn[l + 1][0] = _resnorm_bwd(s["x2"], s["yf"], gn(l, 5), gn(l + 1, 0), dres, dh,
                                                              "resnorm_bwd", deps=deps)
        du = _mm(dyf, W["wdown"], "nt", "mlp_down_dx", b_pre=(0,), out_dtypes=(BF16,), extras=(s["u"],), epi=_relu2_bwd_epi)
        G["wdown"] = _mm(s["a"], dyf, "tn", "mlp_down_dw")[None]
        dhf = _mm(du, W["wupT"], "nn", "mlp_up_dx", b_pre=(0,))
        G["wupT"] = _mm(du, s["hf"], "tn", "mlp_up_dw")[None]
        dx1, dym, dgn[l][3], dgn[l][4] = _resnorm_bwd(s["x1"], s["ym"], gn(l, 3), gn(l, 4), dx2, dhf, "resnorm_bwd")
        dom = _mm(dym, W["woT"], "nn", "mem_o_dx", b_pre=(0,), deps=grads_advance(dx1))
        G["woT"] = _mm(dym, s["om"], "tn", "mem_o_dw")[None]
        dqm, dkv = _memattn_bwd(s["qm"], s["kv"], dom, "memattn_bwd")
        dmemn = _mm(dkv, W["wkv"], "nt", "mem_kv_dx", b_pre=(0,))
        G["wkv"] = _mm(s["memn"], dkv, "tn", "mem_kv_dw")[None]
        _, dsmall["g_mem"][l] = _resnorm_bwd(mem, None, None, P["g_mem"][l][None], None, dmemn, "mem_norm_bwd")
        dhm = _mm(dqm, W["wq"], "nt", "mem_q_dx", b_pre=(0,))
        G["wq"] = _mm(s["hm"], dqm, "tn", "mem_q_dw")[None]
        dx0, dz, dgn[l][1], dgn[l][2] = _resnorm_bwd(s["x0"], s["z"], gn(l, 1), gn(l, 2), dx1, dhm, "resnorm_bwd")
        dmerged = _mm(dz, W["wout"], "nt", "out_proj_dx", b_pre=(0,))
        G["wout"] = _mm(s["merged"], dz, "tn", "out_proj_dw")[None]
        dproj, dpb = _merge_bwd(s["proj"], s["pb"], dmerged, "merge_bwd")
        dbr = lax.empty((3, S, 512), F32)
        G["wbT"] = lax.empty(W["wbT"].shape, F32)
        for n in range(3):
            dbr = _mm(dpb, W["wbT"], "nn", "branch_proj_dx", a_pre=(n,), b_pre=(n,), into=dbr, out_pre=(n,))
            G["wbT"] = _mm(dpb, s["br"], "tn", "branch_proj_dw", a_pre=(n,), b_pre=(n,), into=G["wbT"], out_pre=(n,))
        (dproj, dsmall["sinks"][l], dsmall["ws"][l], dsmall["bs"][l], dsmall["gsgu"][l], dsmall["wp"][l],
         dsmall["ps"][l]) = _mix_bwd(s["proj"], dbr, dproj, *s["small"], "mix_bwd")
        dh = _mm(dproj, W["winT"], "nn", "in_proj_dx", b_pre=(0,), tk=768)
        G["winT"] = _mm(dproj, s["h"], "tn", "in_proj_dw", tm=768)[None]
        deps = grads_done(l, [G[k] for k in _GRAD_ORDER])
        dres = dx0
    grad_x, dgn[0][0] = _resnorm_bwd(x, None, None, gn(0, 0), dres, dh, "norm_in_bwd", deps=deps)
    tail_deps = grads_advance(grad_x)

    small_grads = dict(
        g_norm=jnp.stack([jnp.concatenate(row, axis=0) for row in dgn]),
        g_mem=jnp.concatenate(dsmall["g_mem"], axis=0),
        sinks=jnp.stack([d[:, 0] for d in dsmall["sinks"]]),
        ws=jnp.stack(dsmall["ws"]),
        bs=jnp.stack([d[:, :, 0] for d in dsmall["bs"]]),
        gsgu=jnp.concatenate(dsmall["gsgu"], axis=0),
        wp=jnp.stack(dsmall["wp"]),
        ps=jnp.concatenate(dsmall["ps"], axis=0),
    )
    return loss, grad_x, small_grads, tail_deps


_PACK_ROWS = 512


def _as_rows(a):
    n = math.prod(a.shape)
    if n % 128:
        a = jnp.pad(a.reshape(-1), (0, (-n) % 128))
    r = a.reshape(-1, 128)
    return jnp.pad(r, ((0, (-r.shape[0]) % 8), (0, 0))) if r.shape[0] % 8 else r


def _pack(arrays):
    rows = [_as_rows(a) for a in arrays]
    total = sum(r.shape[0] for r in rows)
    tail = (-total) % _PACK_ROWS
    if tail:
        rows.append(jnp.zeros((tail, 128), rows[0].dtype))
    return jnp.concatenate(rows, axis=0)


def _unpack(packed, like):
    out, pos = [], 0
    for a in like:
        n = math.prod(a.shape)
        nr = -(-n // 128)
        rows = packed[pos:pos + nr]
        out.append((rows.reshape(-1)[:n] if n % 128 else rows).reshape(a.shape))
        pos += nr + (-nr) % 8
    return out


_BIG = ("w_in", "w_branch", "w_out", "w_q_mem", "w_kv_mem", "w_o_mem", "w_up", "w_down")
_SMALL = ("g_norm", "g_mem", "attn_sinks", "w_spatial", "b_spatial", "g_sgu", "w_pool", "pool_scale")
_WEIGHTS = ("g_norm", "g_mem", "w_in", "attn_sinks", "w_spatial", "b_spatial", "g_sgu", "w_pool", "pool_scale",
            "w_branch", "w_out", "w_q_mem", "w_kv_mem", "w_o_mem", "w_up", "w_down")


def _to_working(name, w):
    if name == "w_in":
        return jnp.swapaxes(w, 1, 2)
    if name == "w_branch":
        t = jnp.swapaxes(w, 2, 3)
        return t.reshape(t.shape[0] * 3, t.shape[2], t.shape[3])
    if name in ("w_o_mem", "w_up"):
        return jnp.swapaxes(w, 1, 2)
    return w


def _from_working(name, g):
    if name == "w_in":
        return jnp.swapaxes(g, 1, 2)
    if name == "w_branch":
        return jnp.swapaxes(g.reshape(g.shape[0] // 3, 3, g.shape[1], g.shape[2]), 2, 3)
    if name in ("w_o_mem", "w_up"):
        return jnp.swapaxes(g, 1, 2)
    return g


_WKEY = dict(w_in="winT", w_branch="wbT", w_out="wout", w_q_mem="wq", w_kv_mem="wkv", w_o_mem="woT",
             w_up="wupT", w_down="wdown")


def kernel(x, mem, g_norm, g_mem, w_in, attn_sinks, w_spatial, b_spatial, g_sgu, w_pool, pool_scale, w_branch, w_out, w_q_mem, w_kv_mem, w_o_mem, w_up, w_down, loss_target, m_g_norm, m_g_mem, m_w_in, m_attn_sinks, m_w_spatial, m_b_spatial, m_g_sgu, m_w_pool, m_pool_scale, m_w_branch, m_w_out, m_w_q_mem, m_w_kv_mem, m_w_o_mem, m_w_up, m_w_down, v_g_norm, v_g_mem, v_w_in, v_attn_sinks, v_w_spatial, v_b_spatial, v_g_sgu, v_w_pool, v_pool_scale, v_w_branch, v_w_out, v_w_q_mem, v_w_kv_mem, v_w_o_mem, v_w_up, v_w_down):
    w = dict(g_norm=g_norm, g_mem=g_mem, w_in=w_in, attn_sinks=attn_sinks, w_spatial=w_spatial, b_spatial=b_spatial,
             g_sgu=g_sgu, w_pool=w_pool, pool_scale=pool_scale, w_branch=w_branch, w_out=w_out, w_q_mem=w_q_mem,
             w_kv_mem=w_kv_mem, w_o_mem=w_o_mem, w_up=w_up, w_down=w_down)
    m = dict(g_norm=m_g_norm, g_mem=m_g_mem, w_in=m_w_in, attn_sinks=m_attn_sinks, w_spatial=m_w_spatial,
             b_spatial=m_b_spatial, g_sgu=m_g_sgu, w_pool=m_w_pool, pool_scale=m_pool_scale, w_branch=m_w_branch,
             w_out=m_w_out, w_q_mem=m_w_q_mem, w_kv_mem=m_w_kv_mem, w_o_mem=m_w_o_mem, w_up=m_w_up, w_down=m_w_down)
    v = dict(g_norm=v_g_norm, g_mem=v_g_mem, w_in=v_w_in, attn_sinks=v_attn_sinks, w_spatial=v_w_spatial,
             b_spatial=v_b_spatial, g_sgu=v_g_sgu, w_pool=v_w_pool, pool_scale=v_pool_scale, w_branch=v_w_branch,
             w_out=v_w_out, w_q_mem=v_w_q_mem, w_kv_mem=v_w_kv_mem, w_o_mem=v_w_o_mem, w_up=v_w_up, w_down=v_w_down)
    L = g_norm.shape[0]
    j = 2 * lax.axis_index("x") + lax.axis_index("y")
    c = lax.axis_index("c")
    j_arr = jnp.reshape(j, (1,)).astype(jnp.int32)
    cj_arr = jnp.stack([c, j]).astype(jnp.int32)

    gs = g_norm.shape[2]
    working = [_to_working(n, w[n]) for n in _BIG]
    per_layer = [wk.shape[0] // L for wk in working]

    def own_slabs(l):
        return [_own_slab(wk, BF16, j_arr, "own_slab", first=l * b, count=b) for wk, b in zip(working, per_layer)]

    first_bufs = own_slabs(0) + [_own_slab(g_norm.reshape(1, L * 6 * gs // 128, 128), F32, j_arr, "own_slab_norm")]
    first_handle = _start_copies(first_bufs, _gather_plan, 3 * len(first_bufs), "gather_start_first")
    slabs = {l: own_slabs(l) for l in range(1, L)}
    first = _gather_forward(_wait_copies(first_handle, _gather_plan, [s for l in slabs for s in slabs[l]],
                                         "gather_wait_first"), "gather_forward_first")
    in_flight, dep = {}, first[0]
    for l in range(1, L):
        in_flight[l] = _start_copies(slabs[l], _gather_plan, 3 * len(slabs[l]), "gather_start", deps=(dep,))
        dep = in_flight[l][-1]
    gn_full = jnp.transpose(first[-1].reshape(4, L * 6, gs), (1, 0, 2)).reshape(L, 6, 4 * gs)
    P = dict(g_norm=gn_full, g_mem=g_mem, sinks=attn_sinks, ws=w_spatial, bs=b_spatial, gsgu=g_sgu, wp=w_pool,
             ps=pool_scale)

    def weights_of(l, after):
        got = first[:-1] if l == 0 else _gather_forward(
            _wait_copies(in_flight[l], _gather_plan, (after,), "gather_wait"), "gather_forward")
        return {k: g.reshape(g.shape[0], 8 * g.shape[3], g.shape[4]) for k, g in zip(_GRAD_ORDER, got)}

    rs = _ReduceScatter(L, cj_arr)
    loss_part, grad_x, sg, tail_deps = _forward_backward(
        x[0], mem[0], loss_target[0], weights_of, P, rs.add_layer, rs.advance,
        first_deps=[h[-1] for h in in_flight.values()])
    loss = lax.psum(loss_part[0, 0], ("x", "y", "c"))

    full_small = [sg["g_norm"], sg["g_mem"], sg["sinks"], sg["ws"], sg["bs"], sg["gsgu"], sg["wp"], sg["ps"]]
    packed = _pack(full_small)
    pair_sum = _own_slab(packed[None], F32, j_arr, "small_grads_pair_sum", deps=tail_deps,
                         plus=_pair_swap(packed, "small_grads_swap")[None])
    (chip_sums,) = _gather_weights([pair_sum], "gather_small_grads")
    total = _sum_slots(chip_sums.reshape(4, *packed.shape), "sum_small_grads")
    grads = {n: _from_working(n, g) for n, g in zip(_BIG, rs.result(total))}
    for n, g in zip(_SMALL, _unpack(total, full_small)):
        grads[n] = lax.dynamic_slice_in_dim(g, j * g_norm.shape[2], g_norm.shape[2], axis=2) if n == "g_norm" else g

    delta, new_m, new_v = {}, {}, {}
    for n in _BIG:
        view = (lambda t: jnp.swapaxes(t, 1, 2)) if n == "w_in" else (lambda t: t)
        shp = view(w[n]).shape
        two_d = lambda t: view(t).reshape(-1, shp[-1])
        d_, m_, v_ = _adamw(two_d(w[n]), two_d(grads[n]), two_d(m[n]), two_d(v[n]), "adamw")
        delta[n], new_m[n], new_v[n] = view(d_.reshape(shp)), view(m_.reshape(shp)), view(v_.reshape(shp))
    small_w = [w[n] for n in _SMALL]
    d_, m_, v_ = _adamw(_pack(small_w), _pack([grads[n] for n in _SMALL]), _pack([m[n] for n in _SMALL]),
                        _pack([v[n] for n in _SMALL]), "adamw_small")
    for n, dd, mm_, vv in zip(_SMALL, _unpack(d_, small_w), _unpack(m_, small_w), _unpack(v_, small_w)):
        delta[n], new_m[n], new_v[n] = dd, mm_, vv

    return (loss, grad_x[None], *[grads[n] for n in _WEIGHTS], *[delta[n] for n in _WEIGHTS],
            *[new_m[n] for n in _WEIGHTS], *[new_v[n] for n in _WEIGHTS])
```

```python
import functools
import math

import jax
import jax.numpy as jnp
from jax import lax
from jax.experimental import pallas as pl
from jax.experimental.pallas import tpu as pltpu

F32 = jnp.float32
BF16 = jnp.bfloat16
MESH = pl.DeviceIdType.MESH

EPS = 1e-6
NEG_INF = -1e30
BLK = 128
HALO = 16
POOL_WINDOWS = (2, 4, 8, 16)
ATT_SCALE = 1.0 / math.sqrt(64.0)
MEM_SCALE = 1.0 / math.sqrt(128.0)
C_Q, C_K, C_V, C_SU, C_SV, C_PC, C_GATE, C_END = 0, 512, 640, 768, 1280, 1792, 2304, 5376

ADAM_LR, ADAM_B1, ADAM_B2, ADAM_EPS, ADAM_WD, ADAM_STEP = 0.001, 0.9, 0.999, 1e-08, 0.01, 10

VMEM_LIMIT_BYTES = 56 * 1024 * 1024

_DIMS = {
    "nn": (((1,), (0,)), ((), ())),
    "nt": (((1,), (1,)), ((), ())),
    "tn": (((0,), (0,)), ((), ())),
}


def _dot(a, b, mode):
    return lax.dot_general(a, b, _DIMS[mode], preferred_element_type=F32)


def _params(semantics):
    return pltpu.CompilerParams(dimension_semantics=semantics, vmem_limit_bytes=VMEM_LIMIT_BYTES)


def _tile(dim, pref):
    if dim <= pref:
        return dim
    t = (pref // 128) * 128
    while t >= 128:
        if dim % t == 0:
            return t
        t -= 128
    raise ValueError(f"no tile for {dim}")


def _rms(x, g):
    return x * lax.rsqrt(jnp.mean(x * x, axis=-1, keepdims=True) + EPS) * g


def _mm(a, b, mode, name, *, out_dtypes=(F32,), a_pre=(), b_pre=(), into=None, out_pre=(),
        extras=(), epi=None, deps=(), tm=2048, tn=1024, tk=1024):
    a2, b2 = a.shape[len(a_pre):], b.shape[len(b_pre):]
    if mode == "nn":
        (M, K), (K2, N) = a2, b2
    elif mode == "nt":
        (M, K), (N, K2) = a2, b2
    else:
        (K, M), (K2, N) = a2, b2
    assert K == K2, (a.shape, b.shape, mode)
    tm, tn, tk = _tile(M, tm), _tile(N, tn), _tile(K, tk)
    nk = K // tk
    na, nb_, no = len(a_pre), len(b_pre), len(out_pre)
    if mode == "tn":
        a_spec = pl.BlockSpec((None,) * na + (tk, tm), lambda i, j, k: a_pre + (k, i))
    else:
        a_spec = pl.BlockSpec((None,) * na + (tm, tk), lambda i, j, k: a_pre + (i, k))
    if mode == "nt":
        b_spec = pl.BlockSpec((None,) * nb_ + (tn, tk), lambda i, j, k: b_pre + (j, k))
    else:
        b_spec = pl.BlockSpec((None,) * nb_ + (tk, tn), lambda i, j, k: b_pre + (k, j))
    tile_spec = pl.BlockSpec((tm, tn), lambda i, j, k: (i, j))
    ne, nout = len(extras), len(out_dtypes)
    in_specs = [a_spec, b_spec] + [tile_spec] * ne
    operands = [a, b, *extras]
    aliases = {}
    if into is not None:
        assert nout == 1
        in_specs.append(pl.BlockSpec(memory_space=pl.ANY))
        operands.append(into)
        aliases = {len(operands) - 1: 0}
        out_shape = [jax.ShapeDtypeStruct(into.shape, into.dtype)]
        out_specs = [pl.BlockSpec((None,) * no + (tm, tn), lambda i, j, k: out_pre + (i, j))]
    else:
        out_shape = [jax.ShapeDtypeStruct((M, N), dt) for dt in out_dtypes]
        out_specs = [tile_spec] * nout
    in_specs += [pl.BlockSpec(memory_space=pl.ANY)] * len(deps)
    operands += list(deps)

    def body(*refs):
        a_ref, b_ref = refs[0], refs[1]
        ex = refs[2:2 + ne]
        pos = 2 + ne + (1 if into is not None else 0) + len(deps)
        outs = refs[pos:pos + nout]
        acc_ref = refs[pos + nout] if nk > 1 else None

        def finish(acc):
            vals = epi(acc, *[e[...] for e in ex]) if epi is not None else (acc,)
            for o, v in zip(outs, vals):
                o[...] = v.astype(o.dtype)

        def prod():
            return _dot(a_ref[...].astype(BF16), b_ref[...].astype(BF16), mode)

        if nk == 1:
            finish(prod())
        else:
            k = pl.program_id(2)

            @pl.when(k == 0)
            def _():
                acc_ref[...] = jnp.zeros_like(acc_ref)

            acc_ref[...] += prod()

            @pl.when(k == nk - 1)
            def _():
                finish(acc_ref[...])

    res = pl.pallas_call(
        body, name=name, grid=(M // tm, N // tn, nk),
        in_specs=in_specs, out_specs=out_specs, out_shape=out_shape,
        scratch_shapes=[pltpu.VMEM((tm, tn), F32)] if nk > 1 else [],
        input_output_aliases=aliases,
        compiler_params=_params(("parallel", "parallel", "arbitrary")),
    )(*operands)
    return res[0] if nout == 1 else tuple(res)


def _resnorm_fn(has_post, has_pre):
    def f(*a):
        x, k = a[0], 1
        if has_post:
            x, k = x + _rms(a[1], a[2]), 3
        outs = [x]
        if has_pre:
            outs.append(_rms(x, a[k]))
        return tuple(outs)
    return f


def _row_spec(T, W):
    return pl.BlockSpec((T, W), lambda i: (i, 0))


def _par_spec(W):
    return pl.BlockSpec((1, W), lambda i: (0, 0))


def _resnorm_fwd(xr, y, gp, gq, name, T=512, deps=()):
    S, D = xr.shape
    T = min(T, S)
    has_post, has_pre = y is not None, gq is not None
    f = _resnorm_fn(has_post, has_pre)
    ins = [xr] + ([y, gp] if has_post else []) + ([gq] if has_pre else [])
    in_specs = [_row_spec(T, D)] + ([_row_spec(T, D), _par_spec(D)] if has_post else []) + ([_par_spec(D)] if has_pre else [])
    out_shape, out_specs = [], []
    if has_post:
        out_shape.append(jax.ShapeDtypeStruct((S, D), F32)); out_specs.append(_row_spec(T, D))
    if has_pre:
        out_shape.append(jax.ShapeDtypeStruct((S, D), BF16)); out_specs.append(_row_spec(T, D))
    n_in, n_dep = len(ins), len(deps)

    def body(*refs):
        vals = f(*[r[...] for r in refs[:n_in]])
        outs = list(refs[n_in + n_dep:])
        if has_post:
            outs.pop(0)[...] = vals[0]
        if has_pre:
            outs.pop(0)[...] = vals[1].astype(BF16)

    res = pl.pallas_call(body, name=name, grid=(S // T,),
                         in_specs=in_specs + [pl.BlockSpec(memory_space=pl.ANY)] * n_dep, out_specs=out_specs,
                         out_shape=out_shape, compiler_params=_params(("parallel",)))(*ins, *deps)
    return tuple(res)


def _resnorm_bwd(xr, y, gp, gq, dres, dh, name, T=512, deps=()):
    S, D = xr.shape
    T = min(T, S)
    has_post, has_pre, has_res = y is not None, gq is not None, dres is not None
    f = _resnorm_fn(has_post, has_pre)
    ins = [xr] + ([y, gp] if has_post else []) + ([gq] if has_pre else [])
    in_specs = [_row_spec(T, D)] + ([_row_spec(T, D), _par_spec(D)] if has_post else []) + ([_par_spec(D)] if has_pre else [])
    n_prim = len(ins)
    if has_res:
        ins.append(dres); in_specs.append(_row_spec(T, D))
    if has_pre:
        ins.append(dh); in_specs.append(_row_spec(T, D))
    n_in, n_dep = len(ins), len(deps)
    out_shape = [jax.ShapeDtypeStruct((S, D), F32)]
    out_specs = [_row_spec(T, D)]
    if has_post:
        out_shape += [jax.ShapeDtypeStruct((S, D), BF16), jax.ShapeDtypeStruct((1, D), F32)]
        out_specs += [_row_spec(T, D), _par_spec(D)]
    if has_pre:
        out_shape.append(jax.ShapeDtypeStruct((1, D), F32)); out_specs.append(_par_spec(D))

    def body(*refs):
        i = pl.program_id(0)
        prim = [r[...] for r in refs[:n_prim]]
        rest = list(refs[n_prim:n_in])
        ct_x = rest.pop(0)[...] if has_res else jnp.zeros((T, D), F32)
        cts = [ct_x]
        if has_pre:
            cts.append(rest.pop(0)[...].astype(F32))
        _, vjp = jax.vjp(f, *prim)
        grads = list(vjp(tuple(cts)))
        outs = list(refs[n_in + n_dep:])
        outs.pop(0)[...] = grads.pop(0)
        acc = []
        if has_post:
            outs.pop(0)[...] = grads.pop(0).astype(BF16)
            acc.append((outs.pop(0), grads.pop(0)))
        if has_pre:
            acc.append((outs.pop(0), grads.pop(0)))

        @pl.when(i == 0)
        def _():
            for o, _g in acc:
                o[...] = jnp.zeros_like(o)

        for o, g in acc:
            o[...] += g

    res = pl.pallas_call(body, name=name, grid=(S // T,),
                         in_specs=in_specs + [pl.BlockSpec(memory_space=pl.ANY)] * n_dep, out_specs=out_specs,
                         out_shape=out_shape, compiler_params=_params(("arbitrary",)))(*ins, *deps)
    return tuple(res)


def _final_fwd(xr, y, gp, target, name, T=512):
    S, D = xr.shape
    T = min(T, S)

    def body(x_ref, y_ref, g_ref, t_ref, dy_ref, loss_ref):
        i = pl.program_id(0)
        e = x_ref[...] + _rms(y_ref[...], g_ref[...]) - t_ref[...]
        dy_ref[...] = e / D

        @pl.when(i == 0)
        def _():
            loss_ref[...] = jnp.zeros_like(loss_ref)

        loss_ref[...] += 0.5 * jnp.sum(jnp.sum(e * e, axis=-1, keepdims=True) / D, axis=0, keepdims=True)

    return pl.pallas_call(
        body, name=name, grid=(S // T,),
        in_specs=[_row_spec(T, D), _row_spec(T, D), _par_spec(D), _row_spec(T, D)],
        out_specs=[_row_spec(T, D), pl.BlockSpec((1, 128), lambda i: (0, 0))],
        out_shape=[jax.ShapeDtypeStruct((S, D), F32), jax.ShapeDtypeStruct((1, 128), F32)],
        compiler_params=_params(("arbitrary",)))(xr, y, gp, target)


def _lane_lo():
    return lax.broadcasted_iota(jnp.int32, (1, BLK), 1) < 64


def _att_mask(not_first):
    r = lax.broadcasted_iota(jnp.int32, (BLK, 2 * BLK), 0)
    c = lax.broadcasted_iota(jnp.int32, (BLK, 2 * BLK), 1)
    qc, kc = 2 + r // 64, c // 64
    return (kc <= qc) & (kc >= qc - 2) & (not_first | (c >= BLK))


def _softmax_sink(s, sk):
    m = jnp.maximum(jnp.max(s, axis=-1, keepdims=True), sk)
    e = jnp.exp(s - m)
    es = jnp.exp(sk - m)
    z = jnp.sum(e, axis=-1, keepdims=True) + es
    return e / z, es / z


def _att_operands(cur, kvp, t, lo):
    h = t // 2
    qt = cur[:, C_Q + BLK * t:C_Q + BLK * (t + 1)]
    q_lo = jnp.where(lo, qt, 0.0).astype(BF16)
    q_hi = jnp.where(lo, 0.0, qt).astype(BF16)
    kband = jnp.concatenate([kvp[:, 0:BLK], cur[:, C_K:C_K + BLK]], axis=0)
    vband = jnp.concatenate([kvp[:, BLK:2 * BLK], cur[:, C_V:C_V + BLK]], axis=0)
    kroll = pltpu.roll(kband, 64, 1)
    vroll = pltpu.roll(vband, 64, 1)
    ka, kb = (kband, kroll) if h == 0 else (kroll, kband)
    va = jnp.where(lo, vband if h == 0 else vroll, 0.0)
    vb = jnp.where(lo, 0.0, vroll if h == 0 else vband)
    return q_lo, q_hi, ka.astype(BF16), kb.astype(BF16), va.astype(BF16), vb.astype(BF16)


def _sgu_mask():
    r = lax.broadcasted_iota(jnp.int32, (BLK, BLK), 0)
    c = lax.broadcasted_iota(jnp.int32, (BLK, BLK), 1)
    return (c // 64) <= (r // 64)


def _pool_cnt(blk, w):
    t = blk * BLK + lax.broadcasted_iota(jnp.int32, (BLK, 1), 0)
    return jnp.minimum(t + 1, w).astype(F32)


def _mix_in_specs(nb, rev):
    def b(i):
        return nb - 1 - i if rev else i
    return [
        pl.BlockSpec((BLK, C_GATE), lambda i: (b(i), 0)),
        pl.BlockSpec((BLK, 2 * BLK), lambda i: (jnp.maximum(b(i) - 1, 0), C_K // (2 * BLK))),
        pl.BlockSpec((HALO, C_GATE), lambda i: (jnp.maximum(b(i) * (BLK // HALO) - 1, 0), 0)),
        pl.BlockSpec((8, BLK), lambda i: (0, 0)),
        pl.BlockSpec((4, BLK, BLK), lambda i: (0, 0, 0)),
        pl.BlockSpec((4, BLK, 1), lambda i: (0, 0, 0)),
        pl.BlockSpec((1, 512), lambda i: (0, 0)),
        pl.BlockSpec((4, BLK, BLK), lambda i: (0, 0, 0)),
        pl.BlockSpec((1, 512), lambda i: (0, 0)),
    ]


def _mix_fwd(proj, sinks_b, ws, bs3, gsgu, wp, ps, name):
    S = proj.shape[0]
    nb = S // BLK

    def body(cur_ref, kvp_ref, pcp_ref, sk_ref, ws_ref, bs_ref, gs_ref, wp_ref, ps_ref, br_ref, ext_ref):
        i = pl.program_id(0)
        not_first = i > 0
        lo = _lane_lo()
        cur, kvp = cur_ref[...], kvp_ref[...]
        mask = _att_mask(not_first)
        for t in range(4):
            q_lo, q_hi, ka, kb, va, vb = _att_operands(cur, kvp, t, lo)
            s_lo = jnp.where(mask, _dot(q_lo, ka, "nt") * ATT_SCALE, NEG_INF)
            s_hi = jnp.where(mask, _dot(q_hi, kb, "nt") * ATT_SCALE, NEG_INF)
            p_lo, _ = _softmax_sink(s_lo, sk_ref[2 * t:2 * t + 1, 0:1])
            p_hi, _ = _softmax_sink(s_hi, sk_ref[2 * t + 1:2 * t + 2, 0:1])
            o = _dot(p_lo.astype(BF16), va, "nn") + _dot(p_hi.astype(BF16), vb, "nn")
            br_ref[0, :, BLK * t:BLK * (t + 1)] = o.astype(BF16)
        gu = jax.nn.gelu(cur[:, C_SU:C_SV])
        vn = _rms(jax.nn.gelu(cur[:, C_SV:C_PC]), gs_ref[...]).astype(BF16)
        wmask = _sgu_mask()
        for g in range(4):
            wm = jnp.where(wmask, ws_ref[g], 0.0).astype(BF16)
            sp = _dot(wm, vn[:, BLK * g:BLK * (g + 1)], "nn") + bs_ref[g]
            br_ref[1, :, BLK * g:BLK * (g + 1)] = (gu[:, BLK * g:BLK * (g + 1)] * sp).astype(BF16)
        c = cur[:, C_PC:C_GATE]
        ext_ref[0:HALO, :] = jnp.where(not_first, pcp_ref[:, C_PC:C_GATE], 0.0)
        ext_ref[HALO:HALO + BLK, :] = c
        for g, w in enumerate(POOL_WINDOWS):
            sl = slice(BLK * g, BLK * (g + 1))
            acc = ext_ref[HALO:HALO + BLK, sl]
            for k in range(1, w):
                acc = acc + ext_ref[HALO - k:HALO - k + BLK, sl]
            pooled = acc / _pool_cnt(i, w) - c[:, sl]
            mixed = _dot(pooled.astype(BF16), wp_ref[g].astype(BF16), "nn")
            br_ref[2, :, sl] = (mixed * ps_ref[:, sl]).astype(BF16)

    return pl.pallas_call(
        body, name=name, grid=(nb,),
        in_specs=_mix_in_specs(nb, False),
        out_specs=pl.BlockSpec((3, BLK, 512), lambda i: (0, i, 0)),
        out_shape=jax.ShapeDtypeStruct((3, S, 512), BF16),
        scratch_shapes=[pltpu.VMEM((HALO + BLK, 512), F32)],
        compiler_params=_params(("parallel",)),
    )(proj, proj, proj, sinks_b, ws, bs3, gsgu, wp, ps)


def _mix_bwd(proj, dbr, dproj, sinks_b, ws, bs3, gsgu, wp, ps, name):
    S = proj.shape[0]
    nb = S // BLK

    def body(cur_ref, kvp_ref, pcp_ref, sk_ref, ws_ref, bs_ref, gs_ref, wp_ref, ps_ref, dbr_ref, _dproj_in,
             dp_ref, dsk_ref, dws_ref, dbs_ref, dgs_ref, dwp_ref, dps_ref,
             ext_ref, z_ref, ckv_ref, cpc_ref):
        i = pl.program_id(0)
        blk = nb - 1 - i
        not_first = blk > 0
        lo = _lane_lo()

        @pl.when(i == 0)
        def _():
            for r in (dsk_ref, dws_ref, dbs_ref, dgs_ref, dwp_ref, dps_ref, ckv_ref, cpc_ref, z_ref):
                r[...] = jnp.zeros_like(r)

        cur, kvp = cur_ref[...], kvp_ref[...]
        mask = _att_mask(not_first)
        dk_band = jnp.zeros((2 * BLK, BLK), F32)
        dk_roll = jnp.zeros((2 * BLK, BLK), F32)
        dv_band = jnp.zeros((2 * BLK, BLK), F32)
        dv_roll = jnp.zeros((2 * BLK, BLK), F32)
        for t in range(4):
            h = t // 2
            q_lo, q_hi, ka, kb, va, vb = _att_operands(cur, kvp, t, lo)
            do = dbr_ref[0, :, BLK * t:BLK * (t + 1)].astype(BF16)
            dq = jnp.zeros((BLK, BLK), F32)
            for half, (qm, km, vm) in enumerate(((q_lo, ka, va), (q_hi, kb, vb))):
                sk = sk_ref[2 * t + half:2 * t + half + 1, 0:1]
                s = jnp.where(mask, _dot(qm, km, "nt") * ATT_SCALE, NEG_INF)
                p, p_sink = _softmax_sink(s, sk)
                dp = _dot(do, vm, "nt")
                rs = jnp.sum(p * dp, axis=-1, keepdims=True)
                ds = (p * (dp - rs) * ATT_SCALE).astype(BF16)
                dsk_ref[2 * t + half:2 * t + half + 1, :] += jnp.broadcast_to(
                    -jnp.sum(p_sink * rs, axis=0, keepdims=True), (1, BLK))
                dvm = _dot(p.astype(BF16), do, "tn")
                dkm = _dot(ds, qm, "tn")
                dqm = _dot(ds, km, "nn")
                if half == 0:
                    dq = dq + jnp.where(lo, dqm, 0.0)
                    dvm = jnp.where(lo, dvm, 0.0)
                else:
                    dq = dq + jnp.where(lo, 0.0, dqm)
                    dvm = jnp.where(lo, 0.0, dvm)
                if (h == 0) == (half == 0):
                    dk_band, dv_band = dk_band + dkm, dv_band + dvm
                else:
                    dk_roll, dv_roll = dk_roll + dkm, dv_roll + dvm
            dp_ref[:, C_Q + BLK * t:C_Q + BLK * (t + 1)] = dq.astype(BF16)
        dk = dk_band + pltpu.roll(dk_roll, 64, 1)
        dv = dv_band + pltpu.roll(dv_roll, 64, 1)
        dp_ref[:, C_K:C_K + BLK] = (dk[BLK:] + ckv_ref[:, 0:BLK]).astype(BF16)
        dp_ref[:, C_V:C_V + BLK] = (dv[BLK:] + ckv_ref[:, BLK:]).astype(BF16)
        ckv_ref[:, 0:BLK] = dk[:BLK]
        ckv_ref[:, BLK:] = dv[:BLK]
        su, sv = cur[:, C_SU:C_SV], cur[:, C_SV:C_PC]
        gu, vjp_u = jax.vjp(jax.nn.gelu, su)
        vn, vjp_v = jax.vjp(lambda a, g: _rms(jax.nn.gelu(a), g), sv, gs_ref[...])
        vn16 = vn.astype(BF16)
        wmask = _sgu_mask()
        dgu, dvn = [], []
        for g in range(4):
            sl = slice(BLK * g, BLK * (g + 1))
            wm = jnp.where(wmask, ws_ref[g], 0.0).astype(BF16)
            sp = _dot(wm, vn16[:, sl], "nn") + bs_ref[g]
            dyb = dbr_ref[1, :, sl]
            dgu.append(dyb * sp)
            dsp = dyb * gu[:, sl]
            dsp16 = dsp.astype(BF16)
            dvn.append(_dot(wm, dsp16, "tn"))
            dws_ref[g] += jnp.where(wmask, _dot(dsp16, vn16[:, sl], "nt"), 0.0)
            dbs_ref[g] += jnp.sum(dsp, axis=1, keepdims=True)
        (dsu,) = vjp_u(jnp.concatenate(dgu, axis=1))
        dsv, dgs = vjp_v(jnp.concatenate(dvn, axis=1))
        dp_ref[:, C_SU:C_SV] = dsu.astype(BF16)
        dp_ref[:, C_SV:C_PC] = dsv.astype(BF16)
        dgs_ref[...] += dgs
        c = cur[:, C_PC:C_GATE]
        ext_ref[0:HALO, :] = jnp.where(not_first, pcp_ref[:, C_PC:C_GATE], 0.0)
        ext_ref[HALO:HALO + BLK, :] = c
        for g, w in enumerate(POOL_WINDOWS):
            sl = slice(BLK * g, BLK * (g + 1))
            acc = ext_ref[HALO:HALO + BLK, sl]
            for k in range(1, w):
                acc = acc + ext_ref[HALO - k:HALO - k + BLK, sl]
            cnt = _pool_cnt(blk, w)
            pooled16 = (acc / cnt - c[:, sl]).astype(BF16)
            wp16 = wp_ref[g].astype(BF16)
            mixed = _dot(pooled16, wp16, "nn")
            dyc = dbr_ref[2, :, sl]
            dps_ref[:, sl] += jnp.sum(dyc * mixed, axis=0, keepdims=True)
            dmixed16 = (dyc * ps_ref[:, sl]).astype(BF16)
            dwp_ref[g] += _dot(pooled16, dmixed16, "tn")
            dpooled = _dot(dmixed16, wp16, "nt")
            z_ref[HALO:HALO + BLK, sl] = dpooled / cnt
            dext = z_ref[0:HALO + BLK, sl]
            for k in range(1, w):
                dext = dext + z_ref[k:k + HALO + BLK, sl]
            dp_ref[:, C_PC + BLK * g:C_PC + BLK * (g + 1)] = (
                dext[HALO:] - dpooled + jnp.concatenate([jnp.zeros((BLK - HALO, BLK), F32), cpc_ref[:, sl]], axis=0)
            ).astype(BF16)
            cpc_ref[:, sl] = dext[:HALO]

    n_in = 11
    small = [jax.ShapeDtypeStruct((8, BLK), F32), jax.ShapeDtypeStruct((4, BLK, BLK), F32),
             jax.ShapeDtypeStruct((4, BLK, 1), F32), jax.ShapeDtypeStruct((1, 512), F32),
             jax.ShapeDtypeStruct((4, BLK, BLK), F32), jax.ShapeDtypeStruct((1, 512), F32)]
    small_specs = [pl.BlockSpec((8, BLK), lambda i: (0, 0)), pl.BlockSpec((4, BLK, BLK), lambda i: (0, 0, 0)),
                   pl.BlockSpec((4, BLK, 1), lambda i: (0, 0, 0)), pl.BlockSpec((1, 512), lambda i: (0, 0)),
                   pl.BlockSpec((4, BLK, BLK), lambda i: (0, 0, 0)), pl.BlockSpec((1, 512), lambda i: (0, 0))]
    res = pl.pallas_call(
        body, name=name, grid=(nb,),
        in_specs=_mix_in_specs(nb, True) + [
            pl.BlockSpec((3, BLK, 512), lambda i: (0, nb - 1 - i, 0)),
            pl.BlockSpec(memory_space=pl.ANY)],
        out_specs=[pl.BlockSpec((BLK, C_GATE), lambda i: (nb - 1 - i, 0))] + small_specs,
        out_shape=[jax.ShapeDtypeStruct(dproj.shape, dproj.dtype)] + small,
        scratch_shapes=[pltpu.VMEM((HALO + BLK, 512), F32), pltpu.VMEM((2 * HALO + BLK, 512), F32),
                        pltpu.VMEM((BLK, 2 * BLK), F32), pltpu.VMEM((HALO, 512), F32)],
        input_output_aliases={n_in - 1: 0},
        compiler_params=_params(("arbitrary",)),
    )(proj, proj, proj, sinks_b, ws, bs3, gsgu, wp, ps, dbr, dproj)
    return tuple(res)


_GW = 256


def _merge_fwd(proj, pb, name, T=4096):
    S, D = pb.shape[1], pb.shape[2]
    T = min(T, S)

    def body(gate_ref, pb_ref, out_ref, acc_ref):
        n = pl.program_id(2)

        @pl.when(n == 0)
        def _():
            acc_ref[...] = jnp.zeros_like(acc_ref)

        acc_ref[...] += jax.nn.sigmoid(gate_ref[...]) * pb_ref[...]

        @pl.when(n == 2)
        def _():
            out_ref[...] = acc_ref[...].astype(BF16)

    return pl.pallas_call(
        body, name=name, grid=(S // T, D // _GW, 3),
        in_specs=[pl.BlockSpec((T, _GW), lambda i, j, n: (i, C_GATE // _GW + n * (D // _GW) + j)),
                  pl.BlockSpec((None, T, _GW), lambda i, j, n: (n, i, j))],
        out_specs=pl.BlockSpec((T, _GW), lambda i, j, n: (i, j)),
        out_shape=jax.ShapeDtypeStruct((S, D), BF16),
        scratch_shapes=[pltpu.VMEM((T, _GW), F32)],
        compiler_params=_params(("parallel", "parallel", "arbitrary")),
    )(proj, pb)


def _merge_bwd(proj, pb, dmerged, name, T=4096):
    S, D = pb.shape[1], pb.shape[2]
    T = min(T, S)

    def body(gate_ref, pb_ref, dm_ref, dgate_ref, dpb_ref):
        sg = jax.nn.sigmoid(gate_ref[...])
        dm = dm_ref[...]
        dpb_ref[...] = (dm * sg).astype(BF16)
        dgate_ref[...] = (dm * pb_ref[...] * sg * (1.0 - sg)).astype(BF16)

    gate_map = lambda i, n, j: (i, C_GATE // _GW + n * (D // _GW) + j)
    return pl.pallas_call(
        body, name=name, grid=(S // T, 3, D // _GW),
        in_specs=[pl.BlockSpec((T, _GW), gate_map),
                  pl.BlockSpec((None, T, _GW), lambda i, n, j: (n, i, j)),
                  pl.BlockSpec((T, _GW), lambda i, n, j: (i, j))],
        out_specs=[pl.BlockSpec((T, _GW), gate_map),
                   pl.BlockSpec((None, T, _GW), lambda i, n, j: (n, i, j))],
        out_shape=[jax.ShapeDtypeStruct((S, C_END), BF16), jax.ShapeDtypeStruct((3, S, D), BF16)],
        compiler_params=_params(("parallel", "parallel", "parallel")),
    )(proj, pb, dmerged)


def _memattn_fwd(qm, kv, name, T=512):
    S, NM = qm.shape[0], kv.shape[0]
    T = min(T, S)

    def body(q_ref, kv_ref, o_ref):
        for h in range(4):
            sl = slice(128 * h, 128 * (h + 1))
            k = kv_ref[:, sl].astype(BF16)
            v = kv_ref[:, 512 + 128 * h:512 + 128 * (h + 1)].astype(BF16)
            s = _dot(q_ref[:, sl].astype(BF16), k, "nt") * MEM_SCALE
            p = jax.nn.softmax(s, axis=-1)
            o_ref[:, sl] = _dot(p.astype(BF16), v, "nn").astype(BF16)

    return pl.pallas_call(
        body, name=name, grid=(S // T,),
        in_specs=[_row_spec(T, 512), pl.BlockSpec((NM, 1024), lambda i: (0, 0))],
        out_specs=_row_spec(T, 512), out_shape=jax.ShapeDtypeStruct((S, 512), BF16),
        compiler_params=_params(("parallel",)))(qm, kv)


def _memattn_bwd(qm, kv, dom, name, T=512):
    S, NM = qm.shape[0], kv.shape[0]
    T = min(T, S)

    def body(q_ref, kv_ref, do_ref, dq_ref, dkv_ref):
        i = pl.program_id(0)

        @pl.when(i == 0)
        def _():
            dkv_ref[...] = jnp.zeros_like(dkv_ref)

        for h in range(4):
            sl = slice(128 * h, 128 * (h + 1))
            sv_ = slice(512 + 128 * h, 512 + 128 * (h + 1))
            q = q_ref[:, sl].astype(BF16)
            k = kv_ref[:, sl].astype(BF16)
            v = kv_ref[:, sv_].astype(BF16)
            do = do_ref[:, sl].astype(BF16)
            p = jax.nn.softmax(_dot(q, k, "nt") * MEM_SCALE, axis=-1)
            dp = _dot(do, v, "nt")
            ds = (p * (dp - jnp.sum(p * dp, axis=-1, keepdims=True)) * MEM_SCALE).astype(BF16)
            dq_ref[:, sl] = _dot(ds, k, "nn").astype(BF16)
            dkv_ref[:, sl] += _dot(ds, q, "tn")
            dkv_ref[:, sv_] += _dot(p.astype(BF16), do, "tn")

    return pl.pallas_call(
        body, name=name, grid=(S // T,),
        in_specs=[_row_spec(T, 512), pl.BlockSpec((NM, 1024), lambda i: (0, 0)), _row_spec(T, 512)],
        out_specs=[_row_spec(T, 512), pl.BlockSpec((NM, 1024), lambda i: (0, 0))],
        out_shape=[jax.ShapeDtypeStruct((S, 512), BF16), jax.ShapeDtypeStruct((NM, 1024), F32)],
        compiler_params=_params(("arbitrary",)))(qm, kv, dom)


def _adamw(w, g, m, v, name, TR=512):
    R, C = w.shape
    TR = R if R <= TR else _row_tile(R, TR)
    c1 = 1.0 - ADAM_B1 ** ADAM_STEP
    c2 = 1.0 - ADAM_B2 ** ADAM_STEP

    def body(w_ref, g_ref, m_ref, v_ref, d_ref, nm_ref, nv_ref):
        gv = g_ref[...]
        nm = ADAM_B1 * m_ref[...] + (1.0 - ADAM_B1) * gv
        nv = ADAM_B2 * v_ref[...] + (1.0 - ADAM_B2) * jnp.square(gv)
        d_ref[...] = -ADAM_LR * ((nm / c1) / (jnp.sqrt(nv / c2) + ADAM_EPS) + ADAM_WD * w_ref[...])
        nm_ref[...] = nm
        nv_ref[...] = nv

    spec = pl.BlockSpec((TR, C), lambda i: (i, 0))
    return pl.pallas_call(
        body, name=name, grid=(R // TR,), in_specs=[spec] * 4, out_specs=[spec] * 3,
        out_shape=[jax.ShapeDtypeStruct((R, C), F32)] * 3,
        compiler_params=_params(("parallel",)))(w, g, m, v)


def _row_tile(R, pref):
    t = (pref // 8) * 8
    while t >= 8:
        if R % t == 0:
            return t
        t -= 8
    raise ValueError(f"no row tile for {R}")


def _sum_slots(stack, name, TR=512):
    n, R, C = stack.shape
    TR = R if R <= TR else _row_tile(R, TR)

    def body(s_ref, o_ref):
        acc = s_ref[0]
        for k in range(1, n):
            acc = acc + s_ref[k]
        o_ref[...] = acc

    return pl.pallas_call(
        body, name=name, grid=(R // TR,),
        in_specs=[pl.BlockSpec((n, TR, C), lambda i: (0, i, 0))],
        out_specs=pl.BlockSpec((TR, C), lambda i: (i, 0)),
        out_shape=jax.ShapeDtypeStruct((R, C), F32),
        compiler_params=_params(("parallel",)))(stack)


_ANY = pl.BlockSpec(memory_space=pl.ANY)


def _chip_of(j, c):
    return (j // 2, j % 2, c)


def _own_slab(shard, dtype, j_arr, name, first=0, count=None, plus=None, deps=(), TR=512):
    N, r, C = shard.shape
    B = N if count is None else count
    rh = r // 2
    TR = rh if rh <= TR else _row_tile(rh, TR)
    nt = rh // TR
    ins = [shard] if plus is None else [shard, plus]

    def body(j_ref, *refs):
        val = refs[0][...] if plus is None else refs[0][...] + refs[1][...]
        refs[-1][...] = val.astype(refs[-1].dtype)

    return pl.pallas_call(
        body, name=name,
        grid_spec=pltpu.PrefetchScalarGridSpec(
            num_scalar_prefetch=1, grid=(B, 2, nt),
            in_specs=[pl.BlockSpec((None, TR, C), lambda b, h, t, jr: (first + b, h * nt + t, 0))] * len(ins)
            + [_ANY] * len(deps),
            out_specs=pl.BlockSpec((None, None, None, TR, C), lambda b, h, t, jr: (b, jr[0], h, t, 0))),
        out_shape=jax.ShapeDtypeStruct((B, 4, 2, rh, C), dtype),
        compiler_params=_params(("parallel", "parallel", "parallel")),
    )(j_arr, *ins, *deps)


def _gather_weights(bufs, name):
    n = len(bufs)

    def body(*refs):
        buf = refs[n:2 * n]
        send_sems, recv_sems, fsend_sems, frecv_sems = refs[2 * n:]
        x, y, c = lax.axis_index("x"), lax.axis_index("y"), lax.axis_index("c")
        j = 2 * x + y
        sib = (x, y, 1 - c)
        sends = []
        for d in range(1, 4):
            for a in range(n):
                cp = pltpu.make_async_remote_copy(
                    src_ref=buf[a].at[:, j, c], dst_ref=buf[a].at[:, j, c], send_sem=send_sems.at[a, d - 1],
                    recv_sem=recv_sems.at[a, d - 1], device_id=_chip_of((j + d) % 4, c), device_id_type=MESH)
                cp.start()
                sends.append(cp)
        for d in range(1, 4):
            frm = (j + 4 - d) % 4
            for a in range(n):
                pltpu.make_async_remote_copy(
                    src_ref=buf[a].at[:, frm, c], dst_ref=buf[a].at[:, frm, c], send_sem=send_sems.at[a, d - 1],
                    recv_sem=recv_sems.at[a, d - 1], device_id=_chip_of(frm, c), device_id_type=MESH).wait_recv()
                cp = pltpu.make_async_remote_copy(
                    src_ref=buf[a].at[:, frm, c], dst_ref=buf[a].at[:, frm, c], send_sem=fsend_sems.at[a, d - 1],
                    recv_sem=frecv_sems.at[a, d - 1], device_id=sib, device_id_type=MESH)
                cp.start()
                sends.append(cp)
        for d in range(1, 4):
            frm = (j + 4 - d) % 4
            for a in range(n):
                pltpu.make_async_remote_copy(
                    src_ref=buf[a].at[:, frm, 1 - c], dst_ref=buf[a].at[:, frm, 1 - c], send_sem=fsend_sems.at[a, d - 1],
                    recv_sem=frecv_sems.at[a, d - 1], device_id=sib, device_id_type=MESH).wait_recv()
        for cp in sends:
            cp.wait_send()

    return pl.pallas_call(
        body, name=name,
        in_specs=[_ANY] * n, out_specs=[_ANY] * n,
        out_shape=[jax.ShapeDtypeStruct(b.shape, b.dtype) for b in bufs],
        scratch_shapes=[pltpu.SemaphoreType.DMA((n, 3))] * 4,
        input_output_aliases={a: a for a in range(n)},
    )(*bufs)


_HBM = pl.BlockSpec(memory_space=pltpu.HBM)
_SEM = pl.BlockSpec(memory_space=pltpu.SEMAPHORE)
_DATAFLOW = pltpu.SideEffectType.DATAFLOW_SIDE_EFFECTING


def _in_hbm(arrays):
    return [pltpu.with_memory_space_constraint(a, pltpu.HBM) for a in arrays]


def _start_copies(bufs, plan, count, name, deps=()):
    n, k = len(bufs), len(deps)

    def body(*refs):
        send_sems, recv_sems = refs[n + k], refs[n + k + 1]
        for i, (src, dst, dev) in enumerate(plan(refs[:n], False)):
            pltpu.make_async_remote_copy(src_ref=src, dst_ref=dst, send_sem=send_sems.at[i], recv_sem=recv_sems.at[i],
                                         device_id=dev, device_id_type=MESH).start()
        refs[-1][...] = jnp.zeros_like(refs[-1])

    return pl.pallas_call(
        body, name=name,
        out_shape=(pltpu.SemaphoreType.DMA((count,)), pltpu.SemaphoreType.DMA((count,)),
                   *[pltpu.HBM(b.shape, b.dtype) for b in bufs], jax.ShapeDtypeStruct((8, 128), F32)),
        in_specs=[_HBM] * n + [_ANY] * k,
        out_specs=(_SEM, _SEM, *[_HBM] * n, pl.BlockSpec(memory_space=pltpu.VMEM)),
        input_output_aliases={a: 2 + a for a in range(n)},
        compiler_params=pltpu.CompilerParams(has_side_effects=_DATAFLOW),
    )(*_in_hbm(bufs), *deps)


def _wait_copies(handle, plan, afters, name):
    send_sems, recv_sems, *bufs = handle[:-1]
    n = len(bufs)

    def body(*refs):
        send_sems, recv_sems = refs[n], refs[n + 1]
        for i, (src, dst, dev) in enumerate(plan(refs[:n], True)):
            cp = pltpu.make_async_remote_copy(src_ref=src, dst_ref=dst, send_sem=send_sems.at[i], recv_sem=recv_sems.at[i],
                                              device_id=dev, device_id_type=MESH)
            cp.wait_send()
            cp.wait_recv()

    return list(pl.pallas_call(
        body, name=name,
        out_shape=[pltpu.HBM(b.shape, b.dtype) for b in bufs],
        in_specs=[_HBM] * n + [_SEM, _SEM] + [_ANY] * len(afters), out_specs=[_HBM] * n,
        input_output_aliases={a: a for a in range(n)},
        compiler_params=pltpu.CompilerParams(has_side_effects=_DATAFLOW),
    )(*bufs, send_sems, recv_sems, *afters))


def _gather_plan(buf, waiting):
    c = lax.axis_index("c")
    j = 2 * lax.axis_index("x") + lax.axis_index("y")
    copies = []
    for d in range(1, 4):
        to, frm = (j + d) % 4, (j + 4 - d) % 4
        for b in buf:
            copies.append((b.at[:, j, c], b.at[:, frm if waiting else j, c], _chip_of(frm if waiting else to, c)))
    return copies


def _chip_plan(buf, waiting):
    n = len(buf) // 2
    c = lax.axis_index("c")
    j = 2 * lax.axis_index("x") + lax.axis_index("y")
    copies = []
    for d in range(1, 4):
        to = (j + d) % 4
        for a in range(n):
            copies.append((buf[a].at[to], buf[n + a].at[d - 1], _chip_of(to, c)))
    return copies


def _pair_plan(buf, waiting):
    n = len(buf) // 2
    c = lax.axis_index("c")
    sib = (lax.axis_index("x"), lax.axis_index("y"), 1 - c)
    return [(buf[a].at[:, pl.ds(1 - c, 1)], buf[n + a], sib) for a in range(n)]


def _gather_forward(bufs, name):
    n = len(bufs)

    def body(*refs):
        buf = refs[n:2 * n]
        send_sems, recv_sems = refs[2 * n:]
        x, y, c = lax.axis_index("x"), lax.axis_index("y"), lax.axis_index("c")
        j = 2 * x + y
        sib = (x, y, 1 - c)
        sends = []
        for d in range(1, 4):
            frm = (j + 4 - d) % 4
            for a in range(n):
                cp = pltpu.make_async_remote_copy(
                    src_ref=buf[a].at[:, frm, c], dst_ref=buf[a].at[:, frm, c], send_sem=send_sems.at[a, d - 1],
                    recv_sem=recv_sems.at[a, d - 1], device_id=sib, device_id_type=MESH)
                cp.start()
                sends.append(cp)
        for d in range(1, 4):
            frm = (j + 4 - d) % 4
            for a in range(n):
                pltpu.make_async_remote_copy(
                    src_ref=buf[a].at[:, frm, 1 - c], dst_ref=buf[a].at[:, frm, 1 - c], send_sem=send_sems.at[a, d - 1],
                    recv_sem=recv_sems.at[a, d - 1], device_id=sib, device_id_type=MESH).wait_recv()
        for cp in sends:
            cp.wait_send()

    return pl.pallas_call(
        body, name=name,
        in_specs=[_ANY] * n, out_specs=[_ANY] * n,
        out_shape=[jax.ShapeDtypeStruct(b.shape, b.dtype) for b in bufs],
        scratch_shapes=[pltpu.SemaphoreType.DMA((n, 3))] * 2,
        input_output_aliases={a: a for a in range(n)},
    )(*bufs)


def _pair_add(g4, r1, cj_arr, name, TR=512):
    B4, _, rh, C = g4.shape
    B = B4 // 4
    TR = rh if rh <= TR else _row_tile(rh, TR)

    def body(cj_ref, g_ref, r_ref, o16_ref, own_ref):
        s = g_ref[...] + r_ref[...]
        o16_ref[...] = s.astype(BF16)

        @pl.when(pl.program_id(2) == cj_ref[1])
        def _():
            own_ref[...] = s

    return pl.pallas_call(
        body, name=name,
        grid_spec=pltpu.PrefetchScalarGridSpec(
            num_scalar_prefetch=1, grid=(B, rh // TR, 4),
            in_specs=[pl.BlockSpec((None, None, TR, C), lambda b, t, p, cj: (b * 4 + p, cj[0], t, 0)),
                      pl.BlockSpec((None, None, TR, C), lambda b, t, p, cj: (b * 4 + p, 0, t, 0))],
            out_specs=[pl.BlockSpec((None, None, TR, C), lambda b, t, p, cj: (p, b, t, 0)),
                       pl.BlockSpec((None, TR, C), lambda b, t, p, cj: (b, t, 0))]),
        out_shape=[jax.ShapeDtypeStruct((4, B, rh, C), BF16), jax.ShapeDtypeStruct((B, rh, C), F32)],
        compiler_params=_params(("parallel", "parallel", "arbitrary")),
    )(cj_arr, g4, r1)


def _chip_add(own, r2, cj_arr, into, first, name, TR=512):
    B, rh, C = own.shape
    TR = rh if rh <= TR else _row_tile(rh, TR)

    def body(cj_ref, p_ref, r_ref, _into_ref, o_ref):
        o_ref[...] = p_ref[...] + r_ref[0].astype(F32) + r_ref[1].astype(F32) + r_ref[2].astype(F32)

    return pl.pallas_call(
        body, name=name,
        grid_spec=pltpu.PrefetchScalarGridSpec(
            num_scalar_prefetch=1, grid=(B, rh // TR),
            in_specs=[pl.BlockSpec((None, TR, C), lambda b, t, cj: (b, t, 0)),
                      pl.BlockSpec((3, None, TR, C), lambda b, t, cj: (0, b, t, 0)),
                      _ANY],
            out_specs=pl.BlockSpec((None, None, TR, C), lambda b, t, cj: (first + b, cj[0], t, 0))),
        out_shape=jax.ShapeDtypeStruct(into.shape, F32),
        input_output_aliases={3: 0},
        compiler_params=_params(("parallel", "parallel")),
    )(cj_arr, own, r2, into)


def _pair_share(bufs, name):
    n = len(bufs)

    def body(*refs):
        buf = refs[n:2 * n]
        send_sems, recv_sems = refs[2 * n:]
        c = lax.axis_index("c")
        sib = (lax.axis_index("x"), lax.axis_index("y"), 1 - c)
        cps = []
        for a in range(n):
            cp = pltpu.make_async_remote_copy(
                src_ref=buf[a].at[:, c], dst_ref=buf[a].at[:, c], send_sem=send_sems.at[a],
                recv_sem=recv_sems.at[a], device_id=sib, device_id_type=MESH)
            cp.start()
            cps.append(cp)
        for a in range(n):
            pltpu.make_async_remote_copy(
                src_ref=buf[a].at[:, 1 - c], dst_ref=buf[a].at[:, 1 - c], send_sem=send_sems.at[a],
                recv_sem=recv_sems.at[a], device_id=sib, device_id_type=MESH).wait_recv()
        for cp in cps:
            cp.wait_send()

    return pl.pallas_call(
        body, name=name, in_specs=[_ANY] * n, out_specs=[_ANY] * n,
        out_shape=[jax.ShapeDtypeStruct(b.shape, b.dtype) for b in bufs],
        scratch_shapes=[pltpu.SemaphoreType.DMA((n,)), pltpu.SemaphoreType.DMA((n,))],
        input_output_aliases={a: a for a in range(n)},
    )(*bufs)


def _pair_swap(arr, name):
    def body(src, dst, send_sem, recv_sem):
        sib = (lax.axis_index("x"), lax.axis_index("y"), 1 - lax.axis_index("c"))
        cp = pltpu.make_async_remote_copy(src_ref=src, dst_ref=dst, send_sem=send_sem, recv_sem=recv_sem,
                                          device_id=sib, device_id_type=MESH)
        cp.start()
        cp.wait_recv()
        cp.wait_send()

    return pl.pallas_call(
        body, name=name, in_specs=[_ANY], out_specs=_ANY,
        out_shape=jax.ShapeDtypeStruct(arr.shape, arr.dtype),
        scratch_shapes=[pltpu.SemaphoreType.DMA, pltpu.SemaphoreType.DMA],
    )(arr)


class _ReduceScatter:
    def __init__(self, n_layers, cj_arr):
        self.L, self.cj = n_layers, cj_arr
        self.total = None
        self.pair = None
        self.chip = None

    def _land(self, after):
        handle, layer, owns = self.chip
        n = len(owns)
        r2 = _wait_copies(handle, _chip_plan, (after,), "rs_chip_wait")[n:]
        if self.total is None:
            self.total = [lax.empty((self.L * o.shape[0], 2) + o.shape[1:], F32) for o in owns]
        self.total = [_chip_add(o, r, self.cj, t, layer * o.shape[0], "rs_chip_add")
                      for o, r, t in zip(owns, r2, self.total)]
        self.chip = None

    def add_layer(self, layer, grads):
        g4 = [g.reshape(g.shape[0] * 4, 2, g.shape[1] // 8, g.shape[2]) for g in grads]
        lands = [lax.empty((g.shape[0], 1) + g.shape[2:], F32) for g in g4]
        handle = _start_copies(g4 + lands, _pair_plan, len(g4), "rs_pair_start")
        self.pair = (handle, layer)
        return (handle[-1],)

    def advance(self, after):
        if self.pair is None:
            return ()
        handle, layer = self.pair
        both = _wait_copies(handle, _pair_plan, (after,), "rs_pair_wait")
        n = len(both) // 2
        added = [_pair_add(g, r, self.cj, "rs_pair_add") for g, r in zip(both[:n], both[n:])]
        parts, owns = [p for p, _ in added], [o for _, o in added]
        if self.chip is not None:
            self._land(owns[-1])
        lands = [lax.empty((3,) + p.shape[1:], p.dtype) for p in parts]
        handle = _start_copies(parts + lands, _chip_plan, 3 * n, "rs_chip_start")
        self.pair, self.chip = None, (handle, layer, owns)
        return (handle[-1],)

    def result(self, after):
        self._land(after)
        full = _pair_share(self.total, "rs_pair_share")
        return [f.reshape(f.shape[0], f.shape[1] * f.shape[2], f.shape[3]) for f in full]


def _relu2_epi(acc):
    return acc, jnp.square(jnp.maximum(acc, 0.0))


def _relu2_bwd_epi(acc, u):
    return (acc * (2.0 * jnp.maximum(u, 0.0)),)


_GRAD_ORDER = ("winT", "wbT", "wout", "wq", "wkv", "woT", "wupT", "wdown")
_DW = dict(tm=512, tn=1024, tk=4096)
_LONG_K = dict(tm=1024, tn=1024, tk=2048)


def _forward_backward(x, mem, target, weights_of, P, grads_done, grads_advance, first_deps=()):
    L = P["g_norm"].shape[0]
    S, D = x.shape
    gn = lambda l, i: P["g_norm"][l, i][None]

    saved = []
    (h,) = _resnorm_fwd(x, None, None, gn(0, 0), "norm_in", deps=first_deps)
    xr = x
    for l in range(L):
        W = weights_of(l, xr)
        proj = _mm(h, W["winT"], "nt", "in_proj", b_pre=(0,), tn=1792)
        small = (jnp.broadcast_to(P["sinks"][l][:, None], (8, BLK)), P["ws"][l], P["bs"][l][:, :, None],
                 P["gsgu"][l][None], P["wp"][l], P["ps"][l][None])
        br = _mix_fwd(proj, *small, "mix_fwd")
        pb = lax.empty((3, S, D), F32)
        for n in range(3):
            pb = _mm(br, W["wbT"], "nt", "branch_proj", a_pre=(n,), b_pre=(n,), into=pb, out_pre=(n,))
        merged = _merge_fwd(proj, pb, "merge_fwd")
        z = _mm(merged, W["wout"], "nn", "out_proj", b_pre=(0,))
        x1, hm = _resnorm_fwd(xr, z, gn(l, 1), gn(l, 2), "resnorm_fwd")
        qm = _mm(hm, W["wq"], "nn", "mem_q", b_pre=(0,))
        (memn,) = _resnorm_fwd(mem, None, None, P["g_mem"][l][None], "mem_norm")
        kv = _mm(memn, W["wkv"], "nn", "mem_kv", b_pre=(0,))
        om = _memattn_fwd(qm, kv, "memattn_fwd")
        ym = _mm(om, W["woT"], "nt", "mem_o", b_pre=(0,))
        x2, hf = _resnorm_fwd(x1, ym, gn(l, 3), gn(l, 4), "resnorm_fwd")
        u, a = _mm(hf, W["wupT"], "nt", "mlp_up", b_pre=(0,), out_dtypes=(F32, BF16), epi=_relu2_epi)
        yf = _mm(a, W["wdown"], "nn", "mlp_down", b_pre=(0,), **_LONG_K)
        saved.append(dict(W=W, x0=xr, h=h, proj=proj, small=small, br=br, pb=pb, merged=merged, z=z, x1=x1, hm=hm,
                          qm=qm, memn=memn, kv=kv, om=om, ym=ym, x2=x2, hf=hf, u=u, a=a, yf=yf))
        if l < L - 1:
            xr, h = _resnorm_fwd(x2, yf, gn(l, 5), gn(l + 1, 0), "resnorm_fwd")
    dres, loss = _final_fwd(saved[-1]["x2"], saved[-1]["yf"], gn(L - 1, 5), target, "loss_head")

    dgn = [[None] * 6 for _ in range(L)]
    dsmall = {k: [None] * L for k in ("g_mem", "sinks", "ws", "bs", "gsgu", "wp", "ps")}
    dh = None
    for l in reversed(range(L)):
        s = saved[l]
        W, G = s["W"], {}
        if l == L - 1:
            dx2, dyf, dgn[l][5] = _resnorm_bwd(s["x2"], s["yf"], gn(l, 5), None, dres, None, "resnorm_bwd_top")
        else:
            dx2, dyf, dgn[l][5], dgn[l + 1][0] = _resnorm_bwd(s["x2"], s["yf"], gn(l, 5), gn(l + 1, 0), dres, dh,
                                                              "resnorm_bwd", deps=deps)
        du = _mm(dyf, W["wdown"], "nt", "mlp_down_dx", b_pre=(0,), out_dtypes=(BF16,), extras=(s["u"],), epi=_relu2_bwd_epi)
        G["wdown"] = _mm(s["a"], dyf, "tn", "mlp_down_dw", **_DW)[None]
        dhf = _mm(du, W["wupT"], "nn", "mlp_up_dx", b_pre=(0,), **_LONG_K)
        G["wupT"] = _mm(du, s["hf"], "tn", "mlp_up_dw", **_DW)[None]
        dx1, dym, dgn[l][3], dgn[l][4] = _resnorm_bwd(s["x1"], s["ym"], gn(l, 3), gn(l, 4), dx2, dhf, "resnorm_bwd")
        dom = _mm(dym, W["woT"], "nn", "mem_o_dx", b_pre=(0,), deps=grads_advance(dx1))
        G["woT"] = _mm(dym, s["om"], "tn", "mem_o_dw", **_DW)[None]
        dqm, dkv = _memattn_bwd(s["qm"], s["kv"], dom, "memattn_bwd")
        dmemn = _mm(dkv, W["wkv"], "nt", "mem_kv_dx", b_pre=(0,))
        G["wkv"] = _mm(s["memn"], dkv, "tn", "mem_kv_dw")[None]
        _, dsmall["g_mem"][l] = _resnorm_bwd(mem, None, None, P["g_mem"][l][None], None, dmemn, "mem_norm_bwd")
        dhm = _mm(dqm, W["wq"], "nt", "mem_q_dx", b_pre=(0,))
        G["wq"] = _mm(s["hm"], dqm, "tn", "mem_q_dw", **_DW)[None]
        dx0, dz, dgn[l][1], dgn[l][2] = _resnorm_bwd(s["x0"], s["z"], gn(l, 1), gn(l, 2), dx1, dhm, "resnorm_bwd")
        dmerged = _mm(dz, W["wout"], "nt", "out_proj_dx", b_pre=(0,))
        G["wout"] = _mm(s["merged"], dz, "tn", "out_proj_dw", **_DW)[None]
        dproj, dpb = _merge_bwd(s["proj"], s["pb"], dmerged, "merge_bwd")
        dbr = lax.empty((3, S, 512), F32)
        G["wbT"] = lax.empty(W["wbT"].shape, F32)
        for n in range(3):
            dbr = _mm(dpb, W["wbT"], "nn", "branch_proj_dx", a_pre=(n,), b_pre=(n,), into=dbr, out_pre=(n,))
            G["wbT"] = _mm(dpb, s["br"], "tn", "branch_proj_dw", a_pre=(n,), b_pre=(n,), into=G["wbT"], out_pre=(n,), **_DW)
        (dproj, dsmall["sinks"][l], dsmall["ws"][l], dsmall["bs"][l], dsmall["gsgu"][l], dsmall["wp"][l],
         dsmall["ps"][l]) = _mix_bwd(s["proj"], dbr, dproj, *s["small"], "mix_bwd")
        dh = _mm(dproj, W["winT"], "nn", "in_proj_dx", b_pre=(0,), **_LONG_K)
        G["winT"] = _mm(dproj, s["h"], "tn", "in_proj_dw", **_DW)[None]
        deps = grads_done(l, [G[k] for k in _GRAD_ORDER])
        dres = dx0
    grad_x, dgn[0][0] = _resnorm_bwd(x, None, None, gn(0, 0), dres, dh, "norm_in_bwd", deps=deps)
    tail_deps = grads_advance(grad_x)

    small_grads = dict(
        g_norm=jnp.stack([jnp.concatenate(row, axis=0) for row in dgn]),
        g_mem=jnp.concatenate(dsmall["g_mem"], axis=0),
        sinks=jnp.stack([d[:, 0] for d in dsmall["sinks"]]),
        ws=jnp.stack(dsmall["ws"]),
        bs=jnp.stack([d[:, :, 0] for d in dsmall["bs"]]),
        gsgu=jnp.concatenate(dsmall["gsgu"], axis=0),
        wp=jnp.stack(dsmall["wp"]),
        ps=jnp.concatenate(dsmall["ps"], axis=0),
    )
    return loss, grad_x, small_grads, tail_deps


_PACK_ROWS = 512


def _as_rows(a):
    n = math.prod(a.shape)
    if n % 128:
        a = jnp.pad(a.reshape(-1), (0, (-n) % 128))
    r = a.reshape(-1, 128)
    return jnp.pad(r, ((0, (-r.shape[0]) % 8), (0, 0))) if r.shape[0] % 8 else r


def _pack(arrays):
    rows = [_as_rows(a) for a in arrays]
    total = sum(r.shape[0] for r in rows)
    tail = (-total) % _PACK_ROWS
    if tail:
        rows.append(jnp.zeros((tail, 128), rows[0].dtype))
    return jnp.concatenate(rows, axis=0)


def _unpack(packed, like):
    out, pos = [], 0
    for a in like:
        n = math.prod(a.shape)
        nr = -(-n // 128)
        rows = packed[pos:pos + nr]
        out.append((rows.reshape(-1)[:n] if n % 128 else rows).reshape(a.shape))
        pos += nr + (-nr) % 8
    return out


_BIG = ("w_in", "w_branch", "w_out", "w_q_mem", "w_kv_mem", "w_o_mem", "w_up", "w_down")
_SMALL = ("g_norm", "g_mem", "attn_sinks", "w_spatial", "b_spatial", "g_sgu", "w_pool", "pool_scale")
_WEIGHTS = ("g_norm", "g_mem", "w_in", "attn_sinks", "w_spatial", "b_spatial", "g_sgu", "w_pool", "pool_scale",
            "w_branch", "w_out", "w_q_mem", "w_kv_mem", "w_o_mem", "w_up", "w_down")


def _to_working(name, w):
    if name == "w_in":
        return jnp.swapaxes(w, 1, 2)
    if name == "w_branch":
        t = jnp.swapaxes(w, 2, 3)
        return t.reshape(t.shape[0] * 3, t.shape[2], t.shape[3])
    if name in ("w_o_mem", "w_up"):
        return jnp.swapaxes(w, 1, 2)
    return w


def _from_working(name, g):
    if name == "w_in":
        return jnp.swapaxes(g, 1, 2)
    if name == "w_branch":
        return jnp.swapaxes(g.reshape(g.shape[0] // 3, 3, g.shape[1], g.shape[2]), 2, 3)
    if name in ("w_o_mem", "w_up"):
        return jnp.swapaxes(g, 1, 2)
    return g


_WKEY = dict(w_in="winT", w_branch="wbT", w_out="wout", w_q_mem="wq", w_kv_mem="wkv", w_o_mem="woT",
             w_up="wupT", w_down="wdown")


def kernel(x, mem, g_norm, g_mem, w_in, attn_sinks, w_spatial, b_spatial, g_sgu, w_pool, pool_scale, w_branch, w_out, w_q_mem, w_kv_mem, w_o_mem, w_up, w_down, loss_target, m_g_norm, m_g_mem, m_w_in, m_attn_sinks, m_w_spatial, m_b_spatial, m_g_sgu, m_w_pool, m_pool_scale, m_w_branch, m_w_out, m_w_q_mem, m_w_kv_mem, m_w_o_mem, m_w_up, m_w_down, v_g_norm, v_g_mem, v_w_in, v_attn_sinks, v_w_spatial, v_b_spatial, v_g_sgu, v_w_pool, v_pool_scale, v_w_branch, v_w_out, v_w_q_mem, v_w_kv_mem, v_w_o_mem, v_w_up, v_w_down):
    w = dict(g_norm=g_norm, g_mem=g_mem, w_in=w_in, attn_sinks=attn_sinks, w_spatial=w_spatial, b_spatial=b_spatial,
             g_sgu=g_sgu, w_pool=w_pool, pool_scale=pool_scale, w_branch=w_branch, w_out=w_out, w_q_mem=w_q_mem,
             w_kv_mem=w_kv_mem, w_o_mem=w_o_mem, w_up=w_up, w_down=w_down)
    m = dict(g_norm=m_g_norm, g_mem=m_g_mem, w_in=m_w_in, attn_sinks=m_attn_sinks, w_spatial=m_w_spatial,
             b_spatial=m_b_spatial, g_sgu=m_g_sgu, w_pool=m_w_pool, pool_scale=m_pool_scale, w_branch=m_w_branch,
             w_out=m_w_out, w_q_mem=m_w_q_mem, w_kv_mem=m_w_kv_mem, w_o_mem=m_w_o_mem, w_up=m_w_up, w_down=m_w_down)
    v = dict(g_norm=v_g_norm, g_mem=v_g_mem, w_in=v_w_in, attn_sinks=v_attn_sinks, w_spatial=v_w_spatial,
             b_spatial=v_b_spatial, g_sgu=v_g_sgu, w_pool=v_w_pool, pool_scale=v_pool_scale, w_branch=v_w_branch,
             w_out=v_w_out, w_q_mem=v_w_q_mem, w_kv_mem=v_w_kv_mem, w_o_mem=v_w_o_mem, w_up=v_w_up, w_down=v_w_down)
    L = g_norm.shape[0]
    j = 2 * lax.axis_index("x") + lax.axis_index("y")
    c = lax.axis_index("c")
    j_arr = jnp.reshape(j, (1,)).astype(jnp.int32)
    cj_arr = jnp.stack([c, j]).astype(jnp.int32)

    gs = g_norm.shape[2]
    working = [_to_working(n, w[n]) for n in _BIG]
    per_layer = [wk.shape[0] // L for wk in working]

    def own_slabs(l):
        return [_own_slab(wk, BF16, j_arr, "own_slab", first=l * b, count=b) for wk, b in zip(working, per_layer)]

    first_bufs = own_slabs(0) + [_own_slab(g_norm.reshape(1, L * 6 * gs // 128, 128), F32, j_arr, "own_slab_norm")]
    first_handle = _start_copies(first_bufs, _gather_plan, 3 * len(first_bufs), "gather_start_first")
    slabs = {l: own_slabs(l) for l in range(1, L)}
    first = _gather_forward(_wait_copies(first_handle, _gather_plan, [s for l in slabs for s in slabs[l]],
                                         "gather_wait_first"), "gather_forward_first")
    in_flight, dep = {}, first[0]
    for l in range(1, L):
        in_flight[l] = _start_copies(slabs[l], _gather_plan, 3 * len(slabs[l]), "gather_start", deps=(dep,))
        dep = in_flight[l][-1]
    gn_full = jnp.transpose(first[-1].reshape(4, L * 6, gs), (1, 0, 2)).reshape(L, 6, 4 * gs)
    P = dict(g_norm=gn_full, g_mem=g_mem, sinks=attn_sinks, ws=w_spatial, bs=b_spatial, gsgu=g_sgu, wp=w_pool,
             ps=pool_scale)

    def weights_of(l, after):
        got = first[:-1] if l == 0 else _gather_forward(
            _wait_copies(in_flight[l], _gather_plan, (after,), "gather_wait"), "gather_forward")
        return {k: g.reshape(g.shape[0], 8 * g.shape[3], g.shape[4]) for k, g in zip(_GRAD_ORDER, got)}

    rs = _ReduceScatter(L, cj_arr)
    loss_part, grad_x, sg, tail_deps = _forward_backward(
        x[0], mem[0], loss_target[0], weights_of, P, rs.add_layer, rs.advance,
        first_deps=[h[-1] for h in in_flight.values()])
    loss = lax.psum(loss_part[0, 0], ("x", "y", "c"))

    full_small = [sg["g_norm"], sg["g_mem"], sg["sinks"], sg["ws"], sg["bs"], sg["gsgu"], sg["wp"], sg["ps"]]
    packed = _pack(full_small)
    pair_sum = _own_slab(packed[None], F32, j_arr, "small_grads_pair_sum", deps=tail_deps,
                         plus=_pair_swap(packed, "small_grads_swap")[None])
    (chip_sums,) = _gather_weights([pair_sum], "gather_small_grads")
    total = _sum_slots(chip_sums.reshape(4, *packed.shape), "sum_small_grads")
    grads = {n: _from_working(n, g) for n, g in zip(_BIG, rs.result(total))}
    for n, g in zip(_SMALL, _unpack(total, full_small)):
        grads[n] = lax.dynamic_slice_in_dim(g, j * g_norm.shape[2], g_norm.shape[2], axis=2) if n == "g_norm" else g

    delta, new_m, new_v = {}, {}, {}
    for n in _BIG:
        view = (lambda t: jnp.swapaxes(t, 1, 2)) if n == "w_in" else (lambda t: t)
        shp = view(w[n]).shape
        two_d = lambda t: view(t).reshape(-1, shp[-1])
        d_, m_, v_ = _adamw(two_d(w[n]), two_d(grads[n]), two_d(m[n]), two_d(v[n]), "adamw")
        delta[n], new_m[n], new_v[n] = view(d_.reshape(shp)), view(m_.reshape(shp)), view(v_.reshape(shp))
    small_w = [w[n] for n in _SMALL]
    d_, m_, v_ = _adamw(_pack(small_w), _pack([grads[n] for n in _SMALL]), _pack([m[n] for n in _SMALL]),
                        _pack([v[n] for n in _SMALL]), "adamw_small")
    for n, dd, mm_, vv in zip(_SMALL, _unpack(d_, small_w), _unpack(m_, small_w), _unpack(v_, small_w)):
        delta[n], new_m[n], new_v[n] = dd, mm_, vv

    return (loss, grad_x[None], *[grads[n] for n in _WEIGHTS], *[delta[n] for n in _WEIGHTS],
            *[new_m[n] for n in _WEIGHTS], *[new_v[n] for n in _WEIGHTS])
```

```python
import functools
import math

import jax
import jax.numpy as jnp
from jax import lax
from jax.experimental import pallas as pl
from jax.experimental.pallas import tpu as pltpu

F32 = jnp.float32
BF16 = jnp.bfloat16
MESH = pl.DeviceIdType.MESH

EPS = 1e-6
NEG_INF = -1e30
BLK = 128
HALO = 16
POOL_WINDOWS = (2, 4, 8, 16)
ATT_SCALE = 1.0 / math.sqrt(64.0)
MEM_SCALE = 1.0 / math.sqrt(128.0)
C_Q, C_K, C_V, C_SU, C_SV, C_PC, C_GATE, C_END = 0, 512, 640, 768, 1280, 1792, 2304, 5376

ADAM_LR, ADAM_B1, ADAM_B2, ADAM_EPS, ADAM_WD, ADAM_STEP = 0.001, 0.9, 0.999, 1e-08, 0.01, 10

VMEM_LIMIT_BYTES = 56 * 1024 * 1024

_DIMS = {
    "nn": (((1,), (0,)), ((), ())),
    "nt": (((1,), (1,)), ((), ())),
    "tn": (((0,), (0,)), ((), ())),
}


def _dot(a, b, mode):
    return lax.dot_general(a, b, _DIMS[mode], preferred_element_type=F32)


def _params(semantics):
    return pltpu.CompilerParams(dimension_semantics=semantics, vmem_limit_bytes=VMEM_LIMIT_BYTES)


def _tile(dim, pref):
    if dim <= pref:
        return dim
    t = (pref // 128) * 128
    while t >= 128:
        if dim % t == 0:
            return t
        t -= 128
    raise ValueError(f"no tile for {dim}")


def _rms(x, g):
    return x * lax.rsqrt(jnp.mean(x * x, axis=-1, keepdims=True) + EPS) * g


def _mm(a, b, mode, name, *, out_dtypes=(F32,), a_pre=(), b_pre=(), into=None, out_pre=(),
        extras=(), epi=None, deps=(), tm=2048, tn=1024, tk=1024):
    a2, b2 = a.shape[len(a_pre):], b.shape[len(b_pre):]
    if mode == "nn":
        (M, K), (K2, N) = a2, b2
    elif mode == "nt":
        (M, K), (N, K2) = a2, b2
    else:
        (K, M), (K2, N) = a2, b2
    assert K == K2, (a.shape, b.shape, mode)
    tm, tn, tk = _tile(M, tm), _tile(N, tn), _tile(K, tk)
    nk = K // tk
    na, nb_, no = len(a_pre), len(b_pre), len(out_pre)
    if mode == "tn":
        a_spec = pl.BlockSpec((None,) * na + (tk, tm), lambda i, j, k: a_pre + (k, i))
    else:
        a_spec = pl.BlockSpec((None,) * na + (tm, tk), lambda i, j, k: a_pre + (i, k))
    if mode == "nt":
        b_spec = pl.BlockSpec((None,) * nb_ + (tn, tk), lambda i, j, k: b_pre + (j, k))
    else:
        b_spec = pl.BlockSpec((None,) * nb_ + (tk, tn), lambda i, j, k: b_pre + (k, j))
    tile_spec = pl.BlockSpec((tm, tn), lambda i, j, k: (i, j))
    ne, nout = len(extras), len(out_dtypes)
    in_specs = [a_spec, b_spec] + [tile_spec] * ne
    operands = [a, b, *extras]
    aliases = {}
    if into is not None:
        assert nout == 1
        in_specs.append(pl.BlockSpec(memory_space=pl.ANY))
        operands.append(into)
        aliases = {len(operands) - 1: 0}
        out_shape = [jax.ShapeDtypeStruct(into.shape, into.dtype)]
        out_specs = [pl.BlockSpec((None,) * no + (tm, tn), lambda i, j, k: out_pre + (i, j))]
    else:
        out_shape = [jax.ShapeDtypeStruct((M, N), dt) for dt in out_dtypes]
        out_specs = [tile_spec] * nout
    in_specs += [pl.BlockSpec(memory_space=pl.ANY)] * len(deps)
    operands += list(deps)

    def body(*refs):
        a_ref, b_ref = refs[0], refs[1]
        ex = refs[2:2 + ne]
        pos = 2 + ne + (1 if into is not None else 0) + len(deps)
        outs = refs[pos:pos + nout]
        acc_ref = refs[pos + nout] if nk > 1 else None

        def finish(acc):
            vals = epi(acc, *[e[...] for e in ex]) if epi is not None else (acc,)
            for o, v in zip(outs, vals):
                o[...] = v.astype(o.dtype)

        def prod():
            return _dot(a_ref[...].astype(BF16), b_ref[...].astype(BF16), mode)

        if nk == 1:
            finish(prod())
        else:
            k = pl.program_id(2)

            @pl.when(k == 0)
            def _():
                acc_ref[...] = jnp.zeros_like(acc_ref)

            acc_ref[...] += prod()

            @pl.when(k == nk - 1)
            def _():
                finish(acc_ref[...])

    res = pl.pallas_call(
        body, name=name, grid=(M // tm, N // tn, nk),
        in_specs=in_specs, out_specs=out_specs, out_shape=out_shape,
        scratch_shapes=[pltpu.VMEM((tm, tn), F32)] if nk > 1 else [],
        input_output_aliases=aliases,
        compiler_params=_params(("parallel", "parallel", "arbitrary")),
    )(*operands)
    return res[0] if nout == 1 else tuple(res)


def _resnorm_fn(has_post, has_pre):
    def f(*a):
        x, k = a[0], 1
        if has_post:
            x, k = x + _rms(a[1], a[2]), 3
        outs = [x]
        if has_pre:
            outs.append(_rms(x, a[k]))
        return tuple(outs)
    return f


def _row_spec(T, W):
    return pl.BlockSpec((T, W), lambda i: (i, 0))


def _par_spec(W):
    return pl.BlockSpec((1, W), lambda i: (0, 0))


def _resnorm_fwd(xr, y, gp, gq, name, T=512, deps=()):
    S, D = xr.shape
    T = min(T, S)
    has_post, has_pre = y is not None, gq is not None
    f = _resnorm_fn(has_post, has_pre)
    ins = [xr] + ([y, gp] if has_post else []) + ([gq] if has_pre else [])
    in_specs = [_row_spec(T, D)] + ([_row_spec(T, D), _par_spec(D)] if has_post else []) + ([_par_spec(D)] if has_pre else [])
    out_shape, out_specs = [], []
    if has_post:
        out_shape.append(jax.ShapeDtypeStruct((S, D), F32)); out_specs.append(_row_spec(T, D))
    if has_pre:
        out_shape.append(jax.ShapeDtypeStruct((S, D), BF16)); out_specs.append(_row_spec(T, D))
    n_in, n_dep = len(ins), len(deps)

    def body(*refs):
        vals = f(*[r[...] for r in refs[:n_in]])
        outs = list(refs[n_in + n_dep:])
        if has_post:
            outs.pop(0)[...] = vals[0]
        if has_pre:
            outs.pop(0)[...] = vals[1].astype(BF16)

    res = pl.pallas_call(body, name=name, grid=(S // T,),
                         in_specs=in_specs + [pl.BlockSpec(memory_space=pl.ANY)] * n_dep, out_specs=out_specs,
                         out_shape=out_shape, compiler_params=_params(("parallel",)))(*ins, *deps)
    return tuple(res)


def _resnorm_bwd(xr, y, gp, gq, dres, dh, name, T=512, deps=()):
    S, D = xr.shape
    T = min(T, S)
    has_post, has_pre, has_res = y is not None, gq is not None, dres is not None
    f = _resnorm_fn(has_post, has_pre)
    ins = [xr] + ([y, gp] if has_post else []) + ([gq] if has_pre else [])
    in_specs = [_row_spec(T, D)] + ([_row_spec(T, D), _par_spec(D)] if has_post else []) + ([_par_spec(D)] if has_pre else [])
    n_prim = len(ins)
    if has_res:
        ins.append(dres); in_specs.append(_row_spec(T, D))
    if has_pre:
        ins.append(dh); in_specs.append(_row_spec(T, D))
    n_in, n_dep = len(ins), len(deps)
    out_shape = [jax.ShapeDtypeStruct((S, D), F32)]
    out_specs = [_row_spec(T, D)]
    if has_post:
        out_shape += [jax.ShapeDtypeStruct((S, D), BF16), jax.ShapeDtypeStruct((1, D), F32)]
        out_specs += [_row_spec(T, D), _par_spec(D)]
    if has_pre:
        out_shape.append(jax.ShapeDtypeStruct((1, D), F32)); out_specs.append(_par_spec(D))

    def body(*refs):
        i = pl.program_id(0)
        prim = [r[...] for r in refs[:n_prim]]
        rest = list(refs[n_prim:n_in])
        ct_x = rest.pop(0)[...] if has_res else jnp.zeros((T, D), F32)
        cts = [ct_x]
        if has_pre:
            cts.append(rest.pop(0)[...].astype(F32))
        _, vjp = jax.vjp(f, *prim)
        grads = list(vjp(tuple(cts)))
        outs = list(refs[n_in + n_dep:])
        outs.pop(0)[...] = grads.pop(0)
        acc = []
        if has_post:
            outs.pop(0)[...] = grads.pop(0).astype(BF16)
            acc.append((outs.pop(0), grads.pop(0)))
        if has_pre:
            acc.append((outs.pop(0), grads.pop(0)))

        @pl.when(i == 0)
        def _():
            for o, _g in acc:
                o[...] = jnp.zeros_like(o)

        for o, g in acc:
            o[...] += g

    res = pl.pallas_call(body, name=name, grid=(S // T,),
                         in_specs=in_specs + [pl.BlockSpec(memory_space=pl.ANY)] * n_dep, out_specs=out_specs,
                         out_shape=out_shape, compiler_params=_params(("arbitrary",)))(*ins, *deps)
    return tuple(res)


def _final_fwd(xr, y, gp, target, name, T=512):
    S, D = xr.shape
    T = min(T, S)

    def body(x_ref, y_ref, g_ref, t_ref, dy_ref, loss_ref):
        i = pl.program_id(0)
        e = x_ref[...] + _rms(y_ref[...], g_ref[...]) - t_ref[...]
        dy_ref[...] = e / D

        @pl.when(i == 0)
        def _():
            loss_ref[...] = jnp.zeros_like(loss_ref)

        loss_ref[...] += 0.5 * jnp.sum(jnp.sum(e * e, axis=-1, keepdims=True) / D, axis=0, keepdims=True)

    return pl.pallas_call(
        body, name=name, grid=(S // T,),
        in_specs=[_row_spec(T, D), _row_spec(T, D), _par_spec(D), _row_spec(T, D)],
        out_specs=[_row_spec(T, D), pl.BlockSpec((1, 128), lambda i: (0, 0))],
        out_shape=[jax.ShapeDtypeStruct((S, D), F32), jax.ShapeDtypeStruct((1, 128), F32)],
        compiler_params=_params(("arbitrary",)))(xr, y, gp, target)


_STRAIGHT_HEADS = (0, 2, 5, 7)
_ROLLED_HEADS = (1, 3, 4, 6)


def _straight_lanes():
    r = lax.broadcasted_iota(jnp.int32, (4 * BLK, BLK), 0)
    c = lax.broadcasted_iota(jnp.int32, (4 * BLK, BLK), 1)
    return (r < 2 * BLK) == (c < 64)


def _att_mask(not_first):
    k = lax.broadcasted_iota(jnp.int32, (2 * BLK, 4 * BLK), 0)
    q = lax.broadcasted_iota(jnp.int32, (2 * BLK, 4 * BLK), 1) % BLK
    qc, kc = 2 + q // 64, k // 64
    return (kc <= qc) & (kc >= qc - 2) & (not_first | (k >= BLK))


def _sink_row(sk_ref, heads):
    return jnp.concatenate([sk_ref[h:h + 1, :] for h in heads], axis=1)


def _softmax_sink(s, sk):
    m = jnp.maximum(jnp.max(s, axis=0, keepdims=True), sk)
    e = jnp.exp(s - m)
    es = jnp.exp(sk - m)
    z = jnp.sum(e, axis=0, keepdims=True) + es
    return e / z, es / z


def _att_bands(cur, kvp):
    kband = jnp.concatenate([kvp[:, 0:BLK], cur[:, C_K:C_K + BLK]], axis=0)
    vband = jnp.concatenate([kvp[:, BLK:2 * BLK], cur[:, C_V:C_V + BLK]], axis=0)
    return kband, pltpu.roll(kband, 64, 1), vband, pltpu.roll(vband, 64, 1)


def _stack_tiles(ref_or_val, start):
    return jnp.concatenate([ref_or_val[:, start + BLK * t:start + BLK * (t + 1)] for t in range(4)], axis=0)


def _sgu_mask():
    r = lax.broadcasted_iota(jnp.int32, (BLK, BLK), 0)
    c = lax.broadcasted_iota(jnp.int32, (BLK, BLK), 1)
    return (c // 64) <= (r // 64)


def _pool_cnt(blk, w):
    t = blk * BLK + lax.broadcasted_iota(jnp.int32, (BLK, 1), 0)
    return jnp.minimum(t + 1, w).astype(F32)


def _mix_in_specs(nb, rev):
    def b(i):
        return nb - 1 - i if rev else i
    return [
        pl.BlockSpec((BLK, C_GATE), lambda i: (b(i), 0)),
        pl.BlockSpec((BLK, 2 * BLK), lambda i: (jnp.maximum(b(i) - 1, 0), C_K // (2 * BLK))),
        pl.BlockSpec((HALO, C_GATE), lambda i: (jnp.maximum(b(i) * (BLK // HALO) - 1, 0), 0)),
        pl.BlockSpec((8, BLK), lambda i: (0, 0)),
        pl.BlockSpec((4, BLK, BLK), lambda i: (0, 0, 0)),
        pl.BlockSpec((4, BLK, 1), lambda i: (0, 0, 0)),
        pl.BlockSpec((1, 512), lambda i: (0, 0)),
        pl.BlockSpec((4, BLK, BLK), lambda i: (0, 0, 0)),
        pl.BlockSpec((1, 512), lambda i: (0, 0)),
    ]


def _mix_fwd(proj, sinks_b, ws, bs3, gsgu, wp, ps, name):
    S = proj.shape[0]
    nb = S // BLK

    def body(cur_ref, kvp_ref, pcp_ref, sk_ref, ws_ref, bs_ref, gs_ref, wp_ref, ps_ref, br_ref, ext_ref):
        i = pl.program_id(0)
        not_first = i > 0
        cur, kvp = cur_ref[...], kvp_ref[...]
        mask = _att_mask(not_first)
        own = _straight_lanes()
        q = _stack_tiles(cur, C_Q)
        outs = []
        kband, kroll, vband, vroll = _att_bands(cur, kvp)
        for qg, kg, vg, heads in ((jnp.where(own, q, 0.0), kband, vband, _STRAIGHT_HEADS),
                                  (jnp.where(own, 0.0, q), kroll, vroll, _ROLLED_HEADS)):
            s = jnp.where(mask, _dot(kg.astype(BF16), qg.astype(BF16), "nt") * ATT_SCALE, NEG_INF)
            p, _ = _softmax_sink(s, _sink_row(sk_ref, heads))
            outs.append(_dot(p.astype(BF16), vg.astype(BF16), "tn"))
        o = jnp.where(own, outs[0], outs[1])
        for t in range(4):
            br_ref[0, :, BLK * t:BLK * (t + 1)] = o[BLK * t:BLK * (t + 1)].astype(BF16)
        gu = jax.nn.gelu(cur[:, C_SU:C_SV])
        vn = _rms(jax.nn.gelu(cur[:, C_SV:C_PC]), gs_ref[...]).astype(BF16)
        wmask = _sgu_mask()
        for g in range(4):
            wm = jnp.where(wmask, ws_ref[g], 0.0).astype(BF16)
            sp = _dot(wm, vn[:, BLK * g:BLK * (g + 1)], "nn") + bs_ref[g]
            br_ref[1, :, BLK * g:BLK * (g + 1)] = (gu[:, BLK * g:BLK * (g + 1)] * sp).astype(BF16)
        c = cur[:, C_PC:C_GATE]
        ext_ref[0:HALO, :] = jnp.where(not_first, pcp_ref[:, C_PC:C_GATE], 0.0)
        ext_ref[HALO:HALO + BLK, :] = c
        for g, w in enumerate(POOL_WINDOWS):
            sl = slice(BLK * g, BLK * (g + 1))
            acc = ext_ref[HALO:HALO + BLK, sl]
            for k in range(1, w):
                acc = acc + ext_ref[HALO - k:HALO - k + BLK, sl]
            pooled = acc / _pool_cnt(i, w) - c[:, sl]
            mixed = _dot(pooled.astype(BF16), wp_ref[g].astype(BF16), "nn")
            br_ref[2, :, sl] = (mixed * ps_ref[:, sl]).astype(BF16)

    return pl.pallas_call(
        body, name=name, grid=(nb,),
        in_specs=_mix_in_specs(nb, False),
        out_specs=pl.BlockSpec((3, BLK, 512), lambda i: (0, i, 0)),
        out_shape=jax.ShapeDtypeStruct((3, S, 512), BF16),
        scratch_shapes=[pltpu.VMEM((HALO + BLK, 512), F32)],
        compiler_params=_params(("parallel",)),
    )(proj, proj, proj, sinks_b, ws, bs3, gsgu, wp, ps)


def _mix_bwd(proj, dbr, dproj, sinks_b, ws, bs3, gsgu, wp, ps, name):
    S = proj.shape[0]
    nb = S // BLK

    def body(cur_ref, kvp_ref, pcp_ref, sk_ref, ws_ref, bs_ref, gs_ref, wp_ref, ps_ref, dbr_ref, _dproj_in,
             dp_ref, dsk_ref, dws_ref, dbs_ref, dgs_ref, dwp_ref, dps_ref,
             ext_ref, z_ref, ckv_ref, cpc_ref):
        i = pl.program_id(0)
        blk = nb - 1 - i
        not_first = blk > 0

        @pl.when(i == 0)
        def _():
            for r in (dsk_ref, dws_ref, dbs_ref, dgs_ref, dwp_ref, dps_ref, ckv_ref, cpc_ref, z_ref):
                r[...] = jnp.zeros_like(r)

        cur, kvp = cur_ref[...], kvp_ref[...]
        mask = _att_mask(not_first)
        own = _straight_lanes()
        q = _stack_tiles(cur, C_Q)
        do = jnp.concatenate([dbr_ref[0, :, BLK * t:BLK * (t + 1)] for t in range(4)], axis=0)
        kband, kroll, vband, vroll = _att_bands(cur, kvp)
        dqs, dks, dvs = [], [], []
        for qg, dog, kg, vg, heads in (
                (jnp.where(own, q, 0.0), jnp.where(own, do, 0.0), kband, vband, _STRAIGHT_HEADS),
                (jnp.where(own, 0.0, q), jnp.where(own, 0.0, do), kroll, vroll, _ROLLED_HEADS)):
            qg, dog, kg, vg = qg.astype(BF16), dog.astype(BF16), kg.astype(BF16), vg.astype(BF16)
            s = jnp.where(mask, _dot(kg, qg, "nt") * ATT_SCALE, NEG_INF)
            p, p_sink = _softmax_sink(s, _sink_row(sk_ref, heads))
            dp = _dot(vg, dog, "nt")
            rs = jnp.sum(p * dp, axis=0, keepdims=True)
            ds = (p * (dp - rs) * ATT_SCALE).astype(BF16)
            sink_row = p_sink * rs
            for t, h in enumerate(heads):
                dsk_ref[h:h + 1, :] += jnp.broadcast_to(
                    -jnp.sum(sink_row[:, BLK * t:BLK * (t + 1)], axis=1, keepdims=True), (1, BLK))
            dvs.append(_dot(p.astype(BF16), dog, "nn"))
            dks.append(_dot(ds, qg, "nn"))
            dqs.append(_dot(ds, kg, "tn"))
        dq = jnp.where(own, dqs[0], dqs[1])
        for t in range(4):
            dp_ref[:, C_Q + BLK * t:C_Q + BLK * (t + 1)] = dq[BLK * t:BLK * (t + 1)].astype(BF16)
        dk = dks[0] + pltpu.roll(dks[1], 64, 1)
        dv = dvs[0] + pltpu.roll(dvs[1], 64, 1)
        dp_ref[:, C_K:C_K + BLK] = (dk[BLK:] + ckv_ref[:, 0:BLK]).astype(BF16)
        dp_ref[:, C_V:C_V + BLK] = (dv[BLK:] + ckv_ref[:, BLK:]).astype(BF16)
        ckv_ref[:, 0:BLK] = dk[:BLK]
        ckv_ref[:, BLK:] = dv[:BLK]
        su, sv = cur[:, C_SU:C_SV], cur[:, C_SV:C_PC]
        gu, vjp_u = jax.vjp(jax.nn.gelu, su)
        vn, vjp_v = jax.vjp(lambda a, g: _rms(jax.nn.gelu(a), g), sv, gs_ref[...])
        vn16 = vn.astype(BF16)
        wmask = _sgu_mask()
        dgu, dvn = [], []
        for g in range(4):
            sl = slice(BLK * g, BLK * (g + 1))
            wm = jnp.where(wmask, ws_ref[g], 0.0).astype(BF16)
            sp = _dot(wm, vn16[:, sl], "nn") + bs_ref[g]
            dyb = dbr_ref[1, :, sl]
            dgu.append(dyb * sp)
            dsp = dyb * gu[:, sl]
            dsp16 = dsp.astype(BF16)
            dvn.append(_dot(wm, dsp16, "tn"))
            dws_ref[g] += jnp.where(wmask, _dot(dsp16, vn16[:, sl], "nt"), 0.0)
            dbs_ref[g] += jnp.sum(dsp, axis=1, keepdims=True)
        (dsu,) = vjp_u(jnp.concatenate(dgu, axis=1))
        dsv, dgs = vjp_v(jnp.concatenate(dvn, axis=1))
        dp_ref[:, C_SU:C_SV] = dsu.astype(BF16)
        dp_ref[:, C_SV:C_PC] = dsv.astype(BF16)
        dgs_ref[...] += dgs
        c = cur[:, C_PC:C_GATE]
        ext_ref[0:HALO, :] = jnp.where(not_first, pcp_ref[:, C_PC:C_GATE], 0.0)
        ext_ref[HALO:HALO + BLK, :] = c
        for g, w in enumerate(POOL_WINDOWS):
            sl = slice(BLK * g, BLK * (g + 1))
            acc = ext_ref[HALO:HALO + BLK, sl]
            for k in range(1, w):
                acc = acc + ext_ref[HALO - k:HALO - k + BLK, sl]
            cnt = _pool_cnt(blk, w)
            pooled16 = (acc / cnt - c[:, sl]).astype(BF16)
            wp16 = wp_ref[g].astype(BF16)
            mixed = _dot(pooled16, wp16, "nn")
            dyc = dbr_ref[2, :, sl]
            dps_ref[:, sl] += jnp.sum(dyc * mixed, axis=0, keepdims=True)
            dmixed16 = (dyc * ps_ref[:, sl]).astype(BF16)
            dwp_ref[g] += _dot(pooled16, dmixed16, "tn")
            dpooled = _dot(dmixed16, wp16, "nt")
            z_ref[HALO:HALO + BLK, sl] = dpooled / cnt
            dext = z_ref[0:HALO + BLK, sl]
            for k in range(1, w):
                dext = dext + z_ref[k:k + HALO + BLK, sl]
            dp_ref[:, C_PC + BLK * g:C_PC + BLK * (g + 1)] = (
                dext[HALO:] - dpooled + jnp.concatenate([jnp.zeros((BLK - HALO, BLK), F32), cpc_ref[:, sl]], axis=0)
            ).astype(BF16)
            cpc_ref[:, sl] = dext[:HALO]

    n_in = 11
    small = [jax.ShapeDtypeStruct((8, BLK), F32), jax.ShapeDtypeStruct((4, BLK, BLK), F32),
             jax.ShapeDtypeStruct((4, BLK, 1), F32), jax.ShapeDtypeStruct((1, 512), F32),
             jax.ShapeDtypeStruct((4, BLK, BLK), F32), jax.ShapeDtypeStruct((1, 512), F32)]
    small_specs = [pl.BlockSpec((8, BLK), lambda i: (0, 0)), pl.BlockSpec((4, BLK, BLK), lambda i: (0, 0, 0)),
                   pl.BlockSpec((4, BLK, 1), lambda i: (0, 0, 0)), pl.BlockSpec((1, 512), lambda i: (0, 0)),
                   pl.BlockSpec((4, BLK, BLK), lambda i: (0, 0, 0)), pl.BlockSpec((1, 512), lambda i: (0, 0))]
    res = pl.pallas_call(
        body, name=name, grid=(nb,),
        in_specs=_mix_in_specs(nb, True) + [
            pl.BlockSpec((3, BLK, 512), lambda i: (0, nb - 1 - i, 0)),
            pl.BlockSpec(memory_space=pl.ANY)],
        out_specs=[pl.BlockSpec((BLK, C_GATE), lambda i: (nb - 1 - i, 0))] + small_specs,
        out_shape=[jax.ShapeDtypeStruct(dproj.shape, dproj.dtype)] + small,
        scratch_shapes=[pltpu.VMEM((HALO + BLK, 512), F32), pltpu.VMEM((2 * HALO + BLK, 512), F32),
                        pltpu.VMEM((BLK, 2 * BLK), F32), pltpu.VMEM((HALO, 512), F32)],
        input_output_aliases={n_in - 1: 0},
        compiler_params=_params(("arbitrary",)),
    )(proj, proj, proj, sinks_b, ws, bs3, gsgu, wp, ps, dbr, dproj)
    return tuple(res)


_GW = 256


def _merge_fwd(proj, pb, name, T=4096):
    S, D = pb.shape[1], pb.shape[2]
    T = min(T, S)

    def body(gate_ref, pb_ref, out_ref, acc_ref):
        n = pl.program_id(2)

        @pl.when(n == 0)
        def _():
            acc_ref[...] = jnp.zeros_like(acc_ref)

        acc_ref[...] += jax.nn.sigmoid(gate_ref[...]) * pb_ref[...]

        @pl.when(n == 2)
        def _():
            out_ref[...] = acc_ref[...].astype(BF16)

    return pl.pallas_call(
        body, name=name, grid=(S // T, D // _GW, 3),
        in_specs=[pl.BlockSpec((T, _GW), lambda i, j, n: (i, C_GATE // _GW + n * (D // _GW) + j)),
                  pl.BlockSpec((None, T, _GW), lambda i, j, n: (n, i, j))],
        out_specs=pl.BlockSpec((T, _GW), lambda i, j, n: (i, j)),
        out_shape=jax.ShapeDtypeStruct((S, D), BF16),
        scratch_shapes=[pltpu.VMEM((T, _GW), F32)],
        compiler_params=_params(("parallel", "parallel", "arbitrary")),
    )(proj, pb)


def _merge_bwd(proj, pb, dmerged, name, T=4096):
    S, D = pb.shape[1], pb.shape[2]
    T = min(T, S)

    def body(gate_ref, pb_ref, dm_ref, dgate_ref, dpb_ref):
        sg = jax.nn.sigmoid(gate_ref[...])
        dm = dm_ref[...]
        dpb_ref[...] = (dm * sg).astype(BF16)
        dgate_ref[...] = (dm * pb_ref[...] * sg * (1.0 - sg)).astype(BF16)

    gate_map = lambda i, n, j: (i, C_GATE // _GW + n * (D // _GW) + j)
    return pl.pallas_call(
        body, name=name, grid=(S // T, 3, D // _GW),
        in_specs=[pl.BlockSpec((T, _GW), gate_map),
                  pl.BlockSpec((None, T, _GW), lambda i, n, j: (n, i, j)),
                  pl.BlockSpec((T, _GW), lambda i, n, j: (i, j))],
        out_specs=[pl.BlockSpec((T, _GW), gate_map),
                   pl.BlockSpec((None, T, _GW), lambda i, n, j: (n, i, j))],
        out_shape=[jax.ShapeDtypeStruct((S, C_END), BF16), jax.ShapeDtypeStruct((3, S, D), BF16)],
        compiler_params=_params(("parallel", "parallel", "parallel")),
    )(proj, pb, dmerged)


def _memattn_fwd(qm, kv, name, T=512):
    S, NM = qm.shape[0], kv.shape[0]
    T = min(T, S)

    def body(q_ref, kv_ref, o_ref):
        for h in range(4):
            sl = slice(128 * h, 128 * (h + 1))
            k = kv_ref[:, sl].astype(BF16)
            v = kv_ref[:, 512 + 128 * h:512 + 128 * (h + 1)].astype(BF16)
            s = _dot(q_ref[:, sl].astype(BF16), k, "nt") * MEM_SCALE
            p = jax.nn.softmax(s, axis=-1)
            o_ref[:, sl] = _dot(p.astype(BF16), v, "nn").astype(BF16)

    return pl.pallas_call(
        body, name=name, grid=(S // T,),
        in_specs=[_row_spec(T, 512), pl.BlockSpec((NM, 1024), lambda i: (0, 0))],
        out_specs=_row_spec(T, 512), out_shape=jax.ShapeDtypeStruct((S, 512), BF16),
        compiler_params=_params(("parallel",)))(qm, kv)


def _memattn_bwd(qm, kv, dom, name, T=512):
    S, NM = qm.shape[0], kv.shape[0]
    T = min(T, S)

    def body(q_ref, kv_ref, do_ref, dq_ref, dkv_ref):
        i = pl.program_id(0)

        @pl.when(i == 0)
        def _():
            dkv_ref[...] = jnp.zeros_like(dkv_ref)

        for h in range(4):
            sl = slice(128 * h, 128 * (h + 1))
            sv_ = slice(512 + 128 * h, 512 + 128 * (h + 1))
            q = q_ref[:, sl].astype(BF16)
            k = kv_ref[:, sl].astype(BF16)
            v = kv_ref[:, sv_].astype(BF16)
            do = do_ref[:, sl].astype(BF16)
            p = jax.nn.softmax(_dot(q, k, "nt") * MEM_SCALE, axis=-1)
            dp = _dot(do, v, "nt")
            ds = (p * (dp - jnp.sum(p * dp, axis=-1, keepdims=True)) * MEM_SCALE).astype(BF16)
            dq_ref[:, sl] = _dot(ds, k, "nn").astype(BF16)
            dkv_ref[:, sl] += _dot(ds, q, "tn")
            dkv_ref[:, sv_] += _dot(p.astype(BF16), do, "tn")

    return pl.pallas_call(
        body, name=name, grid=(S // T,),
        in_specs=[_row_spec(T, 512), pl.BlockSpec((NM, 1024), lambda i: (0, 0)), _row_spec(T, 512)],
        out_specs=[_row_spec(T, 512), pl.BlockSpec((NM, 1024), lambda i: (0, 0))],
        out_shape=[jax.ShapeDtypeStruct((S, 512), BF16), jax.ShapeDtypeStruct((NM, 1024), F32)],
        compiler_params=_params(("arbitrary",)))(qm, kv, dom)


def _adamw(w, g, m, v, name, TR=512):
    R, C = w.shape
    TR = R if R <= TR else _row_tile(R, TR)
    c1 = 1.0 - ADAM_B1 ** ADAM_STEP
    c2 = 1.0 - ADAM_B2 ** ADAM_STEP

    def body(w_ref, g_ref, m_ref, v_ref, d_ref, nm_ref, nv_ref):
        gv = g_ref[...]
        nm = ADAM_B1 * m_ref[...] + (1.0 - ADAM_B1) * gv
        nv = ADAM_B2 * v_ref[...] + (1.0 - ADAM_B2) * jnp.square(gv)
        d_ref[...] = -ADAM_LR * ((nm / c1) / (jnp.sqrt(nv / c2) + ADAM_EPS) + ADAM_WD * w_ref[...])
        nm_ref[...] = nm
        nv_ref[...] = nv

    spec = pl.BlockSpec((TR, C), lambda i: (i, 0))
    return pl.pallas_call(
        body, name=name, grid=(R // TR,), in_specs=[spec] * 4, out_specs=[spec] * 3,
        out_shape=[jax.ShapeDtypeStruct((R, C), F32)] * 3,
        compiler_params=_params(("parallel",)))(w, g, m, v)


def _row_tile(R, pref):
    t = (pref // 8) * 8
    while t >= 8:
        if R % t == 0:
            return t
        t -= 8
    raise ValueError(f"no row tile for {R}")


def _sum_slots(stack, name, TR=512):
    n, R, C = stack.shape
    TR = R if R <= TR else _row_tile(R, TR)

    def body(s_ref, o_ref):
        acc = s_ref[0]
        for k in range(1, n):
            acc = acc + s_ref[k]
        o_ref[...] = acc

    return pl.pallas_call(
        body, name=name, grid=(R // TR,),
        in_specs=[pl.BlockSpec((n, TR, C), lambda i: (0, i, 0))],
        out_specs=pl.BlockSpec((TR, C), lambda i: (i, 0)),
        out_shape=jax.ShapeDtypeStruct((R, C), F32),
        compiler_params=_params(("parallel",)))(stack)


_ANY = pl.BlockSpec(memory_space=pl.ANY)


def _chip_of(j, c):
    return (j // 2, j % 2, c)


def _own_slab(shard, dtype, j_arr, name, first=0, count=None, plus=None, deps=(), TR=512):
    N, r, C = shard.shape
    B = N if count is None else count
    rh = r // 2
    TR = rh if rh <= TR else _row_tile(rh, TR)
    nt = rh // TR
    ins = [shard] if plus is None else [shard, plus]

    def body(j_ref, *refs):
        val = refs[0][...] if plus is None else refs[0][...] + refs[1][...]
        refs[-1][...] = val.astype(refs[-1].dtype)

    return pl.pallas_call(
        body, name=name,
        grid_spec=pltpu.PrefetchScalarGridSpec(
            num_scalar_prefetch=1, grid=(B, 2, nt),
            in_specs=[pl.BlockSpec((None, TR, C), lambda b, h, t, jr: (first + b, h * nt + t, 0))] * len(ins)
            + [_ANY] * len(deps),
            out_specs=pl.BlockSpec((None, None, None, TR, C), lambda b, h, t, jr: (b, jr[0], h, t, 0))),
        out_shape=jax.ShapeDtypeStruct((B, 4, 2, rh, C), dtype),
        compiler_params=_params(("parallel", "parallel", "parallel")),
    )(j_arr, *ins, *deps)


def _gather_weights(bufs, name):
    n = len(bufs)

    def body(*refs):
        buf = refs[n:2 * n]
        send_sems, recv_sems, fsend_sems, frecv_sems = refs[2 * n:]
        x, y, c = lax.axis_index("x"), lax.axis_index("y"), lax.axis_index("c")
        j = 2 * x + y
        sib = (x, y, 1 - c)
        sends = []
        for d in range(1, 4):
            for a in range(n):
                cp = pltpu.make_async_remote_copy(
                    src_ref=buf[a].at[:, j, c], dst_ref=buf[a].at[:, j, c], send_sem=send_sems.at[a, d - 1],
                    recv_sem=recv_sems.at[a, d - 1], device_id=_chip_of((j + d) % 4, c), device_id_type=MESH)
                cp.start()
                sends.append(cp)
        for d in range(1, 4):
            frm = (j + 4 - d) % 4
            for a in range(n):
                pltpu.make_async_remote_copy(
                    src_ref=buf[a].at[:, frm, c], dst_ref=buf[a].at[:, frm, c], send_sem=send_sems.at[a, d - 1],
                    recv_sem=recv_sems.at[a, d - 1], device_id=_chip_of(frm, c), device_id_type=MESH).wait_recv()
                cp = pltpu.make_async_remote_copy(
                    src_ref=buf[a].at[:, frm, c], dst_ref=buf[a].at[:, frm, c], send_sem=fsend_sems.at[a, d - 1],
                    recv_sem=frecv_sems.at[a, d - 1], device_id=sib, device_id_type=MESH)
                cp.start()
                sends.append(cp)
        for d in range(1, 4):
            frm = (j + 4 - d) % 4
            for a in range(n):
                pltpu.make_async_remote_copy(
                    src_ref=buf[a].at[:, frm, 1 - c], dst_ref=buf[a].at[:, frm, 1 - c], send_sem=fsend_sems.at[a, d - 1],
                    recv_sem=frecv_sems.at[a, d - 1], device_id=sib, device_id_type=MESH).wait_recv()
        for cp in sends:
            cp.wait_send()

    return pl.pallas_call(
        body, name=name,
        in_specs=[_ANY] * n, out_specs=[_ANY] * n,
        out_shape=[jax.ShapeDtypeStruct(b.shape, b.dtype) for b in bufs],
        scratch_shapes=[pltpu.SemaphoreType.DMA((n, 3))] * 4,
        input_output_aliases={a: a for a in range(n)},
    )(*bufs)


_HBM = pl.BlockSpec(memory_space=pltpu.HBM)
_SEM = pl.BlockSpec(memory_space=pltpu.SEMAPHORE)
_DATAFLOW = pltpu.SideEffectType.DATAFLOW_SIDE_EFFECTING


def _in_hbm(arrays):
    return [pltpu.with_memory_space_constraint(a, pltpu.HBM) for a in arrays]


def _start_copies(bufs, plan, count, name, deps=()):
    n, k = len(bufs), len(deps)

    def body(*refs):
        send_sems, recv_sems = refs[n + k], refs[n + k + 1]
        for i, (src, dst, dev) in enumerate(plan(refs[:n], False)):
            pltpu.make_async_remote_copy(src_ref=src, dst_ref=dst, send_sem=send_sems.at[i], recv_sem=recv_sems.at[i],
                                         device_id=dev, device_id_type=MESH).start()
        refs[-1][...] = jnp.zeros_like(refs[-1])

    return pl.pallas_call(
        body, name=name,
        out_shape=(pltpu.SemaphoreType.DMA((count,)), pltpu.SemaphoreType.DMA((count,)),
                   *[pltpu.HBM(b.shape, b.dtype) for b in bufs], jax.ShapeDtypeStruct((8, 128), F32)),
        in_specs=[_HBM] * n + [_ANY] * k,
        out_specs=(_SEM, _SEM, *[_HBM] * n, pl.BlockSpec(memory_space=pltpu.VMEM)),
        input_output_aliases={a: 2 + a for a in range(n)},
        compiler_params=pltpu.CompilerParams(has_side_effects=_DATAFLOW),
    )(*_in_hbm(bufs), *deps)


def _wait_copies(handle, plan, afters, name):
    send_sems, recv_sems, *bufs = handle[:-1]
    n = len(bufs)

    def body(*refs):
        send_sems, recv_sems = refs[n], refs[n + 1]
        for i, (src, dst, dev) in enumerate(plan(refs[:n], True)):
            cp = pltpu.make_async_remote_copy(src_ref=src, dst_ref=dst, send_sem=send_sems.at[i], recv_sem=recv_sems.at[i],
                                              device_id=dev, device_id_type=MESH)
            cp.wait_send()
            cp.wait_recv()

    return list(pl.pallas_call(
        body, name=name,
        out_shape=[pltpu.HBM(b.shape, b.dtype) for b in bufs],
        in_specs=[_HBM] * n + [_SEM, _SEM] + [_ANY] * len(afters), out_specs=[_HBM] * n,
        input_output_aliases={a: a for a in range(n)},
        compiler_params=pltpu.CompilerParams(has_side_effects=_DATAFLOW),
    )(*bufs, send_sems, recv_sems, *afters))


def _gather_plan(buf, waiting):
    c = lax.axis_index("c")
    j = 2 * lax.axis_index("x") + lax.axis_index("y")
    copies = []
    for d in range(1, 4):
        to, frm = (j + d) % 4, (j + 4 - d) % 4
        for b in buf:
            copies.append((b.at[:, j, c], b.at[:, frm if waiting else j, c], _chip_of(frm if waiting else to, c)))
    return copies


def _chip_plan(buf, waiting):
    n = len(buf) // 2
    c = lax.axis_index("c")
    j = 2 * lax.axis_index("x") + lax.axis_index("y")
    copies = []
    for d in range(1, 4):
        to = (j + d) % 4
        for a in range(n):
            copies.append((buf[a].at[to], buf[n + a].at[d - 1], _chip_of(to, c)))
    return copies


def _pair_plan(buf, waiting):
    n = len(buf) // 2
    c = lax.axis_index("c")
    sib = (lax.axis_index("x"), lax.axis_index("y"), 1 - c)
    return [(buf[a].at[:, pl.ds(1 - c, 1)], buf[n + a], sib) for a in range(n)]


def _gather_forward(bufs, name):
    n = len(bufs)

    def body(*refs):
        buf = refs[n:2 * n]
        send_sems, recv_sems = refs[2 * n:]
        x, y, c = lax.axis_index("x"), lax.axis_index("y"), lax.axis_index("c")
        j = 2 * x + y
        sib = (x, y, 1 - c)
        sends = []
        for d in range(1, 4):
            frm = (j + 4 - d) % 4
            for a in range(n):
                cp = pltpu.make_async_remote_copy(
                    src_ref=buf[a].at[:, frm, c], dst_ref=buf[a].at[:, frm, c], send_sem=send_sems.at[a, d - 1],
                    recv_sem=recv_sems.at[a, d - 1], device_id=sib, device_id_type=MESH)
                cp.start()
                sends.append(cp)
        for d in range(1, 4):
            frm = (j + 4 - d) % 4
            for a in range(n):
                pltpu.make_async_remote_copy(
                    src_ref=buf[a].at[:, frm, 1 - c], dst_ref=buf[a].at[:, frm, 1 - c], send_sem=send_sems.at[a, d - 1],
                    recv_sem=recv_sems.at[a, d - 1], device_id=sib, device_id_type=MESH).wait_recv()
        for cp in sends:
            cp.wait_send()

    return pl.pallas_call(
        body, name=name,
        in_specs=[_ANY] * n, out_specs=[_ANY] * n,
        out_shape=[jax.ShapeDtypeStruct(b.shape, b.dtype) for b in bufs],
        scratch_shapes=[pltpu.SemaphoreType.DMA((n, 3))] * 2,
        input_output_aliases={a: a for a in range(n)},
    )(*bufs)


def _pair_add(g4, r1, cj_arr, name, TR=512):
    B4, _, rh, C = g4.shape
    B = B4 // 4
    TR = rh if rh <= TR else _row_tile(rh, TR)

    def body(cj_ref, g_ref, r_ref, o16_ref, own_ref):
        s = g_ref[...] + r_ref[...]
        o16_ref[...] = s.astype(BF16)

        @pl.when(pl.program_id(2) == cj_ref[1])
        def _():
            own_ref[...] = s

    return pl.pallas_call(
        body, name=name,
        grid_spec=pltpu.PrefetchScalarGridSpec(
            num_scalar_prefetch=1, grid=(B, rh // TR, 4),
            in_specs=[pl.BlockSpec((None, None, TR, C), lambda b, t, p, cj: (b * 4 + p, cj[0], t, 0)),
                      pl.BlockSpec((None, None, TR, C), lambda b, t, p, cj: (b * 4 + p, 0, t, 0))],
            out_specs=[pl.BlockSpec((None, None, TR, C), lambda b, t, p, cj: (p, b, t, 0)),
                       pl.BlockSpec((None, TR, C), lambda b, t, p, cj: (b, t, 0))]),
        out_shape=[jax.ShapeDtypeStruct((4, B, rh, C), BF16), jax.ShapeDtypeStruct((B, rh, C), F32)],
        compiler_params=_params(("parallel", "parallel", "arbitrary")),
    )(cj_arr, g4, r1)


def _chip_add(own, r2, cj_arr, into, first, name, TR=512):
    B, rh, C = own.shape
    TR = rh if rh <= TR else _row_tile(rh, TR)

    def body(cj_ref, p_ref, r_ref, _into_ref, o_ref):
        o_ref[...] = p_ref[...] + r_ref[0].astype(F32) + r_ref[1].astype(F32) + r_ref[2].astype(F32)

    return pl.pallas_call(
        body, name=name,
        grid_spec=pltpu.PrefetchScalarGridSpec(
            num_scalar_prefetch=1, grid=(B, rh // TR),
            in_specs=[pl.BlockSpec((None, TR, C), lambda b, t, cj: (b, t, 0)),
                      pl.BlockSpec((3, None, TR, C), lambda b, t, cj: (0, b, t, 0)),
                      _ANY],
            out_specs=pl.BlockSpec((None, None, TR, C), lambda b, t, cj: (first + b, cj[0], t, 0))),
        out_shape=jax.ShapeDtypeStruct(into.shape, F32),
        input_output_aliases={3: 0},
        compiler_params=_params(("parallel", "parallel")),
    )(cj_arr, own, r2, into)


def _pair_share(bufs, name):
    n = len(bufs)

    def body(*refs):
        buf = refs[n:2 * n]
        send_sems, recv_sems = refs[2 * n:]
        c = lax.axis_index("c")
        sib = (lax.axis_index("x"), lax.axis_index("y"), 1 - c)
        cps = []
        for a in range(n):
            cp = pltpu.make_async_remote_copy(
                src_ref=buf[a].at[:, c], dst_ref=buf[a].at[:, c], send_sem=send_sems.at[a],
                recv_sem=recv_sems.at[a], device_id=sib, device_id_type=MESH)
            cp.start()
            cps.append(cp)
        for a in range(n):
            pltpu.make_async_remote_copy(
                src_ref=buf[a].at[:, 1 - c], dst_ref=buf[a].at[:, 1 - c], send_sem=send_sems.at[a],
                recv_sem=recv_sems.at[a], device_id=sib, device_id_type=MESH).wait_recv()
        for cp in cps:
            cp.wait_send()

    return pl.pallas_call(
        body, name=name, in_specs=[_ANY] * n, out_specs=[_ANY] * n,
        out_shape=[jax.ShapeDtypeStruct(b.shape, b.dtype) for b in bufs],
        scratch_shapes=[pltpu.SemaphoreType.DMA((n,)), pltpu.SemaphoreType.DMA((n,))],
        input_output_aliases={a: a for a in range(n)},
    )(*bufs)


def _pair_swap(arr, name):
    def body(src, dst, send_sem, recv_sem):
        sib = (lax.axis_index("x"), lax.axis_index("y"), 1 - lax.axis_index("c"))
        cp = pltpu.make_async_remote_copy(src_ref=src, dst_ref=dst, send_sem=send_sem, recv_sem=recv_sem,
                                          device_id=sib, device_id_type=MESH)
        cp.start()
        cp.wait_recv()
        cp.wait_send()

    return pl.pallas_call(
        body, name=name, in_specs=[_ANY], out_specs=_ANY,
        out_shape=jax.ShapeDtypeStruct(arr.shape, arr.dtype),
        scratch_shapes=[pltpu.SemaphoreType.DMA, pltpu.SemaphoreType.DMA],
    )(arr)


class _ReduceScatter:
    def __init__(self, n_layers, cj_arr):
        self.L, self.cj = n_layers, cj_arr
        self.total = None
        self.pair = None
        self.chip = None

    def _land(self, after):
        handle, layer, owns = self.chip
        n = len(owns)
        r2 = _wait_copies(handle, _chip_plan, (after,), "rs_chip_wait")[n:]
        if self.total is None:
            self.total = [lax.empty((self.L * o.shape[0], 2) + o.shape[1:], F32) for o in owns]
        self.total = [_chip_add(o, r, self.cj, t, layer * o.shape[0], "rs_chip_add")
                      for o, r, t in zip(owns, r2, self.total)]
        self.chip = None

    def add_layer(self, layer, grads):
        g4 = [g.reshape(g.shape[0] * 4, 2, g.shape[1] // 8, g.shape[2]) for g in grads]
        lands = [lax.empty((g.shape[0], 1) + g.shape[2:], F32) for g in g4]
        handle = _start_copies(g4 + lands, _pair_plan, len(g4), "rs_pair_start")
        self.pair = (handle, layer)
        return (handle[-1],)

    def advance(self, after):
        if self.pair is None:
            return ()
        handle, layer = self.pair
        both = _wait_copies(handle, _pair_plan, (after,), "rs_pair_wait")
        n = len(both) // 2
        added = [_pair_add(g, r, self.cj, "rs_pair_add") for g, r in zip(both[:n], both[n:])]
        parts, owns = [p for p, _ in added], [o for _, o in added]
        if self.chip is not None:
            self._land(owns[-1])
        lands = [lax.empty((3,) + p.shape[1:], p.dtype) for p in parts]
        handle = _start_copies(parts + lands, _chip_plan, 3 * n, "rs_chip_start")
        self.pair, self.chip = None, (handle, layer, owns)
        return (handle[-1],)

    def result(self, after):
        self._land(after)
        full = _pair_share(self.total, "rs_pair_share")
        return [f.reshape(f.shape[0], f.shape[1] * f.shape[2], f.shape[3]) for f in full]


def _relu2_epi(acc):
    return acc, jnp.square(jnp.maximum(acc, 0.0))


def _relu2_bwd_epi(acc, u):
    return (acc * (2.0 * jnp.maximum(u, 0.0)),)


_GRAD_ORDER = ("winT", "wbT", "wout", "wq", "wkv", "woT", "wupT", "wdown")
_DW = dict(tm=512, tn=1024, tk=4096)
_LONG_K = dict(tm=1024, tn=1024, tk=2048)


def _forward_backward(x, mem, target, weights_of, P, grads_done, grads_advance, first_deps=()):
    L = P["g_norm"].shape[0]
    S, D = x.shape
    gn = lambda l, i: P["g_norm"][l, i][None]

    saved = []
    (h,) = _resnorm_fwd(x, None, None, gn(0, 0), "norm_in", deps=first_deps)
    xr = x
    for l in range(L):
        W = weights_of(l, xr)
        proj = _mm(h, W["winT"], "nt", "in_proj", b_pre=(0,), tn=1792)
        small = (jnp.broadcast_to(P["sinks"][l][:, None], (8, BLK)), P["ws"][l], P["bs"][l][:, :, None],
                 P["gsgu"][l][None], P["wp"][l], P["ps"][l][None])
        br = _mix_fwd(proj, *small, "mix_fwd")
        pb = lax.empty((3, S, D), F32)
        for n in range(3):
            pb = _mm(br, W["wbT"], "nt", "branch_proj", a_pre=(n,), b_pre=(n,), into=pb, out_pre=(n,))
        merged = _merge_fwd(proj, pb, "merge_fwd")
        z = _mm(merged, W["wout"], "nn", "out_proj", b_pre=(0,))
        x1, hm = _resnorm_fwd(xr, z, gn(l, 1), gn(l, 2), "resnorm_fwd")
        qm = _mm(hm, W["wq"], "nn", "mem_q", b_pre=(0,))
        (memn,) = _resnorm_fwd(mem, None, None, P["g_mem"][l][None], "mem_norm")
        kv = _mm(memn, W["wkv"], "nn", "mem_kv", b_pre=(0,))
        om = _memattn_fwd(qm, kv, "memattn_fwd")
        ym = _mm(om, W["woT"], "nt", "mem_o", b_pre=(0,))
        x2, hf = _resnorm_fwd(x1, ym, gn(l, 3), gn(l, 4), "resnorm_fwd")
        u, a = _mm(hf, W["wupT"], "nt", "mlp_up", b_pre=(0,), out_dtypes=(F32, BF16), epi=_relu2_epi)
        yf = _mm(a, W["wdown"], "nn", "mlp_down", b_pre=(0,), **_LONG_K)
        saved.append(dict(W=W, x0=xr, h=h, proj=proj, small=small, br=br, pb=pb, merged=merged, z=z, x1=x1, hm=hm,
                          qm=qm, memn=memn, kv=kv, om=om, ym=ym, x2=x2, hf=hf, u=u, a=a, yf=yf))
        if l < L - 1:
            xr, h = _resnorm_fwd(x2, yf, gn(l, 5), gn(l + 1, 0), "resnorm_fwd")
    dres, loss = _final_fwd(saved[-1]["x2"], saved[-1]["yf"], gn(L - 1, 5), target, "loss_head")

    dgn = [[None] * 6 for _ in range(L)]
    dsmall = {k: [None] * L for k in ("g_mem", "sinks", "ws", "bs", "gsgu", "wp", "ps")}
    dh = None
    for l in reversed(range(L)):
        s = saved[l]
        W, G = s["W"], {}
        if l == L - 1:
            dx2, dyf, dgn[l][5] = _resnorm_bwd(s["x2"], s["yf"], gn(l, 5), None, dres, None, "resnorm_bwd_top")
        else:
            dx2, dyf, dgn[l][5], dgn[l + 1][0] = _resnorm_bwd(s["x2"], s["yf"], gn(l, 5), gn(l + 1, 0), dres, dh,
                                                              "resnorm_bwd", deps=deps)
        du = _mm(dyf, W["wdown"], "nt", "mlp_down_dx", b_pre=(0,), out_dtypes=(BF16,), extras=(s["u"],), epi=_relu2_bwd_epi)
        G["wdown"] = _mm(s["a"], dyf, "tn", "mlp_down_dw", **_DW)[None]
        dhf = _mm(du, W["wupT"], "nn", "mlp_up_dx", b_pre=(0,), **_LONG_K)
        G["wupT"] = _mm(du, s["hf"], "tn", "mlp_up_dw", **_DW)[None]
        dx1, dym, dgn[l][3], dgn[l][4] = _resnorm_bwd(s["x1"], s["ym"], gn(l, 3), gn(l, 4), dx2, dhf, "resnorm_bwd")
        dom = _mm(dym, W["woT"], "nn", "mem_o_dx", b_pre=(0,), deps=grads_advance(dx1))
        G["woT"] = _mm(dym, s["om"], "tn", "mem_o_dw", **_DW)[None]
        dqm, dkv = _memattn_bwd(s["qm"], s["kv"], dom, "memattn_bwd")
        dmemn = _mm(dkv, W["wkv"], "nt", "mem_kv_dx", b_pre=(0,))
        G["wkv"] = _mm(s["memn"], dkv, "tn", "mem_kv_dw")[None]
        _, dsmall["g_mem"][l] = _resnorm_bwd(mem, None, None, P["g_mem"][l][None], None, dmemn, "mem_norm_bwd")
        dhm = _mm(dqm, W["wq"], "nt", "mem_q_dx", b_pre=(0,))
        G["wq"] = _mm(s["hm"], dqm, "tn", "mem_q_dw", **_DW)[None]
        dx0, dz, dgn[l][1], dgn[l][2] = _resnorm_bwd(s["x0"], s["z"], gn(l, 1), gn(l, 2), dx1, dhm, "resnorm_bwd")
        dmerged = _mm(dz, W["wout"], "nt", "out_proj_dx", b_pre=(0,))
        G["wout"] = _mm(s["merged"], dz, "tn", "out_proj_dw", **_DW)[None]
        dproj, dpb = _merge_bwd(s["proj"], s["pb"], dmerged, "merge_bwd")
        dbr = lax.empty((3, S, 512), F32)
        G["wbT"] = lax.empty(W["wbT"].shape, F32)
        for n in range(3):
            dbr = _mm(dpb, W["wbT"], "nn", "branch_proj_dx", a_pre=(n,), b_pre=(n,), into=dbr, out_pre=(n,))
            G["wbT"] = _mm(dpb, s["br"], "tn", "branch_proj_dw", a_pre=(n,), b_pre=(n,), into=G["wbT"], out_pre=(n,), **_DW)
        (dproj, dsmall["sinks"][l], dsmall["ws"][l], dsmall["bs"][l], dsmall["gsgu"][l], dsmall["wp"][l],
         dsmall["ps"][l]) = _mix_bwd(s["proj"], dbr, dproj, *s["small"], "mix_bwd")
        dh = _mm(dproj, W["winT"], "nn", "in_proj_dx", b_pre=(0,), **_LONG_K)
        G["winT"] = _mm(dproj, s["h"], "tn", "in_proj_dw", **_DW)[None]
        deps = grads_done(l, [G[k] for k in _GRAD_ORDER])
        dres = dx0
    grad_x, dgn[0][0] = _resnorm_bwd(x, None, None, gn(0, 0), dres, dh, "norm_in_bwd", deps=deps)
    tail_deps = grads_advance(grad_x)

    small_grads = dict(
        g_norm=jnp.stack([jnp.concatenate(row, axis=0) for row in dgn]),
        g_mem=jnp.concatenate(dsmall["g_mem"], axis=0),
        sinks=jnp.stack([d[:, 0] for d in dsmall["sinks"]]),
        ws=jnp.stack(dsmall["ws"]),
        bs=jnp.stack([d[:, :, 0] for d in dsmall["bs"]]),
        gsgu=jnp.concatenate(dsmall["gsgu"], axis=0),
        wp=jnp.stack(dsmall["wp"]),
        ps=jnp.concatenate(dsmall["ps"], axis=0),
    )
    return loss, grad_x, small_grads, tail_deps


_PACK_ROWS = 512


def _as_rows(a):
    n = math.prod(a.shape)
    if n % 128:
        a = jnp.pad(a.reshape(-1), (0, (-n) % 128))
    r = a.reshape(-1, 128)
    return jnp.pad(r, ((0, (-r.shape[0]) % 8), (0, 0))) if r.shape[0] % 8 else r


def _pack(arrays):
    rows = [_as_rows(a) for a in arrays]
    total = sum(r.shape[0] for r in rows)
    tail = (-total) % _PACK_ROWS
    if tail:
        rows.append(jnp.zeros((tail, 128), rows[0].dtype))
    return jnp.concatenate(rows, axis=0)


def _unpack(packed, like):
    out, pos = [], 0
    for a in like:
        n = math.prod(a.shape)
        nr = -(-n // 128)
        rows = packed[pos:pos + nr]
        out.append((rows.reshape(-1)[:n] if n % 128 else rows).reshape(a.shape))
        pos += nr + (-nr) % 8
    return out


_BIG = ("w_in", "w_branch", "w_out", "w_q_mem", "w_kv_mem", "w_o_mem", "w_up", "w_down")
_SMALL = ("g_norm", "g_mem", "attn_sinks", "w_spatial", "b_spatial", "g_sgu", "w_pool", "pool_scale")
_WEIGHTS = ("g_norm", "g_mem", "w_in", "attn_sinks", "w_spatial", "b_spatial", "g_sgu", "w_pool", "pool_scale",
            "w_branch", "w_out", "w_q_mem", "w_kv_mem", "w_o_mem", "w_up", "w_down")


def _to_working(name, w):
    if name == "w_in":
        return jnp.swapaxes(w, 1, 2)
    if name == "w_branch":
        t = jnp.swapaxes(w, 2, 3)
        return t.reshape(t.shape[0] * 3, t.shape[2], t.shape[3])
    if name in ("w_o_mem", "w_up"):
        return jnp.swapaxes(w, 1, 2)
    return w


def _from_working(name, g):
    if name == "w_in":
        return jnp.swapaxes(g, 1, 2)
    if name == "w_branch":
        return jnp.swapaxes(g.reshape(g.shape[0] // 3, 3, g.shape[1], g.shape[2]), 2, 3)
    if name in ("w_o_mem", "w_up"):
        return jnp.swapaxes(g, 1, 2)
    return g


_WKEY = dict(w_in="winT", w_branch="wbT", w_out="wout", w_q_mem="wq", w_kv_mem="wkv", w_o_mem="woT",
             w_up="wupT", w_down="wdown")


def kernel(x, mem, g_norm, g_mem, w_in, attn_sinks, w_spatial, b_spatial, g_sgu, w_pool, pool_scale, w_branch, w_out, w_q_mem, w_kv_mem, w_o_mem, w_up, w_down, loss_target, m_g_norm, m_g_mem, m_w_in, m_attn_sinks, m_w_spatial, m_b_spatial, m_g_sgu, m_w_pool, m_pool_scale, m_w_branch, m_w_out, m_w_q_mem, m_w_kv_mem, m_w_o_mem, m_w_up, m_w_down, v_g_norm, v_g_mem, v_w_in, v_attn_sinks, v_w_spatial, v_b_spatial, v_g_sgu, v_w_pool, v_pool_scale, v_w_branch, v_w_out, v_w_q_mem, v_w_kv_mem, v_w_o_mem, v_w_up, v_w_down):
    w = dict(g_norm=g_norm, g_mem=g_mem, w_in=w_in, attn_sinks=attn_sinks, w_spatial=w_spatial, b_spatial=b_spatial,
             g_sgu=g_sgu, w_pool=w_pool, pool_scale=pool_scale, w_branch=w_branch, w_out=w_out, w_q_mem=w_q_mem,
             w_kv_mem=w_kv_mem, w_o_mem=w_o_mem, w_up=w_up, w_down=w_down)
    m = dict(g_norm=m_g_norm, g_mem=m_g_mem, w_in=m_w_in, attn_sinks=m_attn_sinks, w_spatial=m_w_spatial,
             b_spatial=m_b_spatial, g_sgu=m_g_sgu, w_pool=m_w_pool, pool_scale=m_pool_scale, w_branch=m_w_branch,
             w_out=m_w_out, w_q_mem=m_w_q_mem, w_kv_mem=m_w_kv_mem, w_o_mem=m_w_o_mem, w_up=m_w_up, w_down=m_w_down)
    v = dict(g_norm=v_g_norm, g_mem=v_g_mem, w_in=v_w_in, attn_sinks=v_attn_sinks, w_spatial=v_w_spatial,
             b_spatial=v_b_spatial, g_sgu=v_g_sgu, w_pool=v_w_pool, pool_scale=v_pool_scale, w_branch=v_w_branch,
             w_out=v_w_out, w_q_mem=v_w_q_mem, w_kv_mem=v_w_kv_mem, w_o_mem=v_w_o_mem, w_up=v_w_up, w_down=v_w_down)
    L = g_norm.shape[0]
    j = 2 * lax.axis_index("x") + lax.axis_index("y")
    c = lax.axis_index("c")
    j_arr = jnp.reshape(j, (1,)).astype(jnp.int32)
    cj_arr = jnp.stack([c, j]).astype(jnp.int32)

    gs = g_norm.shape[2]
    working = [_to_working(n, w[n]) for n in _BIG]
    per_layer = [wk.shape[0] // L for wk in working]

    def own_slabs(l):
        return [_own_slab(wk, BF16, j_arr, "own_slab", first=l * b, count=b) for wk, b in zip(working, per_layer)]

    first_bufs = own_slabs(0) + [_own_slab(g_norm.reshape(1, L * 6 * gs // 128, 128), F32, j_arr, "own_slab_norm")]
    first_handle = _start_copies(first_bufs, _gather_plan, 3 * len(first_bufs), "gather_start_first")
    slabs = {l: own_slabs(l) for l in range(1, L)}
    first = _gather_forward(_wait_copies(first_handle, _gather_plan, [s for l in slabs for s in slabs[l]],
                                         "gather_wait_first"), "gather_forward_first")
    in_flight, dep = {}, first[0]
    for l in range(1, L):
        in_flight[l] = _start_copies(slabs[l], _gather_plan, 3 * len(slabs[l]), "gather_start", deps=(dep,))
        dep = in_flight[l][-1]
    gn_full = jnp.transpose(first[-1].reshape(4, L * 6, gs), (1, 0, 2)).reshape(L, 6, 4 * gs)
    P = dict(g_norm=gn_full, g_mem=g_mem, sinks=attn_sinks, ws=w_spatial, bs=b_spatial, gsgu=g_sgu, wp=w_pool,
             ps=pool_scale)

    def weights_of(l, after):
        got = first[:-1] if l == 0 else _gather_forward(
            _wait_copies(in_flight[l], _gather_plan, (after,), "gather_wait"), "gather_forward")
        return {k: g.reshape(g.shape[0], 8 * g.shape[3], g.shape[4]) for k, g in zip(_GRAD_ORDER, got)}

    rs = _ReduceScatter(L, cj_arr)
    loss_part, grad_x, sg, tail_deps = _forward_backward(
        x[0], mem[0], loss_target[0], weights_of, P, rs.add_layer, rs.advance,
        first_deps=[h[-1] for h in in_flight.values()])
    loss = lax.psum(loss_part[0, 0], ("x", "y", "c"))

    full_small = [sg["g_norm"], sg["g_mem"], sg["sinks"], sg["ws"], sg["bs"], sg["gsgu"], sg["wp"], sg["ps"]]
    packed = _pack(full_small)
    pair_sum = _own_slab(packed[None], F32, j_arr, "small_grads_pair_sum", deps=tail_deps,
                         plus=_pair_swap(packed, "small_grads_swap")[None])
    (chip_sums,) = _gather_weights([pair_sum], "gather_small_grads")
    total = _sum_slots(chip_sums.reshape(4, *packed.shape), "sum_small_grads")
    grads = {n: _from_working(n, g) for n, g in zip(_BIG, rs.result(total))}
    for n, g in zip(_SMALL, _unpack(total, full_small)):
        grads[n] = lax.dynamic_slice_in_dim(g, j * g_norm.shape[2], g_norm.shape[2], axis=2) if n == "g_norm" else g

    delta, new_m, new_v = {}, {}, {}
    for n in _BIG:
        view = (lambda t: jnp.swapaxes(t, 1, 2)) if n == "w_in" else (lambda t: t)
        shp = view(w[n]).shape
        two_d = lambda t: view(t).reshape(-1, shp[-1])
        d_, m_, v_ = _adamw(two_d(w[n]), two_d(grads[n]), two_d(m[n]), two_d(v[n]), "adamw")
        delta[n], new_m[n], new_v[n] = view(d_.reshape(shp)), view(m_.reshape(shp)), view(v_.reshape(shp))
    small_w = [w[n] for n in _SMALL]
    d_, m_, v_ = _adamw(_pack(small_w), _pack([grads[n] for n in _SMALL]), _pack([m[n] for n in _SMALL]),
                        _pack([v[n] for n in _SMALL]), "adamw_small")
    for n, dd, mm_, vv in zip(_SMALL, _unpack(d_, small_w), _unpack(m_, small_w), _unpack(v_, small_w)):
        delta[n], new_m[n], new_v[n] = dd, mm_, vv

    return (loss, grad_x[None], *[grads[n] for n in _WEIGHTS], *[delta[n] for n in _WEIGHTS],
            *[new_m[n] for n in _WEIGHTS], *[new_v[n] for n in _WEIGHTS])
```

```python
import functools
import math

import jax
import jax.numpy as jnp
from jax import lax
from jax.experimental import pallas as pl
from jax.experimental.pallas import tpu as pltpu

F32 = jnp.float32
BF16 = jnp.bfloat16
MESH = pl.DeviceIdType.MESH

EPS = 1e-6
NEG_INF = -1e30
BLK = 128
HALO = 16
POOL_WINDOWS = (2, 4, 8, 16)
ATT_SCALE = 1.0 / math.sqrt(64.0)
MEM_SCALE = 1.0 / math.sqrt(128.0)
C_Q, C_K, C_V, C_SU, C_SV, C_PC, C_GATE, C_END = 0, 512, 640, 768, 1280, 1792, 2304, 5376

ADAM_LR, ADAM_B1, ADAM_B2, ADAM_EPS, ADAM_WD, ADAM_STEP = 0.001, 0.9, 0.999, 1e-08, 0.01, 10

VMEM_LIMIT_BYTES = 56 * 1024 * 1024

_DIMS = {
    "nn": (((1,), (0,)), ((), ())),
    "nt": (((1,), (1,)), ((), ())),
    "tn": (((0,), (0,)), ((), ())),
}


def _dot(a, b, mode):
    return lax.dot_general(a, b, _DIMS[mode], preferred_element_type=F32)


def _params(semantics):
    return pltpu.CompilerParams(dimension_semantics=semantics, vmem_limit_bytes=VMEM_LIMIT_BYTES)


def _tile(dim, pref):
    if dim <= pref:
        return dim
    t = (pref // 128) * 128
    while t >= 128:
        if dim % t == 0:
            return t
        t -= 128
    raise ValueError(f"no tile for {dim}")


def _rms(x, g):
    return x * lax.rsqrt(jnp.mean(x * x, axis=-1, keepdims=True) + EPS) * g


def _mm(a, b, mode, name, *, out_dtypes=(F32,), a_pre=(), b_pre=(), into=None, out_pre=(),
        extras=(), epi=None, deps=(), tm=2048, tn=1024, tk=1024):
    a2, b2 = a.shape[len(a_pre):], b.shape[len(b_pre):]
    if mode == "nn":
        (M, K), (K2, N) = a2, b2
    elif mode == "nt":
        (M, K), (N, K2) = a2, b2
    else:
        (K, M), (K2, N) = a2, b2
    assert K == K2, (a.shape, b.shape, mode)
    tm, tn, tk = _tile(M, tm), _tile(N, tn), _tile(K, tk)
    nk = K // tk
    na, nb_, no = len(a_pre), len(b_pre), len(out_pre)
    if mode == "tn":
        a_spec = pl.BlockSpec((None,) * na + (tk, tm), lambda i, j, k: a_pre + (k, i))
    else:
        a_spec = pl.BlockSpec((None,) * na + (tm, tk), lambda i, j, k: a_pre + (i, k))
    if mode == "nt":
        b_spec = pl.BlockSpec((None,) * nb_ + (tn, tk), lambda i, j, k: b_pre + (j, k))
    else:
        b_spec = pl.BlockSpec((None,) * nb_ + (tk, tn), lambda i, j, k: b_pre + (k, j))
    tile_spec = pl.BlockSpec((tm, tn), lambda i, j, k: (i, j))
    ne, nout = len(extras), len(out_dtypes)
    in_specs = [a_spec, b_spec] + [tile_spec] * ne
    operands = [a, b, *extras]
    aliases = {}
    if into is not None:
        assert nout == 1
        in_specs.append(pl.BlockSpec(memory_space=pl.ANY))
        operands.append(into)
        aliases = {len(operands) - 1: 0}
        out_shape = [jax.ShapeDtypeStruct(into.shape, into.dtype)]
        out_specs = [pl.BlockSpec((None,) * no + (tm, tn), lambda i, j, k: out_pre + (i, j))]
    else:
        out_shape = [jax.ShapeDtypeStruct((M, N), dt) for dt in out_dtypes]
        out_specs = [tile_spec] * nout
    in_specs += [pl.BlockSpec(memory_space=pl.ANY)] * len(deps)
    operands += list(deps)

    def body(*refs):
        a_ref, b_ref = refs[0], refs[1]
        ex = refs[2:2 + ne]
        pos = 2 + ne + (1 if into is not None else 0) + len(deps)
        outs = refs[pos:pos + nout]
        acc_ref = refs[pos + nout] if nk > 1 else None

        def finish(acc):
            vals = epi(acc, *[e[...] for e in ex]) if epi is not None else (acc,)
            for o, v in zip(outs, vals):
                o[...] = v.astype(o.dtype)

        def prod():
            return _dot(a_ref[...].astype(BF16), b_ref[...].astype(BF16), mode)

        if nk == 1:
            finish(prod())
        else:
            k = pl.program_id(2)

            @pl.when(k == 0)
            def _():
                acc_ref[...] = jnp.zeros_like(acc_ref)

            acc_ref[...] += prod()

            @pl.when(k == nk - 1)
            def _():
                finish(acc_ref[...])

    res = pl.pallas_call(
        body, name=name, grid=(M // tm, N // tn, nk),
        in_specs=in_specs, out_specs=out_specs, out_shape=out_shape,
        scratch_shapes=[pltpu.VMEM((tm, tn), F32)] if nk > 1 else [],
        input_output_aliases=aliases,
        compiler_params=_params(("parallel", "parallel", "arbitrary")),
    )(*operands)
    return res[0] if nout == 1 else tuple(res)


def _resnorm_fn(has_post, has_pre):
    def f(*a):
        x, k = a[0], 1
        if has_post:
            x, k = x + _rms(a[1], a[2]), 3
        outs = [x]
        if has_pre:
            outs.append(_rms(x, a[k]))
        return tuple(outs)
    return f


def _row_spec(T, W):
    return pl.BlockSpec((T, W), lambda i: (i, 0))


def _par_spec(W):
    return pl.BlockSpec((1, W), lambda i: (0, 0))


def _resnorm_fwd(xr, y, gp, gq, name, T=512, deps=()):
    S, D = xr.shape
    T = min(T, S)
    has_post, has_pre = y is not None, gq is not None
    f = _resnorm_fn(has_post, has_pre)
    ins = [xr] + ([y, gp] if has_post else []) + ([gq] if has_pre else [])
    in_specs = [_row_spec(T, D)] + ([_row_spec(T, D), _par_spec(D)] if has_post else []) + ([_par_spec(D)] if has_pre else [])
    out_shape, out_specs = [], []
    if has_post:
        out_shape.append(jax.ShapeDtypeStruct((S, D), F32)); out_specs.append(_row_spec(T, D))
    if has_pre:
        out_shape.append(jax.ShapeDtypeStruct((S, D), BF16)); out_specs.append(_row_spec(T, D))
    n_in, n_dep = len(ins), len(deps)

    def body(*refs):
        vals = f(*[r[...] for r in refs[:n_in]])
        outs = list(refs[n_in + n_dep:])
        if has_post:
            outs.pop(0)[...] = vals[0]
        if has_pre:
            outs.pop(0)[...] = vals[1].astype(BF16)

    res = pl.pallas_call(body, name=name, grid=(S // T,),
                         in_specs=in_specs + [pl.BlockSpec(memory_space=pl.ANY)] * n_dep, out_specs=out_specs,
                         out_shape=out_shape, compiler_params=_params(("parallel",)))(*ins, *deps)
    return tuple(res)


def _resnorm_bwd(xr, y, gp, gq, dres, dh, name, T=512, deps=()):
    S, D = xr.shape
    T = min(T, S)
    has_post, has_pre, has_res = y is not None, gq is not None, dres is not None
    f = _resnorm_fn(has_post, has_pre)
    ins = [xr] + ([y, gp] if has_post else []) + ([gq] if has_pre else [])
    in_specs = [_row_spec(T, D)] + ([_row_spec(T, D), _par_spec(D)] if has_post else []) + ([_par_spec(D)] if has_pre else [])
    n_prim = len(ins)
    if has_res:
        ins.append(dres); in_specs.append(_row_spec(T, D))
    if has_pre:
        ins.append(dh); in_specs.append(_row_spec(T, D))
    n_in, n_dep = len(ins), len(deps)
    out_shape = [jax.ShapeDtypeStruct((S, D), F32)]
    out_specs = [_row_spec(T, D)]
    if has_post:
        out_shape += [jax.ShapeDtypeStruct((S, D), BF16), jax.ShapeDtypeStruct((1, D), F32)]
        out_specs += [_row_spec(T, D), _par_spec(D)]
    if has_pre:
        out_shape.append(jax.ShapeDtypeStruct((1, D), F32)); out_specs.append(_par_spec(D))

    def body(*refs):
        i = pl.program_id(0)
        prim = [r[...] for r in refs[:n_prim]]
        rest = list(refs[n_prim:n_in])
        ct_x = rest.pop(0)[...] if has_res else jnp.zeros((T, D), F32)
        cts = [ct_x]
        if has_pre:
            cts.append(rest.pop(0)[...].astype(F32))
        _, vjp = jax.vjp(f, *prim)
        grads = list(vjp(tuple(cts)))
        outs = list(refs[n_in + n_dep:])
        outs.pop(0)[...] = grads.pop(0)
        acc = []
        if has_post:
            outs.pop(0)[...] = grads.pop(0).astype(BF16)
            acc.append((outs.pop(0), grads.pop(0)))
        if has_pre:
            acc.append((outs.pop(0), grads.pop(0)))

        @pl.when(i == 0)
        def _():
            for o, _g in acc:
                o[...] = jnp.zeros_like(o)

        for o, g in acc:
            o[...] += g

    res = pl.pallas_call(body, name=name, grid=(S // T,),
                         in_specs=in_specs + [pl.BlockSpec(memory_space=pl.ANY)] * n_dep, out_specs=out_specs,
                         out_shape=out_shape, compiler_params=_params(("arbitrary",)))(*ins, *deps)
    return tuple(res)


def _final_fwd(xr, y, gp, target, name, T=512):
    S, D = xr.shape
    T = min(T, S)

    def body(x_ref, y_ref, g_ref, t_ref, dy_ref, loss_ref):
        i = pl.program_id(0)
        e = x_ref[...] + _rms(y_ref[...], g_ref[...]) - t_ref[...]
        dy_ref[...] = e / D

        @pl.when(i == 0)
        def _():
            loss_ref[...] = jnp.zeros_like(loss_ref)

        loss_ref[...] += 0.5 * jnp.sum(jnp.sum(e * e, axis=-1, keepdims=True) / D, axis=0, keepdims=True)

    return pl.pallas_call(
        body, name=name, grid=(S // T,),
        in_specs=[_row_spec(T, D), _row_spec(T, D), _par_spec(D), _row_spec(T, D)],
        out_specs=[_row_spec(T, D), pl.BlockSpec((1, 128), lambda i: (0, 0))],
        out_shape=[jax.ShapeDtypeStruct((S, D), F32), jax.ShapeDtypeStruct((1, 128), F32)],
        compiler_params=_params(("arbitrary",)))(xr, y, gp, target)


_STRAIGHT_HEADS = (0, 2, 5, 7)
_ROLLED_HEADS = (1, 3, 4, 6)


def _straight_lanes():
    r = lax.broadcasted_iota(jnp.int32, (4 * BLK, BLK), 0)
    c = lax.broadcasted_iota(jnp.int32, (4 * BLK, BLK), 1)
    return (r < 2 * BLK) == (c < 64)


def _att_mask(not_first):
    k = lax.broadcasted_iota(jnp.int32, (2 * BLK, 4 * BLK), 0)
    q = lax.broadcasted_iota(jnp.int32, (2 * BLK, 4 * BLK), 1) % BLK
    qc, kc = 2 + q // 64, k // 64
    return (kc <= qc) & (kc >= qc - 2) & (not_first | (k >= BLK))


def _sink_row(sk_ref, heads):
    return jnp.concatenate([sk_ref[h:h + 1, :] for h in heads], axis=1)


def _softmax_sink(s, sk):
    m = jnp.maximum(jnp.max(s, axis=0, keepdims=True), sk)
    e = jnp.exp(s - m)
    es = jnp.exp(sk - m)
    z = jnp.sum(e, axis=0, keepdims=True) + es
    return e / z, es / z


def _att_bands(cur, kvp):
    kband = jnp.concatenate([kvp[:, 0:BLK], cur[:, C_K:C_K + BLK]], axis=0)
    vband = jnp.concatenate([kvp[:, BLK:2 * BLK], cur[:, C_V:C_V + BLK]], axis=0)
    return kband, pltpu.roll(kband, 64, 1), vband, pltpu.roll(vband, 64, 1)


def _stack_tiles(ref_or_val, start):
    return jnp.concatenate([ref_or_val[:, start + BLK * t:start + BLK * (t + 1)] for t in range(4)], axis=0)


def _sgu_mask():
    r = lax.broadcasted_iota(jnp.int32, (BLK, BLK), 0)
    c = lax.broadcasted_iota(jnp.int32, (BLK, BLK), 1)
    return (c // 64) <= (r // 64)


def _pool_cnt(blk, w):
    t = blk * BLK + lax.broadcasted_iota(jnp.int32, (BLK, 1), 0)
    return jnp.minimum(t + 1, w).astype(F32)


def _mix_in_specs(nb, rev):
    def b(i):
        return nb - 1 - i if rev else i
    return [
        pl.BlockSpec((BLK, C_GATE), lambda i: (b(i), 0)),
        pl.BlockSpec((BLK, 2 * BLK), lambda i: (jnp.maximum(b(i) - 1, 0), C_K // (2 * BLK))),
        pl.BlockSpec((HALO, C_GATE), lambda i: (jnp.maximum(b(i) * (BLK // HALO) - 1, 0), 0)),
        pl.BlockSpec((8, BLK), lambda i: (0, 0)),
        pl.BlockSpec((4, BLK, BLK), lambda i: (0, 0, 0)),
        pl.BlockSpec((4, BLK, 1), lambda i: (0, 0, 0)),
        pl.BlockSpec((1, 512), lambda i: (0, 0)),
        pl.BlockSpec((4, BLK, BLK), lambda i: (0, 0, 0)),
        pl.BlockSpec((1, 512), lambda i: (0, 0)),
    ]


def _mix_fwd(proj, sinks_b, ws, bs3, gsgu, wp, ps, name):
    S = proj.shape[0]
    nb = S // BLK

    def body(cur_ref, kvp_ref, pcp_ref, sk_ref, ws_ref, bs_ref, gs_ref, wp_ref, ps_ref, br_ref, ext_ref):
        i = pl.program_id(0)
        not_first = i > 0
        cur, kvp = cur_ref[...], kvp_ref[...]
        mask = _att_mask(not_first)
        own = _straight_lanes()
        q = _stack_tiles(cur, C_Q)
        outs = []
        kband, kroll, vband, vroll = _att_bands(cur, kvp)
        for qg, kg, vg, heads in ((jnp.where(own, q, 0.0), kband, vband, _STRAIGHT_HEADS),
                                  (jnp.where(own, 0.0, q), kroll, vroll, _ROLLED_HEADS)):
            s = jnp.where(mask, _dot(kg.astype(BF16), qg.astype(BF16), "nt") * ATT_SCALE, NEG_INF)
            p, _ = _softmax_sink(s, _sink_row(sk_ref, heads))
            outs.append(_dot(p.astype(BF16), vg.astype(BF16), "tn"))
        o = jnp.where(own, outs[0], outs[1])
        for t in range(4):
            br_ref[0, :, BLK * t:BLK * (t + 1)] = o[BLK * t:BLK * (t + 1)].astype(BF16)
        gu = jax.nn.gelu(cur[:, C_SU:C_SV])
        vn = _rms(jax.nn.gelu(cur[:, C_SV:C_PC]), gs_ref[...]).astype(BF16)
        wmask = _sgu_mask()
        for g in range(4):
            wm = jnp.where(wmask, ws_ref[g], 0.0).astype(BF16)
            sp = _dot(wm, vn[:, BLK * g:BLK * (g + 1)], "nn") + bs_ref[g]
            br_ref[1, :, BLK * g:BLK * (g + 1)] = (gu[:, BLK * g:BLK * (g + 1)] * sp).astype(BF16)
        c = cur[:, C_PC:C_GATE]
        ext_ref[0:HALO, :] = jnp.where(not_first, pcp_ref[:, C_PC:C_GATE], 0.0)
        ext_ref[HALO:HALO + BLK, :] = c
        for g, w in enumerate(POOL_WINDOWS):
            sl = slice(BLK * g, BLK * (g + 1))
            acc = ext_ref[HALO:HALO + BLK, sl]
            for k in range(1, w):
                acc = acc + ext_ref[HALO - k:HALO - k + BLK, sl]
            pooled = acc / _pool_cnt(i, w) - c[:, sl]
            mixed = _dot(pooled.astype(BF16), wp_ref[g].astype(BF16), "nn")
            br_ref[2, :, sl] = (mixed * ps_ref[:, sl]).astype(BF16)

    return pl.pallas_call(
        body, name=name, grid=(nb,),
        in_specs=_mix_in_specs(nb, False),
        out_specs=pl.BlockSpec((3, BLK, 512), lambda i: (0, i, 0)),
        out_shape=jax.ShapeDtypeStruct((3, S, 512), BF16),
        scratch_shapes=[pltpu.VMEM((HALO + BLK, 512), F32)],
        compiler_params=_params(("parallel",)),
    )(proj, proj, proj, sinks_b, ws, bs3, gsgu, wp, ps)


def _mix_bwd(proj, dbr, dproj, sinks_b, ws, bs3, gsgu, wp, ps, name):
    S = proj.shape[0]
    nb = S // BLK

    def body(cur_ref, kvp_ref, pcp_ref, sk_ref, ws_ref, bs_ref, gs_ref, wp_ref, ps_ref, dbr_ref, _dproj_in,
             dp_ref, dsk_ref, dws_ref, dbs_ref, dgs_ref, dwp_ref, dps_ref,
             ext_ref, z_ref, ckv_ref, cpc_ref):
        i = pl.program_id(0)
        blk = nb - 1 - i
        not_first = blk > 0

        @pl.when(i == 0)
        def _():
            for r in (dsk_ref, dws_ref, dbs_ref, dgs_ref, dwp_ref, dps_ref, ckv_ref, cpc_ref, z_ref):
                r[...] = jnp.zeros_like(r)

        cur, kvp = cur_ref[...], kvp_ref[...]
        mask = _att_mask(not_first)
        own = _straight_lanes()
        q = _stack_tiles(cur, C_Q)
        do = jnp.concatenate([dbr_ref[0, :, BLK * t:BLK * (t + 1)] for t in range(4)], axis=0)
        kband, kroll, vband, vroll = _att_bands(cur, kvp)
        dqs, dks, dvs = [], [], []
        for qg, dog, kg, vg, heads in (
                (jnp.where(own, q, 0.0), jnp.where(own, do, 0.0), kband, vband, _STRAIGHT_HEADS),
                (jnp.where(own, 0.0, q), jnp.where(own, 0.0, do), kroll, vroll, _ROLLED_HEADS)):
            qg, dog, kg, vg = qg.astype(BF16), dog.astype(BF16), kg.astype(BF16), vg.astype(BF16)
            s = jnp.where(mask, _dot(kg, qg, "nt") * ATT_SCALE, NEG_INF)
            p, p_sink = _softmax_sink(s, _sink_row(sk_ref, heads))
            dp = _dot(vg, dog, "nt")
            rs = jnp.sum(p * dp, axis=0, keepdims=True)
            ds = (p * (dp - rs) * ATT_SCALE).astype(BF16)
            sink_row = p_sink * rs
            for t, h in enumerate(heads):
                dsk_ref[h:h + 1, :] += jnp.broadcast_to(
                    -jnp.sum(sink_row[:, BLK * t:BLK * (t + 1)], axis=1, keepdims=True), (1, BLK))
            dvs.append(_dot(p.astype(BF16), dog, "nn"))
            dks.append(_dot(ds, qg, "nn"))
            dqs.append(_dot(ds, kg, "tn"))
        dq = jnp.where(own, dqs[0], dqs[1])
        for t in range(4):
            dp_ref[:, C_Q + BLK * t:C_Q + BLK * (t + 1)] = dq[BLK * t:BLK * (t + 1)].astype(BF16)
        dk = dks[0] + pltpu.roll(dks[1], 64, 1)
        dv = dvs[0] + pltpu.roll(dvs[1], 64, 1)
        dp_ref[:, C_K:C_K + BLK] = (dk[BLK:] + ckv_ref[:, 0:BLK]).astype(BF16)
        dp_ref[:, C_V:C_V + BLK] = (dv[BLK:] + ckv_ref[:, BLK:]).astype(BF16)
        ckv_ref[:, 0:BLK] = dk[:BLK]
        ckv_ref[:, BLK:] = dv[:BLK]
        su, sv = cur[:, C_SU:C_SV], cur[:, C_SV:C_PC]
        gu, vjp_u = jax.vjp(jax.nn.gelu, su)
        vn, vjp_v = jax.vjp(lambda a, g: _rms(jax.nn.gelu(a), g), sv, gs_ref[...])
        vn16 = vn.astype(BF16)
        wmask = _sgu_mask()
        dgu, dvn = [], []
        for g in range(4):
            sl = slice(BLK * g, BLK * (g + 1))
            wm = jnp.where(wmask, ws_ref[g], 0.0).astype(BF16)
            sp = _dot(wm, vn16[:, sl], "nn") + bs_ref[g]
            dyb = dbr_ref[1, :, sl]
            dgu.append(dyb * sp)
            dsp = dyb * gu[:, sl]
            dsp16 = dsp.astype(BF16)
            dvn.append(_dot(wm, dsp16, "tn"))
            dws_ref[g] += jnp.where(wmask, _dot(dsp16, vn16[:, sl], "nt"), 0.0)
            dbs_ref[g] += jnp.sum(dsp, axis=1, keepdims=True)
        (dsu,) = vjp_u(jnp.concatenate(dgu, axis=1))
        dsv, dgs = vjp_v(jnp.concatenate(dvn, axis=1))
        dp_ref[:, C_SU:C_SV] = dsu.astype(BF16)
        dp_ref[:, C_SV:C_PC] = dsv.astype(BF16)
        dgs_ref[...] += dgs
        c = cur[:, C_PC:C_GATE]
        ext_ref[0:HALO, :] = jnp.where(not_first, pcp_ref[:, C_PC:C_GATE], 0.0)
        ext_ref[HALO:HALO + BLK, :] = c
        for g, w in enumerate(POOL_WINDOWS):
            sl = slice(BLK * g, BLK * (g + 1))
            acc = ext_ref[HALO:HALO + BLK, sl]
            for k in range(1, w):
                acc = acc + ext_ref[HALO - k:HALO - k + BLK, sl]
            cnt = _pool_cnt(blk, w)
            pooled16 = (acc / cnt - c[:, sl]).astype(BF16)
            wp16 = wp_ref[g].astype(BF16)
            mixed = _dot(pooled16, wp16, "nn")
            dyc = dbr_ref[2, :, sl]
            dps_ref[:, sl] += jnp.sum(dyc * mixed, axis=0, keepdims=True)
            dmixed16 = (dyc * ps_ref[:, sl]).astype(BF16)
            dwp_ref[g] += _dot(pooled16, dmixed16, "tn")
            dpooled = _dot(dmixed16, wp16, "nt")
            z_ref[HALO:HALO + BLK, sl] = dpooled / cnt
            dext = z_ref[0:HALO + BLK, sl]
            for k in range(1, w):
                dext = dext + z_ref[k:k + HALO + BLK, sl]
            dp_ref[:, C_PC + BLK * g:C_PC + BLK * (g + 1)] = (
                dext[HALO:] - dpooled + jnp.concatenate([jnp.zeros((BLK - HALO, BLK), F32), cpc_ref[:, sl]], axis=0)
            ).astype(BF16)
            cpc_ref[:, sl] = dext[:HALO]

    n_in = 11
    small = [jax.ShapeDtypeStruct((8, BLK), F32), jax.ShapeDtypeStruct((4, BLK, BLK), F32),
             jax.ShapeDtypeStruct((4, BLK, 1), F32), jax.ShapeDtypeStruct((1, 512), F32),
             jax.ShapeDtypeStruct((4, BLK, BLK), F32), jax.ShapeDtypeStruct((1, 512), F32)]
    small_specs = [pl.BlockSpec((8, BLK), lambda i: (0, 0)), pl.BlockSpec((4, BLK, BLK), lambda i: (0, 0, 0)),
                   pl.BlockSpec((4, BLK, 1), lambda i: (0, 0, 0)), pl.BlockSpec((1, 512), lambda i: (0, 0)),
                   pl.BlockSpec((4, BLK, BLK), lambda i: (0, 0, 0)), pl.BlockSpec((1, 512), lambda i: (0, 0))]
    res = pl.pallas_call(
        body, name=name, grid=(nb,),
        in_specs=_mix_in_specs(nb, True) + [
            pl.BlockSpec((3, BLK, 512), lambda i: (0, nb - 1 - i, 0)),
            pl.BlockSpec(memory_space=pl.ANY)],
        out_specs=[pl.BlockSpec((BLK, C_GATE), lambda i: (nb - 1 - i, 0))] + small_specs,
        out_shape=[jax.ShapeDtypeStruct(dproj.shape, dproj.dtype)] + small,
        scratch_shapes=[pltpu.VMEM((HALO + BLK, 512), F32), pltpu.VMEM((2 * HALO + BLK, 512), F32),
                        pltpu.VMEM((BLK, 2 * BLK), F32), pltpu.VMEM((HALO, 512), F32)],
        input_output_aliases={n_in - 1: 0},
        compiler_params=_params(("arbitrary",)),
    )(proj, proj, proj, sinks_b, ws, bs3, gsgu, wp, ps, dbr, dproj)
    return tuple(res)


_GW = 256


def _merge_fwd(proj, pb, name, T=4096):
    S, D = pb.shape[1], pb.shape[2]
    T = min(T, S)

    def body(gate_ref, pb_ref, out_ref, acc_ref):
        n = pl.program_id(2)

        @pl.when(n == 0)
        def _():
            acc_ref[...] = jnp.zeros_like(acc_ref)

        acc_ref[...] += jax.nn.sigmoid(gate_ref[...]) * pb_ref[...]

        @pl.when(n == 2)
        def _():
            out_ref[...] = acc_ref[...].astype(BF16)

    return pl.pallas_call(
        body, name=name, grid=(S // T, D // _GW, 3),
        in_specs=[pl.BlockSpec((T, _GW), lambda i, j, n: (i, C_GATE // _GW + n * (D // _GW) + j)),
                  pl.BlockSpec((None, T, _GW), lambda i, j, n: (n, i, j))],
        out_specs=pl.BlockSpec((T, _GW), lambda i, j, n: (i, j)),
        out_shape=jax.ShapeDtypeStruct((S, D), BF16),
        scratch_shapes=[pltpu.VMEM((T, _GW), F32)],
        compiler_params=_params(("parallel", "parallel", "arbitrary")),
    )(proj, pb)


def _merge_bwd(proj, pb, dmerged, name, T=4096):
    S, D = pb.shape[1], pb.shape[2]
    T = min(T, S)

    def body(gate_ref, pb_ref, dm_ref, dgate_ref, dpb_ref):
        sg = jax.nn.sigmoid(gate_ref[...])
        dm = dm_ref[...]
        dpb_ref[...] = (dm * sg).astype(BF16)
        dgate_ref[...] = (dm * pb_ref[...] * sg * (1.0 - sg)).astype(BF16)

    gate_map = lambda i, n, j: (i, C_GATE // _GW + n * (D // _GW) + j)
    return pl.pallas_call(
        body, name=name, grid=(S // T, 3, D // _GW),
        in_specs=[pl.BlockSpec((T, _GW), gate_map),
                  pl.BlockSpec((None, T, _GW), lambda i, n, j: (n, i, j)),
                  pl.BlockSpec((T, _GW), lambda i, n, j: (i, j))],
        out_specs=[pl.BlockSpec((T, _GW), gate_map),
                   pl.BlockSpec((None, T, _GW), lambda i, n, j: (n, i, j))],
        out_shape=[jax.ShapeDtypeStruct((S, C_END), BF16), jax.ShapeDtypeStruct((3, S, D), BF16)],
        compiler_params=_params(("parallel", "parallel", "parallel")),
    )(proj, pb, dmerged)


def _memattn_fwd(qm, kv, name, T=512):
    S, NM = qm.shape[0], kv.shape[0]
    T = min(T, S)

    def body(q_ref, kv_ref, o_ref):
        for h in range(4):
            sl = slice(128 * h, 128 * (h + 1))
            k = kv_ref[:, sl].astype(BF16)
            v = kv_ref[:, 512 + 128 * h:512 + 128 * (h + 1)].astype(BF16)
            s = _dot(k, q_ref[:, sl].astype(BF16), "nt") * MEM_SCALE
            p = jax.nn.softmax(s, axis=0)
            o_ref[:, sl] = _dot(p.astype(BF16), v, "tn").astype(BF16)

    return pl.pallas_call(
        body, name=name, grid=(S // T,),
        in_specs=[_row_spec(T, 512), pl.BlockSpec((NM, 1024), lambda i: (0, 0))],
        out_specs=_row_spec(T, 512), out_shape=jax.ShapeDtypeStruct((S, 512), BF16),
        compiler_params=_params(("parallel",)))(qm, kv)


def _memattn_bwd(qm, kv, dom, name, T=512):
    S, NM = qm.shape[0], kv.shape[0]
    T = min(T, S)

    def body(q_ref, kv_ref, do_ref, dq_ref, dkv_ref):
        i = pl.program_id(0)

        @pl.when(i == 0)
        def _():
            dkv_ref[...] = jnp.zeros_like(dkv_ref)

        for h in range(4):
            sl = slice(128 * h, 128 * (h + 1))
            sv_ = slice(512 + 128 * h, 512 + 128 * (h + 1))
            q = q_ref[:, sl].astype(BF16)
            k = kv_ref[:, sl].astype(BF16)
            v = kv_ref[:, sv_].astype(BF16)
            do = do_ref[:, sl].astype(BF16)
            p = jax.nn.softmax(_dot(k, q, "nt") * MEM_SCALE, axis=0)
            dp = _dot(v, do, "nt")
            ds = (p * (dp - jnp.sum(p * dp, axis=0, keepdims=True)) * MEM_SCALE).astype(BF16)
            dq_ref[:, sl] = _dot(ds, k, "tn").astype(BF16)
            dkv_ref[:, sl] += _dot(ds, q, "nn")
            dkv_ref[:, sv_] += _dot(p.astype(BF16), do, "nn")

    return pl.pallas_call(
        body, name=name, grid=(S // T,),
        in_specs=[_row_spec(T, 512), pl.BlockSpec((NM, 1024), lambda i: (0, 0)), _row_spec(T, 512)],
        out_specs=[_row_spec(T, 512), pl.BlockSpec((NM, 1024), lambda i: (0, 0))],
        out_shape=[jax.ShapeDtypeStruct((S, 512), BF16), jax.ShapeDtypeStruct((NM, 1024), F32)],
        compiler_params=_params(("arbitrary",)))(qm, kv, dom)


def _adamw(w, g, m, v, name, TR=512):
    R, C = w.shape
    TR = R if R <= TR else _row_tile(R, TR)
    c1 = 1.0 - ADAM_B1 ** ADAM_STEP
    c2 = 1.0 - ADAM_B2 ** ADAM_STEP

    def body(w_ref, g_ref, m_ref, v_ref, d_ref, nm_ref, nv_ref):
        gv = g_ref[...]
        nm = ADAM_B1 * m_ref[...] + (1.0 - ADAM_B1) * gv
        nv = ADAM_B2 * v_ref[...] + (1.0 - ADAM_B2) * jnp.square(gv)
        d_ref[...] = -ADAM_LR * ((nm / c1) / (jnp.sqrt(nv / c2) + ADAM_EPS) + ADAM_WD * w_ref[...])
        nm_ref[...] = nm
        nv_ref[...] = nv

    spec = pl.BlockSpec((TR, C), lambda i: (i, 0))
    return pl.pallas_call(
        body, name=name, grid=(R // TR,), in_specs=[spec] * 4, out_specs=[spec] * 3,
        out_shape=[jax.ShapeDtypeStruct((R, C), F32)] * 3,
        compiler_params=_params(("parallel",)))(w, g, m, v)


def _row_tile(R, pref):
    t = (pref // 8) * 8
    while t >= 8:
        if R % t == 0:
            return t
        t -= 8
    raise ValueError(f"no row tile for {R}")


def _sum_slots(stack, name, TR=512):
    n, R, C = stack.shape
    TR = R if R <= TR else _row_tile(R, TR)

    def body(s_ref, o_ref):
        acc = s_ref[0]
        for k in range(1, n):
            acc = acc + s_ref[k]
        o_ref[...] = acc

    return pl.pallas_call(
        body, name=name, grid=(R // TR,),
        in_specs=[pl.BlockSpec((n, TR, C), lambda i: (0, i, 0))],
        out_specs=pl.BlockSpec((TR, C), lambda i: (i, 0)),
        out_shape=jax.ShapeDtypeStruct((R, C), F32),
        compiler_params=_params(("parallel",)))(stack)


_ANY = pl.BlockSpec(memory_space=pl.ANY)


def _chip_of(j, c):
    return (j // 2, j % 2, c)


def _own_slab(shard, dtype, j_arr, name, first=0, count=None, plus=None, deps=(), TR=512):
    N, r, C = shard.shape
    B = N if count is None else count
    rh = r // 2
    TR = rh if rh <= TR else _row_tile(rh, TR)
    nt = rh // TR
    ins = [shard] if plus is None else [shard, plus]

    def body(j_ref, *refs):
        val = refs[0][...] if plus is None else refs[0][...] + refs[1][...]
        refs[-1][...] = val.astype(refs[-1].dtype)

    return pl.pallas_call(
        body, name=name,
        grid_spec=pltpu.PrefetchScalarGridSpec(
            num_scalar_prefetch=1, grid=(B, 2, nt),
            in_specs=[pl.BlockSpec((None, TR, C), lambda b, h, t, jr: (first + b, h * nt + t, 0))] * len(ins)
            + [_ANY] * len(deps),
            out_specs=pl.BlockSpec((None, None, None, TR, C), lambda b, h, t, jr: (b, jr[0], h, t, 0))),
        out_shape=jax.ShapeDtypeStruct((B, 4, 2, rh, C), dtype),
        compiler_params=_params(("parallel", "parallel", "parallel")),
    )(j_arr, *ins, *deps)


def _gather_weights(bufs, name):
    n = len(bufs)

    def body(*refs):
        buf = refs[n:2 * n]
        send_sems, recv_sems, fsend_sems, frecv_sems = refs[2 * n:]
        x, y, c = lax.axis_index("x"), lax.axis_index("y"), lax.axis_index("c")
        j = 2 * x + y
        sib = (x, y, 1 - c)
        sends = []
        for d in range(1, 4):
            for a in range(n):
                cp = pltpu.make_async_remote_copy(
                    src_ref=buf[a].at[:, j, c], dst_ref=buf[a].at[:, j, c], send_sem=send_sems.at[a, d - 1],
                    recv_sem=recv_sems.at[a, d - 1], device_id=_chip_of((j + d) % 4, c), device_id_type=MESH)
                cp.start()
                sends.append(cp)
        for d in range(1, 4):
            frm = (j + 4 - d) % 4
            for a in range(n):
                pltpu.make_async_remote_copy(
                    src_ref=buf[a].at[:, frm, c], dst_ref=buf[a].at[:, frm, c], send_sem=send_sems.at[a, d - 1],
                    recv_sem=recv_sems.at[a, d - 1], device_id=_chip_of(frm, c), device_id_type=MESH).wait_recv()
                cp = pltpu.make_async_remote_copy(
                    src_ref=buf[a].at[:, frm, c], dst_ref=buf[a].at[:, frm, c], send_sem=fsend_sems.at[a, d - 1],
                    recv_sem=frecv_sems.at[a, d - 1], device_id=sib, device_id_type=MESH)
                cp.start()
                sends.append(cp)
        for d in range(1, 4):
            frm = (j + 4 - d) % 4
            for a in range(n):
                pltpu.make_async_remote_copy(
                    src_ref=buf[a].at[:, frm, 1 - c], dst_ref=buf[a].at[:, frm, 1 - c], send_sem=fsend_sems.at[a, d - 1],
                    recv_sem=frecv_sems.at[a, d - 1], device_id=sib, device_id_type=MESH).wait_recv()
        for cp in sends:
            cp.wait_send()

    return pl.pallas_call(
        body, name=name,
        in_specs=[_ANY] * n, out_specs=[_ANY] * n,
        out_shape=[jax.ShapeDtypeStruct(b.shape, b.dtype) for b in bufs],
        scratch_shapes=[pltpu.SemaphoreType.DMA((n, 3))] * 4,
        input_output_aliases={a: a for a in range(n)},
    )(*bufs)


_HBM = pl.BlockSpec(memory_space=pltpu.HBM)
_SEM = pl.BlockSpec(memory_space=pltpu.SEMAPHORE)
_DATAFLOW = pltpu.SideEffectType.DATAFLOW_SIDE_EFFECTING


def _in_hbm(arrays):
    return [pltpu.with_memory_space_constraint(a, pltpu.HBM) for a in arrays]


def _start_copies(bufs, plan, count, name, deps=()):
    n, k = len(bufs), len(deps)

    def body(*refs):
        send_sems, recv_sems = refs[n + k], refs[n + k + 1]
        for i, (src, dst, dev) in enumerate(plan(refs[:n], False)):
            pltpu.make_async_remote_copy(src_ref=src, dst_ref=dst, send_sem=send_sems.at[i], recv_sem=recv_sems.at[i],
                                         device_id=dev, device_id_type=MESH).start()
        refs[-1][...] = jnp.zeros_like(refs[-1])

    return pl.pallas_call(
        body, name=name,
        out_shape=(pltpu.SemaphoreType.DMA((count,)), pltpu.SemaphoreType.DMA((count,)),
                   *[pltpu.HBM(b.shape, b.dtype) for b in bufs], jax.ShapeDtypeStruct((8, 128), F32)),
        in_specs=[_HBM] * n + [_ANY] * k,
        out_specs=(_SEM, _SEM, *[_HBM] * n, pl.BlockSpec(memory_space=pltpu.VMEM)),
        input_output_aliases={a: 2 + a for a in range(n)},
        compiler_params=pltpu.CompilerParams(has_side_effects=_DATAFLOW),
    )(*_in_hbm(bufs), *deps)


def _wait_copies(handle, plan, afters, name):
    send_sems, recv_sems, *bufs = handle[:-1]
    n = len(bufs)

    def body(*refs):
        send_sems, recv_sems = refs[n], refs[n + 1]
        for i, (src, dst, dev) in enumerate(plan(refs[:n], True)):
            cp = pltpu.make_async_remote_copy(src_ref=src, dst_ref=dst, send_sem=send_sems.at[i], recv_sem=recv_sems.at[i],
                                              device_id=dev, device_id_type=MESH)
            cp.wait_send()
            cp.wait_recv()

    return list(pl.pallas_call(
        body, name=name,
        out_shape=[pltpu.HBM(b.shape, b.dtype) for b in bufs],
        in_specs=[_HBM] * n + [_SEM, _SEM] + [_ANY] * len(afters), out_specs=[_HBM] * n,
        input_output_aliases={a: a for a in range(n)},
        compiler_params=pltpu.CompilerParams(has_side_effects=_DATAFLOW),
    )(*bufs, send_sems, recv_sems, *afters))


def _gather_plan(buf, waiting):
    c = lax.axis_index("c")
    j = 2 * lax.axis_index("x") + lax.axis_index("y")
    copies = []
    for d in range(1, 4):
        to, frm = (j + d) % 4, (j + 4 - d) % 4
        for b in buf:
            copies.append((b.at[:, j, c], b.at[:, frm if waiting else j, c], _chip_of(frm if waiting else to, c)))
    return copies


def _chip_plan(buf, waiting):
    n = len(buf) // 2
    c = lax.axis_index("c")
    j = 2 * lax.axis_index("x") + lax.axis_index("y")
    copies = []
    for d in range(1, 4):
        to = (j + d) % 4
        for a in range(n):
            copies.append((buf[a].at[to], buf[n + a].at[d - 1], _chip_of(to, c)))
    return copies


def _pair_plan(buf, waiting):
    n = len(buf) // 2
    c = lax.axis_index("c")
    sib = (lax.axis_index("x"), lax.axis_index("y"), 1 - c)
    return [(buf[a].at[:, pl.ds(1 - c, 1)], buf[n + a], sib) for a in range(n)]


def _gather_forward(bufs, name):
    n = len(bufs)

    def body(*refs):
        buf = refs[n:2 * n]
        send_sems, recv_sems = refs[2 * n:]
        x, y, c = lax.axis_index("x"), lax.axis_index("y"), lax.axis_index("c")
        j = 2 * x + y
        sib = (x, y, 1 - c)
        sends = []
        for d in range(1, 4):
            frm = (j + 4 - d) % 4
            for a in range(n):
                cp = pltpu.make_async_remote_copy(
                    src_ref=buf[a].at[:, frm, c], dst_ref=buf[a].at[:, frm, c], send_sem=send_sems.at[a, d - 1],
                    recv_sem=recv_sems.at[a, d - 1], device_id=sib, device_id_type=MESH)
                cp.start()
                sends.append(cp)
        for d in range(1, 4):
            frm = (j + 4 - d) % 4
            for a in range(n):
                pltpu.make_async_remote_copy(
                    src_ref=buf[a].at[:, frm, 1 - c], dst_ref=buf[a].at[:, frm, 1 - c], send_sem=send_sems.at[a, d - 1],
                    recv_sem=recv_sems.at[a, d - 1], device_id=sib, device_id_type=MESH).wait_recv()
        for cp in sends:
            cp.wait_send()

    return pl.pallas_call(
        body, name=name,
        in_specs=[_ANY] * n, out_specs=[_ANY] * n,
        out_shape=[jax.ShapeDtypeStruct(b.shape, b.dtype) for b in bufs],
        scratch_shapes=[pltpu.SemaphoreType.DMA((n, 3))] * 2,
        input_output_aliases={a: a for a in range(n)},
    )(*bufs)


def _pair_add(g4, r1, cj_arr, name, TR=512):
    B4, _, rh, C = g4.shape
    B = B4 // 4
    TR = rh if rh <= TR else _row_tile(rh, TR)

    def body(cj_ref, g_ref, r_ref, o16_ref, own_ref):
        s = g_ref[...] + r_ref[...]
        o16_ref[...] = s.astype(BF16)

        @pl.when(pl.program_id(2) == cj_ref[1])
        def _():
            own_ref[...] = s

    return pl.pallas_call(
        body, name=name,
        grid_spec=pltpu.PrefetchScalarGridSpec(
            num_scalar_prefetch=1, grid=(B, rh // TR, 4),
            in_specs=[pl.BlockSpec((None, None, TR, C), lambda b, t, p, cj: (b * 4 + p, cj[0], t, 0)),
                      pl.BlockSpec((None, None, TR, C), lambda b, t, p, cj: (b * 4 + p, 0, t, 0))],
            out_specs=[pl.BlockSpec((None, None, TR, C), lambda b, t, p, cj: (p, b, t, 0)),
                       pl.BlockSpec((None, TR, C), lambda b, t, p, cj: (b, t, 0))]),
        out_shape=[jax.ShapeDtypeStruct((4, B, rh, C), BF16), jax.ShapeDtypeStruct((B, rh, C), F32)],
        compiler_params=_params(("parallel", "parallel", "arbitrary")),
    )(cj_arr, g4, r1)


def _chip_add(own, r2, cj_arr, into, first, name, TR=512):
    B, rh, C = own.shape
    TR = rh if rh <= TR else _row_tile(rh, TR)

    def body(cj_ref, p_ref, r_ref, _into_ref, o_ref):
        o_ref[...] = p_ref[...] + r_ref[0].astype(F32) + r_ref[1].astype(F32) + r_ref[2].astype(F32)

    return pl.pallas_call(
        body, name=name,
        grid_spec=pltpu.PrefetchScalarGridSpec(
            num_scalar_prefetch=1, grid=(B, rh // TR),
            in_specs=[pl.BlockSpec((None, TR, C), lambda b, t, cj: (b, t, 0)),
                      pl.BlockSpec((3, None, TR, C), lambda b, t, cj: (0, b, t, 0)),
                      _ANY],
            out_specs=pl.BlockSpec((None, None, TR, C), lambda b, t, cj: (first + b, cj[0], t, 0))),
        out_shape=jax.ShapeDtypeStruct(into.shape, F32),
        input_output_aliases={3: 0},
        compiler_params=_params(("parallel", "parallel")),
    )(cj_arr, own, r2, into)


def _pair_share(bufs, name):
    n = len(bufs)

    def body(*refs):
        buf = refs[n:2 * n]
        send_sems, recv_sems = refs[2 * n:]
        c = lax.axis_index("c")
        sib = (lax.axis_index("x"), lax.axis_index("y"), 1 - c)
        cps = []
        for a in range(n):
            cp = pltpu.make_async_remote_copy(
                src_ref=buf[a].at[:, c], dst_ref=buf[a].at[:, c], send_sem=send_sems.at[a],
                recv_sem=recv_sems.at[a], device_id=sib, device_id_type=MESH)
            cp.start()
            cps.append(cp)
        for a in range(n):
            pltpu.make_async_remote_copy(
                src_ref=buf[a].at[:, 1 - c], dst_ref=buf[a].at[:, 1 - c], send_sem=send_sems.at[a],
                recv_sem=recv_sems.at[a], device_id=sib, device_id_type=MESH).wait_recv()
        for cp in cps:
            cp.wait_send()

    return pl.pallas_call(
        body, name=name, in_specs=[_ANY] * n, out_specs=[_ANY] * n,
        out_shape=[jax.ShapeDtypeStruct(b.shape, b.dtype) for b in bufs],
        scratch_shapes=[pltpu.SemaphoreType.DMA((n,)), pltpu.SemaphoreType.DMA((n,))],
        input_output_aliases={a: a for a in range(n)},
    )(*bufs)


def _pair_swap(arr, name):
    def body(src, dst, send_sem, recv_sem):
        sib = (lax.axis_index("x"), lax.axis_index("y"), 1 - lax.axis_index("c"))
        cp = pltpu.make_async_remote_copy(src_ref=src, dst_ref=dst, send_sem=send_sem, recv_sem=recv_sem,
                                          device_id=sib, device_id_type=MESH)
        cp.start()
        cp.wait_recv()
        cp.wait_send()

    return pl.pallas_call(
        body, name=name, in_specs=[_ANY], out_specs=_ANY,
        out_shape=jax.ShapeDtypeStruct(arr.shape, arr.dtype),
        scratch_shapes=[pltpu.SemaphoreType.DMA, pltpu.SemaphoreType.DMA],
    )(arr)


class _ReduceScatter:
    def __init__(self, n_layers, cj_arr):
        self.L, self.cj = n_layers, cj_arr
        self.total = None
        self.pair = None
        self.chip = None

    def _land(self, after):
        handle, layer, owns = self.chip
        n = len(owns)
        r2 = _wait_copies(handle, _chip_plan, (after,), "rs_chip_wait")[n:]
        if self.total is None:
            self.total = [lax.empty((self.L * o.shape[0], 2) + o.shape[1:], F32) for o in owns]
        self.total = [_chip_add(o, r, self.cj, t, layer * o.shape[0], "rs_chip_add")
                      for o, r, t in zip(owns, r2, self.total)]
        self.chip = None

    def add_layer(self, layer, grads):
        g4 = [g.reshape(g.shape[0] * 4, 2, g.shape[1] // 8, g.shape[2]) for g in grads]
        lands = [lax.empty((g.shape[0], 1) + g.shape[2:], F32) for g in g4]
        handle = _start_copies(g4 + lands, _pair_plan, len(g4), "rs_pair_start")
        self.pair = (handle, layer)
        return (handle[-1],)

    def advance(self, after):
        if self.pair is None:
            return ()
        handle, layer = self.pair
        both = _wait_copies(handle, _pair_plan, (after,), "rs_pair_wait")
        n = len(both) // 2
        added = [_pair_add(g, r, self.cj, "rs_pair_add") for g, r in zip(both[:n], both[n:])]
        parts, owns = [p for p, _ in added], [o for _, o in added]
        if self.chip is not None:
            self._land(owns[-1])
        lands = [lax.empty((3,) + p.shape[1:], p.dtype) for p in parts]
        handle = _start_copies(parts + lands, _chip_plan, 3 * n, "rs_chip_start")
        self.pair, self.chip = None, (handle, layer, owns)
        return (handle[-1],)

    def result(self, after):
        self._land(after)
        full = _pair_share(self.total, "rs_pair_share")
        return [f.reshape(f.shape[0], f.shape[1] * f.shape[2], f.shape[3]) for f in full]


def _relu2_epi(acc):
    return acc, jnp.square(jnp.maximum(acc, 0.0))


def _relu2_bwd_epi(acc, u):
    return (acc * (2.0 * jnp.maximum(u, 0.0)),)


_GRAD_ORDER = ("winT", "wbT", "wout", "wq", "wkv", "woT", "wupT", "wdown")
_DW = dict(tm=512, tn=1024, tk=4096)
_LONG_K = dict(tm=1024, tn=1024, tk=2048)


def _forward_backward(x, mem, target, input_weight_of, weights_of, P, grads_done, grads_advance):
    L = P["g_norm"].shape[0]
    S, D = x.shape
    gn = lambda l, i: P["g_norm"][l, i][None]

    saved = []
    (h,) = _resnorm_fwd(x, None, None, gn(0, 0), "norm_in")
    xr = x
    for l in range(L):
        w_in_t = input_weight_of(l, xr)
        proj = _mm(h, w_in_t, "nt", "in_proj", b_pre=(0,), tn=1792)
        W, w_deps = weights_of(l, proj)
        small = (jnp.broadcast_to(P["sinks"][l][:, None], (8, BLK)), P["ws"][l], P["bs"][l][:, :, None],
                 P["gsgu"][l][None], P["wp"][l], P["ps"][l][None])
        br = _mix_fwd(proj, *small, "mix_fwd")
        pb = lax.empty((3, S, D), F32)
        for n in range(3):
            pb = _mm(br, W["wbT"], "nt", "branch_proj", a_pre=(n,), b_pre=(n,), into=pb, out_pre=(n,),
                     deps=w_deps if n == 0 else ())
        merged = _merge_fwd(proj, pb, "merge_fwd")
        z = _mm(merged, W["wout"], "nn", "out_proj", b_pre=(0,))
        x1, hm = _resnorm_fwd(xr, z, gn(l, 1), gn(l, 2), "resnorm_fwd")
        qm = _mm(hm, W["wq"], "nn", "mem_q", b_pre=(0,))
        (memn,) = _resnorm_fwd(mem, None, None, P["g_mem"][l][None], "mem_norm")
        kv = _mm(memn, W["wkv"], "nn", "mem_kv", b_pre=(0,))
        om = _memattn_fwd(qm, kv, "memattn_fwd")
        ym = _mm(om, W["woT"], "nt", "mem_o", b_pre=(0,))
        x2, hf = _resnorm_fwd(x1, ym, gn(l, 3), gn(l, 4), "resnorm_fwd")
        u, a = _mm(hf, W["wupT"], "nt", "mlp_up", b_pre=(0,), out_dtypes=(F32, BF16), epi=_relu2_epi)
        yf = _mm(a, W["wdown"], "nn", "mlp_down", b_pre=(0,), **_LONG_K)
        saved.append(dict(W=W, x0=xr, h=h, proj=proj, small=small, br=br, pb=pb, merged=merged, z=z, x1=x1, hm=hm,
                          qm=qm, memn=memn, kv=kv, om=om, ym=ym, x2=x2, hf=hf, u=u, a=a, yf=yf))
        if l < L - 1:
            xr, h = _resnorm_fwd(x2, yf, gn(l, 5), gn(l + 1, 0), "resnorm_fwd")
    dres, loss = _final_fwd(saved[-1]["x2"], saved[-1]["yf"], gn(L - 1, 5), target, "loss_head")

    dgn = [[None] * 6 for _ in range(L)]
    dsmall = {k: [None] * L for k in ("g_mem", "sinks", "ws", "bs", "gsgu", "wp", "ps")}
    dh = None
    for l in reversed(range(L)):
        s = saved[l]
        W, G = s["W"], {}
        if l == L - 1:
            dx2, dyf, dgn[l][5] = _resnorm_bwd(s["x2"], s["yf"], gn(l, 5), None, dres, None, "resnorm_bwd_top")
        else:
            dx2, dyf, dgn[l][5], dgn[l + 1][0] = _resnorm_bwd(s["x2"], s["yf"], gn(l, 5), gn(l + 1, 0), dres, dh,
                                                              "resnorm_bwd", deps=deps)
        du = _mm(dyf, W["wdown"], "nt", "mlp_down_dx", b_pre=(0,), out_dtypes=(BF16,), extras=(s["u"],), epi=_relu2_bwd_epi)
        G["wdown"] = _mm(s["a"], dyf, "tn", "mlp_down_dw", **_DW)[None]
        dhf = _mm(du, W["wupT"], "nn", "mlp_up_dx", b_pre=(0,), **_LONG_K)
        G["wupT"] = _mm(du, s["hf"], "tn", "mlp_up_dw", **_DW)[None]
        dx1, dym, dgn[l][3], dgn[l][4] = _resnorm_bwd(s["x1"], s["ym"], gn(l, 3), gn(l, 4), dx2, dhf, "resnorm_bwd")
        dom = _mm(dym, W["woT"], "nn", "mem_o_dx", b_pre=(0,), deps=grads_advance(dx1))
        G["woT"] = _mm(dym, s["om"], "tn", "mem_o_dw", **_DW)[None]
        dqm, dkv = _memattn_bwd(s["qm"], s["kv"], dom, "memattn_bwd")
        dmemn = _mm(dkv, W["wkv"], "nt", "mem_kv_dx", b_pre=(0,))
        G["wkv"] = _mm(s["memn"], dkv, "tn", "mem_kv_dw")[None]
        _, dsmall["g_mem"][l] = _resnorm_bwd(mem, None, None, P["g_mem"][l][None], None, dmemn, "mem_norm_bwd")
        dhm = _mm(dqm, W["wq"], "nt", "mem_q_dx", b_pre=(0,))
        G["wq"] = _mm(s["hm"], dqm, "tn", "mem_q_dw", **_DW)[None]
        dx0, dz, dgn[l][1], dgn[l][2] = _resnorm_bwd(s["x0"], s["z"], gn(l, 1), gn(l, 2), dx1, dhm, "resnorm_bwd")
        dmerged = _mm(dz, W["wout"], "nt", "out_proj_dx", b_pre=(0,))
        G["wout"] = _mm(s["merged"], dz, "tn", "out_proj_dw", **_DW)[None]
        dproj, dpb = _merge_bwd(s["proj"], s["pb"], dmerged, "merge_bwd")
        dbr = lax.empty((3, S, 512), F32)
        G["wbT"] = lax.empty(W["wbT"].shape, F32)
        for n in range(3):
            dbr = _mm(dpb, W["wbT"], "nn", "branch_proj_dx", a_pre=(n,), b_pre=(n,), into=dbr, out_pre=(n,))
            G["wbT"] = _mm(dpb, s["br"], "tn", "branch_proj_dw", a_pre=(n,), b_pre=(n,), into=G["wbT"], out_pre=(n,), **_DW)
        (dproj, dsmall["sinks"][l], dsmall["ws"][l], dsmall["bs"][l], dsmall["gsgu"][l], dsmall["wp"][l],
         dsmall["ps"][l]) = _mix_bwd(s["proj"], dbr, dproj, *s["small"], "mix_bwd")
        dh = _mm(dproj, W["winT"], "nn", "in_proj_dx", b_pre=(0,), **_LONG_K)
        G["winT"] = _mm(dproj, s["h"], "tn", "in_proj_dw", **_DW)[None]
        deps = grads_done(l, [G[k] for k in _GRAD_ORDER])
        dres = dx0
    grad_x, dgn[0][0] = _resnorm_bwd(x, None, None, gn(0, 0), dres, dh, "norm_in_bwd", deps=deps)
    tail_deps = grads_advance(grad_x)

    small_grads = dict(
        g_norm=jnp.stack([jnp.concatenate(row, axis=0) for row in dgn]),
        g_mem=jnp.concatenate(dsmall["g_mem"], axis=0),
        sinks=jnp.stack([d[:, 0] for d in dsmall["sinks"]]),
        ws=jnp.stack(dsmall["ws"]),
        bs=jnp.stack([d[:, :, 0] for d in dsmall["bs"]]),
        gsgu=jnp.concatenate(dsmall["gsgu"], axis=0),
        wp=jnp.stack(dsmall["wp"]),
        ps=jnp.concatenate(dsmall["ps"], axis=0),
    )
    return loss, grad_x, small_grads, tail_deps


_PACK_ROWS = 512


def _as_rows(a):
    n = math.prod(a.shape)
    if n % 128:
        a = jnp.pad(a.reshape(-1), (0, (-n) % 128))
    r = a.reshape(-1, 128)
    return jnp.pad(r, ((0, (-r.shape[0]) % 8), (0, 0))) if r.shape[0] % 8 else r


def _pack(arrays):
    rows = [_as_rows(a) for a in arrays]
    total = sum(r.shape[0] for r in rows)
    tail = (-total) % _PACK_ROWS
    if tail:
        rows.append(jnp.zeros((tail, 128), rows[0].dtype))
    return jnp.concatenate(rows, axis=0)


def _unpack(packed, like):
    out, pos = [], 0
    for a in like:
        n = math.prod(a.shape)
        nr = -(-n // 128)
        rows = packed[pos:pos + nr]
        out.append((rows.reshape(-1)[:n] if n % 128 else rows).reshape(a.shape))
        pos += nr + (-nr) % 8
    return out


_BIG = ("w_in", "w_branch", "w_out", "w_q_mem", "w_kv_mem", "w_o_mem", "w_up", "w_down")
_SMALL = ("g_norm", "g_mem", "attn_sinks", "w_spatial", "b_spatial", "g_sgu", "w_pool", "pool_scale")
_WEIGHTS = ("g_norm", "g_mem", "w_in", "attn_sinks", "w_spatial", "b_spatial", "g_sgu", "w_pool", "pool_scale",
            "w_branch", "w_out", "w_q_mem", "w_kv_mem", "w_o_mem", "w_up", "w_down")


def _to_working(name, w):
    if name == "w_in":
        return jnp.swapaxes(w, 1, 2)
    if name == "w_branch":
        t = jnp.swapaxes(w, 2, 3)
        return t.reshape(t.shape[0] * 3, t.shape[2], t.shape[3])
    if name in ("w_o_mem", "w_up"):
        return jnp.swapaxes(w, 1, 2)
    return w


def _from_working(name, g):
    if name == "w_in":
        return jnp.swapaxes(g, 1, 2)
    if name == "w_branch":
        return jnp.swapaxes(g.reshape(g.shape[0] // 3, 3, g.shape[1], g.shape[2]), 2, 3)
    if name in ("w_o_mem", "w_up"):
        return jnp.swapaxes(g, 1, 2)
    return g


_WKEY = dict(w_in="winT", w_branch="wbT", w_out="wout", w_q_mem="wq", w_kv_mem="wkv", w_o_mem="woT",
             w_up="wupT", w_down="wdown")


def kernel(x, mem, g_norm, g_mem, w_in, attn_sinks, w_spatial, b_spatial, g_sgu, w_pool, pool_scale, w_branch, w_out, w_q_mem, w_kv_mem, w_o_mem, w_up, w_down, loss_target, m_g_norm, m_g_mem, m_w_in, m_attn_sinks, m_w_spatial, m_b_spatial, m_g_sgu, m_w_pool, m_pool_scale, m_w_branch, m_w_out, m_w_q_mem, m_w_kv_mem, m_w_o_mem, m_w_up, m_w_down, v_g_norm, v_g_mem, v_w_in, v_attn_sinks, v_w_spatial, v_b_spatial, v_g_sgu, v_w_pool, v_pool_scale, v_w_branch, v_w_out, v_w_q_mem, v_w_kv_mem, v_w_o_mem, v_w_up, v_w_down):
    w = dict(g_norm=g_norm, g_mem=g_mem, w_in=w_in, attn_sinks=attn_sinks, w_spatial=w_spatial, b_spatial=b_spatial,
             g_sgu=g_sgu, w_pool=w_pool, pool_scale=pool_scale, w_branch=w_branch, w_out=w_out, w_q_mem=w_q_mem,
             w_kv_mem=w_kv_mem, w_o_mem=w_o_mem, w_up=w_up, w_down=w_down)
    m = dict(g_norm=m_g_norm, g_mem=m_g_mem, w_in=m_w_in, attn_sinks=m_attn_sinks, w_spatial=m_w_spatial,
             b_spatial=m_b_spatial, g_sgu=m_g_sgu, w_pool=m_w_pool, pool_scale=m_pool_scale, w_branch=m_w_branch,
             w_out=m_w_out, w_q_mem=m_w_q_mem, w_kv_mem=m_w_kv_mem, w_o_mem=m_w_o_mem, w_up=m_w_up, w_down=m_w_down)
    v = dict(g_norm=v_g_norm, g_mem=v_g_mem, w_in=v_w_in, attn_sinks=v_attn_sinks, w_spatial=v_w_spatial,
             b_spatial=v_b_spatial, g_sgu=v_g_sgu, w_pool=v_w_pool, pool_scale=v_pool_scale, w_branch=v_w_branch,
             w_out=v_w_out, w_q_mem=v_w_q_mem, w_kv_mem=v_w_kv_mem, w_o_mem=v_w_o_mem, w_up=v_w_up, w_down=v_w_down)
    L = g_norm.shape[0]
    j = 2 * lax.axis_index("x") + lax.axis_index("y")
    c = lax.axis_index("c")
    j_arr = jnp.reshape(j, (1,)).astype(jnp.int32)
    cj_arr = jnp.stack([c, j]).astype(jnp.int32)

    gs = g_norm.shape[2]
    working = [_to_working(n, w[n]) for n in _BIG]
    per_layer = [wk.shape[0] // L for wk in working]

    def own_slabs(l):
        return [_own_slab(wk, BF16, j_arr, "own_slab", first=l * b, count=b) for wk, b in zip(working, per_layer)]

    first_slabs = own_slabs(0)
    lead = [first_slabs[0], _own_slab(g_norm.reshape(1, L * 6 * gs // 128, 128), F32, j_arr, "own_slab_norm")]
    lead_handle = _start_copies(lead, _gather_plan, 3 * len(lead), "gather_start_lead")
    rest_handle = _start_copies(first_slabs[1:], _gather_plan, 3 * (len(first_slabs) - 1), "gather_start_first",
                                deps=(lead_handle[-1],))
    slabs = {l: own_slabs(l) for l in range(1, L)}
    lead = _gather_forward(_wait_copies(lead_handle, _gather_plan,
                                        [rest_handle[-1]] + [s for l in slabs for s in slabs[l]],
                                        "gather_wait_lead"), "gather_forward_lead")
    gn_full = jnp.transpose(lead[1].reshape(4, L * 6, gs), (1, 0, 2)).reshape(L, 6, 4 * gs)
    P = dict(g_norm=gn_full, g_mem=g_mem, sinks=attn_sinks, ws=w_spatial, bs=b_spatial, gsgu=g_sgu, wp=w_pool,
             ps=pool_scale)
    whole = lambda g: g.reshape(g.shape[0], 8 * g.shape[3], g.shape[4])
    gathered, in_flight = {}, {}

    def layer_weights(l, after):
        if l not in gathered:
            gathered[l] = _gather_forward(_wait_copies(in_flight[l], _gather_plan, (after,), "gather_wait"),
                                          "gather_forward")
        return gathered[l]

    def input_weight_of(l, after):
        return whole(lead[0] if l == 0 else layer_weights(l, after)[0])

    def weights_of(l, after):
        deps = ()
        if l == 0:
            rest = _gather_forward(_wait_copies(rest_handle, _gather_plan, (after,), "gather_wait_first"),
                                   "gather_forward_first")
            gathered[0] = [lead[0], *rest]
            dep = rest[0]
            for k in range(1, L):
                in_flight[k] = _start_copies(slabs[k], _gather_plan, 3 * len(slabs[k]), "gather_start", deps=(dep,))
                dep = in_flight[k][-1]
            deps = tuple(h[-1] for h in in_flight.values())
        return {k: whole(g) for k, g in zip(_GRAD_ORDER, layer_weights(l, after))}, deps

    rs = _ReduceScatter(L, cj_arr)
    loss_part, grad_x, sg, tail_deps = _forward_backward(
        x[0], mem[0], loss_target[0], input_weight_of, weights_of, P, rs.add_layer, rs.advance)
    loss = lax.psum(loss_part[0, 0], ("x", "y", "c"))

    full_small = [sg["g_norm"], sg["g_mem"], sg["sinks"], sg["ws"], sg["bs"], sg["gsgu"], sg["wp"], sg["ps"]]
    packed = _pack(full_small)
    pair_sum = _own_slab(packed[None], F32, j_arr, "small_grads_pair_sum", deps=tail_deps,
                         plus=_pair_swap(packed, "small_grads_swap")[None])
    (chip_sums,) = _gather_weights([pair_sum], "gather_small_grads")
    total = _sum_slots(chip_sums.reshape(4, *packed.shape), "sum_small_grads")
    grads = {n: _from_working(n, g) for n, g in zip(_BIG, rs.result(total))}
    for n, g in zip(_SMALL, _unpack(total, full_small)):
        grads[n] = lax.dynamic_slice_in_dim(g, j * g_norm.shape[2], g_norm.shape[2], axis=2) if n == "g_norm" else g

    delta, new_m, new_v = {}, {}, {}
    for n in _BIG:
        view = (lambda t: jnp.swapaxes(t, 1, 2)) if n == "w_in" else (lambda t: t)
        shp = view(w[n]).shape
        two_d = lambda t: view(t).reshape(-1, shp[-1])
        d_, m_, v_ = _adamw(two_d(w[n]), two_d(grads[n]), two_d(m[n]), two_d(v[n]), "adamw")
        delta[n], new_m[n], new_v[n] = view(d_.reshape(shp)), view(m_.reshape(shp)), view(v_.reshape(shp))
    small_w = [w[n] for n in _SMALL]
    d_, m_, v_ = _adamw(_pack(small_w), _pack([grads[n] for n in _SMALL]), _pack([m[n] for n in _SMALL]),
                        _pack([v[n] for n in _SMALL]), "adamw_small")
    for n, dd, mm_, vv in zip(_SMALL, _unpack(d_, small_w), _unpack(m_, small_w), _unpack(v_, small_w)):
        delta[n], new_m[n], new_v[n] = dd, mm_, vv

    return (loss, grad_x[None], *[grads[n] for n in _WEIGHTS], *[delta[n] for n in _WEIGHTS],
            *[new_m[n] for n in _WEIGHTS], *[new_v[n] for n in _WEIGHTS])
```

```python
import functools
import math

import jax
import jax.numpy as jnp
from jax import lax
from jax.experimental import pallas as pl
from jax.experimental.pallas import tpu as pltpu

F32 = jnp.float32
BF16 = jnp.bfloat16
MESH = pl.DeviceIdType.MESH

EPS = 1e-6
NEG_INF = -1e30
BLK = 128
HALO = 16
POOL_WINDOWS = (2, 4, 8, 16)
ATT_SCALE = 1.0 / math.sqrt(64.0)
MEM_SCALE = 1.0 / math.sqrt(128.0)
C_Q, C_K, C_V, C_SU, C_SV, C_PC, C_GATE, C_END = 0, 512, 640, 768, 1280, 1792, 2304, 5376

ADAM_LR, ADAM_B1, ADAM_B2, ADAM_EPS, ADAM_WD, ADAM_STEP = 0.001, 0.9, 0.999, 1e-08, 0.01, 10

VMEM_LIMIT_BYTES = 56 * 1024 * 1024

_DIMS = {
    "nn": (((1,), (0,)), ((), ())),
    "nt": (((1,), (1,)), ((), ())),
    "tn": (((0,), (0,)), ((), ())),
}


def _dot(a, b, mode):
    return lax.dot_general(a, b, _DIMS[mode], preferred_element_type=F32)


def _params(semantics):
    return pltpu.CompilerParams(dimension_semantics=semantics, vmem_limit_bytes=VMEM_LIMIT_BYTES)


def _tile(dim, pref):
    if dim <= pref:
        return dim
    t = (pref // 128) * 128
    while t >= 128:
        if dim % t == 0:
            return t
        t -= 128
    raise ValueError(f"no tile for {dim}")


def _rms(x, g):
    return x * lax.rsqrt(jnp.mean(x * x, axis=-1, keepdims=True) + EPS) * g


def _mm(a, b, mode, name, *, out_dtypes=(F32,), a_pre=(), b_pre=(), into=None, out_pre=(),
        extras=(), epi=None, deps=(), tm=2048, tn=1024, tk=1024):
    a2, b2 = a.shape[len(a_pre):], b.shape[len(b_pre):]
    if mode == "nn":
        (M, K), (K2, N) = a2, b2
    elif mode == "nt":
        (M, K), (N, K2) = a2, b2
    else:
        (K, M), (K2, N) = a2, b2
    assert K == K2, (a.shape, b.shape, mode)
    tm, tn, tk = _tile(M, tm), _tile(N, tn), _tile(K, tk)
    nk = K // tk
    na, nb_, no = len(a_pre), len(b_pre), len(out_pre)
    if mode == "tn":
        a_spec = pl.BlockSpec((None,) * na + (tk, tm), lambda i, j, k: a_pre + (k, i))
    else:
        a_spec = pl.BlockSpec((None,) * na + (tm, tk), lambda i, j, k: a_pre + (i, k))
    if mode == "nt":
        b_spec = pl.BlockSpec((None,) * nb_ + (tn, tk), lambda i, j, k: b_pre + (j, k))
    else:
        b_spec = pl.BlockSpec((None,) * nb_ + (tk, tn), lambda i, j, k: b_pre + (k, j))
    tile_spec = pl.BlockSpec((tm, tn), lambda i, j, k: (i, j))
    ne, nout = len(extras), len(out_dtypes)
    in_specs = [a_spec, b_spec] + [tile_spec] * ne
    operands = [a, b, *extras]
    aliases = {}
    if into is not None:
        assert nout == 1
        in_specs.append(pl.BlockSpec(memory_space=pl.ANY))
        operands.append(into)
        aliases = {len(operands) - 1: 0}
        out_shape = [jax.ShapeDtypeStruct(into.shape, into.dtype)]
        out_specs = [pl.BlockSpec((None,) * no + (tm, tn), lambda i, j, k: out_pre + (i, j))]
    else:
        out_shape = [jax.ShapeDtypeStruct((M, N), dt) for dt in out_dtypes]
        out_specs = [tile_spec] * nout
    in_specs += [pl.BlockSpec(memory_space=pl.ANY)] * len(deps)
    operands += list(deps)

    def body(*refs):
        a_ref, b_ref = refs[0], refs[1]
        ex = refs[2:2 + ne]
        pos = 2 + ne + (1 if into is not None else 0) + len(deps)
        outs = refs[pos:pos + nout]
        acc_ref = refs[pos + nout] if nk > 1 else None

        def finish(acc):
            vals = epi(acc, *[e[...] for e in ex]) if epi is not None else (acc,)
            for o, v in zip(outs, vals):
                o[...] = v.astype(o.dtype)

        def prod():
            return _dot(a_ref[...].astype(BF16), b_ref[...].astype(BF16), mode)

        if nk == 1:
            finish(prod())
        else:
            k = pl.program_id(2)

            @pl.when(k == 0)
            def _():
                acc_ref[...] = jnp.zeros_like(acc_ref)

            acc_ref[...] += prod()

            @pl.when(k == nk - 1)
            def _():
                finish(acc_ref[...])

    res = pl.pallas_call(
        body, name=name, grid=(M // tm, N // tn, nk),
        in_specs=in_specs, out_specs=out_specs, out_shape=out_shape,
        scratch_shapes=[pltpu.VMEM((tm, tn), F32)] if nk > 1 else [],
        input_output_aliases=aliases,
        compiler_params=_params(("parallel", "parallel", "arbitrary")),
    )(*operands)
    return res[0] if nout == 1 else tuple(res)


def _resnorm_fn(has_post, has_pre):
    def f(*a):
        x, k = a[0], 1
        if has_post:
            x, k = x + _rms(a[1], a[2]), 3
        outs = [x]
        if has_pre:
            outs.append(_rms(x, a[k]))
        return tuple(outs)
    return f


def _row_spec(T, W):
    return pl.BlockSpec((T, W), lambda i: (i, 0))


def _par_spec(W):
    return pl.BlockSpec((1, W), lambda i: (0, 0))


def _resnorm_fwd(xr, y, gp, gq, name, T=512, deps=()):
    S, D = xr.shape
    T = min(T, S)
    has_post, has_pre = y is not None, gq is not None
    f = _resnorm_fn(has_post, has_pre)
    ins = [xr] + ([y, gp] if has_post else []) + ([gq] if has_pre else [])
    in_specs = [_row_spec(T, D)] + ([_row_spec(T, D), _par_spec(D)] if has_post else []) + ([_par_spec(D)] if has_pre else [])
    out_shape, out_specs = [], []
    if has_post:
        out_shape.append(jax.ShapeDtypeStruct((S, D), F32)); out_specs.append(_row_spec(T, D))
    if has_pre:
        out_shape.append(jax.ShapeDtypeStruct((S, D), BF16)); out_specs.append(_row_spec(T, D))
    n_in, n_dep = len(ins), len(deps)

    def body(*refs):
        vals = f(*[r[...] for r in refs[:n_in]])
        outs = list(refs[n_in + n_dep:])
        if has_post:
            outs.pop(0)[...] = vals[0]
        if has_pre:
            outs.pop(0)[...] = vals[1].astype(BF16)

    res = pl.pallas_call(body, name=name, grid=(S // T,),
                         in_specs=in_specs + [pl.BlockSpec(memory_space=pl.ANY)] * n_dep, out_specs=out_specs,
                         out_shape=out_shape, compiler_params=_params(("parallel",)))(*ins, *deps)
    return tuple(res)


def _resnorm_bwd(xr, y, gp, gq, dres, dh, name, T=512, deps=()):
    S, D = xr.shape
    T = min(T, S)
    has_post, has_pre, has_res = y is not None, gq is not None, dres is not None
    f = _resnorm_fn(has_post, has_pre)
    ins = [xr] + ([y, gp] if has_post else []) + ([gq] if has_pre else [])
    in_specs = [_row_spec(T, D)] + ([_row_spec(T, D), _par_spec(D)] if has_post else []) + ([_par_spec(D)] if has_pre else [])
    n_prim = len(ins)
    if has_res:
        ins.append(dres); in_specs.append(_row_spec(T, D))
    if has_pre:
        ins.append(dh); in_specs.append(_row_spec(T, D))
    n_in, n_dep = len(ins), len(deps)
    out_shape = [jax.ShapeDtypeStruct((S, D), F32)]
    out_specs = [_row_spec(T, D)]
    if has_post:
        out_shape += [jax.ShapeDtypeStruct((S, D), BF16), jax.ShapeDtypeStruct((1, D), F32)]
        out_specs += [_row_spec(T, D), _par_spec(D)]
    if has_pre:
        out_shape.append(jax.ShapeDtypeStruct((1, D), F32)); out_specs.append(_par_spec(D))

    def body(*refs):
        i = pl.program_id(0)
        prim = [r[...] for r in refs[:n_prim]]
        rest = list(refs[n_prim:n_in])
        ct_x = rest.pop(0)[...] if has_res else jnp.zeros((T, D), F32)
        cts = [ct_x]
        if has_pre:
            cts.append(rest.pop(0)[...].astype(F32))
        _, vjp = jax.vjp(f, *prim)
        grads = list(vjp(tuple(cts)))
        outs = list(refs[n_in + n_dep:])
        outs.pop(0)[...] = grads.pop(0)
        acc = []
        if has_post:
            outs.pop(0)[...] = grads.pop(0).astype(BF16)
            acc.append((outs.pop(0), grads.pop(0)))
        if has_pre:
            acc.append((outs.pop(0), grads.pop(0)))

        @pl.when(i == 0)
        def _():
            for o, _g in acc:
                o[...] = jnp.zeros_like(o)

        for o, g in acc:
            o[...] += g

    res = pl.pallas_call(body, name=name, grid=(S // T,),
                         in_specs=in_specs + [pl.BlockSpec(memory_space=pl.ANY)] * n_dep, out_specs=out_specs,
                         out_shape=out_shape, compiler_params=_params(("arbitrary",)))(*ins, *deps)
    return tuple(res)


def _final_fwd(xr, y, gp, target, name, T=512):
    S, D = xr.shape
    T = min(T, S)

    def body(x_ref, y_ref, g_ref, t_ref, dy_ref, loss_ref):
        i = pl.program_id(0)
        e = x_ref[...] + _rms(y_ref[...], g_ref[...]) - t_ref[...]
        dy_ref[...] = e / D

        @pl.when(i == 0)
        def _():
            loss_ref[...] = jnp.zeros_like(loss_ref)

        loss_ref[...] += 0.5 * jnp.sum(jnp.sum(e * e, axis=-1, keepdims=True) / D, axis=0, keepdims=True)

    return pl.pallas_call(
        body, name=name, grid=(S // T,),
        in_specs=[_row_spec(T, D), _row_spec(T, D), _par_spec(D), _row_spec(T, D)],
        out_specs=[_row_spec(T, D), pl.BlockSpec((1, 128), lambda i: (0, 0))],
        out_shape=[jax.ShapeDtypeStruct((S, D), F32), jax.ShapeDtypeStruct((1, 128), F32)],
        compiler_params=_params(("arbitrary",)))(xr, y, gp, target)


_STRAIGHT_HEADS = (0, 2, 5, 7)
_ROLLED_HEADS = (1, 3, 4, 6)


def _straight_lanes():
    r = lax.broadcasted_iota(jnp.int32, (4 * BLK, BLK), 0)
    c = lax.broadcasted_iota(jnp.int32, (4 * BLK, BLK), 1)
    return (r < 2 * BLK) == (c < 64)


def _att_mask(not_first):
    k = lax.broadcasted_iota(jnp.int32, (2 * BLK, 4 * BLK), 0)
    q = lax.broadcasted_iota(jnp.int32, (2 * BLK, 4 * BLK), 1) % BLK
    qc, kc = 2 + q // 64, k // 64
    return (kc <= qc) & (kc >= qc - 2) & (not_first | (k >= BLK))


def _sink_row(sk_ref, heads):
    return jnp.concatenate([sk_ref[h:h + 1, :] for h in heads], axis=1)


def _softmax_sink(s, sk):
    m = jnp.maximum(jnp.max(s, axis=0, keepdims=True), sk)
    e = jnp.exp(s - m)
    es = jnp.exp(sk - m)
    z = jnp.sum(e, axis=0, keepdims=True) + es
    return e / z, es / z


def _att_bands(cur, kvp):
    kband = jnp.concatenate([kvp[:, 0:BLK], cur[:, C_K:C_K + BLK]], axis=0)
    vband = jnp.concatenate([kvp[:, BLK:2 * BLK], cur[:, C_V:C_V + BLK]], axis=0)
    return kband, pltpu.roll(kband, 64, 1), vband, pltpu.roll(vband, 64, 1)


def _stack_tiles(ref_or_val, start):
    return jnp.concatenate([ref_or_val[:, start + BLK * t:start + BLK * (t + 1)] for t in range(4)], axis=0)


def _sgu_mask():
    r = lax.broadcasted_iota(jnp.int32, (BLK, BLK), 0)
    c = lax.broadcasted_iota(jnp.int32, (BLK, BLK), 1)
    return (c // 64) <= (r // 64)


def _pool_cnt(blk, w):
    t = blk * BLK + lax.broadcasted_iota(jnp.int32, (BLK, 1), 0)
    return jnp.minimum(t + 1, w).astype(F32)


def _mix_in_specs(nb, rev):
    def b(i):
        return nb - 1 - i if rev else i
    return [
        pl.BlockSpec((BLK, C_GATE), lambda i: (b(i), 0)),
        pl.BlockSpec((BLK, 2 * BLK), lambda i: (jnp.maximum(b(i) - 1, 0), C_K // (2 * BLK))),
        pl.BlockSpec((HALO, C_GATE), lambda i: (jnp.maximum(b(i) * (BLK // HALO) - 1, 0), 0)),
        pl.BlockSpec((8, BLK), lambda i: (0, 0)),
        pl.BlockSpec((4, BLK, BLK), lambda i: (0, 0, 0)),
        pl.BlockSpec((4, BLK, 1), lambda i: (0, 0, 0)),
        pl.BlockSpec((1, 512), lambda i: (0, 0)),
        pl.BlockSpec((4, BLK, BLK), lambda i: (0, 0, 0)),
        pl.BlockSpec((1, 512), lambda i: (0, 0)),
    ]


def _mix_fwd(proj, sinks_b, ws, bs3, gsgu, wp, ps, name):
    S = proj.shape[0]
    nb = S // BLK

    def body(cur_ref, kvp_ref, pcp_ref, sk_ref, ws_ref, bs_ref, gs_ref, wp_ref, ps_ref, br_ref, ext_ref):
        i = pl.program_id(0)
        not_first = i > 0
        cur, kvp = cur_ref[...], kvp_ref[...]
        mask = _att_mask(not_first)
        own = _straight_lanes()
        q = _stack_tiles(cur, C_Q)
        outs = []
        kband, kroll, vband, vroll = _att_bands(cur, kvp)
        for qg, kg, vg, heads in ((jnp.where(own, q, 0.0), kband, vband, _STRAIGHT_HEADS),
                                  (jnp.where(own, 0.0, q), kroll, vroll, _ROLLED_HEADS)):
            s = jnp.where(mask, _dot(kg.astype(BF16), qg.astype(BF16), "nt") * ATT_SCALE, NEG_INF)
            p, _ = _softmax_sink(s, _sink_row(sk_ref, heads))
            outs.append(_dot(p.astype(BF16), vg.astype(BF16), "tn"))
        o = jnp.where(own, outs[0], outs[1])
        for t in range(4):
            br_ref[0, :, BLK * t:BLK * (t + 1)] = o[BLK * t:BLK * (t + 1)].astype(BF16)
        gu = jax.nn.gelu(cur[:, C_SU:C_SV])
        vn = _rms(jax.nn.gelu(cur[:, C_SV:C_PC]), gs_ref[...]).astype(BF16)
        wmask = _sgu_mask()
        for g in range(4):
            wm = jnp.where(wmask, ws_ref[g], 0.0).astype(BF16)
            sp = _dot(wm, vn[:, BLK * g:BLK * (g + 1)], "nn") + bs_ref[g]
            br_ref[1, :, BLK * g:BLK * (g + 1)] = (gu[:, BLK * g:BLK * (g + 1)] * sp).astype(BF16)
        c = cur[:, C_PC:C_GATE]
        ext_ref[0:HALO, :] = jnp.where(not_first, pcp_ref[:, C_PC:C_GATE], 0.0)
        ext_ref[HALO:HALO + BLK, :] = c
        for g, w in enumerate(POOL_WINDOWS):
            sl = slice(BLK * g, BLK * (g + 1))
            acc = ext_ref[HALO:HALO + BLK, sl]
            for k in range(1, w):
                acc = acc + ext_ref[HALO - k:HALO - k + BLK, sl]
            pooled = acc / _pool_cnt(i, w) - c[:, sl]
            mixed = _dot(pooled.astype(BF16), wp_ref[g].astype(BF16), "nn")
            br_ref[2, :, sl] = (mixed * ps_ref[:, sl]).astype(BF16)

    return pl.pallas_call(
        body, name=name, grid=(nb,),
        in_specs=_mix_in_specs(nb, False),
        out_specs=pl.BlockSpec((3, BLK, 512), lambda i: (0, i, 0)),
        out_shape=jax.ShapeDtypeStruct((3, S, 512), BF16),
        scratch_shapes=[pltpu.VMEM((HALO + BLK, 512), F32)],
        compiler_params=_params(("parallel",)),
    )(proj, proj, proj, sinks_b, ws, bs3, gsgu, wp, ps)


def _mix_bwd(proj, dbr, dproj, sinks_b, ws, bs3, gsgu, wp, ps, name):
    S = proj.shape[0]
    nb = S // BLK

    def body(cur_ref, kvp_ref, pcp_ref, sk_ref, ws_ref, bs_ref, gs_ref, wp_ref, ps_ref, dbr_ref, _dproj_in,
             dp_ref, dsk_ref, dws_ref, dbs_ref, dgs_ref, dwp_ref, dps_ref,
             ext_ref, z_ref, ckv_ref, cpc_ref):
        i = pl.program_id(0)
        blk = nb - 1 - i
        not_first = blk > 0

        @pl.when(i == 0)
        def _():
            for r in (dsk_ref, dws_ref, dbs_ref, dgs_ref, dwp_ref, dps_ref, ckv_ref, cpc_ref, z_ref):
                r[...] = jnp.zeros_like(r)

        cur, kvp = cur_ref[...], kvp_ref[...]
        mask = _att_mask(not_first)
        own = _straight_lanes()
        q = _stack_tiles(cur, C_Q)
        do = jnp.concatenate([dbr_ref[0, :, BLK * t:BLK * (t + 1)] for t in range(4)], axis=0)
        kband, kroll, vband, vroll = _att_bands(cur, kvp)
        dqs, dks, dvs = [], [], []
        for qg, dog, kg, vg, heads in (
                (jnp.where(own, q, 0.0), jnp.where(own, do, 0.0), kband, vband, _STRAIGHT_HEADS),
                (jnp.where(own, 0.0, q), jnp.where(own, 0.0, do), kroll, vroll, _ROLLED_HEADS)):
            qg, dog, kg, vg = qg.astype(BF16), dog.astype(BF16), kg.astype(BF16), vg.astype(BF16)
            s = jnp.where(mask, _dot(kg, qg, "nt") * ATT_SCALE, NEG_INF)
            p, p_sink = _softmax_sink(s, _sink_row(sk_ref, heads))
            dp = _dot(vg, dog, "nt")
            rs = jnp.sum(p * dp, axis=0, keepdims=True)
            ds = (p * (dp - rs) * ATT_SCALE).astype(BF16)
            sink_row = p_sink * rs
            for t, h in enumerate(heads):
                dsk_ref[h:h + 1, :] += jnp.broadcast_to(
                    -jnp.sum(sink_row[:, BLK * t:BLK * (t + 1)], axis=1, keepdims=True), (1, BLK))
            dvs.append(_dot(p.astype(BF16), dog, "nn"))
            dks.append(_dot(ds, qg, "nn"))
            dqs.append(_dot(ds, kg, "tn"))
        dq = jnp.where(own, dqs[0], dqs[1])
        for t in range(4):
            dp_ref[:, C_Q + BLK * t:C_Q + BLK * (t + 1)] = dq[BLK * t:BLK * (t + 1)].astype(BF16)
        dk = dks[0] + pltpu.roll(dks[1], 64, 1)
        dv = dvs[0] + pltpu.roll(dvs[1], 64, 1)
        dp_ref[:, C_K:C_K + BLK] = (dk[BLK:] + ckv_ref[:, 0:BLK]).astype(BF16)
        dp_ref[:, C_V:C_V + BLK] = (dv[BLK:] + ckv_ref[:, BLK:]).astype(BF16)
        ckv_ref[:, 0:BLK] = dk[:BLK]
        ckv_ref[:, BLK:] = dv[:BLK]
        su, sv = cur[:, C_SU:C_SV], cur[:, C_SV:C_PC]
        gu, vjp_u = jax.vjp(jax.nn.gelu, su)
        vn, vjp_v = jax.vjp(lambda a, g: _rms(jax.nn.gelu(a), g), sv, gs_ref[...])
        vn16 = vn.astype(BF16)
        wmask = _sgu_mask()
        dgu, dvn = [], []
        for g in range(4):
            sl = slice(BLK * g, BLK * (g + 1))
            wm = jnp.where(wmask, ws_ref[g], 0.0).astype(BF16)
            sp = _dot(wm, vn16[:, sl], "nn") + bs_ref[g]
            dyb = dbr_ref[1, :, sl]
            dgu.append(dyb * sp)
            dsp = dyb * gu[:, sl]
            dsp16 = dsp.astype(BF16)
            dvn.append(_dot(wm, dsp16, "tn"))
            dws_ref[g] += jnp.where(wmask, _dot(dsp16, vn16[:, sl], "nt"), 0.0)
            dbs_ref[g] += jnp.sum(dsp, axis=1, keepdims=True)
        (dsu,) = vjp_u(jnp.concatenate(dgu, axis=1))
        dsv, dgs = vjp_v(jnp.concatenate(dvn, axis=1))
        dp_ref[:, C_SU:C_SV] = dsu.astype(BF16)
        dp_ref[:, C_SV:C_PC] = dsv.astype(BF16)
        dgs_ref[...] += dgs
        c = cur[:, C_PC:C_GATE]
        ext_ref[0:HALO, :] = jnp.where(not_first, pcp_ref[:, C_PC:C_GATE], 0.0)
        ext_ref[HALO:HALO + BLK, :] = c
        for g, w in enumerate(POOL_WINDOWS):
            sl = slice(BLK * g, BLK * (g + 1))
            acc = ext_ref[HALO:HALO + BLK, sl]
            for k in range(1, w):
                acc = acc + ext_ref[HALO - k:HALO - k + BLK, sl]
            cnt = _pool_cnt(blk, w)
            pooled16 = (acc / cnt - c[:, sl]).astype(BF16)
            wp16 = wp_ref[g].astype(BF16)
            mixed = _dot(pooled16, wp16, "nn")
            dyc = dbr_ref[2, :, sl]
            dps_ref[:, sl] += jnp.sum(dyc * mixed, axis=0, keepdims=True)
            dmixed16 = (dyc * ps_ref[:, sl]).astype(BF16)
            dwp_ref[g] += _dot(pooled16, dmixed16, "tn")
            dpooled = _dot(dmixed16, wp16, "nt")
            z_ref[HALO:HALO + BLK, sl] = dpooled / cnt
            dext = z_ref[0:HALO + BLK, sl]
            for k in range(1, w):
                dext = dext + z_ref[k:k + HALO + BLK, sl]
            dp_ref[:, C_PC + BLK * g:C_PC + BLK * (g + 1)] = (
                dext[HALO:] - dpooled + jnp.concatenate([jnp.zeros((BLK - HALO, BLK), F32), cpc_ref[:, sl]], axis=0)
            ).astype(BF16)
            cpc_ref[:, sl] = dext[:HALO]

    n_in = 11
    small = [jax.ShapeDtypeStruct((8, BLK), F32), jax.ShapeDtypeStruct((4, BLK, BLK), F32),
             jax.ShapeDtypeStruct((4, BLK, 1), F32), jax.ShapeDtypeStruct((1, 512), F32),
             jax.ShapeDtypeStruct((4, BLK, BLK), F32), jax.ShapeDtypeStruct((1, 512), F32)]
    small_specs = [pl.BlockSpec((8, BLK), lambda i: (0, 0)), pl.BlockSpec((4, BLK, BLK), lambda i: (0, 0, 0)),
                   pl.BlockSpec((4, BLK, 1), lambda i: (0, 0, 0)), pl.BlockSpec((1, 512), lambda i: (0, 0)),
                   pl.BlockSpec((4, BLK, BLK), lambda i: (0, 0, 0)), pl.BlockSpec((1, 512), lambda i: (0, 0))]
    res = pl.pallas_call(
        body, name=name, grid=(nb,),
        in_specs=_mix_in_specs(nb, True) + [
            pl.BlockSpec((3, BLK, 512), lambda i: (0, nb - 1 - i, 0)),
            pl.BlockSpec(memory_space=pl.ANY)],
        out_specs=[pl.BlockSpec((BLK, C_GATE), lambda i: (nb - 1 - i, 0))] + small_specs,
        out_shape=[jax.ShapeDtypeStruct(dproj.shape, dproj.dtype)] + small,
        scratch_shapes=[pltpu.VMEM((HALO + BLK, 512), F32), pltpu.VMEM((2 * HALO + BLK, 512), F32),
                        pltpu.VMEM((BLK, 2 * BLK), F32), pltpu.VMEM((HALO, 512), F32)],
        input_output_aliases={n_in - 1: 0},
        compiler_params=_params(("arbitrary",)),
    )(proj, proj, proj, sinks_b, ws, bs3, gsgu, wp, ps, dbr, dproj)
    return tuple(res)


_GW = 256


def _merge_fwd(proj, pb, name, T=4096):
    S, D = pb.shape[1], pb.shape[2]
    T = min(T, S)

    def body(gate_ref, pb_ref, out_ref, acc_ref):
        n = pl.program_id(2)

        @pl.when(n == 0)
        def _():
            acc_ref[...] = jnp.zeros_like(acc_ref)

        acc_ref[...] += jax.nn.sigmoid(gate_ref[...]) * pb_ref[...]

        @pl.when(n == 2)
        def _():
            out_ref[...] = acc_ref[...].astype(BF16)

    return pl.pallas_call(
        body, name=name, grid=(S // T, D // _GW, 3),
        in_specs=[pl.BlockSpec((T, _GW), lambda i, j, n: (i, C_GATE // _GW + n * (D // _GW) + j)),
                  pl.BlockSpec((None, T, _GW), lambda i, j, n: (n, i, j))],
        out_specs=pl.BlockSpec((T, _GW), lambda i, j, n: (i, j)),
        out_shape=jax.ShapeDtypeStruct((S, D), BF16),
        scratch_shapes=[pltpu.VMEM((T, _GW), F32)],
        compiler_params=_params(("parallel", "parallel", "arbitrary")),
    )(proj, pb)


def _merge_bwd(proj, pb, dmerged, name, T=4096):
    S, D = pb.shape[1], pb.shape[2]
    T = min(T, S)

    def body(gate_ref, pb_ref, dm_ref, dgate_ref, dpb_ref):
        sg = jax.nn.sigmoid(gate_ref[...])
        dm = dm_ref[...]
        dpb_ref[...] = (dm * sg).astype(BF16)
        dgate_ref[...] = (dm * pb_ref[...] * sg * (1.0 - sg)).astype(BF16)

    gate_map = lambda i, n, j: (i, C_GATE // _GW + n * (D // _GW) + j)
    return pl.pallas_call(
        body, name=name, grid=(S // T, 3, D // _GW),
        in_specs=[pl.BlockSpec((T, _GW), gate_map),
                  pl.BlockSpec((None, T, _GW), lambda i, n, j: (n, i, j)),
                  pl.BlockSpec((T, _GW), lambda i, n, j: (i, j))],
        out_specs=[pl.BlockSpec((T, _GW), gate_map),
                   pl.BlockSpec((None, T, _GW), lambda i, n, j: (n, i, j))],
        out_shape=[jax.ShapeDtypeStruct((S, C_END), BF16), jax.ShapeDtypeStruct((3, S, D), BF16)],
        compiler_params=_params(("parallel", "parallel", "parallel")),
    )(proj, pb, dmerged)


def _memattn_fwd(qm, kv, name, T=512):
    S, NM = qm.shape[0], kv.shape[0]
    T = min(T, S)

    def body(q_ref, kv_ref, o_ref):
        for h in range(4):
            sl = slice(128 * h, 128 * (h + 1))
            k = kv_ref[:, sl].astype(BF16)
            v = kv_ref[:, 512 + 128 * h:512 + 128 * (h + 1)].astype(BF16)
            s = _dot(k, q_ref[:, sl].astype(BF16), "nt") * MEM_SCALE
            p = jax.nn.softmax(s, axis=0)
            o_ref[:, sl] = _dot(p.astype(BF16), v, "tn").astype(BF16)

    return pl.pallas_call(
        body, name=name, grid=(S // T,),
        in_specs=[_row_spec(T, 512), pl.BlockSpec((NM, 1024), lambda i: (0, 0))],
        out_specs=_row_spec(T, 512), out_shape=jax.ShapeDtypeStruct((S, 512), BF16),
        compiler_params=_params(("parallel",)))(qm, kv)


def _memattn_bwd(qm, kv, dom, name, T=512):
    S, NM = qm.shape[0], kv.shape[0]
    T = min(T, S)

    def body(q_ref, kv_ref, do_ref, dq_ref, dkv_ref):
        i = pl.program_id(0)

        @pl.when(i == 0)
        def _():
            dkv_ref[...] = jnp.zeros_like(dkv_ref)

        for h in range(4):
            sl = slice(128 * h, 128 * (h + 1))
            sv_ = slice(512 + 128 * h, 512 + 128 * (h + 1))
            q = q_ref[:, sl].astype(BF16)
            k = kv_ref[:, sl].astype(BF16)
            v = kv_ref[:, sv_].astype(BF16)
            do = do_ref[:, sl].astype(BF16)
            p = jax.nn.softmax(_dot(k, q, "nt") * MEM_SCALE, axis=0)
            dp = _dot(v, do, "nt")
            ds = (p * (dp - jnp.sum(p * dp, axis=0, keepdims=True)) * MEM_SCALE).astype(BF16)
            dq_ref[:, sl] = _dot(ds, k, "tn").astype(BF16)
            dkv_ref[:, sl] += _dot(ds, q, "nn")
            dkv_ref[:, sv_] += _dot(p.astype(BF16), do, "nn")

    return pl.pallas_call(
        body, name=name, grid=(S // T,),
        in_specs=[_row_spec(T, 512), pl.BlockSpec((NM, 1024), lambda i: (0, 0)), _row_spec(T, 512)],
        out_specs=[_row_spec(T, 512), pl.BlockSpec((NM, 1024), lambda i: (0, 0))],
        out_shape=[jax.ShapeDtypeStruct((S, 512), BF16), jax.ShapeDtypeStruct((NM, 1024), F32)],
        compiler_params=_params(("arbitrary",)))(qm, kv, dom)


def _adamw(w, g, m, v, name, rows=None, g_row0=None, into=None, TR=512):
    R, C = w.shape
    lo, hi = rows if rows is not None else (0, R)
    g0 = lo if g_row0 is None else g_row0
    TR = _row_tile(math.gcd(math.gcd(lo, g0), hi - lo), TR)
    c1 = 1.0 - ADAM_B1 ** ADAM_STEP
    c2 = 1.0 - ADAM_B2 ** ADAM_STEP

    def body(w_ref, g_ref, m_ref, v_ref, *rest):
        d_ref, nm_ref, nv_ref = rest[-3:]
        gv = g_ref[...]
        nm = ADAM_B1 * m_ref[...] + (1.0 - ADAM_B1) * gv
        nv = ADAM_B2 * v_ref[...] + (1.0 - ADAM_B2) * jnp.square(gv)
        d_ref[...] = -ADAM_LR * ((nm / c1) / (jnp.sqrt(nv / c2) + ADAM_EPS) + ADAM_WD * w_ref[...])
        nm_ref[...] = nm
        nv_ref[...] = nv

    spec = pl.BlockSpec((TR, C), lambda i: (lo // TR + i, 0))
    g_spec = pl.BlockSpec((TR, C), lambda i: (g0 // TR + i, 0))
    prior = list(into) if into is not None else []
    return pl.pallas_call(
        body, name=name, grid=((hi - lo) // TR,),
        in_specs=[spec, g_spec, spec, spec] + [pl.BlockSpec(memory_space=pl.ANY)] * len(prior), out_specs=[spec] * 3,
        out_shape=[jax.ShapeDtypeStruct((R, C), F32)] * 3,
        input_output_aliases={4 + k: k for k in range(len(prior))},
        compiler_params=_params(("parallel",)))(w, g, m, v, *prior)


def _row_tile(R, pref):
    t = (pref // 8) * 8
    while t >= 8:
        if R % t == 0:
            return t
        t -= 8
    raise ValueError(f"no row tile for {R}")


def _sum_slots(stack, name, TR=512):
    n, R, C = stack.shape
    TR = R if R <= TR else _row_tile(R, TR)

    def body(s_ref, o_ref):
        acc = s_ref[0]
        for k in range(1, n):
            acc = acc + s_ref[k]
        o_ref[...] = acc

    return pl.pallas_call(
        body, name=name, grid=(R // TR,),
        in_specs=[pl.BlockSpec((n, TR, C), lambda i: (0, i, 0))],
        out_specs=pl.BlockSpec((TR, C), lambda i: (i, 0)),
        out_shape=jax.ShapeDtypeStruct((R, C), F32),
        compiler_params=_params(("parallel",)))(stack)


_ANY = pl.BlockSpec(memory_space=pl.ANY)


def _chip_of(j, c):
    return (j // 2, j % 2, c)


def _own_slab(shard, dtype, j_arr, name, first=0, count=None, plus=None, deps=(), TR=512):
    N, r, C = shard.shape
    B = N if count is None else count
    rh = r // 2
    TR = rh if rh <= TR else _row_tile(rh, TR)
    nt = rh // TR
    ins = [shard] if plus is None else [shard, plus]

    def body(j_ref, *refs):
        val = refs[0][...] if plus is None else refs[0][...] + refs[1][...]
        refs[-1][...] = val.astype(refs[-1].dtype)

    return pl.pallas_call(
        body, name=name,
        grid_spec=pltpu.PrefetchScalarGridSpec(
            num_scalar_prefetch=1, grid=(B, 2, nt),
            in_specs=[pl.BlockSpec((None, TR, C), lambda b, h, t, jr: (first + b, h * nt + t, 0))] * len(ins)
            + [_ANY] * len(deps),
            out_specs=pl.BlockSpec((None, None, None, TR, C), lambda b, h, t, jr: (b, jr[0], h, t, 0))),
        out_shape=jax.ShapeDtypeStruct((B, 4, 2, rh, C), dtype),
        compiler_params=_params(("parallel", "parallel", "parallel")),
    )(j_arr, *ins, *deps)


def _gather_weights(bufs, name):
    n = len(bufs)

    def body(*refs):
        buf = refs[n:2 * n]
        send_sems, recv_sems, fsend_sems, frecv_sems = refs[2 * n:]
        x, y, c = lax.axis_index("x"), lax.axis_index("y"), lax.axis_index("c")
        j = 2 * x + y
        sib = (x, y, 1 - c)
        sends = []
        for d in range(1, 4):
            for a in range(n):
                cp = pltpu.make_async_remote_copy(
                    src_ref=buf[a].at[:, j, c], dst_ref=buf[a].at[:, j, c], send_sem=send_sems.at[a, d - 1],
                    recv_sem=recv_sems.at[a, d - 1], device_id=_chip_of((j + d) % 4, c), device_id_type=MESH)
                cp.start()
                sends.append(cp)
        for d in range(1, 4):
            frm = (j + 4 - d) % 4
            for a in range(n):
                pltpu.make_async_remote_copy(
                    src_ref=buf[a].at[:, frm, c], dst_ref=buf[a].at[:, frm, c], send_sem=send_sems.at[a, d - 1],
                    recv_sem=recv_sems.at[a, d - 1], device_id=_chip_of(frm, c), device_id_type=MESH).wait_recv()
                cp = pltpu.make_async_remote_copy(
                    src_ref=buf[a].at[:, frm, c], dst_ref=buf[a].at[:, frm, c], send_sem=fsend_sems.at[a, d - 1],
                    recv_sem=frecv_sems.at[a, d - 1], device_id=sib, device_id_type=MESH)
                cp.start()
                sends.append(cp)
        for d in range(1, 4):
            frm = (j + 4 - d) % 4
            for a in range(n):
                pltpu.make_async_remote_copy(
                    src_ref=buf[a].at[:, frm, 1 - c], dst_ref=buf[a].at[:, frm, 1 - c], send_sem=fsend_sems.at[a, d - 1],
                    recv_sem=frecv_sems.at[a, d - 1], device_id=sib, device_id_type=MESH).wait_recv()
        for cp in sends:
            cp.wait_send()

    return pl.pallas_call(
        body, name=name,
        in_specs=[_ANY] * n, out_specs=[_ANY] * n,
        out_shape=[jax.ShapeDtypeStruct(b.shape, b.dtype) for b in bufs],
        scratch_shapes=[pltpu.SemaphoreType.DMA((n, 3))] * 4,
        input_output_aliases={a: a for a in range(n)},
    )(*bufs)


_HBM = pl.BlockSpec(memory_space=pltpu.HBM)
_SEM = pl.BlockSpec(memory_space=pltpu.SEMAPHORE)
_DATAFLOW = pltpu.SideEffectType.DATAFLOW_SIDE_EFFECTING


def _in_hbm(arrays):
    return [pltpu.with_memory_space_constraint(a, pltpu.HBM) for a in arrays]


def _start_copies(bufs, plan, count, name, deps=()):
    n, k = len(bufs), len(deps)

    def body(*refs):
        send_sems, recv_sems = refs[n + k], refs[n + k + 1]
        for i, (src, dst, dev) in enumerate(plan(refs[:n], False)):
            pltpu.make_async_remote_copy(src_ref=src, dst_ref=dst, send_sem=send_sems.at[i], recv_sem=recv_sems.at[i],
                                         device_id=dev, device_id_type=MESH).start()
        refs[-1][...] = jnp.zeros_like(refs[-1])

    return pl.pallas_call(
        body, name=name,
        out_shape=(pltpu.SemaphoreType.DMA((count,)), pltpu.SemaphoreType.DMA((count,)),
                   *[pltpu.HBM(b.shape, b.dtype) for b in bufs], jax.ShapeDtypeStruct((8, 128), F32)),
        in_specs=[_HBM] * n + [_ANY] * k,
        out_specs=(_SEM, _SEM, *[_HBM] * n, pl.BlockSpec(memory_space=pltpu.VMEM)),
        input_output_aliases={a: 2 + a for a in range(n)},
        compiler_params=pltpu.CompilerParams(has_side_effects=_DATAFLOW),
    )(*_in_hbm(bufs), *deps)


def _wait_copies(handle, plan, afters, name):
    send_sems, recv_sems, *bufs = handle[:-1]
    n = len(bufs)

    def body(*refs):
        send_sems, recv_sems = refs[n], refs[n + 1]
        for i, (src, dst, dev) in enumerate(plan(refs[:n], True)):
            cp = pltpu.make_async_remote_copy(src_ref=src, dst_ref=dst, send_sem=send_sems.at[i], recv_sem=recv_sems.at[i],
                                              device_id=dev, device_id_type=MESH)
            cp.wait_send()
            cp.wait_recv()

    return list(pl.pallas_call(
        body, name=name,
        out_shape=[pltpu.HBM(b.shape, b.dtype) for b in bufs],
        in_specs=[_HBM] * n + [_SEM, _SEM] + [_ANY] * len(afters), out_specs=[_HBM] * n,
        input_output_aliases={a: a for a in range(n)},
        compiler_params=pltpu.CompilerParams(has_side_effects=_DATAFLOW),
    )(*bufs, send_sems, recv_sems, *afters))


def _gather_plan(buf, waiting):
    c = lax.axis_index("c")
    j = 2 * lax.axis_index("x") + lax.axis_index("y")
    copies = []
    for d in range(1, 4):
        to, frm = (j + d) % 4, (j + 4 - d) % 4
        for b in buf:
            copies.append((b.at[:, j, c], b.at[:, frm if waiting else j, c], _chip_of(frm if waiting else to, c)))
    return copies


def _chip_plan(buf, waiting):
    n = len(buf) // 2
    c = lax.axis_index("c")
    j = 2 * lax.axis_index("x") + lax.axis_index("y")
    copies = []
    for d in range(1, 4):
        to = (j + d) % 4
        for a in range(n):
            copies.append((buf[a].at[to], buf[n + a].at[d - 1], _chip_of(to, c)))
    return copies


def _pair_plan(buf, waiting):
    n = len(buf) // 2
    c = lax.axis_index("c")
    sib = (lax.axis_index("x"), lax.axis_index("y"), 1 - c)
    return [(buf[a].at[:, pl.ds(1 - c, 1)], buf[n + a], sib) for a in range(n)]


def _gather_forward(bufs, name):
    n = len(bufs)

    def body(*refs):
        buf = refs[n:2 * n]
        send_sems, recv_sems = refs[2 * n:]
        x, y, c = lax.axis_index("x"), lax.axis_index("y"), lax.axis_index("c")
        j = 2 * x + y
        sib = (x, y, 1 - c)
        sends = []
        for d in range(1, 4):
            frm = (j + 4 - d) % 4
            for a in range(n):
                cp = pltpu.make_async_remote_copy(
                    src_ref=buf[a].at[:, frm, c], dst_ref=buf[a].at[:, frm, c], send_sem=send_sems.at[a, d - 1],
                    recv_sem=recv_sems.at[a, d - 1], device_id=sib, device_id_type=MESH)
                cp.start()
                sends.append(cp)
        for d in range(1, 4):
            frm = (j + 4 - d) % 4
            for a in range(n):
                pltpu.make_async_remote_copy(
                    src_ref=buf[a].at[:, frm, 1 - c], dst_ref=buf[a].at[:, frm, 1 - c], send_sem=send_sems.at[a, d - 1],
                    recv_sem=recv_sems.at[a, d - 1], device_id=sib, device_id_type=MESH).wait_recv()
        for cp in sends:
            cp.wait_send()

    return pl.pallas_call(
        body, name=name,
        in_specs=[_ANY] * n, out_specs=[_ANY] * n,
        out_shape=[jax.ShapeDtypeStruct(b.shape, b.dtype) for b in bufs],
        scratch_shapes=[pltpu.SemaphoreType.DMA((n, 3))] * 2,
        input_output_aliases={a: a for a in range(n)},
    )(*bufs)


def _pair_add(g4, r1, cj_arr, name, TR=512):
    B4, _, rh, C = g4.shape
    B = B4 // 4
    TR = rh if rh <= TR else _row_tile(rh, TR)

    def body(cj_ref, g_ref, r_ref, o16_ref, own_ref):
        s = g_ref[...] + r_ref[...]
        o16_ref[...] = s.astype(BF16)

        @pl.when(pl.program_id(2) == cj_ref[1])
        def _():
            own_ref[...] = s

    return pl.pallas_call(
        body, name=name,
        grid_spec=pltpu.PrefetchScalarGridSpec(
            num_scalar_prefetch=1, grid=(B, rh // TR, 4),
            in_specs=[pl.BlockSpec((None, None, TR, C), lambda b, t, p, cj: (b * 4 + p, cj[0], t, 0)),
                      pl.BlockSpec((None, None, TR, C), lambda b, t, p, cj: (b * 4 + p, 0, t, 0))],
            out_specs=[pl.BlockSpec((None, None, TR, C), lambda b, t, p, cj: (p, b, t, 0)),
                       pl.BlockSpec((None, TR, C), lambda b, t, p, cj: (b, t, 0))]),
        out_shape=[jax.ShapeDtypeStruct((4, B, rh, C), BF16), jax.ShapeDtypeStruct((B, rh, C), F32)],
        compiler_params=_params(("parallel", "parallel", "arbitrary")),
    )(cj_arr, g4, r1)


def _chip_add(own, r2, cj_arr, into, first, name, TR=512):
    B, rh, C = own.shape
    TR = rh if rh <= TR else _row_tile(rh, TR)

    def body(cj_ref, p_ref, r_ref, _into_ref, o_ref):
        o_ref[...] = p_ref[...] + r_ref[0].astype(F32) + r_ref[1].astype(F32) + r_ref[2].astype(F32)

    return pl.pallas_call(
        body, name=name,
        grid_spec=pltpu.PrefetchScalarGridSpec(
            num_scalar_prefetch=1, grid=(B, rh // TR),
            in_specs=[pl.BlockSpec((None, TR, C), lambda b, t, cj: (b, t, 0)),
                      pl.BlockSpec((3, None, TR, C), lambda b, t, cj: (0, b, t, 0)),
                      _ANY],
            out_specs=pl.BlockSpec((None, None, TR, C), lambda b, t, cj: (first + b, cj[0], t, 0))),
        out_shape=jax.ShapeDtypeStruct(into.shape, F32),
        input_output_aliases={3: 0},
        compiler_params=_params(("parallel", "parallel")),
    )(cj_arr, own, r2, into)


def _pair_share(bufs, spans, name, deps=()):
    n = len(bufs)

    def body(*refs):
        buf = refs[n + len(deps):2 * n + len(deps)]
        send_sems, recv_sems = refs[2 * n + len(deps):]
        c = lax.axis_index("c")
        sib = (lax.axis_index("x"), lax.axis_index("y"), 1 - c)
        cps = []
        for a, (first, count) in enumerate(spans):
            cp = pltpu.make_async_remote_copy(
                src_ref=buf[a].at[pl.ds(first, count), c], dst_ref=buf[a].at[pl.ds(first, count), c],
                send_sem=send_sems.at[a], recv_sem=recv_sems.at[a], device_id=sib, device_id_type=MESH)
            cp.start()
            cps.append(cp)
        for a, (first, count) in enumerate(spans):
            pltpu.make_async_remote_copy(
                src_ref=buf[a].at[pl.ds(first, count), 1 - c], dst_ref=buf[a].at[pl.ds(first, count), 1 - c],
                send_sem=send_sems.at[a], recv_sem=recv_sems.at[a], device_id=sib, device_id_type=MESH).wait_recv()
        for cp in cps:
            cp.wait_send()

    return pl.pallas_call(
        body, name=name, in_specs=[_ANY] * (n + len(deps)), out_specs=[_ANY] * n,
        out_shape=[jax.ShapeDtypeStruct(b.shape, b.dtype) for b in bufs],
        scratch_shapes=[pltpu.SemaphoreType.DMA((n,)), pltpu.SemaphoreType.DMA((n,))],
        input_output_aliases={a: a for a in range(n)},
    )(*bufs, *deps)


def _pair_swap(arr, name):
    def body(src, dst, send_sem, recv_sem):
        sib = (lax.axis_index("x"), lax.axis_index("y"), 1 - lax.axis_index("c"))
        cp = pltpu.make_async_remote_copy(src_ref=src, dst_ref=dst, send_sem=send_sem, recv_sem=recv_sem,
                                          device_id=sib, device_id_type=MESH)
        cp.start()
        cp.wait_recv()
        cp.wait_send()

    return pl.pallas_call(
        body, name=name, in_specs=[_ANY], out_specs=_ANY,
        out_shape=jax.ShapeDtypeStruct(arr.shape, arr.dtype),
        scratch_shapes=[pltpu.SemaphoreType.DMA, pltpu.SemaphoreType.DMA],
    )(arr)


class _ReduceScatter:
    def __init__(self, n_layers, cj_arr):
        self.L, self.cj = n_layers, cj_arr
        self.total = None
        self.pair = None
        self.chip = None

    def _land(self, after):
        handle, layer, owns = self.chip
        n = len(owns)
        r2 = _wait_copies(handle, _chip_plan, after if isinstance(after, (list, tuple)) else (after,), "rs_chip_wait")[n:]
        if self.total is None:
            self.total = [lax.empty((self.L * o.shape[0], 2) + o.shape[1:], F32) for o in owns]
        self.total = [_chip_add(o, r, self.cj, t, layer * o.shape[0], "rs_chip_add")
                      for o, r, t in zip(owns, r2, self.total)]
        self.chip = None

    def add_layer(self, layer, grads):
        g4 = [g.reshape(g.shape[0] * 4, 2, g.shape[1] // 8, g.shape[2]) for g in grads]
        lands = [lax.empty((g.shape[0], 1) + g.shape[2:], F32) for g in g4]
        handle = _start_copies(g4 + lands, _pair_plan, len(g4), "rs_pair_start")
        self.pair = (handle, layer)
        return (handle[-1],)

    def advance(self, after):
        if self.pair is None:
            return ()
        handle, layer = self.pair
        both = _wait_copies(handle, _pair_plan, (after,), "rs_pair_wait")
        n = len(both) // 2
        added = [_pair_add(g, r, self.cj, "rs_pair_add") for g, r in zip(both[:n], both[n:])]
        parts, owns = [p for p, _ in added], [o for _, o in added]
        lands = [lax.empty((3,) + p.shape[1:], p.dtype) for p in parts]
        handle = _start_copies(parts + lands, _chip_plan, 3 * n, "rs_chip_start")
        if self.chip is not None:
            self._land(handle[-1])
        self.pair, self.chip = None, (handle, layer, owns)
        return (handle[-1],)

    def upper_layers(self, deps):
        per = [t.shape[0] // self.L for t in self.total]
        self.total = _pair_share(self.total, [(b, (self.L - 1) * b) for b in per], "rs_pair_share_upper", deps=deps)
        return [t.reshape(t.shape[0], t.shape[1] * t.shape[2], t.shape[3]) for t in self.total]

    def first_layer(self, after):
        self._land(after)
        per = [t.shape[0] // self.L for t in self.total]
        full = _pair_share(self.total, [(0, b) for b in per], "rs_pair_share_first")
        return [f.reshape(f.shape[0], f.shape[1] * f.shape[2], f.shape[3]) for f in full]


def _relu2_epi(acc):
    return acc, jnp.square(jnp.maximum(acc, 0.0))


def _relu2_bwd_epi(acc, u):
    return (acc * (2.0 * jnp.maximum(u, 0.0)),)


_GRAD_ORDER = ("winT", "wbT", "wout", "wq", "wkv", "woT", "wupT", "wdown")
_DW = dict(tm=512, tn=1024, tk=4096)
_LONG_K = dict(tm=1024, tn=1024, tk=2048)


def _forward_backward(x, mem, target, input_weight_of, weights_of, P, grads_done, grads_advance):
    L = P["g_norm"].shape[0]
    S, D = x.shape
    gn = lambda l, i: P["g_norm"][l, i][None]

    saved = []
    (h,) = _resnorm_fwd(x, None, None, gn(0, 0), "norm_in")
    xr = x
    for l in range(L):
        w_in_t = input_weight_of(l, xr)
        proj = _mm(h, w_in_t, "nt", "in_proj", b_pre=(0,), tn=1792)
        W, w_deps = weights_of(l, proj)
        small = (jnp.broadcast_to(P["sinks"][l][:, None], (8, BLK)), P["ws"][l], P["bs"][l][:, :, None],
                 P["gsgu"][l][None], P["wp"][l], P["ps"][l][None])
        br = _mix_fwd(proj, *small, "mix_fwd")
        pb = lax.empty((3, S, D), BF16)
        for n in range(3):
            pb = _mm(br, W["wbT"], "nt", "branch_proj", a_pre=(n,), b_pre=(n,), into=pb, out_pre=(n,),
                     deps=w_deps if n == 0 else ())
        merged = _merge_fwd(proj, pb, "merge_fwd")
        z = _mm(merged, W["wout"], "nn", "out_proj", b_pre=(0,))
        x1, hm = _resnorm_fwd(xr, z, gn(l, 1), gn(l, 2), "resnorm_fwd")
        qm = _mm(hm, W["wq"], "nn", "mem_q", b_pre=(0,))
        (memn,) = _resnorm_fwd(mem, None, None, P["g_mem"][l][None], "mem_norm")
        kv = _mm(memn, W["wkv"], "nn", "mem_kv", b_pre=(0,))
        om = _memattn_fwd(qm, kv, "memattn_fwd")
        ym = _mm(om, W["woT"], "nt", "mem_o", b_pre=(0,))
        x2, hf = _resnorm_fwd(x1, ym, gn(l, 3), gn(l, 4), "resnorm_fwd")
        u, a = _mm(hf, W["wupT"], "nt", "mlp_up", b_pre=(0,), out_dtypes=(F32, BF16), epi=_relu2_epi)
        yf = _mm(a, W["wdown"], "nn", "mlp_down", b_pre=(0,), **_LONG_K)
        saved.append(dict(W=W, x0=xr, h=h, proj=proj, small=small, br=br, pb=pb, merged=merged, z=z, x1=x1, hm=hm,
                          qm=qm, memn=memn, kv=kv, om=om, ym=ym, x2=x2, hf=hf, u=u, a=a, yf=yf))
        if l < L - 1:
            xr, h = _resnorm_fwd(x2, yf, gn(l, 5), gn(l + 1, 0), "resnorm_fwd")
    dres, loss = _final_fwd(saved[-1]["x2"], saved[-1]["yf"], gn(L - 1, 5), target, "loss_head")

    dgn = [[None] * 6 for _ in range(L)]
    dsmall = {k: [None] * L for k in ("g_mem", "sinks", "ws", "bs", "gsgu", "wp", "ps")}
    dh = None
    for l in reversed(range(L)):
        s = saved[l]
        W, G = s["W"], {}
        if l == L - 1:
            dx2, dyf, dgn[l][5] = _resnorm_bwd(s["x2"], s["yf"], gn(l, 5), None, dres, None, "resnorm_bwd_top")
        else:
            dx2, dyf, dgn[l][5], dgn[l + 1][0] = _resnorm_bwd(s["x2"], s["yf"], gn(l, 5), gn(l + 1, 0), dres, dh,
                                                              "resnorm_bwd", deps=deps)
        du = _mm(dyf, W["wdown"], "nt", "mlp_down_dx", b_pre=(0,), out_dtypes=(BF16,), extras=(s["u"],), epi=_relu2_bwd_epi)
        G["wdown"] = _mm(s["a"], dyf, "tn", "mlp_down_dw", **_DW)[None]
        dhf = _mm(du, W["wupT"], "nn", "mlp_up_dx", b_pre=(0,), **_LONG_K)
        G["wupT"] = _mm(du, s["hf"], "tn", "mlp_up_dw", **_DW)[None]
        dx1, dym, dgn[l][3], dgn[l][4] = _resnorm_bwd(s["x1"], s["ym"], gn(l, 3), gn(l, 4), dx2, dhf, "resnorm_bwd")
        dom = _mm(dym, W["woT"], "nn", "mem_o_dx", b_pre=(0,), deps=grads_advance(dx1))
        G["woT"] = _mm(dym, s["om"], "tn", "mem_o_dw", **_DW)[None]
        dqm, dkv = _memattn_bwd(s["qm"], s["kv"], dom, "memattn_bwd")
        dmemn = _mm(dkv, W["wkv"], "nt", "mem_kv_dx", b_pre=(0,))
        G["wkv"] = _mm(s["memn"], dkv, "tn", "mem_kv_dw")[None]
        _, dsmall["g_mem"][l] = _resnorm_bwd(mem, None, None, P["g_mem"][l][None], None, dmemn, "mem_norm_bwd")
        dhm = _mm(dqm, W["wq"], "nt", "mem_q_dx", b_pre=(0,))
        G["wq"] = _mm(s["hm"], dqm, "tn", "mem_q_dw", **_DW)[None]
        dx0, dz, dgn[l][1], dgn[l][2] = _resnorm_bwd(s["x0"], s["z"], gn(l, 1), gn(l, 2), dx1, dhm, "resnorm_bwd")
        dmerged = _mm(dz, W["wout"], "nt", "out_proj_dx", b_pre=(0,))
        G["wout"] = _mm(s["merged"], dz, "tn", "out_proj_dw", **_DW)[None]
        dproj, dpb = _merge_bwd(s["proj"], s["pb"], dmerged, "merge_bwd")
        dbr = lax.empty((3, S, 512), F32)
        G["wbT"] = lax.empty(W["wbT"].shape, F32)
        for n in range(3):
            dbr = _mm(dpb, W["wbT"], "nn", "branch_proj_dx", a_pre=(n,), b_pre=(n,), into=dbr, out_pre=(n,))
            G["wbT"] = _mm(dpb, s["br"], "tn", "branch_proj_dw", a_pre=(n,), b_pre=(n,), into=G["wbT"], out_pre=(n,), **_DW)
        (dproj, dsmall["sinks"][l], dsmall["ws"][l], dsmall["bs"][l], dsmall["gsgu"][l], dsmall["wp"][l],
         dsmall["ps"][l]) = _mix_bwd(s["proj"], dbr, dproj, *s["small"], "mix_bwd")
        dh = _mm(dproj, W["winT"], "nn", "in_proj_dx", b_pre=(0,), **_LONG_K)
        G["winT"] = _mm(dproj, s["h"], "tn", "in_proj_dw", **_DW)[None]
        deps = grads_done(l, [G[k] for k in _GRAD_ORDER])
        dres = dx0
    grad_x, dgn[0][0] = _resnorm_bwd(x, None, None, gn(0, 0), dres, dh, "norm_in_bwd", deps=deps)
    tail_deps = grads_advance(grad_x)

    small_grads = dict(
        g_norm=jnp.stack([jnp.concatenate(row, axis=0) for row in dgn]),
        g_mem=jnp.concatenate(dsmall["g_mem"], axis=0),
        sinks=jnp.stack([d[:, 0] for d in dsmall["sinks"]]),
        ws=jnp.stack(dsmall["ws"]),
        bs=jnp.stack([d[:, :, 0] for d in dsmall["bs"]]),
        gsgu=jnp.concatenate(dsmall["gsgu"], axis=0),
        wp=jnp.stack(dsmall["wp"]),
        ps=jnp.concatenate(dsmall["ps"], axis=0),
    )
    return loss, grad_x, small_grads, tail_deps


_PACK_ROWS = 512


def _as_rows(a):
    n = math.prod(a.shape)
    if n % 128:
        a = jnp.pad(a.reshape(-1), (0, (-n) % 128))
    r = a.reshape(-1, 128)
    return jnp.pad(r, ((0, (-r.shape[0]) % 8), (0, 0))) if r.shape[0] % 8 else r


def _pack(arrays):
    rows = [_as_rows(a) for a in arrays]
    total = sum(r.shape[0] for r in rows)
    tail = (-total) % _PACK_ROWS
    if tail:
        rows.append(jnp.zeros((tail, 128), rows[0].dtype))
    return jnp.concatenate(rows, axis=0)


def _unpack(packed, like):
    out, pos = [], 0
    for a in like:
        n = math.prod(a.shape)
        nr = -(-n // 128)
        rows = packed[pos:pos + nr]
        out.append((rows.reshape(-1)[:n] if n % 128 else rows).reshape(a.shape))
        pos += nr + (-nr) % 8
    return out


_BIG = ("w_in", "w_branch", "w_out", "w_q_mem", "w_kv_mem", "w_o_mem", "w_up", "w_down")
_SMALL = ("g_norm", "g_mem", "attn_sinks", "w_spatial", "b_spatial", "g_sgu", "w_pool", "pool_scale")
_WEIGHTS = ("g_norm", "g_mem", "w_in", "attn_sinks", "w_spatial", "b_spatial", "g_sgu", "w_pool", "pool_scale",
            "w_branch", "w_out", "w_q_mem", "w_kv_mem", "w_o_mem", "w_up", "w_down")


def _to_working(name, w):
    if name == "w_in":
        return jnp.swapaxes(w, 1, 2)
    if name == "w_branch":
        t = jnp.swapaxes(w, 2, 3)
        return t.reshape(t.shape[0] * 3, t.shape[2], t.shape[3])
    if name in ("w_o_mem", "w_up"):
        return jnp.swapaxes(w, 1, 2)
    return w


def _from_working(name, g):
    if name == "w_in":
        return jnp.swapaxes(g, 1, 2)
    if name == "w_branch":
        return jnp.swapaxes(g.reshape(g.shape[0] // 3, 3, g.shape[1], g.shape[2]), 2, 3)
    if name in ("w_o_mem", "w_up"):
        return jnp.swapaxes(g, 1, 2)
    return g


_WKEY = dict(w_in="winT", w_branch="wbT", w_out="wout", w_q_mem="wq", w_kv_mem="wkv", w_o_mem="woT",
             w_up="wupT", w_down="wdown")


def kernel(x, mem, g_norm, g_mem, w_in, attn_sinks, w_spatial, b_spatial, g_sgu, w_pool, pool_scale, w_branch, w_out, w_q_mem, w_kv_mem, w_o_mem, w_up, w_down, loss_target, m_g_norm, m_g_mem, m_w_in, m_attn_sinks, m_w_spatial, m_b_spatial, m_g_sgu, m_w_pool, m_pool_scale, m_w_branch, m_w_out, m_w_q_mem, m_w_kv_mem, m_w_o_mem, m_w_up, m_w_down, v_g_norm, v_g_mem, v_w_in, v_attn_sinks, v_w_spatial, v_b_spatial, v_g_sgu, v_w_pool, v_pool_scale, v_w_branch, v_w_out, v_w_q_mem, v_w_kv_mem, v_w_o_mem, v_w_up, v_w_down):
    w = dict(g_norm=g_norm, g_mem=g_mem, w_in=w_in, attn_sinks=attn_sinks, w_spatial=w_spatial, b_spatial=b_spatial,
             g_sgu=g_sgu, w_pool=w_pool, pool_scale=pool_scale, w_branch=w_branch, w_out=w_out, w_q_mem=w_q_mem,
             w_kv_mem=w_kv_mem, w_o_mem=w_o_mem, w_up=w_up, w_down=w_down)
    m = dict(g_norm=m_g_norm, g_mem=m_g_mem, w_in=m_w_in, attn_sinks=m_attn_sinks, w_spatial=m_w_spatial,
             b_spatial=m_b_spatial, g_sgu=m_g_sgu, w_pool=m_w_pool, pool_scale=m_pool_scale, w_branch=m_w_branch,
             w_out=m_w_out, w_q_mem=m_w_q_mem, w_kv_mem=m_w_kv_mem, w_o_mem=m_w_o_mem, w_up=m_w_up, w_down=m_w_down)
    v = dict(g_norm=v_g_norm, g_mem=v_g_mem, w_in=v_w_in, attn_sinks=v_attn_sinks, w_spatial=v_w_spatial,
             b_spatial=v_b_spatial, g_sgu=v_g_sgu, w_pool=v_w_pool, pool_scale=v_pool_scale, w_branch=v_w_branch,
             w_out=v_w_out, w_q_mem=v_w_q_mem, w_kv_mem=v_w_kv_mem, w_o_mem=v_w_o_mem, w_up=v_w_up, w_down=v_w_down)
    L = g_norm.shape[0]
    j = 2 * lax.axis_index("x") + lax.axis_index("y")
    c = lax.axis_index("c")
    j_arr = jnp.reshape(j, (1,)).astype(jnp.int32)
    cj_arr = jnp.stack([c, j]).astype(jnp.int32)

    gs = g_norm.shape[2]
    working = [_to_working(n, w[n]) for n in _BIG]
    per_layer = [wk.shape[0] // L for wk in working]

    def own_slabs(l):
        return [_own_slab(wk, BF16, j_arr, "own_slab", first=l * b, count=b) for wk, b in zip(working, per_layer)]

    first_slabs = own_slabs(0)
    lead = [first_slabs[0], _own_slab(g_norm.reshape(1, L * 6 * gs // 128, 128), F32, j_arr, "own_slab_norm")]
    lead_handle = _start_copies(lead, _gather_plan, 3 * len(lead), "gather_start_lead")
    rest_handle = _start_copies(first_slabs[1:], _gather_plan, 3 * (len(first_slabs) - 1), "gather_start_first",
                                deps=(lead_handle[-1],))
    slabs = {l: own_slabs(l) for l in range(1, L)}
    lead = _gather_forward(_wait_copies(lead_handle, _gather_plan,
                                        [rest_handle[-1]] + [s for l in slabs for s in slabs[l]],
                                        "gather_wait_lead"), "gather_forward_lead")
    gn_full = jnp.transpose(lead[1].reshape(4, L * 6, gs), (1, 0, 2)).reshape(L, 6, 4 * gs)
    P = dict(g_norm=gn_full, g_mem=g_mem, sinks=attn_sinks, ws=w_spatial, bs=b_spatial, gsgu=g_sgu, wp=w_pool,
             ps=pool_scale)
    whole = lambda g: g.reshape(g.shape[0], 8 * g.shape[3], g.shape[4])
    gathered, in_flight = {}, {}

    def layer_weights(l, after):
        if l not in gathered:
            gathered[l] = _gather_forward(_wait_copies(in_flight[l], _gather_plan, (after,), "gather_wait"),
                                          "gather_forward")
        return gathered[l]

    def input_weight_of(l, after):
        return whole(lead[0] if l == 0 else layer_weights(l, after)[0])

    def weights_of(l, after):
        deps = ()
        if l == 0:
            rest = _gather_forward(_wait_copies(rest_handle, _gather_plan, (after,), "gather_wait_first"),
                                   "gather_forward_first")
            gathered[0] = [lead[0], *rest]
            dep = rest[0]
            for k in range(1, L):
                in_flight[k] = _start_copies(slabs[k], _gather_plan, 3 * len(slabs[k]), "gather_start", deps=(dep,))
                dep = in_flight[k][-1]
            deps = tuple(h[-1] for h in in_flight.values())
        return {k: whole(g) for k, g in zip(_GRAD_ORDER, layer_weights(l, after))}, deps

    rs = _ReduceScatter(L, cj_arr)
    loss_part, grad_x, sg, tail_deps = _forward_backward(
        x[0], mem[0], loss_target[0], input_weight_of, weights_of, P, rs.add_layer, rs.advance)
    loss = lax.psum(loss_part[0, 0], ("x", "y", "c"))

    transposed = ("w_branch", "w_o_mem", "w_up")
    view = {n: (lambda t: jnp.swapaxes(t, 1, 2)) if n == "w_in" else (lambda t: t) for n in _BIG}
    rows = lambda n, t: view[n](t).reshape(-1, view[n](t).shape[-1])
    state = {n: (rows(n, w[n]), rows(n, m[n]), rows(n, v[n])) for n in _BIG}
    updated, upper_grad = {}, {}
    if L > 1:
        for n, gw in zip(_BIG, rs.upper_layers(tail_deps)):
            per = gw.shape[0] // L
            if n in transposed:
                upper_grad[n] = _from_working(n, gw[per:])
                g2d, g_row0 = upper_grad[n].reshape(-1, upper_grad[n].shape[-1]), 0
            else:
                g2d, g_row0 = gw.reshape(-1, gw.shape[-1]), None
            r_layer = state[n][0].shape[0] // L
            updated[n] = _adamw(state[n][0], g2d, state[n][1], state[n][2], "adamw_upper",
                                rows=(r_layer, L * r_layer), g_row0=g_row0)
    full_small = [sg["g_norm"], sg["g_mem"], sg["sinks"], sg["ws"], sg["bs"], sg["gsgu"], sg["wp"], sg["ps"]]
    packed = _pack(full_small)
    pair_sum = _own_slab(packed[None], F32, j_arr, "small_grads_pair_sum",
                         deps=tuple(tail_deps) + tuple(u[0] for u in updated.values()),
                         plus=_pair_swap(packed, "small_grads_swap")[None])
    (chip_sums,) = _gather_weights([pair_sum], "gather_small_grads")
    total = _sum_slots(chip_sums.reshape(4, *packed.shape), "sum_small_grads")
    grads = {}
    for n, g in zip(_SMALL, _unpack(total, full_small)):
        grads[n] = lax.dynamic_slice_in_dim(g, j * g_norm.shape[2], g_norm.shape[2], axis=2) if n == "g_norm" else g

    after = [total] + [u[0] for u in updated.values()]
    delta, new_m, new_v = {}, {}, {}
    for n, gw in zip(_BIG, rs.first_layer(after)):
        per = gw.shape[0] // L
        if n in transposed:
            g0 = _from_working(n, gw[:per])
            grads[n] = jnp.concatenate([g0, upper_grad[n]], axis=0) if L > 1 else g0
            g2d = g0.reshape(-1, g0.shape[-1])
        else:
            grads[n] = _from_working(n, gw)
            g2d = gw.reshape(-1, gw.shape[-1])
        r_layer = state[n][0].shape[0] // L
        d_, m_, v_ = _adamw(state[n][0], g2d, state[n][1], state[n][2], "adamw_first", rows=(0, r_layer), g_row0=0,
                            into=updated.get(n))
        shp = view[n](w[n]).shape
        delta[n], new_m[n], new_v[n] = view[n](d_.reshape(shp)), view[n](m_.reshape(shp)), view[n](v_.reshape(shp))
    small_w = [w[n] for n in _SMALL]
    d_, m_, v_ = _adamw(_pack(small_w), _pack([grads[n] for n in _SMALL]), _pack([m[n] for n in _SMALL]),
                        _pack([v[n] for n in _SMALL]), "adamw_small")
    for n, dd, mm_, vv in zip(_SMALL, _unpack(d_, small_w), _unpack(m_, small_w), _unpack(v_, small_w)):
        delta[n], new_m[n], new_v[n] = dd, mm_, vv

    return (loss, grad_x[None], *[grads[n] for n in _WEIGHTS], *[delta[n] for n in _WEIGHTS],
            *[new_m[n] for n in _WEIGHTS], *[new_v[n] for n in _WEIGHTS])
```

```python
import functools
import math

import jax
import jax.numpy as jnp
from jax import lax
from jax.experimental import pallas as pl
from jax.experimental.pallas import tpu as pltpu

F32 = jnp.float32
BF16 = jnp.bfloat16
MESH = pl.DeviceIdType.MESH

EPS = 1e-6
NEG_INF = -1e30
BLK = 128
HALO = 16
POOL_WINDOWS = (2, 4, 8, 16)
ATT_SCALE = 1.0 / math.sqrt(64.0)
MEM_SCALE = 1.0 / math.sqrt(128.0)
C_Q, C_K, C_V, C_SU, C_SV, C_PC, C_GATE, C_END = 0, 512, 640, 768, 1280, 1792, 2304, 5376

ADAM_LR, ADAM_B1, ADAM_B2, ADAM_EPS, ADAM_WD, ADAM_STEP = 0.001, 0.9, 0.999, 1e-08, 0.01, 10

VMEM_LIMIT_BYTES = 56 * 1024 * 1024

_DIMS = {
    "nn": (((1,), (0,)), ((), ())),
    "nt": (((1,), (1,)), ((), ())),
    "tn": (((0,), (0,)), ((), ())),
}


def _dot(a, b, mode):
    return lax.dot_general(a, b, _DIMS[mode], preferred_element_type=F32)


def _params(semantics):
    return pltpu.CompilerParams(dimension_semantics=semantics, vmem_limit_bytes=VMEM_LIMIT_BYTES)


def _tile(dim, pref):
    if dim <= pref:
        return dim
    t = (pref // 128) * 128
    while t >= 128:
        if dim % t == 0:
            return t
        t -= 128
    raise ValueError(f"no tile for {dim}")


def _rms(x, g):
    return x * lax.rsqrt(jnp.mean(x * x, axis=-1, keepdims=True) + EPS) * g


def _mm(a, b, mode, name, *, out_dtypes=(F32,), a_pre=(), b_pre=(), into=None, out_pre=(),
        extras=(), epi=None, deps=(), tm=2048, tn=1024, tk=1024):
    a2, b2 = a.shape[len(a_pre):], b.shape[len(b_pre):]
    if mode == "nn":
        (M, K), (K2, N) = a2, b2
    elif mode == "nt":
        (M, K), (N, K2) = a2, b2
    else:
        (K, M), (K2, N) = a2, b2
    assert K == K2, (a.shape, b.shape, mode)
    tm, tn, tk = _tile(M, tm), _tile(N, tn), _tile(K, tk)
    nk = K // tk
    na, nb_, no = len(a_pre), len(b_pre), len(out_pre)
    if mode == "tn":
        a_spec = pl.BlockSpec((None,) * na + (tk, tm), lambda i, j, k: a_pre + (k, i))
    else:
        a_spec = pl.BlockSpec((None,) * na + (tm, tk), lambda i, j, k: a_pre + (i, k))
    if mode == "nt":
        b_spec = pl.BlockSpec((None,) * nb_ + (tn, tk), lambda i, j, k: b_pre + (j, k))
    else:
        b_spec = pl.BlockSpec((None,) * nb_ + (tk, tn), lambda i, j, k: b_pre + (k, j))
    tile_spec = pl.BlockSpec((tm, tn), lambda i, j, k: (i, j))
    ne, nout = len(extras), len(out_dtypes)
    in_specs = [a_spec, b_spec] + [tile_spec] * ne
    operands = [a, b, *extras]
    aliases = {}
    if into is not None:
        assert nout == 1
        in_specs.append(pl.BlockSpec(memory_space=pl.ANY))
        operands.append(into)
        aliases = {len(operands) - 1: 0}
        out_shape = [jax.ShapeDtypeStruct(into.shape, into.dtype)]
        out_specs = [pl.BlockSpec((None,) * no + (tm, tn), lambda i, j, k: out_pre + (i, j))]
    else:
        out_shape = [jax.ShapeDtypeStruct((M, N), dt) for dt in out_dtypes]
        out_specs = [tile_spec] * nout
    in_specs += [pl.BlockSpec(memory_space=pl.ANY)] * len(deps)
    operands += list(deps)

    def body(*refs):
        a_ref, b_ref = refs[0], refs[1]
        ex = refs[2:2 + ne]
        pos = 2 + ne + (1 if into is not None else 0) + len(deps)
        outs = refs[pos:pos + nout]
        acc_ref = refs[pos + nout] if nk > 1 else None

        def finish(acc):
            vals = epi(acc, *[e[...] for e in ex]) if epi is not None else (acc,)
            for o, v in zip(outs, vals):
                o[...] = v.astype(o.dtype)

        def prod():
            return _dot(a_ref[...].astype(BF16), b_ref[...].astype(BF16), mode)

        if nk == 1:
            finish(prod())
        else:
            k = pl.program_id(2)

            @pl.when(k == 0)
            def _():
                acc_ref[...] = jnp.zeros_like(acc_ref)

            acc_ref[...] += prod()

            @pl.when(k == nk - 1)
            def _():
                finish(acc_ref[...])

    res = pl.pallas_call(
        body, name=name, grid=(M // tm, N // tn, nk),
        in_specs=in_specs, out_specs=out_specs, out_shape=out_shape,
        scratch_shapes=[pltpu.VMEM((tm, tn), F32)] if nk > 1 else [],
        input_output_aliases=aliases,
        compiler_params=_params(("parallel", "parallel", "arbitrary")),
    )(*operands)
    return res[0] if nout == 1 else tuple(res)


def _resnorm_fn(has_post, has_pre):
    def f(*a):
        x, k = a[0], 1
        if has_post:
            x, k = x + _rms(a[1], a[2]), 3
        outs = [x]
        if has_pre:
            outs.append(_rms(x, a[k]))
        return tuple(outs)
    return f


def _row_spec(T, W):
    return pl.BlockSpec((T, W), lambda i: (i, 0))


def _par_spec(W):
    return pl.BlockSpec((1, W), lambda i: (0, 0))


def _resnorm_fwd(xr, y, gp, gq, name, T=512, deps=()):
    S, D = xr.shape
    T = min(T, S)
    has_post, has_pre = y is not None, gq is not None
    f = _resnorm_fn(has_post, has_pre)
    ins = [xr] + ([y, gp] if has_post else []) + ([gq] if has_pre else [])
    in_specs = [_row_spec(T, D)] + ([_row_spec(T, D), _par_spec(D)] if has_post else []) + ([_par_spec(D)] if has_pre else [])
    out_shape, out_specs = [], []
    if has_post:
        out_shape.append(jax.ShapeDtypeStruct((S, D), F32)); out_specs.append(_row_spec(T, D))
    if has_pre:
        out_shape.append(jax.ShapeDtypeStruct((S, D), BF16)); out_specs.append(_row_spec(T, D))
    n_in, n_dep = len(ins), len(deps)

    def body(*refs):
        vals = f(*[r[...] for r in refs[:n_in]])
        outs = list(refs[n_in + n_dep:])
        if has_post:
            outs.pop(0)[...] = vals[0]
        if has_pre:
            outs.pop(0)[...] = vals[1].astype(BF16)

    res = pl.pallas_call(body, name=name, grid=(S // T,),
                         in_specs=in_specs + [pl.BlockSpec(memory_space=pl.ANY)] * n_dep, out_specs=out_specs,
                         out_shape=out_shape, compiler_params=_params(("parallel",)))(*ins, *deps)
    return tuple(res)


def _resnorm_bwd(xr, y, gp, gq, dres, dh, name, T=512, deps=()):
    S, D = xr.shape
    T = min(T, S)
    has_post, has_pre, has_res = y is not None, gq is not None, dres is not None
    f = _resnorm_fn(has_post, has_pre)
    ins = [xr] + ([y, gp] if has_post else []) + ([gq] if has_pre else [])
    in_specs = [_row_spec(T, D)] + ([_row_spec(T, D), _par_spec(D)] if has_post else []) + ([_par_spec(D)] if has_pre else [])
    n_prim = len(ins)
    if has_res:
        ins.append(dres); in_specs.append(_row_spec(T, D))
    if has_pre:
        ins.append(dh); in_specs.append(_row_spec(T, D))
    n_in, n_dep = len(ins), len(deps)
    out_shape = [jax.ShapeDtypeStruct((S, D), F32)]
    out_specs = [_row_spec(T, D)]
    if has_post:
        out_shape += [jax.ShapeDtypeStruct((S, D), BF16), jax.ShapeDtypeStruct((1, D), F32)]
        out_specs += [_row_spec(T, D), _par_spec(D)]
    if has_pre:
        out_shape.append(jax.ShapeDtypeStruct((1, D), F32)); out_specs.append(_par_spec(D))

    def body(*refs):
        i = pl.program_id(0)
        prim = [r[...] for r in refs[:n_prim]]
        rest = list(refs[n_prim:n_in])
        ct_x = rest.pop(0)[...] if has_res else jnp.zeros((T, D), F32)
        cts = [ct_x]
        if has_pre:
            cts.append(rest.pop(0)[...].astype(F32))
        _, vjp = jax.vjp(f, *prim)
        grads = list(vjp(tuple(cts)))
        outs = list(refs[n_in + n_dep:])
        outs.pop(0)[...] = grads.pop(0)
        acc = []
        if has_post:
            outs.pop(0)[...] = grads.pop(0).astype(BF16)
            acc.append((outs.pop(0), grads.pop(0)))
        if has_pre:
            acc.append((outs.pop(0), grads.pop(0)))

        @pl.when(i == 0)
        def _():
            for o, _g in acc:
                o[...] = jnp.zeros_like(o)

        for o, g in acc:
            o[...] += g

    res = pl.pallas_call(body, name=name, grid=(S // T,),
                         in_specs=in_specs + [pl.BlockSpec(memory_space=pl.ANY)] * n_dep, out_specs=out_specs,
                         out_shape=out_shape, compiler_params=_params(("arbitrary",)))(*ins, *deps)
    return tuple(res)


def _final_fwd(xr, y, gp, target, name, T=512):
    S, D = xr.shape
    T = min(T, S)

    def body(x_ref, y_ref, g_ref, t_ref, dy_ref, loss_ref):
        i = pl.program_id(0)
        e = x_ref[...] + _rms(y_ref[...], g_ref[...]) - t_ref[...]
        dy_ref[...] = e / D

        @pl.when(i == 0)
        def _():
            loss_ref[...] = jnp.zeros_like(loss_ref)

        loss_ref[...] += 0.5 * jnp.sum(jnp.sum(e * e, axis=-1, keepdims=True) / D, axis=0, keepdims=True)

    return pl.pallas_call(
        body, name=name, grid=(S // T,),
        in_specs=[_row_spec(T, D), _row_spec(T, D), _par_spec(D), _row_spec(T, D)],
        out_specs=[_row_spec(T, D), pl.BlockSpec((1, 128), lambda i: (0, 0))],
        out_shape=[jax.ShapeDtypeStruct((S, D), F32), jax.ShapeDtypeStruct((1, 128), F32)],
        compiler_params=_params(("arbitrary",)))(xr, y, gp, target)


_STRAIGHT_HEADS = (0, 2, 5, 7)
_ROLLED_HEADS = (1, 3, 4, 6)


def _straight_lanes():
    r = lax.broadcasted_iota(jnp.int32, (4 * BLK, BLK), 0)
    c = lax.broadcasted_iota(jnp.int32, (4 * BLK, BLK), 1)
    return (r < 2 * BLK) == (c < 64)


def _att_mask(not_first):
    k = lax.broadcasted_iota(jnp.int32, (2 * BLK, 4 * BLK), 0)
    q = lax.broadcasted_iota(jnp.int32, (2 * BLK, 4 * BLK), 1) % BLK
    qc, kc = 2 + q // 64, k // 64
    return (kc <= qc) & (kc >= qc - 2) & (not_first | (k >= BLK))


def _sink_row(sk_ref, heads):
    return jnp.concatenate([sk_ref[h:h + 1, :] for h in heads], axis=1)


def _softmax_sink(s, sk):
    m = jnp.maximum(jnp.max(s, axis=0, keepdims=True), sk)
    e = jnp.exp(s - m)
    es = jnp.exp(sk - m)
    z = jnp.sum(e, axis=0, keepdims=True) + es
    return e / z, es / z


def _att_bands(cur, kvp):
    kband = jnp.concatenate([kvp[:, 0:BLK], cur[:, C_K:C_K + BLK]], axis=0)
    vband = jnp.concatenate([kvp[:, BLK:2 * BLK], cur[:, C_V:C_V + BLK]], axis=0)
    return kband, pltpu.roll(kband, 64, 1), vband, pltpu.roll(vband, 64, 1)


def _stack_tiles(ref_or_val, start):
    return jnp.concatenate([ref_or_val[:, start + BLK * t:start + BLK * (t + 1)] for t in range(4)], axis=0)


def _sgu_mask():
    r = lax.broadcasted_iota(jnp.int32, (BLK, BLK), 0)
    c = lax.broadcasted_iota(jnp.int32, (BLK, BLK), 1)
    return (c // 64) <= (r // 64)


def _pool_cnt(blk, w):
    t = blk * BLK + lax.broadcasted_iota(jnp.int32, (BLK, 1), 0)
    return jnp.minimum(t + 1, w).astype(F32)


def _mix_in_specs(nb, rev):
    def b(i):
        return nb - 1 - i if rev else i
    return [
        pl.BlockSpec((BLK, C_GATE), lambda i: (b(i), 0)),
        pl.BlockSpec((BLK, 2 * BLK), lambda i: (jnp.maximum(b(i) - 1, 0), C_K // (2 * BLK))),
        pl.BlockSpec((HALO, C_GATE), lambda i: (jnp.maximum(b(i) * (BLK // HALO) - 1, 0), 0)),
        pl.BlockSpec((8, BLK), lambda i: (0, 0)),
        pl.BlockSpec((4, BLK, BLK), lambda i: (0, 0, 0)),
        pl.BlockSpec((4, BLK, 1), lambda i: (0, 0, 0)),
        pl.BlockSpec((1, 512), lambda i: (0, 0)),
        pl.BlockSpec((4, BLK, BLK), lambda i: (0, 0, 0)),
        pl.BlockSpec((1, 512), lambda i: (0, 0)),
    ]


def _mix_fwd(proj, sinks_b, ws, bs3, gsgu, wp, ps, name):
    S = proj.shape[0]
    nb = S // BLK

    def body(cur_ref, kvp_ref, pcp_ref, sk_ref, ws_ref, bs_ref, gs_ref, wp_ref, ps_ref, br_ref, ext_ref):
        i = pl.program_id(0)
        not_first = i > 0
        cur, kvp = cur_ref[...].astype(F32), kvp_ref[...].astype(F32)
        mask = _att_mask(not_first)
        own = _straight_lanes()
        q = _stack_tiles(cur, C_Q)
        outs = []
        kband, kroll, vband, vroll = _att_bands(cur, kvp)
        for qg, kg, vg, heads in ((jnp.where(own, q, 0.0), kband, vband, _STRAIGHT_HEADS),
                                  (jnp.where(own, 0.0, q), kroll, vroll, _ROLLED_HEADS)):
            s = jnp.where(mask, _dot(kg.astype(BF16), qg.astype(BF16), "nt") * ATT_SCALE, NEG_INF)
            p, _ = _softmax_sink(s, _sink_row(sk_ref, heads))
            outs.append(_dot(p.astype(BF16), vg.astype(BF16), "tn"))
        o = jnp.where(own, outs[0], outs[1])
        for t in range(4):
            br_ref[0, :, BLK * t:BLK * (t + 1)] = o[BLK * t:BLK * (t + 1)].astype(BF16)
        gu = jax.nn.gelu(cur[:, C_SU:C_SV])
        vn = _rms(jax.nn.gelu(cur[:, C_SV:C_PC]), gs_ref[...]).astype(BF16)
        wmask = _sgu_mask()
        for g in range(4):
            wm = jnp.where(wmask, ws_ref[g], 0.0).astype(BF16)
            sp = _dot(wm, vn[:, BLK * g:BLK * (g + 1)], "nn") + bs_ref[g]
            br_ref[1, :, BLK * g:BLK * (g + 1)] = (gu[:, BLK * g:BLK * (g + 1)] * sp).astype(BF16)
        c = cur[:, C_PC:C_GATE]
        ext_ref[0:HALO, :] = jnp.where(not_first, pcp_ref[:, C_PC:C_GATE].astype(F32), 0.0)
        ext_ref[HALO:HALO + BLK, :] = c
        for g, w in enumerate(POOL_WINDOWS):
            sl = slice(BLK * g, BLK * (g + 1))
            acc = ext_ref[HALO:HALO + BLK, sl]
            for k in range(1, w):
                acc = acc + ext_ref[HALO - k:HALO - k + BLK, sl]
            pooled = acc / _pool_cnt(i, w) - c[:, sl]
            mixed = _dot(pooled.astype(BF16), wp_ref[g].astype(BF16), "nn")
            br_ref[2, :, sl] = (mixed * ps_ref[:, sl]).astype(BF16)

    return pl.pallas_call(
        body, name=name, grid=(nb,),
        in_specs=_mix_in_specs(nb, False),
        out_specs=pl.BlockSpec((3, BLK, 512), lambda i: (0, i, 0)),
        out_shape=jax.ShapeDtypeStruct((3, S, 512), BF16),
        scratch_shapes=[pltpu.VMEM((HALO + BLK, 512), F32)],
        compiler_params=_params(("parallel",)),
    )(proj, proj, proj, sinks_b, ws, bs3, gsgu, wp, ps)


def _mix_bwd(proj, dbr, dproj, sinks_b, ws, bs3, gsgu, wp, ps, name):
    S = proj.shape[0]
    nb = S // BLK

    def body(cur_ref, kvp_ref, pcp_ref, sk_ref, ws_ref, bs_ref, gs_ref, wp_ref, ps_ref, dbr_ref, _dproj_in,
             dp_ref, dsk_ref, dws_ref, dbs_ref, dgs_ref, dwp_ref, dps_ref,
             ext_ref, z_ref, ckv_ref, cpc_ref):
        i = pl.program_id(0)
        blk = nb - 1 - i
        not_first = blk > 0

        @pl.when(i == 0)
        def _():
            for r in (dsk_ref, dws_ref, dbs_ref, dgs_ref, dwp_ref, dps_ref, ckv_ref, cpc_ref, z_ref):
                r[...] = jnp.zeros_like(r)

        cur, kvp = cur_ref[...].astype(F32), kvp_ref[...].astype(F32)
        mask = _att_mask(not_first)
        own = _straight_lanes()
        q = _stack_tiles(cur, C_Q)
        do = jnp.concatenate([dbr_ref[0, :, BLK * t:BLK * (t + 1)] for t in range(4)], axis=0)
        kband, kroll, vband, vroll = _att_bands(cur, kvp)
        dqs, dks, dvs = [], [], []
        for qg, dog, kg, vg, heads in (
                (jnp.where(own, q, 0.0), jnp.where(own, do, 0.0), kband, vband, _STRAIGHT_HEADS),
                (jnp.where(own, 0.0, q), jnp.where(own, 0.0, do), kroll, vroll, _ROLLED_HEADS)):
            qg, dog, kg, vg = qg.astype(BF16), dog.astype(BF16), kg.astype(BF16), vg.astype(BF16)
            s = jnp.where(mask, _dot(kg, qg, "nt") * ATT_SCALE, NEG_INF)
            p, p_sink = _softmax_sink(s, _sink_row(sk_ref, heads))
            dp = _dot(vg, dog, "nt")
            rs = jnp.sum(p * dp, axis=0, keepdims=True)
            ds = (p * (dp - rs) * ATT_SCALE).astype(BF16)
            sink_row = p_sink * rs
            for t, h in enumerate(heads):
                dsk_ref[h:h + 1, :] += jnp.broadcast_to(
                    -jnp.sum(sink_row[:, BLK * t:BLK * (t + 1)], axis=1, keepdims=True), (1, BLK))
            dvs.append(_dot(p.astype(BF16), dog, "nn"))
            dks.append(_dot(ds, qg, "nn"))
            dqs.append(_dot(ds, kg, "tn"))
        dq = jnp.where(own, dqs[0], dqs[1])
        for t in range(4):
            dp_ref[:, C_Q + BLK * t:C_Q + BLK * (t + 1)] = dq[BLK * t:BLK * (t + 1)].astype(BF16)
        dk = dks[0] + pltpu.roll(dks[1], 64, 1)
        dv = dvs[0] + pltpu.roll(dvs[1], 64, 1)
        dp_ref[:, C_K:C_K + BLK] = (dk[BLK:] + ckv_ref[:, 0:BLK]).astype(BF16)
        dp_ref[:, C_V:C_V + BLK] = (dv[BLK:] + ckv_ref[:, BLK:]).astype(BF16)
        ckv_ref[:, 0:BLK] = dk[:BLK]
        ckv_ref[:, BLK:] = dv[:BLK]
        su, sv = cur[:, C_SU:C_SV], cur[:, C_SV:C_PC]
        gu, vjp_u = jax.vjp(jax.nn.gelu, su)
        vn, vjp_v = jax.vjp(lambda a, g: _rms(jax.nn.gelu(a), g), sv, gs_ref[...])
        vn16 = vn.astype(BF16)
        wmask = _sgu_mask()
        dgu, dvn = [], []
        for g in range(4):
            sl = slice(BLK * g, BLK * (g + 1))
            wm = jnp.where(wmask, ws_ref[g], 0.0).astype(BF16)
            sp = _dot(wm, vn16[:, sl], "nn") + bs_ref[g]
            dyb = dbr_ref[1, :, sl]
            dgu.append(dyb * sp)
            dsp = dyb * gu[:, sl]
            dsp16 = dsp.astype(BF16)
            dvn.append(_dot(wm, dsp16, "tn"))
            dws_ref[g] += jnp.where(wmask, _dot(dsp16, vn16[:, sl], "nt"), 0.0)
            dbs_ref[g] += jnp.sum(dsp, axis=1, keepdims=True)
        (dsu,) = vjp_u(jnp.concatenate(dgu, axis=1))
        dsv, dgs = vjp_v(jnp.concatenate(dvn, axis=1))
        dp_ref[:, C_SU:C_SV] = dsu.astype(BF16)
        dp_ref[:, C_SV:C_PC] = dsv.astype(BF16)
        dgs_ref[...] += dgs
        c = cur[:, C_PC:C_GATE]
        ext_ref[0:HALO, :] = jnp.where(not_first, pcp_ref[:, C_PC:C_GATE].astype(F32), 0.0)
        ext_ref[HALO:HALO + BLK, :] = c
        for g, w in enumerate(POOL_WINDOWS):
            sl = slice(BLK * g, BLK * (g + 1))
            acc = ext_ref[HALO:HALO + BLK, sl]
            for k in range(1, w):
                acc = acc + ext_ref[HALO - k:HALO - k + BLK, sl]
            cnt = _pool_cnt(blk, w)
            pooled16 = (acc / cnt - c[:, sl]).astype(BF16)
            wp16 = wp_ref[g].astype(BF16)
            mixed = _dot(pooled16, wp16, "nn")
            dyc = dbr_ref[2, :, sl]
            dps_ref[:, sl] += jnp.sum(dyc * mixed, axis=0, keepdims=True)
            dmixed16 = (dyc * ps_ref[:, sl]).astype(BF16)
            dwp_ref[g] += _dot(pooled16, dmixed16, "tn")
            dpooled = _dot(dmixed16, wp16, "nt")
            z_ref[HALO:HALO + BLK, sl] = dpooled / cnt
            dext = z_ref[0:HALO + BLK, sl]
            for k in range(1, w):
                dext = dext + z_ref[k:k + HALO + BLK, sl]
            dp_ref[:, C_PC + BLK * g:C_PC + BLK * (g + 1)] = (
                dext[HALO:] - dpooled + jnp.concatenate([jnp.zeros((BLK - HALO, BLK), F32), cpc_ref[:, sl]], axis=0)
            ).astype(BF16)
            cpc_ref[:, sl] = dext[:HALO]

    n_in = 11
    small = [jax.ShapeDtypeStruct((8, BLK), F32), jax.ShapeDtypeStruct((4, BLK, BLK), F32),
             jax.ShapeDtypeStruct((4, BLK, 1), F32), jax.ShapeDtypeStruct((1, 512), F32),
             jax.ShapeDtypeStruct((4, BLK, BLK), F32), jax.ShapeDtypeStruct((1, 512), F32)]
    small_specs = [pl.BlockSpec((8, BLK), lambda i: (0, 0)), pl.BlockSpec((4, BLK, BLK), lambda i: (0, 0, 0)),
                   pl.BlockSpec((4, BLK, 1), lambda i: (0, 0, 0)), pl.BlockSpec((1, 512), lambda i: (0, 0)),
                   pl.BlockSpec((4, BLK, BLK), lambda i: (0, 0, 0)), pl.BlockSpec((1, 512), lambda i: (0, 0))]
    res = pl.pallas_call(
        body, name=name, grid=(nb,),
        in_specs=_mix_in_specs(nb, True) + [
            pl.BlockSpec((3, BLK, 512), lambda i: (0, nb - 1 - i, 0)),
            pl.BlockSpec(memory_space=pl.ANY)],
        out_specs=[pl.BlockSpec((BLK, C_GATE), lambda i: (nb - 1 - i, 0))] + small_specs,
        out_shape=[jax.ShapeDtypeStruct(dproj.shape, dproj.dtype)] + small,
        scratch_shapes=[pltpu.VMEM((HALO + BLK, 512), F32), pltpu.VMEM((2 * HALO + BLK, 512), F32),
                        pltpu.VMEM((BLK, 2 * BLK), F32), pltpu.VMEM((HALO, 512), F32)],
        input_output_aliases={n_in - 1: 0},
        compiler_params=_params(("arbitrary",)),
    )(proj, proj, proj, sinks_b, ws, bs3, gsgu, wp, ps, dbr, dproj)
    return tuple(res)


_GW = 256


def _merge_fwd(proj, pb, name, T=4096):
    S, D = pb.shape[1], pb.shape[2]
    T = min(T, S)

    def body(gate_ref, pb_ref, out_ref, acc_ref):
        n = pl.program_id(2)

        @pl.when(n == 0)
        def _():
            acc_ref[...] = jnp.zeros_like(acc_ref)

        acc_ref[...] += jax.nn.sigmoid(gate_ref[...].astype(F32)) * pb_ref[...]

        @pl.when(n == 2)
        def _():
            out_ref[...] = acc_ref[...].astype(BF16)

    return pl.pallas_call(
        body, name=name, grid=(S // T, D // _GW, 3),
        in_specs=[pl.BlockSpec((T, _GW), lambda i, j, n: (i, C_GATE // _GW + n * (D // _GW) + j)),
                  pl.BlockSpec((None, T, _GW), lambda i, j, n: (n, i, j))],
        out_specs=pl.BlockSpec((T, _GW), lambda i, j, n: (i, j)),
        out_shape=jax.ShapeDtypeStruct((S, D), BF16),
        scratch_shapes=[pltpu.VMEM((T, _GW), F32)],
        compiler_params=_params(("parallel", "parallel", "arbitrary")),
    )(proj, pb)


def _merge_bwd(proj, pb, dmerged, name, T=4096):
    S, D = pb.shape[1], pb.shape[2]
    T = min(T, S)

    def body(gate_ref, pb_ref, dm_ref, dgate_ref, dpb_ref):
        sg = jax.nn.sigmoid(gate_ref[...].astype(F32))
        dm = dm_ref[...]
        dpb_ref[...] = (dm * sg).astype(BF16)
        dgate_ref[...] = (dm * pb_ref[...] * sg * (1.0 - sg)).astype(BF16)

    gate_map = lambda i, n, j: (i, C_GATE // _GW + n * (D // _GW) + j)
    return pl.pallas_call(
        body, name=name, grid=(S // T, 3, D // _GW),
        in_specs=[pl.BlockSpec((T, _GW), gate_map),
                  pl.BlockSpec((None, T, _GW), lambda i, n, j: (n, i, j)),
                  pl.BlockSpec((T, _GW), lambda i, n, j: (i, j))],
        out_specs=[pl.BlockSpec((T, _GW), gate_map),
                   pl.BlockSpec((None, T, _GW), lambda i, n, j: (n, i, j))],
        out_shape=[jax.ShapeDtypeStruct((S, C_END), BF16), jax.ShapeDtypeStruct((3, S, D), BF16)],
        compiler_params=_params(("parallel", "parallel", "parallel")),
    )(proj, pb, dmerged)


def _memattn_fwd(qm, kv, name, T=512):
    S, NM = qm.shape[0], kv.shape[0]
    T = min(T, S)

    def body(q_ref, kv_ref, o_ref):
        for h in range(4):
            sl = slice(128 * h, 128 * (h + 1))
            k = kv_ref[:, sl].astype(BF16)
            v = kv_ref[:, 512 + 128 * h:512 + 128 * (h + 1)].astype(BF16)
            s = _dot(k, q_ref[:, sl].astype(BF16), "nt") * MEM_SCALE
            p = jax.nn.softmax(s, axis=0)
            o_ref[:, sl] = _dot(p.astype(BF16), v, "tn").astype(BF16)

    return pl.pallas_call(
        body, name=name, grid=(S // T,),
        in_specs=[_row_spec(T, 512), pl.BlockSpec((NM, 1024), lambda i: (0, 0))],
        out_specs=_row_spec(T, 512), out_shape=jax.ShapeDtypeStruct((S, 512), BF16),
        compiler_params=_params(("parallel",)))(qm, kv)


def _memattn_bwd(qm, kv, dom, name, T=512):
    S, NM = qm.shape[0], kv.shape[0]
    T = min(T, S)

    def body(q_ref, kv_ref, do_ref, dq_ref, dkv_ref):
        i = pl.program_id(0)

        @pl.when(i == 0)
        def _():
            dkv_ref[...] = jnp.zeros_like(dkv_ref)

        for h in range(4):
            sl = slice(128 * h, 128 * (h + 1))
            sv_ = slice(512 + 128 * h, 512 + 128 * (h + 1))
            q = q_ref[:, sl].astype(BF16)
            k = kv_ref[:, sl].astype(BF16)
            v = kv_ref[:, sv_].astype(BF16)
            do = do_ref[:, sl].astype(BF16)
            p = jax.nn.softmax(_dot(k, q, "nt") * MEM_SCALE, axis=0)
            dp = _dot(v, do, "nt")
            ds = (p * (dp - jnp.sum(p * dp, axis=0, keepdims=True)) * MEM_SCALE).astype(BF16)
            dq_ref[:, sl] = _dot(ds, k, "tn").astype(BF16)
            dkv_ref[:, sl] += _dot(ds, q, "nn")
            dkv_ref[:, sv_] += _dot(p.astype(BF16), do, "nn")

    return pl.pallas_call(
        body, name=name, grid=(S // T,),
        in_specs=[_row_spec(T, 512), pl.BlockSpec((NM, 1024), lambda i: (0, 0)), _row_spec(T, 512)],
        out_specs=[_row_spec(T, 512), pl.BlockSpec((NM, 1024), lambda i: (0, 0))],
        out_shape=[jax.ShapeDtypeStruct((S, 512), BF16), jax.ShapeDtypeStruct((NM, 1024), F32)],
        compiler_params=_params(("arbitrary",)))(qm, kv, dom)


def _adamw(w, g, m, v, name, rows=None, g_row0=None, into=None, TR=512):
    R, C = w.shape
    lo, hi = rows if rows is not None else (0, R)
    g0 = lo if g_row0 is None else g_row0
    TR = _row_tile(math.gcd(math.gcd(lo, g0), hi - lo), TR)
    c1 = 1.0 - ADAM_B1 ** ADAM_STEP
    c2 = 1.0 - ADAM_B2 ** ADAM_STEP

    def body(w_ref, g_ref, m_ref, v_ref, *rest):
        d_ref, nm_ref, nv_ref = rest[-3:]
        gv = g_ref[...]
        nm = ADAM_B1 * m_ref[...] + (1.0 - ADAM_B1) * gv
        nv = ADAM_B2 * v_ref[...] + (1.0 - ADAM_B2) * jnp.square(gv)
        d_ref[...] = -ADAM_LR * ((nm / c1) / (jnp.sqrt(nv / c2) + ADAM_EPS) + ADAM_WD * w_ref[...])
        nm_ref[...] = nm
        nv_ref[...] = nv

    spec = pl.BlockSpec((TR, C), lambda i: (lo // TR + i, 0))
    g_spec = pl.BlockSpec((TR, C), lambda i: (g0 // TR + i, 0))
    prior = list(into) if into is not None else []
    return pl.pallas_call(
        body, name=name, grid=((hi - lo) // TR,),
        in_specs=[spec, g_spec, spec, spec] + [pl.BlockSpec(memory_space=pl.ANY)] * len(prior), out_specs=[spec] * 3,
        out_shape=[jax.ShapeDtypeStruct((R, C), F32)] * 3,
        input_output_aliases={4 + k: k for k in range(len(prior))},
        compiler_params=_params(("parallel",)))(w, g, m, v, *prior)


def _row_tile(R, pref):
    t = (pref // 8) * 8
    while t >= 8:
        if R % t == 0:
            return t
        t -= 8
    raise ValueError(f"no row tile for {R}")


def _sum_slots(stack, name, TR=512):
    n, R, C = stack.shape
    TR = R if R <= TR else _row_tile(R, TR)

    def body(s_ref, o_ref):
        acc = s_ref[0]
        for k in range(1, n):
            acc = acc + s_ref[k]
        o_ref[...] = acc

    return pl.pallas_call(
        body, name=name, grid=(R // TR,),
        in_specs=[pl.BlockSpec((n, TR, C), lambda i: (0, i, 0))],
        out_specs=pl.BlockSpec((TR, C), lambda i: (i, 0)),
        out_shape=jax.ShapeDtypeStruct((R, C), F32),
        compiler_params=_params(("parallel",)))(stack)


_ANY = pl.BlockSpec(memory_space=pl.ANY)


def _chip_of(j, c):
    return (j // 2, j % 2, c)


def _own_slab(shard, dtype, j_arr, name, first=0, count=None, plus=None, deps=(), TR=512):
    N, r, C = shard.shape
    B = N if count is None else count
    rh = r // 2
    TR = rh if rh <= TR else _row_tile(rh, TR)
    nt = rh // TR
    ins = [shard] if plus is None else [shard, plus]

    def body(j_ref, *refs):
        val = refs[0][...] if plus is None else refs[0][...] + refs[1][...]
        refs[-1][...] = val.astype(refs[-1].dtype)

    return pl.pallas_call(
        body, name=name,
        grid_spec=pltpu.PrefetchScalarGridSpec(
            num_scalar_prefetch=1, grid=(B, 2, nt),
            in_specs=[pl.BlockSpec((None, TR, C), lambda b, h, t, jr: (first + b, h * nt + t, 0))] * len(ins)
            + [_ANY] * len(deps),
            out_specs=pl.BlockSpec((None, None, None, TR, C), lambda b, h, t, jr: (b, jr[0], h, t, 0))),
        out_shape=jax.ShapeDtypeStruct((B, 4, 2, rh, C), dtype),
        compiler_params=_params(("parallel", "parallel", "parallel")),
    )(j_arr, *ins, *deps)


def _gather_weights(bufs, name):
    n = len(bufs)

    def body(*refs):
        buf = refs[n:2 * n]
        send_sems, recv_sems, fsend_sems, frecv_sems = refs[2 * n:]
        x, y, c = lax.axis_index("x"), lax.axis_index("y"), lax.axis_index("c")
        j = 2 * x + y
        sib = (x, y, 1 - c)
        sends = []
        for d in range(1, 4):
            for a in range(n):
                cp = pltpu.make_async_remote_copy(
                    src_ref=buf[a].at[:, j, c], dst_ref=buf[a].at[:, j, c], send_sem=send_sems.at[a, d - 1],
                    recv_sem=recv_sems.at[a, d - 1], device_id=_chip_of((j + d) % 4, c), device_id_type=MESH)
                cp.start()
                sends.append(cp)
        for d in range(1, 4):
            frm = (j + 4 - d) % 4
            for a in range(n):
                pltpu.make_async_remote_copy(
                    src_ref=buf[a].at[:, frm, c], dst_ref=buf[a].at[:, frm, c], send_sem=send_sems.at[a, d - 1],
                    recv_sem=recv_sems.at[a, d - 1], device_id=_chip_of(frm, c), device_id_type=MESH).wait_recv()
                cp = pltpu.make_async_remote_copy(
                    src_ref=buf[a].at[:, frm, c], dst_ref=buf[a].at[:, frm, c], send_sem=fsend_sems.at[a, d - 1],
                    recv_sem=frecv_sems.at[a, d - 1], device_id=sib, device_id_type=MESH)
                cp.start()
                sends.append(cp)
        for d in range(1, 4):
            frm = (j + 4 - d) % 4
            for a in range(n):
                pltpu.make_async_remote_copy(
                    src_ref=buf[a].at[:, frm, 1 - c], dst_ref=buf[a].at[:, frm, 1 - c], send_sem=fsend_sems.at[a, d - 1],
                    recv_sem=frecv_sems.at[a, d - 1], device_id=sib, device_id_type=MESH).wait_recv()
        for cp in sends:
            cp.wait_send()

    return pl.pallas_call(
        body, name=name,
        in_specs=[_ANY] * n, out_specs=[_ANY] * n,
        out_shape=[jax.ShapeDtypeStruct(b.shape, b.dtype) for b in bufs],
        scratch_shapes=[pltpu.SemaphoreType.DMA((n, 3))] * 4,
        input_output_aliases={a: a for a in range(n)},
    )(*bufs)


_HBM = pl.BlockSpec(memory_space=pltpu.HBM)
_SEM = pl.BlockSpec(memory_space=pltpu.SEMAPHORE)
_DATAFLOW = pltpu.SideEffectType.DATAFLOW_SIDE_EFFECTING


def _in_hbm(arrays):
    return [pltpu.with_memory_space_constraint(a, pltpu.HBM) for a in arrays]


def _start_copies(bufs, plan, count, name, deps=()):
    n, k = len(bufs), len(deps)

    def body(*refs):
        send_sems, recv_sems = refs[n + k], refs[n + k + 1]
        for i, (src, dst, dev) in enumerate(plan(refs[:n], False)):
            pltpu.make_async_remote_copy(src_ref=src, dst_ref=dst, send_sem=send_sems.at[i], recv_sem=recv_sems.at[i],
                                         device_id=dev, device_id_type=MESH).start()
        refs[-1][...] = jnp.zeros_like(refs[-1])

    return pl.pallas_call(
        body, name=name,
        out_shape=(pltpu.SemaphoreType.DMA((count,)), pltpu.SemaphoreType.DMA((count,)),
                   *[pltpu.HBM(b.shape, b.dtype) for b in bufs], jax.ShapeDtypeStruct((8, 128), F32)),
        in_specs=[_HBM] * n + [_ANY] * k,
        out_specs=(_SEM, _SEM, *[_HBM] * n, pl.BlockSpec(memory_space=pltpu.VMEM)),
        input_output_aliases={a: 2 + a for a in range(n)},
        compiler_params=pltpu.CompilerParams(has_side_effects=_DATAFLOW),
    )(*_in_hbm(bufs), *deps)


def _wait_copies(handle, plan, afters, name):
    send_sems, recv_sems, *bufs = handle[:-1]
    n = len(bufs)

    def body(*refs):
        send_sems, recv_sems = refs[n], refs[n + 1]
        for i, (src, dst, dev) in enumerate(plan(refs[:n], True)):
            cp = pltpu.make_async_remote_copy(src_ref=src, dst_ref=dst, send_sem=send_sems.at[i], recv_sem=recv_sems.at[i],
                                              device_id=dev, device_id_type=MESH)
            cp.wait_send()
            cp.wait_recv()

    return list(pl.pallas_call(
        body, name=name,
        out_shape=[pltpu.HBM(b.shape, b.dtype) for b in bufs],
        in_specs=[_HBM] * n + [_SEM, _SEM] + [_ANY] * len(afters), out_specs=[_HBM] * n,
        input_output_aliases={a: a for a in range(n)},
        compiler_params=pltpu.CompilerParams(has_side_effects=_DATAFLOW),
    )(*bufs, send_sems, recv_sems, *afters))


def _gather_plan(buf, waiting):
    c = lax.axis_index("c")
    j = 2 * lax.axis_index("x") + lax.axis_index("y")
    copies = []
    for d in range(1, 4):
        to, frm = (j + d) % 4, (j + 4 - d) % 4
        for b in buf:
            copies.append((b.at[:, j, c], b.at[:, frm if waiting else j, c], _chip_of(frm if waiting else to, c)))
    return copies


def _chip_plan(buf, waiting):
    n = len(buf) // 2
    c = lax.axis_index("c")
    j = 2 * lax.axis_index("x") + lax.axis_index("y")
    copies = []
    for d in range(1, 4):
        to = (j + d) % 4
        for a in range(n):
            copies.append((buf[a].at[to], buf[n + a].at[d - 1], _chip_of(to, c)))
    return copies


def _pair_plan(buf, waiting):
    n = len(buf) // 2
    c = lax.axis_index("c")
    sib = (lax.axis_index("x"), lax.axis_index("y"), 1 - c)
    return [(buf[a].at[:, pl.ds(1 - c, 1)], buf[n + a], sib) for a in range(n)]


def _gather_forward(bufs, name):
    n = len(bufs)

    def body(*refs):
        buf = refs[n:2 * n]
        send_sems, recv_sems = refs[2 * n:]
        x, y, c = lax.axis_index("x"), lax.axis_index("y"), lax.axis_index("c")
        j = 2 * x + y
        sib = (x, y, 1 - c)
        sends = []
        for d in range(1, 4):
            frm = (j + 4 - d) % 4
            for a in range(n):
                cp = pltpu.make_async_remote_copy(
                    src_ref=buf[a].at[:, frm, c], dst_ref=buf[a].at[:, frm, c], send_sem=send_sems.at[a, d - 1],
                    recv_sem=recv_sems.at[a, d - 1], device_id=sib, device_id_type=MESH)
                cp.start()
                sends.append(cp)
        for d in range(1, 4):
            frm = (j + 4 - d) % 4
            for a in range(n):
                pltpu.make_async_remote_copy(
                    src_ref=buf[a].at[:, frm, 1 - c], dst_ref=buf[a].at[:, frm, 1 - c], send_sem=send_sems.at[a, d - 1],
                    recv_sem=recv_sems.at[a, d - 1], device_id=sib, device_id_type=MESH).wait_recv()
        for cp in sends:
            cp.wait_send()

    return pl.pallas_call(
        body, name=name,
        in_specs=[_ANY] * n, out_specs=[_ANY] * n,
        out_shape=[jax.ShapeDtypeStruct(b.shape, b.dtype) for b in bufs],
        scratch_shapes=[pltpu.SemaphoreType.DMA((n, 3))] * 2,
        input_output_aliases={a: a for a in range(n)},
    )(*bufs)


def _pair_add(g4, r1, cj_arr, name, TR=512):
    B4, _, rh, C = g4.shape
    B = B4 // 4
    TR = rh if rh <= TR else _row_tile(rh, TR)

    def body(cj_ref, g_ref, r_ref, o16_ref, own_ref):
        s = g_ref[...].astype(F32) + r_ref[...].astype(F32)
        o16_ref[...] = s.astype(BF16)

        @pl.when(pl.program_id(2) == cj_ref[1])
        def _():
            own_ref[...] = s

    return pl.pallas_call(
        body, name=name,
        grid_spec=pltpu.PrefetchScalarGridSpec(
            num_scalar_prefetch=1, grid=(B, rh // TR, 4),
            in_specs=[pl.BlockSpec((None, None, TR, C), lambda b, t, p, cj: (b * 4 + p, cj[0], t, 0)),
                      pl.BlockSpec((None, None, TR, C), lambda b, t, p, cj: (b * 4 + p, 0, t, 0))],
            out_specs=[pl.BlockSpec((None, None, TR, C), lambda b, t, p, cj: (p, b, t, 0)),
                       pl.BlockSpec((None, TR, C), lambda b, t, p, cj: (b, t, 0))]),
        out_shape=[jax.ShapeDtypeStruct((4, B, rh, C), BF16), jax.ShapeDtypeStruct((B, rh, C), F32)],
        compiler_params=_params(("parallel", "parallel", "arbitrary")),
    )(cj_arr, g4, r1)


def _chip_add(own, r2, cj_arr, into, first, name, TR=512):
    B, rh, C = own.shape
    TR = rh if rh <= TR else _row_tile(rh, TR)

    def body(cj_ref, p_ref, r_ref, _into_ref, o_ref):
        o_ref[...] = p_ref[...] + r_ref[0].astype(F32) + r_ref[1].astype(F32) + r_ref[2].astype(F32)

    return pl.pallas_call(
        body, name=name,
        grid_spec=pltpu.PrefetchScalarGridSpec(
            num_scalar_prefetch=1, grid=(B, rh // TR),
            in_specs=[pl.BlockSpec((None, TR, C), lambda b, t, cj: (b, t, 0)),
                      pl.BlockSpec((3, None, TR, C), lambda b, t, cj: (0, b, t, 0)),
                      _ANY],
            out_specs=pl.BlockSpec((None, None, TR, C), lambda b, t, cj: (first + b, cj[0], t, 0))),
        out_shape=jax.ShapeDtypeStruct(into.shape, F32),
        input_output_aliases={3: 0},
        compiler_params=_params(("parallel", "parallel")),
    )(cj_arr, own, r2, into)


def _pair_share(bufs, spans, name, deps=()):
    n = len(bufs)

    def body(*refs):
        buf = refs[n + len(deps):2 * n + len(deps)]
        send_sems, recv_sems = refs[2 * n + len(deps):]
        c = lax.axis_index("c")
        sib = (lax.axis_index("x"), lax.axis_index("y"), 1 - c)
        cps = []
        for a, (first, count) in enumerate(spans):
            cp = pltpu.make_async_remote_copy(
                src_ref=buf[a].at[pl.ds(first, count), c], dst_ref=buf[a].at[pl.ds(first, count), c],
                send_sem=send_sems.at[a], recv_sem=recv_sems.at[a], device_id=sib, device_id_type=MESH)
            cp.start()
            cps.append(cp)
        for a, (first, count) in enumerate(spans):
            pltpu.make_async_remote_copy(
                src_ref=buf[a].at[pl.ds(first, count), 1 - c], dst_ref=buf[a].at[pl.ds(first, count), 1 - c],
                send_sem=send_sems.at[a], recv_sem=recv_sems.at[a], device_id=sib, device_id_type=MESH).wait_recv()
        for cp in cps:
            cp.wait_send()

    return pl.pallas_call(
        body, name=name, in_specs=[_ANY] * (n + len(deps)), out_specs=[_ANY] * n,
        out_shape=[jax.ShapeDtypeStruct(b.shape, b.dtype) for b in bufs],
        scratch_shapes=[pltpu.SemaphoreType.DMA((n,)), pltpu.SemaphoreType.DMA((n,))],
        input_output_aliases={a: a for a in range(n)},
    )(*bufs, *deps)


def _pair_swap(arr, name):
    def body(src, dst, send_sem, recv_sem):
        sib = (lax.axis_index("x"), lax.axis_index("y"), 1 - lax.axis_index("c"))
        cp = pltpu.make_async_remote_copy(src_ref=src, dst_ref=dst, send_sem=send_sem, recv_sem=recv_sem,
                                          device_id=sib, device_id_type=MESH)
        cp.start()
        cp.wait_recv()
        cp.wait_send()

    return pl.pallas_call(
        body, name=name, in_specs=[_ANY], out_specs=_ANY,
        out_shape=jax.ShapeDtypeStruct(arr.shape, arr.dtype),
        scratch_shapes=[pltpu.SemaphoreType.DMA, pltpu.SemaphoreType.DMA],
    )(arr)


class _ReduceScatter:
    def __init__(self, n_layers, cj_arr):
        self.L, self.cj = n_layers, cj_arr
        self.total = None
        self.pair = None
        self.chip = None

    def _land(self, after):
        handle, layer, owns = self.chip
        n = len(owns)
        r2 = _wait_copies(handle, _chip_plan, after if isinstance(after, (list, tuple)) else (after,), "rs_chip_wait")[n:]
        if self.total is None:
            self.total = [lax.empty((self.L * o.shape[0], 2) + o.shape[1:], F32) for o in owns]
        self.total = [_chip_add(o, r, self.cj, t, layer * o.shape[0], "rs_chip_add")
                      for o, r, t in zip(owns, r2, self.total)]
        self.chip = None

    def add_layer(self, layer, grads):
        g4 = [g.reshape(g.shape[0] * 4, 2, g.shape[1] // 8, g.shape[2]) for g in grads]
        lands = [lax.empty((g.shape[0], 1) + g.shape[2:], g.dtype) for g in g4]
        handle = _start_copies(g4 + lands, _pair_plan, len(g4), "rs_pair_start")
        self.pair = (handle, layer)
        return (handle[-1],)

    def advance(self, after):
        if self.pair is None:
            return ()
        handle, layer = self.pair
        both = _wait_copies(handle, _pair_plan, (after,), "rs_pair_wait")
        n = len(both) // 2
        added = [_pair_add(g, r, self.cj, "rs_pair_add") for g, r in zip(both[:n], both[n:])]
        parts, owns = [p for p, _ in added], [o for _, o in added]
        lands = [lax.empty((3,) + p.shape[1:], p.dtype) for p in parts]
        handle = _start_copies(parts + lands, _chip_plan, 3 * n, "rs_chip_start")
        if self.chip is not None:
            self._land(handle[-1])
        self.pair, self.chip = None, (handle, layer, owns)
        return (handle[-1],)

    def upper_layers(self, deps):
        per = [t.shape[0] // self.L for t in self.total]
        self.total = _pair_share(self.total, [(b, (self.L - 1) * b) for b in per], "rs_pair_share_upper", deps=deps)
        return [t.reshape(t.shape[0], t.shape[1] * t.shape[2], t.shape[3]) for t in self.total]

    def first_layer(self, after):
        self._land(after)
        per = [t.shape[0] // self.L for t in self.total]
        full = _pair_share(self.total, [(0, b) for b in per], "rs_pair_share_first")
        return [f.reshape(f.shape[0], f.shape[1] * f.shape[2], f.shape[3]) for f in full]


def _relu2_epi(acc):
    return (jnp.square(jnp.maximum(acc, 0.0)),)


def _relu2_bwd_epi(acc, a):
    return (acc * (2.0 * jnp.sqrt(a.astype(F32))),)


_GRAD_ORDER = ("winT", "wbT", "wout", "wq", "wkv", "woT", "wupT", "wdown")
_DW = dict(tm=512, tn=1024, tk=4096, out_dtypes=(BF16,))
_LONG_K = dict(tm=1024, tn=1024, tk=2048)


def _forward_backward(x, mem, target, input_weight_of, weights_of, P, grads_done, grads_advance):
    L = P["g_norm"].shape[0]
    S, D = x.shape
    gn = lambda l, i: P["g_norm"][l, i][None]

    saved = []
    (h,) = _resnorm_fwd(x, None, None, gn(0, 0), "norm_in")
    xr = x
    for l in range(L):
        w_in_t = input_weight_of(l, xr)
        proj = _mm(h, w_in_t, "nt", "in_proj", b_pre=(0,), out_dtypes=(BF16,), tn=1792)
        W, w_deps = weights_of(l, proj)
        small = (jnp.broadcast_to(P["sinks"][l][:, None], (8, BLK)), P["ws"][l], P["bs"][l][:, :, None],
                 P["gsgu"][l][None], P["wp"][l], P["ps"][l][None])
        br = _mix_fwd(proj, *small, "mix_fwd")
        pb = lax.empty((3, S, D), BF16)
        for n in range(3):
            pb = _mm(br, W["wbT"], "nt", "branch_proj", a_pre=(n,), b_pre=(n,), into=pb, out_pre=(n,),
                     deps=w_deps if n == 0 else ())
        merged = _merge_fwd(proj, pb, "merge_fwd")
        z = _mm(merged, W["wout"], "nn", "out_proj", b_pre=(0,))
        x1, hm = _resnorm_fwd(xr, z, gn(l, 1), gn(l, 2), "resnorm_fwd")
        qm = _mm(hm, W["wq"], "nn", "mem_q", b_pre=(0,))
        (memn,) = _resnorm_fwd(mem, None, None, P["g_mem"][l][None], "mem_norm")
        kv = _mm(memn, W["wkv"], "nn", "mem_kv", b_pre=(0,))
        om = _memattn_fwd(qm, kv, "memattn_fwd")
        ym = _mm(om, W["woT"], "nt", "mem_o", b_pre=(0,))
        x2, hf = _resnorm_fwd(x1, ym, gn(l, 3), gn(l, 4), "resnorm_fwd")
        a = _mm(hf, W["wupT"], "nt", "mlp_up", b_pre=(0,), out_dtypes=(BF16,), epi=_relu2_epi)
        yf = _mm(a, W["wdown"], "nn", "mlp_down", b_pre=(0,), **_LONG_K)
        saved.append(dict(W=W, x0=xr, h=h, proj=proj, small=small, br=br, pb=pb, merged=merged, z=z, x1=x1, hm=hm,
                          qm=qm, memn=memn, kv=kv, om=om, ym=ym, x2=x2, hf=hf, a=a, yf=yf))
        if l < L - 1:
            xr, h = _resnorm_fwd(x2, yf, gn(l, 5), gn(l + 1, 0), "resnorm_fwd")
    dres, loss = _final_fwd(saved[-1]["x2"], saved[-1]["yf"], gn(L - 1, 5), target, "loss_head")

    dgn = [[None] * 6 for _ in range(L)]
    dsmall = {k: [None] * L for k in ("g_mem", "sinks", "ws", "bs", "gsgu", "wp", "ps")}
    dh = None
    for l in reversed(range(L)):
        s = saved[l]
        W, G = s["W"], {}
        if l == L - 1:
            dx2, dyf, dgn[l][5] = _resnorm_bwd(s["x2"], s["yf"], gn(l, 5), None, dres, None, "resnorm_bwd_top")
        else:
            dx2, dyf, dgn[l][5], dgn[l + 1][0] = _resnorm_bwd(s["x2"], s["yf"], gn(l, 5), gn(l + 1, 0), dres, dh,
                                                              "resnorm_bwd", deps=deps)
        du = _mm(dyf, W["wdown"], "nt", "mlp_down_dx", b_pre=(0,), out_dtypes=(BF16,), extras=(s["a"],), epi=_relu2_bwd_epi)
        G["wdown"] = _mm(s["a"], dyf, "tn", "mlp_down_dw", **_DW)[None]
        dhf = _mm(du, W["wupT"], "nn", "mlp_up_dx", b_pre=(0,), **_LONG_K)
        G["wupT"] = _mm(du, s["hf"], "tn", "mlp_up_dw", **_DW)[None]
        dx1, dym, dgn[l][3], dgn[l][4] = _resnorm_bwd(s["x1"], s["ym"], gn(l, 3), gn(l, 4), dx2, dhf, "resnorm_bwd")
        dom = _mm(dym, W["woT"], "nn", "mem_o_dx", b_pre=(0,), deps=grads_advance(dx1))
        G["woT"] = _mm(dym, s["om"], "tn", "mem_o_dw", **_DW)[None]
        dqm, dkv = _memattn_bwd(s["qm"], s["kv"], dom, "memattn_bwd")
        dmemn = _mm(dkv, W["wkv"], "nt", "mem_kv_dx", b_pre=(0,))
        G["wkv"] = _mm(s["memn"], dkv, "tn", "mem_kv_dw", out_dtypes=(BF16,))[None]
        _, dsmall["g_mem"][l] = _resnorm_bwd(mem, None, None, P["g_mem"][l][None], None, dmemn, "mem_norm_bwd")
        dhm = _mm(dqm, W["wq"], "nt", "mem_q_dx", b_pre=(0,))
        G["wq"] = _mm(s["hm"], dqm, "tn", "mem_q_dw", **_DW)[None]
        dx0, dz, dgn[l][1], dgn[l][2] = _resnorm_bwd(s["x0"], s["z"], gn(l, 1), gn(l, 2), dx1, dhm, "resnorm_bwd")
        dmerged = _mm(dz, W["wout"], "nt", "out_proj_dx", b_pre=(0,))
        G["wout"] = _mm(s["merged"], dz, "tn", "out_proj_dw", **_DW)[None]
        dproj, dpb = _merge_bwd(s["proj"], s["pb"], dmerged, "merge_bwd")
        dbr = lax.empty((3, S, 512), F32)
        G["wbT"] = lax.empty(W["wbT"].shape, BF16)
        for n in range(3):
            dbr = _mm(dpb, W["wbT"], "nn", "branch_proj_dx", a_pre=(n,), b_pre=(n,), into=dbr, out_pre=(n,))
            G["wbT"] = _mm(dpb, s["br"], "tn", "branch_proj_dw", a_pre=(n,), b_pre=(n,), into=G["wbT"], out_pre=(n,), **_DW)
        (dproj, dsmall["sinks"][l], dsmall["ws"][l], dsmall["bs"][l], dsmall["gsgu"][l], dsmall["wp"][l],
         dsmall["ps"][l]) = _mix_bwd(s["proj"], dbr, dproj, *s["small"], "mix_bwd")
        dh = _mm(dproj, W["winT"], "nn", "in_proj_dx", b_pre=(0,), **_LONG_K)
        G["winT"] = _mm(dproj, s["h"], "tn", "in_proj_dw", **_DW)[None]
        deps = grads_done(l, [G[k] for k in _GRAD_ORDER])
        dres = dx0
    grad_x, dgn[0][0] = _resnorm_bwd(x, None, None, gn(0, 0), dres, dh, "norm_in_bwd", deps=deps)
    tail_deps = grads_advance(grad_x)

    small_grads = dict(
        g_norm=jnp.stack([jnp.concatenate(row, axis=0) for row in dgn]),
        g_mem=jnp.concatenate(dsmall["g_mem"], axis=0),
        sinks=jnp.stack([d[:, 0] for d in dsmall["sinks"]]),
        ws=jnp.stack(dsmall["ws"]),
        bs=jnp.stack([d[:, :, 0] for d in dsmall["bs"]]),
        gsgu=jnp.concatenate(dsmall["gsgu"], axis=0),
        wp=jnp.stack(dsmall["wp"]),
        ps=jnp.concatenate(dsmall["ps"], axis=0),
    )
    return loss, grad_x, small_grads, tail_deps


_PACK_ROWS = 512


def _as_rows(a):
    n = math.prod(a.shape)
    if n % 128:
        a = jnp.pad(a.reshape(-1), (0, (-n) % 128))
    r = a.reshape(-1, 128)
    return jnp.pad(r, ((0, (-r.shape[0]) % 8), (0, 0))) if r.shape[0] % 8 else r


def _pack(arrays):
    rows = [_as_rows(a) for a in arrays]
    total = sum(r.shape[0] for r in rows)
    tail = (-total) % _PACK_ROWS
    if tail:
        rows.append(jnp.zeros((tail, 128), rows[0].dtype))
    return jnp.concatenate(rows, axis=0)


def _unpack(packed, like):
    out, pos = [], 0
    for a in like:
        n = math.prod(a.shape)
        nr = -(-n // 128)
        rows = packed[pos:pos + nr]
        out.append((rows.reshape(-1)[:n] if n % 128 else rows).reshape(a.shape))
        pos += nr + (-nr) % 8
    return out


_BIG = ("w_in", "w_branch", "w_out", "w_q_mem", "w_kv_mem", "w_o_mem", "w_up", "w_down")
_SMALL = ("g_norm", "g_mem", "attn_sinks", "w_spatial", "b_spatial", "g_sgu", "w_pool", "pool_scale")
_WEIGHTS = ("g_norm", "g_mem", "w_in", "attn_sinks", "w_spatial", "b_spatial", "g_sgu", "w_pool", "pool_scale",
            "w_branch", "w_out", "w_q_mem", "w_kv_mem", "w_o_mem", "w_up", "w_down")


def _to_working(name, w):
    if name == "w_in":
        return jnp.swapaxes(w, 1, 2)
    if name == "w_branch":
        t = jnp.swapaxes(w, 2, 3)
        return t.reshape(t.shape[0] * 3, t.shape[2], t.shape[3])
    if name in ("w_o_mem", "w_up"):
        return jnp.swapaxes(w, 1, 2)
    return w


def _from_working(name, g):
    if name == "w_in":
        return jnp.swapaxes(g, 1, 2)
    if name == "w_branch":
        return jnp.swapaxes(g.reshape(g.shape[0] // 3, 3, g.shape[1], g.shape[2]), 2, 3)
    if name in ("w_o_mem", "w_up"):
        return jnp.swapaxes(g, 1, 2)
    return g


_WKEY = dict(w_in="winT", w_branch="wbT", w_out="wout", w_q_mem="wq", w_kv_mem="wkv", w_o_mem="woT",
             w_up="wupT", w_down="wdown")


def kernel(x, mem, g_norm, g_mem, w_in, attn_sinks, w_spatial, b_spatial, g_sgu, w_pool, pool_scale, w_branch, w_out, w_q_mem, w_kv_mem, w_o_mem, w_up, w_down, loss_target, m_g_norm, m_g_mem, m_w_in, m_attn_sinks, m_w_spatial, m_b_spatial, m_g_sgu, m_w_pool, m_pool_scale, m_w_branch, m_w_out, m_w_q_mem, m_w_kv_mem, m_w_o_mem, m_w_up, m_w_down, v_g_norm, v_g_mem, v_w_in, v_attn_sinks, v_w_spatial, v_b_spatial, v_g_sgu, v_w_pool, v_pool_scale, v_w_branch, v_w_out, v_w_q_mem, v_w_kv_mem, v_w_o_mem, v_w_up, v_w_down):
    w = dict(g_norm=g_norm, g_mem=g_mem, w_in=w_in, attn_sinks=attn_sinks, w_spatial=w_spatial, b_spatial=b_spatial,
             g_sgu=g_sgu, w_pool=w_pool, pool_scale=pool_scale, w_branch=w_branch, w_out=w_out, w_q_mem=w_q_mem,
             w_kv_mem=w_kv_mem, w_o_mem=w_o_mem, w_up=w_up, w_down=w_down)
    m = dict(g_norm=m_g_norm, g_mem=m_g_mem, w_in=m_w_in, attn_sinks=m_attn_sinks, w_spatial=m_w_spatial,
             b_spatial=m_b_spatial, g_sgu=m_g_sgu, w_pool=m_w_pool, pool_scale=m_pool_scale, w_branch=m_w_branch,
             w_out=m_w_out, w_q_mem=m_w_q_mem, w_kv_mem=m_w_kv_mem, w_o_mem=m_w_o_mem, w_up=m_w_up, w_down=m_w_down)
    v = dict(g_norm=v_g_norm, g_mem=v_g_mem, w_in=v_w_in, attn_sinks=v_attn_sinks, w_spatial=v_w_spatial,
             b_spatial=v_b_spatial, g_sgu=v_g_sgu, w_pool=v_w_pool, pool_scale=v_pool_scale, w_branch=v_w_branch,
             w_out=v_w_out, w_q_mem=v_w_q_mem, w_kv_mem=v_w_kv_mem, w_o_mem=v_w_o_mem, w_up=v_w_up, w_down=v_w_down)
    L = g_norm.shape[0]
    j = 2 * lax.axis_index("x") + lax.axis_index("y")
    c = lax.axis_index("c")
    j_arr = jnp.reshape(j, (1,)).astype(jnp.int32)
    cj_arr = jnp.stack([c, j]).astype(jnp.int32)

    gs = g_norm.shape[2]
    working = [_to_working(n, w[n]) for n in _BIG]
    per_layer = [wk.shape[0] // L for wk in working]

    def own_slabs(l):
        return [_own_slab(wk, BF16, j_arr, "own_slab", first=l * b, count=b) for wk, b in zip(working, per_layer)]

    first_slabs = own_slabs(0)
    lead = [first_slabs[0], _own_slab(g_norm.reshape(1, L * 6 * gs // 128, 128), F32, j_arr, "own_slab_norm")]
    lead_handle = _start_copies(lead, _gather_plan, 3 * len(lead), "gather_start_lead")
    rest_handle = _start_copies(first_slabs[1:], _gather_plan, 3 * (len(first_slabs) - 1), "gather_start_first",
                                deps=(lead_handle[-1],))
    slabs = {l: own_slabs(l) for l in range(1, L)}
    lead = _gather_forward(_wait_copies(lead_handle, _gather_plan,
                                        [rest_handle[-1]] + [s for l in slabs for s in slabs[l]],
                                        "gather_wait_lead"), "gather_forward_lead")
    gn_full = jnp.transpose(lead[1].reshape(4, L * 6, gs), (1, 0, 2)).reshape(L, 6, 4 * gs)
    P = dict(g_norm=gn_full, g_mem=g_mem, sinks=attn_sinks, ws=w_spatial, bs=b_spatial, gsgu=g_sgu, wp=w_pool,
             ps=pool_scale)
    whole = lambda g: g.reshape(g.shape[0], 8 * g.shape[3], g.shape[4])
    gathered, in_flight = {}, {}

    def layer_weights(l, after):
        if l not in gathered:
            gathered[l] = _gather_forward(_wait_copies(in_flight[l], _gather_plan, (after,), "gather_wait"),
                                          "gather_forward")
        return gathered[l]

    def input_weight_of(l, after):
        return whole(lead[0] if l == 0 else layer_weights(l, after)[0])

    def weights_of(l, after):
        deps = ()
        if l == 0:
            rest = _gather_forward(_wait_copies(rest_handle, _gather_plan, (after,), "gather_wait_first"),
                                   "gather_forward_first")
            gathered[0] = [lead[0], *rest]
            dep = rest[0]
            for k in range(1, L):
                in_flight[k] = _start_copies(slabs[k], _gather_plan, 3 * len(slabs[k]), "gather_start", deps=(dep,))
                dep = in_flight[k][-1]
            deps = tuple(h[-1] for h in in_flight.values())
        return {k: whole(g) for k, g in zip(_GRAD_ORDER, layer_weights(l, after))}, deps

    rs = _ReduceScatter(L, cj_arr)
    loss_part, grad_x, sg, tail_deps = _forward_backward(
        x[0], mem[0], loss_target[0], input_weight_of, weights_of, P, rs.add_layer, rs.advance)
    loss = lax.psum(loss_part[0, 0], ("x", "y", "c"))

    transposed = ("w_branch", "w_o_mem", "w_up")
    view = {n: (lambda t: jnp.swapaxes(t, 1, 2)) if n == "w_in" else (lambda t: t) for n in _BIG}
    rows = lambda n, t: view[n](t).reshape(-1, view[n](t).shape[-1])
    state = {n: (rows(n, w[n]), rows(n, m[n]), rows(n, v[n])) for n in _BIG}
    updated, upper_grad = {}, {}
    if L > 1:
        for n, gw in zip(_BIG, rs.upper_layers(tail_deps)):
            per = gw.shape[0] // L
            if n in transposed:
                upper_grad[n] = _from_working(n, gw[per:])
                g2d, g_row0 = upper_grad[n].reshape(-1, upper_grad[n].shape[-1]), 0
            else:
                g2d, g_row0 = gw.reshape(-1, gw.shape[-1]), None
            r_layer = state[n][0].shape[0] // L
            updated[n] = _adamw(state[n][0], g2d, state[n][1], state[n][2], "adamw_upper",
                                rows=(r_layer, L * r_layer), g_row0=g_row0)
    full_small = [sg["g_norm"], sg["g_mem"], sg["sinks"], sg["ws"], sg["bs"], sg["gsgu"], sg["wp"], sg["ps"]]
    packed = _pack(full_small)
    pair_sum = _own_slab(packed[None], F32, j_arr, "small_grads_pair_sum",
                         deps=tuple(tail_deps) + tuple(u[0] for u in updated.values()),
                         plus=_pair_swap(packed, "small_grads_swap")[None])
    (chip_sums,) = _gather_weights([pair_sum], "gather_small_grads")
    total = _sum_slots(chip_sums.reshape(4, *packed.shape), "sum_small_grads")
    grads = {}
    for n, g in zip(_SMALL, _unpack(total, full_small)):
        grads[n] = lax.dynamic_slice_in_dim(g, j * g_norm.shape[2], g_norm.shape[2], axis=2) if n == "g_norm" else g

    after = [total] + [u[0] for u in updated.values()]
    delta, new_m, new_v = {}, {}, {}
    for n, gw in zip(_BIG, rs.first_layer(after)):
        per = gw.shape[0] // L
        if n in transposed:
            g0 = _from_working(n, gw[:per])
            grads[n] = jnp.concatenate([g0, upper_grad[n]], axis=0) if L > 1 else g0
            g2d = g0.reshape(-1, g0.shape[-1])
        else:
            grads[n] = _from_working(n, gw)
            g2d = gw.reshape(-1, gw.shape[-1])
        r_layer = state[n][0].shape[0] // L
        d_, m_, v_ = _adamw(state[n][0], g2d, state[n][1], state[n][2], "adamw_first", rows=(0, r_layer), g_row0=0,
                            into=updated.get(n))
        shp = view[n](w[n]).shape
        delta[n], new_m[n], new_v[n] = view[n](d_.reshape(shp)), view[n](m_.reshape(shp)), view[n](v_.reshape(shp))
    small_w = [w[n] for n in _SMALL]
    d_, m_, v_ = _adamw(_pack(small_w), _pack([grads[n] for n in _SMALL]), _pack([m[n] for n in _SMALL]),
                        _pack([v[n] for n in _SMALL]), "adamw_small")
    for n, dd, mm_, vv in zip(_SMALL, _unpack(d_, small_w), _unpack(m_, small_w), _unpack(v_, small_w)):
        delta[n], new_m[n], new_v[n] = dd, mm_, vv

    return (loss, grad_x[None], *[grads[n] for n in _WEIGHTS], *[delta[n] for n in _WEIGHTS],
            *[new_m[n] for n in _WEIGHTS], *[new_v[n] for n in _WEIGHTS])
```

```python
import functools
import math

import jax
import jax.numpy as jnp
from jax import lax
from jax.experimental import pallas as pl
from jax.experimental.pallas import tpu as pltpu

F32 = jnp.float32
BF16 = jnp.bfloat16
MESH = pl.DeviceIdType.MESH

EPS = 1e-6
NEG_INF = -1e30
BLK = 128
HALO = 16
POOL_WINDOWS = (2, 4, 8, 16)
ATT_SCALE = 1.0 / math.sqrt(64.0)
MEM_SCALE = 1.0 / math.sqrt(128.0)
C_Q, C_K, C_V, C_SU, C_SV, C_PC, C_GATE, C_END = 0, 512, 640, 768, 1280, 1792, 2304, 5376

ADAM_LR, ADAM_B1, ADAM_B2, ADAM_EPS, ADAM_WD, ADAM_STEP = 0.001, 0.9, 0.999, 1e-08, 0.01, 10

VMEM_LIMIT_BYTES = 56 * 1024 * 1024

_DIMS = {
    "nn": (((1,), (0,)), ((), ())),
    "nt": (((1,), (1,)), ((), ())),
    "tn": (((0,), (0,)), ((), ())),
}


def _dot(a, b, mode):
    return lax.dot_general(a, b, _DIMS[mode], preferred_element_type=F32)


def _params(semantics):
    return pltpu.CompilerParams(dimension_semantics=semantics, vmem_limit_bytes=VMEM_LIMIT_BYTES)


def _tile(dim, pref):
    if dim <= pref:
        return dim
    t = (pref // 128) * 128
    while t >= 128:
        if dim % t == 0:
            return t
        t -= 128
    raise ValueError(f"no tile for {dim}")


def _rms(x, g):
    return x * lax.rsqrt(jnp.mean(x * x, axis=-1, keepdims=True) + EPS) * g


def _mm(a, b, mode, name, *, out_dtypes=(F32,), a_pre=(), b_pre=(), into=None, out_pre=(),
        extras=(), epi=None, deps=(), batch=None, tm=2048, tn=1024, tk=1024):
    lead = 0 if batch is None else 1
    assert not (lead and (a_pre or b_pre or into is not None or extras))
    a2, b2 = a.shape[len(a_pre) + lead:], b.shape[len(b_pre) + lead:]
    if mode == "nn":
        (M, K), (K2, N) = a2, b2
    elif mode == "nt":
        (M, K), (N, K2) = a2, b2
    else:
        (K, M), (K2, N) = a2, b2
    assert K == K2, (a.shape, b.shape, mode)
    tm, tn, tk = _tile(M, tm), _tile(N, tn), _tile(K, tk)
    nk = K // tk
    na, nb_, no = len(a_pre), len(b_pre), len(out_pre)

    def on_grid(f):
        return f if batch is None else (lambda g, i, j, k: (g,) + f(i, j, k))

    if mode == "tn":
        a_spec = pl.BlockSpec((None,) * (na + lead) + (tk, tm), on_grid(lambda i, j, k: a_pre + (k, i)))
    else:
        a_spec = pl.BlockSpec((None,) * (na + lead) + (tm, tk), on_grid(lambda i, j, k: a_pre + (i, k)))
    if mode == "nt":
        b_spec = pl.BlockSpec((None,) * (nb_ + lead) + (tn, tk), on_grid(lambda i, j, k: b_pre + (j, k)))
    else:
        b_spec = pl.BlockSpec((None,) * (nb_ + lead) + (tk, tn), on_grid(lambda i, j, k: b_pre + (k, j)))
    tile_spec = pl.BlockSpec((None,) * lead + (tm, tn), on_grid(lambda i, j, k: (i, j)))
    ne, nout = len(extras), len(out_dtypes)
    in_specs = [a_spec, b_spec] + [tile_spec] * ne
    operands = [a, b, *extras]
    aliases = {}
    if into is not None:
        assert nout == 1
        in_specs.append(pl.BlockSpec(memory_space=pl.ANY))
        operands.append(into)
        aliases = {len(operands) - 1: 0}
        out_shape = [jax.ShapeDtypeStruct(into.shape, into.dtype)]
        out_specs = [pl.BlockSpec((None,) * no + (tm, tn), lambda i, j, k: out_pre + (i, j))]
    else:
        out_shape = [jax.ShapeDtypeStruct(((batch,) if lead else ()) + (M, N), dt) for dt in out_dtypes]
        out_specs = [tile_spec] * nout
    in_specs += [pl.BlockSpec(memory_space=pl.ANY)] * len(deps)
    operands += list(deps)

    def body(*refs):
        a_ref, b_ref = refs[0], refs[1]
        ex = refs[2:2 + ne]
        pos = 2 + ne + (1 if into is not None else 0) + len(deps)
        outs = refs[pos:pos + nout]
        acc_ref = refs[pos + nout] if nk > 1 else None

        def finish(acc):
            vals = epi(acc, *[e[...] for e in ex]) if epi is not None else (acc,)
            for o, v in zip(outs, vals):
                o[...] = v.astype(o.dtype)

        def prod():
            return _dot(a_ref[...].astype(BF16), b_ref[...].astype(BF16), mode)

        if nk == 1:
            finish(prod())
        else:
            k = pl.program_id(2 + lead)

            @pl.when(k == 0)
            def _():
                acc_ref[...] = jnp.zeros_like(acc_ref)

            acc_ref[...] += prod()

            @pl.when(k == nk - 1)
            def _():
                finish(acc_ref[...])

    res = pl.pallas_call(
        body, name=name, grid=((batch,) if lead else ()) + (M // tm, N // tn, nk),
        in_specs=in_specs, out_specs=out_specs, out_shape=out_shape,
        scratch_shapes=[pltpu.VMEM((tm, tn), F32)] if nk > 1 else [],
        input_output_aliases=aliases,
        compiler_params=_params(("parallel",) * (2 + lead) + ("arbitrary",)),
    )(*operands)
    return res[0] if nout == 1 else tuple(res)


def _resnorm_fn(has_post, has_pre):
    def f(*a):
        x, k = a[0], 1
        if has_post:
            x, k = x + _rms(a[1], a[2]), 3
        outs = [x]
        if has_pre:
            outs.append(_rms(x, a[k]))
        return tuple(outs)
    return f


def _row_spec(T, W):
    return pl.BlockSpec((T, W), lambda i: (i, 0))


def _par_spec(W):
    return pl.BlockSpec((1, W), lambda i: (0, 0))


def _resnorm_fwd(xr, y, gp, gq, name, T=512, deps=()):
    S, D = xr.shape
    T = min(T, S)
    has_post, has_pre = y is not None, gq is not None
    f = _resnorm_fn(has_post, has_pre)
    ins = [xr] + ([y, gp] if has_post else []) + ([gq] if has_pre else [])
    in_specs = [_row_spec(T, D)] + ([_row_spec(T, D), _par_spec(D)] if has_post else []) + ([_par_spec(D)] if has_pre else [])
    out_shape, out_specs = [], []
    if has_post:
        out_shape.append(jax.ShapeDtypeStruct((S, D), F32)); out_specs.append(_row_spec(T, D))
    if has_pre:
        out_shape.append(jax.ShapeDtypeStruct((S, D), BF16)); out_specs.append(_row_spec(T, D))
    n_in, n_dep = len(ins), len(deps)

    def body(*refs):
        vals = f(*[r[...] for r in refs[:n_in]])
        outs = list(refs[n_in + n_dep:])
        if has_post:
            outs.pop(0)[...] = vals[0]
        if has_pre:
            outs.pop(0)[...] = vals[1].astype(BF16)

    res = pl.pallas_call(body, name=name, grid=(S // T,),
                         in_specs=in_specs + [pl.BlockSpec(memory_space=pl.ANY)] * n_dep, out_specs=out_specs,
                         out_shape=out_shape, compiler_params=_params(("parallel",)))(*ins, *deps)
    return tuple(res)


def _resnorm_bwd(xr, y, gp, gq, dres, dh, name, T=512, deps=()):
    S, D = xr.shape
    T = min(T, S)
    has_post, has_pre, has_res = y is not None, gq is not None, dres is not None
    f = _resnorm_fn(has_post, has_pre)
    ins = [xr] + ([y, gp] if has_post else []) + ([gq] if has_pre else [])
    in_specs = [_row_spec(T, D)] + ([_row_spec(T, D), _par_spec(D)] if has_post else []) + ([_par_spec(D)] if has_pre else [])
    n_prim = len(ins)
    if has_res:
        ins.append(dres); in_specs.append(_row_spec(T, D))
    if has_pre:
        ins.append(dh); in_specs.append(_row_spec(T, D))
    n_in, n_dep = len(ins), len(deps)
    out_shape = [jax.ShapeDtypeStruct((S, D), F32)]
    out_specs = [_row_spec(T, D)]
    if has_post:
        out_shape += [jax.ShapeDtypeStruct((S, D), BF16), jax.ShapeDtypeStruct((1, D), F32)]
        out_specs += [_row_spec(T, D), _par_spec(D)]
    if has_pre:
        out_shape.append(jax.ShapeDtypeStruct((1, D), F32)); out_specs.append(_par_spec(D))

    def body(*refs):
        i = pl.program_id(0)
        prim = [r[...] for r in refs[:n_prim]]
        rest = list(refs[n_prim:n_in])
        ct_x = rest.pop(0)[...] if has_res else jnp.zeros((T, D), F32)
        cts = [ct_x]
        if has_pre:
            cts.append(rest.pop(0)[...].astype(F32))
        _, vjp = jax.vjp(f, *prim)
        grads = list(vjp(tuple(cts)))
        outs = list(refs[n_in + n_dep:])
        outs.pop(0)[...] = grads.pop(0)
        acc = []
        if has_post:
            outs.pop(0)[...] = grads.pop(0).astype(BF16)
            acc.append((outs.pop(0), grads.pop(0)))
        if has_pre:
            acc.append((outs.pop(0), grads.pop(0)))

        @pl.when(i == 0)
        def _():
            for o, _g in acc:
                o[...] = jnp.zeros_like(o)

        for o, g in acc:
            o[...] += g

    res = pl.pallas_call(body, name=name, grid=(S // T,),
                         in_specs=in_specs + [pl.BlockSpec(memory_space=pl.ANY)] * n_dep, out_specs=out_specs,
                         out_shape=out_shape, compiler_params=_params(("arbitrary",)))(*ins, *deps)
    return tuple(res)


def _final_fwd(xr, y, gp, target, name, T=512):
    S, D = xr.shape
    T = min(T, S)

    def body(x_ref, y_ref, g_ref, t_ref, dy_ref, loss_ref):
        i = pl.program_id(0)
        e = x_ref[...] + _rms(y_ref[...], g_ref[...]) - t_ref[...]
        dy_ref[...] = e / D

        @pl.when(i == 0)
        def _():
            loss_ref[...] = jnp.zeros_like(loss_ref)

        loss_ref[...] += 0.5 * jnp.sum(jnp.sum(e * e, axis=-1, keepdims=True) / D, axis=0, keepdims=True)

    return pl.pallas_call(
        body, name=name, grid=(S // T,),
        in_specs=[_row_spec(T, D), _row_spec(T, D), _par_spec(D), _row_spec(T, D)],
        out_specs=[_row_spec(T, D), pl.BlockSpec((1, 128), lambda i: (0, 0))],
        out_shape=[jax.ShapeDtypeStruct((S, D), F32), jax.ShapeDtypeStruct((1, 128), F32)],
        compiler_params=_params(("arbitrary",)))(xr, y, gp, target)


_STRAIGHT_HEADS = (0, 2, 5, 7)
_ROLLED_HEADS = (1, 3, 4, 6)


def _straight_lanes():
    r = lax.broadcasted_iota(jnp.int32, (4 * BLK, BLK), 0)
    c = lax.broadcasted_iota(jnp.int32, (4 * BLK, BLK), 1)
    return (r < 2 * BLK) == (c < 64)


def _att_mask(not_first):
    k = lax.broadcasted_iota(jnp.int32, (2 * BLK, 4 * BLK), 0)
    q = lax.broadcasted_iota(jnp.int32, (2 * BLK, 4 * BLK), 1) % BLK
    qc, kc = 2 + q // 64, k // 64
    return (kc <= qc) & (kc >= qc - 2) & (not_first | (k >= BLK))


def _sink_row(sk_ref, heads):
    return jnp.concatenate([sk_ref[h:h + 1, :] for h in heads], axis=1)


def _softmax_sink(s, sk):
    m = jnp.maximum(jnp.max(s, axis=0, keepdims=True), sk)
    e = jnp.exp(s - m)
    es = jnp.exp(sk - m)
    inv = 1.0 / (jnp.sum(e, axis=0, keepdims=True) + es)
    return e * inv, es * inv


def _softmax_rows(s):
    e = jnp.exp(s - jnp.max(s, axis=0, keepdims=True))
    return e * (1.0 / jnp.sum(e, axis=0, keepdims=True))


_GELU_C = math.sqrt(2.0 / math.pi)
_GELU_A = 0.044715


def _gelu_with_slope(x):
    x2 = x * x
    t = jnp.tanh(_GELU_C * (x + _GELU_A * (x2 * x)))
    half = 0.5 * (1.0 + t)
    return x * half, half + (0.5 * _GELU_C) * x * (1.0 - t * t) * (1.0 + (3.0 * _GELU_A) * x2)


def _att_bands(cur, kvp):
    kband = jnp.concatenate([kvp[:, 0:BLK], cur[:, C_K:C_K + BLK]], axis=0)
    vband = jnp.concatenate([kvp[:, BLK:2 * BLK], cur[:, C_V:C_V + BLK]], axis=0)
    return kband, pltpu.roll(kband, 64, 1), vband, pltpu.roll(vband, 64, 1)


def _stack_tiles(ref_or_val, start):
    return jnp.concatenate([ref_or_val[:, start + BLK * t:start + BLK * (t + 1)] for t in range(4)], axis=0)


def _sgu_mask():
    r = lax.broadcasted_iota(jnp.int32, (BLK, BLK), 0)
    c = lax.broadcasted_iota(jnp.int32, (BLK, BLK), 1)
    return (c // 64) <= (r // 64)


def _trailing_sums(ext, g):
    s, shift = ext, 1
    for _ in range(g + 1):
        s = s + pltpu.roll(s, shift, 0)
        shift *= 2
    return s


def _leading_sums(z, g):
    d, shift = z, 1
    for _ in range(g + 1):
        d = d + pltpu.roll(d, z.shape[0] - shift, 0)
        shift *= 2
    return d


def _pool_cnt(blk, w):
    t = blk * BLK + lax.broadcasted_iota(jnp.int32, (BLK, 1), 0)
    return jnp.minimum(t + 1, w).astype(F32)


def _mix_in_specs(nb, rev):
    def b(i):
        return nb - 1 - i if rev else i
    return [
        pl.BlockSpec((BLK, C_GATE), lambda i: (b(i), 0)),
        pl.BlockSpec((BLK, 2 * BLK), lambda i: (jnp.maximum(b(i) - 1, 0), C_K // (2 * BLK))),
        pl.BlockSpec((HALO, C_GATE), lambda i: (jnp.maximum(b(i) * (BLK // HALO) - 1, 0), 0)),
        pl.BlockSpec((8, BLK), lambda i: (0, 0)),
        pl.BlockSpec((4, BLK, BLK), lambda i: (0, 0, 0)),
        pl.BlockSpec((4, BLK, 1), lambda i: (0, 0, 0)),
        pl.BlockSpec((1, 512), lambda i: (0, 0)),
        pl.BlockSpec((4, BLK, BLK), lambda i: (0, 0, 0)),
        pl.BlockSpec((1, 512), lambda i: (0, 0)),
    ]


def _mix_fwd(proj, sinks_b, ws, bs3, gsgu, wp, ps, name):
    S = proj.shape[0]
    nb = S // BLK

    def body(cur_ref, kvp_ref, pcp_ref, sk_ref, ws_ref, bs_ref, gs_ref, wp_ref, ps_ref, br_ref, ext_ref):
        i = pl.program_id(0)
        not_first = i > 0
        cur, kvp = cur_ref[...].astype(F32), kvp_ref[...].astype(F32)
        mask = _att_mask(not_first)
        own = _straight_lanes()
        q = _stack_tiles(cur, C_Q)
        outs = []
        kband, kroll, vband, vroll = _att_bands(cur, kvp)
        for qg, kg, vg, heads in ((jnp.where(own, q, 0.0), kband, vband, _STRAIGHT_HEADS),
                                  (jnp.where(own, 0.0, q), kroll, vroll, _ROLLED_HEADS)):
            s = jnp.where(mask, _dot(kg.astype(BF16), qg.astype(BF16), "nt") * ATT_SCALE, NEG_INF)
            p, _ = _softmax_sink(s, _sink_row(sk_ref, heads))
            outs.append(_dot(p.astype(BF16), vg.astype(BF16), "tn"))
        o = jnp.where(own, outs[0], outs[1])
        for t in range(4):
            br_ref[0, :, BLK * t:BLK * (t + 1)] = o[BLK * t:BLK * (t + 1)].astype(BF16)
        gu = jax.nn.gelu(cur[:, C_SU:C_SV])
        vn = _rms(jax.nn.gelu(cur[:, C_SV:C_PC]), gs_ref[...]).astype(BF16)
        wmask = _sgu_mask()
        for g in range(4):
            wm = jnp.where(wmask, ws_ref[g], 0.0).astype(BF16)
            sp = _dot(wm, vn[:, BLK * g:BLK * (g + 1)], "nn") + bs_ref[g]
            br_ref[1, :, BLK * g:BLK * (g + 1)] = (gu[:, BLK * g:BLK * (g + 1)] * sp).astype(BF16)
        c = cur[:, C_PC:C_GATE]
        ext_ref[0:HALO, :] = jnp.where(not_first, pcp_ref[:, C_PC:C_GATE].astype(F32), 0.0)
        ext_ref[HALO:HALO + BLK, :] = c
        for g, w in enumerate(POOL_WINDOWS):
            sl = slice(BLK * g, BLK * (g + 1))
            acc = _trailing_sums(ext_ref[:, sl], g)[HALO:]
            pooled = acc * (1.0 / _pool_cnt(i, w)) - c[:, sl]
            mixed = _dot(pooled.astype(BF16), wp_ref[g].astype(BF16), "nn")
            br_ref[2, :, sl] = (mixed * ps_ref[:, sl]).astype(BF16)

    return pl.pallas_call(
        body, name=name, grid=(nb,),
        in_specs=_mix_in_specs(nb, False),
        out_specs=pl.BlockSpec((3, BLK, 512), lambda i: (0, i, 0)),
        out_shape=jax.ShapeDtypeStruct((3, S, 512), BF16),
        scratch_shapes=[pltpu.VMEM((HALO + BLK, 512), F32)],
        compiler_params=_params(("parallel",)),
    )(proj, proj, proj, sinks_b, ws, bs3, gsgu, wp, ps)


def _mix_bwd(proj, dbr, dproj, sinks_b, ws, bs3, gsgu, wp, ps, name):
    S = proj.shape[0]
    nb = S // BLK

    def body(cur_ref, kvp_ref, pcp_ref, sk_ref, ws_ref, bs_ref, gs_ref, wp_ref, ps_ref, dbr_ref, _dproj_in,
             dp_ref, dsk_ref, dws_ref, dbs_ref, dgs_ref, dwp_ref, dps_ref,
             ext_ref, z_ref, ckv_ref, cpc_ref):
        i = pl.program_id(0)
        blk = nb - 1 - i
        not_first = blk > 0

        @pl.when(i == 0)
        def _():
            for r in (dsk_ref, dws_ref, dbs_ref, dgs_ref, dwp_ref, dps_ref, ckv_ref, cpc_ref, z_ref):
                r[...] = jnp.zeros_like(r)

        cur, kvp = cur_ref[...].astype(F32), kvp_ref[...].astype(F32)
        mask = _att_mask(not_first)
        own = _straight_lanes()
        q = _stack_tiles(cur, C_Q)
        do = jnp.concatenate([dbr_ref[0, :, BLK * t:BLK * (t + 1)] for t in range(4)], axis=0)
        kband, kroll, vband, vroll = _att_bands(cur, kvp)
        dqs, dks, dvs = [], [], []
        for qg, dog, kg, vg, heads in (
                (jnp.where(own, q, 0.0), jnp.where(own, do, 0.0), kband, vband, _STRAIGHT_HEADS),
                (jnp.where(own, 0.0, q), jnp.where(own, 0.0, do), kroll, vroll, _ROLLED_HEADS)):
            qg, dog, kg, vg = qg.astype(BF16), dog.astype(BF16), kg.astype(BF16), vg.astype(BF16)
            s = jnp.where(mask, _dot(kg, qg, "nt") * ATT_SCALE, NEG_INF)
            p, p_sink = _softmax_sink(s, _sink_row(sk_ref, heads))
            dp = _dot(vg, dog, "nt")
            rs = jnp.sum(p * dp, axis=0, keepdims=True)
            ds = (p * (dp - rs) * ATT_SCALE).astype(BF16)
            sink_row = p_sink * rs
            for t, h in enumerate(heads):
                dsk_ref[h:h + 1, :] += jnp.broadcast_to(
                    -jnp.sum(sink_row[:, BLK * t:BLK * (t + 1)], axis=1, keepdims=True), (1, BLK))
            dvs.append(_dot(p.astype(BF16), dog, "nn"))
            dks.append(_dot(ds, qg, "nn"))
            dqs.append(_dot(ds, kg, "tn"))
        dq = jnp.where(own, dqs[0], dqs[1])
        for t in range(4):
            dp_ref[:, C_Q + BLK * t:C_Q + BLK * (t + 1)] = dq[BLK * t:BLK * (t + 1)].astype(BF16)
        dk = dks[0] + pltpu.roll(dks[1], 64, 1)
        dv = dvs[0] + pltpu.roll(dvs[1], 64, 1)
        dp_ref[:, C_K:C_K + BLK] = (dk[BLK:] + ckv_ref[:, 0:BLK]).astype(BF16)
        dp_ref[:, C_V:C_V + BLK] = (dv[BLK:] + ckv_ref[:, BLK:]).astype(BF16)
        ckv_ref[:, 0:BLK] = dk[:BLK]
        ckv_ref[:, BLK:] = dv[:BLK]
        su, sv = cur[:, C_SU:C_SV], cur[:, C_SV:C_PC]
        gu, gu_slope = _gelu_with_slope(su)
        gv, gv_slope = _gelu_with_slope(sv)
        vn, vjp_v = jax.vjp(_rms, gv, gs_ref[...])
        vn16 = vn.astype(BF16)
        wmask = _sgu_mask()
        dgu, dvn = [], []
        for g in range(4):
            sl = slice(BLK * g, BLK * (g + 1))
            wm = jnp.where(wmask, ws_ref[g], 0.0).astype(BF16)
            sp = _dot(wm, vn16[:, sl], "nn") + bs_ref[g]
            dyb = dbr_ref[1, :, sl]
            dgu.append(dyb * sp)
            dsp = dyb * gu[:, sl]
            dsp16 = dsp.astype(BF16)
            dvn.append(_dot(wm, dsp16, "tn"))
            dws_ref[g] += jnp.where(wmask, _dot(dsp16, vn16[:, sl], "nt"), 0.0)
            dbs_ref[g] += jnp.sum(dsp, axis=1, keepdims=True)
        dgv, dgs = vjp_v(jnp.concatenate(dvn, axis=1))
        dp_ref[:, C_SU:C_SV] = (jnp.concatenate(dgu, axis=1) * gu_slope).astype(BF16)
        dp_ref[:, C_SV:C_PC] = (dgv * gv_slope).astype(BF16)
        dgs_ref[...] += dgs
        c = cur[:, C_PC:C_GATE]
        ext_ref[0:HALO, :] = jnp.where(not_first, pcp_ref[:, C_PC:C_GATE].astype(F32), 0.0)
        ext_ref[HALO:HALO + BLK, :] = c
        for g, w in enumerate(POOL_WINDOWS):
            sl = slice(BLK * g, BLK * (g + 1))
            acc = _trailing_sums(ext_ref[:, sl], g)[HALO:]
            inv_cnt = 1.0 / _pool_cnt(blk, w)
            pooled16 = (acc * inv_cnt - c[:, sl]).astype(BF16)
            wp16 = wp_ref[g].astype(BF16)
            mixed = _dot(pooled16, wp16, "nn")
            dyc = dbr_ref[2, :, sl]
            dps_ref[:, sl] += jnp.sum(dyc * mixed, axis=0, keepdims=True)
            dmixed16 = (dyc * ps_ref[:, sl]).astype(BF16)
            dwp_ref[g] += _dot(pooled16, dmixed16, "tn")
            dpooled = _dot(dmixed16, wp16, "nt")
            z_ref[HALO:HALO + BLK, sl] = dpooled * inv_cnt
            dext = _leading_sums(z_ref[:, sl], g)[:HALO + BLK]
            dp_ref[:, C_PC + BLK * g:C_PC + BLK * (g + 1)] = (
                dext[HALO:] - dpooled + jnp.concatenate([jnp.zeros((BLK - HALO, BLK), F32), cpc_ref[:, sl]], axis=0)
            ).astype(BF16)
            cpc_ref[:, sl] = dext[:HALO]

    n_in = 11
    small = [jax.ShapeDtypeStruct((8, BLK), F32), jax.ShapeDtypeStruct((4, BLK, BLK), F32),
             jax.ShapeDtypeStruct((4, BLK, 1), F32), jax.ShapeDtypeStruct((1, 512), F32),
             jax.ShapeDtypeStruct((4, BLK, BLK), F32), jax.ShapeDtypeStruct((1, 512), F32)]
    small_specs = [pl.BlockSpec((8, BLK), lambda i: (0, 0)), pl.BlockSpec((4, BLK, BLK), lambda i: (0, 0, 0)),
                   pl.BlockSpec((4, BLK, 1), lambda i: (0, 0, 0)), pl.BlockSpec((1, 512), lambda i: (0, 0)),
                   pl.BlockSpec((4, BLK, BLK), lambda i: (0, 0, 0)), pl.BlockSpec((1, 512), lambda i: (0, 0))]
    res = pl.pallas_call(
        body, name=name, grid=(nb,),
        in_specs=_mix_in_specs(nb, True) + [
            pl.BlockSpec((3, BLK, 512), lambda i: (0, nb - 1 - i, 0)),
            pl.BlockSpec(memory_space=pl.ANY)],
        out_specs=[pl.BlockSpec((BLK, C_GATE), lambda i: (nb - 1 - i, 0))] + small_specs,
        out_shape=[jax.ShapeDtypeStruct(dproj.shape, dproj.dtype)] + small,
        scratch_shapes=[pltpu.VMEM((HALO + BLK, 512), F32), pltpu.VMEM((2 * HALO + BLK, 512), F32),
                        pltpu.VMEM((BLK, 2 * BLK), F32), pltpu.VMEM((HALO, 512), F32)],
        input_output_aliases={n_in - 1: 0},
        compiler_params=_params(("arbitrary",)),
    )(proj, proj, proj, sinks_b, ws, bs3, gsgu, wp, ps, dbr, dproj)
    return tuple(res)


_GW = 256


def _merge_fwd(proj, pb, name, T=4096):
    S, D = pb.shape[1], pb.shape[2]
    T = min(T, S)

    def body(gate_ref, pb_ref, out_ref, acc_ref):
        n = pl.program_id(2)

        @pl.when(n == 0)
        def _():
            acc_ref[...] = jnp.zeros_like(acc_ref)

        acc_ref[...] += jax.nn.sigmoid(gate_ref[...].astype(F32)) * pb_ref[...]

        @pl.when(n == 2)
        def _():
            out_ref[...] = acc_ref[...].astype(BF16)

    return pl.pallas_call(
        body, name=name, grid=(S // T, D // _GW, 3),
        in_specs=[pl.BlockSpec((T, _GW), lambda i, j, n: (i, C_GATE // _GW + n * (D // _GW) + j)),
                  pl.BlockSpec((None, T, _GW), lambda i, j, n: (n, i, j))],
        out_specs=pl.BlockSpec((T, _GW), lambda i, j, n: (i, j)),
        out_shape=jax.ShapeDtypeStruct((S, D), BF16),
        scratch_shapes=[pltpu.VMEM((T, _GW), F32)],
        compiler_params=_params(("parallel", "parallel", "arbitrary")),
    )(proj, pb)


def _merge_bwd(proj, pb, dmerged, name, T=4096):
    S, D = pb.shape[1], pb.shape[2]
    T = min(T, S)

    def body(gate_ref, pb_ref, dm_ref, dgate_ref, dpb_ref):
        sg = jax.nn.sigmoid(gate_ref[...].astype(F32))
        dm = dm_ref[...]
        dpb_ref[...] = (dm * sg).astype(BF16)
        dgate_ref[...] = (dm * pb_ref[...] * sg * (1.0 - sg)).astype(BF16)

    gate_map = lambda i, n, j: (i, C_GATE // _GW + n * (D // _GW) + j)
    return pl.pallas_call(
        body, name=name, grid=(S // T, 3, D // _GW),
        in_specs=[pl.BlockSpec((T, _GW), gate_map),
                  pl.BlockSpec((None, T, _GW), lambda i, n, j: (n, i, j)),
                  pl.BlockSpec((T, _GW), lambda i, n, j: (i, j))],
        out_specs=[pl.BlockSpec((T, _GW), gate_map),
                   pl.BlockSpec((None, T, _GW), lambda i, n, j: (n, i, j))],
        out_shape=[jax.ShapeDtypeStruct((S, C_END), BF16), jax.ShapeDtypeStruct((3, S, D), BF16)],
        compiler_params=_params(("parallel", "parallel", "parallel")),
    )(proj, pb, dmerged)


def _memattn_fwd(qm, kv, name, T=512):
    S, NM = qm.shape[0], kv.shape[0]
    T = min(T, S)

    def body(q_ref, kv_ref, o_ref):
        for h in range(4):
            sl = slice(128 * h, 128 * (h + 1))
            k = kv_ref[:, sl].astype(BF16)
            v = kv_ref[:, 512 + 128 * h:512 + 128 * (h + 1)].astype(BF16)
            s = _dot(k, q_ref[:, sl].astype(BF16), "nt") * MEM_SCALE
            p = _softmax_rows(s)
            o_ref[:, sl] = _dot(p.astype(BF16), v, "tn").astype(BF16)

    return pl.pallas_call(
        body, name=name, grid=(S // T,),
        in_specs=[_row_spec(T, 512), pl.BlockSpec((NM, 1024), lambda i: (0, 0))],
        out_specs=_row_spec(T, 512), out_shape=jax.ShapeDtypeStruct((S, 512), BF16),
        compiler_params=_params(("parallel",)))(qm, kv)


def _memattn_bwd(qm, kv, dom, name, T=512):
    S, NM = qm.shape[0], kv.shape[0]
    T = min(T, S)

    def body(q_ref, kv_ref, do_ref, dq_ref, dkv_ref):
        i = pl.program_id(0)

        @pl.when(i == 0)
        def _():
            dkv_ref[...] = jnp.zeros_like(dkv_ref)

        for h in range(4):
            sl = slice(128 * h, 128 * (h + 1))
            sv_ = slice(512 + 128 * h, 512 + 128 * (h + 1))
            q = q_ref[:, sl].astype(BF16)
            k = kv_ref[:, sl].astype(BF16)
            v = kv_ref[:, sv_].astype(BF16)
            do = do_ref[:, sl].astype(BF16)
            p = _softmax_rows(_dot(k, q, "nt") * MEM_SCALE)
            dp = _dot(v, do, "nt")
            ds = (p * (dp - jnp.sum(p * dp, axis=0, keepdims=True)) * MEM_SCALE).astype(BF16)
            dq_ref[:, sl] = _dot(ds, k, "tn").astype(BF16)
            dkv_ref[:, sl] += _dot(ds, q, "nn")
            dkv_ref[:, sv_] += _dot(p.astype(BF16), do, "nn")

    return pl.pallas_call(
        body, name=name, grid=(S // T,),
        in_specs=[_row_spec(T, 512), pl.BlockSpec((NM, 1024), lambda i: (0, 0)), _row_spec(T, 512)],
        out_specs=[_row_spec(T, 512), pl.BlockSpec((NM, 1024), lambda i: (0, 0))],
        out_shape=[jax.ShapeDtypeStruct((S, 512), BF16), jax.ShapeDtypeStruct((NM, 1024), F32)],
        compiler_params=_params(("arbitrary",)))(qm, kv, dom)


def _adamw(w, g, m, v, name, rows=None, g_row0=None, into=None, TR=512):
    R, C = w.shape
    lo, hi = rows if rows is not None else (0, R)
    g0 = lo if g_row0 is None else g_row0
    TR = _row_tile(math.gcd(math.gcd(lo, g0), hi - lo), TR)
    c1 = 1.0 - ADAM_B1 ** ADAM_STEP
    c2 = 1.0 - ADAM_B2 ** ADAM_STEP

    def body(w_ref, g_ref, m_ref, v_ref, *rest):
        d_ref, nm_ref, nv_ref = rest[-3:]
        gv = g_ref[...]
        nm = ADAM_B1 * m_ref[...] + (1.0 - ADAM_B1) * gv
        nv = ADAM_B2 * v_ref[...] + (1.0 - ADAM_B2) * jnp.square(gv)
        d_ref[...] = -ADAM_LR * ((nm / c1) / (jnp.sqrt(nv / c2) + ADAM_EPS) + ADAM_WD * w_ref[...])
        nm_ref[...] = nm
        nv_ref[...] = nv

    spec = pl.BlockSpec((TR, C), lambda i: (lo // TR + i, 0))
    g_spec = pl.BlockSpec((TR, C), lambda i: (g0 // TR + i, 0))
    prior = list(into) if into is not None else []
    return pl.pallas_call(
        body, name=name, grid=((hi - lo) // TR,),
        in_specs=[spec, g_spec, spec, spec] + [pl.BlockSpec(memory_space=pl.ANY)] * len(prior), out_specs=[spec] * 3,
        out_shape=[jax.ShapeDtypeStruct((R, C), F32)] * 3,
        input_output_aliases={4 + k: k for k in range(len(prior))},
        compiler_params=_params(("parallel",)))(w, g, m, v, *prior)


def _row_tile(R, pref):
    t = (pref // 8) * 8
    while t >= 8:
        if R % t == 0:
            return t
        t -= 8
    raise ValueError(f"no row tile for {R}")


def _sum_slots(stack, name, TR=512):
    n, R, C = stack.shape
    TR = R if R <= TR else _row_tile(R, TR)

    def body(s_ref, o_ref):
        acc = s_ref[0]
        for k in range(1, n):
            acc = acc + s_ref[k]
        o_ref[...] = acc

    return pl.pallas_call(
        body, name=name, grid=(R // TR,),
        in_specs=[pl.BlockSpec((n, TR, C), lambda i: (0, i, 0))],
        out_specs=pl.BlockSpec((TR, C), lambda i: (i, 0)),
        out_shape=jax.ShapeDtypeStruct((R, C), F32),
        compiler_params=_params(("parallel",)))(stack)


_ANY = pl.BlockSpec(memory_space=pl.ANY)


def _chip_of(j, c):
    return (j // 2, j % 2, c)


def _own_slab(shard, dtype, j_arr, name, first=0, count=None, plus=None, deps=(), TR=512):
    N, r, C = shard.shape
    B = N if count is None else count
    rh = r // 2
    TR = rh if rh <= TR else _row_tile(rh, TR)
    nt = rh // TR
    ins = [shard] if plus is None else [shard, plus]

    def body(j_ref, *refs):
        val = refs[0][...] if plus is None else refs[0][...] + refs[1][...]
        refs[-1][...] = val.astype(refs[-1].dtype)

    return pl.pallas_call(
        body, name=name,
        grid_spec=pltpu.PrefetchScalarGridSpec(
            num_scalar_prefetch=1, grid=(B, 2, nt),
            in_specs=[pl.BlockSpec((None, TR, C), lambda b, h, t, jr: (first + b, h * nt + t, 0))] * len(ins)
            + [_ANY] * len(deps),
            out_specs=pl.BlockSpec((None, None, None, TR, C), lambda b, h, t, jr: (b, jr[0], h, t, 0))),
        out_shape=jax.ShapeDtypeStruct((B, 4, 2, rh, C), dtype),
        compiler_params=_params(("parallel", "parallel", "parallel")),
    )(j_arr, *ins, *deps)


def _gather_weights(bufs, name):
    n = len(bufs)

    def body(*refs):
        buf = refs[n:2 * n]
        send_sems, recv_sems, fsend_sems, frecv_sems = refs[2 * n:]
        x, y, c = lax.axis_index("x"), lax.axis_index("y"), lax.axis_index("c")
        j = 2 * x + y
        sib = (x, y, 1 - c)
        sends = []
        for d in range(1, 4):
            for a in range(n):
                cp = pltpu.make_async_remote_copy(
                    src_ref=buf[a].at[:, j, c], dst_ref=buf[a].at[:, j, c], send_sem=send_sems.at[a, d - 1],
                    recv_sem=recv_sems.at[a, d - 1], device_id=_chip_of((j + d) % 4, c), device_id_type=MESH)
                cp.start()
                sends.append(cp)
        for d in range(1, 4):
            frm = (j + 4 - d) % 4
            for a in range(n):
                pltpu.make_async_remote_copy(
                    src_ref=buf[a].at[:, frm, c], dst_ref=buf[a].at[:, frm, c], send_sem=send_sems.at[a, d - 1],
                    recv_sem=recv_sems.at[a, d - 1], device_id=_chip_of(frm, c), device_id_type=MESH).wait_recv()
                cp = pltpu.make_async_remote_copy(
                    src_ref=buf[a].at[:, frm, c], dst_ref=buf[a].at[:, frm, c], send_sem=fsend_sems.at[a, d - 1],
                    recv_sem=frecv_sems.at[a, d - 1], device_id=sib, device_id_type=MESH)
                cp.start()
                sends.append(cp)
        for d in range(1, 4):
            frm = (j + 4 - d) % 4
            for a in range(n):
                pltpu.make_async_remote_copy(
                    src_ref=buf[a].at[:, frm, 1 - c], dst_ref=buf[a].at[:, frm, 1 - c], send_sem=fsend_sems.at[a, d - 1],
                    recv_sem=frecv_sems.at[a, d - 1], device_id=sib, device_id_type=MESH).wait_recv()
        for cp in sends:
            cp.wait_send()

    return pl.pallas_call(
        body, name=name,
        in_specs=[_ANY] * n, out_specs=[_ANY] * n,
        out_shape=[jax.ShapeDtypeStruct(b.shape, b.dtype) for b in bufs],
        scratch_shapes=[pltpu.SemaphoreType.DMA((n, 3))] * 4,
        input_output_aliases={a: a for a in range(n)},
    )(*bufs)


_HBM = pl.BlockSpec(memory_space=pltpu.HBM)
_SEM = pl.BlockSpec(memory_space=pltpu.SEMAPHORE)
_DATAFLOW = pltpu.SideEffectType.DATAFLOW_SIDE_EFFECTING


def _in_hbm(arrays):
    return [pltpu.with_memory_space_constraint(a, pltpu.HBM) for a in arrays]


def _start_copies(bufs, plan, count, name, deps=()):
    n, k = len(bufs), len(deps)

    def body(*refs):
        send_sems, recv_sems = refs[n + k], refs[n + k + 1]
        for i, (src, dst, dev) in enumerate(plan(refs[:n], False)):
            pltpu.make_async_remote_copy(src_ref=src, dst_ref=dst, send_sem=send_sems.at[i], recv_sem=recv_sems.at[i],
                                         device_id=dev, device_id_type=MESH).start()
        refs[-1][...] = jnp.zeros_like(refs[-1])

    return pl.pallas_call(
        body, name=name,
        out_shape=(pltpu.SemaphoreType.DMA((count,)), pltpu.SemaphoreType.DMA((count,)),
                   *[pltpu.HBM(b.shape, b.dtype) for b in bufs], jax.ShapeDtypeStruct((8, 128), F32)),
        in_specs=[_HBM] * n + [_ANY] * k,
        out_specs=(_SEM, _SEM, *[_HBM] * n, pl.BlockSpec(memory_space=pltpu.VMEM)),
        input_output_aliases={a: 2 + a for a in range(n)},
        compiler_params=pltpu.CompilerParams(has_side_effects=_DATAFLOW),
    )(*_in_hbm(bufs), *deps)


def _wait_copies(handle, plan, afters, name):
    send_sems, recv_sems, *bufs = handle[:-1]
    n = len(bufs)

    def body(*refs):
        send_sems, recv_sems = refs[n], refs[n + 1]
        for i, (src, dst, dev) in enumerate(plan(refs[:n], True)):
            cp = pltpu.make_async_remote_copy(src_ref=src, dst_ref=dst, send_sem=send_sems.at[i], recv_sem=recv_sems.at[i],
                                              device_id=dev, device_id_type=MESH)
            cp.wait_send()
            cp.wait_recv()

    return list(pl.pallas_call(
        body, name=name,
        out_shape=[pltpu.HBM(b.shape, b.dtype) for b in bufs],
        in_specs=[_HBM] * n + [_SEM, _SEM] + [_ANY] * len(afters), out_specs=[_HBM] * n,
        input_output_aliases={a: a for a in range(n)},
        compiler_params=pltpu.CompilerParams(has_side_effects=_DATAFLOW),
    )(*bufs, send_sems, recv_sems, *afters))


def _gather_plan(buf, waiting):
    c = lax.axis_index("c")
    j = 2 * lax.axis_index("x") + lax.axis_index("y")
    copies = []
    for d in range(1, 4):
        to, frm = (j + d) % 4, (j + 4 - d) % 4
        for b in buf:
            copies.append((b.at[:, j, c], b.at[:, frm if waiting else j, c], _chip_of(frm if waiting else to, c)))
    return copies


def _chip_plan(buf, waiting):
    n = len(buf) // 2
    c = lax.axis_index("c")
    j = 2 * lax.axis_index("x") + lax.axis_index("y")
    copies = []
    for d in range(1, 4):
        to = (j + d) % 4
        for a in range(n):
            copies.append((buf[a].at[to], buf[n + a].at[d - 1], _chip_of(to, c)))
    return copies


def _pair_plan(buf, waiting):
    n = len(buf) // 2
    c = lax.axis_index("c")
    sib = (lax.axis_index("x"), lax.axis_index("y"), 1 - c)
    return [(buf[a].at[:, pl.ds(1 - c, 1)], buf[n + a], sib) for a in range(n)]


def _gather_forward(bufs, name):
    n = len(bufs)

    def body(*refs):
        buf = refs[n:2 * n]
        send_sems, recv_sems = refs[2 * n:]
        x, y, c = lax.axis_index("x"), lax.axis_index("y"), lax.axis_index("c")
        j = 2 * x + y
        sib = (x, y, 1 - c)
        sends = []
        for d in range(1, 4):
            frm = (j + 4 - d) % 4
            for a in range(n):
                cp = pltpu.make_async_remote_copy(
                    src_ref=buf[a].at[:, frm, c], dst_ref=buf[a].at[:, frm, c], send_sem=send_sems.at[a, d - 1],
                    recv_sem=recv_sems.at[a, d - 1], device_id=sib, device_id_type=MESH)
                cp.start()
                sends.append(cp)
        for d in range(1, 4):
            frm = (j + 4 - d) % 4
            for a in range(n):
                pltpu.make_async_remote_copy(
                    src_ref=buf[a].at[:, frm, 1 - c], dst_ref=buf[a].at[:, frm, 1 - c], send_sem=send_sems.at[a, d - 1],
                    recv_sem=recv_sems.at[a, d - 1], device_id=sib, device_id_type=MESH).wait_recv()
        for cp in sends:
            cp.wait_send()

    return pl.pallas_call(
        body, name=name,
        in_specs=[_ANY] * n, out_specs=[_ANY] * n,
        out_shape=[jax.ShapeDtypeStruct(b.shape, b.dtype) for b in bufs],
        scratch_shapes=[pltpu.SemaphoreType.DMA((n, 3))] * 2,
        input_output_aliases={a: a for a in range(n)},
    )(*bufs)


def _pair_add(g4, r1, cj_arr, name, TR=512):
    B4, _, rh, C = g4.shape
    B = B4 // 4
    TR = rh if rh <= TR else _row_tile(rh, TR)

    def body(cj_ref, g_ref, r_ref, o16_ref, own_ref):
        s = g_ref[...].astype(F32) + r_ref[...].astype(F32)
        o16_ref[...] = s.astype(BF16)

        @pl.when(pl.program_id(2) == cj_ref[1])
        def _():
            own_ref[...] = s

    return pl.pallas_call(
        body, name=name,
        grid_spec=pltpu.PrefetchScalarGridSpec(
            num_scalar_prefetch=1, grid=(B, rh // TR, 4),
            in_specs=[pl.BlockSpec((None, None, TR, C), lambda b, t, p, cj: (b * 4 + p, cj[0], t, 0)),
                      pl.BlockSpec((None, None, TR, C), lambda b, t, p, cj: (b * 4 + p, 0, t, 0))],
            out_specs=[pl.BlockSpec((None, None, TR, C), lambda b, t, p, cj: (p, b, t, 0)),
                       pl.BlockSpec((None, TR, C), lambda b, t, p, cj: (b, t, 0))]),
        out_shape=[jax.ShapeDtypeStruct((4, B, rh, C), BF16), jax.ShapeDtypeStruct((B, rh, C), F32)],
        compiler_params=_params(("parallel", "parallel", "arbitrary")),
    )(cj_arr, g4, r1)


def _chip_add(own, r2, cj_arr, into, first, name, TR=512):
    B, rh, C = own.shape
    TR = rh if rh <= TR else _row_tile(rh, TR)

    def body(cj_ref, p_ref, r_ref, _into_ref, o_ref):
        o_ref[...] = p_ref[...] + r_ref[0].astype(F32) + r_ref[1].astype(F32) + r_ref[2].astype(F32)

    return pl.pallas_call(
        body, name=name,
        grid_spec=pltpu.PrefetchScalarGridSpec(
            num_scalar_prefetch=1, grid=(B, rh // TR),
            in_specs=[pl.BlockSpec((None, TR, C), lambda b, t, cj: (b, t, 0)),
                      pl.BlockSpec((3, None, TR, C), lambda b, t, cj: (0, b, t, 0)),
                      _ANY],
            out_specs=pl.BlockSpec((None, None, TR, C), lambda b, t, cj: (first + b, cj[0], t, 0))),
        out_shape=jax.ShapeDtypeStruct(into.shape, F32),
        input_output_aliases={3: 0},
        compiler_params=_params(("parallel", "parallel")),
    )(cj_arr, own, r2, into)


def _pair_share(bufs, spans, name, deps=()):
    n = len(bufs)

    def body(*refs):
        buf = refs[n + len(deps):2 * n + len(deps)]
        send_sems, recv_sems = refs[2 * n + len(deps):]
        c = lax.axis_index("c")
        sib = (lax.axis_index("x"), lax.axis_index("y"), 1 - c)
        cps = []
        for a, (first, count) in enumerate(spans):
            cp = pltpu.make_async_remote_copy(
                src_ref=buf[a].at[pl.ds(first, count), c], dst_ref=buf[a].at[pl.ds(first, count), c],
                send_sem=send_sems.at[a], recv_sem=recv_sems.at[a], device_id=sib, device_id_type=MESH)
            cp.start()
            cps.append(cp)
        for a, (first, count) in enumerate(spans):
            pltpu.make_async_remote_copy(
                src_ref=buf[a].at[pl.ds(first, count), 1 - c], dst_ref=buf[a].at[pl.ds(first, count), 1 - c],
                send_sem=send_sems.at[a], recv_sem=recv_sems.at[a], device_id=sib, device_id_type=MESH).wait_recv()
        for cp in cps:
            cp.wait_send()

    return pl.pallas_call(
        body, name=name, in_specs=[_ANY] * (n + len(deps)), out_specs=[_ANY] * n,
        out_shape=[jax.ShapeDtypeStruct(b.shape, b.dtype) for b in bufs],
        scratch_shapes=[pltpu.SemaphoreType.DMA((n,)), pltpu.SemaphoreType.DMA((n,))],
        input_output_aliases={a: a for a in range(n)},
    )(*bufs, *deps)


def _pair_swap(arr, name):
    def body(src, dst, send_sem, recv_sem):
        sib = (lax.axis_index("x"), lax.axis_index("y"), 1 - lax.axis_index("c"))
        cp = pltpu.make_async_remote_copy(src_ref=src, dst_ref=dst, send_sem=send_sem, recv_sem=recv_sem,
                                          device_id=sib, device_id_type=MESH)
        cp.start()
        cp.wait_recv()
        cp.wait_send()

    return pl.pallas_call(
        body, name=name, in_specs=[_ANY], out_specs=_ANY,
        out_shape=jax.ShapeDtypeStruct(arr.shape, arr.dtype),
        scratch_shapes=[pltpu.SemaphoreType.DMA, pltpu.SemaphoreType.DMA],
    )(arr)


class _ReduceScatter:
    def __init__(self, n_layers, cj_arr):
        self.L, self.cj = n_layers, cj_arr
        self.total = None
        self.pair = None
        self.chip = None

    def _land(self, after):
        handle, layer, owns = self.chip
        n = len(owns)
        r2 = _wait_copies(handle, _chip_plan, after if isinstance(after, (list, tuple)) else (after,), "rs_chip_wait")[n:]
        if self.total is None:
            self.total = [lax.empty((self.L * o.shape[0], 2) + o.shape[1:], F32) for o in owns]
        self.total = [_chip_add(o, r, self.cj, t, layer * o.shape[0], "rs_chip_add")
                      for o, r, t in zip(owns, r2, self.total)]
        self.chip = None

    def add_layer(self, layer, grads):
        g4 = [g.reshape(g.shape[0] * 4, 2, g.shape[1] // 8, g.shape[2]) for g in grads]
        lands = [lax.empty((g.shape[0], 1) + g.shape[2:], g.dtype) for g in g4]
        handle = _start_copies(g4 + lands, _pair_plan, len(g4), "rs_pair_start")
        self.pair = (handle, layer)
        return (handle[-1],)

    def advance(self, after):
        if self.pair is None:
            return ()
        handle, layer = self.pair
        both = _wait_copies(handle, _pair_plan, (after,), "rs_pair_wait")
        n = len(both) // 2
        added = [_pair_add(g, r, self.cj, "rs_pair_add") for g, r in zip(both[:n], both[n:])]
        parts, owns = [p for p, _ in added], [o for _, o in added]
        lands = [lax.empty((3,) + p.shape[1:], p.dtype) for p in parts]
        handle = _start_copies(parts + lands, _chip_plan, 3 * n, "rs_chip_start")
        if self.chip is not None:
            self._land(handle[-1])
        self.pair, self.chip = None, (handle, layer, owns)
        return (handle[-1],)

    def upper_layers(self, deps):
        per = [t.shape[0] // self.L for t in self.total]
        self.total = _pair_share(self.total, [(b, (self.L - 1) * b) for b in per], "rs_pair_share_upper", deps=deps)
        return [t.reshape(t.shape[0], t.shape[1] * t.shape[2], t.shape[3]) for t in self.total]

    def first_layer(self, after):
        self._land(after)
        per = [t.shape[0] // self.L for t in self.total]
        full = _pair_share(self.total, [(0, b) for b in per], "rs_pair_share_first")
        return [f.reshape(f.shape[0], f.shape[1] * f.shape[2], f.shape[3]) for f in full]


def _relu2_epi(acc):
    return (jnp.square(jnp.maximum(acc, 0.0)),)


def _relu2_bwd_epi(acc, a):
    return (acc * (2.0 * jnp.sqrt(a.astype(F32))),)


_GRAD_ORDER = ("winT", "wbT", "wout", "wq", "wkv", "woT", "wupT", "wdown")
_DW = dict(tm=512, tn=1024, tk=4096, out_dtypes=(BF16,))
_LONG_K = dict(tm=1024, tn=1024, tk=2048)


def _forward_backward(x, mem, target, input_weight_of, weights_of, P, grads_done, grads_advance):
    L = P["g_norm"].shape[0]
    S, D = x.shape
    gn = lambda l, i: P["g_norm"][l, i][None]

    saved = []
    (h,) = _resnorm_fwd(x, None, None, gn(0, 0), "norm_in")
    xr = x
    for l in range(L):
        w_in_t = input_weight_of(l, xr)
        proj = _mm(h, w_in_t, "nt", "in_proj", b_pre=(0,), out_dtypes=(BF16,), tn=1792)
        W, w_deps = weights_of(l, proj)
        small = (jnp.broadcast_to(P["sinks"][l][:, None], (8, BLK)), P["ws"][l], P["bs"][l][:, :, None],
                 P["gsgu"][l][None], P["wp"][l], P["ps"][l][None])
        br = _mix_fwd(proj, *small, "mix_fwd")
        pb = _mm(br, W["wbT"], "nt", "branch_proj", batch=3, out_dtypes=(BF16,), deps=w_deps)
        merged = _merge_fwd(proj, pb, "merge_fwd")
        z = _mm(merged, W["wout"], "nn", "out_proj", b_pre=(0,))
        x1, hm = _resnorm_fwd(xr, z, gn(l, 1), gn(l, 2), "resnorm_fwd")
        qm = _mm(hm, W["wq"], "nn", "mem_q", b_pre=(0,))
        (memn,) = _resnorm_fwd(mem, None, None, P["g_mem"][l][None], "mem_norm")
        kv = _mm(memn, W["wkv"], "nn", "mem_kv", b_pre=(0,))
        om = _memattn_fwd(qm, kv, "memattn_fwd")
        ym = _mm(om, W["woT"], "nt", "mem_o", b_pre=(0,))
        x2, hf = _resnorm_fwd(x1, ym, gn(l, 3), gn(l, 4), "resnorm_fwd")
        a = _mm(hf, W["wupT"], "nt", "mlp_up", b_pre=(0,), out_dtypes=(BF16,), epi=_relu2_epi)
        yf = _mm(a, W["wdown"], "nn", "mlp_down", b_pre=(0,), **_LONG_K)
        saved.append(dict(W=W, x0=xr, h=h, proj=proj, small=small, br=br, pb=pb, merged=merged, z=z, x1=x1, hm=hm,
                          qm=qm, memn=memn, kv=kv, om=om, ym=ym, x2=x2, hf=hf, a=a, yf=yf))
        if l < L - 1:
            xr, h = _resnorm_fwd(x2, yf, gn(l, 5), gn(l + 1, 0), "resnorm_fwd")
    dres, loss = _final_fwd(saved[-1]["x2"], saved[-1]["yf"], gn(L - 1, 5), target, "loss_head")

    dgn = [[None] * 6 for _ in range(L)]
    dsmall = {k: [None] * L for k in ("g_mem", "sinks", "ws", "bs", "gsgu", "wp", "ps")}
    dh = None
    for l in reversed(range(L)):
        s = saved[l]
        W, G = s["W"], {}
        if l == L - 1:
            dx2, dyf, dgn[l][5] = _resnorm_bwd(s["x2"], s["yf"], gn(l, 5), None, dres, None, "resnorm_bwd_top")
        else:
            dx2, dyf, dgn[l][5], dgn[l + 1][0] = _resnorm_bwd(s["x2"], s["yf"], gn(l, 5), gn(l + 1, 0), dres, dh,
                                                              "resnorm_bwd", deps=deps)
        du = _mm(dyf, W["wdown"], "nt", "mlp_down_dx", b_pre=(0,), out_dtypes=(BF16,), extras=(s["a"],), epi=_relu2_bwd_epi)
        G["wdown"] = _mm(s["a"], dyf, "tn", "mlp_down_dw", **_DW)[None]
        dhf = _mm(du, W["wupT"], "nn", "mlp_up_dx", b_pre=(0,), **_LONG_K)
        G["wupT"] = _mm(du, s["hf"], "tn", "mlp_up_dw", **_DW)[None]
        dx1, dym, dgn[l][3], dgn[l][4] = _resnorm_bwd(s["x1"], s["ym"], gn(l, 3), gn(l, 4), dx2, dhf, "resnorm_bwd")
        dom = _mm(dym, W["woT"], "nn", "mem_o_dx", b_pre=(0,), deps=grads_advance(dx1))
        G["woT"] = _mm(dym, s["om"], "tn", "mem_o_dw", **_DW)[None]
        dqm, dkv = _memattn_bwd(s["qm"], s["kv"], dom, "memattn_bwd")
        dmemn = _mm(dkv, W["wkv"], "nt", "mem_kv_dx", b_pre=(0,))
        G["wkv"] = _mm(s["memn"], dkv, "tn", "mem_kv_dw", out_dtypes=(BF16,))[None]
        _, dsmall["g_mem"][l] = _resnorm_bwd(mem, None, None, P["g_mem"][l][None], None, dmemn, "mem_norm_bwd")
        dhm = _mm(dqm, W["wq"], "nt", "mem_q_dx", b_pre=(0,))
        G["wq"] = _mm(s["hm"], dqm, "tn", "mem_q_dw", **_DW)[None]
        dx0, dz, dgn[l][1], dgn[l][2] = _resnorm_bwd(s["x0"], s["z"], gn(l, 1), gn(l, 2), dx1, dhm, "resnorm_bwd")
        dmerged = _mm(dz, W["wout"], "nt", "out_proj_dx", b_pre=(0,))
        G["wout"] = _mm(s["merged"], dz, "tn", "out_proj_dw", **_DW)[None]
        dproj, dpb = _merge_bwd(s["proj"], s["pb"], dmerged, "merge_bwd")
        dbr = _mm(dpb, W["wbT"], "nn", "branch_proj_dx", batch=3)
        G["wbT"] = _mm(dpb, s["br"], "tn", "branch_proj_dw", batch=3, **_DW)
        (dproj, dsmall["sinks"][l], dsmall["ws"][l], dsmall["bs"][l], dsmall["gsgu"][l], dsmall["wp"][l],
         dsmall["ps"][l]) = _mix_bwd(s["proj"], dbr, dproj, *s["small"], "mix_bwd")
        dh = _mm(dproj, W["winT"], "nn", "in_proj_dx", b_pre=(0,), **_LONG_K)
        G["winT"] = _mm(dproj, s["h"], "tn", "in_proj_dw", **_DW)[None]
        deps = grads_done(l, [G[k] for k in _GRAD_ORDER])
        dres = dx0
    grad_x, dgn[0][0] = _resnorm_bwd(x, None, None, gn(0, 0), dres, dh, "norm_in_bwd", deps=deps)
    tail_deps = grads_advance(grad_x)

    small_grads = dict(
        g_norm=jnp.stack([jnp.concatenate(row, axis=0) for row in dgn]),
        g_mem=jnp.concatenate(dsmall["g_mem"], axis=0),
        sinks=jnp.stack([d[:, 0] for d in dsmall["sinks"]]),
        ws=jnp.stack(dsmall["ws"]),
        bs=jnp.stack([d[:, :, 0] for d in dsmall["bs"]]),
        gsgu=jnp.concatenate(dsmall["gsgu"], axis=0),
        wp=jnp.stack(dsmall["wp"]),
        ps=jnp.concatenate(dsmall["ps"], axis=0),
    )
    return loss, grad_x, small_grads, tail_deps


_PACK_ROWS = 512


def _as_rows(a):
    n = math.prod(a.shape)
    if n % 128:
        a = jnp.pad(a.reshape(-1), (0, (-n) % 128))
    r = a.reshape(-1, 128)
    return jnp.pad(r, ((0, (-r.shape[0]) % 8), (0, 0))) if r.shape[0] % 8 else r


def _pack(arrays):
    rows = [_as_rows(a) for a in arrays]
    total = sum(r.shape[0] for r in rows)
    tail = (-total) % _PACK_ROWS
    if tail:
        rows.append(jnp.zeros((tail, 128), rows[0].dtype))
    return jnp.concatenate(rows, axis=0)


def _unpack(packed, like):
    out, pos = [], 0
    for a in like:
        n = math.prod(a.shape)
        nr = -(-n // 128)
        rows = packed[pos:pos + nr]
        out.append((rows.reshape(-1)[:n] if n % 128 else rows).reshape(a.shape))
        pos += nr + (-nr) % 8
    return out


_BIG = ("w_in", "w_branch", "w_out", "w_q_mem", "w_kv_mem", "w_o_mem", "w_up", "w_down")
_SMALL = ("g_norm", "g_mem", "attn_sinks", "w_spatial", "b_spatial", "g_sgu", "w_pool", "pool_scale")
_WEIGHTS = ("g_norm", "g_mem", "w_in", "attn_sinks", "w_spatial", "b_spatial", "g_sgu", "w_pool", "pool_scale",
            "w_branch", "w_out", "w_q_mem", "w_kv_mem", "w_o_mem", "w_up", "w_down")


def _to_working(name, w):
    if name == "w_in":
        return jnp.swapaxes(w, 1, 2)
    if name == "w_branch":
        t = jnp.swapaxes(w, 2, 3)
        return t.reshape(t.shape[0] * 3, t.shape[2], t.shape[3])
    if name in ("w_o_mem", "w_up"):
        return jnp.swapaxes(w, 1, 2)
    return w


def _from_working(name, g):
    if name == "w_in":
        return jnp.swapaxes(g, 1, 2)
    if name == "w_branch":
        return jnp.swapaxes(g.reshape(g.shape[0] // 3, 3, g.shape[1], g.shape[2]), 2, 3)
    if name in ("w_o_mem", "w_up"):
        return jnp.swapaxes(g, 1, 2)
    return g


_WKEY = dict(w_in="winT", w_branch="wbT", w_out="wout", w_q_mem="wq", w_kv_mem="wkv", w_o_mem="woT",
             w_up="wupT", w_down="wdown")


def kernel(x, mem, g_norm, g_mem, w_in, attn_sinks, w_spatial, b_spatial, g_sgu, w_pool, pool_scale, w_branch, w_out, w_q_mem, w_kv_mem, w_o_mem, w_up, w_down, loss_target, m_g_norm, m_g_mem, m_w_in, m_attn_sinks, m_w_spatial, m_b_spatial, m_g_sgu, m_w_pool, m_pool_scale, m_w_branch, m_w_out, m_w_q_mem, m_w_kv_mem, m_w_o_mem, m_w_up, m_w_down, v_g_norm, v_g_mem, v_w_in, v_attn_sinks, v_w_spatial, v_b_spatial, v_g_sgu, v_w_pool, v_pool_scale, v_w_branch, v_w_out, v_w_q_mem, v_w_kv_mem, v_w_o_mem, v_w_up, v_w_down):
    w = dict(g_norm=g_norm, g_mem=g_mem, w_in=w_in, attn_sinks=attn_sinks, w_spatial=w_spatial, b_spatial=b_spatial,
             g_sgu=g_sgu, w_pool=w_pool, pool_scale=pool_scale, w_branch=w_branch, w_out=w_out, w_q_mem=w_q_mem,
             w_kv_mem=w_kv_mem, w_o_mem=w_o_mem, w_up=w_up, w_down=w_down)
    m = dict(g_norm=m_g_norm, g_mem=m_g_mem, w_in=m_w_in, attn_sinks=m_attn_sinks, w_spatial=m_w_spatial,
             b_spatial=m_b_spatial, g_sgu=m_g_sgu, w_pool=m_w_pool, pool_scale=m_pool_scale, w_branch=m_w_branch,
             w_out=m_w_out, w_q_mem=m_w_q_mem, w_kv_mem=m_w_kv_mem, w_o_mem=m_w_o_mem, w_up=m_w_up, w_down=m_w_down)
    v = dict(g_norm=v_g_norm, g_mem=v_g_mem, w_in=v_w_in, attn_sinks=v_attn_sinks, w_spatial=v_w_spatial,
             b_spatial=v_b_spatial, g_sgu=v_g_sgu, w_pool=v_w_pool, pool_scale=v_pool_scale, w_branch=v_w_branch,
             w_out=v_w_out, w_q_mem=v_w_q_mem, w_kv_mem=v_w_kv_mem, w_o_mem=v_w_o_mem, w_up=v_w_up, w_down=v_w_down)
    L = g_norm.shape[0]
    j = 2 * lax.axis_index("x") + lax.axis_index("y")
    c = lax.axis_index("c")
    j_arr = jnp.reshape(j, (1,)).astype(jnp.int32)
    cj_arr = jnp.stack([c, j]).astype(jnp.int32)

    gs = g_norm.shape[2]
    working = [_to_working(n, w[n]) for n in _BIG]
    per_layer = [wk.shape[0] // L for wk in working]

    def own_slabs(l):
        return [_own_slab(wk, BF16, j_arr, "own_slab", first=l * b, count=b) for wk, b in zip(working, per_layer)]

    first_slabs = own_slabs(0)
    lead = [first_slabs[0], _own_slab(g_norm.reshape(1, L * 6 * gs // 128, 128), F32, j_arr, "own_slab_norm")]
    lead_handle = _start_copies(lead, _gather_plan, 3 * len(lead), "gather_start_lead")
    rest_handle = _start_copies(first_slabs[1:], _gather_plan, 3 * (len(first_slabs) - 1), "gather_start_first",
                                deps=(lead_handle[-1],))
    slabs = {l: own_slabs(l) for l in range(1, L)}
    lead = _gather_forward(_wait_copies(lead_handle, _gather_plan,
                                        [rest_handle[-1]] + [s for l in slabs for s in slabs[l]],
                                        "gather_wait_lead"), "gather_forward_lead")
    gn_full = jnp.transpose(lead[1].reshape(4, L * 6, gs), (1, 0, 2)).reshape(L, 6, 4 * gs)
    P = dict(g_norm=gn_full, g_mem=g_mem, sinks=attn_sinks, ws=w_spatial, bs=b_spatial, gsgu=g_sgu, wp=w_pool,
             ps=pool_scale)
    whole = lambda g: g.reshape(g.shape[0], 8 * g.shape[3], g.shape[4])
    gathered, in_flight = {}, {}

    def layer_weights(l, after):
        if l not in gathered:
            gathered[l] = _gather_forward(_wait_copies(in_flight[l], _gather_plan, (after,), "gather_wait"),
                                          "gather_forward")
        return gathered[l]

    def input_weight_of(l, after):
        return whole(lead[0] if l == 0 else layer_weights(l, after)[0])

    def weights_of(l, after):
        deps = ()
        if l == 0:
            rest = _gather_forward(_wait_copies(rest_handle, _gather_plan, (after,), "gather_wait_first"),
                                   "gather_forward_first")
            gathered[0] = [lead[0], *rest]
            dep = rest[0]
            for k in range(1, L):
                in_flight[k] = _start_copies(slabs[k], _gather_plan, 3 * len(slabs[k]), "gather_start", deps=(dep,))
                dep = in_flight[k][-1]
            deps = tuple(h[-1] for h in in_flight.values())
        return {k: whole(g) for k, g in zip(_GRAD_ORDER, layer_weights(l, after))}, deps

    rs = _ReduceScatter(L, cj_arr)
    loss_part, grad_x, sg, tail_deps = _forward_backward(
        x[0], mem[0], loss_target[0], input_weight_of, weights_of, P, rs.add_layer, rs.advance)
    loss = lax.psum(loss_part[0, 0], ("x", "y", "c"))

    transposed = ("w_branch", "w_o_mem", "w_up")
    view = {n: (lambda t: jnp.swapaxes(t, 1, 2)) if n == "w_in" else (lambda t: t) for n in _BIG}
    rows = lambda n, t: view[n](t).reshape(-1, view[n](t).shape[-1])
    state = {n: (rows(n, w[n]), rows(n, m[n]), rows(n, v[n])) for n in _BIG}
    updated, upper_grad = {}, {}
    if L > 1:
        for n, gw in zip(_BIG, rs.upper_layers(tail_deps)):
            per = gw.shape[0] // L
            if n in transposed:
                upper_grad[n] = _from_working(n, gw[per:])
                g2d, g_row0 = upper_grad[n].reshape(-1, upper_grad[n].shape[-1]), 0
            else:
                g2d, g_row0 = gw.reshape(-1, gw.shape[-1]), None
            r_layer = state[n][0].shape[0] // L
            updated[n] = _adamw(state[n][0], g2d, state[n][1], state[n][2], "adamw_upper",
                                rows=(r_layer, L * r_layer), g_row0=g_row0)
    full_small = [sg["g_norm"], sg["g_mem"], sg["sinks"], sg["ws"], sg["bs"], sg["gsgu"], sg["wp"], sg["ps"]]
    packed = _pack(full_small)
    pair_sum = _own_slab(packed[None], F32, j_arr, "small_grads_pair_sum",
                         deps=tuple(tail_deps) + tuple(u[0] for u in updated.values()),
                         plus=_pair_swap(packed, "small_grads_swap")[None])
    (chip_sums,) = _gather_weights([pair_sum], "gather_small_grads")
    total = _sum_slots(chip_sums.reshape(4, *packed.shape), "sum_small_grads")
    grads = {}
    for n, g in zip(_SMALL, _unpack(total, full_small)):
        grads[n] = lax.dynamic_slice_in_dim(g, j * g_norm.shape[2], g_norm.shape[2], axis=2) if n == "g_norm" else g

    after = [total] + [u[0] for u in updated.values()]
    delta, new_m, new_v = {}, {}, {}
    for n, gw in zip(_BIG, rs.first_layer(after)):
        per = gw.shape[0] // L
        if n in transposed:
            g0 = _from_working(n, gw[:per])
            grads[n] = jnp.concatenate([g0, upper_grad[n]], axis=0) if L > 1 else g0
            g2d = g0.reshape(-1, g0.shape[-1])
        else:
            grads[n] = _from_working(n, gw)
            g2d = gw.reshape(-1, gw.shape[-1])
        r_layer = state[n][0].shape[0] // L
        d_, m_, v_ = _adamw(state[n][0], g2d, state[n][1], state[n][2], "adamw_first", rows=(0, r_layer), g_row0=0,
                            into=updated.get(n))
        shp = view[n](w[n]).shape
        delta[n], new_m[n], new_v[n] = view[n](d_.reshape(shp)), view[n](m_.reshape(shp)), view[n](v_.reshape(shp))
    small_w = [w[n] for n in _SMALL]
    d_, m_, v_ = _adamw(_pack(small_w), _pack([grads[n] for n in _SMALL]), _pack([m[n] for n in _SMALL]),
                        _pack([v[n] for n in _SMALL]), "adamw_small")
    for n, dd, mm_, vv in zip(_SMALL, _unpack(d_, small_w), _unpack(m_, small_w), _unpack(v_, small_w)):
        delta[n], new_m[n], new_v[n] = dd, mm_, vv

    return (loss, grad_x[None], *[grads[n] for n in _WEIGHTS], *[delta[n] for n in _WEIGHTS],
            *[new_m[n] for n in _WEIGHTS], *[new_v[n] for n in _WEIGHTS])
```

```python
import functools
import math

import jax
import jax.numpy as jnp
from jax import lax
from jax.experimental import pallas as pl
from jax.experimental.pallas import tpu as pltpu

F32 = jnp.float32
BF16 = jnp.bfloat16
MESH = pl.DeviceIdType.MESH

EPS = 1e-6
NEG_INF = -1e30
BLK = 128
HALO = 16
POOL_WINDOWS = (2, 4, 8, 16)
ATT_SCALE = 1.0 / math.sqrt(64.0)
MEM_SCALE = 1.0 / math.sqrt(128.0)
C_Q, C_K, C_V, C_SU, C_SV, C_PC, C_GATE, C_END = 0, 512, 640, 768, 1280, 1792, 2304, 5376

ADAM_LR, ADAM_B1, ADAM_B2, ADAM_EPS, ADAM_WD, ADAM_STEP = 0.001, 0.9, 0.999, 1e-08, 0.01, 10

VMEM_LIMIT_BYTES = 56 * 1024 * 1024

_DIMS = {
    "nn": (((1,), (0,)), ((), ())),
    "nt": (((1,), (1,)), ((), ())),
    "tn": (((0,), (0,)), ((), ())),
}


def _dot(a, b, mode):
    return lax.dot_general(a, b, _DIMS[mode], preferred_element_type=F32)


def _params(semantics):
    return pltpu.CompilerParams(dimension_semantics=semantics, vmem_limit_bytes=VMEM_LIMIT_BYTES)


def _tile(dim, pref):
    if dim <= pref:
        return dim
    t = (pref // 128) * 128
    while t >= 128:
        if dim % t == 0:
            return t
        t -= 128
    raise ValueError(f"no tile for {dim}")


def _rms(x, g):
    return x * lax.rsqrt(jnp.mean(x * x, axis=-1, keepdims=True) + EPS) * g


def _mm(a, b, mode, name, *, out_dtypes=(F32,), a_pre=(), b_pre=(), into=None, out_pre=(),
        extras=(), epi=None, deps=(), batch=None, tm=2048, tn=1024, tk=1024):
    lead = 0 if batch is None else 1
    assert not (lead and (a_pre or b_pre or into is not None or extras))
    a2, b2 = a.shape[len(a_pre) + lead:], b.shape[len(b_pre) + lead:]
    if mode == "nn":
        (M, K), (K2, N) = a2, b2
    elif mode == "nt":
        (M, K), (N, K2) = a2, b2
    else:
        (K, M), (K2, N) = a2, b2
    assert K == K2, (a.shape, b.shape, mode)
    tm, tn, tk = _tile(M, tm), _tile(N, tn), _tile(K, tk)
    nk = K // tk
    na, nb_, no = len(a_pre), len(b_pre), len(out_pre)

    def on_grid(f):
        return f if batch is None else (lambda g, i, j, k: (g,) + f(i, j, k))

    if mode == "tn":
        a_spec = pl.BlockSpec((None,) * (na + lead) + (tk, tm), on_grid(lambda i, j, k: a_pre + (k, i)))
    else:
        a_spec = pl.BlockSpec((None,) * (na + lead) + (tm, tk), on_grid(lambda i, j, k: a_pre + (i, k)))
    if mode == "nt":
        b_spec = pl.BlockSpec((None,) * (nb_ + lead) + (tn, tk), on_grid(lambda i, j, k: b_pre + (j, k)))
    else:
        b_spec = pl.BlockSpec((None,) * (nb_ + lead) + (tk, tn), on_grid(lambda i, j, k: b_pre + (k, j)))
    tile_spec = pl.BlockSpec((None,) * lead + (tm, tn), on_grid(lambda i, j, k: (i, j)))
    ne, nout = len(extras), len(out_dtypes)
    in_specs = [a_spec, b_spec] + [tile_spec] * ne
    operands = [a, b, *extras]
    aliases = {}
    if into is not None:
        assert nout == 1
        in_specs.append(pl.BlockSpec(memory_space=pl.ANY))
        operands.append(into)
        aliases = {len(operands) - 1: 0}
        out_shape = [jax.ShapeDtypeStruct(into.shape, into.dtype)]
        out_specs = [pl.BlockSpec((None,) * no + (tm, tn), lambda i, j, k: out_pre + (i, j))]
    else:
        out_shape = [jax.ShapeDtypeStruct(((batch,) if lead else ()) + (M, N), dt) for dt in out_dtypes]
        out_specs = [tile_spec] * nout
    in_specs += [pl.BlockSpec(memory_space=pl.ANY)] * len(deps)
    operands += list(deps)

    def body(*refs):
        a_ref, b_ref = refs[0], refs[1]
        ex = refs[2:2 + ne]
        pos = 2 + ne + (1 if into is not None else 0) + len(deps)
        outs = refs[pos:pos + nout]
        acc_ref = refs[pos + nout] if nk > 1 else None

        def finish(acc):
            vals = epi(acc, *[e[...] for e in ex]) if epi is not None else (acc,)
            for o, v in zip(outs, vals):
                o[...] = v.astype(o.dtype)

        def prod():
            return _dot(a_ref[...].astype(BF16), b_ref[...].astype(BF16), mode)

        if nk == 1:
            finish(prod())
        else:
            k = pl.program_id(2 + lead)

            @pl.when(k == 0)
            def _():
                acc_ref[...] = jnp.zeros_like(acc_ref)

            acc_ref[...] += prod()

            @pl.when(k == nk - 1)
            def _():
                finish(acc_ref[...])

    res = pl.pallas_call(
        body, name=name, grid=((batch,) if lead else ()) + (M // tm, N // tn, nk),
        in_specs=in_specs, out_specs=out_specs, out_shape=out_shape,
        scratch_shapes=[pltpu.VMEM((tm, tn), F32)] if nk > 1 else [],
        input_output_aliases=aliases,
        compiler_params=_params(("parallel",) * (2 + lead) + ("arbitrary",)),
    )(*operands)
    return res[0] if nout == 1 else tuple(res)


def _resnorm_fn(has_post, has_pre):
    def f(*a):
        x, k = a[0], 1
        if has_post:
            x, k = x + _rms(a[1], a[2]), 3
        outs = [x]
        if has_pre:
            outs.append(_rms(x, a[k]))
        return tuple(outs)
    return f


def _row_spec(T, W):
    return pl.BlockSpec((T, W), lambda i: (i, 0))


def _par_spec(W):
    return pl.BlockSpec((1, W), lambda i: (0, 0))


def _resnorm_fwd(xr, y, gp, gq, name, T=512, deps=()):
    S, D = xr.shape
    T = min(T, S)
    has_post, has_pre = y is not None, gq is not None
    f = _resnorm_fn(has_post, has_pre)
    ins = [xr] + ([y, gp] if has_post else []) + ([gq] if has_pre else [])
    in_specs = [_row_spec(T, D)] + ([_row_spec(T, D), _par_spec(D)] if has_post else []) + ([_par_spec(D)] if has_pre else [])
    out_shape, out_specs = [], []
    if has_post:
        out_shape.append(jax.ShapeDtypeStruct((S, D), F32)); out_specs.append(_row_spec(T, D))
    if has_pre:
        out_shape.append(jax.ShapeDtypeStruct((S, D), BF16)); out_specs.append(_row_spec(T, D))
    n_in, n_dep = len(ins), len(deps)

    def body(*refs):
        vals = f(*[r[...] for r in refs[:n_in]])
        outs = list(refs[n_in + n_dep:])
        if has_post:
            outs.pop(0)[...] = vals[0]
        if has_pre:
            outs.pop(0)[...] = vals[1].astype(BF16)

    res = pl.pallas_call(body, name=name, grid=(S // T,),
                         in_specs=in_specs + [pl.BlockSpec(memory_space=pl.ANY)] * n_dep, out_specs=out_specs,
                         out_shape=out_shape, compiler_params=_params(("parallel",)))(*ins, *deps)
    return tuple(res)


def _resnorm_bwd(xr, y, gp, gq, dres, dh, name, T=512, deps=()):
    S, D = xr.shape
    T = min(T, S)
    has_post, has_pre, has_res = y is not None, gq is not None, dres is not None
    f = _resnorm_fn(has_post, has_pre)
    ins = [xr] + ([y, gp] if has_post else []) + ([gq] if has_pre else [])
    in_specs = [_row_spec(T, D)] + ([_row_spec(T, D), _par_spec(D)] if has_post else []) + ([_par_spec(D)] if has_pre else [])
    n_prim = len(ins)
    if has_res:
        ins.append(dres); in_specs.append(_row_spec(T, D))
    if has_pre:
        ins.append(dh); in_specs.append(_row_spec(T, D))
    n_in, n_dep = len(ins), len(deps)
    out_shape = [jax.ShapeDtypeStruct((S, D), F32)]
    out_specs = [_row_spec(T, D)]
    if has_post:
        out_shape += [jax.ShapeDtypeStruct((S, D), BF16), jax.ShapeDtypeStruct((1, D), F32)]
        out_specs += [_row_spec(T, D), _par_spec(D)]
    if has_pre:
        out_shape.append(jax.ShapeDtypeStruct((1, D), F32)); out_specs.append(_par_spec(D))

    def body(*refs):
        i = pl.program_id(0)
        prim = [r[...] for r in refs[:n_prim]]
        rest = list(refs[n_prim:n_in])
        ct_x = rest.pop(0)[...] if has_res else jnp.zeros((T, D), F32)
        cts = [ct_x]
        if has_pre:
            cts.append(rest.pop(0)[...].astype(F32))
        _, vjp = jax.vjp(f, *prim)
        grads = list(vjp(tuple(cts)))
        outs = list(refs[n_in + n_dep:])
        outs.pop(0)[...] = grads.pop(0)
        acc = []
        if has_post:
            outs.pop(0)[...] = grads.pop(0).astype(BF16)
            acc.append((outs.pop(0), grads.pop(0)))
        if has_pre:
            acc.append((outs.pop(0), grads.pop(0)))

        @pl.when(i == 0)
        def _():
            for o, _g in acc:
                o[...] = jnp.zeros_like(o)

        for o, g in acc:
            o[...] += g

    res = pl.pallas_call(body, name=name, grid=(S // T,),
                         in_specs=in_specs + [pl.BlockSpec(memory_space=pl.ANY)] * n_dep, out_specs=out_specs,
                         out_shape=out_shape, compiler_params=_params(("arbitrary",)))(*ins, *deps)
    return tuple(res)


def _final_fwd(xr, y, gp, target, name, T=512):
    S, D = xr.shape
    T = min(T, S)

    def body(x_ref, y_ref, g_ref, t_ref, dy_ref, loss_ref):
        i = pl.program_id(0)
        e = x_ref[...] + _rms(y_ref[...], g_ref[...]) - t_ref[...]
        dy_ref[...] = e / D

        @pl.when(i == 0)
        def _():
            loss_ref[...] = jnp.zeros_like(loss_ref)

        loss_ref[...] += 0.5 * jnp.sum(jnp.sum(e * e, axis=-1, keepdims=True) / D, axis=0, keepdims=True)

    return pl.pallas_call(
        body, name=name, grid=(S // T,),
        in_specs=[_row_spec(T, D), _row_spec(T, D), _par_spec(D), _row_spec(T, D)],
        out_specs=[_row_spec(T, D), pl.BlockSpec((1, 128), lambda i: (0, 0))],
        out_shape=[jax.ShapeDtypeStruct((S, D), F32), jax.ShapeDtypeStruct((1, 128), F32)],
        compiler_params=_params(("arbitrary",)))(xr, y, gp, target)


_STRAIGHT_HEADS = (0, 2, 5, 7)
_ROLLED_HEADS = (1, 3, 4, 6)


def _straight_lanes():
    r = lax.broadcasted_iota(jnp.int32, (4 * BLK, BLK), 0)
    c = lax.broadcasted_iota(jnp.int32, (4 * BLK, BLK), 1)
    return (r < 2 * BLK) == (c < 64)


def _att_mask(not_first):
    k = lax.broadcasted_iota(jnp.int32, (2 * BLK, 4 * BLK), 0)
    q = lax.broadcasted_iota(jnp.int32, (2 * BLK, 4 * BLK), 1) % BLK
    qc, kc = 2 + q // 64, k // 64
    return (kc <= qc) & (kc >= qc - 2) & (not_first | (k >= BLK))


def _sink_row(sk_ref, heads):
    return jnp.concatenate([sk_ref[h:h + 1, :] for h in heads], axis=1)


def _softmax_sink(s, sk):
    m = jnp.maximum(jnp.max(s, axis=0, keepdims=True), sk)
    e = jnp.exp(s - m)
    es = jnp.exp(sk - m)
    inv = 1.0 / (jnp.sum(e, axis=0, keepdims=True) + es)
    return e * inv, es * inv


def _softmax_rows(s):
    e = jnp.exp(s - jnp.max(s, axis=0, keepdims=True))
    return e * (1.0 / jnp.sum(e, axis=0, keepdims=True))


_GELU_C = math.sqrt(2.0 / math.pi)
_GELU_A = 0.044715


def _gelu_with_slope(x):
    x2 = x * x
    t = jnp.tanh(_GELU_C * (x + _GELU_A * (x2 * x)))
    half = 0.5 * (1.0 + t)
    return x * half, half + (0.5 * _GELU_C) * x * (1.0 - t * t) * (1.0 + (3.0 * _GELU_A) * x2)


def _att_bands(cur, kvp):
    kband = jnp.concatenate([kvp[:, 0:BLK], cur[:, C_K:C_K + BLK]], axis=0)
    vband = jnp.concatenate([kvp[:, BLK:2 * BLK], cur[:, C_V:C_V + BLK]], axis=0)
    return kband, pltpu.roll(kband, 64, 1), vband, pltpu.roll(vband, 64, 1)


def _stack_tiles(ref_or_val, start):
    return jnp.concatenate([ref_or_val[:, start + BLK * t:start + BLK * (t + 1)] for t in range(4)], axis=0)


def _sgu_mask():
    r = lax.broadcasted_iota(jnp.int32, (BLK, BLK), 0)
    c = lax.broadcasted_iota(jnp.int32, (BLK, BLK), 1)
    return (c // 64) <= (r // 64)


def _trailing_sums(ext, g):
    s, shift = ext, 1
    for _ in range(g + 1):
        s = s + pltpu.roll(s, shift, 0)
        shift *= 2
    return s


def _leading_sums(z, g):
    d, shift = z, 1
    for _ in range(g + 1):
        d = d + pltpu.roll(d, z.shape[0] - shift, 0)
        shift *= 2
    return d


def _pool_cnt(blk, w):
    t = blk * BLK + lax.broadcasted_iota(jnp.int32, (BLK, 1), 0)
    return jnp.minimum(t + 1, w).astype(F32)


def _mix_in_specs(nb, rev):
    def b(i):
        return nb - 1 - i if rev else i
    return [
        pl.BlockSpec((BLK, C_GATE), lambda i: (b(i), 0)),
        pl.BlockSpec((BLK, 2 * BLK), lambda i: (jnp.maximum(b(i) - 1, 0), C_K // (2 * BLK))),
        pl.BlockSpec((HALO, C_GATE), lambda i: (jnp.maximum(b(i) * (BLK // HALO) - 1, 0), 0)),
        pl.BlockSpec((8, BLK), lambda i: (0, 0)),
        pl.BlockSpec((4, BLK, BLK), lambda i: (0, 0, 0)),
        pl.BlockSpec((4, BLK, 1), lambda i: (0, 0, 0)),
        pl.BlockSpec((1, 512), lambda i: (0, 0)),
        pl.BlockSpec((4, BLK, BLK), lambda i: (0, 0, 0)),
        pl.BlockSpec((1, 512), lambda i: (0, 0)),
    ]


def _mix_fwd(proj, sinks_b, ws, bs3, gsgu, wp, ps, name):
    S = proj.shape[0]
    nb = S // BLK

    def body(cur_ref, kvp_ref, pcp_ref, sk_ref, ws_ref, bs_ref, gs_ref, wp_ref, ps_ref, br_ref, ext_ref):
        i = pl.program_id(0)
        not_first = i > 0
        cur, kvp = cur_ref[...].astype(F32), kvp_ref[...].astype(F32)
        mask = _att_mask(not_first)
        own = _straight_lanes()
        q = _stack_tiles(cur, C_Q)
        outs = []
        kband, kroll, vband, vroll = _att_bands(cur, kvp)
        for qg, kg, vg, heads in ((jnp.where(own, q, 0.0), kband, vband, _STRAIGHT_HEADS),
                                  (jnp.where(own, 0.0, q), kroll, vroll, _ROLLED_HEADS)):
            s = jnp.where(mask, _dot(kg.astype(BF16), qg.astype(BF16), "nt") * ATT_SCALE, NEG_INF)
            p, _ = _softmax_sink(s, _sink_row(sk_ref, heads))
            outs.append(_dot(p.astype(BF16), vg.astype(BF16), "tn"))
        o = jnp.where(own, outs[0], outs[1])
        for t in range(4):
            br_ref[0, :, BLK * t:BLK * (t + 1)] = o[BLK * t:BLK * (t + 1)].astype(BF16)
        gu = jax.nn.gelu(cur[:, C_SU:C_SV])
        vn = _rms(jax.nn.gelu(cur[:, C_SV:C_PC]), gs_ref[...]).astype(BF16)
        wmask = _sgu_mask()
        for g in range(4):
            wm = jnp.where(wmask, ws_ref[g], 0.0).astype(BF16)
            sp = _dot(wm, vn[:, BLK * g:BLK * (g + 1)], "nn") + bs_ref[g]
            br_ref[1, :, BLK * g:BLK * (g + 1)] = (gu[:, BLK * g:BLK * (g + 1)] * sp).astype(BF16)
        c = cur[:, C_PC:C_GATE]
        ext_ref[0:HALO, :] = jnp.where(not_first, pcp_ref[:, C_PC:C_GATE].astype(F32), 0.0)
        ext_ref[HALO:HALO + BLK, :] = c
        for g, w in enumerate(POOL_WINDOWS):
            sl = slice(BLK * g, BLK * (g + 1))
            acc = _trailing_sums(ext_ref[:, sl], g)[HALO:]
            pooled = acc * (1.0 / _pool_cnt(i, w)) - c[:, sl]
            mixed = _dot(pooled.astype(BF16), wp_ref[g].astype(BF16), "nn")
            br_ref[2, :, sl] = (mixed * ps_ref[:, sl]).astype(BF16)

    return pl.pallas_call(
        body, name=name, grid=(nb,),
        in_specs=_mix_in_specs(nb, False),
        out_specs=pl.BlockSpec((3, BLK, 512), lambda i: (0, i, 0)),
        out_shape=jax.ShapeDtypeStruct((3, S, 512), BF16),
        scratch_shapes=[pltpu.VMEM((HALO + BLK, 512), F32)],
        compiler_params=_params(("parallel",)),
    )(proj, proj, proj, sinks_b, ws, bs3, gsgu, wp, ps)


def _mix_bwd(proj, dbr, dproj, sinks_b, ws, bs3, gsgu, wp, ps, name):
    S = proj.shape[0]
    nb = S // BLK

    def body(cur_ref, kvp_ref, pcp_ref, sk_ref, ws_ref, bs_ref, gs_ref, wp_ref, ps_ref, dbr_ref, _dproj_in,
             dp_ref, dsk_ref, dws_ref, dbs_ref, dgs_ref, dwp_ref, dps_ref,
             ext_ref, z_ref, ckv_ref, cpc_ref):
        i = pl.program_id(0)
        blk = nb - 1 - i
        not_first = blk > 0

        @pl.when(i == 0)
        def _():
            for r in (dsk_ref, dws_ref, dbs_ref, dgs_ref, dwp_ref, dps_ref, ckv_ref, cpc_ref, z_ref):
                r[...] = jnp.zeros_like(r)

        cur, kvp = cur_ref[...].astype(F32), kvp_ref[...].astype(F32)
        mask = _att_mask(not_first)
        own = _straight_lanes()
        q = _stack_tiles(cur, C_Q)
        do = jnp.concatenate([dbr_ref[0, :, BLK * t:BLK * (t + 1)] for t in range(4)], axis=0)
        kband, kroll, vband, vroll = _att_bands(cur, kvp)
        dqs, dks, dvs = [], [], []
        for qg, dog, kg, vg, heads in (
                (jnp.where(own, q, 0.0), jnp.where(own, do, 0.0), kband, vband, _STRAIGHT_HEADS),
                (jnp.where(own, 0.0, q), jnp.where(own, 0.0, do), kroll, vroll, _ROLLED_HEADS)):
            qg, dog, kg, vg = qg.astype(BF16), dog.astype(BF16), kg.astype(BF16), vg.astype(BF16)
            s = jnp.where(mask, _dot(kg, qg, "nt") * ATT_SCALE, NEG_INF)
            p, p_sink = _softmax_sink(s, _sink_row(sk_ref, heads))
            dp = _dot(vg, dog, "nt")
            rs = jnp.sum(p * dp, axis=0, keepdims=True)
            ds = (p * (dp - rs) * ATT_SCALE).astype(BF16)
            sink_row = p_sink * rs
            for t, h in enumerate(heads):
                dsk_ref[h:h + 1, :] += jnp.broadcast_to(
                    -jnp.sum(sink_row[:, BLK * t:BLK * (t + 1)], axis=1, keepdims=True), (1, BLK))
            dvs.append(_dot(p.astype(BF16), dog, "nn"))
            dks.append(_dot(ds, qg, "nn"))
            dqs.append(_dot(ds, kg, "tn"))
        dq = jnp.where(own, dqs[0], dqs[1])
        for t in range(4):
            dp_ref[:, C_Q + BLK * t:C_Q + BLK * (t + 1)] = dq[BLK * t:BLK * (t + 1)].astype(BF16)
        dk = dks[0] + pltpu.roll(dks[1], 64, 1)
        dv = dvs[0] + pltpu.roll(dvs[1], 64, 1)
        dp_ref[:, C_K:C_K + BLK] = (dk[BLK:] + ckv_ref[:, 0:BLK]).astype(BF16)
        dp_ref[:, C_V:C_V + BLK] = (dv[BLK:] + ckv_ref[:, BLK:]).astype(BF16)
        ckv_ref[:, 0:BLK] = dk[:BLK]
        ckv_ref[:, BLK:] = dv[:BLK]
        su, sv = cur[:, C_SU:C_SV], cur[:, C_SV:C_PC]
        gu, gu_slope = _gelu_with_slope(su)
        gv, gv_slope = _gelu_with_slope(sv)
        vn, vjp_v = jax.vjp(_rms, gv, gs_ref[...])
        vn16 = vn.astype(BF16)
        wmask = _sgu_mask()
        dgu, dvn = [], []
        for g in range(4):
            sl = slice(BLK * g, BLK * (g + 1))
            wm = jnp.where(wmask, ws_ref[g], 0.0).astype(BF16)
            sp = _dot(wm, vn16[:, sl], "nn") + bs_ref[g]
            dyb = dbr_ref[1, :, sl]
            dgu.append(dyb * sp)
            dsp = dyb * gu[:, sl]
            dsp16 = dsp.astype(BF16)
            dvn.append(_dot(wm, dsp16, "tn"))
            dws_ref[g] += jnp.where(wmask, _dot(dsp16, vn16[:, sl], "nt"), 0.0)
            dbs_ref[g] += jnp.sum(dsp, axis=1, keepdims=True)
        dgv, dgs = vjp_v(jnp.concatenate(dvn, axis=1))
        dp_ref[:, C_SU:C_SV] = (jnp.concatenate(dgu, axis=1) * gu_slope).astype(BF16)
        dp_ref[:, C_SV:C_PC] = (dgv * gv_slope).astype(BF16)
        dgs_ref[...] += dgs
        c = cur[:, C_PC:C_GATE]
        ext_ref[0:HALO, :] = jnp.where(not_first, pcp_ref[:, C_PC:C_GATE].astype(F32), 0.0)
        ext_ref[HALO:HALO + BLK, :] = c
        for g, w in enumerate(POOL_WINDOWS):
            sl = slice(BLK * g, BLK * (g + 1))
            acc = _trailing_sums(ext_ref[:, sl], g)[HALO:]
            inv_cnt = 1.0 / _pool_cnt(blk, w)
            pooled16 = (acc * inv_cnt - c[:, sl]).astype(BF16)
            wp16 = wp_ref[g].astype(BF16)
            mixed = _dot(pooled16, wp16, "nn")
            dyc = dbr_ref[2, :, sl]
            dps_ref[:, sl] += jnp.sum(dyc * mixed, axis=0, keepdims=True)
            dmixed16 = (dyc * ps_ref[:, sl]).astype(BF16)
            dwp_ref[g] += _dot(pooled16, dmixed16, "tn")
            dpooled = _dot(dmixed16, wp16, "nt")
            z_ref[HALO:HALO + BLK, sl] = dpooled * inv_cnt
            dext = _leading_sums(z_ref[:, sl], g)[:HALO + BLK]
            dp_ref[:, C_PC + BLK * g:C_PC + BLK * (g + 1)] = (
                dext[HALO:] - dpooled + jnp.concatenate([jnp.zeros((BLK - HALO, BLK), F32), cpc_ref[:, sl]], axis=0)
            ).astype(BF16)
            cpc_ref[:, sl] = dext[:HALO]

    n_in = 11
    small = [jax.ShapeDtypeStruct((8, BLK), F32), jax.ShapeDtypeStruct((4, BLK, BLK), F32),
             jax.ShapeDtypeStruct((4, BLK, 1), F32), jax.ShapeDtypeStruct((1, 512), F32),
             jax.ShapeDtypeStruct((4, BLK, BLK), F32), jax.ShapeDtypeStruct((1, 512), F32)]
    small_specs = [pl.BlockSpec((8, BLK), lambda i: (0, 0)), pl.BlockSpec((4, BLK, BLK), lambda i: (0, 0, 0)),
                   pl.BlockSpec((4, BLK, 1), lambda i: (0, 0, 0)), pl.BlockSpec((1, 512), lambda i: (0, 0)),
                   pl.BlockSpec((4, BLK, BLK), lambda i: (0, 0, 0)), pl.BlockSpec((1, 512), lambda i: (0, 0))]
    res = pl.pallas_call(
        body, name=name, grid=(nb,),
        in_specs=_mix_in_specs(nb, True) + [
            pl.BlockSpec((3, BLK, 512), lambda i: (0, nb - 1 - i, 0)),
            pl.BlockSpec(memory_space=pl.ANY)],
        out_specs=[pl.BlockSpec((BLK, C_GATE), lambda i: (nb - 1 - i, 0))] + small_specs,
        out_shape=[jax.ShapeDtypeStruct(dproj.shape, dproj.dtype)] + small,
        scratch_shapes=[pltpu.VMEM((HALO + BLK, 512), F32), pltpu.VMEM((2 * HALO + BLK, 512), F32),
                        pltpu.VMEM((BLK, 2 * BLK), F32), pltpu.VMEM((HALO, 512), F32)],
        input_output_aliases={n_in - 1: 0},
        compiler_params=_params(("arbitrary",)),
    )(proj, proj, proj, sinks_b, ws, bs3, gsgu, wp, ps, dbr, dproj)
    return tuple(res)


_GW = 256


def _merge_fwd(proj, pb, name, T=4096):
    S, D = pb.shape[1], pb.shape[2]
    T = min(T, S)

    def body(gate_ref, pb_ref, out_ref, acc_ref):
        n = pl.program_id(2)

        @pl.when(n == 0)
        def _():
            acc_ref[...] = jnp.zeros_like(acc_ref)

        acc_ref[...] += jax.nn.sigmoid(gate_ref[...].astype(F32)) * pb_ref[...]

        @pl.when(n == 2)
        def _():
            out_ref[...] = acc_ref[...].astype(BF16)

    return pl.pallas_call(
        body, name=name, grid=(S // T, D // _GW, 3),
        in_specs=[pl.BlockSpec((T, _GW), lambda i, j, n: (i, C_GATE // _GW + n * (D // _GW) + j)),
                  pl.BlockSpec((None, T, _GW), lambda i, j, n: (n, i, j))],
        out_specs=pl.BlockSpec((T, _GW), lambda i, j, n: (i, j)),
        out_shape=jax.ShapeDtypeStruct((S, D), BF16),
        scratch_shapes=[pltpu.VMEM((T, _GW), F32)],
        compiler_params=_params(("parallel", "parallel", "arbitrary")),
    )(proj, pb)


def _merge_bwd(proj, pb, dmerged, name, T=4096):
    S, D = pb.shape[1], pb.shape[2]
    T = min(T, S)

    def body(gate_ref, pb_ref, dm_ref, dgate_ref, dpb_ref):
        sg = jax.nn.sigmoid(gate_ref[...].astype(F32))
        dm = dm_ref[...]
        dpb_ref[...] = (dm * sg).astype(BF16)
        dgate_ref[...] = (dm * pb_ref[...] * sg * (1.0 - sg)).astype(BF16)

    gate_map = lambda i, n, j: (i, C_GATE // _GW + n * (D // _GW) + j)
    return pl.pallas_call(
        body, name=name, grid=(S // T, 3, D // _GW),
        in_specs=[pl.BlockSpec((T, _GW), gate_map),
                  pl.BlockSpec((None, T, _GW), lambda i, n, j: (n, i, j)),
                  pl.BlockSpec((T, _GW), lambda i, n, j: (i, j))],
        out_specs=[pl.BlockSpec((T, _GW), gate_map),
                   pl.BlockSpec((None, T, _GW), lambda i, n, j: (n, i, j))],
        out_shape=[jax.ShapeDtypeStruct((S, C_END), BF16), jax.ShapeDtypeStruct((3, S, D), BF16)],
        compiler_params=_params(("parallel", "parallel", "parallel")),
    )(proj, pb, dmerged)


def _memattn_fwd(qm, kv, name, T=512):
    S, NM = qm.shape[0], kv.shape[0]
    T = min(T, S)

    def body(q_ref, kv_ref, o_ref):
        for h in range(4):
            sl = slice(128 * h, 128 * (h + 1))
            k = kv_ref[:, sl].astype(BF16)
            v = kv_ref[:, 512 + 128 * h:512 + 128 * (h + 1)].astype(BF16)
            s = _dot(k, q_ref[:, sl].astype(BF16), "nt") * MEM_SCALE
            p = _softmax_rows(s)
            o_ref[:, sl] = _dot(p.astype(BF16), v, "tn").astype(BF16)

    return pl.pallas_call(
        body, name=name, grid=(S // T,),
        in_specs=[_row_spec(T, 512), pl.BlockSpec((NM, 1024), lambda i: (0, 0))],
        out_specs=_row_spec(T, 512), out_shape=jax.ShapeDtypeStruct((S, 512), BF16),
        compiler_params=_params(("parallel",)))(qm, kv)


def _memattn_bwd(qm, kv, dom, name, T=512):
    S, NM = qm.shape[0], kv.shape[0]
    T = min(T, S)

    def body(q_ref, kv_ref, do_ref, dq_ref, dkv_ref):
        i = pl.program_id(0)

        @pl.when(i == 0)
        def _():
            dkv_ref[...] = jnp.zeros_like(dkv_ref)

        for h in range(4):
            sl = slice(128 * h, 128 * (h + 1))
            sv_ = slice(512 + 128 * h, 512 + 128 * (h + 1))
            q = q_ref[:, sl].astype(BF16)
            k = kv_ref[:, sl].astype(BF16)
            v = kv_ref[:, sv_].astype(BF16)
            do = do_ref[:, sl].astype(BF16)
            p = _softmax_rows(_dot(k, q, "nt") * MEM_SCALE)
            dp = _dot(v, do, "nt")
            ds = (p * (dp - jnp.sum(p * dp, axis=0, keepdims=True)) * MEM_SCALE).astype(BF16)
            dq_ref[:, sl] = _dot(ds, k, "tn").astype(BF16)
            dkv_ref[:, sl] += _dot(ds, q, "nn")
            dkv_ref[:, sv_] += _dot(p.astype(BF16), do, "nn")

    return pl.pallas_call(
        body, name=name, grid=(S // T,),
        in_specs=[_row_spec(T, 512), pl.BlockSpec((NM, 1024), lambda i: (0, 0)), _row_spec(T, 512)],
        out_specs=[_row_spec(T, 512), pl.BlockSpec((NM, 1024), lambda i: (0, 0))],
        out_shape=[jax.ShapeDtypeStruct((S, 512), BF16), jax.ShapeDtypeStruct((NM, 1024), F32)],
        compiler_params=_params(("arbitrary",)))(qm, kv, dom)


def _adamw(w, g, m, v, name, rows=None, g_row0=None, into=None, TR=512):
    R, C = w.shape
    lo, hi = rows if rows is not None else (0, R)
    g0 = lo if g_row0 is None else g_row0
    TR = _row_tile(math.gcd(math.gcd(lo, g0), hi - lo), TR)
    c1 = 1.0 - ADAM_B1 ** ADAM_STEP
    c2 = 1.0 - ADAM_B2 ** ADAM_STEP

    def body(w_ref, g_ref, m_ref, v_ref, *rest):
        d_ref, nm_ref, nv_ref = rest[-3:]
        gv = g_ref[...]
        nm = ADAM_B1 * m_ref[...] + (1.0 - ADAM_B1) * gv
        nv = ADAM_B2 * v_ref[...] + (1.0 - ADAM_B2) * jnp.square(gv)
        d_ref[...] = -ADAM_LR * ((nm / c1) / (jnp.sqrt(nv / c2) + ADAM_EPS) + ADAM_WD * w_ref[...])
        nm_ref[...] = nm
        nv_ref[...] = nv

    spec = pl.BlockSpec((TR, C), lambda i: (lo // TR + i, 0))
    g_spec = pl.BlockSpec((TR, C), lambda i: (g0 // TR + i, 0))
    prior = list(into) if into is not None else []
    return pl.pallas_call(
        body, name=name, grid=((hi - lo) // TR,),
        in_specs=[spec, g_spec, spec, spec] + [pl.BlockSpec(memory_space=pl.ANY)] * len(prior), out_specs=[spec] * 3,
        out_shape=[jax.ShapeDtypeStruct((R, C), F32)] * 3,
        input_output_aliases={4 + k: k for k in range(len(prior))},
        compiler_params=_params(("parallel",)))(w, g, m, v, *prior)


def _row_tile(R, pref):
    t = (pref // 8) * 8
    while t >= 8:
        if R % t == 0:
            return t
        t -= 8
    raise ValueError(f"no row tile for {R}")


def _sum_slots(stack, name, TR=512):
    n, R, C = stack.shape
    TR = R if R <= TR else _row_tile(R, TR)

    def body(s_ref, o_ref):
        acc = s_ref[0]
        for k in range(1, n):
            acc = acc + s_ref[k]
        o_ref[...] = acc

    return pl.pallas_call(
        body, name=name, grid=(R // TR,),
        in_specs=[pl.BlockSpec((n, TR, C), lambda i: (0, i, 0))],
        out_specs=pl.BlockSpec((TR, C), lambda i: (i, 0)),
        out_shape=jax.ShapeDtypeStruct((R, C), F32),
        compiler_params=_params(("parallel",)))(stack)


_ANY = pl.BlockSpec(memory_space=pl.ANY)


def _chip_of(j, c):
    return (j // 2, j % 2, c)


def _own_slab(shard, dtype, j_arr, name, first=0, count=None, plus=None, deps=(), TR=512):
    N, r, C = shard.shape
    B = N if count is None else count
    rh = r // 2
    TR = rh if rh <= TR else _row_tile(rh, TR)
    nt = rh // TR
    ins = [shard] if plus is None else [shard, plus]

    def body(j_ref, *refs):
        val = refs[0][...] if plus is None else refs[0][...] + refs[1][...]
        refs[-1][...] = val.astype(refs[-1].dtype)

    return pl.pallas_call(
        body, name=name,
        grid_spec=pltpu.PrefetchScalarGridSpec(
            num_scalar_prefetch=1, grid=(B, 2, nt),
            in_specs=[pl.BlockSpec((None, TR, C), lambda b, h, t, jr: (first + b, h * nt + t, 0))] * len(ins)
            + [_ANY] * len(deps),
            out_specs=pl.BlockSpec((None, None, None, TR, C), lambda b, h, t, jr: (b, jr[0], h, t, 0))),
        out_shape=jax.ShapeDtypeStruct((B, 4, 2, rh, C), dtype),
        compiler_params=_params(("parallel", "parallel", "parallel")),
    )(j_arr, *ins, *deps)


def _gather_weights(bufs, name):
    n = len(bufs)

    def body(*refs):
        buf = refs[n:2 * n]
        send_sems, recv_sems, fsend_sems, frecv_sems = refs[2 * n:]
        x, y, c = lax.axis_index("x"), lax.axis_index("y"), lax.axis_index("c")
        j = 2 * x + y
        sib = (x, y, 1 - c)
        sends = []
        for d in range(1, 4):
            for a in range(n):
                cp = pltpu.make_async_remote_copy(
                    src_ref=buf[a].at[:, j, c], dst_ref=buf[a].at[:, j, c], send_sem=send_sems.at[a, d - 1],
                    recv_sem=recv_sems.at[a, d - 1], device_id=_chip_of((j + d) % 4, c), device_id_type=MESH)
                cp.start()
                sends.append(cp)
        for d in range(1, 4):
            frm = (j + 4 - d) % 4
            for a in range(n):
                pltpu.make_async_remote_copy(
                    src_ref=buf[a].at[:, frm, c], dst_ref=buf[a].at[:, frm, c], send_sem=send_sems.at[a, d - 1],
                    recv_sem=recv_sems.at[a, d - 1], device_id=_chip_of(frm, c), device_id_type=MESH).wait_recv()
                cp = pltpu.make_async_remote_copy(
                    src_ref=buf[a].at[:, frm, c], dst_ref=buf[a].at[:, frm, c], send_sem=fsend_sems.at[a, d - 1],
                    recv_sem=frecv_sems.at[a, d - 1], device_id=sib, device_id_type=MESH)
                cp.start()
                sends.append(cp)
        for d in range(1, 4):
            frm = (j + 4 - d) % 4
            for a in range(n):
                pltpu.make_async_remote_copy(
                    src_ref=buf[a].at[:, frm, 1 - c], dst_ref=buf[a].at[:, frm, 1 - c], send_sem=fsend_sems.at[a, d - 1],
                    recv_sem=frecv_sems.at[a, d - 1], device_id=sib, device_id_type=MESH).wait_recv()
        for cp in sends:
            cp.wait_send()

    return pl.pallas_call(
        body, name=name,
        in_specs=[_ANY] * n, out_specs=[_ANY] * n,
        out_shape=[jax.ShapeDtypeStruct(b.shape, b.dtype) for b in bufs],
        scratch_shapes=[pltpu.SemaphoreType.DMA((n, 3))] * 4,
        input_output_aliases={a: a for a in range(n)},
    )(*bufs)


_HBM = pl.BlockSpec(memory_space=pltpu.HBM)
_SEM = pl.BlockSpec(memory_space=pltpu.SEMAPHORE)
_DATAFLOW = pltpu.SideEffectType.DATAFLOW_SIDE_EFFECTING


def _in_hbm(arrays):
    return [pltpu.with_memory_space_constraint(a, pltpu.HBM) for a in arrays]


def _start_copies(bufs, plan, count, name, deps=()):
    n, k = len(bufs), len(deps)

    def body(*refs):
        send_sems, recv_sems = refs[n + k], refs[n + k + 1]
        for i, (src, dst, dev) in enumerate(plan(refs[:n], False)):
            pltpu.make_async_remote_copy(src_ref=src, dst_ref=dst, send_sem=send_sems.at[i], recv_sem=recv_sems.at[i],
                                         device_id=dev, device_id_type=MESH).start()
        refs[-1][...] = jnp.zeros_like(refs[-1])

    return pl.pallas_call(
        body, name=name,
        out_shape=(pltpu.SemaphoreType.DMA((count,)), pltpu.SemaphoreType.DMA((count,)),
                   *[pltpu.HBM(b.shape, b.dtype) for b in bufs], jax.ShapeDtypeStruct((8, 128), F32)),
        in_specs=[_HBM] * n + [_ANY] * k,
        out_specs=(_SEM, _SEM, *[_HBM] * n, pl.BlockSpec(memory_space=pltpu.VMEM)),
        input_output_aliases={a: 2 + a for a in range(n)},
        compiler_params=pltpu.CompilerParams(has_side_effects=_DATAFLOW),
    )(*_in_hbm(bufs), *deps)


def _wait_copies(handle, plan, afters, name):
    send_sems, recv_sems, *bufs = handle[:-1]
    n = len(bufs)

    def body(*refs):
        send_sems, recv_sems = refs[n], refs[n + 1]
        for i, (src, dst, dev) in enumerate(plan(refs[:n], True)):
            cp = pltpu.make_async_remote_copy(src_ref=src, dst_ref=dst, send_sem=send_sems.at[i], recv_sem=recv_sems.at[i],
                                              device_id=dev, device_id_type=MESH)
            cp.wait_send()
            cp.wait_recv()

    return list(pl.pallas_call(
        body, name=name,
        out_shape=[pltpu.HBM(b.shape, b.dtype) for b in bufs],
        in_specs=[_HBM] * n + [_SEM, _SEM] + [_ANY] * len(afters), out_specs=[_HBM] * n,
        input_output_aliases={a: a for a in range(n)},
        compiler_params=pltpu.CompilerParams(has_side_effects=_DATAFLOW),
    )(*bufs, send_sems, recv_sems, *afters))


def _gather_plan(buf, waiting):
    c = lax.axis_index("c")
    j = 2 * lax.axis_index("x") + lax.axis_index("y")
    copies = []
    for d in range(1, 4):
        to, frm = (j + d) % 4, (j + 4 - d) % 4
        for b in buf:
            copies.append((b.at[:, j, c], b.at[:, frm if waiting else j, c], _chip_of(frm if waiting else to, c)))
    return copies


def _chip_plan(buf, waiting):
    n = len(buf) // 2
    c = lax.axis_index("c")
    j = 2 * lax.axis_index("x") + lax.axis_index("y")
    copies = []
    for d in range(1, 4):
        to = (j + d) % 4
        for a in range(n):
            copies.append((buf[a].at[to], buf[n + a].at[d - 1], _chip_of(to, c)))
    return copies


def _pair_plan(buf, waiting):
    n = len(buf) // 2
    c = lax.axis_index("c")
    sib = (lax.axis_index("x"), lax.axis_index("y"), 1 - c)
    return [(buf[a].at[:, pl.ds(1 - c, 1)], buf[n + a], sib) for a in range(n)]


def _forward_plan(buf, waiting):
    x, y, c = lax.axis_index("x"), lax.axis_index("y"), lax.axis_index("c")
    j = 2 * x + y
    copies = []
    for d in range(1, 4):
        frm = (j + 4 - d) % 4
        for b in buf:
            copies.append((b.at[:, frm, c], b.at[:, frm, 1 - c if waiting else c], (x, y, 1 - c)))
    return copies


def _gather_forward(bufs, name):
    n = len(bufs)

    def body(*refs):
        buf = refs[n:2 * n]
        send_sems, recv_sems = refs[2 * n:]
        x, y, c = lax.axis_index("x"), lax.axis_index("y"), lax.axis_index("c")
        j = 2 * x + y
        sib = (x, y, 1 - c)
        sends = []
        for d in range(1, 4):
            frm = (j + 4 - d) % 4
            for a in range(n):
                cp = pltpu.make_async_remote_copy(
                    src_ref=buf[a].at[:, frm, c], dst_ref=buf[a].at[:, frm, c], send_sem=send_sems.at[a, d - 1],
                    recv_sem=recv_sems.at[a, d - 1], device_id=sib, device_id_type=MESH)
                cp.start()
                sends.append(cp)
        for d in range(1, 4):
            frm = (j + 4 - d) % 4
            for a in range(n):
                pltpu.make_async_remote_copy(
                    src_ref=buf[a].at[:, frm, 1 - c], dst_ref=buf[a].at[:, frm, 1 - c], send_sem=send_sems.at[a, d - 1],
                    recv_sem=recv_sems.at[a, d - 1], device_id=sib, device_id_type=MESH).wait_recv()
        for cp in sends:
            cp.wait_send()

    return pl.pallas_call(
        body, name=name,
        in_specs=[_ANY] * n, out_specs=[_ANY] * n,
        out_shape=[jax.ShapeDtypeStruct(b.shape, b.dtype) for b in bufs],
        scratch_shapes=[pltpu.SemaphoreType.DMA((n, 3))] * 2,
        input_output_aliases={a: a for a in range(n)},
    )(*bufs)


def _pair_add(g4, r1, cj_arr, name, TR=512):
    B4, _, rh, C = g4.shape
    B = B4 // 4
    TR = rh if rh <= TR else _row_tile(rh, TR)

    def body(cj_ref, g_ref, r_ref, o16_ref, own_ref):
        s = g_ref[...].astype(F32) + r_ref[...].astype(F32)
        o16_ref[...] = s.astype(BF16)

        @pl.when(pl.program_id(2) == cj_ref[1])
        def _():
            own_ref[...] = s

    return pl.pallas_call(
        body, name=name,
        grid_spec=pltpu.PrefetchScalarGridSpec(
            num_scalar_prefetch=1, grid=(B, rh // TR, 4),
            in_specs=[pl.BlockSpec((None, None, TR, C), lambda b, t, p, cj: (b * 4 + p, cj[0], t, 0)),
                      pl.BlockSpec((None, None, TR, C), lambda b, t, p, cj: (b * 4 + p, 0, t, 0))],
            out_specs=[pl.BlockSpec((None, None, TR, C), lambda b, t, p, cj: (p, b, t, 0)),
                       pl.BlockSpec((None, TR, C), lambda b, t, p, cj: (b, t, 0))]),
        out_shape=[jax.ShapeDtypeStruct((4, B, rh, C), BF16), jax.ShapeDtypeStruct((B, rh, C), F32)],
        compiler_params=_params(("parallel", "parallel", "arbitrary")),
    )(cj_arr, g4, r1)


def _chip_add(own, r2, cj_arr, into, first, name, TR=512):
    B, rh, C = own.shape
    TR = rh if rh <= TR else _row_tile(rh, TR)

    def body(cj_ref, p_ref, r_ref, _into_ref, o_ref):
        o_ref[...] = p_ref[...] + r_ref[0].astype(F32) + r_ref[1].astype(F32) + r_ref[2].astype(F32)

    return pl.pallas_call(
        body, name=name,
        grid_spec=pltpu.PrefetchScalarGridSpec(
            num_scalar_prefetch=1, grid=(B, rh // TR),
            in_specs=[pl.BlockSpec((None, TR, C), lambda b, t, cj: (b, t, 0)),
                      pl.BlockSpec((3, None, TR, C), lambda b, t, cj: (0, b, t, 0)),
                      _ANY],
            out_specs=pl.BlockSpec((None, None, TR, C), lambda b, t, cj: (first + b, cj[0], t, 0))),
        out_shape=jax.ShapeDtypeStruct(into.shape, F32),
        input_output_aliases={3: 0},
        compiler_params=_params(("parallel", "parallel")),
    )(cj_arr, own, r2, into)


def _pair_share(bufs, spans, name, deps=()):
    n = len(bufs)

    def body(*refs):
        buf = refs[n + len(deps):2 * n + len(deps)]
        send_sems, recv_sems = refs[2 * n + len(deps):]
        c = lax.axis_index("c")
        sib = (lax.axis_index("x"), lax.axis_index("y"), 1 - c)
        cps = []
        for a, (first, count) in enumerate(spans):
            cp = pltpu.make_async_remote_copy(
                src_ref=buf[a].at[pl.ds(first, count), c], dst_ref=buf[a].at[pl.ds(first, count), c],
                send_sem=send_sems.at[a], recv_sem=recv_sems.at[a], device_id=sib, device_id_type=MESH)
            cp.start()
            cps.append(cp)
        for a, (first, count) in enumerate(spans):
            pltpu.make_async_remote_copy(
                src_ref=buf[a].at[pl.ds(first, count), 1 - c], dst_ref=buf[a].at[pl.ds(first, count), 1 - c],
                send_sem=send_sems.at[a], recv_sem=recv_sems.at[a], device_id=sib, device_id_type=MESH).wait_recv()
        for cp in cps:
            cp.wait_send()

    return pl.pallas_call(
        body, name=name, in_specs=[_ANY] * (n + len(deps)), out_specs=[_ANY] * n,
        out_shape=[jax.ShapeDtypeStruct(b.shape, b.dtype) for b in bufs],
        scratch_shapes=[pltpu.SemaphoreType.DMA((n,)), pltpu.SemaphoreType.DMA((n,))],
        input_output_aliases={a: a for a in range(n)},
    )(*bufs, *deps)


def _pair_swap(arr, name):
    def body(src, dst, send_sem, recv_sem):
        sib = (lax.axis_index("x"), lax.axis_index("y"), 1 - lax.axis_index("c"))
        cp = pltpu.make_async_remote_copy(src_ref=src, dst_ref=dst, send_sem=send_sem, recv_sem=recv_sem,
                                          device_id=sib, device_id_type=MESH)
        cp.start()
        cp.wait_recv()
        cp.wait_send()

    return pl.pallas_call(
        body, name=name, in_specs=[_ANY], out_specs=_ANY,
        out_shape=jax.ShapeDtypeStruct(arr.shape, arr.dtype),
        scratch_shapes=[pltpu.SemaphoreType.DMA, pltpu.SemaphoreType.DMA],
    )(arr)


class _ReduceScatter:
    def __init__(self, n_layers, cj_arr):
        self.L, self.cj = n_layers, cj_arr
        self.total = None
        self.pair = None
        self.chip = None

    def _land(self, after):
        handle, layer, owns = self.chip
        n = len(owns)
        r2 = _wait_copies(handle, _chip_plan, after if isinstance(after, (list, tuple)) else (after,), "rs_chip_wait")[n:]
        if self.total is None:
            self.total = [lax.empty((self.L * o.shape[0], 2) + o.shape[1:], F32) for o in owns]
        self.total = [_chip_add(o, r, self.cj, t, layer * o.shape[0], "rs_chip_add")
                      for o, r, t in zip(owns, r2, self.total)]
        self.chip = None

    def add_layer(self, layer, grads):
        g4 = [g.reshape(g.shape[0] * 4, 2, g.shape[1] // 8, g.shape[2]) for g in grads]
        lands = [lax.empty((g.shape[0], 1) + g.shape[2:], g.dtype) for g in g4]
        handle = _start_copies(g4 + lands, _pair_plan, len(g4), "rs_pair_start")
        self.pair = (handle, layer)
        return (handle[-1],)

    def advance(self, after):
        if self.pair is None:
            return ()
        handle, layer = self.pair
        both = _wait_copies(handle, _pair_plan, (after,), "rs_pair_wait")
        n = len(both) // 2
        added = [_pair_add(g, r, self.cj, "rs_pair_add") for g, r in zip(both[:n], both[n:])]
        parts, owns = [p for p, _ in added], [o for _, o in added]
        lands = [lax.empty((3,) + p.shape[1:], p.dtype) for p in parts]
        handle = _start_copies(parts + lands, _chip_plan, 3 * n, "rs_chip_start")
        if self.chip is not None:
            self._land(handle[-1])
        self.pair, self.chip = None, (handle, layer, owns)
        return (handle[-1],)

    def upper_layers(self, deps):
        per = [t.shape[0] // self.L for t in self.total]
        self.total = _pair_share(self.total, [(b, (self.L - 1) * b) for b in per], "rs_pair_share_upper", deps=deps)
        return [t.reshape(t.shape[0], t.shape[1] * t.shape[2], t.shape[3]) for t in self.total]

    def first_layer(self, after):
        self._land(after)
        per = [t.shape[0] // self.L for t in self.total]
        full = _pair_share(self.total, [(0, b) for b in per], "rs_pair_share_first")
        return [f.reshape(f.shape[0], f.shape[1] * f.shape[2], f.shape[3]) for f in full]


def _relu2_epi(acc):
    return (jnp.square(jnp.maximum(acc, 0.0)),)


def _relu2_bwd_epi(acc, a):
    return (acc * (2.0 * jnp.sqrt(a.astype(F32))),)


_GRAD_ORDER = ("winT", "wbT", "wout", "wq", "wkv", "woT", "wupT", "wdown")
_DW = dict(tm=512, tn=1024, tk=4096, out_dtypes=(BF16,))
_LONG_K = dict(tm=1024, tn=1024, tk=2048)


def _forward_backward(x, mem, target, input_weight_of, weights_of, prefetch_weights, P, grads_done, grads_advance):
    L = P["g_norm"].shape[0]
    S, D = x.shape
    gn = lambda l, i: P["g_norm"][l, i][None]

    saved = []
    (h,) = _resnorm_fwd(x, None, None, gn(0, 0), "norm_in")
    xr = x
    for l in range(L):
        w_in_t = input_weight_of(l, xr)
        proj = _mm(h, w_in_t, "nt", "in_proj", b_pre=(0,), out_dtypes=(BF16,), tn=1792)
        W, w_deps = weights_of(l, proj)
        small = (jnp.broadcast_to(P["sinks"][l][:, None], (8, BLK)), P["ws"][l], P["bs"][l][:, :, None],
                 P["gsgu"][l][None], P["wp"][l], P["ps"][l][None])
        br = _mix_fwd(proj, *small, "mix_fwd")
        pb = _mm(br, W["wbT"], "nt", "branch_proj", batch=3, out_dtypes=(BF16,), deps=w_deps)
        merged = _merge_fwd(proj, pb, "merge_fwd")
        z = _mm(merged, W["wout"], "nn", "out_proj", b_pre=(0,))
        x1, hm = _resnorm_fwd(xr, z, gn(l, 1), gn(l, 2), "resnorm_fwd")
        qm = _mm(hm, W["wq"], "nn", "mem_q", b_pre=(0,))
        (memn,) = _resnorm_fwd(mem, None, None, P["g_mem"][l][None], "mem_norm")
        kv = _mm(memn, W["wkv"], "nn", "mem_kv", b_pre=(0,))
        om = _memattn_fwd(qm, kv, "memattn_fwd")
        ym = _mm(om, W["woT"], "nt", "mem_o", b_pre=(0,))
        x2, hf = _resnorm_fwd(x1, ym, gn(l, 3), gn(l, 4), "resnorm_fwd")
        a = _mm(hf, W["wupT"], "nt", "mlp_up", b_pre=(0,), out_dtypes=(BF16,), epi=_relu2_epi)
        yf = _mm(a, W["wdown"], "nn", "mlp_down", b_pre=(0,), **_LONG_K,
                 deps=prefetch_weights(l + 1, a) if l < L - 1 else ())
        saved.append(dict(W=W, x0=xr, h=h, proj=proj, small=small, br=br, pb=pb, merged=merged, z=z, x1=x1, hm=hm,
                          qm=qm, memn=memn, kv=kv, om=om, ym=ym, x2=x2, hf=hf, a=a, yf=yf))
        if l < L - 1:
            xr, h = _resnorm_fwd(x2, yf, gn(l, 5), gn(l + 1, 0), "resnorm_fwd")
    dres, loss = _final_fwd(saved[-1]["x2"], saved[-1]["yf"], gn(L - 1, 5), target, "loss_head")

    dgn = [[None] * 6 for _ in range(L)]
    dsmall = {k: [None] * L for k in ("g_mem", "sinks", "ws", "bs", "gsgu", "wp", "ps")}
    dh = None
    for l in reversed(range(L)):
        s = saved[l]
        W, G = s["W"], {}
        if l == L - 1:
            dx2, dyf, dgn[l][5] = _resnorm_bwd(s["x2"], s["yf"], gn(l, 5), None, dres, None, "resnorm_bwd_top")
        else:
            dx2, dyf, dgn[l][5], dgn[l + 1][0] = _resnorm_bwd(s["x2"], s["yf"], gn(l, 5), gn(l + 1, 0), dres, dh,
                                                              "resnorm_bwd", deps=deps)
        du = _mm(dyf, W["wdown"], "nt", "mlp_down_dx", b_pre=(0,), out_dtypes=(BF16,), extras=(s["a"],), epi=_relu2_bwd_epi)
        G["wdown"] = _mm(s["a"], dyf, "tn", "mlp_down_dw", **_DW)[None]
        dhf = _mm(du, W["wupT"], "nn", "mlp_up_dx", b_pre=(0,), **_LONG_K)
        G["wupT"] = _mm(du, s["hf"], "tn", "mlp_up_dw", **_DW)[None]
        dx1, dym, dgn[l][3], dgn[l][4] = _resnorm_bwd(s["x1"], s["ym"], gn(l, 3), gn(l, 4), dx2, dhf, "resnorm_bwd")
        dom = _mm(dym, W["woT"], "nn", "mem_o_dx", b_pre=(0,), deps=grads_advance(dx1))
        G["woT"] = _mm(dym, s["om"], "tn", "mem_o_dw", **_DW)[None]
        dqm, dkv = _memattn_bwd(s["qm"], s["kv"], dom, "memattn_bwd")
        dmemn = _mm(dkv, W["wkv"], "nt", "mem_kv_dx", b_pre=(0,))
        G["wkv"] = _mm(s["memn"], dkv, "tn", "mem_kv_dw", out_dtypes=(BF16,))[None]
        _, dsmall["g_mem"][l] = _resnorm_bwd(mem, None, None, P["g_mem"][l][None], None, dmemn, "mem_norm_bwd")
        dhm = _mm(dqm, W["wq"], "nt", "mem_q_dx", b_pre=(0,))
        G["wq"] = _mm(s["hm"], dqm, "tn", "mem_q_dw", **_DW)[None]
        dx0, dz, dgn[l][1], dgn[l][2] = _resnorm_bwd(s["x0"], s["z"], gn(l, 1), gn(l, 2), dx1, dhm, "resnorm_bwd")
        dmerged = _mm(dz, W["wout"], "nt", "out_proj_dx", b_pre=(0,))
        G["wout"] = _mm(s["merged"], dz, "tn", "out_proj_dw", **_DW)[None]
        dproj, dpb = _merge_bwd(s["proj"], s["pb"], dmerged, "merge_bwd")
        dbr = _mm(dpb, W["wbT"], "nn", "branch_proj_dx", batch=3)
        G["wbT"] = _mm(dpb, s["br"], "tn", "branch_proj_dw", batch=3, **_DW)
        (dproj, dsmall["sinks"][l], dsmall["ws"][l], dsmall["bs"][l], dsmall["gsgu"][l], dsmall["wp"][l],
         dsmall["ps"][l]) = _mix_bwd(s["proj"], dbr, dproj, *s["small"], "mix_bwd")
        dh = _mm(dproj, W["winT"], "nn", "in_proj_dx", b_pre=(0,), **_LONG_K)
        G["winT"] = _mm(dproj, s["h"], "tn", "in_proj_dw", **_DW)[None]
        deps = grads_done(l, [G[k] for k in _GRAD_ORDER])
        dres = dx0
    grad_x, dgn[0][0] = _resnorm_bwd(x, None, None, gn(0, 0), dres, dh, "norm_in_bwd", deps=deps)
    tail_deps = grads_advance(grad_x)

    small_grads = dict(
        g_norm=jnp.stack([jnp.concatenate(row, axis=0) for row in dgn]),
        g_mem=jnp.concatenate(dsmall["g_mem"], axis=0),
        sinks=jnp.stack([d[:, 0] for d in dsmall["sinks"]]),
        ws=jnp.stack(dsmall["ws"]),
        bs=jnp.stack([d[:, :, 0] for d in dsmall["bs"]]),
        gsgu=jnp.concatenate(dsmall["gsgu"], axis=0),
        wp=jnp.stack(dsmall["wp"]),
        ps=jnp.concatenate(dsmall["ps"], axis=0),
    )
    return loss, grad_x, small_grads, tail_deps


_PACK_ROWS = 512


def _as_rows(a):
    n = math.prod(a.shape)
    if n % 128:
        a = jnp.pad(a.reshape(-1), (0, (-n) % 128))
    r = a.reshape(-1, 128)
    return jnp.pad(r, ((0, (-r.shape[0]) % 8), (0, 0))) if r.shape[0] % 8 else r


def _pack(arrays):
    rows = [_as_rows(a) for a in arrays]
    total = sum(r.shape[0] for r in rows)
    tail = (-total) % _PACK_ROWS
    if tail:
        rows.append(jnp.zeros((tail, 128), rows[0].dtype))
    return jnp.concatenate(rows, axis=0)


def _unpack(packed, like):
    out, pos = [], 0
    for a in like:
        n = math.prod(a.shape)
        nr = -(-n // 128)
        rows = packed[pos:pos + nr]
        out.append((rows.reshape(-1)[:n] if n % 128 else rows).reshape(a.shape))
        pos += nr + (-nr) % 8
    return out


_BIG = ("w_in", "w_branch", "w_out", "w_q_mem", "w_kv_mem", "w_o_mem", "w_up", "w_down")
_SMALL = ("g_norm", "g_mem", "attn_sinks", "w_spatial", "b_spatial", "g_sgu", "w_pool", "pool_scale")
_WEIGHTS = ("g_norm", "g_mem", "w_in", "attn_sinks", "w_spatial", "b_spatial", "g_sgu", "w_pool", "pool_scale",
            "w_branch", "w_out", "w_q_mem", "w_kv_mem", "w_o_mem", "w_up", "w_down")


def _to_working(name, w):
    if name == "w_in":
        return jnp.swapaxes(w, 1, 2)
    if name == "w_branch":
        t = jnp.swapaxes(w, 2, 3)
        return t.reshape(t.shape[0] * 3, t.shape[2], t.shape[3])
    if name in ("w_o_mem", "w_up"):
        return jnp.swapaxes(w, 1, 2)
    return w


def _from_working(name, g):
    if name == "w_in":
        return jnp.swapaxes(g, 1, 2)
    if name == "w_branch":
        return jnp.swapaxes(g.reshape(g.shape[0] // 3, 3, g.shape[1], g.shape[2]), 2, 3)
    if name in ("w_o_mem", "w_up"):
        return jnp.swapaxes(g, 1, 2)
    return g


_WKEY = dict(w_in="winT", w_branch="wbT", w_out="wout", w_q_mem="wq", w_kv_mem="wkv", w_o_mem="woT",
             w_up="wupT", w_down="wdown")


def kernel(x, mem, g_norm, g_mem, w_in, attn_sinks, w_spatial, b_spatial, g_sgu, w_pool, pool_scale, w_branch, w_out, w_q_mem, w_kv_mem, w_o_mem, w_up, w_down, loss_target, m_g_norm, m_g_mem, m_w_in, m_attn_sinks, m_w_spatial, m_b_spatial, m_g_sgu, m_w_pool, m_pool_scale, m_w_branch, m_w_out, m_w_q_mem, m_w_kv_mem, m_w_o_mem, m_w_up, m_w_down, v_g_norm, v_g_mem, v_w_in, v_attn_sinks, v_w_spatial, v_b_spatial, v_g_sgu, v_w_pool, v_pool_scale, v_w_branch, v_w_out, v_w_q_mem, v_w_kv_mem, v_w_o_mem, v_w_up, v_w_down):
    w = dict(g_norm=g_norm, g_mem=g_mem, w_in=w_in, attn_sinks=attn_sinks, w_spatial=w_spatial, b_spatial=b_spatial,
             g_sgu=g_sgu, w_pool=w_pool, pool_scale=pool_scale, w_branch=w_branch, w_out=w_out, w_q_mem=w_q_mem,
             w_kv_mem=w_kv_mem, w_o_mem=w_o_mem, w_up=w_up, w_down=w_down)
    m = dict(g_norm=m_g_norm, g_mem=m_g_mem, w_in=m_w_in, attn_sinks=m_attn_sinks, w_spatial=m_w_spatial,
             b_spatial=m_b_spatial, g_sgu=m_g_sgu, w_pool=m_w_pool, pool_scale=m_pool_scale, w_branch=m_w_branch,
             w_out=m_w_out, w_q_mem=m_w_q_mem, w_kv_mem=m_w_kv_mem, w_o_mem=m_w_o_mem, w_up=m_w_up, w_down=m_w_down)
    v = dict(g_norm=v_g_norm, g_mem=v_g_mem, w_in=v_w_in, attn_sinks=v_attn_sinks, w_spatial=v_w_spatial,
             b_spatial=v_b_spatial, g_sgu=v_g_sgu, w_pool=v_w_pool, pool_scale=v_pool_scale, w_branch=v_w_branch,
             w_out=v_w_out, w_q_mem=v_w_q_mem, w_kv_mem=v_w_kv_mem, w_o_mem=v_w_o_mem, w_up=v_w_up, w_down=v_w_down)
    L = g_norm.shape[0]
    j = 2 * lax.axis_index("x") + lax.axis_index("y")
    c = lax.axis_index("c")
    j_arr = jnp.reshape(j, (1,)).astype(jnp.int32)
    cj_arr = jnp.stack([c, j]).astype(jnp.int32)

    gs = g_norm.shape[2]
    working = [_to_working(n, w[n]) for n in _BIG]
    per_layer = [wk.shape[0] // L for wk in working]

    def own_slabs(l):
        return [_own_slab(wk, BF16, j_arr, "own_slab", first=l * b, count=b) for wk, b in zip(working, per_layer)]

    first_slabs = own_slabs(0)
    lead = [first_slabs[0], _own_slab(g_norm.reshape(1, L * 6 * gs // 128, 128), F32, j_arr, "own_slab_norm")]
    lead_handle = _start_copies(lead, _gather_plan, 3 * len(lead), "gather_start_lead")
    rest_handle = _start_copies(first_slabs[1:], _gather_plan, 3 * (len(first_slabs) - 1), "gather_start_first",
                                deps=(lead_handle[-1],))
    slabs = {l: own_slabs(l) for l in range(1, L)}
    lead = _gather_forward(_wait_copies(lead_handle, _gather_plan,
                                        [rest_handle[-1]] + [s for l in slabs for s in slabs[l]],
                                        "gather_wait_lead"), "gather_forward_lead")
    gn_full = jnp.transpose(lead[1].reshape(4, L * 6, gs), (1, 0, 2)).reshape(L, 6, 4 * gs)
    P = dict(g_norm=gn_full, g_mem=g_mem, sinks=attn_sinks, ws=w_spatial, bs=b_spatial, gsgu=g_sgu, wp=w_pool,
             ps=pool_scale)
    whole = lambda g: g.reshape(g.shape[0], 8 * g.shape[3], g.shape[4])
    gathered, in_flight = {}, {}

    forwarding = {}

    def prefetch_weights(l, after):
        arrived = _wait_copies(in_flight[l], _gather_plan, (after,), "gather_wait")
        forwarding[l] = _start_copies(arrived, _forward_plan, 3 * len(arrived), "gather_forward_start")
        return (forwarding[l][-1],)

    def layer_weights(l, after):
        if l not in gathered:
            gathered[l] = _wait_copies(forwarding[l], _forward_plan, (after,), "gather_forward_wait")
        return gathered[l]

    def input_weight_of(l, after):
        return whole(lead[0] if l == 0 else layer_weights(l, after)[0])

    def weights_of(l, after):
        deps = ()
        if l == 0:
            rest = _gather_forward(_wait_copies(rest_handle, _gather_plan, (after,), "gather_wait_first"),
                                   "gather_forward_first")
            gathered[0] = [lead[0], *rest]
            dep = rest[0]
            for k in range(1, L):
                in_flight[k] = _start_copies(slabs[k], _gather_plan, 3 * len(slabs[k]), "gather_start", deps=(dep,))
                dep = in_flight[k][-1]
            deps = tuple(h[-1] for h in in_flight.values())
        return {k: whole(g) for k, g in zip(_GRAD_ORDER, layer_weights(l, after))}, deps

    rs = _ReduceScatter(L, cj_arr)
    loss_part, grad_x, sg, tail_deps = _forward_backward(
        x[0], mem[0], loss_target[0], input_weight_of, weights_of, prefetch_weights, P, rs.add_layer, rs.advance)
    loss = lax.psum(loss_part[0, 0], ("x", "y", "c"))

    transposed = ("w_branch", "w_o_mem", "w_up")
    view = {n: (lambda t: jnp.swapaxes(t, 1, 2)) if n == "w_in" else (lambda t: t) for n in _BIG}
    rows = lambda n, t: view[n](t).reshape(-1, view[n](t).shape[-1])
    state = {n: (rows(n, w[n]), rows(n, m[n]), rows(n, v[n])) for n in _BIG}
    updated, upper_grad = {}, {}
    if L > 1:
        for n, gw in zip(_BIG, rs.upper_layers(tail_deps)):
            per = gw.shape[0] // L
            if n in transposed:
                upper_grad[n] = _from_working(n, gw[per:])
                g2d, g_row0 = upper_grad[n].reshape(-1, upper_grad[n].shape[-1]), 0
            else:
                g2d, g_row0 = gw.reshape(-1, gw.shape[-1]), None
            r_layer = state[n][0].shape[0] // L
            updated[n] = _adamw(state[n][0], g2d, state[n][1], state[n][2], "adamw_upper",
                                rows=(r_layer, L * r_layer), g_row0=g_row0)
    full_small = [sg["g_norm"], sg["g_mem"], sg["sinks"], sg["ws"], sg["bs"], sg["gsgu"], sg["wp"], sg["ps"]]
    packed = _pack(full_small)
    pair_sum = _own_slab(packed[None], F32, j_arr, "small_grads_pair_sum",
                         deps=tuple(tail_deps) + tuple(u[0] for u in updated.values()),
                         plus=_pair_swap(packed, "small_grads_swap")[None])
    (chip_sums,) = _gather_weights([pair_sum], "gather_small_grads")
    total = _sum_slots(chip_sums.reshape(4, *packed.shape), "sum_small_grads")
    grads = {}
    for n, g in zip(_SMALL, _unpack(total, full_small)):
        grads[n] = lax.dynamic_slice_in_dim(g, j * g_norm.shape[2], g_norm.shape[2], axis=2) if n == "g_norm" else g

    after = [total] + [u[0] for u in updated.values()]
    delta, new_m, new_v = {}, {}, {}
    for n, gw in zip(_BIG, rs.first_layer(after)):
        per = gw.shape[0] // L
        if n in transposed:
            g0 = _from_working(n, gw[:per])
            grads[n] = jnp.concatenate([g0, upper_grad[n]], axis=0) if L > 1 else g0
            g2d = g0.reshape(-1, g0.shape[-1])
        else:
            grads[n] = _from_working(n, gw)
            g2d = gw.reshape(-1, gw.shape[-1])
        r_layer = state[n][0].shape[0] // L
        d_, m_, v_ = _adamw(state[n][0], g2d, state[n][1], state[n][2], "adamw_first", rows=(0, r_layer), g_row0=0,
                            into=updated.get(n))
        shp = view[n](w[n]).shape
        delta[n], new_m[n], new_v[n] = view[n](d_.reshape(shp)), view[n](m_.reshape(shp)), view[n](v_.reshape(shp))
    small_w = [w[n] for n in _SMALL]
    d_, m_, v_ = _adamw(_pack(small_w), _pack([grads[n] for n in _SMALL]), _pack([m[n] for n in _SMALL]),
                        _pack([v[n] for n in _SMALL]), "adamw_small")
    for n, dd, mm_, vv in zip(_SMALL, _unpack(d_, small_w), _unpack(m_, small_w), _unpack(v_, small_w)):
        delta[n], new_m[n], new_v[n] = dd, mm_, vv

    return (loss, grad_x[None], *[grads[n] for n in _WEIGHTS], *[delta[n] for n in _WEIGHTS],
            *[new_m[n] for n in _WEIGHTS], *[new_v[n] for n in _WEIGHTS])
```

```python
import functools
import math

import jax
import jax.numpy as jnp
from jax import lax
from jax.experimental import pallas as pl
from jax.experimental.pallas import tpu as pltpu

F32 = jnp.float32
BF16 = jnp.bfloat16
MESH = pl.DeviceIdType.MESH

EPS = 1e-6
NEG_INF = -1e30
BLK = 128
HALO = 16
POOL_WINDOWS = (2, 4, 8, 16)
ATT_SCALE = 1.0 / math.sqrt(64.0)
MEM_SCALE = 1.0 / math.sqrt(128.0)
C_Q, C_K, C_V, C_SU, C_SV, C_PC, C_GATE, C_END = 0, 512, 640, 768, 1280, 1792, 2304, 5376

ADAM_LR, ADAM_B1, ADAM_B2, ADAM_EPS, ADAM_WD, ADAM_STEP = 0.001, 0.9, 0.999, 1e-08, 0.01, 10

VMEM_LIMIT_BYTES = 56 * 1024 * 1024

_DIMS = {
    "nn": (((1,), (0,)), ((), ())),
    "nt": (((1,), (1,)), ((), ())),
    "tn": (((0,), (0,)), ((), ())),
}


def _dot(a, b, mode):
    return lax.dot_general(a, b, _DIMS[mode], preferred_element_type=F32)


def _params(semantics):
    return pltpu.CompilerParams(dimension_semantics=semantics, vmem_limit_bytes=VMEM_LIMIT_BYTES)


def _tile(dim, pref):
    if dim <= pref:
        return dim
    t = (pref // 128) * 128
    while t >= 128:
        if dim % t == 0:
            return t
        t -= 128
    raise ValueError(f"no tile for {dim}")


def _rms(x, g):
    return x * lax.rsqrt(jnp.mean(x * x, axis=-1, keepdims=True) + EPS) * g


def _mm(a, b, mode, name, *, out_dtypes=(F32,), a_pre=(), b_pre=(), into=None, out_pre=(),
        extras=(), epi=None, deps=(), batch=None, tm=2048, tn=1024, tk=1024):
    lead = 0 if batch is None else 1
    assert not (lead and (a_pre or b_pre or into is not None or extras))
    a2, b2 = a.shape[len(a_pre) + lead:], b.shape[len(b_pre) + lead:]
    if mode == "nn":
        (M, K), (K2, N) = a2, b2
    elif mode == "nt":
        (M, K), (N, K2) = a2, b2
    else:
        (K, M), (K2, N) = a2, b2
    assert K == K2, (a.shape, b.shape, mode)
    tm, tn, tk = _tile(M, tm), _tile(N, tn), _tile(K, tk)
    nk = K // tk
    na, nb_, no = len(a_pre), len(b_pre), len(out_pre)

    def on_grid(f):
        return f if batch is None else (lambda g, i, j, k: (g,) + f(i, j, k))

    if mode == "tn":
        a_spec = pl.BlockSpec((None,) * (na + lead) + (tk, tm), on_grid(lambda i, j, k: a_pre + (k, i)))
    else:
        a_spec = pl.BlockSpec((None,) * (na + lead) + (tm, tk), on_grid(lambda i, j, k: a_pre + (i, k)))
    if mode == "nt":
        b_spec = pl.BlockSpec((None,) * (nb_ + lead) + (tn, tk), on_grid(lambda i, j, k: b_pre + (j, k)))
    else:
        b_spec = pl.BlockSpec((None,) * (nb_ + lead) + (tk, tn), on_grid(lambda i, j, k: b_pre + (k, j)))
    tile_spec = pl.BlockSpec((None,) * lead + (tm, tn), on_grid(lambda i, j, k: (i, j)))
    ne, nout = len(extras), len(out_dtypes)
    in_specs = [a_spec, b_spec] + [tile_spec] * ne
    operands = [a, b, *extras]
    aliases = {}
    if into is not None:
        assert nout == 1
        in_specs.append(pl.BlockSpec(memory_space=pl.ANY))
        operands.append(into)
        aliases = {len(operands) - 1: 0}
        out_shape = [jax.ShapeDtypeStruct(into.shape, into.dtype)]
        out_specs = [pl.BlockSpec((None,) * no + (tm, tn), lambda i, j, k: out_pre + (i, j))]
    else:
        out_shape = [jax.ShapeDtypeStruct(((batch,) if lead else ()) + (M, N), dt) for dt in out_dtypes]
        out_specs = [tile_spec] * nout
    in_specs += [pl.BlockSpec(memory_space=pl.ANY)] * len(deps)
    operands += list(deps)

    def body(*refs):
        a_ref, b_ref = refs[0], refs[1]
        ex = refs[2:2 + ne]
        pos = 2 + ne + (1 if into is not None else 0) + len(deps)
        outs = refs[pos:pos + nout]
        acc_ref = refs[pos + nout] if nk > 1 else None

        def finish(acc):
            vals = epi(acc, *[e[...] for e in ex]) if epi is not None else (acc,)
            for o, v in zip(outs, vals):
                o[...] = v.astype(o.dtype)

        def prod():
            return _dot(a_ref[...].astype(BF16), b_ref[...].astype(BF16), mode)

        if nk == 1:
            finish(prod())
        else:
            k = pl.program_id(2 + lead)

            @pl.when(k == 0)
            def _():
                acc_ref[...] = jnp.zeros_like(acc_ref)

            acc_ref[...] += prod()

            @pl.when(k == nk - 1)
            def _():
                finish(acc_ref[...])

    res = pl.pallas_call(
        body, name=name, grid=((batch,) if lead else ()) + (M // tm, N // tn, nk),
        in_specs=in_specs, out_specs=out_specs, out_shape=out_shape,
        scratch_shapes=[pltpu.VMEM((tm, tn), F32)] if nk > 1 else [],
        input_output_aliases=aliases,
        compiler_params=_params(("parallel",) * (2 + lead) + ("arbitrary",)),
    )(*operands)
    return res[0] if nout == 1 else tuple(res)


def _resnorm_fn(has_post, has_pre):
    def f(*a):
        x, k = a[0], 1
        if has_post:
            x, k = x + _rms(a[1], a[2]), 3
        outs = [x]
        if has_pre:
            outs.append(_rms(x, a[k]))
        return tuple(outs)
    return f


def _row_spec(T, W):
    return pl.BlockSpec((T, W), lambda i: (i, 0))


def _par_spec(W):
    return pl.BlockSpec((1, W), lambda i: (0, 0))


def _resnorm_fwd(xr, y, gp, gq, name, T=512, deps=()):
    S, D = xr.shape
    T = min(T, S)
    has_post, has_pre = y is not None, gq is not None
    f = _resnorm_fn(has_post, has_pre)
    ins = [xr] + ([y, gp] if has_post else []) + ([gq] if has_pre else [])
    in_specs = [_row_spec(T, D)] + ([_row_spec(T, D), _par_spec(D)] if has_post else []) + ([_par_spec(D)] if has_pre else [])
    out_shape, out_specs = [], []
    if has_post:
        out_shape.append(jax.ShapeDtypeStruct((S, D), F32)); out_specs.append(_row_spec(T, D))
    if has_pre:
        out_shape.append(jax.ShapeDtypeStruct((S, D), BF16)); out_specs.append(_row_spec(T, D))
    n_in, n_dep = len(ins), len(deps)

    def body(*refs):
        vals = f(*[r[...].astype(F32) for r in refs[:n_in]])
        outs = list(refs[n_in + n_dep:])
        if has_post:
            outs.pop(0)[...] = vals[0]
        if has_pre:
            outs.pop(0)[...] = vals[1].astype(BF16)

    res = pl.pallas_call(body, name=name, grid=(S // T,),
                         in_specs=in_specs + [pl.BlockSpec(memory_space=pl.ANY)] * n_dep, out_specs=out_specs,
                         out_shape=out_shape, compiler_params=_params(("parallel",)))(*ins, *deps)
    return tuple(res)


def _resnorm_bwd(xr, y, gp, gq, dres, dh, name, T=512, deps=()):
    S, D = xr.shape
    T = min(T, S)
    has_post, has_pre, has_res = y is not None, gq is not None, dres is not None
    f = _resnorm_fn(has_post, has_pre)
    ins = [xr] + ([y, gp] if has_post else []) + ([gq] if has_pre else [])
    in_specs = [_row_spec(T, D)] + ([_row_spec(T, D), _par_spec(D)] if has_post else []) + ([_par_spec(D)] if has_pre else [])
    n_prim = len(ins)
    if has_res:
        ins.append(dres); in_specs.append(_row_spec(T, D))
    if has_pre:
        ins.append(dh); in_specs.append(_row_spec(T, D))
    n_in, n_dep = len(ins), len(deps)
    out_shape = [jax.ShapeDtypeStruct((S, D), F32)]
    out_specs = [_row_spec(T, D)]
    if has_post:
        out_shape += [jax.ShapeDtypeStruct((S, D), BF16), jax.ShapeDtypeStruct((1, D), F32)]
        out_specs += [_row_spec(T, D), _par_spec(D)]
    if has_pre:
        out_shape.append(jax.ShapeDtypeStruct((1, D), F32)); out_specs.append(_par_spec(D))

    def body(*refs):
        i = pl.program_id(0)
        prim = [r[...].astype(F32) for r in refs[:n_prim]]
        rest = list(refs[n_prim:n_in])
        ct_x = rest.pop(0)[...] if has_res else jnp.zeros((T, D), F32)
        cts = [ct_x]
        if has_pre:
            cts.append(rest.pop(0)[...].astype(F32))
        _, vjp = jax.vjp(f, *prim)
        grads = list(vjp(tuple(cts)))
        outs = list(refs[n_in + n_dep:])
        outs.pop(0)[...] = grads.pop(0)
        acc = []
        if has_post:
            outs.pop(0)[...] = grads.pop(0).astype(BF16)
            acc.append((outs.pop(0), grads.pop(0)))
        if has_pre:
            acc.append((outs.pop(0), grads.pop(0)))

        @pl.when(i == 0)
        def _():
            for o, _g in acc:
                o[...] = jnp.zeros_like(o)

        for o, g in acc:
            o[...] += g

    res = pl.pallas_call(body, name=name, grid=(S // T,),
                         in_specs=in_specs + [pl.BlockSpec(memory_space=pl.ANY)] * n_dep, out_specs=out_specs,
                         out_shape=out_shape, compiler_params=_params(("arbitrary",)))(*ins, *deps)
    return tuple(res)


def _final_fwd(xr, y, gp, target, name, T=512):
    S, D = xr.shape
    T = min(T, S)

    def body(x_ref, y_ref, g_ref, t_ref, dy_ref, loss_ref):
        i = pl.program_id(0)
        e = x_ref[...] + _rms(y_ref[...].astype(F32), g_ref[...]) - t_ref[...]
        dy_ref[...] = e / D

        @pl.when(i == 0)
        def _():
            loss_ref[...] = jnp.zeros_like(loss_ref)

        loss_ref[...] += 0.5 * jnp.sum(jnp.sum(e * e, axis=-1, keepdims=True) / D, axis=0, keepdims=True)

    return pl.pallas_call(
        body, name=name, grid=(S // T,),
        in_specs=[_row_spec(T, D), _row_spec(T, D), _par_spec(D), _row_spec(T, D)],
        out_specs=[_row_spec(T, D), pl.BlockSpec((1, 128), lambda i: (0, 0))],
        out_shape=[jax.ShapeDtypeStruct((S, D), F32), jax.ShapeDtypeStruct((1, 128), F32)],
        compiler_params=_params(("arbitrary",)))(xr, y, gp, target)


_STRAIGHT_HEADS = (0, 2, 5, 7)
_ROLLED_HEADS = (1, 3, 4, 6)


def _straight_lanes():
    r = lax.broadcasted_iota(jnp.int32, (4 * BLK, BLK), 0)
    c = lax.broadcasted_iota(jnp.int32, (4 * BLK, BLK), 1)
    return (r < 2 * BLK) == (c < 64)


def _att_mask(not_first):
    k = lax.broadcasted_iota(jnp.int32, (2 * BLK, 4 * BLK), 0)
    q = lax.broadcasted_iota(jnp.int32, (2 * BLK, 4 * BLK), 1) % BLK
    qc, kc = 2 + q // 64, k // 64
    return (kc <= qc) & (kc >= qc - 2) & (not_first | (k >= BLK))


def _sink_row(sk_ref, heads):
    return jnp.concatenate([sk_ref[h:h + 1, :] for h in heads], axis=1)


def _softmax_sink(s, sk):
    m = jnp.maximum(jnp.max(s, axis=0, keepdims=True), sk)
    e = jnp.exp(s - m)
    es = jnp.exp(sk - m)
    inv = 1.0 / (jnp.sum(e, axis=0, keepdims=True) + es)
    return e * inv, es * inv


def _softmax_rows(s):
    e = jnp.exp(s - jnp.max(s, axis=0, keepdims=True))
    return e * (1.0 / jnp.sum(e, axis=0, keepdims=True))


_GELU_C = math.sqrt(2.0 / math.pi)
_GELU_A = 0.044715


def _gelu_with_slope(x):
    x2 = x * x
    t = jnp.tanh(_GELU_C * (x + _GELU_A * (x2 * x)))
    half = 0.5 * (1.0 + t)
    return x * half, half + (0.5 * _GELU_C) * x * (1.0 - t * t) * (1.0 + (3.0 * _GELU_A) * x2)


def _att_bands(cur, kvp):
    kband = jnp.concatenate([kvp[:, 0:BLK], cur[:, C_K:C_K + BLK]], axis=0)
    vband = jnp.concatenate([kvp[:, BLK:2 * BLK], cur[:, C_V:C_V + BLK]], axis=0)
    return kband, pltpu.roll(kband, 64, 1), vband, pltpu.roll(vband, 64, 1)


def _stack_tiles(ref_or_val, start):
    return jnp.concatenate([ref_or_val[:, start + BLK * t:start + BLK * (t + 1)] for t in range(4)], axis=0)


def _sgu_mask():
    r = lax.broadcasted_iota(jnp.int32, (BLK, BLK), 0)
    c = lax.broadcasted_iota(jnp.int32, (BLK, BLK), 1)
    return (c // 64) <= (r // 64)


def _trailing_sums(ext, g):
    s, shift = ext, 1
    for _ in range(g + 1):
        s = s + pltpu.roll(s, shift, 0)
        shift *= 2
    return s


def _leading_sums(z, g):
    d, shift = z, 1
    for _ in range(g + 1):
        d = d + pltpu.roll(d, z.shape[0] - shift, 0)
        shift *= 2
    return d


def _pool_cnt(blk, w):
    t = blk * BLK + lax.broadcasted_iota(jnp.int32, (BLK, 1), 0)
    return jnp.minimum(t + 1, w).astype(F32)


def _mix_in_specs(nb, rev):
    def b(i):
        return nb - 1 - i if rev else i
    return [
        pl.BlockSpec((BLK, C_GATE), lambda i: (b(i), 0)),
        pl.BlockSpec((BLK, 2 * BLK), lambda i: (jnp.maximum(b(i) - 1, 0), C_K // (2 * BLK))),
        pl.BlockSpec((HALO, C_GATE), lambda i: (jnp.maximum(b(i) * (BLK // HALO) - 1, 0), 0)),
        pl.BlockSpec((8, BLK), lambda i: (0, 0)),
        pl.BlockSpec((4, BLK, BLK), lambda i: (0, 0, 0)),
        pl.BlockSpec((4, BLK, 1), lambda i: (0, 0, 0)),
        pl.BlockSpec((1, 512), lambda i: (0, 0)),
        pl.BlockSpec((4, BLK, BLK), lambda i: (0, 0, 0)),
        pl.BlockSpec((1, 512), lambda i: (0, 0)),
    ]


def _mix_fwd(proj, sinks_b, ws, bs3, gsgu, wp, ps, name):
    S = proj.shape[0]
    nb = S // BLK

    def body(cur_ref, kvp_ref, pcp_ref, sk_ref, ws_ref, bs_ref, gs_ref, wp_ref, ps_ref, br_ref, ext_ref):
        i = pl.program_id(0)
        not_first = i > 0
        cur, kvp = cur_ref[...].astype(F32), kvp_ref[...].astype(F32)
        mask = _att_mask(not_first)
        own = _straight_lanes()
        q = _stack_tiles(cur, C_Q)
        outs = []
        kband, kroll, vband, vroll = _att_bands(cur, kvp)
        for qg, kg, vg, heads in ((jnp.where(own, q, 0.0), kband, vband, _STRAIGHT_HEADS),
                                  (jnp.where(own, 0.0, q), kroll, vroll, _ROLLED_HEADS)):
            s = jnp.where(mask, _dot(kg.astype(BF16), qg.astype(BF16), "nt") * ATT_SCALE, NEG_INF)
            p, _ = _softmax_sink(s, _sink_row(sk_ref, heads))
            outs.append(_dot(p.astype(BF16), vg.astype(BF16), "tn"))
        o = jnp.where(own, outs[0], outs[1])
        for t in range(4):
            br_ref[0, :, BLK * t:BLK * (t + 1)] = o[BLK * t:BLK * (t + 1)].astype(BF16)
        gu = jax.nn.gelu(cur[:, C_SU:C_SV])
        vn = _rms(jax.nn.gelu(cur[:, C_SV:C_PC]), gs_ref[...]).astype(BF16)
        wmask = _sgu_mask()
        for g in range(4):
            wm = jnp.where(wmask, ws_ref[g], 0.0).astype(BF16)
            sp = _dot(wm, vn[:, BLK * g:BLK * (g + 1)], "nn") + bs_ref[g]
            br_ref[1, :, BLK * g:BLK * (g + 1)] = (gu[:, BLK * g:BLK * (g + 1)] * sp).astype(BF16)
        c = cur[:, C_PC:C_GATE]
        ext_ref[0:HALO, :] = jnp.where(not_first, pcp_ref[:, C_PC:C_GATE].astype(F32), 0.0)
        ext_ref[HALO:HALO + BLK, :] = c
        for g, w in enumerate(POOL_WINDOWS):
            sl = slice(BLK * g, BLK * (g + 1))
            acc = _trailing_sums(ext_ref[:, sl], g)[HALO:]
            pooled = acc * (1.0 / _pool_cnt(i, w)) - c[:, sl]
            mixed = _dot(pooled.astype(BF16), wp_ref[g].astype(BF16), "nn")
            br_ref[2, :, sl] = (mixed * ps_ref[:, sl]).astype(BF16)

    return pl.pallas_call(
        body, name=name, grid=(nb,),
        in_specs=_mix_in_specs(nb, False),
        out_specs=pl.BlockSpec((3, BLK, 512), lambda i: (0, i, 0)),
        out_shape=jax.ShapeDtypeStruct((3, S, 512), BF16),
        scratch_shapes=[pltpu.VMEM((HALO + BLK, 512), F32)],
        compiler_params=_params(("parallel",)),
    )(proj, proj, proj, sinks_b, ws, bs3, gsgu, wp, ps)


def _mix_bwd(proj, dbr, dproj, sinks_b, ws, bs3, gsgu, wp, ps, name):
    S = proj.shape[0]
    nb = S // BLK

    def body(cur_ref, kvp_ref, pcp_ref, sk_ref, ws_ref, bs_ref, gs_ref, wp_ref, ps_ref, dbr_ref, _dproj_in,
             dp_ref, dsk_ref, dws_ref, dbs_ref, dgs_ref, dwp_ref, dps_ref,
             ext_ref, z_ref, ckv_ref, cpc_ref):
        i = pl.program_id(0)
        blk = nb - 1 - i
        not_first = blk > 0

        @pl.when(i == 0)
        def _():
            for r in (dsk_ref, dws_ref, dbs_ref, dgs_ref, dwp_ref, dps_ref, ckv_ref, cpc_ref, z_ref):
                r[...] = jnp.zeros_like(r)

        cur, kvp = cur_ref[...].astype(F32), kvp_ref[...].astype(F32)
        mask = _att_mask(not_first)
        own = _straight_lanes()
        q = _stack_tiles(cur, C_Q)
        do = jnp.concatenate([dbr_ref[0, :, BLK * t:BLK * (t + 1)] for t in range(4)], axis=0)
        kband, kroll, vband, vroll = _att_bands(cur, kvp)
        dqs, dks, dvs = [], [], []
        for qg, dog, kg, vg, heads in (
                (jnp.where(own, q, 0.0), jnp.where(own, do, 0.0), kband, vband, _STRAIGHT_HEADS),
                (jnp.where(own, 0.0, q), jnp.where(own, 0.0, do), kroll, vroll, _ROLLED_HEADS)):
            qg, dog, kg, vg = qg.astype(BF16), dog.astype(BF16), kg.astype(BF16), vg.astype(BF16)
            s = jnp.where(mask, _dot(kg, qg, "nt") * ATT_SCALE, NEG_INF)
            p, p_sink = _softmax_sink(s, _sink_row(sk_ref, heads))
            dp = _dot(vg, dog, "nt")
            rs = jnp.sum(p * dp, axis=0, keepdims=True)
            ds = (p * (dp - rs) * ATT_SCALE).astype(BF16)
            sink_row = p_sink * rs
            for t, h in enumerate(heads):
                dsk_ref[h:h + 1, :] += jnp.broadcast_to(
                    -jnp.sum(sink_row[:, BLK * t:BLK * (t + 1)], axis=1, keepdims=True), (1, BLK))
            dvs.append(_dot(p.astype(BF16), dog, "nn"))
            dks.append(_dot(ds, qg, "nn"))
            dqs.append(_dot(ds, kg, "tn"))
        dq = jnp.where(own, dqs[0], dqs[1])
        for t in range(4):
            dp_ref[:, C_Q + BLK * t:C_Q + BLK * (t + 1)] = dq[BLK * t:BLK * (t + 1)].astype(BF16)
        dk = dks[0] + pltpu.roll(dks[1], 64, 1)
        dv = dvs[0] + pltpu.roll(dvs[1], 64, 1)
        dp_ref[:, C_K:C_K + BLK] = (dk[BLK:] + ckv_ref[:, 0:BLK]).astype(BF16)
        dp_ref[:, C_V:C_V + BLK] = (dv[BLK:] + ckv_ref[:, BLK:]).astype(BF16)
        ckv_ref[:, 0:BLK] = dk[:BLK]
        ckv_ref[:, BLK:] = dv[:BLK]
        su, sv = cur[:, C_SU:C_SV], cur[:, C_SV:C_PC]
        gu, gu_slope = _gelu_with_slope(su)
        gv, gv_slope = _gelu_with_slope(sv)
        vn, vjp_v = jax.vjp(_rms, gv, gs_ref[...])
        vn16 = vn.astype(BF16)
        wmask = _sgu_mask()
        dgu, dvn = [], []
        for g in range(4):
            sl = slice(BLK * g, BLK * (g + 1))
            wm = jnp.where(wmask, ws_ref[g], 0.0).astype(BF16)
            sp = _dot(wm, vn16[:, sl], "nn") + bs_ref[g]
            dyb = dbr_ref[1, :, sl]
            dgu.append(dyb * sp)
            dsp = dyb * gu[:, sl]
            dsp16 = dsp.astype(BF16)
            dvn.append(_dot(wm, dsp16, "tn"))
            dws_ref[g] += jnp.where(wmask, _dot(dsp16, vn16[:, sl], "nt"), 0.0)
            dbs_ref[g] += jnp.sum(dsp, axis=1, keepdims=True)
        dgv, dgs = vjp_v(jnp.concatenate(dvn, axis=1))
        dp_ref[:, C_SU:C_SV] = (jnp.concatenate(dgu, axis=1) * gu_slope).astype(BF16)
        dp_ref[:, C_SV:C_PC] = (dgv * gv_slope).astype(BF16)
        dgs_ref[...] += dgs
        c = cur[:, C_PC:C_GATE]
        ext_ref[0:HALO, :] = jnp.where(not_first, pcp_ref[:, C_PC:C_GATE].astype(F32), 0.0)
        ext_ref[HALO:HALO + BLK, :] = c
        for g, w in enumerate(POOL_WINDOWS):
            sl = slice(BLK * g, BLK * (g + 1))
            acc = _trailing_sums(ext_ref[:, sl], g)[HALO:]
            inv_cnt = 1.0 / _pool_cnt(blk, w)
            pooled16 = (acc * inv_cnt - c[:, sl]).astype(BF16)
            wp16 = wp_ref[g].astype(BF16)
            mixed = _dot(pooled16, wp16, "nn")
            dyc = dbr_ref[2, :, sl]
            dps_ref[:, sl] += jnp.sum(dyc * mixed, axis=0, keepdims=True)
            dmixed16 = (dyc * ps_ref[:, sl]).astype(BF16)
            dwp_ref[g] += _dot(pooled16, dmixed16, "tn")
            dpooled = _dot(dmixed16, wp16, "nt")
            z_ref[HALO:HALO + BLK, sl] = dpooled * inv_cnt
            dext = _leading_sums(z_ref[:, sl], g)[:HALO + BLK]
            dp_ref[:, C_PC + BLK * g:C_PC + BLK * (g + 1)] = (
                dext[HALO:] - dpooled + jnp.concatenate([jnp.zeros((BLK - HALO, BLK), F32), cpc_ref[:, sl]], axis=0)
            ).astype(BF16)
            cpc_ref[:, sl] = dext[:HALO]

    n_in = 11
    small = [jax.ShapeDtypeStruct((8, BLK), F32), jax.ShapeDtypeStruct((4, BLK, BLK), F32),
             jax.ShapeDtypeStruct((4, BLK, 1), F32), jax.ShapeDtypeStruct((1, 512), F32),
             jax.ShapeDtypeStruct((4, BLK, BLK), F32), jax.ShapeDtypeStruct((1, 512), F32)]
    small_specs = [pl.BlockSpec((8, BLK), lambda i: (0, 0)), pl.BlockSpec((4, BLK, BLK), lambda i: (0, 0, 0)),
                   pl.BlockSpec((4, BLK, 1), lambda i: (0, 0, 0)), pl.BlockSpec((1, 512), lambda i: (0, 0)),
                   pl.BlockSpec((4, BLK, BLK), lambda i: (0, 0, 0)), pl.BlockSpec((1, 512), lambda i: (0, 0))]
    res = pl.pallas_call(
        body, name=name, grid=(nb,),
        in_specs=_mix_in_specs(nb, True) + [
            pl.BlockSpec((3, BLK, 512), lambda i: (0, nb - 1 - i, 0)),
            pl.BlockSpec(memory_space=pl.ANY)],
        out_specs=[pl.BlockSpec((BLK, C_GATE), lambda i: (nb - 1 - i, 0))] + small_specs,
        out_shape=[jax.ShapeDtypeStruct(dproj.shape, dproj.dtype)] + small,
        scratch_shapes=[pltpu.VMEM((HALO + BLK, 512), F32), pltpu.VMEM((2 * HALO + BLK, 512), F32),
                        pltpu.VMEM((BLK, 2 * BLK), F32), pltpu.VMEM((HALO, 512), F32)],
        input_output_aliases={n_in - 1: 0},
        compiler_params=_params(("arbitrary",)),
    )(proj, proj, proj, sinks_b, ws, bs3, gsgu, wp, ps, dbr, dproj)
    return tuple(res)


_GW = 256


def _merge_fwd(proj, pb, name, T=4096):
    S, D = pb.shape[1], pb.shape[2]
    T = min(T, S)

    def body(gate_ref, pb_ref, out_ref, acc_ref):
        n = pl.program_id(2)

        @pl.when(n == 0)
        def _():
            acc_ref[...] = jnp.zeros_like(acc_ref)

        acc_ref[...] += jax.nn.sigmoid(gate_ref[...].astype(F32)) * pb_ref[...]

        @pl.when(n == 2)
        def _():
            out_ref[...] = acc_ref[...].astype(BF16)

    return pl.pallas_call(
        body, name=name, grid=(S // T, D // _GW, 3),
        in_specs=[pl.BlockSpec((T, _GW), lambda i, j, n: (i, C_GATE // _GW + n * (D // _GW) + j)),
                  pl.BlockSpec((None, T, _GW), lambda i, j, n: (n, i, j))],
        out_specs=pl.BlockSpec((T, _GW), lambda i, j, n: (i, j)),
        out_shape=jax.ShapeDtypeStruct((S, D), BF16),
        scratch_shapes=[pltpu.VMEM((T, _GW), F32)],
        compiler_params=_params(("parallel", "parallel", "arbitrary")),
    )(proj, pb)


def _merge_bwd(proj, pb, dmerged, name, T=4096):
    S, D = pb.shape[1], pb.shape[2]
    T = min(T, S)

    def body(gate_ref, pb_ref, dm_ref, dgate_ref, dpb_ref):
        sg = jax.nn.sigmoid(gate_ref[...].astype(F32))
        dm = dm_ref[...]
        dpb_ref[...] = (dm * sg).astype(BF16)
        dgate_ref[...] = (dm * pb_ref[...] * sg * (1.0 - sg)).astype(BF16)

    gate_map = lambda i, n, j: (i, C_GATE // _GW + n * (D // _GW) + j)
    return pl.pallas_call(
        body, name=name, grid=(S // T, 3, D // _GW),
        in_specs=[pl.BlockSpec((T, _GW), gate_map),
                  pl.BlockSpec((None, T, _GW), lambda i, n, j: (n, i, j)),
                  pl.BlockSpec((T, _GW), lambda i, n, j: (i, j))],
        out_specs=[pl.BlockSpec((T, _GW), gate_map),
                   pl.BlockSpec((None, T, _GW), lambda i, n, j: (n, i, j))],
        out_shape=[jax.ShapeDtypeStruct((S, C_END), BF16), jax.ShapeDtypeStruct((3, S, D), BF16)],
        compiler_params=_params(("parallel", "parallel", "parallel")),
    )(proj, pb, dmerged)


def _memattn_fwd(qm, kv, name, T=512):
    S, NM = qm.shape[0], kv.shape[0]
    T = min(T, S)

    def body(q_ref, kv_ref, o_ref):
        for h in range(4):
            sl = slice(128 * h, 128 * (h + 1))
            k = kv_ref[:, sl].astype(BF16)
            v = kv_ref[:, 512 + 128 * h:512 + 128 * (h + 1)].astype(BF16)
            s = _dot(k, q_ref[:, sl].astype(BF16), "nt") * MEM_SCALE
            p = _softmax_rows(s)
            o_ref[:, sl] = _dot(p.astype(BF16), v, "tn").astype(BF16)

    return pl.pallas_call(
        body, name=name, grid=(S // T,),
        in_specs=[_row_spec(T, 512), pl.BlockSpec((NM, 1024), lambda i: (0, 0))],
        out_specs=_row_spec(T, 512), out_shape=jax.ShapeDtypeStruct((S, 512), BF16),
        compiler_params=_params(("parallel",)))(qm, kv)


def _memattn_bwd(qm, kv, dom, name, T=512):
    S, NM = qm.shape[0], kv.shape[0]
    T = min(T, S)

    def body(q_ref, kv_ref, do_ref, dq_ref, dkv_ref):
        i = pl.program_id(0)

        @pl.when(i == 0)
        def _():
            dkv_ref[...] = jnp.zeros_like(dkv_ref)

        for h in range(4):
            sl = slice(128 * h, 128 * (h + 1))
            sv_ = slice(512 + 128 * h, 512 + 128 * (h + 1))
            q = q_ref[:, sl].astype(BF16)
            k = kv_ref[:, sl].astype(BF16)
            v = kv_ref[:, sv_].astype(BF16)
            do = do_ref[:, sl].astype(BF16)
            p = _softmax_rows(_dot(k, q, "nt") * MEM_SCALE)
            dp = _dot(v, do, "nt")
            ds = (p * (dp - jnp.sum(p * dp, axis=0, keepdims=True)) * MEM_SCALE).astype(BF16)
            dq_ref[:, sl] = _dot(ds, k, "tn").astype(BF16)
            dkv_ref[:, sl] += _dot(ds, q, "nn")
            dkv_ref[:, sv_] += _dot(p.astype(BF16), do, "nn")

    return pl.pallas_call(
        body, name=name, grid=(S // T,),
        in_specs=[_row_spec(T, 512), pl.BlockSpec((NM, 1024), lambda i: (0, 0)), _row_spec(T, 512)],
        out_specs=[_row_spec(T, 512), pl.BlockSpec((NM, 1024), lambda i: (0, 0))],
        out_shape=[jax.ShapeDtypeStruct((S, 512), BF16), jax.ShapeDtypeStruct((NM, 1024), F32)],
        compiler_params=_params(("arbitrary",)))(qm, kv, dom)


def _adamw(w, g, m, v, name, rows=None, g_row0=None, into=None, TR=512):
    R, C = w.shape
    lo, hi = rows if rows is not None else (0, R)
    g0 = lo if g_row0 is None else g_row0
    TR = _row_tile(math.gcd(math.gcd(lo, g0), hi - lo), TR)
    c1 = 1.0 - ADAM_B1 ** ADAM_STEP
    c2 = 1.0 - ADAM_B2 ** ADAM_STEP

    def body(w_ref, g_ref, m_ref, v_ref, *rest):
        d_ref, nm_ref, nv_ref = rest[-3:]
        gv = g_ref[...]
        nm = ADAM_B1 * m_ref[...] + (1.0 - ADAM_B1) * gv
        nv = ADAM_B2 * v_ref[...] + (1.0 - ADAM_B2) * jnp.square(gv)
        d_ref[...] = -ADAM_LR * ((nm / c1) / (jnp.sqrt(nv / c2) + ADAM_EPS) + ADAM_WD * w_ref[...])
        nm_ref[...] = nm
        nv_ref[...] = nv

    spec = pl.BlockSpec((TR, C), lambda i: (lo // TR + i, 0))
    g_spec = pl.BlockSpec((TR, C), lambda i: (g0 // TR + i, 0))
    prior = list(into) if into is not None else []
    return pl.pallas_call(
        body, name=name, grid=((hi - lo) // TR,),
        in_specs=[spec, g_spec, spec, spec] + [pl.BlockSpec(memory_space=pl.ANY)] * len(prior), out_specs=[spec] * 3,
        out_shape=[jax.ShapeDtypeStruct((R, C), F32)] * 3,
        input_output_aliases={4 + k: k for k in range(len(prior))},
        compiler_params=_params(("parallel",)))(w, g, m, v, *prior)


def _row_tile(R, pref):
    t = (pref // 8) * 8
    while t >= 8:
        if R % t == 0:
            return t
        t -= 8
    raise ValueError(f"no row tile for {R}")


def _sum_slots(stack, name, TR=512):
    n, R, C = stack.shape
    TR = R if R <= TR else _row_tile(R, TR)

    def body(s_ref, o_ref):
        acc = s_ref[0]
        for k in range(1, n):
            acc = acc + s_ref[k]
        o_ref[...] = acc

    return pl.pallas_call(
        body, name=name, grid=(R // TR,),
        in_specs=[pl.BlockSpec((n, TR, C), lambda i: (0, i, 0))],
        out_specs=pl.BlockSpec((TR, C), lambda i: (i, 0)),
        out_shape=jax.ShapeDtypeStruct((R, C), F32),
        compiler_params=_params(("parallel",)))(stack)


_ANY = pl.BlockSpec(memory_space=pl.ANY)


def _chip_of(j, c):
    return (j // 2, j % 2, c)


def _own_slab(shard, dtype, j_arr, name, first=0, count=None, plus=None, deps=(), TR=512):
    N, r, C = shard.shape
    B = N if count is None else count
    rh = r // 2
    TR = rh if rh <= TR else _row_tile(rh, TR)
    nt = rh // TR
    ins = [shard] if plus is None else [shard, plus]

    def body(j_ref, *refs):
        val = refs[0][...] if plus is None else refs[0][...] + refs[1][...]
        refs[-1][...] = val.astype(refs[-1].dtype)

    return pl.pallas_call(
        body, name=name,
        grid_spec=pltpu.PrefetchScalarGridSpec(
            num_scalar_prefetch=1, grid=(B, 2, nt),
            in_specs=[pl.BlockSpec((None, TR, C), lambda b, h, t, jr: (first + b, h * nt + t, 0))] * len(ins)
            + [_ANY] * len(deps),
            out_specs=pl.BlockSpec((None, None, None, TR, C), lambda b, h, t, jr: (b, jr[0], h, t, 0))),
        out_shape=jax.ShapeDtypeStruct((B, 4, 2, rh, C), dtype),
        compiler_params=_params(("parallel", "parallel", "parallel")),
    )(j_arr, *ins, *deps)


def _gather_weights(bufs, name):
    n = len(bufs)

    def body(*refs):
        buf = refs[n:2 * n]
        send_sems, recv_sems, fsend_sems, frecv_sems = refs[2 * n:]
        x, y, c = lax.axis_index("x"), lax.axis_index("y"), lax.axis_index("c")
        j = 2 * x + y
        sib = (x, y, 1 - c)
        sends = []
        for d in range(1, 4):
            for a in range(n):
                cp = pltpu.make_async_remote_copy(
                    src_ref=buf[a].at[:, j, c], dst_ref=buf[a].at[:, j, c], send_sem=send_sems.at[a, d - 1],
                    recv_sem=recv_sems.at[a, d - 1], device_id=_chip_of((j + d) % 4, c), device_id_type=MESH)
                cp.start()
                sends.append(cp)
        for d in range(1, 4):
            frm = (j + 4 - d) % 4
            for a in range(n):
                pltpu.make_async_remote_copy(
                    src_ref=buf[a].at[:, frm, c], dst_ref=buf[a].at[:, frm, c], send_sem=send_sems.at[a, d - 1],
                    recv_sem=recv_sems.at[a, d - 1], device_id=_chip_of(frm, c), device_id_type=MESH).wait_recv()
                cp = pltpu.make_async_remote_copy(
                    src_ref=buf[a].at[:, frm, c], dst_ref=buf[a].at[:, frm, c], send_sem=fsend_sems.at[a, d - 1],
                    recv_sem=frecv_sems.at[a, d - 1], device_id=sib, device_id_type=MESH)
                cp.start()
                sends.append(cp)
        for d in range(1, 4):
            frm = (j + 4 - d) % 4
            for a in range(n):
                pltpu.make_async_remote_copy(
                    src_ref=buf[a].at[:, frm, 1 - c], dst_ref=buf[a].at[:, frm, 1 - c], send_sem=fsend_sems.at[a, d - 1],
                    recv_sem=frecv_sems.at[a, d - 1], device_id=sib, device_id_type=MESH).wait_recv()
        for cp in sends:
            cp.wait_send()

    return pl.pallas_call(
        body, name=name,
        in_specs=[_ANY] * n, out_specs=[_ANY] * n,
        out_shape=[jax.ShapeDtypeStruct(b.shape, b.dtype) for b in bufs],
        scratch_shapes=[pltpu.SemaphoreType.DMA((n, 3))] * 4,
        input_output_aliases={a: a for a in range(n)},
    )(*bufs)


_HBM = pl.BlockSpec(memory_space=pltpu.HBM)
_SEM = pl.BlockSpec(memory_space=pltpu.SEMAPHORE)
_DATAFLOW = pltpu.SideEffectType.DATAFLOW_SIDE_EFFECTING


def _in_hbm(arrays):
    return [pltpu.with_memory_space_constraint(a, pltpu.HBM) for a in arrays]


def _start_copies(bufs, plan, count, name, deps=()):
    n, k = len(bufs), len(deps)

    def body(*refs):
        send_sems, recv_sems = refs[n + k], refs[n + k + 1]
        for i, (src, dst, dev) in enumerate(plan(refs[:n], False)):
            pltpu.make_async_remote_copy(src_ref=src, dst_ref=dst, send_sem=send_sems.at[i], recv_sem=recv_sems.at[i],
                                         device_id=dev, device_id_type=MESH).start()
        refs[-1][...] = jnp.zeros_like(refs[-1])

    return pl.pallas_call(
        body, name=name,
        out_shape=(pltpu.SemaphoreType.DMA((count,)), pltpu.SemaphoreType.DMA((count,)),
                   *[pltpu.HBM(b.shape, b.dtype) for b in bufs], jax.ShapeDtypeStruct((8, 128), F32)),
        in_specs=[_HBM] * n + [_ANY] * k,
        out_specs=(_SEM, _SEM, *[_HBM] * n, pl.BlockSpec(memory_space=pltpu.VMEM)),
        input_output_aliases={a: 2 + a for a in range(n)},
        compiler_params=pltpu.CompilerParams(has_side_effects=_DATAFLOW),
    )(*_in_hbm(bufs), *deps)


def _wait_copies(handle, plan, afters, name):
    send_sems, recv_sems, *bufs = handle[:-1]
    n = len(bufs)

    def body(*refs):
        send_sems, recv_sems = refs[n], refs[n + 1]
        for i, (src, dst, dev) in enumerate(plan(refs[:n], True)):
            cp = pltpu.make_async_remote_copy(src_ref=src, dst_ref=dst, send_sem=send_sems.at[i], recv_sem=recv_sems.at[i],
                                              device_id=dev, device_id_type=MESH)
            cp.wait_send()
            cp.wait_recv()

    return list(pl.pallas_call(
        body, name=name,
        out_shape=[pltpu.HBM(b.shape, b.dtype) for b in bufs],
        in_specs=[_HBM] * n + [_SEM, _SEM] + [_ANY] * len(afters), out_specs=[_HBM] * n,
        input_output_aliases={a: a for a in range(n)},
        compiler_params=pltpu.CompilerParams(has_side_effects=_DATAFLOW),
    )(*bufs, send_sems, recv_sems, *afters))


def _gather_plan(buf, waiting):
    c = lax.axis_index("c")
    j = 2 * lax.axis_index("x") + lax.axis_index("y")
    copies = []
    for d in range(1, 4):
        to, frm = (j + d) % 4, (j + 4 - d) % 4
        for b in buf:
            copies.append((b.at[:, j, c], b.at[:, frm if waiting else j, c], _chip_of(frm if waiting else to, c)))
    return copies


def _chip_plan(buf, waiting):
    n = len(buf) // 2
    c = lax.axis_index("c")
    j = 2 * lax.axis_index("x") + lax.axis_index("y")
    copies = []
    for d in range(1, 4):
        to = (j + d) % 4
        for a in range(n):
            copies.append((buf[a].at[to], buf[n + a].at[d - 1], _chip_of(to, c)))
    return copies


def _pair_plan(buf, waiting):
    n = len(buf) // 2
    c = lax.axis_index("c")
    sib = (lax.axis_index("x"), lax.axis_index("y"), 1 - c)
    return [(buf[a].at[:, pl.ds(1 - c, 1)], buf[n + a], sib) for a in range(n)]


def _forward_plan(buf, waiting):
    x, y, c = lax.axis_index("x"), lax.axis_index("y"), lax.axis_index("c")
    j = 2 * x + y
    copies = []
    for d in range(1, 4):
        frm = (j + 4 - d) % 4
        for b in buf:
            copies.append((b.at[:, frm, c], b.at[:, frm, 1 - c if waiting else c], (x, y, 1 - c)))
    return copies


def _gather_forward(bufs, name):
    n = len(bufs)

    def body(*refs):
        buf = refs[n:2 * n]
        send_sems, recv_sems = refs[2 * n:]
        x, y, c = lax.axis_index("x"), lax.axis_index("y"), lax.axis_index("c")
        j = 2 * x + y
        sib = (x, y, 1 - c)
        sends = []
        for d in range(1, 4):
            frm = (j + 4 - d) % 4
            for a in range(n):
                cp = pltpu.make_async_remote_copy(
                    src_ref=buf[a].at[:, frm, c], dst_ref=buf[a].at[:, frm, c], send_sem=send_sems.at[a, d - 1],
                    recv_sem=recv_sems.at[a, d - 1], device_id=sib, device_id_type=MESH)
                cp.start()
                sends.append(cp)
        for d in range(1, 4):
            frm = (j + 4 - d) % 4
            for a in range(n):
                pltpu.make_async_remote_copy(
                    src_ref=buf[a].at[:, frm, 1 - c], dst_ref=buf[a].at[:, frm, 1 - c], send_sem=send_sems.at[a, d - 1],
                    recv_sem=recv_sems.at[a, d - 1], device_id=sib, device_id_type=MESH).wait_recv()
        for cp in sends:
            cp.wait_send()

    return pl.pallas_call(
        body, name=name,
        in_specs=[_ANY] * n, out_specs=[_ANY] * n,
        out_shape=[jax.ShapeDtypeStruct(b.shape, b.dtype) for b in bufs],
        scratch_shapes=[pltpu.SemaphoreType.DMA((n, 3))] * 2,
        input_output_aliases={a: a for a in range(n)},
    )(*bufs)


def _pair_add(g4, r1, cj_arr, name, TR=512):
    B4, _, rh, C = g4.shape
    B = B4 // 4
    TR = rh if rh <= TR else _row_tile(rh, TR)

    def body(cj_ref, g_ref, r_ref, o16_ref, own_ref):
        s = g_ref[...].astype(F32) + r_ref[...].astype(F32)
        o16_ref[...] = s.astype(BF16)

        @pl.when(pl.program_id(2) == cj_ref[1])
        def _():
            own_ref[...] = s

    return pl.pallas_call(
        body, name=name,
        grid_spec=pltpu.PrefetchScalarGridSpec(
            num_scalar_prefetch=1, grid=(B, rh // TR, 4),
            in_specs=[pl.BlockSpec((None, None, TR, C), lambda b, t, p, cj: (b * 4 + p, cj[0], t, 0)),
                      pl.BlockSpec((None, None, TR, C), lambda b, t, p, cj: (b * 4 + p, 0, t, 0))],
            out_specs=[pl.BlockSpec((None, None, TR, C), lambda b, t, p, cj: (p, b, t, 0)),
                       pl.BlockSpec((None, TR, C), lambda b, t, p, cj: (b, t, 0))]),
        out_shape=[jax.ShapeDtypeStruct((4, B, rh, C), BF16), jax.ShapeDtypeStruct((B, rh, C), F32)],
        compiler_params=_params(("parallel", "parallel", "arbitrary")),
    )(cj_arr, g4, r1)


def _chip_add(own, r2, cj_arr, into, first, name, TR=512):
    B, rh, C = own.shape
    TR = rh if rh <= TR else _row_tile(rh, TR)

    def body(cj_ref, p_ref, r_ref, _into_ref, o_ref):
        o_ref[...] = p_ref[...] + r_ref[0].astype(F32) + r_ref[1].astype(F32) + r_ref[2].astype(F32)

    return pl.pallas_call(
        body, name=name,
        grid_spec=pltpu.PrefetchScalarGridSpec(
            num_scalar_prefetch=1, grid=(B, rh // TR),
            in_specs=[pl.BlockSpec((None, TR, C), lambda b, t, cj: (b, t, 0)),
                      pl.BlockSpec((3, None, TR, C), lambda b, t, cj: (0, b, t, 0)),
                      _ANY],
            out_specs=pl.BlockSpec((None, None, TR, C), lambda b, t, cj: (first + b, cj[0], t, 0))),
        out_shape=jax.ShapeDtypeStruct(into.shape, F32),
        input_output_aliases={3: 0},
        compiler_params=_params(("parallel", "parallel")),
    )(cj_arr, own, r2, into)


def _pair_share(bufs, spans, name, deps=()):
    n = len(bufs)

    def body(*refs):
        buf = refs[n + len(deps):2 * n + len(deps)]
        send_sems, recv_sems = refs[2 * n + len(deps):]
        c = lax.axis_index("c")
        sib = (lax.axis_index("x"), lax.axis_index("y"), 1 - c)
        cps = []
        for a, (first, count) in enumerate(spans):
            cp = pltpu.make_async_remote_copy(
                src_ref=buf[a].at[pl.ds(first, count), c], dst_ref=buf[a].at[pl.ds(first, count), c],
                send_sem=send_sems.at[a], recv_sem=recv_sems.at[a], device_id=sib, device_id_type=MESH)
            cp.start()
            cps.append(cp)
        for a, (first, count) in enumerate(spans):
            pltpu.make_async_remote_copy(
                src_ref=buf[a].at[pl.ds(first, count), 1 - c], dst_ref=buf[a].at[pl.ds(first, count), 1 - c],
                send_sem=send_sems.at[a], recv_sem=recv_sems.at[a], device_id=sib, device_id_type=MESH).wait_recv()
        for cp in cps:
            cp.wait_send()

    return pl.pallas_call(
        body, name=name, in_specs=[_ANY] * (n + len(deps)), out_specs=[_ANY] * n,
        out_shape=[jax.ShapeDtypeStruct(b.shape, b.dtype) for b in bufs],
        scratch_shapes=[pltpu.SemaphoreType.DMA((n,)), pltpu.SemaphoreType.DMA((n,))],
        input_output_aliases={a: a for a in range(n)},
    )(*bufs, *deps)


def _pair_swap(arr, name):
    def body(src, dst, send_sem, recv_sem):
        sib = (lax.axis_index("x"), lax.axis_index("y"), 1 - lax.axis_index("c"))
        cp = pltpu.make_async_remote_copy(src_ref=src, dst_ref=dst, send_sem=send_sem, recv_sem=recv_sem,
                                          device_id=sib, device_id_type=MESH)
        cp.start()
        cp.wait_recv()
        cp.wait_send()

    return pl.pallas_call(
        body, name=name, in_specs=[_ANY], out_specs=_ANY,
        out_shape=jax.ShapeDtypeStruct(arr.shape, arr.dtype),
        scratch_shapes=[pltpu.SemaphoreType.DMA, pltpu.SemaphoreType.DMA],
    )(arr)


class _ReduceScatter:
    def __init__(self, n_layers, cj_arr):
        self.L, self.cj = n_layers, cj_arr
        self.total = None
        self.pair = None
        self.chip = None

    def _land(self, after):
        handle, layer, owns = self.chip
        n = len(owns)
        r2 = _wait_copies(handle, _chip_plan, after if isinstance(after, (list, tuple)) else (after,), "rs_chip_wait")[n:]
        if self.total is None:
            self.total = [lax.empty((self.L * o.shape[0], 2) + o.shape[1:], F32) for o in owns]
        self.total = [_chip_add(o, r, self.cj, t, layer * o.shape[0], "rs_chip_add")
                      for o, r, t in zip(owns, r2, self.total)]
        self.chip = None

    def add_layer(self, layer, grads):
        g4 = [g.reshape(g.shape[0] * 4, 2, g.shape[1] // 8, g.shape[2]) for g in grads]
        lands = [lax.empty((g.shape[0], 1) + g.shape[2:], g.dtype) for g in g4]
        handle = _start_copies(g4 + lands, _pair_plan, len(g4), "rs_pair_start")
        self.pair = (handle, layer)
        return (handle[-1],)

    def advance(self, after):
        if self.pair is None:
            return ()
        handle, layer = self.pair
        both = _wait_copies(handle, _pair_plan, (after,), "rs_pair_wait")
        n = len(both) // 2
        added = [_pair_add(g, r, self.cj, "rs_pair_add") for g, r in zip(both[:n], both[n:])]
        parts, owns = [p for p, _ in added], [o for _, o in added]
        lands = [lax.empty((3,) + p.shape[1:], p.dtype) for p in parts]
        handle = _start_copies(parts + lands, _chip_plan, 3 * n, "rs_chip_start")
        if self.chip is not None:
            self._land(handle[-1])
        self.pair, self.chip = None, (handle, layer, owns)
        return (handle[-1],)

    def upper_layers(self, deps):
        per = [t.shape[0] // self.L for t in self.total]
        self.total = _pair_share(self.total, [(b, (self.L - 1) * b) for b in per], "rs_pair_share_upper", deps=deps)
        return [t.reshape(t.shape[0], t.shape[1] * t.shape[2], t.shape[3]) for t in self.total]

    def first_layer(self, after):
        self._land(after)
        per = [t.shape[0] // self.L for t in self.total]
        full = _pair_share(self.total, [(0, b) for b in per], "rs_pair_share_first")
        return [f.reshape(f.shape[0], f.shape[1] * f.shape[2], f.shape[3]) for f in full]


def _relu2_epi(acc):
    return (jnp.square(jnp.maximum(acc, 0.0)),)


def _relu2_bwd_epi(acc, a):
    return (acc * (2.0 * jnp.sqrt(a.astype(F32))),)


_GRAD_ORDER = ("winT", "wbT", "wout", "wq", "wkv", "woT", "wupT", "wdown")
_DW = dict(tm=512, tn=1024, tk=4096, out_dtypes=(BF16,))
_LONG_K = dict(tm=1024, tn=1024, tk=2048)


def _forward_backward(x, mem, target, input_weight_of, weights_of, prefetch_weights, P, grads_done, grads_advance):
    L = P["g_norm"].shape[0]
    S, D = x.shape
    gn = lambda l, i: P["g_norm"][l, i][None]

    saved = []
    (h,) = _resnorm_fwd(x, None, None, gn(0, 0), "norm_in")
    xr = x
    for l in range(L):
        w_in_t = input_weight_of(l, xr)
        proj = _mm(h, w_in_t, "nt", "in_proj", b_pre=(0,), out_dtypes=(BF16,), tn=1792)
        W, w_deps = weights_of(l, proj)
        small = (jnp.broadcast_to(P["sinks"][l][:, None], (8, BLK)), P["ws"][l], P["bs"][l][:, :, None],
                 P["gsgu"][l][None], P["wp"][l], P["ps"][l][None])
        br = _mix_fwd(proj, *small, "mix_fwd")
        pb = _mm(br, W["wbT"], "nt", "branch_proj", batch=3, out_dtypes=(BF16,), deps=w_deps)
        merged = _merge_fwd(proj, pb, "merge_fwd")
        z = _mm(merged, W["wout"], "nn", "out_proj", b_pre=(0,), out_dtypes=(BF16,))
        x1, hm = _resnorm_fwd(xr, z, gn(l, 1), gn(l, 2), "resnorm_fwd")
        qm = _mm(hm, W["wq"], "nn", "mem_q", b_pre=(0,))
        (memn,) = _resnorm_fwd(mem, None, None, P["g_mem"][l][None], "mem_norm")
        kv = _mm(memn, W["wkv"], "nn", "mem_kv", b_pre=(0,))
        om = _memattn_fwd(qm, kv, "memattn_fwd")
        ym = _mm(om, W["woT"], "nt", "mem_o", b_pre=(0,), out_dtypes=(BF16,))
        x2, hf = _resnorm_fwd(x1, ym, gn(l, 3), gn(l, 4), "resnorm_fwd")
        a = _mm(hf, W["wupT"], "nt", "mlp_up", b_pre=(0,), out_dtypes=(BF16,), epi=_relu2_epi)
        yf = _mm(a, W["wdown"], "nn", "mlp_down", b_pre=(0,), out_dtypes=(BF16,), **_LONG_K,
                 deps=prefetch_weights(l + 1, a) if l < L - 1 else ())
        saved.append(dict(W=W, x0=xr, h=h, proj=proj, small=small, br=br, pb=pb, merged=merged, z=z, x1=x1, hm=hm,
                          qm=qm, memn=memn, kv=kv, om=om, ym=ym, x2=x2, hf=hf, a=a, yf=yf))
        if l < L - 1:
            xr, h = _resnorm_fwd(x2, yf, gn(l, 5), gn(l + 1, 0), "resnorm_fwd")
    dres, loss = _final_fwd(saved[-1]["x2"], saved[-1]["yf"], gn(L - 1, 5), target, "loss_head")

    dgn = [[None] * 6 for _ in range(L)]
    dsmall = {k: [None] * L for k in ("g_mem", "sinks", "ws", "bs", "gsgu", "wp", "ps")}
    dh = None
    for l in reversed(range(L)):
        s = saved[l]
        W, G = s["W"], {}
        if l == L - 1:
            dx2, dyf, dgn[l][5] = _resnorm_bwd(s["x2"], s["yf"], gn(l, 5), None, dres, None, "resnorm_bwd_top")
        else:
            dx2, dyf, dgn[l][5], dgn[l + 1][0] = _resnorm_bwd(s["x2"], s["yf"], gn(l, 5), gn(l + 1, 0), dres, dh,
                                                              "resnorm_bwd", deps=deps)
        du = _mm(dyf, W["wdown"], "nt", "mlp_down_dx", b_pre=(0,), out_dtypes=(BF16,), extras=(s["a"],), epi=_relu2_bwd_epi)
        G["wdown"] = _mm(s["a"], dyf, "tn", "mlp_down_dw", **_DW)[None]
        dhf = _mm(du, W["wupT"], "nn", "mlp_up_dx", b_pre=(0,), out_dtypes=(BF16,), **_LONG_K)
        G["wupT"] = _mm(du, s["hf"], "tn", "mlp_up_dw", **_DW)[None]
        dx1, dym, dgn[l][3], dgn[l][4] = _resnorm_bwd(s["x1"], s["ym"], gn(l, 3), gn(l, 4), dx2, dhf, "resnorm_bwd")
        dom = _mm(dym, W["woT"], "nn", "mem_o_dx", b_pre=(0,), out_dtypes=(BF16,), deps=grads_advance(dx1))
        G["woT"] = _mm(dym, s["om"], "tn", "mem_o_dw", **_DW)[None]
        dqm, dkv = _memattn_bwd(s["qm"], s["kv"], dom, "memattn_bwd")
        dmemn = _mm(dkv, W["wkv"], "nt", "mem_kv_dx", b_pre=(0,))
        G["wkv"] = _mm(s["memn"], dkv, "tn", "mem_kv_dw", out_dtypes=(BF16,))[None]
        _, dsmall["g_mem"][l] = _resnorm_bwd(mem, None, None, P["g_mem"][l][None], None, dmemn, "mem_norm_bwd")
        dhm = _mm(dqm, W["wq"], "nt", "mem_q_dx", b_pre=(0,), out_dtypes=(BF16,))
        G["wq"] = _mm(s["hm"], dqm, "tn", "mem_q_dw", **_DW)[None]
        dx0, dz, dgn[l][1], dgn[l][2] = _resnorm_bwd(s["x0"], s["z"], gn(l, 1), gn(l, 2), dx1, dhm, "resnorm_bwd")
        dmerged = _mm(dz, W["wout"], "nt", "out_proj_dx", b_pre=(0,), out_dtypes=(BF16,))
        G["wout"] = _mm(s["merged"], dz, "tn", "out_proj_dw", **_DW)[None]
        dproj, dpb = _merge_bwd(s["proj"], s["pb"], dmerged, "merge_bwd")
        dbr = _mm(dpb, W["wbT"], "nn", "branch_proj_dx", batch=3, out_dtypes=(BF16,))
        G["wbT"] = _mm(dpb, s["br"], "tn", "branch_proj_dw", batch=3, **_DW)
        (dproj, dsmall["sinks"][l], dsmall["ws"][l], dsmall["bs"][l], dsmall["gsgu"][l], dsmall["wp"][l],
         dsmall["ps"][l]) = _mix_bwd(s["proj"], dbr, dproj, *s["small"], "mix_bwd")
        dh = _mm(dproj, W["winT"], "nn", "in_proj_dx", b_pre=(0,), out_dtypes=(BF16,), **_LONG_K)
        G["winT"] = _mm(dproj, s["h"], "tn", "in_proj_dw", **_DW)[None]
        deps = grads_done(l, [G[k] for k in _GRAD_ORDER])
        dres = dx0
    grad_x, dgn[0][0] = _resnorm_bwd(x, None, None, gn(0, 0), dres, dh, "norm_in_bwd", deps=deps)
    tail_deps = grads_advance(grad_x)

    small_grads = dict(
        g_norm=jnp.stack([jnp.concatenate(row, axis=0) for row in dgn]),
        g_mem=jnp.concatenate(dsmall["g_mem"], axis=0),
        sinks=jnp.stack([d[:, 0] for d in dsmall["sinks"]]),
        ws=jnp.stack(dsmall["ws"]),
        bs=jnp.stack([d[:, :, 0] for d in dsmall["bs"]]),
        gsgu=jnp.concatenate(dsmall["gsgu"], axis=0),
        wp=jnp.stack(dsmall["wp"]),
        ps=jnp.concatenate(dsmall["ps"], axis=0),
    )
    return loss, grad_x, small_grads, tail_deps


_PACK_ROWS = 512


def _as_rows(a):
    n = math.prod(a.shape)
    if n % 128:
        a = jnp.pad(a.reshape(-1), (0, (-n) % 128))
    r = a.reshape(-1, 128)
    return jnp.pad(r, ((0, (-r.shape[0]) % 8), (0, 0))) if r.shape[0] % 8 else r


def _pack(arrays):
    rows = [_as_rows(a) for a in arrays]
    total = sum(r.shape[0] for r in rows)
    tail = (-total) % _PACK_ROWS
    if tail:
        rows.append(jnp.zeros((tail, 128), rows[0].dtype))
    return jnp.concatenate(rows, axis=0)


def _unpack(packed, like):
    out, pos = [], 0
    for a in like:
        n = math.prod(a.shape)
        nr = -(-n // 128)
        rows = packed[pos:pos + nr]
        out.append((rows.reshape(-1)[:n] if n % 128 else rows).reshape(a.shape))
        pos += nr + (-nr) % 8
    return out


_BIG = ("w_in", "w_branch", "w_out", "w_q_mem", "w_kv_mem", "w_o_mem", "w_up", "w_down")
_SMALL = ("g_norm", "g_mem", "attn_sinks", "w_spatial", "b_spatial", "g_sgu", "w_pool", "pool_scale")
_WEIGHTS = ("g_norm", "g_mem", "w_in", "attn_sinks", "w_spatial", "b_spatial", "g_sgu", "w_pool", "pool_scale",
            "w_branch", "w_out", "w_q_mem", "w_kv_mem", "w_o_mem", "w_up", "w_down")


def _to_working(name, w):
    if name == "w_in":
        return jnp.swapaxes(w, 1, 2)
    if name == "w_branch":
        t = jnp.swapaxes(w, 2, 3)
        return t.reshape(t.shape[0] * 3, t.shape[2], t.shape[3])
    if name in ("w_o_mem", "w_up"):
        return jnp.swapaxes(w, 1, 2)
    return w


def _from_working(name, g):
    if name == "w_in":
        return jnp.swapaxes(g, 1, 2)
    if name == "w_branch":
        return jnp.swapaxes(g.reshape(g.shape[0] // 3, 3, g.shape[1], g.shape[2]), 2, 3)
    if name in ("w_o_mem", "w_up"):
        return jnp.swapaxes(g, 1, 2)
    return g


_WKEY = dict(w_in="winT", w_branch="wbT", w_out="wout", w_q_mem="wq", w_kv_mem="wkv", w_o_mem="woT",
             w_up="wupT", w_down="wdown")


def kernel(x, mem, g_norm, g_mem, w_in, attn_sinks, w_spatial, b_spatial, g_sgu, w_pool, pool_scale, w_branch, w_out, w_q_mem, w_kv_mem, w_o_mem, w_up, w_down, loss_target, m_g_norm, m_g_mem, m_w_in, m_attn_sinks, m_w_spatial, m_b_spatial, m_g_sgu, m_w_pool, m_pool_scale, m_w_branch, m_w_out, m_w_q_mem, m_w_kv_mem, m_w_o_mem, m_w_up, m_w_down, v_g_norm, v_g_mem, v_w_in, v_attn_sinks, v_w_spatial, v_b_spatial, v_g_sgu, v_w_pool, v_pool_scale, v_w_branch, v_w_out, v_w_q_mem, v_w_kv_mem, v_w_o_mem, v_w_up, v_w_down):
    w = dict(g_norm=g_norm, g_mem=g_mem, w_in=w_in, attn_sinks=attn_sinks, w_spatial=w_spatial, b_spatial=b_spatial,
             g_sgu=g_sgu, w_pool=w_pool, pool_scale=pool_scale, w_branch=w_branch, w_out=w_out, w_q_mem=w_q_mem,
             w_kv_mem=w_kv_mem, w_o_mem=w_o_mem, w_up=w_up, w_down=w_down)
    m = dict(g_norm=m_g_norm, g_mem=m_g_mem, w_in=m_w_in, attn_sinks=m_attn_sinks, w_spatial=m_w_spatial,
             b_spatial=m_b_spatial, g_sgu=m_g_sgu, w_pool=m_w_pool, pool_scale=m_pool_scale, w_branch=m_w_branch,
             w_out=m_w_out, w_q_mem=m_w_q_mem, w_kv_mem=m_w_kv_mem, w_o_mem=m_w_o_mem, w_up=m_w_up, w_down=m_w_down)
    v = dict(g_norm=v_g_norm, g_mem=v_g_mem, w_in=v_w_in, attn_sinks=v_attn_sinks, w_spatial=v_w_spatial,
             b_spatial=v_b_spatial, g_sgu=v_g_sgu, w_pool=v_w_pool, pool_scale=v_pool_scale, w_branch=v_w_branch,
             w_out=v_w_out, w_q_mem=v_w_q_mem, w_kv_mem=v_w_kv_mem, w_o_mem=v_w_o_mem, w_up=v_w_up, w_down=v_w_down)
    L = g_norm.shape[0]
    j = 2 * lax.axis_index("x") + lax.axis_index("y")
    c = lax.axis_index("c")
    j_arr = jnp.reshape(j, (1,)).astype(jnp.int32)
    cj_arr = jnp.stack([c, j]).astype(jnp.int32)

    gs = g_norm.shape[2]
    working = [_to_working(n, w[n]) for n in _BIG]
    per_layer = [wk.shape[0] // L for wk in working]

    def own_slabs(l):
        return [_own_slab(wk, BF16, j_arr, "own_slab", first=l * b, count=b) for wk, b in zip(working, per_layer)]

    first_slabs = own_slabs(0)
    lead = [first_slabs[0], _own_slab(g_norm.reshape(1, L * 6 * gs // 128, 128), F32, j_arr, "own_slab_norm")]
    lead_handle = _start_copies(lead, _gather_plan, 3 * len(lead), "gather_start_lead")
    rest_handle = _start_copies(first_slabs[1:], _gather_plan, 3 * (len(first_slabs) - 1), "gather_start_first",
                                deps=(lead_handle[-1],))
    slabs = {l: own_slabs(l) for l in range(1, L)}
    lead = _gather_forward(_wait_copies(lead_handle, _gather_plan,
                                        [rest_handle[-1]] + [s for l in slabs for s in slabs[l]],
                                        "gather_wait_lead"), "gather_forward_lead")
    gn_full = jnp.transpose(lead[1].reshape(4, L * 6, gs), (1, 0, 2)).reshape(L, 6, 4 * gs)
    P = dict(g_norm=gn_full, g_mem=g_mem, sinks=attn_sinks, ws=w_spatial, bs=b_spatial, gsgu=g_sgu, wp=w_pool,
             ps=pool_scale)
    whole = lambda g: g.reshape(g.shape[0], 8 * g.shape[3], g.shape[4])
    gathered, in_flight = {}, {}

    forwarding = {}

    def prefetch_weights(l, after):
        arrived = _wait_copies(in_flight[l], _gather_plan, (after,), "gather_wait")
        forwarding[l] = _start_copies(arrived, _forward_plan, 3 * len(arrived), "gather_forward_start")
        return (forwarding[l][-1],)

    def layer_weights(l, after):
        if l not in gathered:
            gathered[l] = _wait_copies(forwarding[l], _forward_plan, (after,), "gather_forward_wait")
        return gathered[l]

    def input_weight_of(l, after):
        return whole(lead[0] if l == 0 else layer_weights(l, after)[0])

    def weights_of(l, after):
        deps = ()
        if l == 0:
            rest = _gather_forward(_wait_copies(rest_handle, _gather_plan, (after,), "gather_wait_first"),
                                   "gather_forward_first")
            gathered[0] = [lead[0], *rest]
            dep = rest[0]
            for k in range(1, L):
                in_flight[k] = _start_copies(slabs[k], _gather_plan, 3 * len(slabs[k]), "gather_start", deps=(dep,))
                dep = in_flight[k][-1]
            deps = tuple(h[-1] for h in in_flight.values())
        return {k: whole(g) for k, g in zip(_GRAD_ORDER, layer_weights(l, after))}, deps

    rs = _ReduceScatter(L, cj_arr)
    loss_part, grad_x, sg, tail_deps = _forward_backward(
        x[0], mem[0], loss_target[0], input_weight_of, weights_of, prefetch_weights, P, rs.add_layer, rs.advance)
    loss = lax.psum(loss_part[0, 0], ("x", "y", "c"))

    transposed = ("w_branch", "w_o_mem", "w_up")
    view = {n: (lambda t: jnp.swapaxes(t, 1, 2)) if n == "w_in" else (lambda t: t) for n in _BIG}
    rows = lambda n, t: view[n](t).reshape(-1, view[n](t).shape[-1])
    state = {n: (rows(n, w[n]), rows(n, m[n]), rows(n, v[n])) for n in _BIG}
    updated, upper_grad = {}, {}
    if L > 1:
        for n, gw in zip(_BIG, rs.upper_layers(tail_deps)):
            per = gw.shape[0] // L
            if n in transposed:
                upper_grad[n] = _from_working(n, gw[per:])
                g2d, g_row0 = upper_grad[n].reshape(-1, upper_grad[n].shape[-1]), 0
            else:
                g2d, g_row0 = gw.reshape(-1, gw.shape[-1]), None
            r_layer = state[n][0].shape[0] // L
            updated[n] = _adamw(state[n][0], g2d, state[n][1], state[n][2], "adamw_upper",
                                rows=(r_layer, L * r_layer), g_row0=g_row0)
    full_small = [sg["g_norm"], sg["g_mem"], sg["sinks"], sg["ws"], sg["bs"], sg["gsgu"], sg["wp"], sg["ps"]]
    packed = _pack(full_small)
    pair_sum = _own_slab(packed[None], F32, j_arr, "small_grads_pair_sum",
                         deps=tuple(tail_deps) + tuple(u[0] for u in updated.values()),
                         plus=_pair_swap(packed, "small_grads_swap")[None])
    (chip_sums,) = _gather_weights([pair_sum], "gather_small_grads")
    total = _sum_slots(chip_sums.reshape(4, *packed.shape), "sum_small_grads")
    grads = {}
    for n, g in zip(_SMALL, _unpack(total, full_small)):
        grads[n] = lax.dynamic_slice_in_dim(g, j * g_norm.shape[2], g_norm.shape[2], axis=2) if n == "g_norm" else g

    after = [total] + [u[0] for u in updated.values()]
    delta, new_m, new_v = {}, {}, {}
    for n, gw in zip(_BIG, rs.first_layer(after)):
        per = gw.shape[0] // L
        if n in transposed:
            g0 = _from_working(n, gw[:per])
            grads[n] = jnp.concatenate([g0, upper_grad[n]], axis=0) if L > 1 else g0
            g2d = g0.reshape(-1, g0.shape[-1])
        else:
            grads[n] = _from_working(n, gw)
            g2d = gw.reshape(-1, gw.shape[-1])
        r_layer = state[n][0].shape[0] // L
        d_, m_, v_ = _adamw(state[n][0], g2d, state[n][1], state[n][2], "adamw_first", rows=(0, r_layer), g_row0=0,
                            into=updated.get(n))
        shp = view[n](w[n]).shape
        delta[n], new_m[n], new_v[n] = view[n](d_.reshape(shp)), view[n](m_.reshape(shp)), view[n](v_.reshape(shp))
    small_w = [w[n] for n in _SMALL]
    d_, m_, v_ = _adamw(_pack(small_w), _pack([grads[n] for n in _SMALL]), _pack([m[n] for n in _SMALL]),
                        _pack([v[n] for n in _SMALL]), "adamw_small")
    for n, dd, mm_, vv in zip(_SMALL, _unpack(d_, small_w), _unpack(m_, small_w), _unpack(v_, small_w)):
        delta[n], new_m[n], new_v[n] = dd, mm_, vv

    return (loss, grad_x[None], *[grads[n] for n in _WEIGHTS], *[delta[n] for n in _WEIGHTS],
            *[new_m[n] for n in _WEIGHTS], *[new_v[n] for n in _WEIGHTS])
```

```python
import math

import jax
import jax.numpy as jnp
from jax import lax
from jax.experimental import pallas as pl
from jax.experimental.pallas import tpu as pltpu

F32 = jnp.float32
BF16 = jnp.bfloat16
MESH = pl.DeviceIdType.MESH

EPS = 1e-6
NEG_INF = -1e30
BLK = 128
HALO = 16
POOL_WINDOWS = (2, 4, 8, 16)
ATT_SCALE = 1.0 / math.sqrt(64.0)
MEM_SCALE = 1.0 / math.sqrt(128.0)
C_Q, C_K, C_V, C_SU, C_SV, C_PC, C_GATE, C_END = 0, 512, 640, 768, 1280, 1792, 2304, 5376

ADAM_LR, ADAM_B1, ADAM_B2, ADAM_EPS, ADAM_WD, ADAM_STEP = 0.001, 0.9, 0.999, 1e-08, 0.01, 10

VMEM_LIMIT_BYTES = 56 * 1024 * 1024

_DIMS = {
    "nn": (((1,), (0,)), ((), ())),
    "nt": (((1,), (1,)), ((), ())),
    "tn": (((0,), (0,)), ((), ())),
}


def _dot(a, b, mode):
    return lax.dot_general(a, b, _DIMS[mode], preferred_element_type=F32)


def _params(semantics):
    return pltpu.CompilerParams(dimension_semantics=semantics, vmem_limit_bytes=VMEM_LIMIT_BYTES)


def _tile(dim, pref):
    if dim <= pref:
        return dim
    t = (pref // 128) * 128
    while t >= 128:
        if dim % t == 0:
            return t
        t -= 128
    raise ValueError(f"no tile for {dim}")


def _rms(x, g):
    return x * lax.rsqrt(jnp.mean(x * x, axis=-1, keepdims=True) + EPS) * g


def _mm(a, b, mode, name, *, out_dtypes=(F32,), a_pre=(), b_pre=(), into=None, out_pre=(),
        extras=(), epi=None, deps=(), batch=None, tm=2048, tn=1024, tk=1024):
    lead = 0 if batch is None else 1
    assert not (lead and (a_pre or b_pre or into is not None or extras))
    a2, b2 = a.shape[len(a_pre) + lead:], b.shape[len(b_pre) + lead:]
    if mode == "nn":
        (M, K), (K2, N) = a2, b2
    elif mode == "nt":
        (M, K), (N, K2) = a2, b2
    else:
        (K, M), (K2, N) = a2, b2
    assert K == K2, (a.shape, b.shape, mode)
    tm, tn, tk = _tile(M, tm), _tile(N, tn), _tile(K, tk)
    nk = K // tk
    na, nb_, no = len(a_pre), len(b_pre), len(out_pre)

    def on_grid(f):
        return f if batch is None else (lambda g, i, j, k: (g,) + f(i, j, k))

    if mode == "tn":
        a_spec = pl.BlockSpec((None,) * (na + lead) + (tk, tm), on_grid(lambda i, j, k: a_pre + (k, i)))
    else:
        a_spec = pl.BlockSpec((None,) * (na + lead) + (tm, tk), on_grid(lambda i, j, k: a_pre + (i, k)))
    if mode == "nt":
        b_spec = pl.BlockSpec((None,) * (nb_ + lead) + (tn, tk), on_grid(lambda i, j, k: b_pre + (j, k)))
    else:
        b_spec = pl.BlockSpec((None,) * (nb_ + lead) + (tk, tn), on_grid(lambda i, j, k: b_pre + (k, j)))
    tile_spec = pl.BlockSpec((None,) * lead + (tm, tn), on_grid(lambda i, j, k: (i, j)))
    ne, nout = len(extras), len(out_dtypes)
    in_specs = [a_spec, b_spec] + [tile_spec] * ne
    operands = [a, b, *extras]
    aliases = {}
    if into is not None:
        assert nout == 1
        in_specs.append(pl.BlockSpec(memory_space=pl.ANY))
        operands.append(into)
        aliases = {len(operands) - 1: 0}
        out_shape = [jax.ShapeDtypeStruct(into.shape, into.dtype)]
        out_specs = [pl.BlockSpec((None,) * no + (tm, tn), lambda i, j, k: out_pre + (i, j))]
    else:
        out_shape = [jax.ShapeDtypeStruct(((batch,) if lead else ()) + (M, N), dt) for dt in out_dtypes]
        out_specs = [tile_spec] * nout
    in_specs += [pl.BlockSpec(memory_space=pl.ANY)] * len(deps)
    operands += list(deps)

    def body(*refs):
        a_ref, b_ref = refs[0], refs[1]
        ex = refs[2:2 + ne]
        pos = 2 + ne + (1 if into is not None else 0) + len(deps)
        outs = refs[pos:pos + nout]
        acc_ref = refs[pos + nout] if nk > 1 else None

        def finish(acc):
            vals = epi(acc, *[e[...] for e in ex]) if epi is not None else (acc,)
            for o, v in zip(outs, vals):
                o[...] = v.astype(o.dtype)

        def prod():
            return _dot(a_ref[...].astype(BF16), b_ref[...].astype(BF16), mode)

        if nk == 1:
            finish(prod())
        else:
            k = pl.program_id(2 + lead)

            @pl.when(k == 0)
            def _():
                acc_ref[...] = jnp.zeros_like(acc_ref)

            acc_ref[...] += prod()

            @pl.when(k == nk - 1)
            def _():
                finish(acc_ref[...])

    res = pl.pallas_call(
        body, name=name, grid=((batch,) if lead else ()) + (M // tm, N // tn, nk),
        in_specs=in_specs, out_specs=out_specs, out_shape=out_shape,
        scratch_shapes=[pltpu.VMEM((tm, tn), F32)] if nk > 1 else [],
        input_output_aliases=aliases,
        compiler_params=_params(("parallel",) * (2 + lead) + ("arbitrary",)),
    )(*operands)
    return res[0] if nout == 1 else tuple(res)


def _resnorm_fn(has_post, has_pre):
    def f(*a):
        x, k = a[0], 1
        if has_post:
            x, k = x + _rms(a[1], a[2]), 3
        outs = [x]
        if has_pre:
            outs.append(_rms(x, a[k]))
        return tuple(outs)
    return f


def _row_spec(T, W):
    return pl.BlockSpec((T, W), lambda i: (i, 0))


def _par_spec(W):
    return pl.BlockSpec((1, W), lambda i: (0, 0))


def _resnorm_fwd(xr, y, gp, gq, name, T=512, deps=()):
    S, D = xr.shape
    T = min(T, S)
    has_post, has_pre = y is not None, gq is not None
    f = _resnorm_fn(has_post, has_pre)
    ins = [xr] + ([y, gp] if has_post else []) + ([gq] if has_pre else [])
    in_specs = [_row_spec(T, D)] + ([_row_spec(T, D), _par_spec(D)] if has_post else []) + ([_par_spec(D)] if has_pre else [])
    out_shape, out_specs = [], []
    if has_post:
        out_shape.append(jax.ShapeDtypeStruct((S, D), F32)); out_specs.append(_row_spec(T, D))
    if has_pre:
        out_shape.append(jax.ShapeDtypeStruct((S, D), BF16)); out_specs.append(_row_spec(T, D))
    n_in, n_dep = len(ins), len(deps)

    def body(*refs):
        vals = f(*[r[...].astype(F32) for r in refs[:n_in]])
        outs = list(refs[n_in + n_dep:])
        if has_post:
            outs.pop(0)[...] = vals[0]
        if has_pre:
            outs.pop(0)[...] = vals[1].astype(BF16)

    res = pl.pallas_call(body, name=name, grid=(S // T,),
                         in_specs=in_specs + [pl.BlockSpec(memory_space=pl.ANY)] * n_dep, out_specs=out_specs,
                         out_shape=out_shape, compiler_params=_params(("parallel",)))(*ins, *deps)
    return tuple(res)


def _resnorm_bwd(xr, y, gp, gq, dres, dh, name, T=512, deps=()):
    S, D = xr.shape
    T = min(T, S)
    has_post, has_pre, has_res = y is not None, gq is not None, dres is not None
    f = _resnorm_fn(has_post, has_pre)
    ins = [xr] + ([y, gp] if has_post else []) + ([gq] if has_pre else [])
    in_specs = [_row_spec(T, D)] + ([_row_spec(T, D), _par_spec(D)] if has_post else []) + ([_par_spec(D)] if has_pre else [])
    n_prim = len(ins)
    if has_res:
        ins.append(dres); in_specs.append(_row_spec(T, D))
    if has_pre:
        ins.append(dh); in_specs.append(_row_spec(T, D))
    n_in, n_dep = len(ins), len(deps)
    out_shape = [jax.ShapeDtypeStruct((S, D), F32)]
    out_specs = [_row_spec(T, D)]
    if has_post:
        out_shape += [jax.ShapeDtypeStruct((S, D), BF16), jax.ShapeDtypeStruct((1, D), F32)]
        out_specs += [_row_spec(T, D), _par_spec(D)]
    if has_pre:
        out_shape.append(jax.ShapeDtypeStruct((1, D), F32)); out_specs.append(_par_spec(D))

    def body(*refs):
        i = pl.program_id(0)
        prim = [r[...].astype(F32) for r in refs[:n_prim]]
        rest = list(refs[n_prim:n_in])
        ct_x = rest.pop(0)[...] if has_res else jnp.zeros((T, D), F32)
        cts = [ct_x]
        if has_pre:
            cts.append(rest.pop(0)[...].astype(F32))
        _, vjp = jax.vjp(f, *prim)
        grads = list(vjp(tuple(cts)))
        outs = list(refs[n_in + n_dep:])
        outs.pop(0)[...] = grads.pop(0)
        acc = []
        if has_post:
            outs.pop(0)[...] = grads.pop(0).astype(BF16)
            acc.append((outs.pop(0), grads.pop(0)))
        if has_pre:
            acc.append((outs.pop(0), grads.pop(0)))

        @pl.when(i == 0)
        def _():
            for o, _g in acc:
                o[...] = jnp.zeros_like(o)

        for o, g in acc:
            o[...] += g

    res = pl.pallas_call(body, name=name, grid=(S // T,),
                         in_specs=in_specs + [pl.BlockSpec(memory_space=pl.ANY)] * n_dep, out_specs=out_specs,
                         out_shape=out_shape, compiler_params=_params(("arbitrary",)))(*ins, *deps)
    return tuple(res)


def _final_fwd(xr, y, gp, target, name, T=512):
    S, D = xr.shape
    T = min(T, S)

    def body(x_ref, y_ref, g_ref, t_ref, dy_ref, loss_ref):
        i = pl.program_id(0)
        e = x_ref[...] + _rms(y_ref[...].astype(F32), g_ref[...]) - t_ref[...]
        dy_ref[...] = e / D

        @pl.when(i == 0)
        def _():
            loss_ref[...] = jnp.zeros_like(loss_ref)

        loss_ref[...] += 0.5 * jnp.sum(jnp.sum(e * e, axis=-1, keepdims=True) / D, axis=0, keepdims=True)

    return pl.pallas_call(
        body, name=name, grid=(S // T,),
        in_specs=[_row_spec(T, D), _row_spec(T, D), _par_spec(D), _row_spec(T, D)],
        out_specs=[_row_spec(T, D), pl.BlockSpec((1, 128), lambda i: (0, 0))],
        out_shape=[jax.ShapeDtypeStruct((S, D), F32), jax.ShapeDtypeStruct((1, 128), F32)],
        compiler_params=_params(("arbitrary",)))(xr, y, gp, target)


_STRAIGHT_HEADS = (0, 2, 5, 7)
_ROLLED_HEADS = (1, 3, 4, 6)


def _straight_lanes():
    r = lax.broadcasted_iota(jnp.int32, (4 * BLK, BLK), 0)
    c = lax.broadcasted_iota(jnp.int32, (4 * BLK, BLK), 1)
    return (r < 2 * BLK) == (c < 64)


def _att_mask(not_first):
    k = lax.broadcasted_iota(jnp.int32, (2 * BLK, 4 * BLK), 0)
    q = lax.broadcasted_iota(jnp.int32, (2 * BLK, 4 * BLK), 1) % BLK
    qc, kc = 2 + q // 64, k // 64
    return (kc <= qc) & (kc >= qc - 2) & (not_first | (k >= BLK))


def _sink_row(sk_ref, heads):
    return jnp.concatenate([sk_ref[h:h + 1, :] for h in heads], axis=1)


def _softmax_sink(s, sk):
    m = jnp.maximum(jnp.max(s, axis=0, keepdims=True), sk)
    e = jnp.exp(s - m)
    es = jnp.exp(sk - m)
    inv = 1.0 / (jnp.sum(e, axis=0, keepdims=True) + es)
    return e * inv, es * inv


def _softmax_rows(s):
    e = jnp.exp(s - jnp.max(s, axis=0, keepdims=True))
    return e * (1.0 / jnp.sum(e, axis=0, keepdims=True))


_GELU_C = math.sqrt(2.0 / math.pi)
_GELU_A = 0.044715


def _gelu_with_slope(x):
    x2 = x * x
    t = jnp.tanh(_GELU_C * (x + _GELU_A * (x2 * x)))
    half = 0.5 * (1.0 + t)
    return x * half, half + (0.5 * _GELU_C) * x * (1.0 - t * t) * (1.0 + (3.0 * _GELU_A) * x2)


def _att_bands(cur, kvp):
    kband = jnp.concatenate([kvp[:, 0:BLK], cur[:, C_K:C_K + BLK]], axis=0)
    vband = jnp.concatenate([kvp[:, BLK:2 * BLK], cur[:, C_V:C_V + BLK]], axis=0)
    return kband, pltpu.roll(kband, 64, 1), vband, pltpu.roll(vband, 64, 1)


def _stack_tiles(ref_or_val, start):
    return jnp.concatenate([ref_or_val[:, start + BLK * t:start + BLK * (t + 1)] for t in range(4)], axis=0)


def _sgu_mask():
    r = lax.broadcasted_iota(jnp.int32, (BLK, BLK), 0)
    c = lax.broadcasted_iota(jnp.int32, (BLK, BLK), 1)
    return (c // 64) <= (r // 64)


def _trailing_sums(ext, g):
    s, shift = ext, 1
    for _ in range(g + 1):
        s = s + pltpu.roll(s, shift, 0)
        shift *= 2
    return s


def _leading_sums(z, g):
    d, shift = z, 1
    for _ in range(g + 1):
        d = d + pltpu.roll(d, z.shape[0] - shift, 0)
        shift *= 2
    return d


def _pool_cnt(blk, w):
    t = blk * BLK + lax.broadcasted_iota(jnp.int32, (BLK, 1), 0)
    return jnp.minimum(t + 1, w).astype(F32)


def _mix_in_specs(nb, rev):
    def b(i):
        return nb - 1 - i if rev else i
    return [
        pl.BlockSpec((BLK, C_GATE), lambda i: (b(i), 0)),
        pl.BlockSpec((BLK, 2 * BLK), lambda i: (jnp.maximum(b(i) - 1, 0), C_K // (2 * BLK))),
        pl.BlockSpec((HALO, C_GATE), lambda i: (jnp.maximum(b(i) * (BLK // HALO) - 1, 0), 0)),
        pl.BlockSpec((8, BLK), lambda i: (0, 0)),
        pl.BlockSpec((4, BLK, BLK), lambda i: (0, 0, 0)),
        pl.BlockSpec((4, BLK, 1), lambda i: (0, 0, 0)),
        pl.BlockSpec((1, 512), lambda i: (0, 0)),
        pl.BlockSpec((4, BLK, BLK), lambda i: (0, 0, 0)),
        pl.BlockSpec((1, 512), lambda i: (0, 0)),
    ]


def _mix_fwd(proj, sinks_b, ws, bs3, gsgu, wp, ps, name):
    S = proj.shape[0]
    nb = S // BLK

    def body(cur_ref, kvp_ref, pcp_ref, sk_ref, ws_ref, bs_ref, gs_ref, wp_ref, ps_ref, br_ref, ext_ref):
        i = pl.program_id(0)
        not_first = i > 0
        cur, kvp = cur_ref[...].astype(F32), kvp_ref[...].astype(F32)
        mask = _att_mask(not_first)
        own = _straight_lanes()
        q = _stack_tiles(cur, C_Q)
        outs = []
        kband, kroll, vband, vroll = _att_bands(cur, kvp)
        for qg, kg, vg, heads in ((jnp.where(own, q, 0.0), kband, vband, _STRAIGHT_HEADS),
                                  (jnp.where(own, 0.0, q), kroll, vroll, _ROLLED_HEADS)):
            s = jnp.where(mask, _dot(kg.astype(BF16), qg.astype(BF16), "nt") * ATT_SCALE, NEG_INF)
            p, _ = _softmax_sink(s, _sink_row(sk_ref, heads))
            outs.append(_dot(p.astype(BF16), vg.astype(BF16), "tn"))
        o = jnp.where(own, outs[0], outs[1])
        for t in range(4):
            br_ref[0, :, BLK * t:BLK * (t + 1)] = o[BLK * t:BLK * (t + 1)].astype(BF16)
        gu = jax.nn.gelu(cur[:, C_SU:C_SV])
        vn = _rms(jax.nn.gelu(cur[:, C_SV:C_PC]), gs_ref[...]).astype(BF16)
        wmask = _sgu_mask()
        for g in range(4):
            wm = jnp.where(wmask, ws_ref[g], 0.0).astype(BF16)
            sp = _dot(wm, vn[:, BLK * g:BLK * (g + 1)], "nn") + bs_ref[g]
            br_ref[1, :, BLK * g:BLK * (g + 1)] = (gu[:, BLK * g:BLK * (g + 1)] * sp).astype(BF16)
        c = cur[:, C_PC:C_GATE]
        ext_ref[0:HALO, :] = jnp.where(not_first, pcp_ref[:, C_PC:C_GATE].astype(F32), 0.0)
        ext_ref[HALO:HALO + BLK, :] = c
        for g, w in enumerate(POOL_WINDOWS):
            sl = slice(BLK * g, BLK * (g + 1))
            acc = _trailing_sums(ext_ref[:, sl], g)[HALO:]
            pooled = acc * (1.0 / _pool_cnt(i, w)) - c[:, sl]
            mixed = _dot(pooled.astype(BF16), wp_ref[g].astype(BF16), "nn")
            br_ref[2, :, sl] = (mixed * ps_ref[:, sl]).astype(BF16)

    return pl.pallas_call(
        body, name=name, grid=(nb,),
        in_specs=_mix_in_specs(nb, False),
        out_specs=pl.BlockSpec((3, BLK, 512), lambda i: (0, i, 0)),
        out_shape=jax.ShapeDtypeStruct((3, S, 512), BF16),
        scratch_shapes=[pltpu.VMEM((HALO + BLK, 512), F32)],
        compiler_params=_params(("parallel",)),
    )(proj, proj, proj, sinks_b, ws, bs3, gsgu, wp, ps)


def _mix_bwd(proj, dbr, dproj, sinks_b, ws, bs3, gsgu, wp, ps, name):
    S = proj.shape[0]
    nb = S // BLK

    def body(cur_ref, kvp_ref, pcp_ref, sk_ref, ws_ref, bs_ref, gs_ref, wp_ref, ps_ref, dbr_ref, _dproj_in,
             dp_ref, dsk_ref, dws_ref, dbs_ref, dgs_ref, dwp_ref, dps_ref,
             ext_ref, z_ref, ckv_ref, cpc_ref):
        i = pl.program_id(0)
        blk = nb - 1 - i
        not_first = blk > 0

        @pl.when(i == 0)
        def _():
            for r in (dsk_ref, dws_ref, dbs_ref, dgs_ref, dwp_ref, dps_ref, ckv_ref, cpc_ref, z_ref):
                r[...] = jnp.zeros_like(r)

        cur, kvp = cur_ref[...].astype(F32), kvp_ref[...].astype(F32)
        mask = _att_mask(not_first)
        own = _straight_lanes()
        q = _stack_tiles(cur, C_Q)
        do = jnp.concatenate([dbr_ref[0, :, BLK * t:BLK * (t + 1)] for t in range(4)], axis=0)
        kband, kroll, vband, vroll = _att_bands(cur, kvp)
        dqs, dks, dvs = [], [], []
        for qg, dog, kg, vg, heads in (
                (jnp.where(own, q, 0.0), jnp.where(own, do, 0.0), kband, vband, _STRAIGHT_HEADS),
                (jnp.where(own, 0.0, q), jnp.where(own, 0.0, do), kroll, vroll, _ROLLED_HEADS)):
            qg, dog, kg, vg = qg.astype(BF16), dog.astype(BF16), kg.astype(BF16), vg.astype(BF16)
            s = jnp.where(mask, _dot(kg, qg, "nt") * ATT_SCALE, NEG_INF)
            p, p_sink = _softmax_sink(s, _sink_row(sk_ref, heads))
            dp = _dot(vg, dog, "nt")
            rs = jnp.sum(p * dp, axis=0, keepdims=True)
            ds = (p * (dp - rs) * ATT_SCALE).astype(BF16)
            sink_row = p_sink * rs
            for t, h in enumerate(heads):
                dsk_ref[h:h + 1, :] += jnp.broadcast_to(
                    -jnp.sum(sink_row[:, BLK * t:BLK * (t + 1)], axis=1, keepdims=True), (1, BLK))
            dvs.append(_dot(p.astype(BF16), dog, "nn"))
            dks.append(_dot(ds, qg, "nn"))
            dqs.append(_dot(ds, kg, "tn"))
        dq = jnp.where(own, dqs[0], dqs[1])
        for t in range(4):
            dp_ref[:, C_Q + BLK * t:C_Q + BLK * (t + 1)] = dq[BLK * t:BLK * (t + 1)].astype(BF16)
        dk = dks[0] + pltpu.roll(dks[1], 64, 1)
        dv = dvs[0] + pltpu.roll(dvs[1], 64, 1)
        dp_ref[:, C_K:C_K + BLK] = (dk[BLK:] + ckv_ref[:, 0:BLK]).astype(BF16)
        dp_ref[:, C_V:C_V + BLK] = (dv[BLK:] + ckv_ref[:, BLK:]).astype(BF16)
        ckv_ref[:, 0:BLK] = dk[:BLK]
        ckv_ref[:, BLK:] = dv[:BLK]
        su, sv = cur[:, C_SU:C_SV], cur[:, C_SV:C_PC]
        gu, gu_slope = _gelu_with_slope(su)
        gv, gv_slope = _gelu_with_slope(sv)
        vn, vjp_v = jax.vjp(_rms, gv, gs_ref[...])
        vn16 = vn.astype(BF16)
        wmask = _sgu_mask()
        dgu, dvn = [], []
        for g in range(4):
            sl = slice(BLK * g, BLK * (g + 1))
            wm = jnp.where(wmask, ws_ref[g], 0.0).astype(BF16)
            sp = _dot(wm, vn16[:, sl], "nn") + bs_ref[g]
            dyb = dbr_ref[1, :, sl]
            dgu.append(dyb * sp)
            dsp = dyb * gu[:, sl]
            dsp16 = dsp.astype(BF16)
            dvn.append(_dot(wm, dsp16, "tn"))
            dws_ref[g] += jnp.where(wmask, _dot(dsp16, vn16[:, sl], "nt"), 0.0)
            dbs_ref[g] += jnp.sum(dsp, axis=1, keepdims=True)
        dgv, dgs = vjp_v(jnp.concatenate(dvn, axis=1))
        dp_ref[:, C_SU:C_SV] = (jnp.concatenate(dgu, axis=1) * gu_slope).astype(BF16)
        dp_ref[:, C_SV:C_PC] = (dgv * gv_slope).astype(BF16)
        dgs_ref[...] += dgs
        c = cur[:, C_PC:C_GATE]
        ext_ref[0:HALO, :] = jnp.where(not_first, pcp_ref[:, C_PC:C_GATE].astype(F32), 0.0)
        ext_ref[HALO:HALO + BLK, :] = c
        for g, w in enumerate(POOL_WINDOWS):
            sl = slice(BLK * g, BLK * (g + 1))
            acc = _trailing_sums(ext_ref[:, sl], g)[HALO:]
            inv_cnt = 1.0 / _pool_cnt(blk, w)
            pooled16 = (acc * inv_cnt - c[:, sl]).astype(BF16)
            wp16 = wp_ref[g].astype(BF16)
            mixed = _dot(pooled16, wp16, "nn")
            dyc = dbr_ref[2, :, sl]
            dps_ref[:, sl] += jnp.sum(dyc * mixed, axis=0, keepdims=True)
            dmixed16 = (dyc * ps_ref[:, sl]).astype(BF16)
            dwp_ref[g] += _dot(pooled16, dmixed16, "tn")
            dpooled = _dot(dmixed16, wp16, "nt")
            z_ref[HALO:HALO + BLK, sl] = dpooled * inv_cnt
            dext = _leading_sums(z_ref[:, sl], g)[:HALO + BLK]
            dp_ref[:, C_PC + BLK * g:C_PC + BLK * (g + 1)] = (
                dext[HALO:] - dpooled + jnp.concatenate([jnp.zeros((BLK - HALO, BLK), F32), cpc_ref[:, sl]], axis=0)
            ).astype(BF16)
            cpc_ref[:, sl] = dext[:HALO]

    n_in = 11
    small = [jax.ShapeDtypeStruct((8, BLK), F32), jax.ShapeDtypeStruct((4, BLK, BLK), F32),
             jax.ShapeDtypeStruct((4, BLK, 1), F32), jax.ShapeDtypeStruct((1, 512), F32),
             jax.ShapeDtypeStruct((4, BLK, BLK), F32), jax.ShapeDtypeStruct((1, 512), F32)]
    small_specs = [pl.BlockSpec((8, BLK), lambda i: (0, 0)), pl.BlockSpec((4, BLK, BLK), lambda i: (0, 0, 0)),
                   pl.BlockSpec((4, BLK, 1), lambda i: (0, 0, 0)), pl.BlockSpec((1, 512), lambda i: (0, 0)),
                   pl.BlockSpec((4, BLK, BLK), lambda i: (0, 0, 0)), pl.BlockSpec((1, 512), lambda i: (0, 0))]
    res = pl.pallas_call(
        body, name=name, grid=(nb,),
        in_specs=_mix_in_specs(nb, True) + [
            pl.BlockSpec((3, BLK, 512), lambda i: (0, nb - 1 - i, 0)),
            pl.BlockSpec(memory_space=pl.ANY)],
        out_specs=[pl.BlockSpec((BLK, C_GATE), lambda i: (nb - 1 - i, 0))] + small_specs,
        out_shape=[jax.ShapeDtypeStruct(dproj.shape, dproj.dtype)] + small,
        scratch_shapes=[pltpu.VMEM((HALO + BLK, 512), F32), pltpu.VMEM((2 * HALO + BLK, 512), F32),
                        pltpu.VMEM((BLK, 2 * BLK), F32), pltpu.VMEM((HALO, 512), F32)],
        input_output_aliases={n_in - 1: 0},
        compiler_params=_params(("arbitrary",)),
    )(proj, proj, proj, sinks_b, ws, bs3, gsgu, wp, ps, dbr, dproj)
    return tuple(res)


_GW = 256


def _merge_fwd(proj, pb, name, T=4096):
    S, D = pb.shape[1], pb.shape[2]
    T = min(T, S)

    def body(gate_ref, pb_ref, out_ref, acc_ref):
        n = pl.program_id(2)

        @pl.when(n == 0)
        def _():
            acc_ref[...] = jnp.zeros_like(acc_ref)

        acc_ref[...] += jax.nn.sigmoid(gate_ref[...].astype(F32)) * pb_ref[...]

        @pl.when(n == 2)
        def _():
            out_ref[...] = acc_ref[...].astype(BF16)

    return pl.pallas_call(
        body, name=name, grid=(S // T, D // _GW, 3),
        in_specs=[pl.BlockSpec((T, _GW), lambda i, j, n: (i, C_GATE // _GW + n * (D // _GW) + j)),
                  pl.BlockSpec((None, T, _GW), lambda i, j, n: (n, i, j))],
        out_specs=pl.BlockSpec((T, _GW), lambda i, j, n: (i, j)),
        out_shape=jax.ShapeDtypeStruct((S, D), BF16),
        scratch_shapes=[pltpu.VMEM((T, _GW), F32)],
        compiler_params=_params(("parallel", "parallel", "arbitrary")),
    )(proj, pb)


def _merge_bwd(proj, pb, dmerged, name, T=4096):
    S, D = pb.shape[1], pb.shape[2]
    T = min(T, S)

    def body(gate_ref, pb_ref, dm_ref, dgate_ref, dpb_ref):
        sg = jax.nn.sigmoid(gate_ref[...].astype(F32))
        dm = dm_ref[...]
        dpb_ref[...] = (dm * sg).astype(BF16)
        dgate_ref[...] = (dm * pb_ref[...] * sg * (1.0 - sg)).astype(BF16)

    gate_map = lambda i, n, j: (i, C_GATE // _GW + n * (D // _GW) + j)
    return pl.pallas_call(
        body, name=name, grid=(S // T, 3, D // _GW),
        in_specs=[pl.BlockSpec((T, _GW), gate_map),
                  pl.BlockSpec((None, T, _GW), lambda i, n, j: (n, i, j)),
                  pl.BlockSpec((T, _GW), lambda i, n, j: (i, j))],
        out_specs=[pl.BlockSpec((T, _GW), gate_map),
                   pl.BlockSpec((None, T, _GW), lambda i, n, j: (n, i, j))],
        out_shape=[jax.ShapeDtypeStruct((S, C_END), BF16), jax.ShapeDtypeStruct((3, S, D), BF16)],
        compiler_params=_params(("parallel", "parallel", "parallel")),
    )(proj, pb, dmerged)


def _memattn_fwd(qm, kv, name, T=512):
    S, NM = qm.shape[0], kv.shape[0]
    T = min(T, S)

    def body(q_ref, kv_ref, o_ref):
        for h in range(4):
            sl = slice(128 * h, 128 * (h + 1))
            k = kv_ref[:, sl].astype(BF16)
            v = kv_ref[:, 512 + 128 * h:512 + 128 * (h + 1)].astype(BF16)
            s = _dot(k, q_ref[:, sl].astype(BF16), "nt") * MEM_SCALE
            p = _softmax_rows(s)
            o_ref[:, sl] = _dot(p.astype(BF16), v, "tn").astype(BF16)

    return pl.pallas_call(
        body, name=name, grid=(S // T,),
        in_specs=[_row_spec(T, 512), pl.BlockSpec((NM, 1024), lambda i: (0, 0))],
        out_specs=_row_spec(T, 512), out_shape=jax.ShapeDtypeStruct((S, 512), BF16),
        compiler_params=_params(("parallel",)))(qm, kv)


def _memattn_bwd(qm, kv, dom, name, T=512):
    S, NM = qm.shape[0], kv.shape[0]
    T = min(T, S)

    def body(q_ref, kv_ref, do_ref, dq_ref, dkv_ref):
        i = pl.program_id(0)

        @pl.when(i == 0)
        def _():
            dkv_ref[...] = jnp.zeros_like(dkv_ref)

        for h in range(4):
            sl = slice(128 * h, 128 * (h + 1))
            sv_ = slice(512 + 128 * h, 512 + 128 * (h + 1))
            q = q_ref[:, sl].astype(BF16)
            k = kv_ref[:, sl].astype(BF16)
            v = kv_ref[:, sv_].astype(BF16)
            do = do_ref[:, sl].astype(BF16)
            p = _softmax_rows(_dot(k, q, "nt") * MEM_SCALE)
            dp = _dot(v, do, "nt")
            ds = (p * (dp - jnp.sum(p * dp, axis=0, keepdims=True)) * MEM_SCALE).astype(BF16)
            dq_ref[:, sl] = _dot(ds, k, "tn").astype(BF16)
            dkv_ref[:, sl] += _dot(ds, q, "nn")
            dkv_ref[:, sv_] += _dot(p.astype(BF16), do, "nn")

    return pl.pallas_call(
        body, name=name, grid=(S // T,),
        in_specs=[_row_spec(T, 512), pl.BlockSpec((NM, 1024), lambda i: (0, 0)), _row_spec(T, 512)],
        out_specs=[_row_spec(T, 512), pl.BlockSpec((NM, 1024), lambda i: (0, 0))],
        out_shape=[jax.ShapeDtypeStruct((S, 512), BF16), jax.ShapeDtypeStruct((NM, 1024), F32)],
        compiler_params=_params(("arbitrary",)))(qm, kv, dom)


def _adamw(w, g, m, v, name, rows=None, g_row0=None, into=None, TR=512):
    R, C = w.shape
    lo, hi = rows if rows is not None else (0, R)
    g0 = lo if g_row0 is None else g_row0
    TR = _row_tile(math.gcd(math.gcd(lo, g0), hi - lo), TR)
    c1 = 1.0 - ADAM_B1 ** ADAM_STEP
    c2 = 1.0 - ADAM_B2 ** ADAM_STEP

    def body(w_ref, g_ref, m_ref, v_ref, *rest):
        d_ref, nm_ref, nv_ref = rest[-3:]
        gv = g_ref[...]
        nm = ADAM_B1 * m_ref[...] + (1.0 - ADAM_B1) * gv
        nv = ADAM_B2 * v_ref[...] + (1.0 - ADAM_B2) * jnp.square(gv)
        d_ref[...] = -ADAM_LR * ((nm / c1) / (jnp.sqrt(nv / c2) + ADAM_EPS) + ADAM_WD * w_ref[...])
        nm_ref[...] = nm
        nv_ref[...] = nv

    spec = pl.BlockSpec((TR, C), lambda i: (lo // TR + i, 0))
    g_spec = pl.BlockSpec((TR, C), lambda i: (g0 // TR + i, 0))
    prior = list(into) if into is not None else []
    return pl.pallas_call(
        body, name=name, grid=((hi - lo) // TR,),
        in_specs=[spec, g_spec, spec, spec] + [pl.BlockSpec(memory_space=pl.ANY)] * len(prior), out_specs=[spec] * 3,
        out_shape=[jax.ShapeDtypeStruct((R, C), F32)] * 3,
        input_output_aliases={4 + k: k for k in range(len(prior))},
        compiler_params=_params(("parallel",)))(w, g, m, v, *prior)


def _row_tile(R, pref):
    t = (pref // 8) * 8
    while t >= 8:
        if R % t == 0:
            return t
        t -= 8
    raise ValueError(f"no row tile for {R}")


def _sum_slots(stack, name, TR=512):
    n, R, C = stack.shape
    TR = R if R <= TR else _row_tile(R, TR)

    def body(s_ref, o_ref):
        acc = s_ref[0]
        for k in range(1, n):
            acc = acc + s_ref[k]
        o_ref[...] = acc

    return pl.pallas_call(
        body, name=name, grid=(R // TR,),
        in_specs=[pl.BlockSpec((n, TR, C), lambda i: (0, i, 0))],
        out_specs=pl.BlockSpec((TR, C), lambda i: (i, 0)),
        out_shape=jax.ShapeDtypeStruct((R, C), F32),
        compiler_params=_params(("parallel",)))(stack)


_ANY = pl.BlockSpec(memory_space=pl.ANY)


def _chip_of(j, c):
    return (j // 2, j % 2, c)


def _own_slab(shard, dtype, j_arr, name, first=0, count=None, plus=None, deps=(), TR=512):
    N, r, C = shard.shape
    B = N if count is None else count
    rh = r // 2
    TR = rh if rh <= TR else _row_tile(rh, TR)
    nt = rh // TR
    ins = [shard] if plus is None else [shard, plus]

    def body(j_ref, *refs):
        val = refs[0][...] if plus is None else refs[0][...] + refs[1][...]
        refs[-1][...] = val.astype(refs[-1].dtype)

    return pl.pallas_call(
        body, name=name,
        grid_spec=pltpu.PrefetchScalarGridSpec(
            num_scalar_prefetch=1, grid=(B, 2, nt),
            in_specs=[pl.BlockSpec((None, TR, C), lambda b, h, t, jr: (first + b, h * nt + t, 0))] * len(ins)
            + [_ANY] * len(deps),
            out_specs=pl.BlockSpec((None, None, None, TR, C), lambda b, h, t, jr: (b, jr[0], h, t, 0))),
        out_shape=jax.ShapeDtypeStruct((B, 4, 2, rh, C), dtype),
        compiler_params=_params(("parallel", "parallel", "parallel")),
    )(j_arr, *ins, *deps)


def _gather_weights(bufs, name):
    n = len(bufs)

    def body(*refs):
        buf = refs[n:2 * n]
        send_sems, recv_sems, fsend_sems, frecv_sems = refs[2 * n:]
        x, y, c = lax.axis_index("x"), lax.axis_index("y"), lax.axis_index("c")
        j = 2 * x + y
        sib = (x, y, 1 - c)
        sends = []
        for d in range(1, 4):
            for a in range(n):
                cp = pltpu.make_async_remote_copy(
                    src_ref=buf[a].at[:, j, c], dst_ref=buf[a].at[:, j, c], send_sem=send_sems.at[a, d - 1],
                    recv_sem=recv_sems.at[a, d - 1], device_id=_chip_of((j + d) % 4, c), device_id_type=MESH)
                cp.start()
                sends.append(cp)
        for d in range(1, 4):
            frm = (j + 4 - d) % 4
            for a in range(n):
                pltpu.make_async_remote_copy(
                    src_ref=buf[a].at[:, frm, c], dst_ref=buf[a].at[:, frm, c], send_sem=send_sems.at[a, d - 1],
                    recv_sem=recv_sems.at[a, d - 1], device_id=_chip_of(frm, c), device_id_type=MESH).wait_recv()
                cp = pltpu.make_async_remote_copy(
                    src_ref=buf[a].at[:, frm, c], dst_ref=buf[a].at[:, frm, c], send_sem=fsend_sems.at[a, d - 1],
                    recv_sem=frecv_sems.at[a, d - 1], device_id=sib, device_id_type=MESH)
                cp.start()
                sends.append(cp)
        for d in range(1, 4):
            frm = (j + 4 - d) % 4
            for a in range(n):
                pltpu.make_async_remote_copy(
                    src_ref=buf[a].at[:, frm, 1 - c], dst_ref=buf[a].at[:, frm, 1 - c], send_sem=fsend_sems.at[a, d - 1],
                    recv_sem=frecv_sems.at[a, d - 1], device_id=sib, device_id_type=MESH).wait_recv()
        for cp in sends:
            cp.wait_send()

    return pl.pallas_call(
        body, name=name,
        in_specs=[_ANY] * n, out_specs=[_ANY] * n,
        out_shape=[jax.ShapeDtypeStruct(b.shape, b.dtype) for b in bufs],
        scratch_shapes=[pltpu.SemaphoreType.DMA((n, 3))] * 4,
        input_output_aliases={a: a for a in range(n)},
    )(*bufs)


_HBM = pl.BlockSpec(memory_space=pltpu.HBM)
_SEM = pl.BlockSpec(memory_space=pltpu.SEMAPHORE)
_DATAFLOW = pltpu.SideEffectType.DATAFLOW_SIDE_EFFECTING


def _in_hbm(arrays):
    return [pltpu.with_memory_space_constraint(a, pltpu.HBM) for a in arrays]


def _start_copies(bufs, plan, count, name, deps=()):
    n, k = len(bufs), len(deps)

    def body(*refs):
        send_sems, recv_sems = refs[n + k], refs[n + k + 1]
        for i, (src, dst, dev) in enumerate(plan(refs[:n], False)):
            pltpu.make_async_remote_copy(src_ref=src, dst_ref=dst, send_sem=send_sems.at[i], recv_sem=recv_sems.at[i],
                                         device_id=dev, device_id_type=MESH).start()
        refs[-1][...] = jnp.zeros_like(refs[-1])

    return pl.pallas_call(
        body, name=name,
        out_shape=(pltpu.SemaphoreType.DMA((count,)), pltpu.SemaphoreType.DMA((count,)),
                   *[pltpu.HBM(b.shape, b.dtype) for b in bufs], jax.ShapeDtypeStruct((8, 128), F32)),
        in_specs=[_HBM] * n + [_ANY] * k,
        out_specs=(_SEM, _SEM, *[_HBM] * n, pl.BlockSpec(memory_space=pltpu.VMEM)),
        input_output_aliases={a: 2 + a for a in range(n)},
        compiler_params=pltpu.CompilerParams(has_side_effects=_DATAFLOW),
    )(*_in_hbm(bufs), *deps)


def _wait_copies(handle, plan, afters, name):
    send_sems, recv_sems, *bufs = handle[:-1]
    n = len(bufs)

    def body(*refs):
        send_sems, recv_sems = refs[n], refs[n + 1]
        for i, (src, dst, dev) in enumerate(plan(refs[:n], True)):
            cp = pltpu.make_async_remote_copy(src_ref=src, dst_ref=dst, send_sem=send_sems.at[i], recv_sem=recv_sems.at[i],
                                              device_id=dev, device_id_type=MESH)
            cp.wait_send()
            cp.wait_recv()

    return list(pl.pallas_call(
        body, name=name,
        out_shape=[pltpu.HBM(b.shape, b.dtype) for b in bufs],
        in_specs=[_HBM] * n + [_SEM, _SEM] + [_ANY] * len(afters), out_specs=[_HBM] * n,
        input_output_aliases={a: a for a in range(n)},
        compiler_params=pltpu.CompilerParams(has_side_effects=_DATAFLOW),
    )(*bufs, send_sems, recv_sems, *afters))


def _gather_plan(buf, waiting):
    c = lax.axis_index("c")
    j = 2 * lax.axis_index("x") + lax.axis_index("y")
    copies = []
    for d in range(1, 4):
        to, frm = (j + d) % 4, (j + 4 - d) % 4
        for b in buf:
            copies.append((b.at[:, j, c], b.at[:, frm if waiting else j, c], _chip_of(frm if waiting else to, c)))
    return copies


def _chip_plan(buf, waiting):
    n = len(buf) // 2
    c = lax.axis_index("c")
    j = 2 * lax.axis_index("x") + lax.axis_index("y")
    copies = []
    for d in range(1, 4):
        to = (j + d) % 4
        for a in range(n):
            copies.append((buf[a].at[to], buf[n + a].at[d - 1], _chip_of(to, c)))
    return copies


def _pair_plan(buf, waiting):
    n = len(buf) // 2
    c = lax.axis_index("c")
    sib = (lax.axis_index("x"), lax.axis_index("y"), 1 - c)
    return [(buf[a].at[:, pl.ds(1 - c, 1)], buf[n + a], sib) for a in range(n)]


def _forward_plan(buf, waiting):
    x, y, c = lax.axis_index("x"), lax.axis_index("y"), lax.axis_index("c")
    j = 2 * x + y
    copies = []
    for d in range(1, 4):
        frm = (j + 4 - d) % 4
        for b in buf:
            copies.append((b.at[:, frm, c], b.at[:, frm, 1 - c if waiting else c], (x, y, 1 - c)))
    return copies


def _gather_forward(bufs, name):
    n = len(bufs)

    def body(*refs):
        buf = refs[n:2 * n]
        send_sems, recv_sems = refs[2 * n:]
        x, y, c = lax.axis_index("x"), lax.axis_index("y"), lax.axis_index("c")
        j = 2 * x + y
        sib = (x, y, 1 - c)
        sends = []
        for d in range(1, 4):
            frm = (j + 4 - d) % 4
            for a in range(n):
                cp = pltpu.make_async_remote_copy(
                    src_ref=buf[a].at[:, frm, c], dst_ref=buf[a].at[:, frm, c], send_sem=send_sems.at[a, d - 1],
                    recv_sem=recv_sems.at[a, d - 1], device_id=sib, device_id_type=MESH)
                cp.start()
                sends.append(cp)
        for d in range(1, 4):
            frm = (j + 4 - d) % 4
            for a in range(n):
                pltpu.make_async_remote_copy(
                    src_ref=buf[a].at[:, frm, 1 - c], dst_ref=buf[a].at[:, frm, 1 - c], send_sem=send_sems.at[a, d - 1],
                    recv_sem=recv_sems.at[a, d - 1], device_id=sib, device_id_type=MESH).wait_recv()
        for cp in sends:
            cp.wait_send()

    return pl.pallas_call(
        body, name=name,
        in_specs=[_ANY] * n, out_specs=[_ANY] * n,
        out_shape=[jax.ShapeDtypeStruct(b.shape, b.dtype) for b in bufs],
        scratch_shapes=[pltpu.SemaphoreType.DMA((n, 3))] * 2,
        input_output_aliases={a: a for a in range(n)},
    )(*bufs)


def _pair_add(g4, r1, cj_arr, name, TR=512):
    B4, _, rh, C = g4.shape
    B = B4 // 4
    TR = rh if rh <= TR else _row_tile(rh, TR)

    def body(cj_ref, g_ref, r_ref, o16_ref, own_ref):
        s = g_ref[...].astype(F32) + r_ref[...].astype(F32)
        o16_ref[...] = s.astype(BF16)

        @pl.when(pl.program_id(2) == cj_ref[1])
        def _():
            own_ref[...] = s

    return pl.pallas_call(
        body, name=name,
        grid_spec=pltpu.PrefetchScalarGridSpec(
            num_scalar_prefetch=1, grid=(B, rh // TR, 4),
            in_specs=[pl.BlockSpec((None, None, TR, C), lambda b, t, p, cj: (b * 4 + p, cj[0], t, 0)),
                      pl.BlockSpec((None, None, TR, C), lambda b, t, p, cj: (b * 4 + p, 0, t, 0))],
            out_specs=[pl.BlockSpec((None, None, TR, C), lambda b, t, p, cj: (p, b, t, 0)),
                       pl.BlockSpec((None, TR, C), lambda b, t, p, cj: (b, t, 0))]),
        out_shape=[jax.ShapeDtypeStruct((4, B, rh, C), BF16), jax.ShapeDtypeStruct((B, rh, C), F32)],
        compiler_params=_params(("parallel", "parallel", "arbitrary")),
    )(cj_arr, g4, r1)


def _chip_add(own, r2, cj_arr, into, first, name, TR=512):
    B, rh, C = own.shape
    TR = rh if rh <= TR else _row_tile(rh, TR)

    def body(cj_ref, p_ref, r_ref, _into_ref, o_ref):
        o_ref[...] = p_ref[...] + r_ref[0].astype(F32) + r_ref[1].astype(F32) + r_ref[2].astype(F32)

    return pl.pallas_call(
        body, name=name,
        grid_spec=pltpu.PrefetchScalarGridSpec(
            num_scalar_prefetch=1, grid=(B, rh // TR),
            in_specs=[pl.BlockSpec((None, TR, C), lambda b, t, cj: (b, t, 0)),
                      pl.BlockSpec((3, None, TR, C), lambda b, t, cj: (0, b, t, 0)),
                      _ANY],
            out_specs=pl.BlockSpec((None, None, TR, C), lambda b, t, cj: (first + b, cj[0], t, 0))),
        out_shape=jax.ShapeDtypeStruct(into.shape, F32),
        input_output_aliases={3: 0},
        compiler_params=_params(("parallel", "parallel")),
    )(cj_arr, own, r2, into)


def _pair_share(bufs, spans, name, deps=()):
    n = len(bufs)

    def body(*refs):
        buf = refs[n + len(deps):2 * n + len(deps)]
        send_sems, recv_sems = refs[2 * n + len(deps):]
        c = lax.axis_index("c")
        sib = (lax.axis_index("x"), lax.axis_index("y"), 1 - c)
        cps = []
        for a, (first, count) in enumerate(spans):
            cp = pltpu.make_async_remote_copy(
                src_ref=buf[a].at[pl.ds(first, count), c], dst_ref=buf[a].at[pl.ds(first, count), c],
                send_sem=send_sems.at[a], recv_sem=recv_sems.at[a], device_id=sib, device_id_type=MESH)
            cp.start()
            cps.append(cp)
        for a, (first, count) in enumerate(spans):
            pltpu.make_async_remote_copy(
                src_ref=buf[a].at[pl.ds(first, count), 1 - c], dst_ref=buf[a].at[pl.ds(first, count), 1 - c],
                send_sem=send_sems.at[a], recv_sem=recv_sems.at[a], device_id=sib, device_id_type=MESH).wait_recv()
        for cp in cps:
            cp.wait_send()

    return pl.pallas_call(
        body, name=name, in_specs=[_ANY] * (n + len(deps)), out_specs=[_ANY] * n,
        out_shape=[jax.ShapeDtypeStruct(b.shape, b.dtype) for b in bufs],
        scratch_shapes=[pltpu.SemaphoreType.DMA((n,)), pltpu.SemaphoreType.DMA((n,))],
        input_output_aliases={a: a for a in range(n)},
    )(*bufs, *deps)


def _pair_swap(arr, name):
    def body(src, dst, send_sem, recv_sem):
        sib = (lax.axis_index("x"), lax.axis_index("y"), 1 - lax.axis_index("c"))
        cp = pltpu.make_async_remote_copy(src_ref=src, dst_ref=dst, send_sem=send_sem, recv_sem=recv_sem,
                                          device_id=sib, device_id_type=MESH)
        cp.start()
        cp.wait_recv()
        cp.wait_send()

    return pl.pallas_call(
        body, name=name, in_specs=[_ANY], out_specs=_ANY,
        out_shape=jax.ShapeDtypeStruct(arr.shape, arr.dtype),
        scratch_shapes=[pltpu.SemaphoreType.DMA, pltpu.SemaphoreType.DMA],
    )(arr)


class _ReduceScatter:
    def __init__(self, n_layers, cj_arr):
        self.L, self.cj = n_layers, cj_arr
        self.total = None
        self.pair = None
        self.chip = None

    def _land(self, after):
        handle, layer, owns = self.chip
        n = len(owns)
        r2 = _wait_copies(handle, _chip_plan, after if isinstance(after, (list, tuple)) else (after,), "rs_chip_wait")[n:]
        if self.total is None:
            self.total = [lax.empty((self.L * o.shape[0], 2) + o.shape[1:], F32) for o in owns]
        self.total = [_chip_add(o, r, self.cj, t, layer * o.shape[0], "rs_chip_add")
                      for o, r, t in zip(owns, r2, self.total)]
        self.chip = None

    def add_layer(self, layer, grads):
        g4 = [g.reshape(g.shape[0] * 4, 2, g.shape[1] // 8, g.shape[2]) for g in grads]
        lands = [lax.empty((g.shape[0], 1) + g.shape[2:], g.dtype) for g in g4]
        handle = _start_copies(g4 + lands, _pair_plan, len(g4), "rs_pair_start")
        self.pair = (handle, layer)
        return (handle[-1],)

    def advance(self, after):
        if self.pair is None:
            return ()
        handle, layer = self.pair
        both = _wait_copies(handle, _pair_plan, (after,), "rs_pair_wait")
        n = len(both) // 2
        added = [_pair_add(g, r, self.cj, "rs_pair_add") for g, r in zip(both[:n], both[n:])]
        parts, owns = [p for p, _ in added], [o for _, o in added]
        lands = [lax.empty((3,) + p.shape[1:], p.dtype) for p in parts]
        handle = _start_copies(parts + lands, _chip_plan, 3 * n, "rs_chip_start")
        if self.chip is not None:
            self._land(handle[-1])
        self.pair, self.chip = None, (handle, layer, owns)
        return (handle[-1],)

    def upper_layers(self, deps):
        per = [t.shape[0] // self.L for t in self.total]
        self.total = _pair_share(self.total, [(b, (self.L - 1) * b) for b in per], "rs_pair_share_upper", deps=deps)
        return [t.reshape(t.shape[0], t.shape[1] * t.shape[2], t.shape[3]) for t in self.total]

    def first_layer(self, after):
        self._land(after)
        per = [t.shape[0] // self.L for t in self.total]
        full = _pair_share(self.total, [(0, b) for b in per], "rs_pair_share_first")
        return [f.reshape(f.shape[0], f.shape[1] * f.shape[2], f.shape[3]) for f in full]


def _relu2_epi(acc):
    return (jnp.square(jnp.maximum(acc, 0.0)),)


def _relu2_bwd_epi(acc, a):
    return (acc * (2.0 * jnp.sqrt(a.astype(F32))),)


_GRAD_ORDER = ("winT", "wbT", "wout", "wq", "wkv", "woT", "wupT", "wdown")
_DW = dict(tm=512, tn=1024, tk=4096, out_dtypes=(BF16,))
_LONG_K = dict(tm=1024, tn=1024, tk=2048)


def _forward_backward(x, mem, target, input_weight_of, weights_of, prefetch_weights, P, grads_done, grads_advance):
    L = P["g_norm"].shape[0]
    S, D = x.shape
    gn = lambda l, i: P["g_norm"][l, i][None]

    saved = []
    (h,) = _resnorm_fwd(x, None, None, gn(0, 0), "norm_in")
    xr = x
    for l in range(L):
        w_in_t = input_weight_of(l, xr)
        proj = _mm(h, w_in_t, "nt", "in_proj", b_pre=(0,), out_dtypes=(BF16,), tn=1792)
        W, w_deps = weights_of(l, proj)
        small = (jnp.broadcast_to(P["sinks"][l][:, None], (8, BLK)), P["ws"][l], P["bs"][l][:, :, None],
                 P["gsgu"][l][None], P["wp"][l], P["ps"][l][None])
        br = _mix_fwd(proj, *small, "mix_fwd")
        pb = _mm(br, W["wbT"], "nt", "branch_proj", batch=3, out_dtypes=(BF16,), deps=w_deps)
        merged = _merge_fwd(proj, pb, "merge_fwd")
        z = _mm(merged, W["wout"], "nn", "out_proj", b_pre=(0,), out_dtypes=(BF16,))
        x1, hm = _resnorm_fwd(xr, z, gn(l, 1), gn(l, 2), "resnorm_fwd")
        qm = _mm(hm, W["wq"], "nn", "mem_q", b_pre=(0,))
        (memn,) = _resnorm_fwd(mem, None, None, P["g_mem"][l][None], "mem_norm")
        kv = _mm(memn, W["wkv"], "nn", "mem_kv", b_pre=(0,))
        om = _memattn_fwd(qm, kv, "memattn_fwd")
        ym = _mm(om, W["woT"], "nt", "mem_o", b_pre=(0,), out_dtypes=(BF16,))
        x2, hf = _resnorm_fwd(x1, ym, gn(l, 3), gn(l, 4), "resnorm_fwd")
        a = _mm(hf, W["wupT"], "nt", "mlp_up", b_pre=(0,), out_dtypes=(BF16,), epi=_relu2_epi)
        yf = _mm(a, W["wdown"], "nn", "mlp_down", b_pre=(0,), out_dtypes=(BF16,), **_LONG_K,
                 deps=prefetch_weights(l + 1, a) if l < L - 1 else ())
        saved.append(dict(W=W, x0=xr, h=h, proj=proj, small=small, br=br, pb=pb, merged=merged, z=z, x1=x1, hm=hm,
                          qm=qm, memn=memn, kv=kv, om=om, ym=ym, x2=x2, hf=hf, a=a, yf=yf))
        if l < L - 1:
            xr, h = _resnorm_fwd(x2, yf, gn(l, 5), gn(l + 1, 0), "resnorm_fwd")
    dres, loss = _final_fwd(saved[-1]["x2"], saved[-1]["yf"], gn(L - 1, 5), target, "loss_head")

    dgn = [[None] * 6 for _ in range(L)]
    dsmall = {k: [None] * L for k in ("g_mem", "sinks", "ws", "bs", "gsgu", "wp", "ps")}
    dh = None
    for l in reversed(range(L)):
        s = saved[l]
        W, G = s["W"], {}
        if l == L - 1:
            dx2, dyf, dgn[l][5] = _resnorm_bwd(s["x2"], s["yf"], gn(l, 5), None, dres, None, "resnorm_bwd_top")
        else:
            dx2, dyf, dgn[l][5], dgn[l + 1][0] = _resnorm_bwd(s["x2"], s["yf"], gn(l, 5), gn(l + 1, 0), dres, dh,
                                                              "resnorm_bwd", deps=deps)
        du = _mm(dyf, W["wdown"], "nt", "mlp_down_dx", b_pre=(0,), out_dtypes=(BF16,), extras=(s["a"],), epi=_relu2_bwd_epi)
        G["wdown"] = _mm(s["a"], dyf, "tn", "mlp_down_dw", **_DW)[None]
        dhf = _mm(du, W["wupT"], "nn", "mlp_up_dx", b_pre=(0,), out_dtypes=(BF16,), **_LONG_K)
        G["wupT"] = _mm(du, s["hf"], "tn", "mlp_up_dw", **_DW)[None]
        dx1, dym, dgn[l][3], dgn[l][4] = _resnorm_bwd(s["x1"], s["ym"], gn(l, 3), gn(l, 4), dx2, dhf, "resnorm_bwd")
        dom = _mm(dym, W["woT"], "nn", "mem_o_dx", b_pre=(0,), out_dtypes=(BF16,), deps=grads_advance(dx1))
        G["woT"] = _mm(dym, s["om"], "tn", "mem_o_dw", **_DW)[None]
        dqm, dkv = _memattn_bwd(s["qm"], s["kv"], dom, "memattn_bwd")
        dmemn = _mm(dkv, W["wkv"], "nt", "mem_kv_dx", b_pre=(0,))
        G["wkv"] = _mm(s["memn"], dkv, "tn", "mem_kv_dw", out_dtypes=(BF16,))[None]
        _, dsmall["g_mem"][l] = _resnorm_bwd(mem, None, None, P["g_mem"][l][None], None, dmemn, "mem_norm_bwd")
        dhm = _mm(dqm, W["wq"], "nt", "mem_q_dx", b_pre=(0,), out_dtypes=(BF16,))
        G["wq"] = _mm(s["hm"], dqm, "tn", "mem_q_dw", **_DW)[None]
        dx0, dz, dgn[l][1], dgn[l][2] = _resnorm_bwd(s["x0"], s["z"], gn(l, 1), gn(l, 2), dx1, dhm, "resnorm_bwd")
        dmerged = _mm(dz, W["wout"], "nt", "out_proj_dx", b_pre=(0,), out_dtypes=(BF16,))
        G["wout"] = _mm(s["merged"], dz, "tn", "out_proj_dw", **_DW)[None]
        dproj, dpb = _merge_bwd(s["proj"], s["pb"], dmerged, "merge_bwd")
        dbr = _mm(dpb, W["wbT"], "nn", "branch_proj_dx", batch=3, out_dtypes=(BF16,))
        G["wbT"] = _mm(dpb, s["br"], "tn", "branch_proj_dw", batch=3, **_DW)
        (dproj, dsmall["sinks"][l], dsmall["ws"][l], dsmall["bs"][l], dsmall["gsgu"][l], dsmall["wp"][l],
         dsmall["ps"][l]) = _mix_bwd(s["proj"], dbr, dproj, *s["small"], "mix_bwd")
        dh = _mm(dproj, W["winT"], "nn", "in_proj_dx", b_pre=(0,), out_dtypes=(BF16,), **_LONG_K)
        G["winT"] = _mm(dproj, s["h"], "tn", "in_proj_dw", **_DW)[None]
        deps = grads_done(l, [G[k] for k in _GRAD_ORDER])
        dres = dx0
    grad_x, dgn[0][0] = _resnorm_bwd(x, None, None, gn(0, 0), dres, dh, "norm_in_bwd", deps=deps)
    tail_deps = grads_advance(grad_x)

    small_grads = dict(
        g_norm=jnp.stack([jnp.concatenate(row, axis=0) for row in dgn]),
        g_mem=jnp.concatenate(dsmall["g_mem"], axis=0),
        sinks=jnp.stack([d[:, 0] for d in dsmall["sinks"]]),
        ws=jnp.stack(dsmall["ws"]),
        bs=jnp.stack([d[:, :, 0] for d in dsmall["bs"]]),
        gsgu=jnp.concatenate(dsmall["gsgu"], axis=0),
        wp=jnp.stack(dsmall["wp"]),
        ps=jnp.concatenate(dsmall["ps"], axis=0),
    )
    return loss, grad_x, small_grads, tail_deps


_PACK_ROWS = 512


def _as_rows(a):
    n = math.prod(a.shape)
    if n % 128:
        a = jnp.pad(a.reshape(-1), (0, (-n) % 128))
    r = a.reshape(-1, 128)
    return jnp.pad(r, ((0, (-r.shape[0]) % 8), (0, 0))) if r.shape[0] % 8 else r


def _pack(arrays):
    rows = [_as_rows(a) for a in arrays]
    total = sum(r.shape[0] for r in rows)
    tail = (-total) % _PACK_ROWS
    if tail:
        rows.append(jnp.zeros((tail, 128), rows[0].dtype))
    return jnp.concatenate(rows, axis=0)


def _unpack(packed, like):
    out, pos = [], 0
    for a in like:
        n = math.prod(a.shape)
        nr = -(-n // 128)
        rows = packed[pos:pos + nr]
        out.append((rows.reshape(-1)[:n] if n % 128 else rows).reshape(a.shape))
        pos += nr + (-nr) % 8
    return out


_BIG = ("w_in", "w_branch", "w_out", "w_q_mem", "w_kv_mem", "w_o_mem", "w_up", "w_down")
_SMALL = ("g_norm", "g_mem", "attn_sinks", "w_spatial", "b_spatial", "g_sgu", "w_pool", "pool_scale")
_WEIGHTS = ("g_norm", "g_mem", "w_in", "attn_sinks", "w_spatial", "b_spatial", "g_sgu", "w_pool", "pool_scale",
            "w_branch", "w_out", "w_q_mem", "w_kv_mem", "w_o_mem", "w_up", "w_down")


def _to_working(name, w):
    if name == "w_in":
        return jnp.swapaxes(w, 1, 2)
    if name == "w_branch":
        t = jnp.swapaxes(w, 2, 3)
        return t.reshape(t.shape[0] * 3, t.shape[2], t.shape[3])
    if name in ("w_o_mem", "w_up"):
        return jnp.swapaxes(w, 1, 2)
    return w


def _from_working(name, g):
    if name == "w_in":
        return jnp.swapaxes(g, 1, 2)
    if name == "w_branch":
        return jnp.swapaxes(g.reshape(g.shape[0] // 3, 3, g.shape[1], g.shape[2]), 2, 3)
    if name in ("w_o_mem", "w_up"):
        return jnp.swapaxes(g, 1, 2)
    return g


def kernel(x, mem, g_norm, g_mem, w_in, attn_sinks, w_spatial, b_spatial, g_sgu, w_pool, pool_scale, w_branch, w_out, w_q_mem, w_kv_mem, w_o_mem, w_up, w_down, loss_target, m_g_norm, m_g_mem, m_w_in, m_attn_sinks, m_w_spatial, m_b_spatial, m_g_sgu, m_w_pool, m_pool_scale, m_w_branch, m_w_out, m_w_q_mem, m_w_kv_mem, m_w_o_mem, m_w_up, m_w_down, v_g_norm, v_g_mem, v_w_in, v_attn_sinks, v_w_spatial, v_b_spatial, v_g_sgu, v_w_pool, v_pool_scale, v_w_branch, v_w_out, v_w_q_mem, v_w_kv_mem, v_w_o_mem, v_w_up, v_w_down):
    w = dict(g_norm=g_norm, g_mem=g_mem, w_in=w_in, attn_sinks=attn_sinks, w_spatial=w_spatial, b_spatial=b_spatial,
             g_sgu=g_sgu, w_pool=w_pool, pool_scale=pool_scale, w_branch=w_branch, w_out=w_out, w_q_mem=w_q_mem,
             w_kv_mem=w_kv_mem, w_o_mem=w_o_mem, w_up=w_up, w_down=w_down)
    m = dict(g_norm=m_g_norm, g_mem=m_g_mem, w_in=m_w_in, attn_sinks=m_attn_sinks, w_spatial=m_w_spatial,
             b_spatial=m_b_spatial, g_sgu=m_g_sgu, w_pool=m_w_pool, pool_scale=m_pool_scale, w_branch=m_w_branch,
             w_out=m_w_out, w_q_mem=m_w_q_mem, w_kv_mem=m_w_kv_mem, w_o_mem=m_w_o_mem, w_up=m_w_up, w_down=m_w_down)
    v = dict(g_norm=v_g_norm, g_mem=v_g_mem, w_in=v_w_in, attn_sinks=v_attn_sinks, w_spatial=v_w_spatial,
             b_spatial=v_b_spatial, g_sgu=v_g_sgu, w_pool=v_w_pool, pool_scale=v_pool_scale, w_branch=v_w_branch,
             w_out=v_w_out, w_q_mem=v_w_q_mem, w_kv_mem=v_w_kv_mem, w_o_mem=v_w_o_mem, w_up=v_w_up, w_down=v_w_down)
    L = g_norm.shape[0]
    j = 2 * lax.axis_index("x") + lax.axis_index("y")
    c = lax.axis_index("c")
    j_arr = jnp.reshape(j, (1,)).astype(jnp.int32)
    cj_arr = jnp.stack([c, j]).astype(jnp.int32)

    gs = g_norm.shape[2]
    working = [_to_working(n, w[n]) for n in _BIG]
    per_layer = [wk.shape[0] // L for wk in working]

    def own_slabs(l):
        return [_own_slab(wk, BF16, j_arr, "own_slab", first=l * b, count=b) for wk, b in zip(working, per_layer)]

    first_slabs = own_slabs(0)
    lead = [first_slabs[0], _own_slab(g_norm.reshape(1, L * 6 * gs // 128, 128), F32, j_arr, "own_slab_norm")]
    lead_handle = _start_copies(lead, _gather_plan, 3 * len(lead), "gather_start_lead")
    rest_handle = _start_copies(first_slabs[1:], _gather_plan, 3 * (len(first_slabs) - 1), "gather_start_first",
                                deps=(lead_handle[-1],))
    slabs = {l: own_slabs(l) for l in range(1, L)}
    lead = _gather_forward(_wait_copies(lead_handle, _gather_plan,
                                        [rest_handle[-1]] + [s for l in slabs for s in slabs[l]],
                                        "gather_wait_lead"), "gather_forward_lead")
    gn_full = jnp.transpose(lead[1].reshape(4, L * 6, gs), (1, 0, 2)).reshape(L, 6, 4 * gs)
    P = dict(g_norm=gn_full, g_mem=g_mem, sinks=attn_sinks, ws=w_spatial, bs=b_spatial, gsgu=g_sgu, wp=w_pool,
             ps=pool_scale)
    whole = lambda g: g.reshape(g.shape[0], 8 * g.shape[3], g.shape[4])
    gathered, in_flight = {}, {}

    forwarding = {}

    def prefetch_weights(l, after):
        arrived = _wait_copies(in_flight[l], _gather_plan, (after,), "gather_wait")
        forwarding[l] = _start_copies(arrived, _forward_plan, 3 * len(arrived), "gather_forward_start")
        return (forwarding[l][-1],)

    def layer_weights(l, after):
        if l not in gathered:
            gathered[l] = _wait_copies(forwarding[l], _forward_plan, (after,), "gather_forward_wait")
        return gathered[l]

    def input_weight_of(l, after):
        return whole(lead[0] if l == 0 else layer_weights(l, after)[0])

    def weights_of(l, after):
        deps = ()
        if l == 0:
            rest = _gather_forward(_wait_copies(rest_handle, _gather_plan, (after,), "gather_wait_first"),
                                   "gather_forward_first")
            gathered[0] = [lead[0], *rest]
            dep = rest[0]
            for k in range(1, L):
                in_flight[k] = _start_copies(slabs[k], _gather_plan, 3 * len(slabs[k]), "gather_start", deps=(dep,))
                dep = in_flight[k][-1]
            deps = tuple(h[-1] for h in in_flight.values())
        return {k: whole(g) for k, g in zip(_GRAD_ORDER, layer_weights(l, after))}, deps

    rs = _ReduceScatter(L, cj_arr)
    loss_part, grad_x, sg, tail_deps = _forward_backward(
        x[0], mem[0], loss_target[0], input_weight_of, weights_of, prefetch_weights, P, rs.add_layer, rs.advance)

    transposed = ("w_branch", "w_o_mem", "w_up")
    view = {n: (lambda t: jnp.swapaxes(t, 1, 2)) if n == "w_in" else (lambda t: t) for n in _BIG}
    rows = lambda n, t: view[n](t).reshape(-1, view[n](t).shape[-1])
    state = {n: (rows(n, w[n]), rows(n, m[n]), rows(n, v[n])) for n in _BIG}
    updated, upper_grad = {}, {}
    if L > 1:
        for n, gw in zip(_BIG, rs.upper_layers(tail_deps)):
            per = gw.shape[0] // L
            if n in transposed:
                upper_grad[n] = _from_working(n, gw[per:])
                g2d, g_row0 = upper_grad[n].reshape(-1, upper_grad[n].shape[-1]), 0
            else:
                g2d, g_row0 = gw.reshape(-1, gw.shape[-1]), None
            r_layer = state[n][0].shape[0] // L
            updated[n] = _adamw(state[n][0], g2d, state[n][1], state[n][2], "adamw_upper",
                                rows=(r_layer, L * r_layer), g_row0=g_row0)
    full_small = [sg["g_norm"], sg["g_mem"], sg["sinks"], sg["ws"], sg["bs"], sg["gsgu"], sg["wp"], sg["ps"],
                  loss_part[:1, :1]]
    packed = _pack(full_small)
    pair_sum = _own_slab(packed[None], F32, j_arr, "small_grads_pair_sum",
                         deps=tuple(tail_deps) + tuple(u[0] for u in updated.values()),
                         plus=_pair_swap(packed, "small_grads_swap")[None])
    (chip_sums,) = _gather_weights([pair_sum], "gather_small_grads")
    total = _sum_slots(chip_sums.reshape(4, *packed.shape), "sum_small_grads")
    grads = {}
    summed = _unpack(total, full_small)
    loss = summed[-1][0, 0]
    for n, g in zip(_SMALL, summed):
        grads[n] = lax.dynamic_slice_in_dim(g, j * g_norm.shape[2], g_norm.shape[2], axis=2) if n == "g_norm" else g

    after = [total] + [u[0] for u in updated.values()]
    delta, new_m, new_v = {}, {}, {}
    for n, gw in zip(_BIG, rs.first_layer(after)):
        per = gw.shape[0] // L
        if n in transposed:
            g0 = _from_working(n, gw[:per])
            grads[n] = jnp.concatenate([g0, upper_grad[n]], axis=0) if L > 1 else g0
            g2d = g0.reshape(-1, g0.shape[-1])
        else:
            grads[n] = _from_working(n, gw)
            g2d = gw.reshape(-1, gw.shape[-1])
        r_layer = state[n][0].shape[0] // L
        d_, m_, v_ = _adamw(state[n][0], g2d, state[n][1], state[n][2], "adamw_first", rows=(0, r_layer), g_row0=0,
                            into=updated.get(n))
        shp = view[n](w[n]).shape
        delta[n], new_m[n], new_v[n] = view[n](d_.reshape(shp)), view[n](m_.reshape(shp)), view[n](v_.reshape(shp))
    small_w = [w[n] for n in _SMALL]
    d_, m_, v_ = _adamw(_pack(small_w), _pack([grads[n] for n in _SMALL]), _pack([m[n] for n in _SMALL]),
                        _pack([v[n] for n in _SMALL]), "adamw_small")
    for n, dd, mm_, vv in zip(_SMALL, _unpack(d_, small_w), _unpack(m_, small_w), _unpack(v_, small_w)):
        delta[n], new_m[n], new_v[n] = dd, mm_, vv

    return (loss, grad_x[None], *[grads[n] for n in _WEIGHTS], *[delta[n] for n in _WEIGHTS],
            *[new_m[n] for n in _WEIGHTS], *[new_v[n] for n in _WEIGHTS])
```

```python
import math

import jax
import jax.numpy as jnp
from jax import lax
from jax.experimental import pallas as pl
from jax.experimental.pallas import tpu as pltpu

F32 = jnp.float32
BF16 = jnp.bfloat16
MESH = pl.DeviceIdType.MESH

EPS = 1e-6
NEG_INF = -1e30
BLK = 128
HALO = 16
POOL_WINDOWS = (2, 4, 8, 16)
ATT_SCALE = 1.0 / math.sqrt(64.0)
MEM_SCALE = 1.0 / math.sqrt(128.0)
C_Q, C_K, C_V, C_SU, C_SV, C_PC, C_GATE, C_END = 0, 512, 640, 768, 1280, 1792, 2304, 5376

ADAM_LR, ADAM_B1, ADAM_B2, ADAM_EPS, ADAM_WD, ADAM_STEP = 0.001, 0.9, 0.999, 1e-08, 0.01, 10

VMEM_LIMIT_BYTES = 56 * 1024 * 1024

_DIMS = {
    "nn": (((1,), (0,)), ((), ())),
    "nt": (((1,), (1,)), ((), ())),
    "tn": (((0,), (0,)), ((), ())),
}


def _dot(a, b, mode):
    return lax.dot_general(a, b, _DIMS[mode], preferred_element_type=F32)


def _params(semantics):
    return pltpu.CompilerParams(dimension_semantics=semantics, vmem_limit_bytes=VMEM_LIMIT_BYTES)


def _tile(dim, pref):
    if dim <= pref:
        return dim
    t = (pref // 128) * 128
    while t >= 128:
        if dim % t == 0:
            return t
        t -= 128
    raise ValueError(f"no tile for {dim}")


def _rms(x, g):
    return x * lax.rsqrt(jnp.mean(x * x, axis=-1, keepdims=True) + EPS) * g


def _mm(a, b, mode, name, *, out_dtypes=(F32,), a_pre=(), b_pre=(), into=None, out_pre=(),
        extras=(), epi=None, deps=(), batch=None, tm=2048, tn=1024, tk=1024):
    lead = 0 if batch is None else 1
    assert not (lead and (a_pre or b_pre or into is not None or extras))
    a2, b2 = a.shape[len(a_pre) + lead:], b.shape[len(b_pre) + lead:]
    if mode == "nn":
        (M, K), (K2, N) = a2, b2
    elif mode == "nt":
        (M, K), (N, K2) = a2, b2
    else:
        (K, M), (K2, N) = a2, b2
    assert K == K2, (a.shape, b.shape, mode)
    tm, tn, tk = _tile(M, tm), _tile(N, tn), _tile(K, tk)
    nk = K // tk
    na, nb_, no = len(a_pre), len(b_pre), len(out_pre)

    def on_grid(f):
        return f if batch is None else (lambda g, i, j, k: (g,) + f(i, j, k))

    if mode == "tn":
        a_spec = pl.BlockSpec((None,) * (na + lead) + (tk, tm), on_grid(lambda i, j, k: a_pre + (k, i)))
    else:
        a_spec = pl.BlockSpec((None,) * (na + lead) + (tm, tk), on_grid(lambda i, j, k: a_pre + (i, k)))
    if mode == "nt":
        b_spec = pl.BlockSpec((None,) * (nb_ + lead) + (tn, tk), on_grid(lambda i, j, k: b_pre + (j, k)))
    else:
        b_spec = pl.BlockSpec((None,) * (nb_ + lead) + (tk, tn), on_grid(lambda i, j, k: b_pre + (k, j)))
    tile_spec = pl.BlockSpec((None,) * lead + (tm, tn), on_grid(lambda i, j, k: (i, j)))
    ne, nout = len(extras), len(out_dtypes)
    in_specs = [a_spec, b_spec] + [tile_spec] * ne
    operands = [a, b, *extras]
    aliases = {}
    if into is not None:
        assert nout == 1
        in_specs.append(pl.BlockSpec(memory_space=pl.ANY))
        operands.append(into)
        aliases = {len(operands) - 1: 0}
        out_shape = [jax.ShapeDtypeStruct(into.shape, into.dtype)]
        out_specs = [pl.BlockSpec((None,) * no + (tm, tn), lambda i, j, k: out_pre + (i, j))]
    else:
        out_shape = [jax.ShapeDtypeStruct(((batch,) if lead else ()) + (M, N), dt) for dt in out_dtypes]
        out_specs = [tile_spec] * nout
    in_specs += [pl.BlockSpec(memory_space=pl.ANY)] * len(deps)
    operands += list(deps)

    def body(*refs):
        a_ref, b_ref = refs[0], refs[1]
        ex = refs[2:2 + ne]
        pos = 2 + ne + (1 if into is not None else 0) + len(deps)
        outs = refs[pos:pos + nout]
        acc_ref = refs[pos + nout] if nk > 1 else None

        def finish(acc):
            vals = epi(acc, *[e[...] for e in ex]) if epi is not None else (acc,)
            for o, v in zip(outs, vals):
                o[...] = v.astype(o.dtype)

        def prod():
            return _dot(a_ref[...].astype(BF16), b_ref[...].astype(BF16), mode)

        if nk == 1:
            finish(prod())
        else:
            k = pl.program_id(2 + lead)

            @pl.when(k == 0)
            def _():
                acc_ref[...] = jnp.zeros_like(acc_ref)

            acc_ref[...] += prod()

            @pl.when(k == nk - 1)
            def _():
                finish(acc_ref[...])

    res = pl.pallas_call(
        body, name=name, grid=((batch,) if lead else ()) + (M // tm, N // tn, nk),
        in_specs=in_specs, out_specs=out_specs, out_shape=out_shape,
        scratch_shapes=[pltpu.VMEM((tm, tn), F32)] if nk > 1 else [],
        input_output_aliases=aliases,
        compiler_params=_params(("parallel",) * (2 + lead) + ("arbitrary",)),
    )(*operands)
    return res[0] if nout == 1 else tuple(res)


def _resnorm_fn(has_post, has_pre):
    def f(*a):
        x, k = a[0], 1
        if has_post:
            x, k = x + _rms(a[1], a[2]), 3
        outs = [x]
        if has_pre:
            outs.append(_rms(x, a[k]))
        return tuple(outs)
    return f


def _row_spec(T, W):
    return pl.BlockSpec((T, W), lambda i: (i, 0))


def _par_spec(W):
    return pl.BlockSpec((1, W), lambda i: (0, 0))


def _resnorm_fwd(xr, y, gp, gq, name, T=512, deps=()):
    S, D = xr.shape
    T = min(T, S)
    has_post, has_pre = y is not None, gq is not None
    f = _resnorm_fn(has_post, has_pre)
    ins = [xr] + ([y, gp] if has_post else []) + ([gq] if has_pre else [])
    in_specs = [_row_spec(T, D)] + ([_row_spec(T, D), _par_spec(D)] if has_post else []) + ([_par_spec(D)] if has_pre else [])
    out_shape, out_specs = [], []
    if has_post:
        out_shape.append(jax.ShapeDtypeStruct((S, D), F32)); out_specs.append(_row_spec(T, D))
    if has_pre:
        out_shape.append(jax.ShapeDtypeStruct((S, D), BF16)); out_specs.append(_row_spec(T, D))
    n_in, n_dep = len(ins), len(deps)

    def body(*refs):
        vals = f(*[r[...].astype(F32) for r in refs[:n_in]])
        outs = list(refs[n_in + n_dep:])
        if has_post:
            outs.pop(0)[...] = vals[0]
        if has_pre:
            outs.pop(0)[...] = vals[1].astype(BF16)

    res = pl.pallas_call(body, name=name, grid=(S // T,),
                         in_specs=in_specs + [pl.BlockSpec(memory_space=pl.ANY)] * n_dep, out_specs=out_specs,
                         out_shape=out_shape, compiler_params=_params(("parallel",)))(*ins, *deps)
    return tuple(res)


def _resnorm_bwd(xr, y, gp, gq, dres, dh, name, T=512, deps=()):
    S, D = xr.shape
    T = min(T, S)
    has_post, has_pre, has_res = y is not None, gq is not None, dres is not None
    f = _resnorm_fn(has_post, has_pre)
    ins = [xr] + ([y, gp] if has_post else []) + ([gq] if has_pre else [])
    in_specs = [_row_spec(T, D)] + ([_row_spec(T, D), _par_spec(D)] if has_post else []) + ([_par_spec(D)] if has_pre else [])
    n_prim = len(ins)
    if has_res:
        ins.append(dres); in_specs.append(_row_spec(T, D))
    if has_pre:
        ins.append(dh); in_specs.append(_row_spec(T, D))
    n_in, n_dep = len(ins), len(deps)
    out_shape = [jax.ShapeDtypeStruct((S, D), F32)]
    out_specs = [_row_spec(T, D)]
    if has_post:
        out_shape += [jax.ShapeDtypeStruct((S, D), BF16), jax.ShapeDtypeStruct((1, D), F32)]
        out_specs += [_row_spec(T, D), _par_spec(D)]
    if has_pre:
        out_shape.append(jax.ShapeDtypeStruct((1, D), F32)); out_specs.append(_par_spec(D))

    def body(*refs):
        i = pl.program_id(0)
        prim = [r[...].astype(F32) for r in refs[:n_prim]]
        rest = list(refs[n_prim:n_in])
        ct_x = rest.pop(0)[...] if has_res else jnp.zeros((T, D), F32)
        cts = [ct_x]
        if has_pre:
            cts.append(rest.pop(0)[...].astype(F32))
        _, vjp = jax.vjp(f, *prim)
        grads = list(vjp(tuple(cts)))
        outs = list(refs[n_in + n_dep:])
        outs.pop(0)[...] = grads.pop(0)
        acc = []
        if has_post:
            outs.pop(0)[...] = grads.pop(0).astype(BF16)
            acc.append((outs.pop(0), grads.pop(0)))
        if has_pre:
            acc.append((outs.pop(0), grads.pop(0)))

        @pl.when(i == 0)
        def _():
            for o, _g in acc:
                o[...] = jnp.zeros_like(o)

        for o, g in acc:
            o[...] += g

    res = pl.pallas_call(body, name=name, grid=(S // T,),
                         in_specs=in_specs + [pl.BlockSpec(memory_space=pl.ANY)] * n_dep, out_specs=out_specs,
                         out_shape=out_shape, compiler_params=_params(("arbitrary",)))(*ins, *deps)
    return tuple(res)


def _final_fwd(xr, y, gp, target, name, T=512):
    S, D = xr.shape
    T = min(T, S)

    def body(x_ref, y_ref, g_ref, t_ref, dy_ref, loss_ref):
        i = pl.program_id(0)
        e = x_ref[...] + _rms(y_ref[...].astype(F32), g_ref[...]) - t_ref[...]
        dy_ref[...] = e / D

        @pl.when(i == 0)
        def _():
            loss_ref[...] = jnp.zeros_like(loss_ref)

        loss_ref[...] += 0.5 * jnp.sum(jnp.sum(e * e, axis=-1, keepdims=True) / D, axis=0, keepdims=True)

    return pl.pallas_call(
        body, name=name, grid=(S // T,),
        in_specs=[_row_spec(T, D), _row_spec(T, D), _par_spec(D), _row_spec(T, D)],
        out_specs=[_row_spec(T, D), pl.BlockSpec((1, 128), lambda i: (0, 0))],
        out_shape=[jax.ShapeDtypeStruct((S, D), F32), jax.ShapeDtypeStruct((1, 128), F32)],
        compiler_params=_params(("arbitrary",)))(xr, y, gp, target)


_STRAIGHT_HEADS = (0, 2, 5, 7)
_ROLLED_HEADS = (1, 3, 4, 6)


def _straight_lanes():
    r = lax.broadcasted_iota(jnp.int32, (4 * BLK, BLK), 0)
    c = lax.broadcasted_iota(jnp.int32, (4 * BLK, BLK), 1)
    return (r < 2 * BLK) == (c < 64)


def _att_mask(not_first):
    k = lax.broadcasted_iota(jnp.int32, (2 * BLK, 4 * BLK), 0)
    q = lax.broadcasted_iota(jnp.int32, (2 * BLK, 4 * BLK), 1) % BLK
    qc, kc = 2 + q // 64, k // 64
    return (kc <= qc) & (kc >= qc - 2) & (not_first | (k >= BLK))


def _sink_row(sk_ref, heads):
    return jnp.concatenate([sk_ref[h:h + 1, :] for h in heads], axis=1)


def _softmax_sink(s, sk):
    m = jnp.maximum(jnp.max(s, axis=0, keepdims=True), sk)
    e = jnp.exp(s - m)
    es = jnp.exp(sk - m)
    inv = 1.0 / (jnp.sum(e, axis=0, keepdims=True) + es)
    return e * inv, es * inv


def _softmax_rows(s):
    e = jnp.exp(s - jnp.max(s, axis=0, keepdims=True))
    return e * (1.0 / jnp.sum(e, axis=0, keepdims=True))


_GELU_C = math.sqrt(2.0 / math.pi)
_GELU_A = 0.044715


def _gelu_with_slope(x):
    x2 = x * x
    t = jnp.tanh(_GELU_C * (x + _GELU_A * (x2 * x)))
    half = 0.5 * (1.0 + t)
    return x * half, half + (0.5 * _GELU_C) * x * (1.0 - t * t) * (1.0 + (3.0 * _GELU_A) * x2)


def _att_bands(cur, kvp):
    kband = jnp.concatenate([kvp[:, 0:BLK], cur[:, C_K:C_K + BLK]], axis=0)
    vband = jnp.concatenate([kvp[:, BLK:2 * BLK], cur[:, C_V:C_V + BLK]], axis=0)
    return kband, pltpu.roll(kband, 64, 1), vband, pltpu.roll(vband, 64, 1)


def _stack_tiles(ref_or_val, start):
    return jnp.concatenate([ref_or_val[:, start + BLK * t:start + BLK * (t + 1)] for t in range(4)], axis=0)


def _sgu_mask():
    r = lax.broadcasted_iota(jnp.int32, (BLK, BLK), 0)
    c = lax.broadcasted_iota(jnp.int32, (BLK, BLK), 1)
    return (c // 64) <= (r // 64)


def _trailing_sums(ext, g):
    s, shift = ext, 1
    for _ in range(g + 1):
        s = s + pltpu.roll(s, shift, 0)
        shift *= 2
    return s


def _leading_sums(z, g):
    d, shift = z, 1
    for _ in range(g + 1):
        d = d + pltpu.roll(d, z.shape[0] - shift, 0)
        shift *= 2
    return d


def _pool_cnt(blk, w):
    t = blk * BLK + lax.broadcasted_iota(jnp.int32, (BLK, 1), 0)
    return jnp.minimum(t + 1, w).astype(F32)


def _mix_in_specs(nb, rev):
    def b(i):
        return nb - 1 - i if rev else i
    return [
        pl.BlockSpec((BLK, C_GATE), lambda i: (b(i), 0)),
        pl.BlockSpec((BLK, 2 * BLK), lambda i: (jnp.maximum(b(i) - 1, 0), C_K // (2 * BLK))),
        pl.BlockSpec((HALO, C_GATE), lambda i: (jnp.maximum(b(i) * (BLK // HALO) - 1, 0), 0)),
        pl.BlockSpec((8, BLK), lambda i: (0, 0)),
        pl.BlockSpec((4, BLK, BLK), lambda i: (0, 0, 0)),
        pl.BlockSpec((4, BLK, 1), lambda i: (0, 0, 0)),
        pl.BlockSpec((1, 512), lambda i: (0, 0)),
        pl.BlockSpec((4, BLK, BLK), lambda i: (0, 0, 0)),
        pl.BlockSpec((1, 512), lambda i: (0, 0)),
    ]


def _mix_fwd(proj, sinks_b, ws, bs3, gsgu, wp, ps, name):
    S = proj.shape[0]
    nb = S // BLK

    def body(cur_ref, kvp_ref, pcp_ref, sk_ref, ws_ref, bs_ref, gs_ref, wp_ref, ps_ref, br_ref, ext_ref):
        i = pl.program_id(0)
        not_first = i > 0
        cur, kvp = cur_ref[...].astype(F32), kvp_ref[...].astype(F32)
        mask = _att_mask(not_first)
        own = _straight_lanes()
        q = _stack_tiles(cur, C_Q)
        outs = []
        kband, kroll, vband, vroll = _att_bands(cur, kvp)
        for qg, kg, vg, heads in ((jnp.where(own, q, 0.0), kband, vband, _STRAIGHT_HEADS),
                                  (jnp.where(own, 0.0, q), kroll, vroll, _ROLLED_HEADS)):
            s = jnp.where(mask, _dot(kg.astype(BF16), qg.astype(BF16), "nt") * ATT_SCALE, NEG_INF)
            p, _ = _softmax_sink(s, _sink_row(sk_ref, heads))
            outs.append(_dot(p.astype(BF16), vg.astype(BF16), "tn"))
        o = jnp.where(own, outs[0], outs[1])
        for t in range(4):
            br_ref[0, :, BLK * t:BLK * (t + 1)] = o[BLK * t:BLK * (t + 1)].astype(BF16)
        gu = jax.nn.gelu(cur[:, C_SU:C_SV])
        vn = _rms(jax.nn.gelu(cur[:, C_SV:C_PC]), gs_ref[...]).astype(BF16)
        wmask = _sgu_mask()
        for g in range(4):
            wm = jnp.where(wmask, ws_ref[g], 0.0).astype(BF16)
            sp = _dot(wm, vn[:, BLK * g:BLK * (g + 1)], "nn") + bs_ref[g]
            br_ref[1, :, BLK * g:BLK * (g + 1)] = (gu[:, BLK * g:BLK * (g + 1)] * sp).astype(BF16)
        c = cur[:, C_PC:C_GATE]
        ext_ref[0:HALO, :] = jnp.where(not_first, pcp_ref[:, C_PC:C_GATE].astype(F32), 0.0)
        ext_ref[HALO:HALO + BLK, :] = c
        for g, w in enumerate(POOL_WINDOWS):
            sl = slice(BLK * g, BLK * (g + 1))
            acc = _trailing_sums(ext_ref[:, sl], g)[HALO:]
            pooled = acc * (1.0 / _pool_cnt(i, w)) - c[:, sl]
            mixed = _dot(pooled.astype(BF16), wp_ref[g].astype(BF16), "nn")
            br_ref[2, :, sl] = (mixed * ps_ref[:, sl]).astype(BF16)

    return pl.pallas_call(
        body, name=name, grid=(nb,),
        in_specs=_mix_in_specs(nb, False),
        out_specs=pl.BlockSpec((3, BLK, 512), lambda i: (0, i, 0)),
        out_shape=jax.ShapeDtypeStruct((3, S, 512), BF16),
        scratch_shapes=[pltpu.VMEM((HALO + BLK, 512), F32)],
        compiler_params=_params(("parallel",)),
    )(proj, proj, proj, sinks_b, ws, bs3, gsgu, wp, ps)


def _mix_bwd(proj, dbr, dproj, sinks_b, ws, bs3, gsgu, wp, ps, name):
    S = proj.shape[0]
    nb = S // BLK

    def body(cur_ref, kvp_ref, pcp_ref, sk_ref, ws_ref, bs_ref, gs_ref, wp_ref, ps_ref, dbr_ref, _dproj_in,
             dp_ref, dsk_ref, dws_ref, dbs_ref, dgs_ref, dwp_ref, dps_ref,
             ext_ref, z_ref, ckv_ref, cpc_ref):
        i = pl.program_id(0)
        blk = nb - 1 - i
        not_first = blk > 0

        @pl.when(i == 0)
        def _():
            for r in (dsk_ref, dws_ref, dbs_ref, dgs_ref, dwp_ref, dps_ref, ckv_ref, cpc_ref, z_ref):
                r[...] = jnp.zeros_like(r)

        cur, kvp = cur_ref[...].astype(F32), kvp_ref[...].astype(F32)
        mask = _att_mask(not_first)
        own = _straight_lanes()
        q = _stack_tiles(cur, C_Q)
        do = jnp.concatenate([dbr_ref[0, :, BLK * t:BLK * (t + 1)] for t in range(4)], axis=0)
        kband, kroll, vband, vroll = _att_bands(cur, kvp)
        dqs, dks, dvs = [], [], []
        for qg, dog, kg, vg, heads in (
                (jnp.where(own, q, 0.0), jnp.where(own, do, 0.0), kband, vband, _STRAIGHT_HEADS),
                (jnp.where(own, 0.0, q), jnp.where(own, 0.0, do), kroll, vroll, _ROLLED_HEADS)):
            qg, dog, kg, vg = qg.astype(BF16), dog.astype(BF16), kg.astype(BF16), vg.astype(BF16)
            s = jnp.where(mask, _dot(kg, qg, "nt") * ATT_SCALE, NEG_INF)
            p, p_sink = _softmax_sink(s, _sink_row(sk_ref, heads))
            dp = _dot(vg, dog, "nt")
            rs = jnp.sum(p * dp, axis=0, keepdims=True)
            ds = (p * (dp - rs) * ATT_SCALE).astype(BF16)
            sink_row = p_sink * rs
            for t, h in enumerate(heads):
                dsk_ref[h:h + 1, :] += jnp.broadcast_to(
                    -jnp.sum(sink_row[:, BLK * t:BLK * (t + 1)], axis=1, keepdims=True), (1, BLK))
            dvs.append(_dot(p.astype(BF16), dog, "nn"))
            dks.append(_dot(ds, qg, "nn"))
            dqs.append(_dot(ds, kg, "tn"))
        dq = jnp.where(own, dqs[0], dqs[1])
        for t in range(4):
            dp_ref[:, C_Q + BLK * t:C_Q + BLK * (t + 1)] = dq[BLK * t:BLK * (t + 1)].astype(BF16)
        dk = dks[0] + pltpu.roll(dks[1], 64, 1)
        dv = dvs[0] + pltpu.roll(dvs[1], 64, 1)
        dp_ref[:, C_K:C_K + BLK] = (dk[BLK:] + ckv_ref[:, 0:BLK]).astype(BF16)
        dp_ref[:, C_V:C_V + BLK] = (dv[BLK:] + ckv_ref[:, BLK:]).astype(BF16)
        ckv_ref[:, 0:BLK] = dk[:BLK]
        ckv_ref[:, BLK:] = dv[:BLK]
        su, sv = cur[:, C_SU:C_SV], cur[:, C_SV:C_PC]
        gu, gu_slope = _gelu_with_slope(su)
        gv, gv_slope = _gelu_with_slope(sv)
        vn, vjp_v = jax.vjp(_rms, gv, gs_ref[...])
        vn16 = vn.astype(BF16)
        wmask = _sgu_mask()
        dgu, dvn = [], []
        for g in range(4):
            sl = slice(BLK * g, BLK * (g + 1))
            wm = jnp.where(wmask, ws_ref[g], 0.0).astype(BF16)
            sp = _dot(wm, vn16[:, sl], "nn") + bs_ref[g]
            dyb = dbr_ref[1, :, sl]
            dgu.append(dyb * sp)
            dsp = dyb * gu[:, sl]
            dsp16 = dsp.astype(BF16)
            dvn.append(_dot(wm, dsp16, "tn"))
            dws_ref[g] += jnp.where(wmask, _dot(dsp16, vn16[:, sl], "nt"), 0.0)
            dbs_ref[g] += jnp.sum(dsp, axis=1, keepdims=True)
        dgv, dgs = vjp_v(jnp.concatenate(dvn, axis=1))
        dp_ref[:, C_SU:C_SV] = (jnp.concatenate(dgu, axis=1) * gu_slope).astype(BF16)
        dp_ref[:, C_SV:C_PC] = (dgv * gv_slope).astype(BF16)
        dgs_ref[...] += dgs
        c = cur[:, C_PC:C_GATE]
        ext_ref[0:HALO, :] = jnp.where(not_first, pcp_ref[:, C_PC:C_GATE].astype(F32), 0.0)
        ext_ref[HALO:HALO + BLK, :] = c
        for g, w in enumerate(POOL_WINDOWS):
            sl = slice(BLK * g, BLK * (g + 1))
            acc = _trailing_sums(ext_ref[:, sl], g)[HALO:]
            inv_cnt = 1.0 / _pool_cnt(blk, w)
            pooled16 = (acc * inv_cnt - c[:, sl]).astype(BF16)
            wp16 = wp_ref[g].astype(BF16)
            mixed = _dot(pooled16, wp16, "nn")
            dyc = dbr_ref[2, :, sl]
            dps_ref[:, sl] += jnp.sum(dyc * mixed, axis=0, keepdims=True)
            dmixed16 = (dyc * ps_ref[:, sl]).astype(BF16)
            dwp_ref[g] += _dot(pooled16, dmixed16, "tn")
            dpooled = _dot(dmixed16, wp16, "nt")
            z_ref[HALO:HALO + BLK, sl] = dpooled * inv_cnt
            dext = _leading_sums(z_ref[:, sl], g)[:HALO + BLK]
            dp_ref[:, C_PC + BLK * g:C_PC + BLK * (g + 1)] = (
                dext[HALO:] - dpooled + jnp.concatenate([jnp.zeros((BLK - HALO, BLK), F32), cpc_ref[:, sl]], axis=0)
            ).astype(BF16)
            cpc_ref[:, sl] = dext[:HALO]

    n_in = 11
    small = [jax.ShapeDtypeStruct((8, BLK), F32), jax.ShapeDtypeStruct((4, BLK, BLK), F32),
             jax.ShapeDtypeStruct((4, BLK, 1), F32), jax.ShapeDtypeStruct((1, 512), F32),
             jax.ShapeDtypeStruct((4, BLK, BLK), F32), jax.ShapeDtypeStruct((1, 512), F32)]
    small_specs = [pl.BlockSpec((8, BLK), lambda i: (0, 0)), pl.BlockSpec((4, BLK, BLK), lambda i: (0, 0, 0)),
                   pl.BlockSpec((4, BLK, 1), lambda i: (0, 0, 0)), pl.BlockSpec((1, 512), lambda i: (0, 0)),
                   pl.BlockSpec((4, BLK, BLK), lambda i: (0, 0, 0)), pl.BlockSpec((1, 512), lambda i: (0, 0))]
    res = pl.pallas_call(
        body, name=name, grid=(nb,),
        in_specs=_mix_in_specs(nb, True) + [
            pl.BlockSpec((3, BLK, 512), lambda i: (0, nb - 1 - i, 0)),
            pl.BlockSpec(memory_space=pl.ANY)],
        out_specs=[pl.BlockSpec((BLK, C_GATE), lambda i: (nb - 1 - i, 0))] + small_specs,
        out_shape=[jax.ShapeDtypeStruct(dproj.shape, dproj.dtype)] + small,
        scratch_shapes=[pltpu.VMEM((HALO + BLK, 512), F32), pltpu.VMEM((2 * HALO + BLK, 512), F32),
                        pltpu.VMEM((BLK, 2 * BLK), F32), pltpu.VMEM((HALO, 512), F32)],
        input_output_aliases={n_in - 1: 0},
        compiler_params=_params(("arbitrary",)),
    )(proj, proj, proj, sinks_b, ws, bs3, gsgu, wp, ps, dbr, dproj)
    return tuple(res)


_GW = 256


def _merge_fwd(proj, pb, name, T=4096):
    S, D = pb.shape[1], pb.shape[2]
    T = min(T, S)

    def body(gate_ref, pb_ref, out_ref, acc_ref):
        n = pl.program_id(2)

        @pl.when(n == 0)
        def _():
            acc_ref[...] = jnp.zeros_like(acc_ref)

        acc_ref[...] += jax.nn.sigmoid(gate_ref[...].astype(F32)) * pb_ref[...]

        @pl.when(n == 2)
        def _():
            out_ref[...] = acc_ref[...].astype(BF16)

    return pl.pallas_call(
        body, name=name, grid=(S // T, D // _GW, 3),
        in_specs=[pl.BlockSpec((T, _GW), lambda i, j, n: (i, C_GATE // _GW + n * (D // _GW) + j)),
                  pl.BlockSpec((None, T, _GW), lambda i, j, n: (n, i, j))],
        out_specs=pl.BlockSpec((T, _GW), lambda i, j, n: (i, j)),
        out_shape=jax.ShapeDtypeStruct((S, D), BF16),
        scratch_shapes=[pltpu.VMEM((T, _GW), F32)],
        compiler_params=_params(("parallel", "parallel", "arbitrary")),
    )(proj, pb)


def _merge_bwd(proj, pb, dmerged, name, T=4096):
    S, D = pb.shape[1], pb.shape[2]
    T = min(T, S)

    def body(gate_ref, pb_ref, dm_ref, dgate_ref, dpb_ref):
        sg = jax.nn.sigmoid(gate_ref[...].astype(F32))
        dm = dm_ref[...]
        dpb_ref[...] = (dm * sg).astype(BF16)
        dgate_ref[...] = (dm * pb_ref[...] * sg * (1.0 - sg)).astype(BF16)

    gate_map = lambda i, n, j: (i, C_GATE // _GW + n * (D // _GW) + j)
    return pl.pallas_call(
        body, name=name, grid=(S // T, 3, D // _GW),
        in_specs=[pl.BlockSpec((T, _GW), gate_map),
                  pl.BlockSpec((None, T, _GW), lambda i, n, j: (n, i, j)),
                  pl.BlockSpec((T, _GW), lambda i, n, j: (i, j))],
        out_specs=[pl.BlockSpec((T, _GW), gate_map),
                   pl.BlockSpec((None, T, _GW), lambda i, n, j: (n, i, j))],
        out_shape=[jax.ShapeDtypeStruct((S, C_END), BF16), jax.ShapeDtypeStruct((3, S, D), BF16)],
        compiler_params=_params(("parallel", "parallel", "parallel")),
    )(proj, pb, dmerged)


def _memattn_fwd(qm, kv, name, T=512):
    S, NM = qm.shape[0], kv.shape[0]
    T = min(T, S)

    def body(q_ref, kv_ref, o_ref):
        for h in range(4):
            sl = slice(128 * h, 128 * (h + 1))
            k = kv_ref[:, sl].astype(BF16)
            v = kv_ref[:, 512 + 128 * h:512 + 128 * (h + 1)].astype(BF16)
            s = _dot(k, q_ref[:, sl].astype(BF16), "nt") * MEM_SCALE
            p = _softmax_rows(s)
            o_ref[:, sl] = _dot(p.astype(BF16), v, "tn").astype(BF16)

    return pl.pallas_call(
        body, name=name, grid=(S // T,),
        in_specs=[_row_spec(T, 512), pl.BlockSpec((NM, 1024), lambda i: (0, 0))],
        out_specs=_row_spec(T, 512), out_shape=jax.ShapeDtypeStruct((S, 512), BF16),
        compiler_params=_params(("parallel",)))(qm, kv)


def _memattn_bwd(qm, kv, dom, name, T=512):
    S, NM = qm.shape[0], kv.shape[0]
    T = min(T, S)

    def body(q_ref, kv_ref, do_ref, dq_ref, dkv_ref):
        i = pl.program_id(0)

        @pl.when(i == 0)
        def _():
            dkv_ref[...] = jnp.zeros_like(dkv_ref)

        for h in range(4):
            sl = slice(128 * h, 128 * (h + 1))
            sv_ = slice(512 + 128 * h, 512 + 128 * (h + 1))
            q = q_ref[:, sl].astype(BF16)
            k = kv_ref[:, sl].astype(BF16)
            v = kv_ref[:, sv_].astype(BF16)
            do = do_ref[:, sl].astype(BF16)
            p = _softmax_rows(_dot(k, q, "nt") * MEM_SCALE)
            dp = _dot(v, do, "nt")
            ds = (p * (dp - jnp.sum(p * dp, axis=0, keepdims=True)) * MEM_SCALE).astype(BF16)
            dq_ref[:, sl] = _dot(ds, k, "tn").astype(BF16)
            dkv_ref[:, sl] += _dot(ds, q, "nn")
            dkv_ref[:, sv_] += _dot(p.astype(BF16), do, "nn")

    return pl.pallas_call(
        body, name=name, grid=(S // T,),
        in_specs=[_row_spec(T, 512), pl.BlockSpec((NM, 1024), lambda i: (0, 0)), _row_spec(T, 512)],
        out_specs=[_row_spec(T, 512), pl.BlockSpec((NM, 1024), lambda i: (0, 0))],
        out_shape=[jax.ShapeDtypeStruct((S, 512), BF16), jax.ShapeDtypeStruct((NM, 1024), F32)],
        compiler_params=_params(("arbitrary",)))(qm, kv, dom)


def _adamw(w, g, m, v, name, rows=None, g_row0=None, into=None, TR=512):
    R, C = w.shape
    lo, hi = rows if rows is not None else (0, R)
    g0 = lo if g_row0 is None else g_row0
    TR = _row_tile(math.gcd(math.gcd(lo, g0), hi - lo), TR)
    c1 = 1.0 - ADAM_B1 ** ADAM_STEP
    c2 = 1.0 - ADAM_B2 ** ADAM_STEP

    def body(w_ref, g_ref, m_ref, v_ref, *rest):
        d_ref, nm_ref, nv_ref = rest[-3:]
        gv = g_ref[...]
        nm = ADAM_B1 * m_ref[...] + (1.0 - ADAM_B1) * gv
        nv = ADAM_B2 * v_ref[...] + (1.0 - ADAM_B2) * jnp.square(gv)
        d_ref[...] = -ADAM_LR * ((nm / c1) / (jnp.sqrt(nv / c2) + ADAM_EPS) + ADAM_WD * w_ref[...])
        nm_ref[...] = nm
        nv_ref[...] = nv

    spec = pl.BlockSpec((TR, C), lambda i: (lo // TR + i, 0))
    g_spec = pl.BlockSpec((TR, C), lambda i: (g0 // TR + i, 0))
    prior = list(into) if into is not None else []
    return pl.pallas_call(
        body, name=name, grid=((hi - lo) // TR,),
        in_specs=[spec, g_spec, spec, spec] + [pl.BlockSpec(memory_space=pl.ANY)] * len(prior), out_specs=[spec] * 3,
        out_shape=[jax.ShapeDtypeStruct((R, C), F32)] * 3,
        input_output_aliases={4 + k: k for k in range(len(prior))},
        compiler_params=_params(("parallel",)))(w, g, m, v, *prior)


def _row_tile(R, pref):
    t = (pref // 8) * 8
    while t >= 8:
        if R % t == 0:
            return t
        t -= 8
    raise ValueError(f"no row tile for {R}")


def _sum_slots(stack, name, TR=512):
    n, R, C = stack.shape
    TR = R if R <= TR else _row_tile(R, TR)

    def body(s_ref, o_ref):
        acc = s_ref[0]
        for k in range(1, n):
            acc = acc + s_ref[k]
        o_ref[...] = acc

    return pl.pallas_call(
        body, name=name, grid=(R // TR,),
        in_specs=[pl.BlockSpec((n, TR, C), lambda i: (0, i, 0))],
        out_specs=pl.BlockSpec((TR, C), lambda i: (i, 0)),
        out_shape=jax.ShapeDtypeStruct((R, C), F32),
        compiler_params=_params(("parallel",)))(stack)


_ANY = pl.BlockSpec(memory_space=pl.ANY)


def _chip_of(j, c):
    return (j // 2, j % 2, c)


def _own_slab(shard, dtype, j_arr, name, first=0, count=None, plus=None, deps=(), TR=512):
    N, r, C = shard.shape
    B = N if count is None else count
    rh = r // 2
    TR = rh if rh <= TR else _row_tile(rh, TR)
    nt = rh // TR
    ins = [shard] if plus is None else [shard, plus]

    def body(j_ref, *refs):
        val = refs[0][...] if plus is None else refs[0][...] + refs[1][...]
        refs[-1][...] = val.astype(refs[-1].dtype)

    return pl.pallas_call(
        body, name=name,
        grid_spec=pltpu.PrefetchScalarGridSpec(
            num_scalar_prefetch=1, grid=(B, 2, nt),
            in_specs=[pl.BlockSpec((None, TR, C), lambda b, h, t, jr: (first + b, h * nt + t, 0))] * len(ins)
            + [_ANY] * len(deps),
            out_specs=pl.BlockSpec((None, None, None, TR, C), lambda b, h, t, jr: (b, jr[0], h, t, 0))),
        out_shape=jax.ShapeDtypeStruct((B, 4, 2, rh, C), dtype),
        compiler_params=_params(("parallel", "parallel", "parallel")),
    )(j_arr, *ins, *deps)


def _gather_weights(bufs, name):
    n = len(bufs)

    def body(*refs):
        buf = refs[n:2 * n]
        send_sems, recv_sems, fsend_sems, frecv_sems = refs[2 * n:]
        x, y, c = lax.axis_index("x"), lax.axis_index("y"), lax.axis_index("c")
        j = 2 * x + y
        sib = (x, y, 1 - c)
        sends = []
        for d in range(1, 4):
            for a in range(n):
                cp = pltpu.make_async_remote_copy(
                    src_ref=buf[a].at[:, j, c], dst_ref=buf[a].at[:, j, c], send_sem=send_sems.at[a, d - 1],
                    recv_sem=recv_sems.at[a, d - 1], device_id=_chip_of((j + d) % 4, c), device_id_type=MESH)
                cp.start()
                sends.append(cp)
        for d in range(1, 4):
            frm = (j + 4 - d) % 4
            for a in range(n):
                pltpu.make_async_remote_copy(
                    src_ref=buf[a].at[:, frm, c], dst_ref=buf[a].at[:, frm, c], send_sem=send_sems.at[a, d - 1],
                    recv_sem=recv_sems.at[a, d - 1], device_id=_chip_of(frm, c), device_id_type=MESH).wait_recv()
                cp = pltpu.make_async_remote_copy(
                    src_ref=buf[a].at[:, frm, c], dst_ref=buf[a].at[:, frm, c], send_sem=fsend_sems.at[a, d - 1],
                    recv_sem=frecv_sems.at[a, d - 1], device_id=sib, device_id_type=MESH)
                cp.start()
                sends.append(cp)
        for d in range(1, 4):
            frm = (j + 4 - d) % 4
            for a in range(n):
                pltpu.make_async_remote_copy(
                    src_ref=buf[a].at[:, frm, 1 - c], dst_ref=buf[a].at[:, frm, 1 - c], send_sem=fsend_sems.at[a, d - 1],
                    recv_sem=frecv_sems.at[a, d - 1], device_id=sib, device_id_type=MESH).wait_recv()
        for cp in sends:
            cp.wait_send()

    return pl.pallas_call(
        body, name=name,
        in_specs=[_ANY] * n, out_specs=[_ANY] * n,
        out_shape=[jax.ShapeDtypeStruct(b.shape, b.dtype) for b in bufs],
        scratch_shapes=[pltpu.SemaphoreType.DMA((n, 3))] * 4,
        input_output_aliases={a: a for a in range(n)},
    )(*bufs)


_HBM = pl.BlockSpec(memory_space=pltpu.HBM)
_SEM = pl.BlockSpec(memory_space=pltpu.SEMAPHORE)
_DATAFLOW = pltpu.SideEffectType.DATAFLOW_SIDE_EFFECTING


def _in_hbm(arrays):
    return [pltpu.with_memory_space_constraint(a, pltpu.HBM) for a in arrays]


def _start_copies(bufs, plan, count, name, deps=()):
    n, k = len(bufs), len(deps)

    def body(*refs):
        send_sems, recv_sems = refs[n + k], refs[n + k + 1]
        for i, (src, dst, dev) in enumerate(plan(refs[:n], False)):
            pltpu.make_async_remote_copy(src_ref=src, dst_ref=dst, send_sem=send_sems.at[i], recv_sem=recv_sems.at[i],
                                         device_id=dev, device_id_type=MESH).start()
        refs[-1][...] = jnp.zeros_like(refs[-1])

    return pl.pallas_call(
        body, name=name,
        out_shape=(pltpu.SemaphoreType.DMA((count,)), pltpu.SemaphoreType.DMA((count,)),
                   *[pltpu.HBM(b.shape, b.dtype) for b in bufs], jax.ShapeDtypeStruct((8, 128), F32)),
        in_specs=[_HBM] * n + [_ANY] * k,
        out_specs=(_SEM, _SEM, *[_HBM] * n, pl.BlockSpec(memory_space=pltpu.VMEM)),
        input_output_aliases={a: 2 + a for a in range(n)},
        compiler_params=pltpu.CompilerParams(has_side_effects=_DATAFLOW),
    )(*_in_hbm(bufs), *deps)


def _wait_copies(handle, plan, afters, name):
    send_sems, recv_sems, *bufs = handle[:-1]
    n = len(bufs)

    def body(*refs):
        send_sems, recv_sems = refs[n], refs[n + 1]
        for i, (src, dst, dev) in enumerate(plan(refs[:n], True)):
            cp = pltpu.make_async_remote_copy(src_ref=src, dst_ref=dst, send_sem=send_sems.at[i], recv_sem=recv_sems.at[i],
                                              device_id=dev, device_id_type=MESH)
            cp.wait_send()
            cp.wait_recv()

    return list(pl.pallas_call(
        body, name=name,
        out_shape=[pltpu.HBM(b.shape, b.dtype) for b in bufs],
        in_specs=[_HBM] * n + [_SEM, _SEM] + [_ANY] * len(afters), out_specs=[_HBM] * n,
        input_output_aliases={a: a for a in range(n)},
        compiler_params=pltpu.CompilerParams(has_side_effects=_DATAFLOW),
    )(*bufs, send_sems, recv_sems, *afters))


def _gather_plan(buf, waiting):
    c = lax.axis_index("c")
    j = 2 * lax.axis_index("x") + lax.axis_index("y")
    copies = []
    for d in range(1, 4):
        to, frm = (j + d) % 4, (j + 4 - d) % 4
        for b in buf:
            copies.append((b.at[:, j, c], b.at[:, frm if waiting else j, c], _chip_of(frm if waiting else to, c)))
    return copies


def _chip_plan(buf, waiting):
    n = len(buf) // 2
    c = lax.axis_index("c")
    j = 2 * lax.axis_index("x") + lax.axis_index("y")
    copies = []
    for d in range(1, 4):
        to = (j + d) % 4
        for a in range(n):
            copies.append((buf[a].at[to], buf[n + a].at[d - 1], _chip_of(to, c)))
    return copies


def _pair_plan(buf, waiting):
    n = len(buf) // 2
    c = lax.axis_index("c")
    sib = (lax.axis_index("x"), lax.axis_index("y"), 1 - c)
    return [(buf[a].at[:, pl.ds(1 - c, 1)], buf[n + a], sib) for a in range(n)]


def _forward_plan(buf, waiting):
    x, y, c = lax.axis_index("x"), lax.axis_index("y"), lax.axis_index("c")
    j = 2 * x + y
    copies = []
    for d in range(1, 4):
        frm = (j + 4 - d) % 4
        for b in buf:
            copies.append((b.at[:, frm, c], b.at[:, frm, 1 - c if waiting else c], (x, y, 1 - c)))
    return copies


def _gather_forward(bufs, name):
    n = len(bufs)

    def body(*refs):
        buf = refs[n:2 * n]
        send_sems, recv_sems = refs[2 * n:]
        x, y, c = lax.axis_index("x"), lax.axis_index("y"), lax.axis_index("c")
        j = 2 * x + y
        sib = (x, y, 1 - c)
        sends = []
        for d in range(1, 4):
            frm = (j + 4 - d) % 4
            for a in range(n):
                cp = pltpu.make_async_remote_copy(
                    src_ref=buf[a].at[:, frm, c], dst_ref=buf[a].at[:, frm, c], send_sem=send_sems.at[a, d - 1],
                    recv_sem=recv_sems.at[a, d - 1], device_id=sib, device_id_type=MESH)
                cp.start()
                sends.append(cp)
        for d in range(1, 4):
            frm = (j + 4 - d) % 4
            for a in range(n):
                pltpu.make_async_remote_copy(
                    src_ref=buf[a].at[:, frm, 1 - c], dst_ref=buf[a].at[:, frm, 1 - c], send_sem=send_sems.at[a, d - 1],
                    recv_sem=recv_sems.at[a, d - 1], device_id=sib, device_id_type=MESH).wait_recv()
        for cp in sends:
            cp.wait_send()

    return pl.pallas_call(
        body, name=name,
        in_specs=[_ANY] * n, out_specs=[_ANY] * n,
        out_shape=[jax.ShapeDtypeStruct(b.shape, b.dtype) for b in bufs],
        scratch_shapes=[pltpu.SemaphoreType.DMA((n, 3))] * 2,
        input_output_aliases={a: a for a in range(n)},
    )(*bufs)


def _pair_add(g4, r1, cj_arr, name, TR=512):
    B4, _, rh, C = g4.shape
    B = B4 // 4
    TR = rh if rh <= TR else _row_tile(rh, TR)

    def body(cj_ref, g_ref, r_ref, o16_ref, own_ref):
        s = g_ref[...].astype(F32) + r_ref[...].astype(F32)
        o16_ref[...] = s.astype(BF16)

        @pl.when(pl.program_id(2) == cj_ref[1])
        def _():
            own_ref[...] = s

    return pl.pallas_call(
        body, name=name,
        grid_spec=pltpu.PrefetchScalarGridSpec(
            num_scalar_prefetch=1, grid=(B, rh // TR, 4),
            in_specs=[pl.BlockSpec((None, None, TR, C), lambda b, t, p, cj: (b * 4 + p, cj[0], t, 0)),
                      pl.BlockSpec((None, None, TR, C), lambda b, t, p, cj: (b * 4 + p, 0, t, 0))],
            out_specs=[pl.BlockSpec((None, None, TR, C), lambda b, t, p, cj: (p, b, t, 0)),
                       pl.BlockSpec((None, TR, C), lambda b, t, p, cj: (b, t, 0))]),
        out_shape=[jax.ShapeDtypeStruct((4, B, rh, C), BF16), jax.ShapeDtypeStruct((B, rh, C), F32)],
        compiler_params=_params(("parallel", "parallel", "arbitrary")),
    )(cj_arr, g4, r1)


def _chip_add(own, r2, cj_arr, into, first, name, TR=512):
    B, rh, C = own.shape
    TR = rh if rh <= TR else _row_tile(rh, TR)

    def body(cj_ref, p_ref, r_ref, _into_ref, o_ref):
        o_ref[...] = p_ref[...] + r_ref[0].astype(F32) + r_ref[1].astype(F32) + r_ref[2].astype(F32)

    return pl.pallas_call(
        body, name=name,
        grid_spec=pltpu.PrefetchScalarGridSpec(
            num_scalar_prefetch=1, grid=(B, rh // TR),
            in_specs=[pl.BlockSpec((None, TR, C), lambda b, t, cj: (b, t, 0)),
                      pl.BlockSpec((3, None, TR, C), lambda b, t, cj: (0, b, t, 0)),
                      _ANY],
            out_specs=pl.BlockSpec((None, None, TR, C), lambda b, t, cj: (first + b, cj[0], t, 0))),
        out_shape=jax.ShapeDtypeStruct(into.shape, F32),
        input_output_aliases={3: 0},
        compiler_params=_params(("parallel", "parallel")),
    )(cj_arr, own, r2, into)


def _pair_share(bufs, spans, name, deps=()):
    n = len(bufs)

    def body(*refs):
        buf = refs[n + len(deps):2 * n + len(deps)]
        send_sems, recv_sems = refs[2 * n + len(deps):]
        c = lax.axis_index("c")
        sib = (lax.axis_index("x"), lax.axis_index("y"), 1 - c)
        cps = []
        for a, (first, count) in enumerate(spans):
            cp = pltpu.make_async_remote_copy(
                src_ref=buf[a].at[pl.ds(first, count), c], dst_ref=buf[a].at[pl.ds(first, count), c],
                send_sem=send_sems.at[a], recv_sem=recv_sems.at[a], device_id=sib, device_id_type=MESH)
            cp.start()
            cps.append(cp)
        for a, (first, count) in enumerate(spans):
            pltpu.make_async_remote_copy(
                src_ref=buf[a].at[pl.ds(first, count), 1 - c], dst_ref=buf[a].at[pl.ds(first, count), 1 - c],
                send_sem=send_sems.at[a], recv_sem=recv_sems.at[a], device_id=sib, device_id_type=MESH).wait_recv()
        for cp in cps:
            cp.wait_send()

    return pl.pallas_call(
        body, name=name, in_specs=[_ANY] * (n + len(deps)), out_specs=[_ANY] * n,
        out_shape=[jax.ShapeDtypeStruct(b.shape, b.dtype) for b in bufs],
        scratch_shapes=[pltpu.SemaphoreType.DMA((n,)), pltpu.SemaphoreType.DMA((n,))],
        input_output_aliases={a: a for a in range(n)},
    )(*bufs, *deps)


def _pair_swap(arr, name):
    def body(src, dst, send_sem, recv_sem):
        sib = (lax.axis_index("x"), lax.axis_index("y"), 1 - lax.axis_index("c"))
        cp = pltpu.make_async_remote_copy(src_ref=src, dst_ref=dst, send_sem=send_sem, recv_sem=recv_sem,
                                          device_id=sib, device_id_type=MESH)
        cp.start()
        cp.wait_recv()
        cp.wait_send()

    return pl.pallas_call(
        body, name=name, in_specs=[_ANY], out_specs=_ANY,
        out_shape=jax.ShapeDtypeStruct(arr.shape, arr.dtype),
        scratch_shapes=[pltpu.SemaphoreType.DMA, pltpu.SemaphoreType.DMA],
    )(arr)


class _ReduceScatter:
    def __init__(self, n_layers, cj_arr):
        self.L, self.cj = n_layers, cj_arr
        self.total = None
        self.pair = None
        self.chip = None

    def _land(self, after):
        handle, layer, owns = self.chip
        n = len(owns)
        r2 = _wait_copies(handle, _chip_plan, after if isinstance(after, (list, tuple)) else (after,), "rs_chip_wait")[n:]
        if self.total is None:
            self.total = [lax.empty((self.L * o.shape[0], 2) + o.shape[1:], F32) for o in owns]
        self.total = [_chip_add(o, r, self.cj, t, layer * o.shape[0], "rs_chip_add")
                      for o, r, t in zip(owns, r2, self.total)]
        self.chip = None

    def add_layer(self, layer, grads):
        g4 = [g.reshape(g.shape[0] * 4, 2, g.shape[1] // 8, g.shape[2]) for g in grads]
        lands = [lax.empty((g.shape[0], 1) + g.shape[2:], g.dtype) for g in g4]
        handle = _start_copies(g4 + lands, _pair_plan, len(g4), "rs_pair_start")
        self.pair = (handle, layer)
        return (handle[-1],)

    def advance(self, after):
        if self.pair is None:
            return ()
        handle, layer = self.pair
        both = _wait_copies(handle, _pair_plan, (after,), "rs_pair_wait")
        n = len(both) // 2
        added = [_pair_add(g, r, self.cj, "rs_pair_add") for g, r in zip(both[:n], both[n:])]
        parts, owns = [p for p, _ in added], [o for _, o in added]
        lands = [lax.empty((3,) + p.shape[1:], p.dtype) for p in parts]
        handle = _start_copies(parts + lands, _chip_plan, 3 * n, "rs_chip_start")
        if self.chip is not None:
            self._land(handle[-1])
        self.pair, self.chip = None, (handle, layer, owns)
        return (handle[-1],)

    def upper_layers(self, deps):
        per = [t.shape[0] // self.L for t in self.total]
        self.total = _pair_share(self.total, [(b, (self.L - 1) * b) for b in per], "rs_pair_share_upper", deps=deps)
        return [t.reshape(t.shape[0], t.shape[1] * t.shape[2], t.shape[3]) for t in self.total]

    def first_layer(self, after):
        self._land(after)
        per = [t.shape[0] // self.L for t in self.total]
        full = _pair_share(self.total, [(0, b) for b in per], "rs_pair_share_first")
        return [f.reshape(f.shape[0], f.shape[1] * f.shape[2], f.shape[3]) for f in full]


def _relu2_epi(acc):
    return (jnp.square(jnp.maximum(acc, 0.0)),)


def _relu2_bwd_epi(acc, a):
    return (acc * (2.0 * jnp.sqrt(a.astype(F32))),)


_GRAD_ORDER = ("winT", "wbT", "wout", "wq", "wkv", "woT", "wupT", "wdown")
_DW = dict(tm=512, tn=1024, tk=4096, out_dtypes=(BF16,))
_LONG_K = dict(tm=1024, tn=1024, tk=2048)


def _forward_backward(x, mem, target, input_weight_of, weights_of, prefetch_weights, P, grads_done, grads_advance):
    L = P["g_norm"].shape[0]
    S, D = x.shape
    gn = lambda l, i: P["g_norm"][l, i][None]

    saved = []
    (h,) = _resnorm_fwd(x, None, None, gn(0, 0), "norm_in")
    xr = x
    for l in range(L):
        w_in_t = input_weight_of(l, xr)
        proj = _mm(h, w_in_t, "nt", "in_proj", b_pre=(0,), out_dtypes=(BF16,), tn=1792)
        W, w_deps = weights_of(l, proj)
        small = (jnp.broadcast_to(P["sinks"][l][:, None], (8, BLK)), P["ws"][l], P["bs"][l][:, :, None],
                 P["gsgu"][l][None], P["wp"][l], P["ps"][l][None])
        br = _mix_fwd(proj, *small, "mix_fwd")
        pb = _mm(br, W["wbT"], "nt", "branch_proj", batch=3, out_dtypes=(BF16,), deps=w_deps)
        merged = _merge_fwd(proj, pb, "merge_fwd")
        z = _mm(merged, W["wout"], "nn", "out_proj", b_pre=(0,), out_dtypes=(BF16,))
        x1, hm = _resnorm_fwd(xr, z, gn(l, 1), gn(l, 2), "resnorm_fwd")
        qm = _mm(hm, W["wq"], "nn", "mem_q", b_pre=(0,))
        (memn,) = _resnorm_fwd(mem, None, None, P["g_mem"][l][None], "mem_norm")
        kv = _mm(memn, W["wkv"], "nn", "mem_kv", b_pre=(0,))
        om = _memattn_fwd(qm, kv, "memattn_fwd")
        ym = _mm(om, W["woT"], "nt", "mem_o", b_pre=(0,), out_dtypes=(BF16,))
        x2, hf = _resnorm_fwd(x1, ym, gn(l, 3), gn(l, 4), "resnorm_fwd")
        a = _mm(hf, W["wupT"], "nt", "mlp_up", b_pre=(0,), out_dtypes=(BF16,), epi=_relu2_epi)
        yf = _mm(a, W["wdown"], "nn", "mlp_down", b_pre=(0,), out_dtypes=(BF16,), **_LONG_K,
                 deps=prefetch_weights(l + 1, a) if l < L - 1 else ())
        saved.append(dict(W=W, x0=xr, h=h, proj=proj, small=small, br=br, pb=pb, merged=merged, z=z, x1=x1, hm=hm,
                          qm=qm, memn=memn, kv=kv, om=om, ym=ym, x2=x2, hf=hf, a=a, yf=yf))
        if l < L - 1:
            xr, h = _resnorm_fwd(x2, yf, gn(l, 5), gn(l + 1, 0), "resnorm_fwd")
    dres, loss = _final_fwd(saved[-1]["x2"], saved[-1]["yf"], gn(L - 1, 5), target, "loss_head")

    dgn = [[None] * 6 for _ in range(L)]
    dsmall = {k: [None] * L for k in ("g_mem", "sinks", "ws", "bs", "gsgu", "wp", "ps")}
    dh = None
    for l in reversed(range(L)):
        s = saved[l]
        W, G = s["W"], {}
        if l == L - 1:
            dx2, dyf, dgn[l][5] = _resnorm_bwd(s["x2"], s["yf"], gn(l, 5), None, dres, None, "resnorm_bwd_top")
        else:
            dx2, dyf, dgn[l][5], dgn[l + 1][0] = _resnorm_bwd(s["x2"], s["yf"], gn(l, 5), gn(l + 1, 0), dres, dh,
                                                              "resnorm_bwd", deps=deps)
        du = _mm(dyf, W["wdown"], "nt", "mlp_down_dx", b_pre=(0,), out_dtypes=(BF16,), extras=(s["a"],), epi=_relu2_bwd_epi)
        G["wdown"] = _mm(s["a"], dyf, "tn", "mlp_down_dw", **_DW)[None]
        dhf = _mm(du, W["wupT"], "nn", "mlp_up_dx", b_pre=(0,), out_dtypes=(BF16,), **_LONG_K)
        G["wupT"] = _mm(du, s["hf"], "tn", "mlp_up_dw", **_DW)[None]
        dx1, dym, dgn[l][3], dgn[l][4] = _resnorm_bwd(s["x1"], s["ym"], gn(l, 3), gn(l, 4), dx2, dhf, "resnorm_bwd")
        dom = _mm(dym, W["woT"], "nn", "mem_o_dx", b_pre=(0,), out_dtypes=(BF16,), deps=grads_advance(dx1))
        G["woT"] = _mm(dym, s["om"], "tn", "mem_o_dw", **_DW)[None]
        dqm, dkv = _memattn_bwd(s["qm"], s["kv"], dom, "memattn_bwd")
        dmemn = _mm(dkv, W["wkv"], "nt", "mem_kv_dx", b_pre=(0,))
        G["wkv"] = _mm(s["memn"], dkv, "tn", "mem_kv_dw", out_dtypes=(BF16,))[None]
        _, dsmall["g_mem"][l] = _resnorm_bwd(mem, None, None, P["g_mem"][l][None], None, dmemn, "mem_norm_bwd")
        dhm = _mm(dqm, W["wq"], "nt", "mem_q_dx", b_pre=(0,), out_dtypes=(BF16,))
        G["wq"] = _mm(s["hm"], dqm, "tn", "mem_q_dw", **_DW)[None]
        dx0, dz, dgn[l][1], dgn[l][2] = _resnorm_bwd(s["x0"], s["z"], gn(l, 1), gn(l, 2), dx1, dhm, "resnorm_bwd")
        dmerged = _mm(dz, W["wout"], "nt", "out_proj_dx", b_pre=(0,), out_dtypes=(BF16,))
        G["wout"] = _mm(s["merged"], dz, "tn", "out_proj_dw", **_DW)[None]
        dproj, dpb = _merge_bwd(s["proj"], s["pb"], dmerged, "merge_bwd")
        dbr = _mm(dpb, W["wbT"], "nn", "branch_proj_dx", batch=3, out_dtypes=(BF16,))
        G["wbT"] = _mm(dpb, s["br"], "tn", "branch_proj_dw", batch=3, **_DW)
        (dproj, dsmall["sinks"][l], dsmall["ws"][l], dsmall["bs"][l], dsmall["gsgu"][l], dsmall["wp"][l],
         dsmall["ps"][l]) = _mix_bwd(s["proj"], dbr, dproj, *s["small"], "mix_bwd")
        dh = _mm(dproj, W["winT"], "nn", "in_proj_dx", b_pre=(0,), out_dtypes=(BF16,), **_LONG_K)
        G["winT"] = _mm(dproj, s["h"], "tn", "in_proj_dw", **_DW)[None]
        deps = grads_done(l, [G[k] for k in _GRAD_ORDER])
        dres = dx0
    grad_x, dgn[0][0] = _resnorm_bwd(x, None, None, gn(0, 0), dres, dh, "norm_in_bwd", deps=deps)
    tail_deps = grads_advance(grad_x)

    small_grads = dict(
        g_norm=jnp.stack([jnp.concatenate(row, axis=0) for row in dgn]),
        g_mem=jnp.concatenate(dsmall["g_mem"], axis=0),
        sinks=jnp.stack([d[:, 0] for d in dsmall["sinks"]]),
        ws=jnp.stack(dsmall["ws"]),
        bs=jnp.stack([d[:, :, 0] for d in dsmall["bs"]]),
        gsgu=jnp.concatenate(dsmall["gsgu"], axis=0),
        wp=jnp.stack(dsmall["wp"]),
        ps=jnp.concatenate(dsmall["ps"], axis=0),
    )
    return loss, grad_x, small_grads, tail_deps


_PACK_ROWS = 512


def _as_rows(a):
    n = math.prod(a.shape)
    if n % 128:
        a = jnp.pad(a.reshape(-1), (0, (-n) % 128))
    r = a.reshape(-1, 128)
    return jnp.pad(r, ((0, (-r.shape[0]) % 8), (0, 0))) if r.shape[0] % 8 else r


def _pack(arrays):
    rows = [_as_rows(a) for a in arrays]
    total = sum(r.shape[0] for r in rows)
    tail = (-total) % _PACK_ROWS
    if tail:
        rows.append(jnp.zeros((tail, 128), rows[0].dtype))
    return jnp.concatenate(rows, axis=0)


def _unpack(packed, like):
    out, pos = [], 0
    for a in like:
        n = math.prod(a.shape)
        nr = -(-n // 128)
        rows = packed[pos:pos + nr]
        out.append((rows.reshape(-1)[:n] if n % 128 else rows).reshape(a.shape))
        pos += nr + (-nr) % 8
    return out


_BIG = ("w_in", "w_branch", "w_out", "w_q_mem", "w_kv_mem", "w_o_mem", "w_up", "w_down")
_SMALL = ("g_norm", "g_mem", "attn_sinks", "w_spatial", "b_spatial", "g_sgu", "w_pool", "pool_scale")
_WEIGHTS = ("g_norm", "g_mem", "w_in", "attn_sinks", "w_spatial", "b_spatial", "g_sgu", "w_pool", "pool_scale",
            "w_branch", "w_out", "w_q_mem", "w_kv_mem", "w_o_mem", "w_up", "w_down")


def _to_working(name, w):
    if name == "w_in":
        return jnp.swapaxes(w, 1, 2)
    if name == "w_branch":
        t = jnp.swapaxes(w, 2, 3)
        return t.reshape(t.shape[0] * 3, t.shape[2], t.shape[3])
    if name in ("w_o_mem", "w_up"):
        return jnp.swapaxes(w, 1, 2)
    return w


def _from_working(name, g):
    if name == "w_in":
        return jnp.swapaxes(g, 1, 2)
    if name == "w_branch":
        return jnp.swapaxes(g.reshape(g.shape[0] // 3, 3, g.shape[1], g.shape[2]), 2, 3)
    if name in ("w_o_mem", "w_up"):
        return jnp.swapaxes(g, 1, 2)
    return g


def kernel(x, mem, g_norm, g_mem, w_in, attn_sinks, w_spatial, b_spatial, g_sgu, w_pool, pool_scale, w_branch, w_out, w_q_mem, w_kv_mem, w_o_mem, w_up, w_down, loss_target, m_g_norm, m_g_mem, m_w_in, m_attn_sinks, m_w_spatial, m_b_spatial, m_g_sgu, m_w_pool, m_pool_scale, m_w_branch, m_w_out, m_w_q_mem, m_w_kv_mem, m_w_o_mem, m_w_up, m_w_down, v_g_norm, v_g_mem, v_w_in, v_attn_sinks, v_w_spatial, v_b_spatial, v_g_sgu, v_w_pool, v_pool_scale, v_w_branch, v_w_out, v_w_q_mem, v_w_kv_mem, v_w_o_mem, v_w_up, v_w_down):
    w = dict(g_norm=g_norm, g_mem=g_mem, w_in=w_in, attn_sinks=attn_sinks, w_spatial=w_spatial, b_spatial=b_spatial,
             g_sgu=g_sgu, w_pool=w_pool, pool_scale=pool_scale, w_branch=w_branch, w_out=w_out, w_q_mem=w_q_mem,
             w_kv_mem=w_kv_mem, w_o_mem=w_o_mem, w_up=w_up, w_down=w_down)
    m = dict(g_norm=m_g_norm, g_mem=m_g_mem, w_in=m_w_in, attn_sinks=m_attn_sinks, w_spatial=m_w_spatial,
             b_spatial=m_b_spatial, g_sgu=m_g_sgu, w_pool=m_w_pool, pool_scale=m_pool_scale, w_branch=m_w_branch,
             w_out=m_w_out, w_q_mem=m_w_q_mem, w_kv_mem=m_w_kv_mem, w_o_mem=m_w_o_mem, w_up=m_w_up, w_down=m_w_down)
    v = dict(g_norm=v_g_norm, g_mem=v_g_mem, w_in=v_w_in, attn_sinks=v_attn_sinks, w_spatial=v_w_spatial,
             b_spatial=v_b_spatial, g_sgu=v_g_sgu, w_pool=v_w_pool, pool_scale=v_pool_scale, w_branch=v_w_branch,
             w_out=v_w_out, w_q_mem=v_w_q_mem, w_kv_mem=v_w_kv_mem, w_o_mem=v_w_o_mem, w_up=v_w_up, w_down=v_w_down)
    L = g_norm.shape[0]
    j = 2 * lax.axis_index("x") + lax.axis_index("y")
    c = lax.axis_index("c")
    j_arr = jnp.reshape(j, (1,)).astype(jnp.int32)
    cj_arr = jnp.stack([c, j]).astype(jnp.int32)

    gs = g_norm.shape[2]
    working = [_to_working(n, w[n].astype(BF16) if n in ("w_branch", "w_o_mem", "w_up") else w[n]) for n in _BIG]
    per_layer = [wk.shape[0] // L for wk in working]

    def own_slabs(l):
        return [_own_slab(wk, BF16, j_arr, "own_slab", first=l * b, count=b) for wk, b in zip(working, per_layer)]

    first_slabs = own_slabs(0)
    lead = [first_slabs[0], _own_slab(g_norm.reshape(1, L * 6 * gs // 128, 128), F32, j_arr, "own_slab_norm")]
    lead_handle = _start_copies(lead, _gather_plan, 3 * len(lead), "gather_start_lead")
    rest_handle = _start_copies(first_slabs[1:], _gather_plan, 3 * (len(first_slabs) - 1), "gather_start_first",
                                deps=(lead_handle[-1],))
    slabs = {l: own_slabs(l) for l in range(1, L)}
    lead = _gather_forward(_wait_copies(lead_handle, _gather_plan,
                                        [rest_handle[-1]] + [s for l in slabs for s in slabs[l]],
                                        "gather_wait_lead"), "gather_forward_lead")
    gn_full = jnp.transpose(lead[1].reshape(4, L * 6, gs), (1, 0, 2)).reshape(L, 6, 4 * gs)
    P = dict(g_norm=gn_full, g_mem=g_mem, sinks=attn_sinks, ws=w_spatial, bs=b_spatial, gsgu=g_sgu, wp=w_pool,
             ps=pool_scale)
    whole = lambda g: g.reshape(g.shape[0], 8 * g.shape[3], g.shape[4])
    gathered, in_flight = {}, {}

    forwarding = {}

    def prefetch_weights(l, after):
        arrived = _wait_copies(in_flight[l], _gather_plan, (after,), "gather_wait")
        forwarding[l] = _start_copies(arrived, _forward_plan, 3 * len(arrived), "gather_forward_start")
        return (forwarding[l][-1],)

    def layer_weights(l, after):
        if l not in gathered:
            gathered[l] = _wait_copies(forwarding[l], _forward_plan, (after,), "gather_forward_wait")
        return gathered[l]

    def input_weight_of(l, after):
        return whole(lead[0] if l == 0 else layer_weights(l, after)[0])

    def weights_of(l, after):
        deps = ()
        if l == 0:
            rest = _gather_forward(_wait_copies(rest_handle, _gather_plan, (after,), "gather_wait_first"),
                                   "gather_forward_first")
            gathered[0] = [lead[0], *rest]
            dep = rest[0]
            for k in range(1, L):
                in_flight[k] = _start_copies(slabs[k], _gather_plan, 3 * len(slabs[k]), "gather_start", deps=(dep,))
                dep = in_flight[k][-1]
            deps = tuple(h[-1] for h in in_flight.values())
        return {k: whole(g) for k, g in zip(_GRAD_ORDER, layer_weights(l, after))}, deps

    rs = _ReduceScatter(L, cj_arr)
    loss_part, grad_x, sg, tail_deps = _forward_backward(
        x[0], mem[0], loss_target[0], input_weight_of, weights_of, prefetch_weights, P, rs.add_layer, rs.advance)

    transposed = ("w_branch", "w_o_mem", "w_up")
    view = {n: (lambda t: jnp.swapaxes(t, 1, 2)) if n == "w_in" else (lambda t: t) for n in _BIG}
    rows = lambda n, t: view[n](t).reshape(-1, view[n](t).shape[-1])
    state = {n: (rows(n, w[n]), rows(n, m[n]), rows(n, v[n])) for n in _BIG}
    updated, upper_grad = {}, {}
    if L > 1:
        for n, gw in zip(_BIG, rs.upper_layers(tail_deps)):
            per = gw.shape[0] // L
            if n in transposed:
                upper_grad[n] = _from_working(n, gw[per:])
                g2d, g_row0 = upper_grad[n].reshape(-1, upper_grad[n].shape[-1]), 0
            else:
                g2d, g_row0 = gw.reshape(-1, gw.shape[-1]), None
            r_layer = state[n][0].shape[0] // L
            updated[n] = _adamw(state[n][0], g2d, state[n][1], state[n][2], "adamw_upper",
                                rows=(r_layer, L * r_layer), g_row0=g_row0)
    full_small = [sg["g_norm"], sg["g_mem"], sg["sinks"], sg["ws"], sg["bs"], sg["gsgu"], sg["wp"], sg["ps"],
                  loss_part[:1, :1]]
    packed = _pack(full_small)
    pair_sum = _own_slab(packed[None], F32, j_arr, "small_grads_pair_sum",
                         deps=tuple(tail_deps) + tuple(u[0] for u in updated.values()),
                         plus=_pair_swap(packed, "small_grads_swap")[None])
    (chip_sums,) = _gather_weights([pair_sum], "gather_small_grads")
    total = _sum_slots(chip_sums.reshape(4, *packed.shape), "sum_small_grads")
    grads = {}
    summed = _unpack(total, full_small)
    loss = summed[-1][0, 0]
    for n, g in zip(_SMALL, summed):
        grads[n] = lax.dynamic_slice_in_dim(g, j * g_norm.shape[2], g_norm.shape[2], axis=2) if n == "g_norm" else g

    after = [total] + [u[0] for u in updated.values()]
    delta, new_m, new_v = {}, {}, {}
    for n, gw in zip(_BIG, rs.first_layer(after)):
        per = gw.shape[0] // L
        if n in transposed:
            g0 = _from_working(n, gw[:per])
            grads[n] = jnp.concatenate([g0, upper_grad[n]], axis=0) if L > 1 else g0
            g2d = g0.reshape(-1, g0.shape[-1])
        else:
            grads[n] = _from_working(n, gw)
            g2d = gw.reshape(-1, gw.shape[-1])
        r_layer = state[n][0].shape[0] // L
        d_, m_, v_ = _adamw(state[n][0], g2d, state[n][1], state[n][2], "adamw_first", rows=(0, r_layer), g_row0=0,
                            into=updated.get(n))
        shp = view[n](w[n]).shape
        delta[n], new_m[n], new_v[n] = view[n](d_.reshape(shp)), view[n](m_.reshape(shp)), view[n](v_.reshape(shp))
    small_w = [w[n] for n in _SMALL]
    d_, m_, v_ = _adamw(_pack(small_w), _pack([grads[n] for n in _SMALL]), _pack([m[n] for n in _SMALL]),
                        _pack([v[n] for n in _SMALL]), "adamw_small")
    for n, dd, mm_, vv in zip(_SMALL, _unpack(d_, small_w), _unpack(m_, small_w), _unpack(v_, small_w)):
        delta[n], new_m[n], new_v[n] = dd, mm_, vv

    return (loss, grad_x[None], *[grads[n] for n in _WEIGHTS], *[delta[n] for n in _WEIGHTS],
            *[new_m[n] for n in _WEIGHTS], *[new_v[n] for n in _WEIGHTS])
```
